```python
import jax, jax.numpy as jnp
from jax import lax
import numpy as np

D_MODEL = 1024
BATCH = 8
SEQ = 4096
DEPTH = 2

N_A_LAYERS = DEPTH // 2
N_B_LAYERS = DEPTH - N_A_LAYERS
HGRN_EXPAND = 128
HGRN_HEADS = D_MODEL // HGRN_EXPAND
HGRN_DK = HGRN_EXPAND
HGRN_DV = D_MODEL // HGRN_HEADS
HGRN_CHUNK = 64
MLA_HEADS = 16
MLA_NOPE = 128
MLA_ROPE = 64
MLA_V = 128
MLA_Q_LORA = 256
MLA_KV_LORA = 256
ROPE_THETA = 10000.0
QBLOCK = 128
D_FF = 4 * D_MODEL
EPS = 1e-6

kernel_name = 'hybrid_hgrn2_mla_yoco'


def rmsnorm(x, gain):
    xf = x.astype(jnp.float32)
    y = xf * lax.rsqrt(jnp.mean(xf * xf, axis=-1, keepdims=True) + EPS)
    return (y * gain.astype(jnp.float32)).astype(x.dtype)


def rope_tables(seq):
    half = MLA_ROPE // 2
    inv_freq = ROPE_THETA ** (-jnp.arange(half, dtype=jnp.float32) / half)
    ang = jnp.arange(seq, dtype=jnp.float32)[:, None] * inv_freq[None, :]
    return jnp.cos(ang), jnp.sin(ang)


def apply_rope(x, cos, sin):
    half = MLA_ROPE // 2
    xf = x.astype(jnp.float32)
    x1, x2 = xf[..., :half], xf[..., half:]
    return jnp.concatenate([x1 * cos - x2 * sin, x2 * cos + x1 * sin], axis=-1).astype(x.dtype)


def hgrn_lower_bounds(lb_logits):
    return jnp.cumsum(jax.nn.softmax(lb_logits.astype(jnp.float32), axis=0), axis=0)


def hgrn2_mixer(xn, w_q, w_f, w_i, w_g, g_norm, w_o, lb):
    bsz, seq, _ = xn.shape
    nc = seq // HGRN_CHUNK
    f32 = jnp.float32
    q = jax.nn.silu((xn @ w_q).astype(f32))
    forget = lb + (1.0 - lb) * jax.nn.sigmoid((xn @ w_f).astype(f32))
    log_f = jnp.log(forget)
    k = 1.0 - forget
    v = (xn @ w_i).astype(f32)

    def chunks(t, d):
        return t.reshape(bsz, nc, HGRN_CHUNK, HGRN_HEADS, d).transpose(1, 0, 3, 2, 4)

    causal = jnp.tril(jnp.ones((HGRN_CHUNK, HGRN_CHUNK), dtype=bool))

    def step(state, inp):
        qc, kc, vc, gc = inp
        b = jnp.cumsum(gc, axis=2)
        o_inter = jnp.einsum('bhtd,bhdv->bhtv', qc * jnp.exp(b), state)
        diff = b[:, :, :, None, :] - b[:, :, None, :, :]
        decay = jnp.exp(jnp.where(causal[:, :, None], diff, -jnp.inf))
        scores = jnp.einsum('bhtd,bhsd,bhtsd->bhts', qc, kc, decay)
        o_intra = jnp.einsum('bhts,bhsv->bhtv', scores, vc)
        b_last = b[:, :, -1:, :]
        new_state = jnp.exp(b_last[:, :, 0, :])[..., None] * state + jnp.einsum(
            'bhsd,bhsv->bhdv', kc * jnp.exp(b_last - b), vc)
        return new_state, o_inter + o_intra

    state0 = jnp.zeros((bsz, HGRN_HEADS, HGRN_DK, HGRN_DV), f32)
    _, o = lax.scan(step, state0, (chunks(q, HGRN_DK), chunks(k, HGRN_DK),
                                   chunks(v, HGRN_DV), chunks(log_f, HGRN_DK)))
    o = o.transpose(1, 0, 3, 2, 4).reshape(bsz, seq, HGRN_HEADS, HGRN_DV)
    o = rmsnorm(o, g_norm)
    gate = jax.nn.silu((xn @ w_g).astype(f32)).reshape(bsz, seq, HGRN_HEADS, HGRN_DV)
    o = (o * gate).reshape(bsz, seq, D_MODEL).astype(xn.dtype)
    return o @ w_o


def shared_mla_kv(h, in_norm, w_dkv, kv_norm, w_uk, w_uv, cos, sin):
    bsz, seq, _ = h.shape
    hn = rmsnorm(h, in_norm)
    ckr = hn @ w_dkv
    c_kv = rmsnorm(ckr[..., :MLA_KV_LORA], kv_norm)
    k_rope = apply_rope(ckr[..., MLA_KV_LORA:], cos, sin)
    k_nope = (c_kv @ w_uk).reshape(bsz, seq, MLA_HEADS, MLA_NOPE)
    v = (c_kv @ w_uv).reshape(bsz, seq, MLA_HEADS, MLA_V)
    return k_nope, k_rope, v


def mla_mixer(xn, w_dq, q_norm, w_uq, w_o, k_nope, k_rope, v, cos, sin):
    bsz, seq, _ = xn.shape
    nb = seq // QBLOCK
    c_q = rmsnorm(xn @ w_dq, q_norm)
    q = (c_q @ w_uq).reshape(bsz, seq, MLA_HEADS, MLA_NOPE + MLA_ROPE)
    q_nope = q[..., :MLA_NOPE]
    q_rope = apply_rope(q[..., MLA_NOPE:], cos[:, None, :], sin[:, None, :])
    qn_b = q_nope.reshape(bsz, nb, QBLOCK, MLA_HEADS, MLA_NOPE).transpose(1, 0, 2, 3, 4)
    qr_b = q_rope.reshape(bsz, nb, QBLOCK, MLA_HEADS, MLA_ROPE).transpose(1, 0, 2, 3, 4)
    starts = jnp.arange(nb, dtype=jnp.int32) * QBLOCK
    key_pos = jnp.arange(seq, dtype=jnp.int32)
    scale = (MLA_NOPE + MLA_ROPE) ** -0.5

    def block(args):
        qn, qr, start = args
        s = jnp.einsum('bqhd,bkhd->bhqk', qn, k_nope) + jnp.einsum('bqhr,bkr->bhqk', qr, k_rope)
        s = s.astype(jnp.float32) * scale
        q_pos = start + jnp.arange(QBLOCK, dtype=jnp.int32)
        s = jnp.where(key_pos[None, :] <= q_pos[:, None], s, -jnp.inf)
        p = jax.nn.softmax(s, axis=-1).astype(v.dtype)
        return jnp.einsum('bhqk,bkhv->bqhv', p, v)

    o = lax.map(block, (qn_b, qr_b, starts))
    o = o.transpose(1, 0, 2, 3, 4).reshape(bsz, seq, MLA_HEADS * MLA_V)
    return o @ w_o


def sq_relu_mlp(xn, w_up, w_down):
    return jnp.square(jax.nn.relu(xn @ w_up)) @ w_down


def _fwd_setup_inputs(seed: int = 0) -> dict:
    key = jax.random.key(seed)
    ks = jax.random.split(key, 24)
    f32 = jnp.float32

    def w(k, shape, fan_in):
        return jax.random.normal(k, shape, f32) * (fan_in ** -0.5)

    def gain(k, shape):
        return 1.0 + 0.02 * jax.random.normal(k, shape, f32)

    D = D_MODEL
    return {
        'x': jax.random.normal(ks[0], (BATCH, SEQ, D), f32),
        'hgrn_norm': gain(ks[1], (N_A_LAYERS, D)),
        'hgrn_w_q': w(ks[2], (N_A_LAYERS, D, D), D),
        'hgrn_w_f': w(ks[3], (N_A_LAYERS, D, D), D),
        'hgrn_w_i': w(ks[4], (N_A_LAYERS, D, D), D),
        'hgrn_w_g': w(ks[5], (N_A_LAYERS, D, D), D),
        'hgrn_g_norm': gain(ks[6], (N_A_LAYERS, HGRN_DV)),
        'hgrn_w_o': w(ks[7], (N_A_LAYERS, D, D), D),
        'hgrn_lb_logits': 0.5 * jax.random.normal(ks[8], (N_A_LAYERS + 1, D), f32),
        'mla_norm': gain(ks[9], (N_B_LAYERS, D)),
        'mla_w_dq': w(ks[10], (N_B_LAYERS, D, MLA_Q_LORA), D),
        'mla_q_norm': gain(ks[11], (N_B_LAYERS, MLA_Q_LORA)),
        'mla_w_uq': w(ks[12], (N_B_LAYERS, MLA_Q_LORA, MLA_HEADS * (MLA_NOPE + MLA_ROPE)), MLA_Q_LORA),
        'mla_w_o': w(ks[13], (N_B_LAYERS, MLA_HEADS * MLA_V, D), MLA_HEADS * MLA_V),
        'kv_in_norm': gain(ks[14], (D,)),
        'kv_w_dkv': w(ks[15], (D, MLA_KV_LORA + MLA_ROPE), D),
        'kv_norm': gain(ks[16], (MLA_KV_LORA,)),
        'kv_w_uk': w(ks[17], (MLA_KV_LORA, MLA_HEADS * MLA_NOPE), MLA_KV_LORA),
        'kv_w_uv': w(ks[18], (MLA_KV_LORA, MLA_HEADS * MLA_V), MLA_KV_LORA),
        'mlp_norm': gain(ks[19], (DEPTH, D)),
        'mlp_w_up': w(ks[20], (DEPTH, D, D_FF), D),
        'mlp_w_down': w(ks[21], (DEPTH, D_FF, D), D_FF),
        'final_norm': gain(ks[22], (D,)),
    }


def _fwd_reference(x, hgrn_norm, hgrn_w_q, hgrn_w_f, hgrn_w_i, hgrn_w_g, hgrn_g_norm, hgrn_w_o,
              hgrn_lb_logits, mla_norm, mla_w_dq, mla_q_norm, mla_w_uq, mla_w_o,
              kv_in_norm, kv_w_dkv, kv_norm, kv_w_uk, kv_w_uv,
              mlp_norm, mlp_w_up, mlp_w_down, final_norm):
    seq = x.shape[1]
    cos, sin = rope_tables(seq)
    lower_bounds = hgrn_lower_bounds(hgrn_lb_logits)
    h = x
    k_nope = k_rope = v = None
    for l in range(DEPTH):
        if l < N_A_LAYERS:
            h = h + hgrn2_mixer(rmsnorm(h, hgrn_norm[l]), hgrn_w_q[l], hgrn_w_f[l], hgrn_w_i[l],
                                hgrn_w_g[l], hgrn_g_norm[l], hgrn_w_o[l], lower_bounds[l])
        else:
            j = l - N_A_LAYERS
            h = h + mla_mixer(rmsnorm(h, mla_norm[j]), mla_w_dq[j], mla_q_norm[j], mla_w_uq[j],
                              mla_w_o[j], k_nope, k_rope, v, cos, sin)
        h = h + sq_relu_mlp(rmsnorm(h, mlp_norm[l]), mlp_w_up[l], mlp_w_down[l])
        if l == N_A_LAYERS - 1:
            k_nope, k_rope, v = shared_mla_kv(h, kv_in_norm, kv_w_dkv, kv_norm, kv_w_uk, kv_w_uv, cos, sin)
    return rmsnorm(h, final_norm)


import jax as _jax
import jax.numpy as _jnp

TWIN_FORMAT = 'train_step'
FWD_PARAMS = ['x', 'hgrn_norm', 'hgrn_w_q', 'hgrn_w_f', 'hgrn_w_i', 'hgrn_w_g', 'hgrn_g_norm', 'hgrn_w_o', 'hgrn_lb_logits', 'mla_norm', 'mla_w_dq', 'mla_q_norm', 'mla_w_uq', 'mla_w_o', 'kv_in_norm', 'kv_w_dkv', 'kv_norm', 'kv_w_uk', 'kv_w_uv', 'mlp_norm', 'mlp_w_up', 'mlp_w_down', 'final_norm']
TWIN_WEIGHTS = ['hgrn_norm', 'hgrn_w_q', 'hgrn_w_f', 'hgrn_w_i', 'hgrn_w_g', 'hgrn_g_norm', 'hgrn_w_o', 'hgrn_lb_logits', 'mla_norm', 'mla_w_dq', 'mla_q_norm', 'mla_w_uq', 'mla_w_o', 'kv_in_norm', 'kv_w_dkv', 'kv_norm', 'kv_w_uk', 'kv_w_uv', 'mlp_norm', 'mlp_w_up', 'mlp_w_down', 'final_norm']
TWIN_DIFF_INPUT = 'x'
TWIN_INPUTS = ['x', 'hgrn_norm', 'hgrn_w_q', 'hgrn_w_f', 'hgrn_w_i', 'hgrn_w_g', 'hgrn_g_norm', 'hgrn_w_o', 'hgrn_lb_logits', 'mla_norm', 'mla_w_dq', 'mla_q_norm', 'mla_w_uq', 'mla_w_o', 'kv_in_norm', 'kv_w_dkv', 'kv_norm', 'kv_w_uk', 'kv_w_uv', 'mlp_norm', 'mlp_w_up', 'mlp_w_down', 'final_norm', 'loss_target', 'm_hgrn_norm', 'm_hgrn_w_q', 'm_hgrn_w_f', 'm_hgrn_w_i', 'm_hgrn_w_g', 'm_hgrn_g_norm', 'm_hgrn_w_o', 'm_hgrn_lb_logits', 'm_mla_norm', 'm_mla_w_dq', 'm_mla_q_norm', 'm_mla_w_uq', 'm_mla_w_o', 'm_kv_in_norm', 'm_kv_w_dkv', 'm_kv_norm', 'm_kv_w_uk', 'm_kv_w_uv', 'm_mlp_norm', 'm_mlp_w_up', 'm_mlp_w_down', 'm_final_norm', 'v_hgrn_norm', 'v_hgrn_w_q', 'v_hgrn_w_f', 'v_hgrn_w_i', 'v_hgrn_w_g', 'v_hgrn_g_norm', 'v_hgrn_w_o', 'v_hgrn_lb_logits', 'v_mla_norm', 'v_mla_w_dq', 'v_mla_q_norm', 'v_mla_w_uq', 'v_mla_w_o', 'v_kv_in_norm', 'v_kv_w_dkv', 'v_kv_norm', 'v_kv_w_uk', 'v_kv_w_uv', 'v_mlp_norm', 'v_mlp_w_up', 'v_mlp_w_down', 'v_final_norm']
TWIN_OUTPUTS = ['loss', 'grad_x', 'grad_hgrn_norm', 'grad_hgrn_w_q', 'grad_hgrn_w_f', 'grad_hgrn_w_i', 'grad_hgrn_w_g', 'grad_hgrn_g_norm', 'grad_hgrn_w_o', 'grad_hgrn_lb_logits', 'grad_mla_norm', 'grad_mla_w_dq', 'grad_mla_q_norm', 'grad_mla_w_uq', 'grad_mla_w_o', 'grad_kv_in_norm', 'grad_kv_w_dkv', 'grad_kv_norm', 'grad_kv_w_uk', 'grad_kv_w_uv', 'grad_mlp_norm', 'grad_mlp_w_up', 'grad_mlp_w_down', 'grad_final_norm', 'delta_hgrn_norm', 'delta_hgrn_w_q', 'delta_hgrn_w_f', 'delta_hgrn_w_i', 'delta_hgrn_w_g', 'delta_hgrn_g_norm', 'delta_hgrn_w_o', 'delta_hgrn_lb_logits', 'delta_mla_norm', 'delta_mla_w_dq', 'delta_mla_q_norm', 'delta_mla_w_uq', 'delta_mla_w_o', 'delta_kv_in_norm', 'delta_kv_w_dkv', 'delta_kv_norm', 'delta_kv_w_uk', 'delta_kv_w_uv', 'delta_mlp_norm', 'delta_mlp_w_up', 'delta_mlp_w_down', 'delta_final_norm', 'new_m_hgrn_norm', 'new_m_hgrn_w_q', 'new_m_hgrn_w_f', 'new_m_hgrn_w_i', 'new_m_hgrn_w_g', 'new_m_hgrn_g_norm', 'new_m_hgrn_w_o', 'new_m_hgrn_lb_logits', 'new_m_mla_norm', 'new_m_mla_w_dq', 'new_m_mla_q_norm', 'new_m_mla_w_uq', 'new_m_mla_w_o', 'new_m_kv_in_norm', 'new_m_kv_w_dkv', 'new_m_kv_norm', 'new_m_kv_w_uk', 'new_m_kv_w_uv', 'new_m_mlp_norm', 'new_m_mlp_w_up', 'new_m_mlp_w_down', 'new_m_final_norm', 'new_v_hgrn_norm', 'new_v_hgrn_w_q', 'new_v_hgrn_w_f', 'new_v_hgrn_w_i', 'new_v_hgrn_w_g', 'new_v_hgrn_g_norm', 'new_v_hgrn_w_o', 'new_v_hgrn_lb_logits', 'new_v_mla_norm', 'new_v_mla_w_dq', 'new_v_mla_q_norm', 'new_v_mla_w_uq', 'new_v_mla_w_o', 'new_v_kv_in_norm', 'new_v_kv_w_dkv', 'new_v_kv_norm', 'new_v_kv_w_uk', 'new_v_kv_w_uv', 'new_v_mlp_norm', 'new_v_mlp_w_up', 'new_v_mlp_w_down', 'new_v_final_norm']
TWIN_LEAF_KINDS = {'loss': 'loss', 'grad_x': 'grad_x', 'grad_hgrn_norm': 'grad_w', 'grad_hgrn_w_q': 'grad_w', 'grad_hgrn_w_f': 'grad_w', 'grad_hgrn_w_i': 'grad_w', 'grad_hgrn_w_g': 'grad_w', 'grad_hgrn_g_norm': 'grad_w', 'grad_hgrn_w_o': 'grad_w', 'grad_hgrn_lb_logits': 'grad_w', 'grad_mla_norm': 'grad_w', 'grad_mla_w_dq': 'grad_w', 'grad_mla_q_norm': 'grad_w', 'grad_mla_w_uq': 'grad_w', 'grad_mla_w_o': 'grad_w', 'grad_kv_in_norm': 'grad_w', 'grad_kv_w_dkv': 'grad_w', 'grad_kv_norm': 'grad_w', 'grad_kv_w_uk': 'grad_w', 'grad_kv_w_uv': 'grad_w', 'grad_mlp_norm': 'grad_w', 'grad_mlp_w_up': 'grad_w', 'grad_mlp_w_down': 'grad_w', 'grad_final_norm': 'grad_w', 'delta_hgrn_norm': 'delta_w', 'delta_hgrn_w_q': 'delta_w', 'delta_hgrn_w_f': 'delta_w', 'delta_hgrn_w_i': 'delta_w', 'delta_hgrn_w_g': 'delta_w', 'delta_hgrn_g_norm': 'delta_w', 'delta_hgrn_w_o': 'delta_w', 'delta_hgrn_lb_logits': 'delta_w', 'delta_mla_norm': 'delta_w', 'delta_mla_w_dq': 'delta_w', 'delta_mla_q_norm': 'delta_w', 'delta_mla_w_uq': 'delta_w', 'delta_mla_w_o': 'delta_w', 'delta_kv_in_norm': 'delta_w', 'delta_kv_w_dkv': 'delta_w', 'delta_kv_norm': 'delta_w', 'delta_kv_w_uk': 'delta_w', 'delta_kv_w_uv': 'delta_w', 'delta_mlp_norm': 'delta_w', 'delta_mlp_w_up': 'delta_w', 'delta_mlp_w_down': 'delta_w', 'delta_final_norm': 'delta_w', 'new_m_hgrn_norm': 'new_m', 'new_m_hgrn_w_q': 'new_m', 'new_m_hgrn_w_f': 'new_m', 'new_m_hgrn_w_i': 'new_m', 'new_m_hgrn_w_g': 'new_m', 'new_m_hgrn_g_norm': 'new_m', 'new_m_hgrn_w_o': 'new_m', 'new_m_hgrn_lb_logits': 'new_m', 'new_m_mla_norm': 'new_m', 'new_m_mla_w_dq': 'new_m', 'new_m_mla_q_norm': 'new_m', 'new_m_mla_w_uq': 'new_m', 'new_m_mla_w_o': 'new_m', 'new_m_kv_in_norm': 'new_m', 'new_m_kv_w_dkv': 'new_m', 'new_m_kv_norm': 'new_m', 'new_m_kv_w_uk': 'new_m', 'new_m_kv_w_uv': 'new_m', 'new_m_mlp_norm': 'new_m', 'new_m_mlp_w_up': 'new_m', 'new_m_mlp_w_down': 'new_m', 'new_m_final_norm': 'new_m', 'new_v_hgrn_norm': 'new_v', 'new_v_hgrn_w_q': 'new_v', 'new_v_hgrn_w_f': 'new_v', 'new_v_hgrn_w_i': 'new_v', 'new_v_hgrn_w_g': 'new_v', 'new_v_hgrn_g_norm': 'new_v', 'new_v_hgrn_w_o': 'new_v', 'new_v_hgrn_lb_logits': 'new_v', 'new_v_mla_norm': 'new_v', 'new_v_mla_w_dq': 'new_v', 'new_v_mla_q_norm': 'new_v', 'new_v_mla_w_uq': 'new_v', 'new_v_mla_w_o': 'new_v', 'new_v_kv_in_norm': 'new_v', 'new_v_kv_w_dkv': 'new_v', 'new_v_kv_norm': 'new_v', 'new_v_kv_w_uk': 'new_v', 'new_v_kv_w_uv': 'new_v', 'new_v_mlp_norm': 'new_v', 'new_v_mlp_w_up': 'new_v', 'new_v_mlp_w_down': 'new_v', 'new_v_final_norm': 'new_v'}


def _forward(args):
    return _fwd_reference(*[args[k] for k in FWD_PARAMS])


def _output_shape():
    out = _jax.eval_shape(lambda: _forward(_fwd_setup_inputs(0)))
    return out.shape, out.dtype

N_MICROBATCH = 1
ADAM_LR = 0.001
ADAM_B1 = 0.9
ADAM_B2 = 0.999
ADAM_EPS = 1e-08
ADAM_WD = 0.01
ADAM_STEP = 10
PER_EXAMPLE_BATCH_AXIS = {'x': 0, 'loss_target': 0}
SHARED_INPUTS = []
_WEIGHT_DTYPES = {'hgrn_norm': _jnp.float32, 'hgrn_w_q': _jnp.float32, 'hgrn_w_f': _jnp.float32, 'hgrn_w_i': _jnp.float32, 'hgrn_w_g': _jnp.float32, 'hgrn_g_norm': _jnp.float32, 'hgrn_w_o': _jnp.float32, 'hgrn_lb_logits': _jnp.float32, 'mla_norm': _jnp.float32, 'mla_w_dq': _jnp.float32, 'mla_q_norm': _jnp.float32, 'mla_w_uq': _jnp.float32, 'mla_w_o': _jnp.float32, 'kv_in_norm': _jnp.float32, 'kv_w_dkv': _jnp.float32, 'kv_norm': _jnp.float32, 'kv_w_uk': _jnp.float32, 'kv_w_uv': _jnp.float32, 'mlp_norm': _jnp.float32, 'mlp_w_up': _jnp.float32, 'mlp_w_down': _jnp.float32, 'final_norm': _jnp.float32}
MOMENT_SCALE = {'hgrn_norm': 1.497393e-01, 'hgrn_w_q': 1.448632e-02, 'hgrn_w_f': 1.302425e-02, 'hgrn_w_i': 1.042175e-01, 'hgrn_w_g': 1.069296e-01, 'hgrn_g_norm': 3.419922e-01, 'hgrn_w_o': 1.032867e-01, 'hgrn_lb_logits': 9.214745e-03, 'mla_norm': 1.765774e-02, 'mla_w_dq': 3.562958e-02, 'mla_q_norm': 4.075413e-02, 'mla_w_uq': 1.027383e-02, 'mla_w_o': 4.711409e-02, 'kv_in_norm': 4.530338e-02, 'kv_w_dkv': 8.664753e-02, 'kv_norm': 1.115471e-01, 'kv_w_uk': 1.046282e-02, 'kv_w_uv': 3.585610e-02, 'mlp_norm': 1.420252e-01, 'mlp_w_up': 6.979030e-02, 'mlp_w_down': 1.436499e-01, 'final_norm': 3.262511e+01}


def _to_microbatches(a, axis):
    t = _jnp.moveaxis(a, axis, 0)
    t = t.reshape((N_MICROBATCH, t.shape[0] // N_MICROBATCH) + t.shape[1:])
    return _jnp.moveaxis(t, 1, axis + 1)


def setup_inputs(seed: int = 0) -> dict:
    inp = _fwd_setup_inputs(seed)
    key = _jax.random.fold_in(_jax.random.key(seed), 7919)
    shape, _ = _output_shape()
    out = dict(inp)
    out["loss_target"] = _jax.random.normal(_jax.random.fold_in(key, 0), shape, _jnp.float32)
    for i, name in enumerate(TWIN_WEIGHTS):
        w = inp[name].astype(_jnp.float32)
        if MOMENT_SCALE is None:
            s = _jnp.sqrt(_jnp.mean(_jnp.square(w)) + 1e-30)
        else:
            s = MOMENT_SCALE[name]
        km, kv = _jax.random.split(_jax.random.fold_in(key, i + 1))
        out[name] = w
        out["m_" + name] = s * _jax.random.normal(km, w.shape, _jnp.float32)
        out["v_" + name] = (s * s) * _jax.random.uniform(kv, w.shape, _jnp.float32, 0.5, 1.5)
    if N_MICROBATCH > 1:
        for name, axis in PER_EXAMPLE_BATCH_AXIS.items():
            out[name] = _to_microbatches(out[name], axis)
    return {'x': out['x'], 'hgrn_norm': out['hgrn_norm'], 'hgrn_w_q': out['hgrn_w_q'], 'hgrn_w_f': out['hgrn_w_f'], 'hgrn_w_i': out['hgrn_w_i'], 'hgrn_w_g': out['hgrn_w_g'], 'hgrn_g_norm': out['hgrn_g_norm'], 'hgrn_w_o': out['hgrn_w_o'], 'hgrn_lb_logits': out['hgrn_lb_logits'], 'mla_norm': out['mla_norm'], 'mla_w_dq': out['mla_w_dq'], 'mla_q_norm': out['mla_q_norm'], 'mla_w_uq': out['mla_w_uq'], 'mla_w_o': out['mla_w_o'], 'kv_in_norm': out['kv_in_norm'], 'kv_w_dkv': out['kv_w_dkv'], 'kv_norm': out['kv_norm'], 'kv_w_uk': out['kv_w_uk'], 'kv_w_uv': out['kv_w_uv'], 'mlp_norm': out['mlp_norm'], 'mlp_w_up': out['mlp_w_up'], 'mlp_w_down': out['mlp_w_down'], 'final_norm': out['final_norm'], 'loss_target': out['loss_target'], 'm_hgrn_norm': out['m_hgrn_norm'], 'm_hgrn_w_q': out['m_hgrn_w_q'], 'm_hgrn_w_f': out['m_hgrn_w_f'], 'm_hgrn_w_i': out['m_hgrn_w_i'], 'm_hgrn_w_g': out['m_hgrn_w_g'], 'm_hgrn_g_norm': out['m_hgrn_g_norm'], 'm_hgrn_w_o': out['m_hgrn_w_o'], 'm_hgrn_lb_logits': out['m_hgrn_lb_logits'], 'm_mla_norm': out['m_mla_norm'], 'm_mla_w_dq': out['m_mla_w_dq'], 'm_mla_q_norm': out['m_mla_q_norm'], 'm_mla_w_uq': out['m_mla_w_uq'], 'm_mla_w_o': out['m_mla_w_o'], 'm_kv_in_norm': out['m_kv_in_norm'], 'm_kv_w_dkv': out['m_kv_w_dkv'], 'm_kv_norm': out['m_kv_norm'], 'm_kv_w_uk': out['m_kv_w_uk'], 'm_kv_w_uv': out['m_kv_w_uv'], 'm_mlp_norm': out['m_mlp_norm'], 'm_mlp_w_up': out['m_mlp_w_up'], 'm_mlp_w_down': out['m_mlp_w_down'], 'm_final_norm': out['m_final_norm'], 'v_hgrn_norm': out['v_hgrn_norm'], 'v_hgrn_w_q': out['v_hgrn_w_q'], 'v_hgrn_w_f': out['v_hgrn_w_f'], 'v_hgrn_w_i': out['v_hgrn_w_i'], 'v_hgrn_w_g': out['v_hgrn_w_g'], 'v_hgrn_g_norm': out['v_hgrn_g_norm'], 'v_hgrn_w_o': out['v_hgrn_w_o'], 'v_hgrn_lb_logits': out['v_hgrn_lb_logits'], 'v_mla_norm': out['v_mla_norm'], 'v_mla_w_dq': out['v_mla_w_dq'], 'v_mla_q_norm': out['v_mla_q_norm'], 'v_mla_w_uq': out['v_mla_w_uq'], 'v_mla_w_o': out['v_mla_w_o'], 'v_kv_in_norm': out['v_kv_in_norm'], 'v_kv_w_dkv': out['v_kv_w_dkv'], 'v_kv_norm': out['v_kv_norm'], 'v_kv_w_uk': out['v_kv_w_uk'], 'v_kv_w_uv': out['v_kv_w_uv'], 'v_mlp_norm': out['v_mlp_norm'], 'v_mlp_w_up': out['v_mlp_w_up'], 'v_mlp_w_down': out['v_mlp_w_down'], 'v_final_norm': out['v_final_norm']}


def _loss(weights, diff, rest, loss_target):
    with _jax.named_scope("forward"):
        args = {**rest, TWIN_DIFF_INPUT: diff, **{k: w.astype(_WEIGHT_DTYPES[k]) for k, w in weights.items()}}
        y = _forward(args)
    with _jax.named_scope("loss_head"):
        err = _jnp.square(y.astype(_jnp.float32) - loss_target)
        return 0.5 * _jnp.sum(_jnp.mean(err, axis=-1)) if err.ndim else 0.5 * err


def _adamw(w, g, m, v):
    m = ADAM_B1 * m + (1.0 - ADAM_B1) * g
    v = ADAM_B2 * v + (1.0 - ADAM_B2) * _jnp.square(g)
    m_hat = m / (1.0 - ADAM_B1 ** ADAM_STEP)
    v_hat = v / (1.0 - ADAM_B2 ** ADAM_STEP)
    delta = -ADAM_LR * (m_hat / (_jnp.sqrt(v_hat) + ADAM_EPS) + ADAM_WD * w)
    return delta, m, v


def reference(x, hgrn_norm, hgrn_w_q, hgrn_w_f, hgrn_w_i, hgrn_w_g, hgrn_g_norm, hgrn_w_o, hgrn_lb_logits, mla_norm, mla_w_dq, mla_q_norm, mla_w_uq, mla_w_o, kv_in_norm, kv_w_dkv, kv_norm, kv_w_uk, kv_w_uv, mlp_norm, mlp_w_up, mlp_w_down, final_norm, loss_target, m_hgrn_norm, m_hgrn_w_q, m_hgrn_w_f, m_hgrn_w_i, m_hgrn_w_g, m_hgrn_g_norm, m_hgrn_w_o, m_hgrn_lb_logits, m_mla_norm, m_mla_w_dq, m_mla_q_norm, m_mla_w_uq, m_mla_w_o, m_kv_in_norm, m_kv_w_dkv, m_kv_norm, m_kv_w_uk, m_kv_w_uv, m_mlp_norm, m_mlp_w_up, m_mlp_w_down, m_final_norm, v_hgrn_norm, v_hgrn_w_q, v_hgrn_w_f, v_hgrn_w_i, v_hgrn_w_g, v_hgrn_g_norm, v_hgrn_w_o, v_hgrn_lb_logits, v_mla_norm, v_mla_w_dq, v_mla_q_norm, v_mla_w_uq, v_mla_w_o, v_kv_in_norm, v_kv_w_dkv, v_kv_norm, v_kv_w_uk, v_kv_w_uv, v_mlp_norm, v_mlp_w_up, v_mlp_w_down, v_final_norm):
    given = dict(x=x, hgrn_norm=hgrn_norm, hgrn_w_q=hgrn_w_q, hgrn_w_f=hgrn_w_f, hgrn_w_i=hgrn_w_i, hgrn_w_g=hgrn_w_g, hgrn_g_norm=hgrn_g_norm, hgrn_w_o=hgrn_w_o, hgrn_lb_logits=hgrn_lb_logits, mla_norm=mla_norm, mla_w_dq=mla_w_dq, mla_q_norm=mla_q_norm, mla_w_uq=mla_w_uq, mla_w_o=mla_w_o, kv_in_norm=kv_in_norm, kv_w_dkv=kv_w_dkv, kv_norm=kv_norm, kv_w_uk=kv_w_uk, kv_w_uv=kv_w_uv, mlp_norm=mlp_norm, mlp_w_up=mlp_w_up, mlp_w_down=mlp_w_down, final_norm=final_norm, loss_target=loss_target, m_hgrn_norm=m_hgrn_norm, m_hgrn_w_q=m_hgrn_w_q, m_hgrn_w_f=m_hgrn_w_f, m_hgrn_w_i=m_hgrn_w_i, m_hgrn_w_g=m_hgrn_w_g, m_hgrn_g_norm=m_hgrn_g_norm, m_hgrn_w_o=m_hgrn_w_o, m_hgrn_lb_logits=m_hgrn_lb_logits, m_mla_norm=m_mla_norm, m_mla_w_dq=m_mla_w_dq, m_mla_q_norm=m_mla_q_norm, m_mla_w_uq=m_mla_w_uq, m_mla_w_o=m_mla_w_o, m_kv_in_norm=m_kv_in_norm, m_kv_w_dkv=m_kv_w_dkv, m_kv_norm=m_kv_norm, m_kv_w_uk=m_kv_w_uk, m_kv_w_uv=m_kv_w_uv, m_mlp_norm=m_mlp_norm, m_mlp_w_up=m_mlp_w_up, m_mlp_w_down=m_mlp_w_down, m_final_norm=m_final_norm, v_hgrn_norm=v_hgrn_norm, v_hgrn_w_q=v_hgrn_w_q, v_hgrn_w_f=v_hgrn_w_f, v_hgrn_w_i=v_hgrn_w_i, v_hgrn_w_g=v_hgrn_w_g, v_hgrn_g_norm=v_hgrn_g_norm, v_hgrn_w_o=v_hgrn_w_o, v_hgrn_lb_logits=v_hgrn_lb_logits, v_mla_norm=v_mla_norm, v_mla_w_dq=v_mla_w_dq, v_mla_q_norm=v_mla_q_norm, v_mla_w_uq=v_mla_w_uq, v_mla_w_o=v_mla_w_o, v_kv_in_norm=v_kv_in_norm, v_kv_w_dkv=v_kv_w_dkv, v_kv_norm=v_kv_norm, v_kv_w_uk=v_kv_w_uk, v_kv_w_uv=v_kv_w_uv, v_mlp_norm=v_mlp_norm, v_mlp_w_up=v_mlp_w_up, v_mlp_w_down=v_mlp_w_down, v_final_norm=v_final_norm)
    weights = {n: given[n] for n in TWIN_WEIGHTS}
    shared = {n: given[n] for n in SHARED_INPUTS}
    per_example = {n: given[n] for n in ['x']}
    grad_fn = _jax.value_and_grad(_loss, argnums=(0, 1))

    def one_microbatch(ex, loss_target):
        ex = dict(ex)
        diff = ex.pop(TWIN_DIFF_INPUT)
        return grad_fn(weights, diff, {**shared, **ex}, loss_target)

    if N_MICROBATCH == 1:
        loss, (grad_w, grad_x) = one_microbatch(per_example, given["loss_target"])
    else:
        def body(carry, xs):
            loss_sum, grad_sum = carry
            l_k, (gw_k, gx_k) = one_microbatch(xs[0], xs[1])
            with _jax.named_scope("update"):
                return (loss_sum + l_k, _jax.tree.map(_jnp.add, grad_sum, gw_k)), gx_k

        init = (_jnp.zeros((), _jnp.float32), _jax.tree.map(_jnp.zeros_like, weights))
        (loss, grad_w), grad_x = _jax.lax.scan(body, init, (per_example, given["loss_target"]))
    with _jax.named_scope("update"):
        delta_w, new_m, new_v = {}, {}, {}
        for n in TWIN_WEIGHTS:
            delta_w[n], new_m[n], new_v[n] = _adamw(weights[n], grad_w[n], given["m_" + n], given["v_" + n])
    return (loss, grad_x, *[grad_w[n] for n in TWIN_WEIGHTS], *[delta_w[n] for n in TWIN_WEIGHTS],
            *[new_m[n] for n in TWIN_WEIGHTS], *[new_v[n] for n in TWIN_WEIGHTS])
```

```python
import functools

import jax
import jax.numpy as jnp
from jax import lax
from jax.experimental import pallas as pl
from jax.experimental.pallas import tpu as pltpu

F32 = jnp.float32
BF16 = jnp.bfloat16

EPS = 1e-6
LANES = 128
N_DEV = 8
V7X_VMEM_LIMIT_BYTES = 56 << 20

HGRN_HEADS = 8
HGRN_CHUNK = 64
HGRN_SUB = 16
EXP_CLAMP = 80.0
MLA_HEADS = 16
MLA_NOPE = 128
MLA_ROPE = 64
ROPE_THETA = 10000.0
ATTN_SCALE = (MLA_NOPE + MLA_ROPE) ** -0.5

ADAM_LR = 0.001
ADAM_B1 = 0.9
ADAM_B2 = 0.999
ADAM_EPS = 1e-08
ADAM_WD = 0.01
ADAM_STEP = 10

_NN = ((1,), (0,))
_NT = ((1,), (1,))
_TN = ((0,), (0,))


def _params(*sem):
    return pltpu.CompilerParams(dimension_semantics=sem, vmem_limit_bytes=V7X_VMEM_LIMIT_BYTES)


def _dot(a, b, dims):
    return lax.dot_general(a.astype(BF16), b.astype(BF16), (dims, ((), ())), preferred_element_type=F32)


def _dot_f32(a, b, dims=_NN):
    return lax.dot_general(a, b, (dims, ((), ())), precision=lax.Precision.HIGHEST, preferred_element_type=F32)


def _sigmoid(x):
    return 1.0 / (1.0 + jnp.exp(-x))


def _rms(x, w):
    r = lax.rsqrt(jnp.mean(x * x, axis=-1, keepdims=True) + EPS)
    return x * r * w


def _rms_bwd(x, w, dy):
    r = lax.rsqrt(jnp.mean(x * x, axis=-1, keepdims=True) + EPS)
    xh = x * r
    dw = jnp.sum(dy * xh, axis=0, keepdims=True)
    dxh = dy * w
    dx = r * (dxh - xh * jnp.mean(dxh * xh, axis=-1, keepdims=True))
    return dx, dw


def _mm(a, b, *, mode, name, out_dtype=F32, add=None, epilogue=None, aux=None):
    if mode == "nn":
        (m, k), (k2, n) = a.shape, b.shape
    elif mode == "nt":
        (m, k), (n, k2) = a.shape, b.shape
    else:
        (k, m), (k2, n) = a.shape, b.shape
    assert k == k2, (name, a.shape, b.shape)
    tm, tn = min(m, 512), min(n, 512)
    assert m % tm == 0 and n % tn == 0, (name, m, n)
    dims = {"nn": _NN, "nt": _NT, "tn": _TN}[mode]
    a_spec = pl.BlockSpec((k, tm), lambda i, j: (0, i)) if mode == "tn" else pl.BlockSpec((tm, k), lambda i, j: (i, 0))
    b_spec = pl.BlockSpec((tn, k), lambda i, j: (j, 0)) if mode == "nt" else pl.BlockSpec((k, tn), lambda i, j: (0, j))
    o_spec = pl.BlockSpec((tm, tn), lambda i, j: (i, j))
    operands, in_specs = [a, b], [a_spec, b_spec]
    for extra in (add, aux):
        if extra is not None:
            assert extra.shape == (m, n), (name, extra.shape)
            operands.append(extra)
            in_specs.append(o_spec)
    if epilogue == "relu2":
        out_shape = [jax.ShapeDtypeStruct((m, n), F32), jax.ShapeDtypeStruct((m, n), BF16)]
        out_specs = [o_spec, o_spec]
    else:
        out_shape = jax.ShapeDtypeStruct((m, n), out_dtype)
        out_specs = o_spec

    def body(*refs):
        acc = _dot(refs[0][...], refs[1][...], dims)
        rest = refs[2:]
        if add is not None:
            acc = acc + rest[0][...]
            rest = rest[1:]
        if epilogue == "relu2":
            rest[0][...] = acc
            rest[1][...] = jnp.square(jnp.maximum(acc, 0.0)).astype(BF16)
        elif epilogue == "relu2_bwd":
            rest[1][...] = (acc * (2.0 * jnp.maximum(rest[0][...], 0.0))).astype(out_dtype)
        else:
            rest[0][...] = acc.astype(out_dtype)

    return pl.pallas_call(
        body, name=name, grid=(m // tm, n // tn), in_specs=in_specs, out_specs=out_specs, out_shape=out_shape,
        compiler_params=_params("parallel", "parallel"),
    )(*operands)


def _rowcall(fn, rows, consts, outs, accs, *, name, tr=256):
    s = rows[0].shape[0]
    tr = min(tr, s)
    assert s % tr == 0
    n_out = len(outs)
    in_specs = [pl.BlockSpec((tr, r.shape[1]), lambda i: (i, 0)) for r in rows]
    in_specs += [pl.BlockSpec(c.shape, lambda i: (0, 0)) for c in consts]
    out_shape = [jax.ShapeDtypeStruct((s, w), dt) for w, dt in outs] + [jax.ShapeDtypeStruct((1, w), F32) for w in accs]
    out_specs = [pl.BlockSpec((tr, w), lambda i: (i, 0)) for w, _ in outs] + [pl.BlockSpec((1, w), lambda i: (0, 0)) for w in accs]
    n_in = len(rows) + len(consts)

    def body(*refs):
        res = fn(*[r[...] for r in refs[:n_in]])
        out_refs = refs[n_in:]
        for ref, val in zip(out_refs[:n_out], res[:n_out]):
            ref[...] = val.astype(ref.dtype)
        i = pl.program_id(0)
        for ref, val in zip(out_refs[n_out:], res[n_out:]):
            @pl.when(i == 0)
            def _(ref=ref, val=val):
                ref[...] = val

            @pl.when(i > 0)
            def _(ref=ref, val=val):
                ref[...] += val

    return pl.pallas_call(
        body, name=name, grid=(s // tr,), in_specs=in_specs, out_specs=out_specs, out_shape=out_shape,
        compiler_params=_params("arbitrary" if accs else "parallel"),
    )(*rows, *consts)


def _rope_tables(seq):
    half = MLA_ROPE // 2
    inv_freq = ROPE_THETA ** (-jnp.arange(half, dtype=F32) / half)
    ang = jnp.arange(seq, dtype=F32)[:, None] * inv_freq[None, :]
    cos, sin, zero = jnp.cos(ang), jnp.sin(ang), jnp.zeros((seq, half), F32)
    t_c = jnp.concatenate([cos, cos, zero, zero], axis=1)
    t_s1 = jnp.concatenate([-sin, zero, zero, zero], axis=1)
    t_s2 = jnp.concatenate([zero, sin, zero, zero], axis=1)
    return t_c, t_s1, t_s2


def _rope(slab, t_c, t_s1, t_s2):
    return slab * t_c + pltpu.roll(slab, 96, 1) * t_s1 + pltpu.roll(slab, 32, 1) * t_s2


def _rope_t(d, t_c, t_s1, t_s2):
    return d * t_c + pltpu.roll(d * t_s1, 32, 1) + pltpu.roll(d * t_s2, 96, 1)


def _lower_bound(logits):
    l0, l1 = logits[0:1, :], logits[1:2, :]
    mx = jnp.maximum(l0, l1)
    e0, e1 = jnp.exp(l0 - mx), jnp.exp(l1 - mx)
    return e0 / (e0 + e1)


def _tri(n, lower):
    row = lax.broadcasted_iota(jnp.int32, (n, n), 0)
    col = lax.broadcasted_iota(jnp.int32, (n, n), 1)
    return (row >= col) if lower else (row <= col)


def _hgrn_intra(q, k, b, b_sc):
    c = HGRN_CHUNK
    b_sc[...] = b
    qts, decs, scores = [], [], []
    for i in range(c // HGRN_SUB):
        lo = i * HGRN_SUB
        ref = b_sc[lo - 1:lo, :] if i > 0 else jnp.zeros((1, LANES), F32)
        qt = q[lo:lo + HGRN_SUB, :] * jnp.exp(b[lo:lo + HGRN_SUB, :] - ref)
        dec = jnp.exp(jnp.minimum(ref - b, EXP_CLAMP))
        qts.append(qt)
        decs.append(dec)
        scores.append(_dot(qt, k * dec, _NT))
    a = jnp.where(_tri(c, True), jnp.concatenate(scores, axis=0), 0.0)
    return a, qts, decs


def _hgrn_fwd(zq, zf, zi, lb_logits, *, name):
    s, d = zq.shape
    h_n, c = d // LANES, HGRN_CHUNK
    nc = s // c

    def body(zq_ref, zf_ref, zi_ref, lb_ref, o_ref, st_ref, state_sc, b_sc):
        @pl.when(pl.program_id(1) == 0)
        def _():
            state_sc[...] = jnp.zeros_like(state_sc)

        lb = _lower_bound(lb_ref[...])
        zq_v = zq_ref[...]
        q = zq_v * _sigmoid(zq_v)
        f = lb + (1.0 - lb) * _sigmoid(zf_ref[...])
        g = jnp.log(f)
        k = 1.0 - f
        v = zi_ref[...]
        b = _dot_f32(_tri(c, True).astype(F32), g)
        s0t = state_sc[...]
        st_ref[...] = s0t
        a, _, _ = _hgrn_intra(q, k, b, b_sc)
        o_ref[...] = _dot(q * jnp.exp(b), s0t, _NT) + _dot(a, v, _NN)
        bl = b_sc[c - 1:c, :]
        state_sc[...] = s0t * jnp.exp(bl) + _dot(v, k * jnp.exp(bl - b), _TN)

    tile = pl.BlockSpec((c, LANES), lambda h, i: (i, h))
    return pl.pallas_call(
        body, name=name, grid=(h_n, nc),
        in_specs=[tile, tile, tile, pl.BlockSpec((2, LANES), lambda h, i: (0, h))],
        out_specs=[tile, pl.BlockSpec((None, None, LANES, LANES), lambda h, i: (h, i, 0, 0))],
        out_shape=[jax.ShapeDtypeStruct((s, d), F32), jax.ShapeDtypeStruct((h_n, nc, LANES, LANES), F32)],
        scratch_shapes=[pltpu.VMEM((LANES, LANES), F32), pltpu.VMEM((c, LANES), F32)],
        compiler_params=_params("parallel", "arbitrary"),
    )(zq, zf, zi, lb_logits)


def _hgrn_bwd(zq, zf, zi, lb_logits, states, do, *, name):
    s, d = zq.shape
    h_n, c = d // LANES, HGRN_CHUNK
    nc = s // c

    def body(zq_ref, zf_ref, zi_ref, lb_ref, st_ref, do_ref, dzq_ref, dzf_ref, dzi_ref, dlb_ref, dstate_sc, b_sc):
        @pl.when(pl.program_id(1) == 0)
        def _():
            dstate_sc[...] = jnp.zeros_like(dstate_sc)
            dlb_ref[...] = jnp.zeros_like(dlb_ref)

        lb = _lower_bound(lb_ref[...])
        zq_v = zq_ref[...]
        sq = _sigmoid(zq_v)
        q = zq_v * sq
        sf = _sigmoid(zf_ref[...])
        f = lb + (1.0 - lb) * sf
        g = jnp.log(f)
        k = 1.0 - f
        v = zi_ref[...]
        d_o = do_ref[...]
        b = _dot_f32(_tri(c, True).astype(F32), g)
        s0t = st_ref[...]
        ds1t = dstate_sc[...]
        a, qts, decs = _hgrn_intra(q, k, b, b_sc)
        bl = b_sc[c - 1:c, :]
        eb, ebl, dec_end = jnp.exp(b), jnp.exp(bl), jnp.exp(bl - b)
        qe = q * eb
        da = jnp.where(_tri(c, True), _dot(d_o, v, _NT), 0.0)
        dv = _dot(a, d_o, _TN) + _dot(k * dec_end, ds1t, _NT)
        dk_state = _dot(v, ds1t, _NN) * dec_end
        dk = dk_state
        dq_blocks = []
        for i in range(c // HGRN_SUB):
            lo = i * HGRN_SUB
            ref = b_sc[lo - 1:lo, :] if i > 0 else jnp.zeros((1, LANES), F32)
            da_i = da[lo:lo + HGRN_SUB, :]
            dq_blocks.append(_dot_f32(da_i, k * decs[i], _NN) * jnp.exp(b[lo:lo + HGRN_SUB, :] - ref))
            dk = dk + _dot_f32(da_i, qts[i], _TN) * decs[i]
        dq = _dot(d_o, s0t, _NN) * eb + jnp.concatenate(dq_blocks, axis=0)
        db_last = jnp.sum(k * dk_state, axis=0, keepdims=True) + ebl * jnp.sum(s0t * ds1t, axis=0, keepdims=True)
        last_row = lax.broadcasted_iota(jnp.int32, (c, LANES), 0) == c - 1
        db = q * dq - k * dk + jnp.where(last_row, db_last, 0.0)
        dg = _dot_f32(_tri(c, False).astype(F32), db)
        df = dg / f - dk
        dzf_ref[...] = (df * (1.0 - lb) * sf * (1.0 - sf)).astype(BF16)
        dlb_ref[...] += jnp.sum(df * (1.0 - sf), axis=0, keepdims=True)
        dzq_ref[...] = (dq * sq * (1.0 + zq_v * (1.0 - sq))).astype(BF16)
        dzi_ref[...] = dv.astype(BF16)
        dstate_sc[...] = ds1t * ebl + _dot(d_o, qe, _TN)

    tile = pl.BlockSpec((c, LANES), lambda h, i: (nc - 1 - i, h))
    out = jax.ShapeDtypeStruct((s, d), BF16)
    return pl.pallas_call(
        body, name=name, grid=(h_n, nc),
        in_specs=[tile, tile, tile, pl.BlockSpec((2, LANES), lambda h, i: (0, h)),
                  pl.BlockSpec((None, None, LANES, LANES), lambda h, i: (h, nc - 1 - i, 0, 0)), tile],
        out_specs=[tile, tile, tile, pl.BlockSpec((1, LANES), lambda h, i: (0, h))],
        out_shape=[out, out, out, jax.ShapeDtypeStruct((1, d), F32)],
        scratch_shapes=[pltpu.VMEM((LANES, LANES), F32), pltpu.VMEM((c, LANES), F32)],
        compiler_params=_params("parallel", "arbitrary"),
    )(zq, zf, zi, lb_logits, states, do)


def _attn_tile(s):
    return min(512, max(128, s // 2))


def _causal_scores(qn_ref, qr_ref, kn_ref, kr_ref, q_blk, k_blk, t):
    q = jnp.concatenate([qn_ref[...], qr_ref[...]], axis=1)
    k = jnp.concatenate([kn_ref[...], kr_ref[...]], axis=1)
    sc = _dot(q, k, _NT) * ATTN_SCALE
    q_pos = q_blk * t + lax.broadcasted_iota(jnp.int32, (t, t), 0)
    k_pos = k_blk * t + lax.broadcasted_iota(jnp.int32, (t, t), 1)
    return q, k, jnp.where(k_pos <= q_pos, sc, -jnp.inf)


def _attn_fwd(qn, qr, kn, kr, v, *, name):
    s, t = qn.shape[0], _attn_tile(qn.shape[0])
    n = s // t

    def body(qn_ref, qr_ref, kn_ref, kr_ref, v_ref, o_ref, lse_ref, m_sc, l_sc, acc_sc):
        i, j = pl.program_id(1), pl.program_id(2)

        @pl.when(j == 0)
        def _():
            m_sc[...] = jnp.full_like(m_sc, -jnp.inf)
            l_sc[...] = jnp.zeros_like(l_sc)
            acc_sc[...] = jnp.zeros_like(acc_sc)

        @pl.when(j <= i)
        def _():
            _, _, sc = _causal_scores(qn_ref, qr_ref, kn_ref, kr_ref, i, j, t)
            m_prev = m_sc[...]
            m_new = jnp.maximum(m_prev, jnp.max(sc, axis=1, keepdims=True))
            alpha = jnp.exp(m_prev - m_new)
            p = jnp.exp(sc - m_new[:, :1])
            l_sc[...] = alpha * l_sc[...] + jnp.sum(p, axis=1, keepdims=True)
            acc_sc[...] = alpha * acc_sc[...] + _dot(p, v_ref[...], _NN)
            m_sc[...] = m_new

        @pl.when(j == i)
        def _():
            o_ref[...] = (acc_sc[...] / l_sc[...]).astype(BF16)
            lse_ref[...] = m_sc[...] + jnp.log(l_sc[...])

    q_spec = pl.BlockSpec((t, LANES), lambda h, i, j: (i, h))
    k_spec = pl.BlockSpec((t, LANES), lambda h, i, j: (jnp.minimum(i, j), h))
    kr_spec = pl.BlockSpec((t, LANES), lambda h, i, j: (jnp.minimum(i, j), 0))
    stat = pltpu.VMEM((t, LANES), F32)
    return pl.pallas_call(
        body, name=name, grid=(MLA_HEADS, n, n),
        in_specs=[q_spec, q_spec, k_spec, kr_spec, k_spec], out_specs=[q_spec, q_spec],
        out_shape=[jax.ShapeDtypeStruct(qn.shape, BF16), jax.ShapeDtypeStruct(qn.shape, F32)],
        scratch_shapes=[stat, stat, stat],
        compiler_params=_params("parallel", "parallel", "arbitrary"),
    )(qn, qr, kn, kr, v)


def _attn_probs(qn_ref, qr_ref, kn_ref, kr_ref, v_ref, do_ref, lse_ref, delta_ref, q_blk, k_blk, t):
    q, k, sc = _causal_scores(qn_ref, qr_ref, kn_ref, kr_ref, q_blk, k_blk, t)
    p = jnp.exp(sc - lse_ref[...][:, :1])
    dp = _dot(do_ref[...], v_ref[...], _NT)
    ds = p * (dp - delta_ref[...][:, :1]) * ATTN_SCALE
    return q, k, p, ds


def _attn_bwd_q(qn, qr, kn, kr, v, do, lse, delta, *, name):
    s, t = qn.shape[0], _attn_tile(qn.shape[0])
    n = s // t

    def body(qn_ref, qr_ref, kn_ref, kr_ref, v_ref, do_ref, lse_ref, delta_ref, dqn_ref, dqr_ref, acc_sc):
        i, j = pl.program_id(1), pl.program_id(2)

        @pl.when(j == 0)
        def _():
            acc_sc[...] = jnp.zeros_like(acc_sc)

        @pl.when(j <= i)
        def _():
            _, k, _, ds = _attn_probs(qn_ref, qr_ref, kn_ref, kr_ref, v_ref, do_ref, lse_ref, delta_ref, i, j, t)
            acc_sc[...] += _dot(ds, k, _NN)

        @pl.when(j == i)
        def _():
            dqn_ref[...] = acc_sc[:, :LANES]
            dqr_ref[...] = acc_sc[:, LANES:]

    q_spec = pl.BlockSpec((t, LANES), lambda h, i, j: (i, h))
    k_spec = pl.BlockSpec((t, LANES), lambda h, i, j: (jnp.minimum(i, j), h))
    kr_spec = pl.BlockSpec((t, LANES), lambda h, i, j: (jnp.minimum(i, j), 0))
    out = jax.ShapeDtypeStruct(qn.shape, F32)
    return pl.pallas_call(
        body, name=name, grid=(MLA_HEADS, n, n),
        in_specs=[q_spec, q_spec, k_spec, kr_spec, k_spec, q_spec, q_spec, q_spec], out_specs=[q_spec, q_spec],
        out_shape=[out, out], scratch_shapes=[pltpu.VMEM((t, 2 * LANES), F32)],
        compiler_params=_params("parallel", "parallel", "arbitrary"),
    )(qn, qr, kn, kr, v, do, lse, delta)


def _attn_bwd_kv(qn, qr, kn, kr, v, do, lse, delta, *, name):
    s, t = qn.shape[0], _attn_tile(qn.shape[0])
    n = s // t

    def body(qn_ref, qr_ref, kn_ref, kr_ref, v_ref, do_ref, lse_ref, delta_ref, dkn_ref, dv_ref, dkr_ref, dk_sc, dv_sc):
        j, h, i = pl.program_id(0), pl.program_id(1), pl.program_id(2)

        @pl.when(i == 0)
        def _():
            dk_sc[...] = jnp.zeros_like(dk_sc)
            dv_sc[...] = jnp.zeros_like(dv_sc)

        @pl.when(i >= j)
        def _():
            q, _, p, ds = _attn_probs(qn_ref, qr_ref, kn_ref, kr_ref, v_ref, do_ref, lse_ref, delta_ref, i, j, t)
            dv_sc[...] += _dot(p, do_ref[...], _TN)
            dk_sc[...] += _dot(ds, q, _TN)

        @pl.when(i == n - 1)
        def _():
            dkn_ref[...] = dk_sc[:, :LANES].astype(BF16)
            dv_ref[...] = dv_sc[...].astype(BF16)

            @pl.when(h == 0)
            def _():
                dkr_ref[...] = dk_sc[:, LANES:]

            @pl.when(h > 0)
            def _():
                dkr_ref[...] += dk_sc[:, LANES:]

    q_spec = pl.BlockSpec((t, LANES), lambda j, h, i: (jnp.maximum(i, j), h))
    k_spec = pl.BlockSpec((t, LANES), lambda j, h, i: (j, h))
    kr_spec = pl.BlockSpec((t, LANES), lambda j, h, i: (j, 0))
    out = jax.ShapeDtypeStruct(qn.shape, BF16)
    return pl.pallas_call(
        body, name=name, grid=(n, MLA_HEADS, n),
        in_specs=[q_spec, q_spec, k_spec, kr_spec, k_spec, q_spec, q_spec, q_spec], out_specs=[k_spec, k_spec, kr_spec],
        out_shape=[out, out, jax.ShapeDtypeStruct((s, LANES), F32)],
        scratch_shapes=[pltpu.VMEM((t, 2 * LANES), F32), pltpu.VMEM((t, LANES), F32)],
        compiler_params=_params("parallel", "arbitrary", "arbitrary"),
    )(qn, qr, kn, kr, v, do, lse, delta)


def _exchange(arrs, *, scatter, name):
    n = len(arrs)
    out_shape = [jax.ShapeDtypeStruct(a.shape if scatter else (N_DEV, *a.shape), a.dtype) for a in arrs]

    def body(*refs):
        ins, outs = refs[:n], refs[n:2 * n]
        send_sems, recv_sems, local_sems = refs[2 * n:]
        x, y, c = lax.axis_index("x"), lax.axis_index("y"), lax.axis_index("c")
        me = 4 * x + 2 * y + c
        copies = []
        for k in range(n):
            local = pltpu.make_async_copy(ins[k].at[me] if scatter else ins[k], outs[k].at[me], local_sems.at[k])
            local.start()
            copies.append(local)
            for d in range(1, N_DEV):
                px, py, pc = (x + (d >> 2)) % 2, (y + ((d >> 1) & 1)) % 2, (c + (d & 1)) % 2
                peer = 4 * px + 2 * py + pc
                remote = pltpu.make_async_remote_copy(
                    src_ref=ins[k].at[peer] if scatter else ins[k], dst_ref=outs[k].at[me],
                    send_sem=send_sems.at[k, d - 1], recv_sem=recv_sems.at[k, d - 1],
                    device_id=(px, py, pc), device_id_type=pl.DeviceIdType.MESH)
                remote.start()
                copies.append(remote)
        for cp in copies:
            cp.wait()

    any_spec = pl.BlockSpec(memory_space=pl.ANY)
    return pl.pallas_call(
        body, name=name, in_specs=[any_spec] * n, out_specs=[any_spec] * n, out_shape=out_shape,
        scratch_shapes=[pltpu.SemaphoreType.DMA((n, N_DEV - 1)), pltpu.SemaphoreType.DMA((n, N_DEV - 1)),
                        pltpu.SemaphoreType.DMA((n,))],
    )(*arrs)


def _adam(w, terms, m, v, *, name):
    r, c = w.shape
    n = terms.shape[0]
    tr = min(r, 128)
    assert r % tr == 0

    def body(w_ref, t_ref, m_ref, v_ref, g_out, d_out, m_out, v_out):
        g = t_ref[0]
        for s in range(1, n):
            g = g + t_ref[s]
        m1 = ADAM_B1 * m_ref[...] + (1.0 - ADAM_B1) * g
        v1 = ADAM_B2 * v_ref[...] + (1.0 - ADAM_B2) * jnp.square(g)
        m_hat = m1 / (1.0 - ADAM_B1 ** ADAM_STEP)
        v_hat = v1 / (1.0 - ADAM_B2 ** ADAM_STEP)
        g_out[...] = g
        d_out[...] = -ADAM_LR * (m_hat / (jnp.sqrt(v_hat) + ADAM_EPS) + ADAM_WD * w_ref[...])
        m_out[...] = m1
        v_out[...] = v1

    spec = pl.BlockSpec((tr, c), lambda i: (i, 0))
    out = jax.ShapeDtypeStruct((r, c), F32)
    return pl.pallas_call(
        body, name=name, grid=(r // tr,),
        in_specs=[spec, pl.BlockSpec((n, tr, c), lambda i: (0, i, 0)), spec, spec], out_specs=[spec] * 4,
        out_shape=[out] * 4, compiler_params=_params("parallel"),
    )(w, terms, m, v)


def _sum_terms(terms, *, name):
    n, _, p = terms.shape

    def body(t_ref, o_ref):
        acc = t_ref[0]
        for s in range(1, n):
            acc = acc + t_ref[s]
        o_ref[...] = acc

    return pl.pallas_call(body, name=name, out_shape=jax.ShapeDtypeStruct((1, p), F32))(terms)


def _lb_logits_grad(dlb, logits, *, name):
    def body(dlb_ref, l_ref, o_ref):
        lb = _lower_bound(l_ref[...])
        d0 = dlb_ref[...] * lb * (1.0 - lb)
        o_ref[...] = jnp.concatenate([d0, -d0], axis=0)

    return pl.pallas_call(body, name=name, out_shape=jax.ShapeDtypeStruct(logits.shape, F32))(dlb, logits)


def _silu_grad(z):
    sg = _sigmoid(z)
    return sg * (1.0 + z * (1.0 - sg))


def _head_norm_gate(o, zg, gn):
    outs = []
    for h in range(HGRN_HEADS):
        sl = slice(h * LANES, (h + 1) * LANES)
        zg_h = zg[:, sl]
        outs.append(_rms(o[:, sl], gn) * (zg_h * _sigmoid(zg_h)))
    return (jnp.concatenate(outs, axis=1),)


def _head_norm_gate_bwd(o, zg, dm, gn):
    do_parts, dzg_parts, dgn = [], [], jnp.zeros((1, LANES), F32)
    for h in range(HGRN_HEADS):
        sl = slice(h * LANES, (h + 1) * LANES)
        o_h, zg_h, dm_h = o[:, sl], zg[:, sl], dm[:, sl]
        gate = zg_h * _sigmoid(zg_h)
        do_h, dgn_h = _rms_bwd(o_h, gn, dm_h * gate)
        dgn = dgn + dgn_h
        do_parts.append(do_h)
        dzg_parts.append(dm_h * _rms(o_h, gn) * _silu_grad(zg_h))
    return jnp.concatenate(do_parts, axis=1), jnp.concatenate(dzg_parts, axis=1), dgn


def _rope_slabs(x, t_c, t_s1, t_s2, transpose):
    fn = _rope_t if transpose else _rope
    return jnp.concatenate(
        [fn(x[:, h * LANES:(h + 1) * LANES], t_c, t_s1, t_s2) for h in range(x.shape[1] // LANES)], axis=1)


def _loss_head(h, tgt, w):
    d = h.shape[1]
    r = lax.rsqrt(jnp.mean(h * h, axis=-1, keepdims=True) + EPS)
    xh = h * r
    err = xh * w - tgt
    loss = 0.5 * jnp.sum(jnp.mean(err * err, axis=-1, keepdims=True), axis=0, keepdims=True)
    dy = err / d
    dxh = dy * w
    dh = r * (dxh - xh * jnp.mean(dxh * xh, axis=-1, keepdims=True))
    return dh, jnp.sum(dy * xh, axis=0, keepdims=True), jnp.broadcast_to(loss, (1, LANES))


def _mlp_fwd(h, norm, w_up, w_down, tag):
    d = h.shape[1]
    xn = _rowcall(lambda x, w: (_rms(x, w),), [h], [norm], [(d, BF16)], [], name=f"{tag}_norm")[0]
    u, act = _mm(xn, w_up, mode="nn", epilogue="relu2", name=f"{tag}_up")
    return _mm(act, w_down, mode="nn", add=h, name=f"{tag}_down"), (h, xn, u, act)


def _mlp_bwd(dh_out, saved, norm, w_up, w_down, tag):
    h, xn, u, act = saved
    d = h.shape[1]
    du = _mm(dh_out, w_down, mode="nt", epilogue="relu2_bwd", aux=u, out_dtype=BF16, name=f"{tag}_bwd_du")
    dw_down = _mm(act, dh_out, mode="tn", name=f"{tag}_bwd_wdown")
    dxn = _mm(du, w_up, mode="nt", name=f"{tag}_bwd_dxn")
    dw_up = _mm(xn, du, mode="tn", name=f"{tag}_bwd_wup")

    def norm_bwd(x, dy, dres, w):
        dx, dw = _rms_bwd(x, w, dy)
        return dx + dres, dw

    dh, dnorm = _rowcall(norm_bwd, [h, dxn, dh_out], [norm], [(d, F32)], [d], name=f"{tag}_bwd_norm")
    return dh, dnorm, dw_up, dw_down


def _row_major(g):
    return g.reshape(g.shape[0] * g.shape[1], g.shape[2])


def _col_major(g):
    return jnp.transpose(g, (1, 0, 2)).reshape(g.shape[1], g.shape[0] * g.shape[2])


def _col_terms(dw):
    k, n = dw.shape
    return jnp.transpose(dw.reshape(k, N_DEV, n // N_DEV), (1, 0, 2))


def _row_terms(dw):
    return dw.reshape(N_DEV, dw.shape[0] // N_DEV, dw.shape[1])


def kernel(x, hgrn_norm, hgrn_w_q, hgrn_w_f, hgrn_w_i, hgrn_w_g, hgrn_g_norm, hgrn_w_o, hgrn_lb_logits, mla_norm, mla_w_dq, mla_q_norm, mla_w_uq, mla_w_o, kv_in_norm, kv_w_dkv, kv_norm, kv_w_uk, kv_w_uv, mlp_norm, mlp_w_up, mlp_w_down, final_norm, loss_target, m_hgrn_norm, m_hgrn_w_q, m_hgrn_w_f, m_hgrn_w_i, m_hgrn_w_g, m_hgrn_g_norm, m_hgrn_w_o, m_hgrn_lb_logits, m_mla_norm, m_mla_w_dq, m_mla_q_norm, m_mla_w_uq, m_mla_w_o, m_kv_in_norm, m_kv_w_dkv, m_kv_norm, m_kv_w_uk, m_kv_w_uv, m_mlp_norm, m_mlp_w_up, m_mlp_w_down, m_final_norm, v_hgrn_norm, v_hgrn_w_q, v_hgrn_w_f, v_hgrn_w_i, v_hgrn_w_g, v_hgrn_g_norm, v_hgrn_w_o, v_hgrn_lb_logits, v_mla_norm, v_mla_w_dq, v_mla_q_norm, v_mla_w_uq, v_mla_w_o, v_kv_in_norm, v_kv_w_dkv, v_kv_norm, v_kv_w_uk, v_kv_w_uv, v_mlp_norm, v_mlp_w_up, v_mlp_w_down, v_final_norm):
    given = dict(locals())
    weight_names = ["hgrn_norm", "hgrn_w_q", "hgrn_w_f", "hgrn_w_i", "hgrn_w_g", "hgrn_g_norm", "hgrn_w_o",
                    "hgrn_lb_logits", "mla_norm", "mla_w_dq", "mla_q_norm", "mla_w_uq", "mla_w_o", "kv_in_norm",
                    "kv_w_dkv", "kv_norm", "kv_w_uk", "kv_w_uv", "mlp_norm", "mlp_w_up", "mlp_w_down", "final_norm"]
    me = 4 * lax.axis_index("x") + 2 * lax.axis_index("y") + lax.axis_index("c")
    xs, tgt = x[0], loss_target[0]
    seq, d_model = xs.shape
    n_heads, hd = MLA_HEADS, LANES

    big_local = {
        "hgrn_w_q": hgrn_w_q[0], "hgrn_w_f": hgrn_w_f[0], "hgrn_w_i": hgrn_w_i[0], "hgrn_w_g": hgrn_w_g[0],
        "hgrn_w_o": hgrn_w_o[0], "mla_w_dq": mla_w_dq[0], "mla_w_uq": mla_w_uq[0], "mla_w_o": mla_w_o[0],
        "kv_w_dkv": kv_w_dkv, "kv_w_uk": kv_w_uk, "kv_w_uv": kv_w_uv,
        "mlp_w_up0": mlp_w_up[0], "mlp_w_up1": mlp_w_up[1], "mlp_w_down0": mlp_w_down[0], "mlp_w_down1": mlp_w_down[1],
    }
    big_names = list(big_local)
    col_sharded = {"mla_w_uq", "kv_w_uk", "kv_w_uv", "mlp_w_up0", "mlp_w_up1"}
    vec_local = jnp.concatenate([hgrn_norm, hgrn_lb_logits], axis=0)
    gathered = _exchange([big_local[k].astype(BF16) for k in big_names] + [vec_local], scatter=False, name="gather_weights")
    w = {k: (_col_major(g) if k in col_sharded else _row_major(g)) for k, g in zip(big_names, gathered[:-1])}
    vec_full = jnp.transpose(gathered[-1], (1, 0, 2)).reshape(3, d_model)
    hgrn_norm_full, lb_logits_full = vec_full[0:1], vec_full[1:3]
    w_uq3 = w["mla_w_uq"].reshape(-1, n_heads, MLA_NOPE + MLA_ROPE)
    w_uq_nope = w_uq3[:, :, :MLA_NOPE].reshape(-1, n_heads * hd)
    w_uq_rope = jnp.pad(w_uq3[:, :, MLA_NOPE:], ((0, 0), (0, 0), (0, hd - MLA_ROPE))).reshape(-1, n_heads * hd)
    kv_lora = kv_w_uk.shape[0]
    w_dkv_pad = jnp.pad(w["kv_w_dkv"], ((0, 0), (0, kv_lora + hd - w["kv_w_dkv"].shape[1])))
    t_c, t_s1, t_s2 = _rope_tables(seq)

    xn0 = _rowcall(lambda a, g: (_rms(a, g),), [xs], [hgrn_norm_full], [(d_model, BF16)], [], name="hgrn_norm")[0]
    zq = _mm(xn0, w["hgrn_w_q"], mode="nn", name="hgrn_zq")
    zf = _mm(xn0, w["hgrn_w_f"], mode="nn", name="hgrn_zf")
    zi = _mm(xn0, w["hgrn_w_i"], mode="nn", name="hgrn_zi")
    zg = _mm(xn0, w["hgrn_w_g"], mode="nn", name="hgrn_zg")
    o_rec, states = _hgrn_fwd(zq, zf, zi, lb_logits_full, name="hgrn_fwd")
    mixed = _rowcall(_head_norm_gate, [o_rec, zg], [hgrn_g_norm], [(d_model, BF16)], [], name="hgrn_gate")[0]
    h1 = _mm(mixed, w["hgrn_w_o"], mode="nn", add=xs, name="hgrn_out")
    h2, mlp0_saved = _mlp_fwd(h1, mlp_norm[0:1], w["mlp_w_up0"], w["mlp_w_down0"], "mlp0")

    hn, xn2 = _rowcall(lambda a, g1, g2: (_rms(a, g1), _rms(a, g2)), [h2], [kv_in_norm[None, :], mla_norm],
                       [(d_model, BF16), (d_model, BF16)], [], name="kv_mla_norm")
    ckr = _mm(hn, w_dkv_pad, mode="nn", name="kv_down")

    def kv_latent(c_all, tc, ts1, ts2, g):
        return _rms(c_all[:, :kv_lora], g), _rope(c_all[:, kv_lora:], tc, ts1, ts2)

    c_kv, kr = _rowcall(kv_latent, [ckr, t_c, t_s1, t_s2], [kv_norm[None, :]], [(kv_lora, BF16), (hd, BF16)], [],
                        name="kv_latent")
    kn = _mm(c_kv, w["kv_w_uk"], mode="nn", out_dtype=BF16, name="kv_up_k")
    vv = _mm(c_kv, w["kv_w_uv"], mode="nn", out_dtype=BF16, name="kv_up_v")
    cq_pre = _mm(xn2, w["mla_w_dq"], mode="nn", name="q_down")
    c_q = _rowcall(lambda a, g: (_rms(a, g),), [cq_pre], [mla_q_norm], [(cq_pre.shape[1], BF16)], [], name="q_norm")[0]
    qn = _mm(c_q, w_uq_nope, mode="nn", out_dtype=BF16, name="q_up_nope")
    qr_pre = _mm(c_q, w_uq_rope, mode="nn", name="q_up_rope")
    qr = _rowcall(lambda a, tc, ts1, ts2: (_rope_slabs(a, tc, ts1, ts2, False),), [qr_pre, t_c, t_s1, t_s2], [],
                  [(n_heads * hd, BF16)], [], name="q_rope")[0]
    o_att, lse = _attn_fwd(qn, qr, kn, kr, vv, name="attn_fwd")
    h3 = _mm(o_att, w["mla_w_o"], mode="nn", add=h2, name="attn_out")
    h4, mlp1_saved = _mlp_fwd(h3, mlp_norm[1:2], w["mlp_w_up1"], w["mlp_w_down1"], "mlp1")
    dh4, g_final_norm, loss_part = _rowcall(_loss_head, [h4, tgt], [final_norm[None, :]], [(d_model, F32)],
                                            [d_model, LANES], name="loss_head")

    g = {}
    dh3, g_mlp_norm1, g["mlp_w_up1"], g["mlp_w_down1"] = _mlp_bwd(
        dh4, mlp1_saved, mlp_norm[1:2], w["mlp_w_up1"], w["mlp_w_down1"], "mlp1")
    d_oatt = _mm(dh3, w["mla_w_o"], mode="nt", out_dtype=BF16, name="attn_out_bwd_x")
    g["mla_w_o"] = _mm(o_att, dh3, mode="tn", name="attn_out_bwd_w")

    def head_delta(do, o):
        prod = do.astype(F32) * o.astype(F32)
        return (jnp.concatenate([jnp.broadcast_to(jnp.sum(prod[:, h * hd:(h + 1) * hd], axis=1, keepdims=True),
                                                  (prod.shape[0], hd)) for h in range(n_heads)], axis=1),)

    delta = _rowcall(head_delta, [d_oatt, o_att], [], [(n_heads * hd, F32)], [], name="attn_delta")[0]
    dqn, dqr = _attn_bwd_q(qn, qr, kn, kr, vv, d_oatt, lse, delta, name="attn_bwd_q")
    dkn, dvv, dkr = _attn_bwd_kv(qn, qr, kn, kr, vv, d_oatt, lse, delta, name="attn_bwd_kv")
    dqr_pre = _rowcall(lambda a, tc, ts1, ts2: (_rope_slabs(a, tc, ts1, ts2, True),), [dqr, t_c, t_s1, t_s2], [],
                       [(n_heads * hd, BF16)], [], name="q_rope_bwd")[0]
    dcq = _mm(dqn, w_uq_nope, mode="nt", name="q_up_nope_bwd_x")
    dcq = _mm(dqr_pre, w_uq_rope, mode="nt", add=dcq, name="q_up_rope_bwd_x")
    g_uq_nope = _mm(c_q, dqn, mode="tn", name="q_up_nope_bwd_w")
    g_uq_rope = _mm(c_q, dqr_pre, mode="tn", name="q_up_rope_bwd_w")
    q_lora = c_q.shape[1]
    g["mla_w_uq"] = jnp.concatenate([g_uq_nope.reshape(q_lora, n_heads, hd),
                                     g_uq_rope.reshape(q_lora, n_heads, hd)[:, :, :MLA_ROPE]], axis=2).reshape(q_lora, -1)
    dcq_pre, g_q_norm = _rowcall(lambda a, dy, gq: _rms_bwd(a, gq, dy), [cq_pre, dcq], [mla_q_norm],
                                 [(q_lora, BF16)], [q_lora], name="q_norm_bwd")
    dxn2 = _mm(dcq_pre, w["mla_w_dq"], mode="nt", name="q_down_bwd_x")
    g["mla_w_dq"] = _mm(xn2, dcq_pre, mode="tn", name="q_down_bwd_w")

    dc_kv = _mm(dkn, w["kv_w_uk"], mode="nt", name="kv_up_k_bwd_x")
    dc_kv = _mm(dvv, w["kv_w_uv"], mode="nt", add=dc_kv, name="kv_up_v_bwd_x")
    g["kv_w_uk"] = _mm(c_kv, dkn, mode="tn", name="kv_up_k_bwd_w")
    g["kv_w_uv"] = _mm(c_kv, dvv, mode="tn", name="kv_up_v_bwd_w")

    def kv_latent_bwd(c_all, dc, dkr_slab, tc, ts1, ts2, gk):
        dlat, dgk = _rms_bwd(c_all[:, :kv_lora], gk, dc)
        return jnp.concatenate([dlat, _rope_t(dkr_slab, tc, ts1, ts2)], axis=1), dgk

    dckr, g_kv_norm = _rowcall(kv_latent_bwd, [ckr, dc_kv, dkr, t_c, t_s1, t_s2], [kv_norm[None, :]],
                               [(kv_lora + hd, BF16)], [kv_lora], name="kv_latent_bwd")
    dhn = _mm(dckr, w_dkv_pad, mode="nt", name="kv_down_bwd_x")
    g["kv_w_dkv"] = _mm(hn, dckr, mode="tn", name="kv_down_bwd_w")[:, :kv_w_dkv.shape[1]]

    def kv_mla_norm_bwd(a, d1, d2, dres, g1, g2):
        dx1, dw1 = _rms_bwd(a, g1, d1)
        dx2, dw2 = _rms_bwd(a, g2, d2)
        return dx1 + dx2 + dres, dw1, dw2

    dh2, g_kv_in_norm, g_mla_norm = _rowcall(kv_mla_norm_bwd, [h2, dhn, dxn2, dh3], [kv_in_norm[None, :], mla_norm],
                                             [(d_model, F32)], [d_model, d_model], name="kv_mla_norm_bwd")
    dh1, g_mlp_norm0, g["mlp_w_up0"], g["mlp_w_down0"] = _mlp_bwd(
        dh2, mlp0_saved, mlp_norm[0:1], w["mlp_w_up0"], w["mlp_w_down0"], "mlp0")

    dmixed = _mm(dh1, w["hgrn_w_o"], mode="nt", name="hgrn_out_bwd_x")
    g["hgrn_w_o"] = _mm(mixed, dh1, mode="tn", name="hgrn_out_bwd_w")
    do_rec, dzg, g_g_norm = _rowcall(_head_norm_gate_bwd, [o_rec, zg, dmixed], [hgrn_g_norm],
                                     [(d_model, F32), (d_model, BF16)], [hd], name="hgrn_gate_bwd")
    dzq, dzf, dzi, g_lb = _hgrn_bwd(zq, zf, zi, lb_logits_full, states, do_rec, name="hgrn_bwd")
    dxn0 = None
    for nm, dz in (("hgrn_w_q", dzq), ("hgrn_w_f", dzf), ("hgrn_w_i", dzi), ("hgrn_w_g", dzg)):
        dxn0 = _mm(dz, w[nm], mode="nt", add=dxn0, name=f"{nm}_bwd_x")
        g[nm] = _mm(xn0, dz, mode="tn", name=f"{nm}_bwd_w")

    def in_norm_bwd(a, dy, dres, gw):
        dx, dw = _rms_bwd(a, gw, dy)
        return dx + dres, dw

    grad_x, g_hgrn_norm = _rowcall(in_norm_bwd, [xs, dxn0, dh1], [hgrn_norm_full], [(d_model, F32)], [d_model],
                                   name="hgrn_norm_bwd")

    terms = _exchange([(_col_terms if k in col_sharded else _row_terms)(g[k]) for k in big_names], scatter=True,
                      name="scatter_grads")
    small_parts = [g_hgrn_norm, g_lb, g_g_norm, g_mla_norm, g_q_norm, g_kv_in_norm, g_kv_norm, g_mlp_norm0,
                   g_mlp_norm1, g_final_norm, loss_part]
    small_sizes = [p.shape[1] for p in small_parts]
    small_terms = _exchange([jnp.concatenate(small_parts, axis=1)], scatter=False, name="gather_small")[0]
    small_sum = _sum_terms(small_terms, name="sum_small")
    offs = [0]
    for sz in small_sizes:
        offs.append(offs[-1] + sz)
    (s_hgrn_norm, s_lb, s_g_norm, s_mla_norm, s_q_norm, s_kv_in_norm, s_kv_norm, s_mlp_norm0, s_mlp_norm1, s_final_norm,
     s_loss) = [small_sum[:, a:b] for a, b in zip(offs[:-1], offs[1:])]
    shard = hgrn_norm.shape[1]
    g_lb_logits = _lb_logits_grad(lax.dynamic_slice_in_dim(s_lb, me * shard, shard, axis=1), hgrn_lb_logits,
                                  name="lb_logits_grad")
    loss = s_loss[0, 0]

    res = {}
    for k, t in zip(big_names, terms):
        if k.startswith("mlp_w_"):
            base, layer = k[:-1], int(k[-1])
            wk, mk, vk = given[base][layer], given["m_" + base][layer], given["v_" + base][layer]
        else:
            wk, mk, vk = given[k], given["m_" + k], given["v_" + k]
        shape = wk.shape
        wk, mk, vk = (a.reshape(shape[-2], shape[-1]) for a in (wk, mk, vk))
        res[k] = [o.reshape(shape) for o in _adam(wk, t, mk, vk, name=f"adam_{k}")]
    for base in ("mlp_w_up", "mlp_w_down"):
        res[base] = [jnp.stack([res[base + "0"][i], res[base + "1"][i]], axis=0) for i in range(4)]

    small_grads = {
        "hgrn_norm": lax.dynamic_slice_in_dim(s_hgrn_norm, me * shard, shard, axis=1),
        "hgrn_g_norm": s_g_norm, "hgrn_lb_logits": g_lb_logits, "mla_norm": s_mla_norm, "mla_q_norm": s_q_norm,
        "kv_in_norm": s_kv_in_norm, "kv_norm": s_kv_norm,
        "mlp_norm": jnp.concatenate([s_mlp_norm0, s_mlp_norm1], axis=0), "final_norm": s_final_norm,
    }
    small_names = list(small_grads)

    def flat(a):
        return a.reshape(1, -1)

    packed = [jnp.concatenate([flat(src[pre + k]) for k in small_names], axis=1)
              for src, pre in ((given, ""), (small_grads, ""), (given, "m_"), (given, "v_"))]
    small_out = _adam(packed[0], packed[1][None], packed[2], packed[3], name="adam_small")
    off = 0
    for k in small_names:
        size = given[k].size
        res[k] = [o[:, off:off + size].reshape(given[k].shape) for o in small_out]
        off += size

    outs = [loss, grad_x[None]]
    for i in range(4):
        outs += [res[k][i] for k in weight_names]
    return tuple(outs)
```

```python
import functools

import jax
import jax.numpy as jnp
from jax import lax
from jax.experimental import pallas as pl
from jax.experimental.pallas import tpu as pltpu

F32 = jnp.float32
BF16 = jnp.bfloat16

EPS = 1e-6
LANES = 128
N_DEV = 8
V7X_VMEM_LIMIT_BYTES = 56 << 20

HGRN_HEADS = 8
HGRN_CHUNK = 64
HGRN_SUB = 16
EXP_CLAMP = 80.0
MLA_HEADS = 16
MLA_NOPE = 128
MLA_ROPE = 64
ROPE_THETA = 10000.0
ATTN_SCALE = (MLA_NOPE + MLA_ROPE) ** -0.5

ADAM_LR = 0.001
ADAM_B1 = 0.9
ADAM_B2 = 0.999
ADAM_EPS = 1e-08
ADAM_WD = 0.01
ADAM_STEP = 10

_NN = ((1,), (0,))
_NT = ((1,), (1,))
_TN = ((0,), (0,))


def _params(*sem):
    return pltpu.CompilerParams(dimension_semantics=sem, vmem_limit_bytes=V7X_VMEM_LIMIT_BYTES)


def _dot(a, b, dims):
    return lax.dot_general(a.astype(BF16), b.astype(BF16), (dims, ((), ())), preferred_element_type=F32)


def _dot_f32(a, b, dims=_NN):
    return lax.dot_general(a, b, (dims, ((), ())), precision=lax.Precision.HIGHEST, preferred_element_type=F32)


def _sigmoid(x):
    return 1.0 / (1.0 + jnp.exp(-x))


def _rms(x, w):
    r = lax.rsqrt(jnp.mean(x * x, axis=-1, keepdims=True) + EPS)
    return x * r * w


def _rms_bwd(x, w, dy):
    r = lax.rsqrt(jnp.mean(x * x, axis=-1, keepdims=True) + EPS)
    xh = x * r
    dw = jnp.sum(dy * xh, axis=0, keepdims=True)
    dxh = dy * w
    dx = r * (dxh - xh * jnp.mean(dxh * xh, axis=-1, keepdims=True))
    return dx, dw


def _mm(a, b, *, mode, name, out_dtype=F32, add=None, epilogue=None, aux=None, after=None):
    if mode == "nn":
        (m, k), (k2, n) = a.shape, b.shape
    elif mode == "nt":
        (m, k), (n, k2) = a.shape, b.shape
    else:
        (k, m), (k2, n) = a.shape, b.shape
    assert k == k2, (name, a.shape, b.shape)
    tm, tn = min(m, 512), min(n, 512)
    assert m % tm == 0 and n % tn == 0, (name, m, n)
    dims = {"nn": _NN, "nt": _NT, "tn": _TN}[mode]
    a_spec = pl.BlockSpec((k, tm), lambda i, j: (0, i)) if mode == "tn" else pl.BlockSpec((tm, k), lambda i, j: (i, 0))
    b_spec = pl.BlockSpec((tn, k), lambda i, j: (j, 0)) if mode == "nt" else pl.BlockSpec((k, tn), lambda i, j: (0, j))
    o_spec = pl.BlockSpec((tm, tn), lambda i, j: (i, j))
    operands, in_specs = [a, b], [a_spec, b_spec]
    for extra in (add, aux):
        if extra is not None:
            assert extra.shape == (m, n), (name, extra.shape)
            operands.append(extra)
            in_specs.append(o_spec)
    n_in = len(operands)
    if after is not None:
        operands.append(after)
        in_specs.append(pl.BlockSpec(memory_space=pl.ANY))
    if epilogue == "relu2":
        out_shape = [jax.ShapeDtypeStruct((m, n), F32), jax.ShapeDtypeStruct((m, n), BF16)]
        out_specs = [o_spec, o_spec]
    else:
        out_shape = jax.ShapeDtypeStruct((m, n), out_dtype)
        out_specs = o_spec

    def body(*refs):
        acc = _dot(refs[0][...], refs[1][...], dims)
        extras, outs = refs[2:n_in], refs[len(operands):]
        if add is not None:
            acc = acc + extras[0][...]
        if epilogue == "relu2":
            outs[0][...] = acc
            outs[1][...] = jnp.square(jnp.maximum(acc, 0.0)).astype(BF16)
        elif epilogue == "relu2_bwd":
            outs[0][...] = (acc * (2.0 * jnp.maximum(extras[-1][...], 0.0))).astype(out_dtype)
        else:
            outs[0][...] = acc.astype(out_dtype)

    return pl.pallas_call(
        body, name=name, grid=(m // tm, n // tn), in_specs=in_specs, out_specs=out_specs, out_shape=out_shape,
        compiler_params=_params("parallel", "parallel"),
    )(*operands)


def _rowcall(fn, rows, consts, outs, accs, *, name, tr=256):
    s = rows[0].shape[0]
    tr = min(tr, s)
    assert s % tr == 0
    n_out = len(outs)
    in_specs = [pl.BlockSpec((tr, r.shape[1]), lambda i: (i, 0)) for r in rows]
    in_specs += [pl.BlockSpec(c.shape, lambda i: (0, 0)) for c in consts]
    out_shape = [jax.ShapeDtypeStruct((s, w), dt) for w, dt in outs] + [jax.ShapeDtypeStruct((1, w), F32) for w in accs]
    out_specs = [pl.BlockSpec((tr, w), lambda i: (i, 0)) for w, _ in outs] + [pl.BlockSpec((1, w), lambda i: (0, 0)) for w in accs]
    n_in = len(rows) + len(consts)

    def body(*refs):
        res = fn(*[r[...] for r in refs[:n_in]])
        out_refs = refs[n_in:]
        for ref, val in zip(out_refs[:n_out], res[:n_out]):
            ref[...] = val.astype(ref.dtype)
        i = pl.program_id(0)
        for ref, val in zip(out_refs[n_out:], res[n_out:]):
            @pl.when(i == 0)
            def _(ref=ref, val=val):
                ref[...] = val

            @pl.when(i > 0)
            def _(ref=ref, val=val):
                ref[...] += val

    return pl.pallas_call(
        body, name=name, grid=(s // tr,), in_specs=in_specs, out_specs=out_specs, out_shape=out_shape,
        compiler_params=_params("arbitrary" if accs else "parallel"),
    )(*rows, *consts)


def _rope_tables(seq):
    half = MLA_ROPE // 2
    inv_freq = ROPE_THETA ** (-jnp.arange(half, dtype=F32) / half)
    ang = jnp.arange(seq, dtype=F32)[:, None] * inv_freq[None, :]
    cos, sin, zero = jnp.cos(ang), jnp.sin(ang), jnp.zeros((seq, half), F32)
    t_c = jnp.concatenate([cos, cos, zero, zero], axis=1)
    t_s1 = jnp.concatenate([-sin, zero, zero, zero], axis=1)
    t_s2 = jnp.concatenate([zero, sin, zero, zero], axis=1)
    return t_c, t_s1, t_s2


def _rope(slab, t_c, t_s1, t_s2):
    return slab * t_c + pltpu.roll(slab, 96, 1) * t_s1 + pltpu.roll(slab, 32, 1) * t_s2


def _rope_t(d, t_c, t_s1, t_s2):
    return d * t_c + pltpu.roll(d * t_s1, 32, 1) + pltpu.roll(d * t_s2, 96, 1)


def _lower_bound(logits):
    l0, l1 = logits[0:1, :], logits[1:2, :]
    mx = jnp.maximum(l0, l1)
    e0, e1 = jnp.exp(l0 - mx), jnp.exp(l1 - mx)
    return e0 / (e0 + e1)


def _tri(n, lower):
    row = lax.broadcasted_iota(jnp.int32, (n, n), 0)
    col = lax.broadcasted_iota(jnp.int32, (n, n), 1)
    return (row >= col) if lower else (row <= col)


def _hgrn_intra(q, k, b, b_sc):
    c = HGRN_CHUNK
    b_sc[...] = b
    qts, decs, scores = [], [], []
    for i in range(c // HGRN_SUB):
        lo = i * HGRN_SUB
        ref = b_sc[lo - 1:lo, :] if i > 0 else jnp.zeros((1, LANES), F32)
        qt = q[lo:lo + HGRN_SUB, :] * jnp.exp(b[lo:lo + HGRN_SUB, :] - ref)
        dec = jnp.exp(jnp.minimum(ref - b, EXP_CLAMP))
        qts.append(qt)
        decs.append(dec)
        scores.append(_dot(qt, k * dec, _NT))
    a = jnp.where(_tri(c, True), jnp.concatenate(scores, axis=0), 0.0)
    return a, qts, decs


def _hgrn_fwd(zq, zf, zi, lb_logits, *, name):
    s, d = zq.shape
    h_n, c = d // LANES, HGRN_CHUNK
    nc = s // c

    def body(zq_ref, zf_ref, zi_ref, lb_ref, o_ref, st_ref, state_sc, b_sc):
        @pl.when(pl.program_id(1) == 0)
        def _():
            state_sc[...] = jnp.zeros_like(state_sc)

        lb = _lower_bound(lb_ref[...])
        zq_v = zq_ref[...]
        q = zq_v * _sigmoid(zq_v)
        f = lb + (1.0 - lb) * _sigmoid(zf_ref[...])
        g = jnp.log(f)
        k = 1.0 - f
        v = zi_ref[...]
        b = _dot_f32(_tri(c, True).astype(F32), g)
        s0t = state_sc[...]
        st_ref[...] = s0t
        a, _, _ = _hgrn_intra(q, k, b, b_sc)
        o_ref[...] = _dot(q * jnp.exp(b), s0t, _NT) + _dot(a, v, _NN)
        bl = b_sc[c - 1:c, :]
        state_sc[...] = s0t * jnp.exp(bl) + _dot(v, k * jnp.exp(bl - b), _TN)

    tile = pl.BlockSpec((c, LANES), lambda h, i: (i, h))
    return pl.pallas_call(
        body, name=name, grid=(h_n, nc),
        in_specs=[tile, tile, tile, pl.BlockSpec((2, LANES), lambda h, i: (0, h))],
        out_specs=[tile, pl.BlockSpec((None, None, LANES, LANES), lambda h, i: (h, i, 0, 0))],
        out_shape=[jax.ShapeDtypeStruct((s, d), F32), jax.ShapeDtypeStruct((h_n, nc, LANES, LANES), F32)],
        scratch_shapes=[pltpu.VMEM((LANES, LANES), F32), pltpu.VMEM((c, LANES), F32)],
        compiler_params=_params("parallel", "arbitrary"),
    )(zq, zf, zi, lb_logits)


def _hgrn_bwd(zq, zf, zi, lb_logits, states, do, *, name):
    s, d = zq.shape
    h_n, c = d // LANES, HGRN_CHUNK
    nc = s // c

    def body(zq_ref, zf_ref, zi_ref, lb_ref, st_ref, do_ref, dzq_ref, dzf_ref, dzi_ref, dlb_ref, dstate_sc, b_sc):
        @pl.when(pl.program_id(1) == 0)
        def _():
            dstate_sc[...] = jnp.zeros_like(dstate_sc)
            dlb_ref[...] = jnp.zeros_like(dlb_ref)

        lb = _lower_bound(lb_ref[...])
        zq_v = zq_ref[...]
        sq = _sigmoid(zq_v)
        q = zq_v * sq
        sf = _sigmoid(zf_ref[...])
        f = lb + (1.0 - lb) * sf
        g = jnp.log(f)
        k = 1.0 - f
        v = zi_ref[...]
        d_o = do_ref[...]
        b = _dot_f32(_tri(c, True).astype(F32), g)
        s0t = st_ref[...]
        ds1t = dstate_sc[...]
        a, qts, decs = _hgrn_intra(q, k, b, b_sc)
        bl = b_sc[c - 1:c, :]
        eb, ebl, dec_end = jnp.exp(b), jnp.exp(bl), jnp.exp(bl - b)
        qe = q * eb
        da = jnp.where(_tri(c, True), _dot(d_o, v, _NT), 0.0)
        dv = _dot(a, d_o, _TN) + _dot(k * dec_end, ds1t, _NT)
        dk_state = _dot(v, ds1t, _NN) * dec_end
        dk = dk_state
        dq_blocks = []
        for i in range(c // HGRN_SUB):
            lo = i * HGRN_SUB
            ref = b_sc[lo - 1:lo, :] if i > 0 else jnp.zeros((1, LANES), F32)
            da_i = da[lo:lo + HGRN_SUB, :]
            dq_blocks.append(_dot_f32(da_i, k * decs[i], _NN) * jnp.exp(b[lo:lo + HGRN_SUB, :] - ref))
            dk = dk + _dot_f32(da_i, qts[i], _TN) * decs[i]
        dq = _dot(d_o, s0t, _NN) * eb + jnp.concatenate(dq_blocks, axis=0)
        db_last = jnp.sum(k * dk_state, axis=0, keepdims=True) + ebl * jnp.sum(s0t * ds1t, axis=0, keepdims=True)
        last_row = lax.broadcasted_iota(jnp.int32, (c, LANES), 0) == c - 1
        db = q * dq - k * dk + jnp.where(last_row, db_last, 0.0)
        dg = _dot_f32(_tri(c, False).astype(F32), db)
        df = dg / f - dk
        dzf_ref[...] = (df * (1.0 - lb) * sf * (1.0 - sf)).astype(BF16)
        dlb_ref[...] += jnp.sum(df * (1.0 - sf), axis=0, keepdims=True)
        dzq_ref[...] = (dq * sq * (1.0 + zq_v * (1.0 - sq))).astype(BF16)
        dzi_ref[...] = dv.astype(BF16)
        dstate_sc[...] = ds1t * ebl + _dot(d_o, qe, _TN)

    tile = pl.BlockSpec((c, LANES), lambda h, i: (nc - 1 - i, h))
    out = jax.ShapeDtypeStruct((s, d), BF16)
    return pl.pallas_call(
        body, name=name, grid=(h_n, nc),
        in_specs=[tile, tile, tile, pl.BlockSpec((2, LANES), lambda h, i: (0, h)),
                  pl.BlockSpec((None, None, LANES, LANES), lambda h, i: (h, nc - 1 - i, 0, 0)), tile],
        out_specs=[tile, tile, tile, pl.BlockSpec((1, LANES), lambda h, i: (0, h))],
        out_shape=[out, out, out, jax.ShapeDtypeStruct((1, d), F32)],
        scratch_shapes=[pltpu.VMEM((LANES, LANES), F32), pltpu.VMEM((c, LANES), F32)],
        compiler_params=_params("parallel", "arbitrary"),
    )(zq, zf, zi, lb_logits, states, do)


def _attn_tile(s):
    return min(512, max(128, s // 2))


def _causal_scores(qn_ref, qr_ref, kn_ref, kr_ref, q_blk, k_blk, t):
    q = jnp.concatenate([qn_ref[...], qr_ref[...]], axis=1)
    k = jnp.concatenate([kn_ref[...], kr_ref[...]], axis=1)
    sc = _dot(q, k, _NT) * ATTN_SCALE
    q_pos = q_blk * t + lax.broadcasted_iota(jnp.int32, (t, t), 0)
    k_pos = k_blk * t + lax.broadcasted_iota(jnp.int32, (t, t), 1)
    return q, k, jnp.where(k_pos <= q_pos, sc, -jnp.inf)


def _attn_fwd(qn, qr, kn, kr, v, *, name):
    s, t = qn.shape[0], _attn_tile(qn.shape[0])
    n = s // t

    def body(qn_ref, qr_ref, kn_ref, kr_ref, v_ref, o_ref, lse_ref, m_sc, l_sc, acc_sc):
        i, j = pl.program_id(1), pl.program_id(2)

        @pl.when(j == 0)
        def _():
            m_sc[...] = jnp.full_like(m_sc, -jnp.inf)
            l_sc[...] = jnp.zeros_like(l_sc)
            acc_sc[...] = jnp.zeros_like(acc_sc)

        @pl.when(j <= i)
        def _():
            _, _, sc = _causal_scores(qn_ref, qr_ref, kn_ref, kr_ref, i, j, t)
            m_prev = m_sc[...]
            m_new = jnp.maximum(m_prev, jnp.max(sc, axis=1, keepdims=True))
            alpha = jnp.exp(m_prev - m_new)
            p = jnp.exp(sc - m_new[:, :1])
            l_sc[...] = alpha * l_sc[...] + jnp.sum(p, axis=1, keepdims=True)
            acc_sc[...] = alpha * acc_sc[...] + _dot(p, v_ref[...], _NN)
            m_sc[...] = m_new

        @pl.when(j == i)
        def _():
            o_ref[...] = (acc_sc[...] / l_sc[...]).astype(BF16)
            lse_ref[...] = m_sc[...] + jnp.log(l_sc[...])

    q_spec = pl.BlockSpec((t, LANES), lambda h, i, j: (i, h))
    k_spec = pl.BlockSpec((t, LANES), lambda h, i, j: (jnp.minimum(i, j), h))
    kr_spec = pl.BlockSpec((t, LANES), lambda h, i, j: (jnp.minimum(i, j), 0))
    stat = pltpu.VMEM((t, LANES), F32)
    return pl.pallas_call(
        body, name=name, grid=(MLA_HEADS, n, n),
        in_specs=[q_spec, q_spec, k_spec, kr_spec, k_spec], out_specs=[q_spec, q_spec],
        out_shape=[jax.ShapeDtypeStruct(qn.shape, BF16), jax.ShapeDtypeStruct(qn.shape, F32)],
        scratch_shapes=[stat, stat, stat],
        compiler_params=_params("parallel", "parallel", "arbitrary"),
    )(qn, qr, kn, kr, v)


def _attn_probs(qn_ref, qr_ref, kn_ref, kr_ref, v_ref, do_ref, lse_ref, delta_ref, q_blk, k_blk, t):
    q, k, sc = _causal_scores(qn_ref, qr_ref, kn_ref, kr_ref, q_blk, k_blk, t)
    p = jnp.exp(sc - lse_ref[...][:, :1])
    dp = _dot(do_ref[...], v_ref[...], _NT)
    ds = p * (dp - delta_ref[...][:, :1]) * ATTN_SCALE
    return q, k, p, ds


def _attn_bwd_q(qn, qr, kn, kr, v, do, lse, delta, *, name):
    s, t = qn.shape[0], _attn_tile(qn.shape[0])
    n = s // t

    def body(qn_ref, qr_ref, kn_ref, kr_ref, v_ref, do_ref, lse_ref, delta_ref, dqn_ref, dqr_ref, acc_sc):
        i, j = pl.program_id(1), pl.program_id(2)

        @pl.when(j == 0)
        def _():
            acc_sc[...] = jnp.zeros_like(acc_sc)

        @pl.when(j <= i)
        def _():
            _, k, _, ds = _attn_probs(qn_ref, qr_ref, kn_ref, kr_ref, v_ref, do_ref, lse_ref, delta_ref, i, j, t)
            acc_sc[...] += _dot(ds, k, _NN)

        @pl.when(j == i)
        def _():
            dqn_ref[...] = acc_sc[:, :LANES]
            dqr_ref[...] = acc_sc[:, LANES:]

    q_spec = pl.BlockSpec((t, LANES), lambda h, i, j: (i, h))
    k_spec = pl.BlockSpec((t, LANES), lambda h, i, j: (jnp.minimum(i, j), h))
    kr_spec = pl.BlockSpec((t, LANES), lambda h, i, j: (jnp.minimum(i, j), 0))
    out = jax.ShapeDtypeStruct(qn.shape, F32)
    return pl.pallas_call(
        body, name=name, grid=(MLA_HEADS, n, n),
        in_specs=[q_spec, q_spec, k_spec, kr_spec, k_spec, q_spec, q_spec, q_spec], out_specs=[q_spec, q_spec],
        out_shape=[out, out], scratch_shapes=[pltpu.VMEM((t, 2 * LANES), F32)],
        compiler_params=_params("parallel", "parallel", "arbitrary"),
    )(qn, qr, kn, kr, v, do, lse, delta)


def _attn_bwd_kv(qn, qr, kn, kr, v, do, lse, delta, *, name):
    s, t = qn.shape[0], _attn_tile(qn.shape[0])
    n = s // t

    def body(qn_ref, qr_ref, kn_ref, kr_ref, v_ref, do_ref, lse_ref, delta_ref, dkn_ref, dv_ref, dkr_ref, dk_sc, dv_sc):
        j, h, i = pl.program_id(0), pl.program_id(1), pl.program_id(2)

        @pl.when(i == 0)
        def _():
            dk_sc[...] = jnp.zeros_like(dk_sc)
            dv_sc[...] = jnp.zeros_like(dv_sc)

        @pl.when(i >= j)
        def _():
            q, _, p, ds = _attn_probs(qn_ref, qr_ref, kn_ref, kr_ref, v_ref, do_ref, lse_ref, delta_ref, i, j, t)
            dv_sc[...] += _dot(p, do_ref[...], _TN)
            dk_sc[...] += _dot(ds, q, _TN)

        @pl.when(i == n - 1)
        def _():
            dkn_ref[...] = dk_sc[:, :LANES].astype(BF16)
            dv_ref[...] = dv_sc[...].astype(BF16)

            @pl.when(h == 0)
            def _():
                dkr_ref[...] = dk_sc[:, LANES:]

            @pl.when(h > 0)
            def _():
                dkr_ref[...] += dk_sc[:, LANES:]

    q_spec = pl.BlockSpec((t, LANES), lambda j, h, i: (jnp.maximum(i, j), h))
    k_spec = pl.BlockSpec((t, LANES), lambda j, h, i: (j, h))
    kr_spec = pl.BlockSpec((t, LANES), lambda j, h, i: (j, 0))
    out = jax.ShapeDtypeStruct(qn.shape, BF16)
    return pl.pallas_call(
        body, name=name, grid=(n, MLA_HEADS, n),
        in_specs=[q_spec, q_spec, k_spec, kr_spec, k_spec, q_spec, q_spec, q_spec], out_specs=[k_spec, k_spec, kr_spec],
        out_shape=[out, out, jax.ShapeDtypeStruct((s, LANES), F32)],
        scratch_shapes=[pltpu.VMEM((t, 2 * LANES), F32), pltpu.VMEM((t, LANES), F32)],
        compiler_params=_params("parallel", "arbitrary", "arbitrary"),
    )(qn, qr, kn, kr, v, do, lse, delta)


def _exchange(arrs, *, scatter, name):
    n = len(arrs)
    out_shape = [jax.ShapeDtypeStruct(a.shape if scatter else (N_DEV, *a.shape), a.dtype) for a in arrs]

    def body(*refs):
        ins, outs = refs[:n], refs[n:2 * n]
        send_sems, recv_sems, local_sems = refs[2 * n:]
        x, y, c = lax.axis_index("x"), lax.axis_index("y"), lax.axis_index("c")
        me = 4 * x + 2 * y + c
        copies = []
        for k in range(n):
            local = pltpu.make_async_copy(ins[k].at[me] if scatter else ins[k], outs[k].at[me], local_sems.at[k])
            local.start()
            copies.append(local)
            for d in range(1, N_DEV):
                px, py, pc = (x + (d >> 2)) % 2, (y + ((d >> 1) & 1)) % 2, (c + (d & 1)) % 2
                peer = 4 * px + 2 * py + pc
                remote = pltpu.make_async_remote_copy(
                    src_ref=ins[k].at[peer] if scatter else ins[k], dst_ref=outs[k].at[me],
                    send_sem=send_sems.at[k, d - 1], recv_sem=recv_sems.at[k, d - 1],
                    device_id=(px, py, pc), device_id_type=pl.DeviceIdType.MESH)
                remote.start()
                copies.append(remote)
        for cp in copies:
            cp.wait()

    any_spec = pl.BlockSpec(memory_space=pl.ANY)
    return pl.pallas_call(
        body, name=name, in_specs=[any_spec] * n, out_specs=[any_spec] * n, out_shape=out_shape,
        scratch_shapes=[pltpu.SemaphoreType.DMA((n, N_DEV - 1)), pltpu.SemaphoreType.DMA((n, N_DEV - 1)),
                        pltpu.SemaphoreType.DMA((n,))],
    )(*arrs)


def _peers(x, y, c):
    out = []
    for d in range(1, N_DEV):
        px, py, pc = (x + (d >> 2)) % 2, (y + ((d >> 1) & 1)) % 2, (c + (d & 1)) % 2
        out.append(((px, py, pc), 4 * px + 2 * py + pc))
    return out


def _exchange_copies(ins, lands, send_sems, recv_sems, scatter):
    x, y, c = lax.axis_index("x"), lax.axis_index("y"), lax.axis_index("c")
    me = 4 * x + 2 * y + c
    local, remote = [], []
    for k in range(len(ins)):
        local.append(pltpu.make_async_copy(ins[k].at[me] if scatter else ins[k], lands[k].at[me],
                                           recv_sems.at[k * N_DEV + N_DEV - 1]))
        for d, (coords, peer) in enumerate(_peers(x, y, c)):
            remote.append(pltpu.make_async_remote_copy(
                src_ref=ins[k].at[peer] if scatter else ins[k], dst_ref=lands[k].at[me],
                send_sem=send_sems.at[k * N_DEV + d], recv_sem=recv_sems.at[k * N_DEV + d],
                device_id=coords, device_id_type=pl.DeviceIdType.MESH))
    return local, remote


def _exchange_start(arrs, *, scatter, name):
    n = len(arrs)
    hbm = pl.BlockSpec(memory_space=pltpu.HBM)
    sem = pl.BlockSpec(memory_space=pltpu.SEMAPHORE)
    lands = [lax.empty(a.shape if scatter else (N_DEV, *a.shape), a.dtype) for a in arrs]

    def body(*refs):
        ins, land_refs = refs[:n], refs[n:2 * n]
        send_sems, recv_sems, token = refs[2 * n], refs[2 * n + 1], refs[-1]
        local, remote = _exchange_copies(ins, land_refs, send_sems, recv_sems, scatter)
        for cp in local + remote:
            cp.start()
        token[...] = jnp.zeros_like(token)

    operands = [pltpu.with_memory_space_constraint(a, pltpu.HBM) for a in list(arrs) + lands]
    res = pl.pallas_call(
        body, name=name,
        out_shape=(pltpu.SemaphoreType.DMA((n * N_DEV,)), pltpu.SemaphoreType.DMA((n * N_DEV,)),
                   *[pltpu.HBM(o.shape, o.dtype) for o in operands], jax.ShapeDtypeStruct((8, LANES), F32)),
        in_specs=[hbm] * (2 * n), out_specs=(sem, sem, *[hbm] * (2 * n), pl.BlockSpec(memory_space=pltpu.VMEM)),
        input_output_aliases={i: 2 + i for i in range(2 * n)},
        compiler_params=pltpu.CompilerParams(has_side_effects=pltpu.SideEffectType.DATAFLOW_SIDE_EFFECTING),
    )(*operands)
    return (res[0], res[1], list(res[2:2 + n]), list(res[2 + n:2 + 2 * n]), scatter), res[-1]


def _exchange_wait(state, after, *, name):
    send_sems, recv_sems, ins, lands, scatter = state
    n = len(ins)
    hbm = pl.BlockSpec(memory_space=pltpu.HBM)
    sem = pl.BlockSpec(memory_space=pltpu.SEMAPHORE)

    def body(*refs):
        in_refs, land_refs = refs[:n], refs[n:2 * n]
        local, remote = _exchange_copies(in_refs, land_refs, refs[2 * n], refs[2 * n + 1], scatter)
        for cp in local:
            cp.wait()
        for cp in remote:
            cp.wait_send()
            cp.wait_recv()

    res = pl.pallas_call(
        body, name=name, out_shape=tuple(pltpu.HBM(o.shape, o.dtype) for o in ins + lands),
        in_specs=[hbm] * (2 * n) + [sem, sem, pl.BlockSpec(memory_space=pl.ANY)], out_specs=tuple([hbm] * (2 * n)),
        input_output_aliases={i: i for i in range(2 * n)},
        compiler_params=pltpu.CompilerParams(has_side_effects=pltpu.SideEffectType.DATAFLOW_SIDE_EFFECTING),
    )(*ins, *lands, send_sems, recv_sems, after)
    return list(res[n:])


def _adam(w, terms, m, v, *, name):
    r, c = w.shape
    n = terms.shape[0]
    tr = min(r, 128)
    assert r % tr == 0

    def body(w_ref, t_ref, m_ref, v_ref, g_out, d_out, m_out, v_out):
        g = t_ref[0]
        for s in range(1, n):
            g = g + t_ref[s]
        m1 = ADAM_B1 * m_ref[...] + (1.0 - ADAM_B1) * g
        v1 = ADAM_B2 * v_ref[...] + (1.0 - ADAM_B2) * jnp.square(g)
        m_hat = m1 / (1.0 - ADAM_B1 ** ADAM_STEP)
        v_hat = v1 / (1.0 - ADAM_B2 ** ADAM_STEP)
        g_out[...] = g
        d_out[...] = -ADAM_LR * (m_hat / (jnp.sqrt(v_hat) + ADAM_EPS) + ADAM_WD * w_ref[...])
        m_out[...] = m1
        v_out[...] = v1

    spec = pl.BlockSpec((tr, c), lambda i: (i, 0))
    out = jax.ShapeDtypeStruct((r, c), F32)
    return pl.pallas_call(
        body, name=name, grid=(r // tr,),
        in_specs=[spec, pl.BlockSpec((n, tr, c), lambda i: (0, i, 0)), spec, spec], out_specs=[spec] * 4,
        out_shape=[out] * 4, compiler_params=_params("parallel"),
    )(w, terms, m, v)


def _sum_terms(terms, *, name):
    n, _, p = terms.shape

    def body(t_ref, o_ref):
        acc = t_ref[0]
        for s in range(1, n):
            acc = acc + t_ref[s]
        o_ref[...] = acc

    return pl.pallas_call(body, name=name, out_shape=jax.ShapeDtypeStruct((1, p), F32))(terms)


def _lb_logits_grad(dlb, logits, *, name):
    def body(dlb_ref, l_ref, o_ref):
        lb = _lower_bound(l_ref[...])
        d0 = dlb_ref[...] * lb * (1.0 - lb)
        o_ref[...] = jnp.concatenate([d0, -d0], axis=0)

    return pl.pallas_call(body, name=name, out_shape=jax.ShapeDtypeStruct(logits.shape, F32))(dlb, logits)


def _silu_grad(z):
    sg = _sigmoid(z)
    return sg * (1.0 + z * (1.0 - sg))


def _head_norm_gate(o, zg, gn):
    outs = []
    for h in range(HGRN_HEADS):
        sl = slice(h * LANES, (h + 1) * LANES)
        zg_h = zg[:, sl]
        outs.append(_rms(o[:, sl], gn) * (zg_h * _sigmoid(zg_h)))
    return (jnp.concatenate(outs, axis=1),)


def _head_norm_gate_bwd(o, zg, dm, gn):
    do_parts, dzg_parts, dgn = [], [], jnp.zeros((1, LANES), F32)
    for h in range(HGRN_HEADS):
        sl = slice(h * LANES, (h + 1) * LANES)
        o_h, zg_h, dm_h = o[:, sl], zg[:, sl], dm[:, sl]
        gate = zg_h * _sigmoid(zg_h)
        do_h, dgn_h = _rms_bwd(o_h, gn, dm_h * gate)
        dgn = dgn + dgn_h
        do_parts.append(do_h)
        dzg_parts.append(dm_h * _rms(o_h, gn) * _silu_grad(zg_h))
    return jnp.concatenate(do_parts, axis=1), jnp.concatenate(dzg_parts, axis=1), dgn


def _rope_slabs(x, t_c, t_s1, t_s2, transpose):
    fn = _rope_t if transpose else _rope
    return jnp.concatenate(
        [fn(x[:, h * LANES:(h + 1) * LANES], t_c, t_s1, t_s2) for h in range(x.shape[1] // LANES)], axis=1)


def _loss_head(h, tgt, w):
    d = h.shape[1]
    r = lax.rsqrt(jnp.mean(h * h, axis=-1, keepdims=True) + EPS)
    xh = h * r
    err = xh * w - tgt
    loss = 0.5 * jnp.sum(jnp.mean(err * err, axis=-1, keepdims=True), axis=0, keepdims=True)
    dy = err / d
    dxh = dy * w
    dh = r * (dxh - xh * jnp.mean(dxh * xh, axis=-1, keepdims=True))
    return dh, jnp.sum(dy * xh, axis=0, keepdims=True), jnp.broadcast_to(loss, (1, LANES))


def _mlp_fwd(h, norm, w_up, w_down, tag):
    d = h.shape[1]
    xn = _rowcall(lambda x, w: (_rms(x, w),), [h], [norm], [(d, BF16)], [], name=f"{tag}_norm")[0]
    u, act = _mm(xn, w_up, mode="nn", epilogue="relu2", name=f"{tag}_up")
    return _mm(act, w_down, mode="nn", add=h, name=f"{tag}_down"), (h, xn, u, act)


def _mlp_bwd(dh_out, saved, norm, w_up, w_down, tag, after=None):
    h, xn, u, act = saved
    d = h.shape[1]
    du = _mm(dh_out, w_down, mode="nt", epilogue="relu2_bwd", aux=u, out_dtype=BF16, after=after,
             name=f"{tag}_bwd_du")
    dw_down = _mm(act, dh_out, mode="tn", name=f"{tag}_bwd_wdown")
    dxn = _mm(du, w_up, mode="nt", name=f"{tag}_bwd_dxn")
    dw_up = _mm(xn, du, mode="tn", name=f"{tag}_bwd_wup")

    def norm_bwd(x, dy, dres, w):
        dx, dw = _rms_bwd(x, w, dy)
        return dx + dres, dw

    dh, dnorm = _rowcall(norm_bwd, [h, dxn, dh_out], [norm], [(d, F32)], [d], name=f"{tag}_bwd_norm")
    return dh, dnorm, dw_up, dw_down


def _row_major(g):
    return g.reshape(g.shape[0] * g.shape[1], g.shape[2])


def _col_major(g):
    return jnp.transpose(g, (1, 0, 2)).reshape(g.shape[1], g.shape[0] * g.shape[2])


def _col_terms(dw):
    k, n = dw.shape
    return jnp.transpose(dw.reshape(k, N_DEV, n // N_DEV), (1, 0, 2))


def _row_terms(dw):
    return dw.reshape(N_DEV, dw.shape[0] // N_DEV, dw.shape[1])


def kernel(x, hgrn_norm, hgrn_w_q, hgrn_w_f, hgrn_w_i, hgrn_w_g, hgrn_g_norm, hgrn_w_o, hgrn_lb_logits, mla_norm, mla_w_dq, mla_q_norm, mla_w_uq, mla_w_o, kv_in_norm, kv_w_dkv, kv_norm, kv_w_uk, kv_w_uv, mlp_norm, mlp_w_up, mlp_w_down, final_norm, loss_target, m_hgrn_norm, m_hgrn_w_q, m_hgrn_w_f, m_hgrn_w_i, m_hgrn_w_g, m_hgrn_g_norm, m_hgrn_w_o, m_hgrn_lb_logits, m_mla_norm, m_mla_w_dq, m_mla_q_norm, m_mla_w_uq, m_mla_w_o, m_kv_in_norm, m_kv_w_dkv, m_kv_norm, m_kv_w_uk, m_kv_w_uv, m_mlp_norm, m_mlp_w_up, m_mlp_w_down, m_final_norm, v_hgrn_norm, v_hgrn_w_q, v_hgrn_w_f, v_hgrn_w_i, v_hgrn_w_g, v_hgrn_g_norm, v_hgrn_w_o, v_hgrn_lb_logits, v_mla_norm, v_mla_w_dq, v_mla_q_norm, v_mla_w_uq, v_mla_w_o, v_kv_in_norm, v_kv_w_dkv, v_kv_norm, v_kv_w_uk, v_kv_w_uv, v_mlp_norm, v_mlp_w_up, v_mlp_w_down, v_final_norm):
    given = dict(locals())
    weight_names = ["hgrn_norm", "hgrn_w_q", "hgrn_w_f", "hgrn_w_i", "hgrn_w_g", "hgrn_g_norm", "hgrn_w_o",
                    "hgrn_lb_logits", "mla_norm", "mla_w_dq", "mla_q_norm", "mla_w_uq", "mla_w_o", "kv_in_norm",
                    "kv_w_dkv", "kv_norm", "kv_w_uk", "kv_w_uv", "mlp_norm", "mlp_w_up", "mlp_w_down", "final_norm"]
    me = 4 * lax.axis_index("x") + 2 * lax.axis_index("y") + lax.axis_index("c")
    xs, tgt = x[0], loss_target[0]
    seq, d_model = xs.shape
    n_heads, hd = MLA_HEADS, LANES

    big_local = {
        "hgrn_w_q": hgrn_w_q[0], "hgrn_w_f": hgrn_w_f[0], "hgrn_w_i": hgrn_w_i[0], "hgrn_w_g": hgrn_w_g[0],
        "hgrn_w_o": hgrn_w_o[0], "mla_w_dq": mla_w_dq[0], "mla_w_uq": mla_w_uq[0], "mla_w_o": mla_w_o[0],
        "kv_w_dkv": kv_w_dkv, "kv_w_uk": kv_w_uk, "kv_w_uv": kv_w_uv,
        "mlp_w_up0": mlp_w_up[0], "mlp_w_up1": mlp_w_up[1], "mlp_w_down0": mlp_w_down[0], "mlp_w_down1": mlp_w_down[1],
    }
    big_names = list(big_local)
    col_sharded = {"mla_w_uq", "kv_w_uk", "kv_w_uv", "mlp_w_up0", "mlp_w_up1"}
    vec_local = jnp.concatenate([hgrn_norm, hgrn_lb_logits], axis=0)
    first_names = ["hgrn_w_q", "hgrn_w_f", "hgrn_w_i", "hgrn_w_g"]
    later_names = [k for k in big_names if k not in first_names]

    def unshard(names, arrays):
        return {k: (_col_major(a) if k in col_sharded else _row_major(a)) for k, a in zip(names, arrays)}

    gathered = _exchange([big_local[k].astype(BF16) for k in first_names] + [vec_local], scatter=False, name="gather_first")
    gather_state, token = _exchange_start([big_local[k].astype(BF16) for k in later_names], scatter=False,
                                          name="gather_rest_start")
    w = unshard(first_names, gathered[:-1])
    vec_full = jnp.transpose(gathered[-1], (1, 0, 2)).reshape(3, d_model)
    hgrn_norm_full, lb_logits_full = vec_full[0:1], vec_full[1:3]
    t_c, t_s1, t_s2 = _rope_tables(seq)
    kv_lora = kv_w_uk.shape[0]

    xn0 = _rowcall(lambda a, g: (_rms(a, g),), [xs], [hgrn_norm_full], [(d_model, BF16)], [], name="hgrn_norm")[0]
    zq = _mm(xn0, w["hgrn_w_q"], mode="nn", after=token, name="hgrn_zq")
    zf = _mm(xn0, w["hgrn_w_f"], mode="nn", name="hgrn_zf")
    zi = _mm(xn0, w["hgrn_w_i"], mode="nn", name="hgrn_zi")
    zg = _mm(xn0, w["hgrn_w_g"], mode="nn", name="hgrn_zg")
    o_rec, states = _hgrn_fwd(zq, zf, zi, lb_logits_full, name="hgrn_fwd")
    mixed = _rowcall(_head_norm_gate, [o_rec, zg], [hgrn_g_norm], [(d_model, BF16)], [], name="hgrn_gate")[0]
    w.update(unshard(later_names, _exchange_wait(gather_state, mixed, name="gather_rest_wait")))
    w_uq3 = w["mla_w_uq"].reshape(-1, n_heads, MLA_NOPE + MLA_ROPE)
    w_uq_nope = w_uq3[:, :, :MLA_NOPE].reshape(-1, n_heads * hd)
    w_uq_rope = jnp.pad(w_uq3[:, :, MLA_NOPE:], ((0, 0), (0, 0), (0, hd - MLA_ROPE))).reshape(-1, n_heads * hd)
    w_dkv_pad = jnp.pad(w["kv_w_dkv"], ((0, 0), (0, kv_lora + hd - w["kv_w_dkv"].shape[1])))
    h1 = _mm(mixed, w["hgrn_w_o"], mode="nn", add=xs, name="hgrn_out")
    h2, mlp0_saved = _mlp_fwd(h1, mlp_norm[0:1], w["mlp_w_up0"], w["mlp_w_down0"], "mlp0")

    hn, xn2 = _rowcall(lambda a, g1, g2: (_rms(a, g1), _rms(a, g2)), [h2], [kv_in_norm[None, :], mla_norm],
                       [(d_model, BF16), (d_model, BF16)], [], name="kv_mla_norm")
    ckr = _mm(hn, w_dkv_pad, mode="nn", name="kv_down")

    def kv_latent(c_all, tc, ts1, ts2, g):
        return _rms(c_all[:, :kv_lora], g), _rope(c_all[:, kv_lora:], tc, ts1, ts2)

    c_kv, kr = _rowcall(kv_latent, [ckr, t_c, t_s1, t_s2], [kv_norm[None, :]], [(kv_lora, BF16), (hd, BF16)], [],
                        name="kv_latent")
    kn = _mm(c_kv, w["kv_w_uk"], mode="nn", out_dtype=BF16, name="kv_up_k")
    vv = _mm(c_kv, w["kv_w_uv"], mode="nn", out_dtype=BF16, name="kv_up_v")
    cq_pre = _mm(xn2, w["mla_w_dq"], mode="nn", name="q_down")
    c_q = _rowcall(lambda a, g: (_rms(a, g),), [cq_pre], [mla_q_norm], [(cq_pre.shape[1], BF16)], [], name="q_norm")[0]
    qn = _mm(c_q, w_uq_nope, mode="nn", out_dtype=BF16, name="q_up_nope")
    qr_pre = _mm(c_q, w_uq_rope, mode="nn", name="q_up_rope")
    qr = _rowcall(lambda a, tc, ts1, ts2: (_rope_slabs(a, tc, ts1, ts2, False),), [qr_pre, t_c, t_s1, t_s2], [],
                  [(n_heads * hd, BF16)], [], name="q_rope")[0]
    o_att, lse = _attn_fwd(qn, qr, kn, kr, vv, name="attn_fwd")
    h3 = _mm(o_att, w["mla_w_o"], mode="nn", add=h2, name="attn_out")
    h4, mlp1_saved = _mlp_fwd(h3, mlp_norm[1:2], w["mlp_w_up1"], w["mlp_w_down1"], "mlp1")
    dh4, g_final_norm, loss_part = _rowcall(_loss_head, [h4, tgt], [final_norm[None, :]], [(d_model, F32)],
                                            [d_model, LANES], name="loss_head")

    g = {}
    groups = {"mlp1": ["mlp_w_up1", "mlp_w_down1"],
              "mla": ["mla_w_o", "mla_w_uq", "mla_w_dq", "kv_w_uk", "kv_w_uv", "kv_w_dkv"],
              "mlp0": ["mlp_w_up0", "mlp_w_down0"],
              "hgrn": ["hgrn_w_o", "hgrn_w_q", "hgrn_w_f", "hgrn_w_i", "hgrn_w_g"]}
    scatter_state = {}

    def scatter_start(tag):
        scatter_state[tag], tok = _exchange_start(
            [(_col_terms if k in col_sharded else _row_terms)(g[k]) for k in groups[tag]], scatter=True,
            name=f"scatter_{tag}_start")
        return tok

    dh3, g_mlp_norm1, g["mlp_w_up1"], g["mlp_w_down1"] = _mlp_bwd(
        dh4, mlp1_saved, mlp_norm[1:2], w["mlp_w_up1"], w["mlp_w_down1"], "mlp1")
    d_oatt = _mm(dh3, w["mla_w_o"], mode="nt", out_dtype=BF16, after=scatter_start("mlp1"), name="attn_out_bwd_x")
    g["mla_w_o"] = _mm(o_att, dh3, mode="tn", name="attn_out_bwd_w")

    def head_delta(do, o):
        prod = do.astype(F32) * o.astype(F32)
        return (jnp.concatenate([jnp.broadcast_to(jnp.sum(prod[:, h * hd:(h + 1) * hd], axis=1, keepdims=True),
                                                  (prod.shape[0], hd)) for h in range(n_heads)], axis=1),)

    delta = _rowcall(head_delta, [d_oatt, o_att], [], [(n_heads * hd, F32)], [], name="attn_delta")[0]
    dqn, dqr = _attn_bwd_q(qn, qr, kn, kr, vv, d_oatt, lse, delta, name="attn_bwd_q")
    dkn, dvv, dkr = _attn_bwd_kv(qn, qr, kn, kr, vv, d_oatt, lse, delta, name="attn_bwd_kv")
    dqr_pre = _rowcall(lambda a, tc, ts1, ts2: (_rope_slabs(a, tc, ts1, ts2, True),), [dqr, t_c, t_s1, t_s2], [],
                       [(n_heads * hd, BF16)], [], name="q_rope_bwd")[0]
    dcq = _mm(dqn, w_uq_nope, mode="nt", name="q_up_nope_bwd_x")
    dcq = _mm(dqr_pre, w_uq_rope, mode="nt", add=dcq, name="q_up_rope_bwd_x")
    g_uq_nope = _mm(c_q, dqn, mode="tn", name="q_up_nope_bwd_w")
    g_uq_rope = _mm(c_q, dqr_pre, mode="tn", name="q_up_rope_bwd_w")
    q_lora = c_q.shape[1]
    g["mla_w_uq"] = jnp.concatenate([g_uq_nope.reshape(q_lora, n_heads, hd),
                                     g_uq_rope.reshape(q_lora, n_heads, hd)[:, :, :MLA_ROPE]], axis=2).reshape(q_lora, -1)
    dcq_pre, g_q_norm = _rowcall(lambda a, dy, gq: _rms_bwd(a, gq, dy), [cq_pre, dcq], [mla_q_norm],
                                 [(q_lora, BF16)], [q_lora], name="q_norm_bwd")
    dxn2 = _mm(dcq_pre, w["mla_w_dq"], mode="nt", name="q_down_bwd_x")
    g["mla_w_dq"] = _mm(xn2, dcq_pre, mode="tn", name="q_down_bwd_w")

    dc_kv = _mm(dkn, w["kv_w_uk"], mode="nt", name="kv_up_k_bwd_x")
    dc_kv = _mm(dvv, w["kv_w_uv"], mode="nt", add=dc_kv, name="kv_up_v_bwd_x")
    g["kv_w_uk"] = _mm(c_kv, dkn, mode="tn", name="kv_up_k_bwd_w")
    g["kv_w_uv"] = _mm(c_kv, dvv, mode="tn", name="kv_up_v_bwd_w")

    def kv_latent_bwd(c_all, dc, dkr_slab, tc, ts1, ts2, gk):
        dlat, dgk = _rms_bwd(c_all[:, :kv_lora], gk, dc)
        return jnp.concatenate([dlat, _rope_t(dkr_slab, tc, ts1, ts2)], axis=1), dgk

    dckr, g_kv_norm = _rowcall(kv_latent_bwd, [ckr, dc_kv, dkr, t_c, t_s1, t_s2], [kv_norm[None, :]],
                               [(kv_lora + hd, BF16)], [kv_lora], name="kv_latent_bwd")
    dhn = _mm(dckr, w_dkv_pad, mode="nt", name="kv_down_bwd_x")
    g["kv_w_dkv"] = _mm(hn, dckr, mode="tn", name="kv_down_bwd_w")[:, :kv_w_dkv.shape[1]]

    def kv_mla_norm_bwd(a, d1, d2, dres, g1, g2):
        dx1, dw1 = _rms_bwd(a, g1, d1)
        dx2, dw2 = _rms_bwd(a, g2, d2)
        return dx1 + dx2 + dres, dw1, dw2

    dh2, g_kv_in_norm, g_mla_norm = _rowcall(kv_mla_norm_bwd, [h2, dhn, dxn2, dh3], [kv_in_norm[None, :], mla_norm],
                                             [(d_model, F32)], [d_model, d_model], name="kv_mla_norm_bwd")
    dh1, g_mlp_norm0, g["mlp_w_up0"], g["mlp_w_down0"] = _mlp_bwd(
        dh2, mlp0_saved, mlp_norm[0:1], w["mlp_w_up0"], w["mlp_w_down0"], "mlp0", after=scatter_start("mla"))

    dmixed = _mm(dh1, w["hgrn_w_o"], mode="nt", after=scatter_start("mlp0"), name="hgrn_out_bwd_x")
    g["hgrn_w_o"] = _mm(mixed, dh1, mode="tn", name="hgrn_out_bwd_w")
    do_rec, dzg, g_g_norm = _rowcall(_head_norm_gate_bwd, [o_rec, zg, dmixed], [hgrn_g_norm],
                                     [(d_model, F32), (d_model, BF16)], [hd], name="hgrn_gate_bwd")
    dzq, dzf, dzi, g_lb = _hgrn_bwd(zq, zf, zi, lb_logits_full, states, do_rec, name="hgrn_bwd")
    dxn0 = None
    for nm, dz in (("hgrn_w_q", dzq), ("hgrn_w_f", dzf), ("hgrn_w_i", dzi), ("hgrn_w_g", dzg)):
        dxn0 = _mm(dz, w[nm], mode="nt", add=dxn0, name=f"{nm}_bwd_x")
        g[nm] = _mm(xn0, dz, mode="tn", name=f"{nm}_bwd_w")

    def in_norm_bwd(a, dy, dres, gw):
        dx, dw = _rms_bwd(a, gw, dy)
        return dx + dres, dw

    grad_x, g_hgrn_norm = _rowcall(in_norm_bwd, [xs, dxn0, dh1], [hgrn_norm_full], [(d_model, F32)], [d_model],
                                   name="hgrn_norm_bwd")

    last = scatter_start("hgrn")
    small_parts = [g_hgrn_norm, g_lb, g_g_norm, g_mla_norm, g_q_norm, g_kv_in_norm, g_kv_norm, g_mlp_norm0,
                   g_mlp_norm1, g_final_norm, loss_part]
    small_sizes = [p.shape[1] for p in small_parts]
    small_terms = _exchange([jnp.concatenate(small_parts, axis=1)], scatter=False, name="gather_small")[0]
    small_sum = _sum_terms(small_terms, name="sum_small")
    offs = [0]
    for sz in small_sizes:
        offs.append(offs[-1] + sz)
    (s_hgrn_norm, s_lb, s_g_norm, s_mla_norm, s_q_norm, s_kv_in_norm, s_kv_norm, s_mlp_norm0, s_mlp_norm1, s_final_norm,
     s_loss) = [small_sum[:, a:b] for a, b in zip(offs[:-1], offs[1:])]
    shard = hgrn_norm.shape[1]
    g_lb_logits = _lb_logits_grad(lax.dynamic_slice_in_dim(s_lb, me * shard, shard, axis=1), hgrn_lb_logits,
                                  name="lb_logits_grad")
    loss = s_loss[0, 0]

    res = {}
    for tag, names in groups.items():
        for k, t in zip(names, _exchange_wait(scatter_state[tag], last, name=f"scatter_{tag}_wait")):
            if k.startswith("mlp_w_"):
                base, layer = k[:-1], int(k[-1])
                wk, mk, vk = given[base][layer], given["m_" + base][layer], given["v_" + base][layer]
            else:
                wk, mk, vk = given[k], given["m_" + k], given["v_" + k]
            shape = wk.shape
            wk, mk, vk = (a.reshape(shape[-2], shape[-1]) for a in (wk, mk, vk))
            upd = _adam(wk, t, mk, vk, name=f"adam_{k}")
            last = upd[0]
            res[k] = [o.reshape(shape) for o in upd]
    for base in ("mlp_w_up", "mlp_w_down"):
        res[base] = [jnp.stack([res[base + "0"][i], res[base + "1"][i]], axis=0) for i in range(4)]

    small_grads = {
        "hgrn_norm": lax.dynamic_slice_in_dim(s_hgrn_norm, me * shard, shard, axis=1),
        "hgrn_g_norm": s_g_norm, "hgrn_lb_logits": g_lb_logits, "mla_norm": s_mla_norm, "mla_q_norm": s_q_norm,
        "kv_in_norm": s_kv_in_norm, "kv_norm": s_kv_norm,
        "mlp_norm": jnp.concatenate([s_mlp_norm0, s_mlp_norm1], axis=0), "final_norm": s_final_norm,
    }
    small_names = list(small_grads)

    def flat(a):
        return a.reshape(1, -1)

    packed = [jnp.concatenate([flat(src[pre + k]) for k in small_names], axis=1)
              for src, pre in ((given, ""), (small_grads, ""), (given, "m_"), (given, "v_"))]
    small_out = _adam(packed[0], packed[1][None], packed[2], packed[3], name="adam_small")
    off = 0
    for k in small_names:
        size = given[k].size
        res[k] = [o[:, off:off + size].reshape(given[k].shape) for o in small_out]
        off += size

    outs = [loss, grad_x[None]]
    for i in range(4):
        outs += [res[k][i] for k in weight_names]
    return tuple(outs)
```

```python
import functools

import jax
import jax.numpy as jnp
from jax import lax
from jax.experimental import pallas as pl
from jax.experimental.pallas import tpu as pltpu

F32 = jnp.float32
BF16 = jnp.bfloat16

EPS = 1e-6
LANES = 128
N_DEV = 8
V7X_VMEM_LIMIT_BYTES = 56 << 20

HGRN_HEADS = 8
HGRN_CHUNK = 64
HGRN_SUB = 16
EXP_CLAMP = 80.0
MLA_HEADS = 16
MLA_NOPE = 128
MLA_ROPE = 64
ROPE_THETA = 10000.0
ATTN_SCALE = (MLA_NOPE + MLA_ROPE) ** -0.5

ADAM_LR = 0.001
ADAM_B1 = 0.9
ADAM_B2 = 0.999
ADAM_EPS = 1e-08
ADAM_WD = 0.01
ADAM_STEP = 10

_NN = ((1,), (0,))
_NT = ((1,), (1,))
_TN = ((0,), (0,))


def _params(*sem):
    return pltpu.CompilerParams(dimension_semantics=sem, vmem_limit_bytes=V7X_VMEM_LIMIT_BYTES)


def _dot(a, b, dims):
    return lax.dot_general(a.astype(BF16), b.astype(BF16), (dims, ((), ())), preferred_element_type=F32)


def _dot_f32(a, b, dims=_NN):
    return lax.dot_general(a, b, (dims, ((), ())), precision=lax.Precision.HIGHEST, preferred_element_type=F32)


def _sigmoid(x):
    return 1.0 / (1.0 + jnp.exp(-x))


def _rms(x, w):
    r = lax.rsqrt(jnp.mean(x * x, axis=-1, keepdims=True) + EPS)
    return x * r * w


def _rms_bwd(x, w, dy):
    r = lax.rsqrt(jnp.mean(x * x, axis=-1, keepdims=True) + EPS)
    xh = x * r
    dw = jnp.sum(dy * xh, axis=0, keepdims=True)
    dxh = dy * w
    dx = r * (dxh - xh * jnp.mean(dxh * xh, axis=-1, keepdims=True))
    return dx, dw


def _mm(a, b, *, mode, name, out_dtype=F32, add=None, epilogue=None, aux=None, after=None, scale=None):
    if mode == "nn":
        (m, k), (k2, n) = a.shape, b.shape
    elif mode == "nt":
        (m, k), (n, k2) = a.shape, b.shape
    else:
        (k, m), (k2, n) = a.shape, b.shape
    assert k == k2, (name, a.shape, b.shape)
    tm, tn = min(m, 512), min(n, 512)
    assert m % tm == 0 and n % tn == 0, (name, m, n)
    dims = {"nn": _NN, "nt": _NT, "tn": _TN}[mode]
    a_spec = pl.BlockSpec((k, tm), lambda i, j: (0, i)) if mode == "tn" else pl.BlockSpec((tm, k), lambda i, j: (i, 0))
    b_spec = pl.BlockSpec((tn, k), lambda i, j: (j, 0)) if mode == "nt" else pl.BlockSpec((k, tn), lambda i, j: (0, j))
    o_spec = pl.BlockSpec((tm, tn), lambda i, j: (i, j))
    operands, in_specs = [a, b], [a_spec, b_spec]
    for extra in (add, aux):
        if extra is not None:
            assert extra.shape == (m, n), (name, extra.shape)
            operands.append(extra)
            in_specs.append(o_spec)
    n_in = len(operands)
    if after is not None:
        operands.append(after)
        in_specs.append(pl.BlockSpec(memory_space=pl.ANY))
    if epilogue == "relu2":
        out_shape = [jax.ShapeDtypeStruct((m, n), F32), jax.ShapeDtypeStruct((m, n), BF16)]
        out_specs = [o_spec, o_spec]
    else:
        out_shape = jax.ShapeDtypeStruct((m, n), out_dtype)
        out_specs = o_spec

    def body(*refs):
        acc = _dot(refs[0][...], refs[1][...], dims)
        extras, outs = refs[2:n_in], refs[len(operands):]
        if scale is not None:
            acc = acc * scale
        if add is not None:
            acc = acc + extras[0][...]
        if epilogue == "relu2":
            outs[0][...] = acc
            outs[1][...] = jnp.square(jnp.maximum(acc, 0.0)).astype(BF16)
        elif epilogue == "relu2_bwd":
            outs[0][...] = (acc * (2.0 * jnp.maximum(extras[-1][...], 0.0))).astype(out_dtype)
        else:
            outs[0][...] = acc.astype(out_dtype)

    return pl.pallas_call(
        body, name=name, grid=(m // tm, n // tn), in_specs=in_specs, out_specs=out_specs, out_shape=out_shape,
        compiler_params=_params("parallel", "parallel"),
    )(*operands)


def _rowcall(fn, rows, consts, outs, accs, *, name, tr=256):
    s = rows[0].shape[0]
    tr = min(tr, s)
    assert s % tr == 0
    n_out = len(outs)
    in_specs = [pl.BlockSpec((tr, r.shape[1]), lambda i: (i, 0)) for r in rows]
    in_specs += [pl.BlockSpec(c.shape, lambda i: (0, 0)) for c in consts]
    out_shape = [jax.ShapeDtypeStruct((s, w), dt) for w, dt in outs] + [jax.ShapeDtypeStruct((1, w), F32) for w in accs]
    out_specs = [pl.BlockSpec((tr, w), lambda i: (i, 0)) for w, _ in outs] + [pl.BlockSpec((1, w), lambda i: (0, 0)) for w in accs]
    n_in = len(rows) + len(consts)

    def body(*refs):
        res = fn(*[r[...] for r in refs[:n_in]])
        out_refs = refs[n_in:]
        for ref, val in zip(out_refs[:n_out], res[:n_out]):
            ref[...] = val.astype(ref.dtype)
        i = pl.program_id(0)
        for ref, val in zip(out_refs[n_out:], res[n_out:]):
            @pl.when(i == 0)
            def _(ref=ref, val=val):
                ref[...] = val

            @pl.when(i > 0)
            def _(ref=ref, val=val):
                ref[...] += val

    return pl.pallas_call(
        body, name=name, grid=(s // tr,), in_specs=in_specs, out_specs=out_specs, out_shape=out_shape,
        compiler_params=_params("arbitrary" if accs else "parallel"),
    )(*rows, *consts)


def _rope_tables(seq):
    half = MLA_ROPE // 2
    inv_freq = ROPE_THETA ** (-jnp.arange(half, dtype=F32) / half)
    ang = jnp.arange(seq, dtype=F32)[:, None] * inv_freq[None, :]
    cos, sin, zero = jnp.cos(ang), jnp.sin(ang), jnp.zeros((seq, half), F32)
    t_c = jnp.concatenate([cos, cos, zero, zero], axis=1)
    t_s1 = jnp.concatenate([-sin, zero, zero, zero], axis=1)
    t_s2 = jnp.concatenate([zero, sin, zero, zero], axis=1)
    return t_c, t_s1, t_s2


def _rope(slab, t_c, t_s1, t_s2):
    return slab * t_c + pltpu.roll(slab, 96, 1) * t_s1 + pltpu.roll(slab, 32, 1) * t_s2


def _rope_t(d, t_c, t_s1, t_s2):
    return d * t_c + pltpu.roll(d * t_s1, 32, 1) + pltpu.roll(d * t_s2, 96, 1)


def _lower_bound(logits):
    l0, l1 = logits[0:1, :], logits[1:2, :]
    mx = jnp.maximum(l0, l1)
    e0, e1 = jnp.exp(l0 - mx), jnp.exp(l1 - mx)
    return e0 / (e0 + e1)


def _tri(n, lower):
    row = lax.broadcasted_iota(jnp.int32, (n, n), 0)
    col = lax.broadcasted_iota(jnp.int32, (n, n), 1)
    return (row >= col) if lower else (row <= col)


def _hgrn_intra(q, k, b, b_sc):
    c = HGRN_CHUNK
    b_sc[...] = b
    qts, decs, scores = [], [], []
    for i in range(c // HGRN_SUB):
        lo = i * HGRN_SUB
        ref = b_sc[lo - 1:lo, :] if i > 0 else jnp.zeros((1, LANES), F32)
        qt = q[lo:lo + HGRN_SUB, :] * jnp.exp(b[lo:lo + HGRN_SUB, :] - ref)
        dec = jnp.exp(jnp.minimum(ref - b, EXP_CLAMP))
        qts.append(qt)
        decs.append(dec)
        scores.append(_dot(qt, k * dec, _NT))
    a = jnp.where(_tri(c, True), jnp.concatenate(scores, axis=0), 0.0)
    return a, qts, decs


def _hgrn_fwd(zq, zf, zi, lb_logits, *, name):
    s, d = zq.shape
    h_n, c = d // LANES, HGRN_CHUNK
    nc = s // c

    def body(zq_ref, zf_ref, zi_ref, lb_ref, o_ref, st_ref, state_sc, b_sc):
        @pl.when(pl.program_id(1) == 0)
        def _():
            state_sc[...] = jnp.zeros_like(state_sc)

        lb = _lower_bound(lb_ref[...])
        zq_v = zq_ref[...]
        q = zq_v * _sigmoid(zq_v)
        f = lb + (1.0 - lb) * _sigmoid(zf_ref[...])
        g = jnp.log(f)
        k = 1.0 - f
        v = zi_ref[...]
        b = _dot_f32(_tri(c, True).astype(F32), g)
        s0t = state_sc[...]
        st_ref[...] = s0t
        a, _, _ = _hgrn_intra(q, k, b, b_sc)
        o_ref[...] = _dot(q * jnp.exp(b), s0t, _NT) + _dot(a, v, _NN)
        bl = b_sc[c - 1:c, :]
        state_sc[...] = s0t * jnp.exp(bl) + _dot(v, k * jnp.exp(bl - b), _TN)

    tile = pl.BlockSpec((c, LANES), lambda h, i: (i, h))
    return pl.pallas_call(
        body, name=name, grid=(h_n, nc),
        in_specs=[tile, tile, tile, pl.BlockSpec((2, LANES), lambda h, i: (0, h))],
        out_specs=[tile, pl.BlockSpec((None, None, LANES, LANES), lambda h, i: (h, i, 0, 0))],
        out_shape=[jax.ShapeDtypeStruct((s, d), F32), jax.ShapeDtypeStruct((h_n, nc, LANES, LANES), F32)],
        scratch_shapes=[pltpu.VMEM((LANES, LANES), F32), pltpu.VMEM((c, LANES), F32)],
        compiler_params=_params("parallel", "arbitrary"),
    )(zq, zf, zi, lb_logits)


def _hgrn_bwd(zq, zf, zi, lb_logits, states, do, *, name):
    s, d = zq.shape
    h_n, c = d // LANES, HGRN_CHUNK
    nc = s // c

    def body(zq_ref, zf_ref, zi_ref, lb_ref, st_ref, do_ref, dzq_ref, dzf_ref, dzi_ref, dlb_ref, dstate_sc, b_sc):
        @pl.when(pl.program_id(1) == 0)
        def _():
            dstate_sc[...] = jnp.zeros_like(dstate_sc)
            dlb_ref[...] = jnp.zeros_like(dlb_ref)

        lb = _lower_bound(lb_ref[...])
        zq_v = zq_ref[...]
        sq = _sigmoid(zq_v)
        q = zq_v * sq
        sf = _sigmoid(zf_ref[...])
        f = lb + (1.0 - lb) * sf
        g = jnp.log(f)
        k = 1.0 - f
        v = zi_ref[...]
        d_o = do_ref[...]
        b = _dot_f32(_tri(c, True).astype(F32), g)
        s0t = st_ref[...]
        ds1t = dstate_sc[...]
        a, qts, decs = _hgrn_intra(q, k, b, b_sc)
        bl = b_sc[c - 1:c, :]
        eb, ebl, dec_end = jnp.exp(b), jnp.exp(bl), jnp.exp(bl - b)
        qe = q * eb
        da = jnp.where(_tri(c, True), _dot(d_o, v, _NT), 0.0)
        dv = _dot(a, d_o, _TN) + _dot(k * dec_end, ds1t, _NT)
        dk_state = _dot(v, ds1t, _NN) * dec_end
        dk = dk_state
        dq_blocks = []
        for i in range(c // HGRN_SUB):
            lo = i * HGRN_SUB
            ref = b_sc[lo - 1:lo, :] if i > 0 else jnp.zeros((1, LANES), F32)
            da_i = da[lo:lo + HGRN_SUB, :]
            dq_blocks.append(_dot_f32(da_i, k * decs[i], _NN) * jnp.exp(b[lo:lo + HGRN_SUB, :] - ref))
            dk = dk + _dot_f32(da_i, qts[i], _TN) * decs[i]
        dq = _dot(d_o, s0t, _NN) * eb + jnp.concatenate(dq_blocks, axis=0)
        db_last = jnp.sum(k * dk_state, axis=0, keepdims=True) + ebl * jnp.sum(s0t * ds1t, axis=0, keepdims=True)
        last_row = lax.broadcasted_iota(jnp.int32, (c, LANES), 0) == c - 1
        db = q * dq - k * dk + jnp.where(last_row, db_last, 0.0)
        dg = _dot_f32(_tri(c, False).astype(F32), db)
        df = dg / f - dk
        dzf_ref[...] = (df * (1.0 - lb) * sf * (1.0 - sf)).astype(BF16)
        dlb_ref[...] += jnp.sum(df * (1.0 - sf), axis=0, keepdims=True)
        dzq_ref[...] = (dq * sq * (1.0 + zq_v * (1.0 - sq))).astype(BF16)
        dzi_ref[...] = dv.astype(BF16)
        dstate_sc[...] = ds1t * ebl + _dot(d_o, qe, _TN)

    tile = pl.BlockSpec((c, LANES), lambda h, i: (nc - 1 - i, h))
    out = jax.ShapeDtypeStruct((s, d), BF16)
    return pl.pallas_call(
        body, name=name, grid=(h_n, nc),
        in_specs=[tile, tile, tile, pl.BlockSpec((2, LANES), lambda h, i: (0, h)),
                  pl.BlockSpec((None, None, LANES, LANES), lambda h, i: (h, nc - 1 - i, 0, 0)), tile],
        out_specs=[tile, tile, tile, pl.BlockSpec((1, LANES), lambda h, i: (0, h))],
        out_shape=[out, out, out, jax.ShapeDtypeStruct((1, d), F32)],
        scratch_shapes=[pltpu.VMEM((LANES, LANES), F32), pltpu.VMEM((c, LANES), F32)],
        compiler_params=_params("parallel", "arbitrary"),
    )(zq, zf, zi, lb_logits, states, do)


LOG2E = 1.4426950408889634
LN2 = 0.6931471805599453
Q_PRESCALE = ATTN_SCALE * LOG2E


def _attn_tile(s):
    return min(512, max(128, s // 2))


def _causal_pairs(n, q_major):
    pairs = [(i, j) for i in range(n) for j in range(i + 1)] if q_major else [(i, j) for j in range(n) for i in range(j, n)]
    return jnp.asarray([p[0] for p in pairs], jnp.int32), jnp.asarray([p[1] for p in pairs], jnp.int32)


def _scores(qn_ref, qr_ref, kn_ref, kr_ref):
    q = jnp.concatenate([qn_ref[...], qr_ref[...]], axis=1)
    k = jnp.concatenate([kn_ref[...], kr_ref[...]], axis=1)
    return q, k, _dot(q, k, _NT)


def _attn_fwd(qn, qr, kn, kr, v, *, name):
    s, t = qn.shape[0], _attn_tile(qn.shape[0])
    q_blk, k_blk = _causal_pairs(s // t, True)

    def body(qi_ref, kj_ref, qn_ref, qr_ref, kn_ref, kr_ref, v_ref, o_ref, lse_ref, m_sc, l_sc, acc_sc):
        p_id = pl.program_id(1)
        i, j = qi_ref[p_id], kj_ref[p_id]

        @pl.when(j == 0)
        def _():
            m_sc[...] = jnp.full_like(m_sc, -jnp.inf)
            l_sc[...] = jnp.zeros_like(l_sc)
            acc_sc[...] = jnp.zeros_like(acc_sc)

        def update(sc):
            m_prev = m_sc[...]
            m_new = jnp.maximum(m_prev, jnp.max(sc, axis=1, keepdims=True))
            alpha = jnp.exp2(m_prev - m_new)
            p = jnp.exp2(sc - m_new[:, :1])
            l_sc[...] = alpha * l_sc[...] + jnp.sum(p, axis=1, keepdims=True)
            acc_sc[...] = alpha * acc_sc[...] + _dot(p, v_ref[...], _NN)
            m_sc[...] = m_new

        @pl.when(j < i)
        def _():
            update(_scores(qn_ref, qr_ref, kn_ref, kr_ref)[2])

        @pl.when(j == i)
        def _():
            update(jnp.where(_tri(t, True), _scores(qn_ref, qr_ref, kn_ref, kr_ref)[2], -jnp.inf))
            o_ref[...] = (acc_sc[...] / l_sc[...]).astype(BF16)
            lse_ref[...] = m_sc[...] + jnp.log(l_sc[...]) * LOG2E

    q_spec = pl.BlockSpec((t, LANES), lambda h, p, qi, kj: (qi[p], h))
    k_spec = pl.BlockSpec((t, LANES), lambda h, p, qi, kj: (kj[p], h))
    kr_spec = pl.BlockSpec((t, LANES), lambda h, p, qi, kj: (kj[p], 0))
    stat = pltpu.VMEM((t, LANES), F32)
    return pl.pallas_call(
        body, name=name,
        grid_spec=pltpu.PrefetchScalarGridSpec(
            num_scalar_prefetch=2, grid=(MLA_HEADS, q_blk.shape[0]),
            in_specs=[q_spec, q_spec, k_spec, kr_spec, k_spec], out_specs=[q_spec, q_spec],
            scratch_shapes=[stat, stat, stat]),
        out_shape=[jax.ShapeDtypeStruct(qn.shape, BF16), jax.ShapeDtypeStruct(qn.shape, F32)],
        compiler_params=_params("parallel", "arbitrary"),
    )(q_blk, k_blk, qn, qr, kn, kr, v)


def _attn_bwd(qn, qr, kn, kr, v, do, lse, delta, *, name):
    s, t = qn.shape[0], _attn_tile(qn.shape[0])
    n = s // t
    q_blk, k_blk = _causal_pairs(n, False)

    def body(qi_ref, kj_ref, qn_ref, qr_ref, kn_ref, kr_ref, v_ref, do_ref, lse_ref, delta_ref,
             dqn_ref, dqr_ref, dkn_ref, dv_ref, dkr_ref, dk_sc, dv_sc):
        p_id = pl.program_id(1)
        i, j = qi_ref[p_id], kj_ref[p_id]

        @pl.when(p_id == 0)
        def _():
            dqn_ref[...] = jnp.zeros_like(dqn_ref)
            dqr_ref[...] = jnp.zeros_like(dqr_ref)

        @pl.when(i == j)
        def _():
            dk_sc[...] = jnp.zeros_like(dk_sc)
            dv_sc[...] = jnp.zeros_like(dv_sc)

        def accumulate(q, k, sc):
            p = jnp.exp2(sc - lse_ref[...][:, :1])
            d_o = do_ref[...]
            ds = (p * (_dot(d_o, v_ref[...], _NT) - delta_ref[...][:, :1])).astype(BF16)
            dv_sc[...] += _dot(p, d_o, _TN)
            dk_sc[...] += _dot(ds, q, _TN)
            dq = _dot(ds, k, _NN) * ATTN_SCALE
            rows = pl.ds(pl.multiple_of(i * t, t), t)
            dqn_ref[rows, :] += dq[:, :LANES]
            dqr_ref[rows, :] += dq[:, LANES:]

        @pl.when(j < i)
        def _():
            accumulate(*_scores(qn_ref, qr_ref, kn_ref, kr_ref))

        @pl.when(j == i)
        def _():
            q, k, sc = _scores(qn_ref, qr_ref, kn_ref, kr_ref)
            accumulate(q, k, jnp.where(_tri(t, True), sc, -jnp.inf))

        @pl.when(i == n - 1)
        def _():
            dkn_ref[...] = (dk_sc[:, :LANES] * LN2).astype(BF16)
            dkr_ref[...] = dk_sc[:, LANES:] * LN2
            dv_ref[...] = dv_sc[...].astype(BF16)

    q_spec = pl.BlockSpec((t, LANES), lambda h, p, qi, kj: (qi[p], h))
    k_spec = pl.BlockSpec((t, LANES), lambda h, p, qi, kj: (kj[p], h))
    kr_spec = pl.BlockSpec((t, LANES), lambda h, p, qi, kj: (kj[p], 0))
    head_spec = pl.BlockSpec((s, LANES), lambda h, p, qi, kj: (0, h))
    f32_out, bf16_out = jax.ShapeDtypeStruct(qn.shape, F32), jax.ShapeDtypeStruct(qn.shape, BF16)
    return pl.pallas_call(
        body, name=name,
        grid_spec=pltpu.PrefetchScalarGridSpec(
            num_scalar_prefetch=2, grid=(MLA_HEADS, q_blk.shape[0]),
            in_specs=[q_spec, q_spec, k_spec, kr_spec, k_spec, q_spec, q_spec, q_spec],
            out_specs=[head_spec, head_spec, k_spec, k_spec, k_spec],
            scratch_shapes=[pltpu.VMEM((t, 2 * LANES), F32), pltpu.VMEM((t, LANES), F32)]),
        out_shape=[f32_out, f32_out, bf16_out, bf16_out, f32_out],
        compiler_params=_params("parallel", "arbitrary"),
    )(q_blk, k_blk, qn, qr, kn, kr, v, do, lse, delta)


def _exchange(arrs, *, scatter, name):
    n = len(arrs)
    out_shape = [jax.ShapeDtypeStruct(a.shape if scatter else (N_DEV, *a.shape), a.dtype) for a in arrs]

    def body(*refs):
        ins, outs = refs[:n], refs[n:2 * n]
        send_sems, recv_sems, local_sems = refs[2 * n:]
        x, y, c = lax.axis_index("x"), lax.axis_index("y"), lax.axis_index("c")
        me = 4 * x + 2 * y + c
        copies = []
        for k in range(n):
            local = pltpu.make_async_copy(ins[k].at[me] if scatter else ins[k], outs[k].at[me], local_sems.at[k])
            local.start()
            copies.append(local)
            for d in range(1, N_DEV):
                px, py, pc = (x + (d >> 2)) % 2, (y + ((d >> 1) & 1)) % 2, (c + (d & 1)) % 2
                peer = 4 * px + 2 * py + pc
                remote = pltpu.make_async_remote_copy(
                    src_ref=ins[k].at[peer] if scatter else ins[k], dst_ref=outs[k].at[me],
                    send_sem=send_sems.at[k, d - 1], recv_sem=recv_sems.at[k, d - 1],
                    device_id=(px, py, pc), device_id_type=pl.DeviceIdType.MESH)
                remote.start()
                copies.append(remote)
        for cp in copies:
            cp.wait()

    any_spec = pl.BlockSpec(memory_space=pl.ANY)
    return pl.pallas_call(
        body, name=name, in_specs=[any_spec] * n, out_specs=[any_spec] * n, out_shape=out_shape,
        scratch_shapes=[pltpu.SemaphoreType.DMA((n, N_DEV - 1)), pltpu.SemaphoreType.DMA((n, N_DEV - 1)),
                        pltpu.SemaphoreType.DMA((n,))],
    )(*arrs)


def _peers(x, y, c):
    out = []
    for d in range(1, N_DEV):
        px, py, pc = (x + (d >> 2)) % 2, (y + ((d >> 1) & 1)) % 2, (c + (d & 1)) % 2
        out.append(((px, py, pc), 4 * px + 2 * py + pc))
    return out


def _exchange_copies(ins, lands, send_sems, recv_sems, scatter):
    x, y, c = lax.axis_index("x"), lax.axis_index("y"), lax.axis_index("c")
    me = 4 * x + 2 * y + c
    local, remote = [], []
    for k in range(len(ins)):
        local.append(pltpu.make_async_copy(ins[k].at[me] if scatter else ins[k], lands[k].at[me],
                                           recv_sems.at[k * N_DEV + N_DEV - 1]))
        for d, (coords, peer) in enumerate(_peers(x, y, c)):
            remote.append(pltpu.make_async_remote_copy(
                src_ref=ins[k].at[peer] if scatter else ins[k], dst_ref=lands[k].at[me],
                send_sem=send_sems.at[k * N_DEV + d], recv_sem=recv_sems.at[k * N_DEV + d],
                device_id=coords, device_id_type=pl.DeviceIdType.MESH))
    return local, remote


def _exchange_start(arrs, *, scatter, name):
    n = len(arrs)
    hbm = pl.BlockSpec(memory_space=pltpu.HBM)
    sem = pl.BlockSpec(memory_space=pltpu.SEMAPHORE)
    lands = [lax.empty(a.shape if scatter else (N_DEV, *a.shape), a.dtype) for a in arrs]

    def body(*refs):
        ins, land_refs = refs[:n], refs[n:2 * n]
        send_sems, recv_sems, token = refs[2 * n], refs[2 * n + 1], refs[-1]
        local, remote = _exchange_copies(ins, land_refs, send_sems, recv_sems, scatter)
        for cp in local + remote:
            cp.start()
        token[...] = jnp.zeros_like(token)

    operands = [pltpu.with_memory_space_constraint(a, pltpu.HBM) for a in list(arrs) + lands]
    res = pl.pallas_call(
        body, name=name,
        out_shape=(pltpu.SemaphoreType.DMA((n * N_DEV,)), pltpu.SemaphoreType.DMA((n * N_DEV,)),
                   *[pltpu.HBM(o.shape, o.dtype) for o in operands], jax.ShapeDtypeStruct((8, LANES), F32)),
        in_specs=[hbm] * (2 * n), out_specs=(sem, sem, *[hbm] * (2 * n), pl.BlockSpec(memory_space=pltpu.VMEM)),
        input_output_aliases={i: 2 + i for i in range(2 * n)},
        compiler_params=pltpu.CompilerParams(has_side_effects=pltpu.SideEffectType.DATAFLOW_SIDE_EFFECTING),
    )(*operands)
    return (res[0], res[1], list(res[2:2 + n]), list(res[2 + n:2 + 2 * n]), scatter), res[-1]


def _exchange_wait(state, after, *, name):
    send_sems, recv_sems, ins, lands, scatter = state
    n = len(ins)
    hbm = pl.BlockSpec(memory_space=pltpu.HBM)
    sem = pl.BlockSpec(memory_space=pltpu.SEMAPHORE)

    def body(*refs):
        in_refs, land_refs = refs[:n], refs[n:2 * n]
        local, remote = _exchange_copies(in_refs, land_refs, refs[2 * n], refs[2 * n + 1], scatter)
        for cp in local:
            cp.wait()
        for cp in remote:
            cp.wait_send()
            cp.wait_recv()

    res = pl.pallas_call(
        body, name=name, out_shape=tuple(pltpu.HBM(o.shape, o.dtype) for o in ins + lands),
        in_specs=[hbm] * (2 * n) + [sem, sem, pl.BlockSpec(memory_space=pl.ANY)], out_specs=tuple([hbm] * (2 * n)),
        input_output_aliases={i: i for i in range(2 * n)},
        compiler_params=pltpu.CompilerParams(has_side_effects=pltpu.SideEffectType.DATAFLOW_SIDE_EFFECTING),
    )(*ins, *lands, send_sems, recv_sems, after)
    return list(res[n:])


def _adam(w, terms, m, v, *, name):
    r, c = w.shape
    n = terms.shape[0]
    tr = min(r, 128)
    assert r % tr == 0

    def body(w_ref, t_ref, m_ref, v_ref, g_out, d_out, m_out, v_out):
        g = t_ref[0]
        for s in range(1, n):
            g = g + t_ref[s]
        m1 = ADAM_B1 * m_ref[...] + (1.0 - ADAM_B1) * g
        v1 = ADAM_B2 * v_ref[...] + (1.0 - ADAM_B2) * jnp.square(g)
        m_hat = m1 / (1.0 - ADAM_B1 ** ADAM_STEP)
        v_hat = v1 / (1.0 - ADAM_B2 ** ADAM_STEP)
        g_out[...] = g
        d_out[...] = -ADAM_LR * (m_hat / (jnp.sqrt(v_hat) + ADAM_EPS) + ADAM_WD * w_ref[...])
        m_out[...] = m1
        v_out[...] = v1

    spec = pl.BlockSpec((tr, c), lambda i: (i, 0))
    out = jax.ShapeDtypeStruct((r, c), F32)
    return pl.pallas_call(
        body, name=name, grid=(r // tr,),
        in_specs=[spec, pl.BlockSpec((n, tr, c), lambda i: (0, i, 0)), spec, spec], out_specs=[spec] * 4,
        out_shape=[out] * 4, compiler_params=_params("parallel"),
    )(w, terms, m, v)


def _sum_terms(terms, *, name):
    n, _, p = terms.shape

    def body(t_ref, o_ref):
        acc = t_ref[0]
        for s in range(1, n):
            acc = acc + t_ref[s]
        o_ref[...] = acc

    return pl.pallas_call(body, name=name, out_shape=jax.ShapeDtypeStruct((1, p), F32))(terms)


def _lb_logits_grad(dlb, logits, *, name):
    def body(dlb_ref, l_ref, o_ref):
        lb = _lower_bound(l_ref[...])
        d0 = dlb_ref[...] * lb * (1.0 - lb)
        o_ref[...] = jnp.concatenate([d0, -d0], axis=0)

    return pl.pallas_call(body, name=name, out_shape=jax.ShapeDtypeStruct(logits.shape, F32))(dlb, logits)


def _silu_grad(z):
    sg = _sigmoid(z)
    return sg * (1.0 + z * (1.0 - sg))


def _head_norm_gate(o, zg, gn):
    outs = []
    for h in range(HGRN_HEADS):
        sl = slice(h * LANES, (h + 1) * LANES)
        zg_h = zg[:, sl]
        outs.append(_rms(o[:, sl], gn) * (zg_h * _sigmoid(zg_h)))
    return (jnp.concatenate(outs, axis=1),)


def _head_norm_gate_bwd(o, zg, dm, gn):
    do_parts, dzg_parts, dgn = [], [], jnp.zeros((1, LANES), F32)
    for h in range(HGRN_HEADS):
        sl = slice(h * LANES, (h + 1) * LANES)
        o_h, zg_h, dm_h = o[:, sl], zg[:, sl], dm[:, sl]
        gate = zg_h * _sigmoid(zg_h)
        do_h, dgn_h = _rms_bwd(o_h, gn, dm_h * gate)
        dgn = dgn + dgn_h
        do_parts.append(do_h)
        dzg_parts.append(dm_h * _rms(o_h, gn) * _silu_grad(zg_h))
    return jnp.concatenate(do_parts, axis=1), jnp.concatenate(dzg_parts, axis=1), dgn


def _rope_slabs(x, t_c, t_s1, t_s2, transpose):
    fn = _rope_t if transpose else _rope
    return jnp.concatenate(
        [fn(x[:, h * LANES:(h + 1) * LANES], t_c, t_s1, t_s2) for h in range(x.shape[1] // LANES)], axis=1)


def _loss_head(h, tgt, w):
    d = h.shape[1]
    r = lax.rsqrt(jnp.mean(h * h, axis=-1, keepdims=True) + EPS)
    xh = h * r
    err = xh * w - tgt
    loss = 0.5 * jnp.sum(jnp.mean(err * err, axis=-1, keepdims=True), axis=0, keepdims=True)
    dy = err / d
    dxh = dy * w
    dh = r * (dxh - xh * jnp.mean(dxh * xh, axis=-1, keepdims=True))
    return dh, jnp.sum(dy * xh, axis=0, keepdims=True), jnp.broadcast_to(loss, (1, LANES))


def _mlp_fwd(h, norm, w_up, w_down, tag):
    d = h.shape[1]
    xn = _rowcall(lambda x, w: (_rms(x, w),), [h], [norm], [(d, BF16)], [], name=f"{tag}_norm")[0]
    u, act = _mm(xn, w_up, mode="nn", epilogue="relu2", name=f"{tag}_up")
    return _mm(act, w_down, mode="nn", add=h, name=f"{tag}_down"), (h, xn, u, act)


def _mlp_bwd(dh_out, saved, norm, w_up, w_down, tag, after=None):
    h, xn, u, act = saved
    d = h.shape[1]
    du = _mm(dh_out, w_down, mode="nt", epilogue="relu2_bwd", aux=u, out_dtype=BF16, after=after,
             name=f"{tag}_bwd_du")
    dw_down = _mm(act, dh_out, mode="tn", name=f"{tag}_bwd_wdown")
    dxn = _mm(du, w_up, mode="nt", name=f"{tag}_bwd_dxn")
    dw_up = _mm(xn, du, mode="tn", name=f"{tag}_bwd_wup")

    def norm_bwd(x, dy, dres, w):
        dx, dw = _rms_bwd(x, w, dy)
        return dx + dres, dw

    dh, dnorm = _rowcall(norm_bwd, [h, dxn, dh_out], [norm], [(d, F32)], [d], name=f"{tag}_bwd_norm")
    return dh, dnorm, dw_up, dw_down


def _row_major(g):
    return g.reshape(g.shape[0] * g.shape[1], g.shape[2])


def _col_major(g):
    return jnp.transpose(g, (1, 0, 2)).reshape(g.shape[1], g.shape[0] * g.shape[2])


def _col_terms(dw):
    k, n = dw.shape
    return jnp.transpose(dw.reshape(k, N_DEV, n // N_DEV), (1, 0, 2))


def _row_terms(dw):
    return dw.reshape(N_DEV, dw.shape[0] // N_DEV, dw.shape[1])


def kernel(x, hgrn_norm, hgrn_w_q, hgrn_w_f, hgrn_w_i, hgrn_w_g, hgrn_g_norm, hgrn_w_o, hgrn_lb_logits, mla_norm, mla_w_dq, mla_q_norm, mla_w_uq, mla_w_o, kv_in_norm, kv_w_dkv, kv_norm, kv_w_uk, kv_w_uv, mlp_norm, mlp_w_up, mlp_w_down, final_norm, loss_target, m_hgrn_norm, m_hgrn_w_q, m_hgrn_w_f, m_hgrn_w_i, m_hgrn_w_g, m_hgrn_g_norm, m_hgrn_w_o, m_hgrn_lb_logits, m_mla_norm, m_mla_w_dq, m_mla_q_norm, m_mla_w_uq, m_mla_w_o, m_kv_in_norm, m_kv_w_dkv, m_kv_norm, m_kv_w_uk, m_kv_w_uv, m_mlp_norm, m_mlp_w_up, m_mlp_w_down, m_final_norm, v_hgrn_norm, v_hgrn_w_q, v_hgrn_w_f, v_hgrn_w_i, v_hgrn_w_g, v_hgrn_g_norm, v_hgrn_w_o, v_hgrn_lb_logits, v_mla_norm, v_mla_w_dq, v_mla_q_norm, v_mla_w_uq, v_mla_w_o, v_kv_in_norm, v_kv_w_dkv, v_kv_norm, v_kv_w_uk, v_kv_w_uv, v_mlp_norm, v_mlp_w_up, v_mlp_w_down, v_final_norm):
    given = dict(locals())
    weight_names = ["hgrn_norm", "hgrn_w_q", "hgrn_w_f", "hgrn_w_i", "hgrn_w_g", "hgrn_g_norm", "hgrn_w_o",
                    "hgrn_lb_logits", "mla_norm", "mla_w_dq", "mla_q_norm", "mla_w_uq", "mla_w_o", "kv_in_norm",
                    "kv_w_dkv", "kv_norm", "kv_w_uk", "kv_w_uv", "mlp_norm", "mlp_w_up", "mlp_w_down", "final_norm"]
    me = 4 * lax.axis_index("x") + 2 * lax.axis_index("y") + lax.axis_index("c")
    xs, tgt = x[0], loss_target[0]
    seq, d_model = xs.shape
    n_heads, hd = MLA_HEADS, LANES

    big_local = {
        "hgrn_w_q": hgrn_w_q[0], "hgrn_w_f": hgrn_w_f[0], "hgrn_w_i": hgrn_w_i[0], "hgrn_w_g": hgrn_w_g[0],
        "hgrn_w_o": hgrn_w_o[0], "mla_w_dq": mla_w_dq[0], "mla_w_uq": mla_w_uq[0], "mla_w_o": mla_w_o[0],
        "kv_w_dkv": kv_w_dkv, "kv_w_uk": kv_w_uk, "kv_w_uv": kv_w_uv,
        "mlp_w_up0": mlp_w_up[0], "mlp_w_up1": mlp_w_up[1], "mlp_w_down0": mlp_w_down[0], "mlp_w_down1": mlp_w_down[1],
    }
    big_names = list(big_local)
    col_sharded = {"mla_w_uq", "kv_w_uk", "kv_w_uv", "mlp_w_up0", "mlp_w_up1"}
    vec_local = jnp.concatenate([hgrn_norm, hgrn_lb_logits], axis=0)
    first_names = ["hgrn_w_q", "hgrn_w_f", "hgrn_w_i", "hgrn_w_g"]
    later_names = [k for k in big_names if k not in first_names]

    def unshard(names, arrays):
        return {k: (_col_major(a) if k in col_sharded else _row_major(a)) for k, a in zip(names, arrays)}

    gathered = _exchange([big_local[k].astype(BF16) for k in first_names] + [vec_local], scatter=False, name="gather_first")
    gather_state, token = _exchange_start([big_local[k].astype(BF16) for k in later_names], scatter=False,
                                          name="gather_rest_start")
    w = unshard(first_names, gathered[:-1])
    vec_full = jnp.transpose(gathered[-1], (1, 0, 2)).reshape(3, d_model)
    hgrn_norm_full, lb_logits_full = vec_full[0:1], vec_full[1:3]
    t_c, t_s1, t_s2 = _rope_tables(seq)
    kv_lora = kv_w_uk.shape[0]

    xn0 = _rowcall(lambda a, g: (_rms(a, g),), [xs], [hgrn_norm_full], [(d_model, BF16)], [], name="hgrn_norm")[0]
    zq = _mm(xn0, w["hgrn_w_q"], mode="nn", after=token, name="hgrn_zq")
    zf = _mm(xn0, w["hgrn_w_f"], mode="nn", name="hgrn_zf")
    zi = _mm(xn0, w["hgrn_w_i"], mode="nn", name="hgrn_zi")
    zg = _mm(xn0, w["hgrn_w_g"], mode="nn", name="hgrn_zg")
    o_rec, states = _hgrn_fwd(zq, zf, zi, lb_logits_full, name="hgrn_fwd")
    mixed = _rowcall(_head_norm_gate, [o_rec, zg], [hgrn_g_norm], [(d_model, BF16)], [], name="hgrn_gate")[0]
    w.update(unshard(later_names, _exchange_wait(gather_state, mixed, name="gather_rest_wait")))
    w_uq3 = w["mla_w_uq"].reshape(-1, n_heads, MLA_NOPE + MLA_ROPE)
    w_uq_nope = w_uq3[:, :, :MLA_NOPE].reshape(-1, n_heads * hd)
    w_uq_rope = jnp.pad(w_uq3[:, :, MLA_NOPE:], ((0, 0), (0, 0), (0, hd - MLA_ROPE))).reshape(-1, n_heads * hd)
    w_dkv_pad = jnp.pad(w["kv_w_dkv"], ((0, 0), (0, kv_lora + hd - w["kv_w_dkv"].shape[1])))
    h1 = _mm(mixed, w["hgrn_w_o"], mode="nn", add=xs, name="hgrn_out")
    h2, mlp0_saved = _mlp_fwd(h1, mlp_norm[0:1], w["mlp_w_up0"], w["mlp_w_down0"], "mlp0")

    hn, xn2 = _rowcall(lambda a, g1, g2: (_rms(a, g1), _rms(a, g2)), [h2], [kv_in_norm[None, :], mla_norm],
                       [(d_model, BF16), (d_model, BF16)], [], name="kv_mla_norm")
    ckr = _mm(hn, w_dkv_pad, mode="nn", name="kv_down")

    def kv_latent(c_all, tc, ts1, ts2, g):
        return _rms(c_all[:, :kv_lora], g), _rope(c_all[:, kv_lora:], tc, ts1, ts2)

    c_kv, kr = _rowcall(kv_latent, [ckr, t_c, t_s1, t_s2], [kv_norm[None, :]], [(kv_lora, BF16), (hd, BF16)], [],
                        name="kv_latent")
    kn = _mm(c_kv, w["kv_w_uk"], mode="nn", out_dtype=BF16, name="kv_up_k")
    vv = _mm(c_kv, w["kv_w_uv"], mode="nn", out_dtype=BF16, name="kv_up_v")
    cq_pre = _mm(xn2, w["mla_w_dq"], mode="nn", name="q_down")
    c_q = _rowcall(lambda a, g: (_rms(a, g),), [cq_pre], [mla_q_norm], [(cq_pre.shape[1], BF16)], [], name="q_norm")[0]
    qn = _mm(c_q, w_uq_nope, mode="nn", out_dtype=BF16, scale=Q_PRESCALE, name="q_up_nope")
    qr_pre = _mm(c_q, w_uq_rope, mode="nn", name="q_up_rope")
    qr = _rowcall(lambda a, tc, ts1, ts2: (_rope_slabs(a, tc, ts1, ts2, False) * Q_PRESCALE,), [qr_pre, t_c, t_s1, t_s2], [],
                  [(n_heads * hd, BF16)], [], name="q_rope")[0]
    o_att, lse = _attn_fwd(qn, qr, kn, kr, vv, name="attn_fwd")
    h3 = _mm(o_att, w["mla_w_o"], mode="nn", add=h2, name="attn_out")
    h4, mlp1_saved = _mlp_fwd(h3, mlp_norm[1:2], w["mlp_w_up1"], w["mlp_w_down1"], "mlp1")
    dh4, g_final_norm, loss_part = _rowcall(_loss_head, [h4, tgt], [final_norm[None, :]], [(d_model, F32)],
                                            [d_model, LANES], name="loss_head")

    g = {}
    groups = {"mlp1": ["mlp_w_up1", "mlp_w_down1"],
              "mla": ["mla_w_o", "mla_w_uq", "mla_w_dq", "kv_w_uk", "kv_w_uv", "kv_w_dkv"],
              "mlp0": ["mlp_w_up0", "mlp_w_down0"],
              "hgrn": ["hgrn_w_o", "hgrn_w_q", "hgrn_w_f", "hgrn_w_i", "hgrn_w_g"]}
    scatter_state = {}

    def scatter_start(tag):
        scatter_state[tag], tok = _exchange_start(
            [(_col_terms if k in col_sharded else _row_terms)(g[k]) for k in groups[tag]], scatter=True,
            name=f"scatter_{tag}_start")
        return tok

    dh3, g_mlp_norm1, g["mlp_w_up1"], g["mlp_w_down1"] = _mlp_bwd(
        dh4, mlp1_saved, mlp_norm[1:2], w["mlp_w_up1"], w["mlp_w_down1"], "mlp1")
    d_oatt = _mm(dh3, w["mla_w_o"], mode="nt", out_dtype=BF16, after=scatter_start("mlp1"), name="attn_out_bwd_x")
    g["mla_w_o"] = _mm(o_att, dh3, mode="tn", name="attn_out_bwd_w")

    def head_delta(do, o):
        prod = do.astype(F32) * o.astype(F32)
        return (jnp.concatenate([jnp.broadcast_to(jnp.sum(prod[:, h * hd:(h + 1) * hd], axis=1, keepdims=True),
                                                  (prod.shape[0], hd)) for h in range(n_heads)], axis=1),)

    delta = _rowcall(head_delta, [d_oatt, o_att], [], [(n_heads * hd, F32)], [], name="attn_delta")[0]
    dqn, dqr, dkn, dvv, dkr = _attn_bwd(qn, qr, kn, kr, vv, d_oatt, lse, delta, name="attn_bwd")
    dqr_pre = _rowcall(lambda a, tc, ts1, ts2: (_rope_slabs(a, tc, ts1, ts2, True),), [dqr, t_c, t_s1, t_s2], [],
                       [(n_heads * hd, BF16)], [], name="q_rope_bwd")[0]
    dcq = _mm(dqn, w_uq_nope, mode="nt", name="q_up_nope_bwd_x")
    dcq = _mm(dqr_pre, w_uq_rope, mode="nt", add=dcq, name="q_up_rope_bwd_x")
    g_uq_nope = _mm(c_q, dqn, mode="tn", name="q_up_nope_bwd_w")
    g_uq_rope = _mm(c_q, dqr_pre, mode="tn", name="q_up_rope_bwd_w")
    q_lora = c_q.shape[1]
    g["mla_w_uq"] = jnp.concatenate([g_uq_nope.reshape(q_lora, n_heads, hd),
                                     g_uq_rope.reshape(q_lora, n_heads, hd)[:, :, :MLA_ROPE]], axis=2).reshape(q_lora, -1)
    dcq_pre, g_q_norm = _rowcall(lambda a, dy, gq: _rms_bwd(a, gq, dy), [cq_pre, dcq], [mla_q_norm],
                                 [(q_lora, BF16)], [q_lora], name="q_norm_bwd")
    dxn2 = _mm(dcq_pre, w["mla_w_dq"], mode="nt", name="q_down_bwd_x")
    g["mla_w_dq"] = _mm(xn2, dcq_pre, mode="tn", name="q_down_bwd_w")

    dc_kv = _mm(dkn, w["kv_w_uk"], mode="nt", name="kv_up_k_bwd_x")
    dc_kv = _mm(dvv, w["kv_w_uv"], mode="nt", add=dc_kv, name="kv_up_v_bwd_x")
    g["kv_w_uk"] = _mm(c_kv, dkn, mode="tn", name="kv_up_k_bwd_w")
    g["kv_w_uv"] = _mm(c_kv, dvv, mode="tn", name="kv_up_v_bwd_w")

    def kv_latent_bwd(c_all, dc, dkr_heads, tc, ts1, ts2, gk):
        dlat, dgk = _rms_bwd(c_all[:, :kv_lora], gk, dc)
        dkr_slab = dkr_heads[:, :hd]
        for h in range(1, n_heads):
            dkr_slab = dkr_slab + dkr_heads[:, h * hd:(h + 1) * hd]
        return jnp.concatenate([dlat, _rope_t(dkr_slab, tc, ts1, ts2)], axis=1), dgk

    dckr, g_kv_norm = _rowcall(kv_latent_bwd, [ckr, dc_kv, dkr, t_c, t_s1, t_s2], [kv_norm[None, :]],
                               [(kv_lora + hd, BF16)], [kv_lora], name="kv_latent_bwd")
    dhn = _mm(dckr, w_dkv_pad, mode="nt", name="kv_down_bwd_x")
    g["kv_w_dkv"] = _mm(hn, dckr, mode="tn", name="kv_down_bwd_w")[:, :kv_w_dkv.shape[1]]

    def kv_mla_norm_bwd(a, d1, d2, dres, g1, g2):
        dx1, dw1 = _rms_bwd(a, g1, d1)
        dx2, dw2 = _rms_bwd(a, g2, d2)
        return dx1 + dx2 + dres, dw1, dw2

    dh2, g_kv_in_norm, g_mla_norm = _rowcall(kv_mla_norm_bwd, [h2, dhn, dxn2, dh3], [kv_in_norm[None, :], mla_norm],
                                             [(d_model, F32)], [d_model, d_model], name="kv_mla_norm_bwd")
    dh1, g_mlp_norm0, g["mlp_w_up0"], g["mlp_w_down0"] = _mlp_bwd(
        dh2, mlp0_saved, mlp_norm[0:1], w["mlp_w_up0"], w["mlp_w_down0"], "mlp0", after=scatter_start("mla"))

    dmixed = _mm(dh1, w["hgrn_w_o"], mode="nt", after=scatter_start("mlp0"), name="hgrn_out_bwd_x")
    g["hgrn_w_o"] = _mm(mixed, dh1, mode="tn", name="hgrn_out_bwd_w")
    do_rec, dzg, g_g_norm = _rowcall(_head_norm_gate_bwd, [o_rec, zg, dmixed], [hgrn_g_norm],
                                     [(d_model, F32), (d_model, BF16)], [hd], name="hgrn_gate_bwd")
    dzq, dzf, dzi, g_lb = _hgrn_bwd(zq, zf, zi, lb_logits_full, states, do_rec, name="hgrn_bwd")
    dxn0 = None
    for nm, dz in (("hgrn_w_q", dzq), ("hgrn_w_f", dzf), ("hgrn_w_i", dzi), ("hgrn_w_g", dzg)):
        dxn0 = _mm(dz, w[nm], mode="nt", add=dxn0, name=f"{nm}_bwd_x")
        g[nm] = _mm(xn0, dz, mode="tn", name=f"{nm}_bwd_w")

    def in_norm_bwd(a, dy, dres, gw):
        dx, dw = _rms_bwd(a, gw, dy)
        return dx + dres, dw

    grad_x, g_hgrn_norm = _rowcall(in_norm_bwd, [xs, dxn0, dh1], [hgrn_norm_full], [(d_model, F32)], [d_model],
                                   name="hgrn_norm_bwd")

    last = scatter_start("hgrn")
    small_parts = [g_hgrn_norm, g_lb, g_g_norm, g_mla_norm, g_q_norm, g_kv_in_norm, g_kv_norm, g_mlp_norm0,
                   g_mlp_norm1, g_final_norm, loss_part]
    small_sizes = [p.shape[1] for p in small_parts]
    small_terms = _exchange([jnp.concatenate(small_parts, axis=1)], scatter=False, name="gather_small")[0]
    small_sum = _sum_terms(small_terms, name="sum_small")
    offs = [0]
    for sz in small_sizes:
        offs.append(offs[-1] + sz)
    (s_hgrn_norm, s_lb, s_g_norm, s_mla_norm, s_q_norm, s_kv_in_norm, s_kv_norm, s_mlp_norm0, s_mlp_norm1, s_final_norm,
     s_loss) = [small_sum[:, a:b] for a, b in zip(offs[:-1], offs[1:])]
    shard = hgrn_norm.shape[1]
    g_lb_logits = _lb_logits_grad(lax.dynamic_slice_in_dim(s_lb, me * shard, shard, axis=1), hgrn_lb_logits,
                                  name="lb_logits_grad")
    loss = s_loss[0, 0]

    res = {}
    for tag, names in groups.items():
        for k, t in zip(names, _exchange_wait(scatter_state[tag], last, name=f"scatter_{tag}_wait")):
            if k.startswith("mlp_w_"):
                base, layer = k[:-1], int(k[-1])
                wk, mk, vk = given[base][layer], given["m_" + base][layer], given["v_" + base][layer]
            else:
                wk, mk, vk = given[k], given["m_" + k], given["v_" + k]
            shape = wk.shape
            wk, mk, vk = (a.reshape(shape[-2], shape[-1]) for a in (wk, mk, vk))
            upd = _adam(wk, t, mk, vk, name=f"adam_{k}")
            last = upd[0]
            res[k] = [o.reshape(shape) for o in upd]
    for base in ("mlp_w_up", "mlp_w_down"):
        res[base] = [jnp.stack([res[base + "0"][i], res[base + "1"][i]], axis=0) for i in range(4)]

    small_grads = {
        "hgrn_norm": lax.dynamic_slice_in_dim(s_hgrn_norm, me * shard, shard, axis=1),
        "hgrn_g_norm": s_g_norm, "hgrn_lb_logits": g_lb_logits, "mla_norm": s_mla_norm, "mla_q_norm": s_q_norm,
        "kv_in_norm": s_kv_in_norm, "kv_norm": s_kv_norm,
        "mlp_norm": jnp.concatenate([s_mlp_norm0, s_mlp_norm1], axis=0), "final_norm": s_final_norm,
    }
    small_names = list(small_grads)

    def flat(a):
        return a.reshape(1, -1)

    packed = [jnp.concatenate([flat(src[pre + k]) for k in small_names], axis=1)
              for src, pre in ((given, ""), (small_grads, ""), (given, "m_"), (given, "v_"))]
    small_out = _adam(packed[0], packed[1][None], packed[2], packed[3], name="adam_small")
    off = 0
    for k in small_names:
        size = given[k].size
        res[k] = [o[:, off:off + size].reshape(given[k].shape) for o in small_out]
        off += size

    outs = [loss, grad_x[None]]
    for i in range(4):
        outs += [res[k][i] for k in weight_names]
    return tuple(outs)
```

```python
import functools

import jax
import jax.numpy as jnp
from jax import lax
from jax.experimental import pallas as pl
from jax.experimental.pallas import tpu as pltpu

F32 = jnp.float32
BF16 = jnp.bfloat16

EPS = 1e-6
LANES = 128
N_DEV = 8
V7X_VMEM_LIMIT_BYTES = 56 << 20
MM_PIPELINE_BYTES = 30 << 20
MM_ROW_TILE = 512

HGRN_HEADS = 8
HGRN_CHUNK = 64
HGRN_SUB = 16
EXP_CLAMP = 80.0
MLA_HEADS = 16
MLA_NOPE = 128
MLA_ROPE = 64
ROPE_THETA = 10000.0
ATTN_SCALE = (MLA_NOPE + MLA_ROPE) ** -0.5

ADAM_LR = 0.001
ADAM_B1 = 0.9
ADAM_B2 = 0.999
ADAM_EPS = 1e-08
ADAM_WD = 0.01
ADAM_STEP = 10

_NN = ((1,), (0,))
_NT = ((1,), (1,))
_TN = ((0,), (0,))


def _params(*sem):
    return pltpu.CompilerParams(dimension_semantics=sem, vmem_limit_bytes=V7X_VMEM_LIMIT_BYTES)


def _dot(a, b, dims):
    return lax.dot_general(a.astype(BF16), b.astype(BF16), (dims, ((), ())), preferred_element_type=F32)


def _dot_f32(a, b, dims=_NN):
    return lax.dot_general(a, b, (dims, ((), ())), precision=lax.Precision.HIGHEST, preferred_element_type=F32)


def _sigmoid(x):
    return 1.0 / (1.0 + jnp.exp(-x))


def _rms(x, w):
    r = lax.rsqrt(jnp.mean(x * x, axis=-1, keepdims=True) + EPS)
    return x * r * w


def _rms_bwd(x, w, dy):
    r = lax.rsqrt(jnp.mean(x * x, axis=-1, keepdims=True) + EPS)
    xh = x * r
    dw = jnp.sum(dy * xh, axis=0, keepdims=True)
    dxh = dy * w
    dx = r * (dxh - xh * jnp.mean(dxh * xh, axis=-1, keepdims=True))
    return dx, dw


def _mm_tiles(m, n, k, a_bytes, b_bytes, out_tile_bytes):
    tm = min(m, MM_ROW_TILE)
    for tn in (n, 2048, 1024, 512, 256, LANES):
        if tn <= n and n % tn == 0:
            if 2 * (tm * k * a_bytes + k * tn * b_bytes + tm * tn * out_tile_bytes) <= MM_PIPELINE_BYTES:
                return tm, tn
    return tm, min(n, LANES)


def _mm(a, b, *, mode, name, out_dtype=F32, add=None, epilogue=None, aux=None, after=None, scale=None):
    if mode == "nn":
        (m, k), (k2, n) = a.shape, b.shape
    elif mode == "nt":
        (m, k), (n, k2) = a.shape, b.shape
    else:
        (k, m), (k2, n) = a.shape, b.shape
    assert k == k2, (name, a.shape, b.shape)
    tile_bytes = sum(x.dtype.itemsize for x in (add, aux) if x is not None)
    tile_bytes += 6 if epilogue == "relu2" else jnp.dtype(out_dtype).itemsize
    tm, tn = _mm_tiles(m, n, k, a.dtype.itemsize, b.dtype.itemsize, tile_bytes)
    assert m % tm == 0 and n % tn == 0, (name, m, n)
    dims = {"nn": _NN, "nt": _NT, "tn": _TN}[mode]
    a_spec = pl.BlockSpec((k, tm), lambda i, j: (0, i)) if mode == "tn" else pl.BlockSpec((tm, k), lambda i, j: (i, 0))
    b_spec = pl.BlockSpec((tn, k), lambda i, j: (j, 0)) if mode == "nt" else pl.BlockSpec((k, tn), lambda i, j: (0, j))
    o_spec = pl.BlockSpec((tm, tn), lambda i, j: (i, j))
    operands, in_specs = [a, b], [a_spec, b_spec]
    for extra in (add, aux):
        if extra is not None:
            assert extra.shape == (m, n), (name, extra.shape)
            operands.append(extra)
            in_specs.append(o_spec)
    n_in = len(operands)
    if after is not None:
        operands.append(after)
        in_specs.append(pl.BlockSpec(memory_space=pl.ANY))
    if epilogue == "relu2":
        out_shape = [jax.ShapeDtypeStruct((m, n), F32), jax.ShapeDtypeStruct((m, n), BF16)]
        out_specs = [o_spec, o_spec]
    else:
        out_shape = jax.ShapeDtypeStruct((m, n), out_dtype)
        out_specs = o_spec

    def body(*refs):
        acc = _dot(refs[0][...], refs[1][...], dims)
        extras, outs = refs[2:n_in], refs[len(operands):]
        if scale is not None:
            acc = acc * scale
        if add is not None:
            acc = acc + extras[0][...]
        if epilogue == "relu2":
            outs[0][...] = acc
            outs[1][...] = jnp.square(jnp.maximum(acc, 0.0)).astype(BF16)
        elif epilogue == "relu2_bwd":
            outs[0][...] = (acc * (2.0 * jnp.maximum(extras[-1][...], 0.0))).astype(out_dtype)
        else:
            outs[0][...] = acc.astype(out_dtype)

    return pl.pallas_call(
        body, name=name, grid=(m // tm, n // tn), in_specs=in_specs, out_specs=out_specs, out_shape=out_shape,
        compiler_params=_params("parallel", "parallel"),
    )(*operands)


def _rowcall(fn, rows, consts, outs, accs, *, name, tr=256):
    s = rows[0].shape[0]
    tr = min(tr, s)
    assert s % tr == 0
    n_out = len(outs)
    in_specs = [pl.BlockSpec((tr, r.shape[1]), lambda i: (i, 0)) for r in rows]
    in_specs += [pl.BlockSpec(c.shape, lambda i: (0, 0)) for c in consts]
    out_shape = [jax.ShapeDtypeStruct((s, w), dt) for w, dt in outs] + [jax.ShapeDtypeStruct((1, w), F32) for w in accs]
    out_specs = [pl.BlockSpec((tr, w), lambda i: (i, 0)) for w, _ in outs] + [pl.BlockSpec((1, w), lambda i: (0, 0)) for w in accs]
    n_in = len(rows) + len(consts)

    def body(*refs):
        res = fn(*[r[...] for r in refs[:n_in]])
        out_refs = refs[n_in:]
        for ref, val in zip(out_refs[:n_out], res[:n_out]):
            ref[...] = val.astype(ref.dtype)
        i = pl.program_id(0)
        for ref, val in zip(out_refs[n_out:], res[n_out:]):
            @pl.when(i == 0)
            def _(ref=ref, val=val):
                ref[...] = val

            @pl.when(i > 0)
            def _(ref=ref, val=val):
                ref[...] += val

    return pl.pallas_call(
        body, name=name, grid=(s // tr,), in_specs=in_specs, out_specs=out_specs, out_shape=out_shape,
        compiler_params=_params("arbitrary" if accs else "parallel"),
    )(*rows, *consts)


def _rope_tables(seq):
    half = MLA_ROPE // 2
    inv_freq = ROPE_THETA ** (-jnp.arange(half, dtype=F32) / half)
    ang = jnp.arange(seq, dtype=F32)[:, None] * inv_freq[None, :]
    cos, sin, zero = jnp.cos(ang), jnp.sin(ang), jnp.zeros((seq, half), F32)
    t_c = jnp.concatenate([cos, cos, zero, zero], axis=1)
    t_s1 = jnp.concatenate([-sin, zero, zero, zero], axis=1)
    t_s2 = jnp.concatenate([zero, sin, zero, zero], axis=1)
    return t_c, t_s1, t_s2


def _rope(slab, t_c, t_s1, t_s2):
    return slab * t_c + pltpu.roll(slab, 96, 1) * t_s1 + pltpu.roll(slab, 32, 1) * t_s2


def _rope_t(d, t_c, t_s1, t_s2):
    return d * t_c + pltpu.roll(d * t_s1, 32, 1) + pltpu.roll(d * t_s2, 96, 1)


def _lower_bound(logits):
    l0, l1 = logits[0:1, :], logits[1:2, :]
    mx = jnp.maximum(l0, l1)
    e0, e1 = jnp.exp(l0 - mx), jnp.exp(l1 - mx)
    return e0 / (e0 + e1)


def _tri(n, lower):
    row = lax.broadcasted_iota(jnp.int32, (n, n), 0)
    col = lax.broadcasted_iota(jnp.int32, (n, n), 1)
    return (row >= col) if lower else (row <= col)


def _hgrn_intra(q, k, b, b_sc):
    c = HGRN_CHUNK
    b_sc[...] = b
    qts, decs, scores = [], [], []
    for i in range(c // HGRN_SUB):
        lo = i * HGRN_SUB
        ref = b_sc[lo - 1:lo, :] if i > 0 else jnp.zeros((1, LANES), F32)
        qt = q[lo:lo + HGRN_SUB, :] * jnp.exp(b[lo:lo + HGRN_SUB, :] - ref)
        dec = jnp.exp(jnp.minimum(ref - b, EXP_CLAMP))
        qts.append(qt)
        decs.append(dec)
        scores.append(_dot(qt, k * dec, _NT))
    a = jnp.where(_tri(c, True), jnp.concatenate(scores, axis=0), 0.0)
    return a, qts, decs


def _hgrn_fwd(zq, zf, zi, lb_logits, *, name):
    s, d = zq.shape
    h_n, c = d // LANES, HGRN_CHUNK
    nc = s // c

    def body(zq_ref, zf_ref, zi_ref, lb_ref, o_ref, st_ref, state_sc, b_sc):
        @pl.when(pl.program_id(1) == 0)
        def _():
            state_sc[...] = jnp.zeros_like(state_sc)

        lb = _lower_bound(lb_ref[...])
        zq_v = zq_ref[...]
        q = zq_v * _sigmoid(zq_v)
        f = lb + (1.0 - lb) * _sigmoid(zf_ref[...])
        g = jnp.log(f)
        k = 1.0 - f
        v = zi_ref[...]
        b = _dot_f32(_tri(c, True).astype(F32), g)
        s0t = state_sc[...]
        st_ref[...] = s0t
        a, _, _ = _hgrn_intra(q, k, b, b_sc)
        o_ref[...] = _dot(q * jnp.exp(b), s0t, _NT) + _dot(a, v, _NN)
        bl = b_sc[c - 1:c, :]
        state_sc[...] = s0t * jnp.exp(bl) + _dot(v, k * jnp.exp(bl - b), _TN)

    tile = pl.BlockSpec((c, LANES), lambda h, i: (i, h))
    return pl.pallas_call(
        body, name=name, grid=(h_n, nc),
        in_specs=[tile, tile, tile, pl.BlockSpec((2, LANES), lambda h, i: (0, h))],
        out_specs=[tile, pl.BlockSpec((None, None, LANES, LANES), lambda h, i: (h, i, 0, 0))],
        out_shape=[jax.ShapeDtypeStruct((s, d), F32), jax.ShapeDtypeStruct((h_n, nc, LANES, LANES), F32)],
        scratch_shapes=[pltpu.VMEM((LANES, LANES), F32), pltpu.VMEM((c, LANES), F32)],
        compiler_params=_params("parallel", "arbitrary"),
    )(zq, zf, zi, lb_logits)


def _hgrn_bwd(zq, zf, zi, lb_logits, states, do, *, name):
    s, d = zq.shape
    h_n, c = d // LANES, HGRN_CHUNK
    nc = s // c

    def body(zq_ref, zf_ref, zi_ref, lb_ref, st_ref, do_ref, dzq_ref, dzf_ref, dzi_ref, dlb_ref, dstate_sc, b_sc):
        @pl.when(pl.program_id(1) == 0)
        def _():
            dstate_sc[...] = jnp.zeros_like(dstate_sc)
            dlb_ref[...] = jnp.zeros_like(dlb_ref)

        lb = _lower_bound(lb_ref[...])
        zq_v = zq_ref[...]
        sq = _sigmoid(zq_v)
        q = zq_v * sq
        sf = _sigmoid(zf_ref[...])
        f = lb + (1.0 - lb) * sf
        g = jnp.log(f)
        k = 1.0 - f
        v = zi_ref[...]
        d_o = do_ref[...]
        b = _dot_f32(_tri(c, True).astype(F32), g)
        s0t = st_ref[...]
        ds1t = dstate_sc[...]
        a, qts, decs = _hgrn_intra(q, k, b, b_sc)
        bl = b_sc[c - 1:c, :]
        eb, ebl, dec_end = jnp.exp(b), jnp.exp(bl), jnp.exp(bl - b)
        qe = q * eb
        da = jnp.where(_tri(c, True), _dot(d_o, v, _NT), 0.0)
        dv = _dot(a, d_o, _TN) + _dot(k * dec_end, ds1t, _NT)
        dk_state = _dot(v, ds1t, _NN) * dec_end
        dk = dk_state
        dq_blocks = []
        for i in range(c // HGRN_SUB):
            lo = i * HGRN_SUB
            ref = b_sc[lo - 1:lo, :] if i > 0 else jnp.zeros((1, LANES), F32)
            da_i = da[lo:lo + HGRN_SUB, :]
            dq_blocks.append(_dot_f32(da_i, k * decs[i], _NN) * jnp.exp(b[lo:lo + HGRN_SUB, :] - ref))
            dk = dk + _dot_f32(da_i, qts[i], _TN) * decs[i]
        dq = _dot(d_o, s0t, _NN) * eb + jnp.concatenate(dq_blocks, axis=0)
        db_last = jnp.sum(k * dk_state, axis=0, keepdims=True) + ebl * jnp.sum(s0t * ds1t, axis=0, keepdims=True)
        last_row = lax.broadcasted_iota(jnp.int32, (c, LANES), 0) == c - 1
        db = q * dq - k * dk + jnp.where(last_row, db_last, 0.0)
        dg = _dot_f32(_tri(c, False).astype(F32), db)
        df = dg / f - dk
        dzf_ref[...] = (df * (1.0 - lb) * sf * (1.0 - sf)).astype(BF16)
        dlb_ref[...] += jnp.sum(df * (1.0 - sf), axis=0, keepdims=True)
        dzq_ref[...] = (dq * sq * (1.0 + zq_v * (1.0 - sq))).astype(BF16)
        dzi_ref[...] = dv.astype(BF16)
        dstate_sc[...] = ds1t * ebl + _dot(d_o, qe, _TN)

    tile = pl.BlockSpec((c, LANES), lambda h, i: (nc - 1 - i, h))
    out = jax.ShapeDtypeStruct((s, d), BF16)
    return pl.pallas_call(
        body, name=name, grid=(h_n, nc),
        in_specs=[tile, tile, tile, pl.BlockSpec((2, LANES), lambda h, i: (0, h)),
                  pl.BlockSpec((None, None, LANES, LANES), lambda h, i: (h, nc - 1 - i, 0, 0)), tile],
        out_specs=[tile, tile, tile, pl.BlockSpec((1, LANES), lambda h, i: (0, h))],
        out_shape=[out, out, out, jax.ShapeDtypeStruct((1, d), F32)],
        scratch_shapes=[pltpu.VMEM((LANES, LANES), F32), pltpu.VMEM((c, LANES), F32)],
        compiler_params=_params("parallel", "arbitrary"),
    )(zq, zf, zi, lb_logits, states, do)


LOG2E = 1.4426950408889634
LN2 = 0.6931471805599453
Q_PRESCALE = ATTN_SCALE * LOG2E


def _attn_tile(s):
    return min(1024, max(128, s // 2))


def _causal_pairs(n, q_major):
    pairs = [(i, j) for i in range(n) for j in range(i + 1)] if q_major else [(i, j) for j in range(n) for i in range(j, n)]
    return jnp.asarray([p[0] for p in pairs], jnp.int32), jnp.asarray([p[1] for p in pairs], jnp.int32)


def _scores(qn_ref, qr_ref, kn_ref, kr_ref):
    q = jnp.concatenate([qn_ref[...], qr_ref[...]], axis=1)
    k = jnp.concatenate([kn_ref[...], kr_ref[...]], axis=1)
    return q, k, _dot(q, k, _NT)


def _attn_fwd(qn, qr, kn, kr, v, *, name):
    s, t = qn.shape[0], _attn_tile(qn.shape[0])
    q_blk, k_blk = _causal_pairs(s // t, True)

    def body(qi_ref, kj_ref, qn_ref, qr_ref, kn_ref, kr_ref, v_ref, o_ref, lse_ref, m_sc, l_sc, acc_sc):
        p_id = pl.program_id(1)
        i, j = qi_ref[p_id], kj_ref[p_id]

        @pl.when(j == 0)
        def _():
            m_sc[...] = jnp.full_like(m_sc, -jnp.inf)
            l_sc[...] = jnp.zeros_like(l_sc)
            acc_sc[...] = jnp.zeros_like(acc_sc)

        def update(sc):
            m_prev = m_sc[...]
            m_new = jnp.maximum(m_prev, jnp.max(sc, axis=1, keepdims=True))
            alpha = jnp.exp2(m_prev - m_new)
            p = jnp.exp2(sc - m_new[:, :1])
            l_sc[...] = alpha * l_sc[...] + jnp.sum(p, axis=1, keepdims=True)
            acc_sc[...] = alpha * acc_sc[...] + _dot(p, v_ref[...], _NN)
            m_sc[...] = m_new

        @pl.when(j < i)
        def _():
            update(_scores(qn_ref, qr_ref, kn_ref, kr_ref)[2])

        @pl.when(j == i)
        def _():
            update(jnp.where(_tri(t, True), _scores(qn_ref, qr_ref, kn_ref, kr_ref)[2], -jnp.inf))
            o_ref[...] = (acc_sc[...] / l_sc[...]).astype(BF16)
            lse_ref[...] = m_sc[...] + jnp.log(l_sc[...]) * LOG2E

    q_spec = pl.BlockSpec((t, LANES), lambda h, p, qi, kj: (qi[p], h))
    k_spec = pl.BlockSpec((t, LANES), lambda h, p, qi, kj: (kj[p], h))
    kr_spec = pl.BlockSpec((t, LANES), lambda h, p, qi, kj: (kj[p], 0))
    stat = pltpu.VMEM((t, LANES), F32)
    return pl.pallas_call(
        body, name=name,
        grid_spec=pltpu.PrefetchScalarGridSpec(
            num_scalar_prefetch=2, grid=(MLA_HEADS, q_blk.shape[0]),
            in_specs=[q_spec, q_spec, k_spec, kr_spec, k_spec], out_specs=[q_spec, q_spec],
            scratch_shapes=[stat, stat, stat]),
        out_shape=[jax.ShapeDtypeStruct(qn.shape, BF16), jax.ShapeDtypeStruct(qn.shape, F32)],
        compiler_params=_params("parallel", "arbitrary"),
    )(q_blk, k_blk, qn, qr, kn, kr, v)


def _attn_bwd(qn, qr, kn, kr, v, do, lse, delta, *, name):
    s, t = qn.shape[0], _attn_tile(qn.shape[0])
    n = s // t
    q_blk, k_blk = _causal_pairs(n, False)

    def body(qi_ref, kj_ref, qn_ref, qr_ref, kn_ref, kr_ref, v_ref, do_ref, lse_ref, delta_ref,
             dqn_ref, dqr_ref, dkn_ref, dv_ref, dkr_ref, dk_sc, dv_sc):
        p_id = pl.program_id(1)
        i, j = qi_ref[p_id], kj_ref[p_id]

        @pl.when(p_id == 0)
        def _():
            dqn_ref[...] = jnp.zeros_like(dqn_ref)
            dqr_ref[...] = jnp.zeros_like(dqr_ref)

        @pl.when(i == j)
        def _():
            dk_sc[...] = jnp.zeros_like(dk_sc)
            dv_sc[...] = jnp.zeros_like(dv_sc)

        def accumulate(q, k, sc):
            p = jnp.exp2(sc - lse_ref[...][:, :1])
            d_o = do_ref[...]
            ds = (p * (_dot(d_o, v_ref[...], _NT) - delta_ref[...][:, :1])).astype(BF16)
            dv_sc[...] += _dot(p, d_o, _TN)
            dk_sc[...] += _dot(ds, q, _TN)
            dq = _dot(ds, k, _NN) * ATTN_SCALE
            rows = pl.ds(pl.multiple_of(i * t, t), t)
            dqn_ref[rows, :] += dq[:, :LANES]
            dqr_ref[rows, :] += dq[:, LANES:]

        @pl.when(j < i)
        def _():
            accumulate(*_scores(qn_ref, qr_ref, kn_ref, kr_ref))

        @pl.when(j == i)
        def _():
            q, k, sc = _scores(qn_ref, qr_ref, kn_ref, kr_ref)
            accumulate(q, k, jnp.where(_tri(t, True), sc, -jnp.inf))

        @pl.when(i == n - 1)
        def _():
            dkn_ref[...] = (dk_sc[:, :LANES] * LN2).astype(BF16)
            dkr_ref[...] = dk_sc[:, LANES:] * LN2
            dv_ref[...] = dv_sc[...].astype(BF16)

    q_spec = pl.BlockSpec((t, LANES), lambda h, p, qi, kj: (qi[p], h))
    k_spec = pl.BlockSpec((t, LANES), lambda h, p, qi, kj: (kj[p], h))
    kr_spec = pl.BlockSpec((t, LANES), lambda h, p, qi, kj: (kj[p], 0))
    head_spec = pl.BlockSpec((s, LANES), lambda h, p, qi, kj: (0, h))
    f32_out, bf16_out = jax.ShapeDtypeStruct(qn.shape, F32), jax.ShapeDtypeStruct(qn.shape, BF16)
    return pl.pallas_call(
        body, name=name,
        grid_spec=pltpu.PrefetchScalarGridSpec(
            num_scalar_prefetch=2, grid=(MLA_HEADS, q_blk.shape[0]),
            in_specs=[q_spec, q_spec, k_spec, kr_spec, k_spec, q_spec, q_spec, q_spec],
            out_specs=[head_spec, head_spec, k_spec, k_spec, k_spec],
            scratch_shapes=[pltpu.VMEM((t, 2 * LANES), F32), pltpu.VMEM((t, LANES), F32)]),
        out_shape=[f32_out, f32_out, bf16_out, bf16_out, f32_out],
        compiler_params=_params("parallel", "arbitrary"),
    )(q_blk, k_blk, qn, qr, kn, kr, v, do, lse, delta)


def _exchange(arrs, *, scatter, name):
    n = len(arrs)
    out_shape = [jax.ShapeDtypeStruct(a.shape if scatter else (N_DEV, *a.shape), a.dtype) for a in arrs]

    def body(*refs):
        ins, outs = refs[:n], refs[n:2 * n]
        send_sems, recv_sems, local_sems = refs[2 * n:]
        x, y, c = lax.axis_index("x"), lax.axis_index("y"), lax.axis_index("c")
        me = 4 * x + 2 * y + c
        copies = []
        for k in range(n):
            local = pltpu.make_async_copy(ins[k].at[me] if scatter else ins[k], outs[k].at[me], local_sems.at[k])
            local.start()
            copies.append(local)
            for d in range(1, N_DEV):
                px, py, pc = (x + (d >> 2)) % 2, (y + ((d >> 1) & 1)) % 2, (c + (d & 1)) % 2
                peer = 4 * px + 2 * py + pc
                remote = pltpu.make_async_remote_copy(
                    src_ref=ins[k].at[peer] if scatter else ins[k], dst_ref=outs[k].at[me],
                    send_sem=send_sems.at[k, d - 1], recv_sem=recv_sems.at[k, d - 1],
                    device_id=(px, py, pc), device_id_type=pl.DeviceIdType.MESH)
                remote.start()
                copies.append(remote)
        for cp in copies:
            cp.wait()

    any_spec = pl.BlockSpec(memory_space=pl.ANY)
    return pl.pallas_call(
        body, name=name, in_specs=[any_spec] * n, out_specs=[any_spec] * n, out_shape=out_shape,
        scratch_shapes=[pltpu.SemaphoreType.DMA((n, N_DEV - 1)), pltpu.SemaphoreType.DMA((n, N_DEV - 1)),
                        pltpu.SemaphoreType.DMA((n,))],
    )(*arrs)


def _peers(x, y, c):
    out = []
    for d in range(1, N_DEV):
        px, py, pc = (x + (d >> 2)) % 2, (y + ((d >> 1) & 1)) % 2, (c + (d & 1)) % 2
        out.append(((px, py, pc), 4 * px + 2 * py + pc))
    return out


def _exchange_copies(ins, lands, send_sems, recv_sems, scatter):
    x, y, c = lax.axis_index("x"), lax.axis_index("y"), lax.axis_index("c")
    me = 4 * x + 2 * y + c
    local, remote = [], []
    for k in range(len(ins)):
        local.append(pltpu.make_async_copy(ins[k].at[me] if scatter else ins[k], lands[k].at[me],
                                           recv_sems.at[k * N_DEV + N_DEV - 1]))
        for d, (coords, peer) in enumerate(_peers(x, y, c)):
            remote.append(pltpu.make_async_remote_copy(
                src_ref=ins[k].at[peer] if scatter else ins[k], dst_ref=lands[k].at[me],
                send_sem=send_sems.at[k * N_DEV + d], recv_sem=recv_sems.at[k * N_DEV + d],
                device_id=coords, device_id_type=pl.DeviceIdType.MESH))
    return local, remote


def _exchange_start(arrs, *, scatter, name):
    n = len(arrs)
    hbm = pl.BlockSpec(memory_space=pltpu.HBM)
    sem = pl.BlockSpec(memory_space=pltpu.SEMAPHORE)
    lands = [lax.empty(a.shape if scatter else (N_DEV, *a.shape), a.dtype) for a in arrs]

    def body(*refs):
        ins, land_refs = refs[:n], refs[n:2 * n]
        send_sems, recv_sems, token = refs[2 * n], refs[2 * n + 1], refs[-1]
        local, remote = _exchange_copies(ins, land_refs, send_sems, recv_sems, scatter)
        for cp in local + remote:
            cp.start()
        token[...] = jnp.zeros_like(token)

    operands = [pltpu.with_memory_space_constraint(a, pltpu.HBM) for a in list(arrs) + lands]
    res = pl.pallas_call(
        body, name=name,
        out_shape=(pltpu.SemaphoreType.DMA((n * N_DEV,)), pltpu.SemaphoreType.DMA((n * N_DEV,)),
                   *[pltpu.HBM(o.shape, o.dtype) for o in operands], jax.ShapeDtypeStruct((8, LANES), F32)),
        in_specs=[hbm] * (2 * n), out_specs=(sem, sem, *[hbm] * (2 * n), pl.BlockSpec(memory_space=pltpu.VMEM)),
        input_output_aliases={i: 2 + i for i in range(2 * n)},
        compiler_params=pltpu.CompilerParams(has_side_effects=pltpu.SideEffectType.DATAFLOW_SIDE_EFFECTING),
    )(*operands)
    return (res[0], res[1], list(res[2:2 + n]), list(res[2 + n:2 + 2 * n]), scatter), res[-1]


def _exchange_wait(state, after, *, name):
    send_sems, recv_sems, ins, lands, scatter = state
    n = len(ins)
    hbm = pl.BlockSpec(memory_space=pltpu.HBM)
    sem = pl.BlockSpec(memory_space=pltpu.SEMAPHORE)

    def body(*refs):
        in_refs, land_refs = refs[:n], refs[n:2 * n]
        local, remote = _exchange_copies(in_refs, land_refs, refs[2 * n], refs[2 * n + 1], scatter)
        for cp in local:
            cp.wait()
        for cp in remote:
            cp.wait_send()
            cp.wait_recv()

    res = pl.pallas_call(
        body, name=name, out_shape=tuple(pltpu.HBM(o.shape, o.dtype) for o in ins + lands),
        in_specs=[hbm] * (2 * n) + [sem, sem, pl.BlockSpec(memory_space=pl.ANY)], out_specs=tuple([hbm] * (2 * n)),
        input_output_aliases={i: i for i in range(2 * n)},
        compiler_params=pltpu.CompilerParams(has_side_effects=pltpu.SideEffectType.DATAFLOW_SIDE_EFFECTING),
    )(*ins, *lands, send_sems, recv_sems, after)
    return list(res[n:])


def _adam(w, terms, m, v, *, name):
    r, c = w.shape
    n = terms.shape[0]
    tr = min(r, 128)
    assert r % tr == 0

    def body(w_ref, t_ref, m_ref, v_ref, g_out, d_out, m_out, v_out):
        g = t_ref[0]
        for s in range(1, n):
            g = g + t_ref[s]
        m1 = ADAM_B1 * m_ref[...] + (1.0 - ADAM_B1) * g
        v1 = ADAM_B2 * v_ref[...] + (1.0 - ADAM_B2) * jnp.square(g)
        m_hat = m1 / (1.0 - ADAM_B1 ** ADAM_STEP)
        v_hat = v1 / (1.0 - ADAM_B2 ** ADAM_STEP)
        g_out[...] = g
        d_out[...] = -ADAM_LR * (m_hat / (jnp.sqrt(v_hat) + ADAM_EPS) + ADAM_WD * w_ref[...])
        m_out[...] = m1
        v_out[...] = v1

    spec = pl.BlockSpec((tr, c), lambda i: (i, 0))
    out = jax.ShapeDtypeStruct((r, c), F32)
    return pl.pallas_call(
        body, name=name, grid=(r // tr,),
        in_specs=[spec, pl.BlockSpec((n, tr, c), lambda i: (0, i, 0)), spec, spec], out_specs=[spec] * 4,
        out_shape=[out] * 4, compiler_params=_params("parallel"),
    )(w, terms, m, v)


def _sum_terms(terms, *, name):
    n, _, p = terms.shape

    def body(t_ref, o_ref):
        acc = t_ref[0]
        for s in range(1, n):
            acc = acc + t_ref[s]
        o_ref[...] = acc

    return pl.pallas_call(body, name=name, out_shape=jax.ShapeDtypeStruct((1, p), F32))(terms)


def _lb_logits_grad(dlb, logits, *, name):
    def body(dlb_ref, l_ref, o_ref):
        lb = _lower_bound(l_ref[...])
        d0 = dlb_ref[...] * lb * (1.0 - lb)
        o_ref[...] = jnp.concatenate([d0, -d0], axis=0)

    return pl.pallas_call(body, name=name, out_shape=jax.ShapeDtypeStruct(logits.shape, F32))(dlb, logits)


def _silu_grad(z):
    sg = _sigmoid(z)
    return sg * (1.0 + z * (1.0 - sg))


def _head_norm_gate(o, zg, gn):
    outs = []
    for h in range(HGRN_HEADS):
        sl = slice(h * LANES, (h + 1) * LANES)
        zg_h = zg[:, sl]
        outs.append(_rms(o[:, sl], gn) * (zg_h * _sigmoid(zg_h)))
    return (jnp.concatenate(outs, axis=1),)


def _head_norm_gate_bwd(o, zg, dm, gn):
    do_parts, dzg_parts, dgn = [], [], jnp.zeros((1, LANES), F32)
    for h in range(HGRN_HEADS):
        sl = slice(h * LANES, (h + 1) * LANES)
        o_h, zg_h, dm_h = o[:, sl], zg[:, sl], dm[:, sl]
        gate = zg_h * _sigmoid(zg_h)
        do_h, dgn_h = _rms_bwd(o_h, gn, dm_h * gate)
        dgn = dgn + dgn_h
        do_parts.append(do_h)
        dzg_parts.append(dm_h * _rms(o_h, gn) * _silu_grad(zg_h))
    return jnp.concatenate(do_parts, axis=1), jnp.concatenate(dzg_parts, axis=1), dgn


def _rope_slabs(x, t_c, t_s1, t_s2, transpose):
    fn = _rope_t if transpose else _rope
    return jnp.concatenate(
        [fn(x[:, h * LANES:(h + 1) * LANES], t_c, t_s1, t_s2) for h in range(x.shape[1] // LANES)], axis=1)


def _loss_head(h, tgt, w):
    d = h.shape[1]
    r = lax.rsqrt(jnp.mean(h * h, axis=-1, keepdims=True) + EPS)
    xh = h * r
    err = xh * w - tgt
    loss = 0.5 * jnp.sum(jnp.mean(err * err, axis=-1, keepdims=True), axis=0, keepdims=True)
    dy = err / d
    dxh = dy * w
    dh = r * (dxh - xh * jnp.mean(dxh * xh, axis=-1, keepdims=True))
    return dh, jnp.sum(dy * xh, axis=0, keepdims=True), jnp.broadcast_to(loss, (1, LANES))


def _mlp_fwd(h, norm, w_up, w_down, tag):
    d = h.shape[1]
    xn = _rowcall(lambda x, w: (_rms(x, w),), [h], [norm], [(d, BF16)], [], name=f"{tag}_norm")[0]
    u, act = _mm(xn, w_up, mode="nn", epilogue="relu2", name=f"{tag}_up")
    return _mm(act, w_down, mode="nn", add=h, name=f"{tag}_down"), (h, xn, u, act)


def _mlp_bwd(dh_out, saved, norm, w_up, w_down, tag, after=None):
    h, xn, u, act = saved
    d = h.shape[1]
    du = _mm(dh_out, w_down, mode="nt", epilogue="relu2_bwd", aux=u, out_dtype=BF16, after=after,
             name=f"{tag}_bwd_du")
    dw_down = _mm(act, dh_out, mode="tn", name=f"{tag}_bwd_wdown")
    dxn = _mm(du, w_up, mode="nt", name=f"{tag}_bwd_dxn")
    dw_up = _mm(xn, du, mode="tn", name=f"{tag}_bwd_wup")

    def norm_bwd(x, dy, dres, w):
        dx, dw = _rms_bwd(x, w, dy)
        return dx + dres, dw

    dh, dnorm = _rowcall(norm_bwd, [h, dxn, dh_out], [norm], [(d, F32)], [d], name=f"{tag}_bwd_norm")
    return dh, dnorm, dw_up, dw_down


def _row_major(g):
    return g.reshape(g.shape[0] * g.shape[1], g.shape[2])


def _col_major(g):
    return jnp.transpose(g, (1, 0, 2)).reshape(g.shape[1], g.shape[0] * g.shape[2])


def _col_terms(dw):
    k, n = dw.shape
    return jnp.transpose(dw.reshape(k, N_DEV, n // N_DEV), (1, 0, 2))


def _row_terms(dw):
    return dw.reshape(N_DEV, dw.shape[0] // N_DEV, dw.shape[1])


def kernel(x, hgrn_norm, hgrn_w_q, hgrn_w_f, hgrn_w_i, hgrn_w_g, hgrn_g_norm, hgrn_w_o, hgrn_lb_logits, mla_norm, mla_w_dq, mla_q_norm, mla_w_uq, mla_w_o, kv_in_norm, kv_w_dkv, kv_norm, kv_w_uk, kv_w_uv, mlp_norm, mlp_w_up, mlp_w_down, final_norm, loss_target, m_hgrn_norm, m_hgrn_w_q, m_hgrn_w_f, m_hgrn_w_i, m_hgrn_w_g, m_hgrn_g_norm, m_hgrn_w_o, m_hgrn_lb_logits, m_mla_norm, m_mla_w_dq, m_mla_q_norm, m_mla_w_uq, m_mla_w_o, m_kv_in_norm, m_kv_w_dkv, m_kv_norm, m_kv_w_uk, m_kv_w_uv, m_mlp_norm, m_mlp_w_up, m_mlp_w_down, m_final_norm, v_hgrn_norm, v_hgrn_w_q, v_hgrn_w_f, v_hgrn_w_i, v_hgrn_w_g, v_hgrn_g_norm, v_hgrn_w_o, v_hgrn_lb_logits, v_mla_norm, v_mla_w_dq, v_mla_q_norm, v_mla_w_uq, v_mla_w_o, v_kv_in_norm, v_kv_w_dkv, v_kv_norm, v_kv_w_uk, v_kv_w_uv, v_mlp_norm, v_mlp_w_up, v_mlp_w_down, v_final_norm):
    given = dict(locals())
    weight_names = ["hgrn_norm", "hgrn_w_q", "hgrn_w_f", "hgrn_w_i", "hgrn_w_g", "hgrn_g_norm", "hgrn_w_o",
                    "hgrn_lb_logits", "mla_norm", "mla_w_dq", "mla_q_norm", "mla_w_uq", "mla_w_o", "kv_in_norm",
                    "kv_w_dkv", "kv_norm", "kv_w_uk", "kv_w_uv", "mlp_norm", "mlp_w_up", "mlp_w_down", "final_norm"]
    me = 4 * lax.axis_index("x") + 2 * lax.axis_index("y") + lax.axis_index("c")
    xs, tgt = x[0], loss_target[0]
    seq, d_model = xs.shape
    n_heads, hd = MLA_HEADS, LANES

    big_local = {
        "hgrn_w_q": hgrn_w_q[0], "hgrn_w_f": hgrn_w_f[0], "hgrn_w_i": hgrn_w_i[0], "hgrn_w_g": hgrn_w_g[0],
        "hgrn_w_o": hgrn_w_o[0], "mla_w_dq": mla_w_dq[0], "mla_w_uq": mla_w_uq[0], "mla_w_o": mla_w_o[0],
        "kv_w_dkv": kv_w_dkv, "kv_w_uk": kv_w_uk, "kv_w_uv": kv_w_uv,
        "mlp_w_up0": mlp_w_up[0], "mlp_w_up1": mlp_w_up[1], "mlp_w_down0": mlp_w_down[0], "mlp_w_down1": mlp_w_down[1],
    }
    big_names = list(big_local)
    col_sharded = {"mla_w_uq", "kv_w_uk", "kv_w_uv", "mlp_w_up0", "mlp_w_up1"}
    vec_local = jnp.concatenate([hgrn_norm, hgrn_lb_logits], axis=0)
    first_names = ["hgrn_w_q", "hgrn_w_f", "hgrn_w_i", "hgrn_w_g"]
    later_names = [k for k in big_names if k not in first_names]

    def unshard(names, arrays):
        return {k: (_col_major(a) if k in col_sharded else _row_major(a)) for k, a in zip(names, arrays)}

    gathered = _exchange([big_local[k].astype(BF16) for k in first_names] + [vec_local], scatter=False, name="gather_first")
    gather_state, token = _exchange_start([big_local[k].astype(BF16) for k in later_names], scatter=False,
                                          name="gather_rest_start")
    w = unshard(first_names, gathered[:-1])
    vec_full = jnp.transpose(gathered[-1], (1, 0, 2)).reshape(3, d_model)
    hgrn_norm_full, lb_logits_full = vec_full[0:1], vec_full[1:3]
    t_c, t_s1, t_s2 = _rope_tables(seq)
    kv_lora = kv_w_uk.shape[0]

    xn0 = _rowcall(lambda a, g: (_rms(a, g),), [xs], [hgrn_norm_full], [(d_model, BF16)], [], name="hgrn_norm")[0]
    zq = _mm(xn0, w["hgrn_w_q"], mode="nn", after=token, name="hgrn_zq")
    zf = _mm(xn0, w["hgrn_w_f"], mode="nn", name="hgrn_zf")
    zi = _mm(xn0, w["hgrn_w_i"], mode="nn", name="hgrn_zi")
    zg = _mm(xn0, w["hgrn_w_g"], mode="nn", name="hgrn_zg")
    o_rec, states = _hgrn_fwd(zq, zf, zi, lb_logits_full, name="hgrn_fwd")
    mixed = _rowcall(_head_norm_gate, [o_rec, zg], [hgrn_g_norm], [(d_model, BF16)], [], name="hgrn_gate")[0]
    w.update(unshard(later_names, _exchange_wait(gather_state, mixed, name="gather_rest_wait")))
    w_uq3 = w["mla_w_uq"].reshape(-1, n_heads, MLA_NOPE + MLA_ROPE)
    w_uq_nope = w_uq3[:, :, :MLA_NOPE].reshape(-1, n_heads * hd)
    w_uq_rope = jnp.pad(w_uq3[:, :, MLA_NOPE:], ((0, 0), (0, 0), (0, hd - MLA_ROPE))).reshape(-1, n_heads * hd)
    w_dkv_pad = jnp.pad(w["kv_w_dkv"], ((0, 0), (0, kv_lora + hd - w["kv_w_dkv"].shape[1])))
    h1 = _mm(mixed, w["hgrn_w_o"], mode="nn", add=xs, name="hgrn_out")
    h2, mlp0_saved = _mlp_fwd(h1, mlp_norm[0:1], w["mlp_w_up0"], w["mlp_w_down0"], "mlp0")

    hn, xn2 = _rowcall(lambda a, g1, g2: (_rms(a, g1), _rms(a, g2)), [h2], [kv_in_norm[None, :], mla_norm],
                       [(d_model, BF16), (d_model, BF16)], [], name="kv_mla_norm")
    ckr = _mm(hn, w_dkv_pad, mode="nn", name="kv_down")

    def kv_latent(c_all, tc, ts1, ts2, g):
        return _rms(c_all[:, :kv_lora], g), _rope(c_all[:, kv_lora:], tc, ts1, ts2)

    c_kv, kr = _rowcall(kv_latent, [ckr, t_c, t_s1, t_s2], [kv_norm[None, :]], [(kv_lora, BF16), (hd, BF16)], [],
                        name="kv_latent")
    kn = _mm(c_kv, w["kv_w_uk"], mode="nn", out_dtype=BF16, name="kv_up_k")
    vv = _mm(c_kv, w["kv_w_uv"], mode="nn", out_dtype=BF16, name="kv_up_v")
    cq_pre = _mm(xn2, w["mla_w_dq"], mode="nn", name="q_down")
    c_q = _rowcall(lambda a, g: (_rms(a, g),), [cq_pre], [mla_q_norm], [(cq_pre.shape[1], BF16)], [], name="q_norm")[0]
    qn = _mm(c_q, w_uq_nope, mode="nn", out_dtype=BF16, scale=Q_PRESCALE, name="q_up_nope")
    qr_pre = _mm(c_q, w_uq_rope, mode="nn", name="q_up_rope")
    qr = _rowcall(lambda a, tc, ts1, ts2: (_rope_slabs(a, tc, ts1, ts2, False) * Q_PRESCALE,), [qr_pre, t_c, t_s1, t_s2], [],
                  [(n_heads * hd, BF16)], [], name="q_rope")[0]
    o_att, lse = _attn_fwd(qn, qr, kn, kr, vv, name="attn_fwd")
    h3 = _mm(o_att, w["mla_w_o"], mode="nn", add=h2, name="attn_out")
    h4, mlp1_saved = _mlp_fwd(h3, mlp_norm[1:2], w["mlp_w_up1"], w["mlp_w_down1"], "mlp1")
    dh4, g_final_norm, loss_part = _rowcall(_loss_head, [h4, tgt], [final_norm[None, :]], [(d_model, F32)],
                                            [d_model, LANES], name="loss_head")

    g = {}
    groups = {"mlp1": ["mlp_w_up1", "mlp_w_down1"],
              "mla": ["mla_w_o", "mla_w_uq", "mla_w_dq", "kv_w_uk", "kv_w_uv", "kv_w_dkv"],
              "mlp0": ["mlp_w_up0", "mlp_w_down0"],
              "hgrn": ["hgrn_w_o", "hgrn_w_q", "hgrn_w_f", "hgrn_w_i", "hgrn_w_g"]}
    scatter_state = {}

    def scatter_start(tag):
        scatter_state[tag], tok = _exchange_start(
            [(_col_terms if k in col_sharded else _row_terms)(g[k]) for k in groups[tag]], scatter=True,
            name=f"scatter_{tag}_start")
        return tok

    dh3, g_mlp_norm1, g["mlp_w_up1"], g["mlp_w_down1"] = _mlp_bwd(
        dh4, mlp1_saved, mlp_norm[1:2], w["mlp_w_up1"], w["mlp_w_down1"], "mlp1")
    d_oatt = _mm(dh3, w["mla_w_o"], mode="nt", out_dtype=BF16, after=scatter_start("mlp1"), name="attn_out_bwd_x")
    g["mla_w_o"] = _mm(o_att, dh3, mode="tn", name="attn_out_bwd_w")

    def head_delta(do, o):
        prod = do.astype(F32) * o.astype(F32)
        return (jnp.concatenate([jnp.broadcast_to(jnp.sum(prod[:, h * hd:(h + 1) * hd], axis=1, keepdims=True),
                                                  (prod.shape[0], hd)) for h in range(n_heads)], axis=1),)

    delta = _rowcall(head_delta, [d_oatt, o_att], [], [(n_heads * hd, F32)], [], name="attn_delta")[0]
    dqn, dqr, dkn, dvv, dkr = _attn_bwd(qn, qr, kn, kr, vv, d_oatt, lse, delta, name="attn_bwd")
    dqr_pre = _rowcall(lambda a, tc, ts1, ts2: (_rope_slabs(a, tc, ts1, ts2, True),), [dqr, t_c, t_s1, t_s2], [],
                       [(n_heads * hd, BF16)], [], name="q_rope_bwd")[0]
    dcq = _mm(dqn, w_uq_nope, mode="nt", name="q_up_nope_bwd_x")
    dcq = _mm(dqr_pre, w_uq_rope, mode="nt", add=dcq, name="q_up_rope_bwd_x")
    g_uq_nope = _mm(c_q, dqn, mode="tn", name="q_up_nope_bwd_w")
    g_uq_rope = _mm(c_q, dqr_pre, mode="tn", name="q_up_rope_bwd_w")
    q_lora = c_q.shape[1]
    g["mla_w_uq"] = jnp.concatenate([g_uq_nope.reshape(q_lora, n_heads, hd),
                                     g_uq_rope.reshape(q_lora, n_heads, hd)[:, :, :MLA_ROPE]], axis=2).reshape(q_lora, -1)
    dcq_pre, g_q_norm = _rowcall(lambda a, dy, gq: _rms_bwd(a, gq, dy), [cq_pre, dcq], [mla_q_norm],
                                 [(q_lora, BF16)], [q_lora], name="q_norm_bwd")
    dxn2 = _mm(dcq_pre, w["mla_w_dq"], mode="nt", name="q_down_bwd_x")
    g["mla_w_dq"] = _mm(xn2, dcq_pre, mode="tn", name="q_down_bwd_w")

    dc_kv = _mm(dkn, w["kv_w_uk"], mode="nt", name="kv_up_k_bwd_x")
    dc_kv = _mm(dvv, w["kv_w_uv"], mode="nt", add=dc_kv, name="kv_up_v_bwd_x")
    g["kv_w_uk"] = _mm(c_kv, dkn, mode="tn", name="kv_up_k_bwd_w")
    g["kv_w_uv"] = _mm(c_kv, dvv, mode="tn", name="kv_up_v_bwd_w")

    def kv_latent_bwd(c_all, dc, dkr_heads, tc, ts1, ts2, gk):
        dlat, dgk = _rms_bwd(c_all[:, :kv_lora], gk, dc)
        dkr_slab = dkr_heads[:, :hd]
        for h in range(1, n_heads):
            dkr_slab = dkr_slab + dkr_heads[:, h * hd:(h + 1) * hd]
        return jnp.concatenate([dlat, _rope_t(dkr_slab, tc, ts1, ts2)], axis=1), dgk

    dckr, g_kv_norm = _rowcall(kv_latent_bwd, [ckr, dc_kv, dkr, t_c, t_s1, t_s2], [kv_norm[None, :]],
                               [(kv_lora + hd, BF16)], [kv_lora], name="kv_latent_bwd")
    dhn = _mm(dckr, w_dkv_pad, mode="nt", name="kv_down_bwd_x")
    g["kv_w_dkv"] = _mm(hn, dckr, mode="tn", name="kv_down_bwd_w")[:, :kv_w_dkv.shape[1]]

    def kv_mla_norm_bwd(a, d1, d2, dres, g1, g2):
        dx1, dw1 = _rms_bwd(a, g1, d1)
        dx2, dw2 = _rms_bwd(a, g2, d2)
        return dx1 + dx2 + dres, dw1, dw2

    dh2, g_kv_in_norm, g_mla_norm = _rowcall(kv_mla_norm_bwd, [h2, dhn, dxn2, dh3], [kv_in_norm[None, :], mla_norm],
                                             [(d_model, F32)], [d_model, d_model], name="kv_mla_norm_bwd")
    dh1, g_mlp_norm0, g["mlp_w_up0"], g["mlp_w_down0"] = _mlp_bwd(
        dh2, mlp0_saved, mlp_norm[0:1], w["mlp_w_up0"], w["mlp_w_down0"], "mlp0", after=scatter_start("mla"))

    dmixed = _mm(dh1, w["hgrn_w_o"], mode="nt", after=scatter_start("mlp0"), name="hgrn_out_bwd_x")
    g["hgrn_w_o"] = _mm(mixed, dh1, mode="tn", name="hgrn_out_bwd_w")
    do_rec, dzg, g_g_norm = _rowcall(_head_norm_gate_bwd, [o_rec, zg, dmixed], [hgrn_g_norm],
                                     [(d_model, F32), (d_model, BF16)], [hd], name="hgrn_gate_bwd")
    dzq, dzf, dzi, g_lb = _hgrn_bwd(zq, zf, zi, lb_logits_full, states, do_rec, name="hgrn_bwd")
    dxn0 = None
    for nm, dz in (("hgrn_w_q", dzq), ("hgrn_w_f", dzf), ("hgrn_w_i", dzi), ("hgrn_w_g", dzg)):
        dxn0 = _mm(dz, w[nm], mode="nt", add=dxn0, name=f"{nm}_bwd_x")
        g[nm] = _mm(xn0, dz, mode="tn", name=f"{nm}_bwd_w")

    def in_norm_bwd(a, dy, dres, gw):
        dx, dw = _rms_bwd(a, gw, dy)
        return dx + dres, dw

    grad_x, g_hgrn_norm = _rowcall(in_norm_bwd, [xs, dxn0, dh1], [hgrn_norm_full], [(d_model, F32)], [d_model],
                                   name="hgrn_norm_bwd")

    last = scatter_start("hgrn")
    small_parts = [g_hgrn_norm, g_lb, g_g_norm, g_mla_norm, g_q_norm, g_kv_in_norm, g_kv_norm, g_mlp_norm0,
                   g_mlp_norm1, g_final_norm, loss_part]
    small_sizes = [p.shape[1] for p in small_parts]
    small_terms = _exchange([jnp.concatenate(small_parts, axis=1)], scatter=False, name="gather_small")[0]
    small_sum = _sum_terms(small_terms, name="sum_small")
    offs = [0]
    for sz in small_sizes:
        offs.append(offs[-1] + sz)
    (s_hgrn_norm, s_lb, s_g_norm, s_mla_norm, s_q_norm, s_kv_in_norm, s_kv_norm, s_mlp_norm0, s_mlp_norm1, s_final_norm,
     s_loss) = [small_sum[:, a:b] for a, b in zip(offs[:-1], offs[1:])]
    shard = hgrn_norm.shape[1]
    g_lb_logits = _lb_logits_grad(lax.dynamic_slice_in_dim(s_lb, me * shard, shard, axis=1), hgrn_lb_logits,
                                  name="lb_logits_grad")
    loss = s_loss[0, 0]

    res = {}
    for tag, names in groups.items():
        for k, t in zip(names, _exchange_wait(scatter_state[tag], last, name=f"scatter_{tag}_wait")):
            if k.startswith("mlp_w_"):
                base, layer = k[:-1], int(k[-1])
                wk, mk, vk = given[base][layer], given["m_" + base][layer], given["v_" + base][layer]
            else:
                wk, mk, vk = given[k], given["m_" + k], given["v_" + k]
            shape = wk.shape
            wk, mk, vk = (a.reshape(shape[-2], shape[-1]) for a in (wk, mk, vk))
            upd = _adam(wk, t, mk, vk, name=f"adam_{k}")
            last = upd[0]
            res[k] = [o.reshape(shape) for o in upd]
    for base in ("mlp_w_up", "mlp_w_down"):
        res[base] = [jnp.stack([res[base + "0"][i], res[base + "1"][i]], axis=0) for i in range(4)]

    small_grads = {
        "hgrn_norm": lax.dynamic_slice_in_dim(s_hgrn_norm, me * shard, shard, axis=1),
        "hgrn_g_norm": s_g_norm, "hgrn_lb_logits": g_lb_logits, "mla_norm": s_mla_norm, "mla_q_norm": s_q_norm,
        "kv_in_norm": s_kv_in_norm, "kv_norm": s_kv_norm,
        "mlp_norm": jnp.concatenate([s_mlp_norm0, s_mlp_norm1], axis=0), "final_norm": s_final_norm,
    }
    small_names = list(small_grads)

    def flat(a):
        return a.reshape(1, -1)

    packed = [jnp.concatenate([flat(src[pre + k]) for k in small_names], axis=1)
              for src, pre in ((given, ""), (small_grads, ""), (given, "m_"), (given, "v_"))]
    small_out = _adam(packed[0], packed[1][None], packed[2], packed[3], name="adam_small")
    off = 0
    for k in small_names:
        size = given[k].size
        res[k] = [o[:, off:off + size].reshape(given[k].shape) for o in small_out]
        off += size

    outs = [loss, grad_x[None]]
    for i in range(4):
        outs += [res[k][i] for k in weight_names]
    return tuple(outs)
```

```python
import functools

import jax
import jax.numpy as jnp
from jax import lax
from jax.experimental import pallas as pl
from jax.experimental.pallas import tpu as pltpu

F32 = jnp.float32
BF16 = jnp.bfloat16

EPS = 1e-6
LANES = 128
N_DEV = 8
V7X_VMEM_LIMIT_BYTES = 56 << 20
MM_PIPELINE_BYTES = 30 << 20
MM_ROW_TILE = 512

HGRN_HEADS = 8
HGRN_CHUNK = 64
HGRN_SUB = 16
HGRN_HEADS_PER_STEP = 8
EXP_CLAMP = 80.0
MLA_HEADS = 16
MLA_NOPE = 128
MLA_ROPE = 64
ROPE_THETA = 10000.0
ATTN_SCALE = (MLA_NOPE + MLA_ROPE) ** -0.5

ADAM_LR = 0.001
ADAM_B1 = 0.9
ADAM_B2 = 0.999
ADAM_EPS = 1e-08
ADAM_WD = 0.01
ADAM_STEP = 10

_NN = ((1,), (0,))
_NT = ((1,), (1,))
_TN = ((0,), (0,))


def _params(*sem):
    return pltpu.CompilerParams(dimension_semantics=sem, vmem_limit_bytes=V7X_VMEM_LIMIT_BYTES)


def _dot(a, b, dims):
    return lax.dot_general(a.astype(BF16), b.astype(BF16), (dims, ((), ())), preferred_element_type=F32)


def _dot_f32(a, b, dims=_NN):
    return lax.dot_general(a, b, (dims, ((), ())), precision=lax.Precision.HIGHEST, preferred_element_type=F32)


def _sigmoid(x):
    return 1.0 / (1.0 + jnp.exp(-x))


def _rms(x, w):
    r = lax.rsqrt(jnp.mean(x * x, axis=-1, keepdims=True) + EPS)
    return x * r * w


def _rms_bwd(x, w, dy):
    r = lax.rsqrt(jnp.mean(x * x, axis=-1, keepdims=True) + EPS)
    xh = x * r
    dw = jnp.sum(dy * xh, axis=0, keepdims=True)
    dxh = dy * w
    dx = r * (dxh - xh * jnp.mean(dxh * xh, axis=-1, keepdims=True))
    return dx, dw


def _mm_tiles(m, n, k, a_bytes, b_bytes, out_tile_bytes):
    tm = min(m, MM_ROW_TILE)
    for tn in (n, 2048, 1024, 512, 256, LANES):
        if tn <= n and n % tn == 0:
            if 2 * (tm * k * a_bytes + k * tn * b_bytes + tm * tn * out_tile_bytes) <= MM_PIPELINE_BYTES:
                return tm, tn
    return tm, min(n, LANES)


def _mm(a, b, *, mode, name, out_dtype=F32, add=None, epilogue=None, aux=None, after=None, scale=None):
    if mode == "nn":
        (m, k), (k2, n) = a.shape, b.shape
    elif mode == "nt":
        (m, k), (n, k2) = a.shape, b.shape
    else:
        (k, m), (k2, n) = a.shape, b.shape
    assert k == k2, (name, a.shape, b.shape)
    tile_bytes = sum(x.dtype.itemsize for x in (add, aux) if x is not None)
    tile_bytes += 6 if epilogue == "relu2" else jnp.dtype(out_dtype).itemsize
    tm, tn = _mm_tiles(m, n, k, a.dtype.itemsize, b.dtype.itemsize, tile_bytes)
    assert m % tm == 0 and n % tn == 0, (name, m, n)
    dims = {"nn": _NN, "nt": _NT, "tn": _TN}[mode]
    a_spec = pl.BlockSpec((k, tm), lambda i, j: (0, i)) if mode == "tn" else pl.BlockSpec((tm, k), lambda i, j: (i, 0))
    b_spec = pl.BlockSpec((tn, k), lambda i, j: (j, 0)) if mode == "nt" else pl.BlockSpec((k, tn), lambda i, j: (0, j))
    o_spec = pl.BlockSpec((tm, tn), lambda i, j: (i, j))
    operands, in_specs = [a, b], [a_spec, b_spec]
    for extra in (add, aux):
        if extra is not None:
            assert extra.shape == (m, n), (name, extra.shape)
            operands.append(extra)
            in_specs.append(o_spec)
    n_in = len(operands)
    if after is not None:
        operands.append(after)
        in_specs.append(pl.BlockSpec(memory_space=pl.ANY))
    if epilogue == "relu2":
        out_shape = [jax.ShapeDtypeStruct((m, n), F32), jax.ShapeDtypeStruct((m, n), BF16)]
        out_specs = [o_spec, o_spec]
    else:
        out_shape = jax.ShapeDtypeStruct((m, n), out_dtype)
        out_specs = o_spec

    def body(*refs):
        acc = _dot(refs[0][...], refs[1][...], dims)
        extras, outs = refs[2:n_in], refs[len(operands):]
        if scale is not None:
            acc = acc * scale
        if add is not None:
            acc = acc + extras[0][...]
        if epilogue == "relu2":
            outs[0][...] = acc
            outs[1][...] = jnp.square(jnp.maximum(acc, 0.0)).astype(BF16)
        elif epilogue == "relu2_bwd":
            outs[0][...] = (acc * (2.0 * jnp.maximum(extras[-1][...], 0.0))).astype(out_dtype)
        else:
            outs[0][...] = acc.astype(out_dtype)

    return pl.pallas_call(
        body, name=name, grid=(m // tm, n // tn), in_specs=in_specs, out_specs=out_specs, out_shape=out_shape,
        compiler_params=_params("parallel", "parallel"),
    )(*operands)


def _rowcall(fn, rows, consts, outs, accs, *, name, tr=256):
    s = rows[0].shape[0]
    tr = min(tr, s)
    assert s % tr == 0
    n_out = len(outs)
    in_specs = [pl.BlockSpec((tr, r.shape[1]), lambda i: (i, 0)) for r in rows]
    in_specs += [pl.BlockSpec(c.shape, lambda i: (0, 0)) for c in consts]
    out_shape = [jax.ShapeDtypeStruct((s, w), dt) for w, dt in outs] + [jax.ShapeDtypeStruct((1, w), F32) for w in accs]
    out_specs = [pl.BlockSpec((tr, w), lambda i: (i, 0)) for w, _ in outs] + [pl.BlockSpec((1, w), lambda i: (0, 0)) for w in accs]
    n_in = len(rows) + len(consts)

    def body(*refs):
        res = fn(*[r[...] for r in refs[:n_in]])
        out_refs = refs[n_in:]
        for ref, val in zip(out_refs[:n_out], res[:n_out]):
            ref[...] = val.astype(ref.dtype)
        i = pl.program_id(0)
        for ref, val in zip(out_refs[n_out:], res[n_out:]):
            @pl.when(i == 0)
            def _(ref=ref, val=val):
                ref[...] = val

            @pl.when(i > 0)
            def _(ref=ref, val=val):
                ref[...] += val

    return pl.pallas_call(
        body, name=name, grid=(s // tr,), in_specs=in_specs, out_specs=out_specs, out_shape=out_shape,
        compiler_params=_params("arbitrary" if accs else "parallel"),
    )(*rows, *consts)


def _rope_tables(seq):
    half = MLA_ROPE // 2
    inv_freq = ROPE_THETA ** (-jnp.arange(half, dtype=F32) / half)
    ang = jnp.arange(seq, dtype=F32)[:, None] * inv_freq[None, :]
    cos, sin, zero = jnp.cos(ang), jnp.sin(ang), jnp.zeros((seq, half), F32)
    t_c = jnp.concatenate([cos, cos, zero, zero], axis=1)
    t_s1 = jnp.concatenate([-sin, zero, zero, zero], axis=1)
    t_s2 = jnp.concatenate([zero, sin, zero, zero], axis=1)
    return t_c, t_s1, t_s2


def _rope(slab, t_c, t_s1, t_s2):
    return slab * t_c + pltpu.roll(slab, 96, 1) * t_s1 + pltpu.roll(slab, 32, 1) * t_s2


def _rope_t(d, t_c, t_s1, t_s2):
    return d * t_c + pltpu.roll(d * t_s1, 32, 1) + pltpu.roll(d * t_s2, 96, 1)


def _lower_bound(logits):
    l0, l1 = logits[0:1, :], logits[1:2, :]
    mx = jnp.maximum(l0, l1)
    e0, e1 = jnp.exp(l0 - mx), jnp.exp(l1 - mx)
    return e0 / (e0 + e1)


def _tri(n, lower):
    row = lax.broadcasted_iota(jnp.int32, (n, n), 0)
    col = lax.broadcasted_iota(jnp.int32, (n, n), 1)
    return (row >= col) if lower else (row <= col)


def _hgrn_fwd(zq, zf, zi, lb_logits, *, name):
    s, d = zq.shape
    h_n, c, hp = d // LANES, HGRN_CHUNK, HGRN_HEADS_PER_STEP
    nc = s // c

    def body(zq_ref, zf_ref, zi_ref, lb_ref, o_ref, st_ref, state_sc, b_sc):
        @pl.when(pl.program_id(1) == 0)
        def _():
            state_sc[...] = jnp.zeros_like(state_sc)

        lower = _tri(c, True).astype(F32)
        hs = range(hp)
        sls = [slice(hh * LANES, (hh + 1) * LANES) for hh in hs]
        lb = [_lower_bound(lb_ref[:, sl]) for sl in sls]
        zq_v = [zq_ref[:, sl] for sl in sls]
        q = [z * _sigmoid(z) for z in zq_v]
        f = [lb[hh] + (1.0 - lb[hh]) * _sigmoid(zf_ref[:, sls[hh]]) for hh in hs]
        g = [jnp.log(x) for x in f]
        k = [1.0 - x for x in f]
        v = [zi_ref[:, sl] for sl in sls]
        b = [_dot_f32(lower, x) for x in g]
        s0t = [state_sc[hh] for hh in hs]
        for hh in hs:
            st_ref[hh] = s0t[hh]
            b_sc[hh] = b[hh]
        o_inter = [_dot(q[hh] * jnp.exp(b[hh]), s0t[hh], _NT) for hh in hs]
        scores = [[] for _ in hs]
        for i in range(c // HGRN_SUB):
            lo = i * HGRN_SUB
            for hh in hs:
                ref = b_sc[hh, lo - 1:lo, :] if i > 0 else jnp.zeros((1, LANES), F32)
                qt = q[hh][lo:lo + HGRN_SUB, :] * jnp.exp(b[hh][lo:lo + HGRN_SUB, :] - ref)
                dec = jnp.exp(jnp.minimum(ref - b[hh], EXP_CLAMP))
                scores[hh].append(_dot(qt, k[hh] * dec, _NT))
        a = [jnp.where(_tri(c, True), jnp.concatenate(sc, axis=0), 0.0) for sc in scores]
        for hh in hs:
            o_ref[:, sls[hh]] = o_inter[hh] + _dot(a[hh], v[hh], _NN)
        bl = [b_sc[hh, c - 1:c, :] for hh in hs]
        for hh in hs:
            state_sc[hh] = s0t[hh] * jnp.exp(bl[hh]) + _dot(v[hh], k[hh] * jnp.exp(bl[hh] - b[hh]), _TN)

    tile = pl.BlockSpec((c, hp * LANES), lambda h, i: (i, h))
    return pl.pallas_call(
        body, name=name, grid=(h_n // hp, nc),
        in_specs=[tile, tile, tile, pl.BlockSpec((2, hp * LANES), lambda h, i: (0, h))],
        out_specs=[tile, pl.BlockSpec((hp, None, LANES, LANES), lambda h, i: (h, i, 0, 0))],
        out_shape=[jax.ShapeDtypeStruct((s, d), F32), jax.ShapeDtypeStruct((h_n, nc, LANES, LANES), F32)],
        scratch_shapes=[pltpu.VMEM((hp, LANES, LANES), F32), pltpu.VMEM((hp, c, LANES), F32)],
        compiler_params=_params("parallel", "arbitrary"),
    )(zq, zf, zi, lb_logits)


def _hgrn_bwd(zq, zf, zi, lb_logits, states, do, after, *, name):
    s, d = zq.shape
    h_n, c, hp = d // LANES, HGRN_CHUNK, HGRN_HEADS_PER_STEP
    nc = s // c

    def body(zq_ref, zf_ref, zi_ref, lb_ref, st_ref, do_ref, _, dzq_ref, dzf_ref, dzi_ref, dlb_ref, dstate_sc, b_sc):
        @pl.when(pl.program_id(1) == 0)
        def _():
            dstate_sc[...] = jnp.zeros_like(dstate_sc)
            dlb_ref[...] = jnp.zeros_like(dlb_ref)

        lower, upper = _tri(c, True), _tri(c, False).astype(F32)
        lower_f = lower.astype(F32)
        last_row = lax.broadcasted_iota(jnp.int32, (c, LANES), 0) == c - 1
        hs = range(hp)
        sls = [slice(hh * LANES, (hh + 1) * LANES) for hh in hs]
        lb = [_lower_bound(lb_ref[:, sl]) for sl in sls]
        zq_v = [zq_ref[:, sl] for sl in sls]
        sq = [_sigmoid(z) for z in zq_v]
        q = [zq_v[hh] * sq[hh] for hh in hs]
        sf = [_sigmoid(zf_ref[:, sl]) for sl in sls]
        f = [lb[hh] + (1.0 - lb[hh]) * sf[hh] for hh in hs]
        g = [jnp.log(x) for x in f]
        k = [1.0 - x for x in f]
        v = [zi_ref[:, sl] for sl in sls]
        d_o = [do_ref[:, sl] for sl in sls]
        b = [_dot_f32(lower_f, x) for x in g]
        s0t = [st_ref[hh] for hh in hs]
        ds1t = [dstate_sc[hh] for hh in hs]
        for hh in hs:
            b_sc[hh] = b[hh]
        bl = [b_sc[hh, c - 1:c, :] for hh in hs]
        eb = [jnp.exp(x) for x in b]
        ebl = [jnp.exp(x) for x in bl]
        dec_end = [jnp.exp(bl[hh] - b[hh]) for hh in hs]
        da = [jnp.where(lower, _dot(d_o[hh], v[hh], _NT), 0.0) for hh in hs]
        dq_inter = [_dot(d_o[hh], s0t[hh], _NN) * eb[hh] for hh in hs]
        dk_state = [_dot(v[hh], ds1t[hh], _NN) * dec_end[hh] for hh in hs]
        dv_state = [_dot(k[hh] * dec_end[hh], ds1t[hh], _NT) for hh in hs]
        for hh in hs:
            dstate_sc[hh] = ds1t[hh] * ebl[hh] + _dot(d_o[hh], q[hh] * eb[hh], _TN)
        dk = list(dk_state)
        scores, dq_blocks = [[] for _ in hs], [[] for _ in hs]
        for i in range(c // HGRN_SUB):
            lo = i * HGRN_SUB
            for hh in hs:
                ref = b_sc[hh, lo - 1:lo, :] if i > 0 else jnp.zeros((1, LANES), F32)
                grow = jnp.exp(b[hh][lo:lo + HGRN_SUB, :] - ref)
                qt = q[hh][lo:lo + HGRN_SUB, :] * grow
                dec = jnp.exp(jnp.minimum(ref - b[hh], EXP_CLAMP))
                kd = k[hh] * dec
                scores[hh].append(_dot(qt, kd, _NT))
                da_i = da[hh][lo:lo + HGRN_SUB, :]
                dq_blocks[hh].append(_dot_f32(da_i, kd, _NN) * grow)
                dk[hh] = dk[hh] + _dot_f32(da_i, qt, _TN) * dec
        a = [jnp.where(lower, jnp.concatenate(sc, axis=0), 0.0) for sc in scores]
        dv = [_dot(a[hh], d_o[hh], _TN) + dv_state[hh] for hh in hs]
        dq = [dq_inter[hh] + jnp.concatenate(dq_blocks[hh], axis=0) for hh in hs]
        db_last = [jnp.sum(k[hh] * dk_state[hh], axis=0, keepdims=True)
                   + ebl[hh] * jnp.sum(s0t[hh] * ds1t[hh], axis=0, keepdims=True) for hh in hs]
        db = [q[hh] * dq[hh] - k[hh] * dk[hh] + jnp.where(last_row, db_last[hh], 0.0) for hh in hs]
        dg = [_dot_f32(upper, x) for x in db]
        df = [dg[hh] / f[hh] - dk[hh] for hh in hs]
        for hh in hs:
            sl = sls[hh]
            dzf_ref[:, sl] = (df[hh] * (1.0 - lb[hh]) * sf[hh] * (1.0 - sf[hh])).astype(BF16)
            dlb_ref[:, sl] += jnp.sum(df[hh] * (1.0 - sf[hh]), axis=0, keepdims=True)
            dzq_ref[:, sl] = (dq[hh] * sq[hh] * (1.0 + zq_v[hh] * (1.0 - sq[hh]))).astype(BF16)
            dzi_ref[:, sl] = dv[hh].astype(BF16)

    tile = pl.BlockSpec((c, hp * LANES), lambda h, i: (nc - 1 - i, h))
    out = jax.ShapeDtypeStruct((s, d), BF16)
    return pl.pallas_call(
        body, name=name, grid=(h_n // hp, nc),
        in_specs=[tile, tile, tile, pl.BlockSpec((2, hp * LANES), lambda h, i: (0, h)),
                  pl.BlockSpec((hp, None, LANES, LANES), lambda h, i: (h, nc - 1 - i, 0, 0)), tile,
                  pl.BlockSpec(memory_space=pl.ANY)],
        out_specs=[tile, tile, tile, pl.BlockSpec((1, hp * LANES), lambda h, i: (0, h))],
        out_shape=[out, out, out, jax.ShapeDtypeStruct((1, d), F32)],
        scratch_shapes=[pltpu.VMEM((hp, LANES, LANES), F32), pltpu.VMEM((hp, c, LANES), F32)],
        compiler_params=_params("parallel", "arbitrary"),
    )(zq, zf, zi, lb_logits, states, do, after)


LOG2E = 1.4426950408889634
LN2 = 0.6931471805599453
Q_PRESCALE = ATTN_SCALE * LOG2E


def _attn_tile(s):
    return min(1024, max(128, s // 2))


def _causal_pairs(n, q_major):
    pairs = [(i, j) for i in range(n) for j in range(i + 1)] if q_major else [(i, j) for j in range(n) for i in range(j, n)]
    return jnp.asarray([p[0] for p in pairs], jnp.int32), jnp.asarray([p[1] for p in pairs], jnp.int32)


def _scores(qn_ref, qr_ref, kn_ref, kr_ref):
    q = jnp.concatenate([qn_ref[...], qr_ref[...]], axis=1)
    k = jnp.concatenate([kn_ref[...], kr_ref[...]], axis=1)
    return q, k, _dot(q, k, _NT)


def _attn_fwd(qn, qr, kn, kr, v, *, name):
    s, t = qn.shape[0], _attn_tile(qn.shape[0])
    q_blk, k_blk = _causal_pairs(s // t, True)

    def body(qi_ref, kj_ref, qn_ref, qr_ref, kn_ref, kr_ref, v_ref, o_ref, lse_ref, m_sc, l_sc, acc_sc):
        p_id = pl.program_id(1)
        i, j = qi_ref[p_id], kj_ref[p_id]

        @pl.when(j == 0)
        def _():
            m_sc[...] = jnp.full_like(m_sc, -jnp.inf)
            l_sc[...] = jnp.zeros_like(l_sc)
            acc_sc[...] = jnp.zeros_like(acc_sc)

        def update(sc):
            m_prev = m_sc[...]
            m_new = jnp.maximum(m_prev, jnp.max(sc, axis=1, keepdims=True))
            alpha = jnp.exp2(m_prev - m_new)
            p = jnp.exp2(sc - m_new[:, :1])
            l_sc[...] = alpha * l_sc[...] + jnp.sum(p, axis=1, keepdims=True)
            acc_sc[...] = alpha * acc_sc[...] + _dot(p, v_ref[...], _NN)
            m_sc[...] = m_new

        @pl.when(j < i)
        def _():
            update(_scores(qn_ref, qr_ref, kn_ref, kr_ref)[2])

        @pl.when(j == i)
        def _():
            update(jnp.where(_tri(t, True), _scores(qn_ref, qr_ref, kn_ref, kr_ref)[2], -jnp.inf))
            o_ref[...] = (acc_sc[...] / l_sc[...]).astype(BF16)
            lse_ref[...] = m_sc[...] + jnp.log(l_sc[...]) * LOG2E

    q_spec = pl.BlockSpec((t, LANES), lambda h, p, qi, kj: (qi[p], h))
    k_spec = pl.BlockSpec((t, LANES), lambda h, p, qi, kj: (kj[p], h))
    kr_spec = pl.BlockSpec((t, LANES), lambda h, p, qi, kj: (kj[p], 0))
    stat = pltpu.VMEM((t, LANES), F32)
    return pl.pallas_call(
        body, name=name,
        grid_spec=pltpu.PrefetchScalarGridSpec(
            num_scalar_prefetch=2, grid=(MLA_HEADS, q_blk.shape[0]),
            in_specs=[q_spec, q_spec, k_spec, kr_spec, k_spec], out_specs=[q_spec, q_spec],
            scratch_shapes=[stat, stat, stat]),
        out_shape=[jax.ShapeDtypeStruct(qn.shape, BF16), jax.ShapeDtypeStruct(qn.shape, F32)],
        compiler_params=_params("parallel", "arbitrary"),
    )(q_blk, k_blk, qn, qr, kn, kr, v)


def _attn_bwd(qn, qr, kn, kr, v, do, lse, delta, *, name):
    s, t = qn.shape[0], _attn_tile(qn.shape[0])
    n = s // t
    q_blk, k_blk = _causal_pairs(n, False)

    def body(qi_ref, kj_ref, qn_ref, qr_ref, kn_ref, kr_ref, v_ref, do_ref, lse_ref, delta_ref,
             dqn_ref, dqr_ref, dkn_ref, dv_ref, dkr_ref, dk_sc, dv_sc):
        p_id = pl.program_id(1)
        i, j = qi_ref[p_id], kj_ref[p_id]

        @pl.when(p_id == 0)
        def _():
            dqn_ref[...] = jnp.zeros_like(dqn_ref)
            dqr_ref[...] = jnp.zeros_like(dqr_ref)

        @pl.when(i == j)
        def _():
            dk_sc[...] = jnp.zeros_like(dk_sc)
            dv_sc[...] = jnp.zeros_like(dv_sc)

        def accumulate(q, k, sc):
            p = jnp.exp2(sc - lse_ref[...][:, :1])
            d_o = do_ref[...]
            ds = (p * (_dot(d_o, v_ref[...], _NT) - delta_ref[...][:, :1])).astype(BF16)
            dv_sc[...] += _dot(p, d_o, _TN)
            dk_sc[...] += _dot(ds, q, _TN)
            dq = _dot(ds, k, _NN) * ATTN_SCALE
            rows = pl.ds(pl.multiple_of(i * t, t), t)
            dqn_ref[rows, :] += dq[:, :LANES]
            dqr_ref[rows, :] += dq[:, LANES:]

        @pl.when(j < i)
        def _():
            accumulate(*_scores(qn_ref, qr_ref, kn_ref, kr_ref))

        @pl.when(j == i)
        def _():
            q, k, sc = _scores(qn_ref, qr_ref, kn_ref, kr_ref)
            accumulate(q, k, jnp.where(_tri(t, True), sc, -jnp.inf))

        @pl.when(i == n - 1)
        def _():
            dkn_ref[...] = (dk_sc[:, :LANES] * LN2).astype(BF16)
            dkr_ref[...] = dk_sc[:, LANES:] * LN2
            dv_ref[...] = dv_sc[...].astype(BF16)

    q_spec = pl.BlockSpec((t, LANES), lambda h, p, qi, kj: (qi[p], h))
    k_spec = pl.BlockSpec((t, LANES), lambda h, p, qi, kj: (kj[p], h))
    kr_spec = pl.BlockSpec((t, LANES), lambda h, p, qi, kj: (kj[p], 0))
    head_spec = pl.BlockSpec((s, LANES), lambda h, p, qi, kj: (0, h))
    f32_out, bf16_out = jax.ShapeDtypeStruct(qn.shape, F32), jax.ShapeDtypeStruct(qn.shape, BF16)
    return pl.pallas_call(
        body, name=name,
        grid_spec=pltpu.PrefetchScalarGridSpec(
            num_scalar_prefetch=2, grid=(MLA_HEADS, q_blk.shape[0]),
            in_specs=[q_spec, q_spec, k_spec, kr_spec, k_spec, q_spec, q_spec, q_spec],
            out_specs=[head_spec, head_spec, k_spec, k_spec, k_spec],
            scratch_shapes=[pltpu.VMEM((t, 2 * LANES), F32), pltpu.VMEM((t, LANES), F32)]),
        out_shape=[f32_out, f32_out, bf16_out, bf16_out, f32_out],
        compiler_params=_params("parallel", "arbitrary"),
    )(q_blk, k_blk, qn, qr, kn, kr, v, do, lse, delta)


def _exchange(arrs, *, scatter, name):
    n = len(arrs)
    out_shape = [jax.ShapeDtypeStruct(a.shape if scatter else (N_DEV, *a.shape), a.dtype) for a in arrs]

    def body(*refs):
        ins, outs = refs[:n], refs[n:2 * n]
        send_sems, recv_sems, local_sems = refs[2 * n:]
        x, y, c = lax.axis_index("x"), lax.axis_index("y"), lax.axis_index("c")
        me = 4 * x + 2 * y + c
        copies = []
        for k in range(n):
            local = pltpu.make_async_copy(ins[k].at[me] if scatter else ins[k], outs[k].at[me], local_sems.at[k])
            local.start()
            copies.append(local)
            for d in range(1, N_DEV):
                px, py, pc = (x + (d >> 2)) % 2, (y + ((d >> 1) & 1)) % 2, (c + (d & 1)) % 2
                peer = 4 * px + 2 * py + pc
                remote = pltpu.make_async_remote_copy(
                    src_ref=ins[k].at[peer] if scatter else ins[k], dst_ref=outs[k].at[me],
                    send_sem=send_sems.at[k, d - 1], recv_sem=recv_sems.at[k, d - 1],
                    device_id=(px, py, pc), device_id_type=pl.DeviceIdType.MESH)
                remote.start()
                copies.append(remote)
        for cp in copies:
            cp.wait()

    any_spec = pl.BlockSpec(memory_space=pl.ANY)
    return pl.pallas_call(
        body, name=name, in_specs=[any_spec] * n, out_specs=[any_spec] * n, out_shape=out_shape,
        scratch_shapes=[pltpu.SemaphoreType.DMA((n, N_DEV - 1)), pltpu.SemaphoreType.DMA((n, N_DEV - 1)),
                        pltpu.SemaphoreType.DMA((n,))],
    )(*arrs)


def _peers(x, y, c):
    out = []
    for d in range(1, N_DEV):
        px, py, pc = (x + (d >> 2)) % 2, (y + ((d >> 1) & 1)) % 2, (c + (d & 1)) % 2
        out.append(((px, py, pc), 4 * px + 2 * py + pc))
    return out


def _exchange_copies(ins, lands, send_sems, recv_sems, scatter):
    x, y, c = lax.axis_index("x"), lax.axis_index("y"), lax.axis_index("c")
    me = 4 * x + 2 * y + c
    local, remote = [], []
    for k in range(len(ins)):
        local.append(pltpu.make_async_copy(ins[k].at[me] if scatter else ins[k], lands[k].at[me],
                                           recv_sems.at[k * N_DEV + N_DEV - 1]))
        for d, (coords, peer) in enumerate(_peers(x, y, c)):
            remote.append(pltpu.make_async_remote_copy(
                src_ref=ins[k].at[peer] if scatter else ins[k], dst_ref=lands[k].at[me],
                send_sem=send_sems.at[k * N_DEV + d], recv_sem=recv_sems.at[k * N_DEV + d],
                device_id=coords, device_id_type=pl.DeviceIdType.MESH))
    return local, remote


def _exchange_start(arrs, *, scatter, name):
    n = len(arrs)
    hbm = pl.BlockSpec(memory_space=pltpu.HBM)
    sem = pl.BlockSpec(memory_space=pltpu.SEMAPHORE)
    lands = [lax.empty(a.shape if scatter else (N_DEV, *a.shape), a.dtype) for a in arrs]

    def body(*refs):
        ins, land_refs = refs[:n], refs[n:2 * n]
        send_sems, recv_sems, token = refs[2 * n], refs[2 * n + 1], refs[-1]
        local, remote = _exchange_copies(ins, land_refs, send_sems, recv_sems, scatter)
        for cp in local + remote:
            cp.start()
        token[...] = jnp.zeros_like(token)

    operands = [pltpu.with_memory_space_constraint(a, pltpu.HBM) for a in list(arrs) + lands]
    res = pl.pallas_call(
        body, name=name,
        out_shape=(pltpu.SemaphoreType.DMA((n * N_DEV,)), pltpu.SemaphoreType.DMA((n * N_DEV,)),
                   *[pltpu.HBM(o.shape, o.dtype) for o in operands], jax.ShapeDtypeStruct((8, LANES), F32)),
        in_specs=[hbm] * (2 * n), out_specs=(sem, sem, *[hbm] * (2 * n), pl.BlockSpec(memory_space=pltpu.VMEM)),
        input_output_aliases={i: 2 + i for i in range(2 * n)},
        compiler_params=pltpu.CompilerParams(has_side_effects=pltpu.SideEffectType.DATAFLOW_SIDE_EFFECTING),
    )(*operands)
    return (res[0], res[1], list(res[2:2 + n]), list(res[2 + n:2 + 2 * n]), scatter), res[-1]


def _exchange_wait(state, after, *, name):
    send_sems, recv_sems, ins, lands, scatter = state
    n = len(ins)
    hbm = pl.BlockSpec(memory_space=pltpu.HBM)
    sem = pl.BlockSpec(memory_space=pltpu.SEMAPHORE)

    def body(*refs):
        in_refs, land_refs = refs[:n], refs[n:2 * n]
        local, remote = _exchange_copies(in_refs, land_refs, refs[2 * n], refs[2 * n + 1], scatter)
        for cp in local:
            cp.wait()
        for cp in remote:
            cp.wait_send()
            cp.wait_recv()

    res = pl.pallas_call(
        body, name=name, out_shape=tuple(pltpu.HBM(o.shape, o.dtype) for o in ins + lands),
        in_specs=[hbm] * (2 * n) + [sem, sem, pl.BlockSpec(memory_space=pl.ANY)], out_specs=tuple([hbm] * (2 * n)),
        input_output_aliases={i: i for i in range(2 * n)},
        compiler_params=pltpu.CompilerParams(has_side_effects=pltpu.SideEffectType.DATAFLOW_SIDE_EFFECTING),
    )(*ins, *lands, send_sems, recv_sems, after)
    return list(res[n:])


def _adam(w, terms, m, v, *, name):
    r, c = w.shape
    n = terms.shape[0]
    tr = min(r, 128)
    assert r % tr == 0

    def body(w_ref, t_ref, m_ref, v_ref, g_out, d_out, m_out, v_out):
        g = t_ref[0]
        for s in range(1, n):
            g = g + t_ref[s]
        m1 = ADAM_B1 * m_ref[...] + (1.0 - ADAM_B1) * g
        v1 = ADAM_B2 * v_ref[...] + (1.0 - ADAM_B2) * jnp.square(g)
        m_hat = m1 / (1.0 - ADAM_B1 ** ADAM_STEP)
        v_hat = v1 / (1.0 - ADAM_B2 ** ADAM_STEP)
        g_out[...] = g
        d_out[...] = -ADAM_LR * (m_hat / (jnp.sqrt(v_hat) + ADAM_EPS) + ADAM_WD * w_ref[...])
        m_out[...] = m1
        v_out[...] = v1

    spec = pl.BlockSpec((tr, c), lambda i: (i, 0))
    out = jax.ShapeDtypeStruct((r, c), F32)
    return pl.pallas_call(
        body, name=name, grid=(r // tr,),
        in_specs=[spec, pl.BlockSpec((n, tr, c), lambda i: (0, i, 0)), spec, spec], out_specs=[spec] * 4,
        out_shape=[out] * 4, compiler_params=_params("parallel"),
    )(w, terms, m, v)


def _sum_terms(terms, *, name):
    n, _, p = terms.shape

    def body(t_ref, o_ref):
        acc = t_ref[0]
        for s in range(1, n):
            acc = acc + t_ref[s]
        o_ref[...] = acc

    return pl.pallas_call(body, name=name, out_shape=jax.ShapeDtypeStruct((1, p), F32))(terms)


def _lb_logits_grad(dlb, logits, *, name):
    def body(dlb_ref, l_ref, o_ref):
        lb = _lower_bound(l_ref[...])
        d0 = dlb_ref[...] * lb * (1.0 - lb)
        o_ref[...] = jnp.concatenate([d0, -d0], axis=0)

    return pl.pallas_call(body, name=name, out_shape=jax.ShapeDtypeStruct(logits.shape, F32))(dlb, logits)


def _silu_grad(z):
    sg = _sigmoid(z)
    return sg * (1.0 + z * (1.0 - sg))


def _head_norm_gate(o, zg, gn):
    outs = []
    for h in range(HGRN_HEADS):
        sl = slice(h * LANES, (h + 1) * LANES)
        zg_h = zg[:, sl]
        outs.append(_rms(o[:, sl], gn) * (zg_h * _sigmoid(zg_h)))
    return (jnp.concatenate(outs, axis=1),)


def _head_norm_gate_bwd(o, zg, dm, gn):
    do_parts, dzg_parts, dgn = [], [], jnp.zeros((1, LANES), F32)
    for h in range(HGRN_HEADS):
        sl = slice(h * LANES, (h + 1) * LANES)
        o_h, zg_h, dm_h = o[:, sl], zg[:, sl], dm[:, sl]
        gate = zg_h * _sigmoid(zg_h)
        do_h, dgn_h = _rms_bwd(o_h, gn, dm_h * gate)
        dgn = dgn + dgn_h
        do_parts.append(do_h)
        dzg_parts.append(dm_h * _rms(o_h, gn) * _silu_grad(zg_h))
    return jnp.concatenate(do_parts, axis=1), jnp.concatenate(dzg_parts, axis=1), dgn


def _rope_slabs(x, t_c, t_s1, t_s2, transpose):
    fn = _rope_t if transpose else _rope
    return jnp.concatenate(
        [fn(x[:, h * LANES:(h + 1) * LANES], t_c, t_s1, t_s2) for h in range(x.shape[1] // LANES)], axis=1)


def _loss_head(h, tgt, w):
    d = h.shape[1]
    r = lax.rsqrt(jnp.mean(h * h, axis=-1, keepdims=True) + EPS)
    xh = h * r
    err = xh * w - tgt
    loss = 0.5 * jnp.sum(jnp.mean(err * err, axis=-1, keepdims=True), axis=0, keepdims=True)
    dy = err / d
    dxh = dy * w
    dh = r * (dxh - xh * jnp.mean(dxh * xh, axis=-1, keepdims=True))
    return dh, jnp.sum(dy * xh, axis=0, keepdims=True), jnp.broadcast_to(loss, (1, LANES))


def _mlp_fwd(h, norm, w_up, w_down, tag):
    d = h.shape[1]
    xn = _rowcall(lambda x, w: (_rms(x, w),), [h], [norm], [(d, BF16)], [], name=f"{tag}_norm")[0]
    u, act = _mm(xn, w_up, mode="nn", epilogue="relu2", name=f"{tag}_up")
    return _mm(act, w_down, mode="nn", add=h, name=f"{tag}_down"), (h, xn, u, act)


def _mlp_bwd(dh_out, saved, norm, w_up, w_down, tag, after=None):
    h, xn, u, act = saved
    d = h.shape[1]
    du = _mm(dh_out, w_down, mode="nt", epilogue="relu2_bwd", aux=u, out_dtype=BF16, after=after,
             name=f"{tag}_bwd_du")
    dw_down = _mm(act, dh_out, mode="tn", name=f"{tag}_bwd_wdown")
    dxn = _mm(du, w_up, mode="nt", name=f"{tag}_bwd_dxn")
    dw_up = _mm(xn, du, mode="tn", name=f"{tag}_bwd_wup")

    def norm_bwd(x, dy, dres, w):
        dx, dw = _rms_bwd(x, w, dy)
        return dx + dres, dw

    dh, dnorm = _rowcall(norm_bwd, [h, dxn, dh_out], [norm], [(d, F32)], [d], name=f"{tag}_bwd_norm")
    return dh, dnorm, dw_up, dw_down


def _row_major(g):
    return g.reshape(g.shape[0] * g.shape[1], g.shape[2])


def _col_major(g):
    return jnp.transpose(g, (1, 0, 2)).reshape(g.shape[1], g.shape[0] * g.shape[2])


def _col_terms(dw):
    k, n = dw.shape
    return jnp.transpose(dw.reshape(k, N_DEV, n // N_DEV), (1, 0, 2))


def _row_terms(dw):
    return dw.reshape(N_DEV, dw.shape[0] // N_DEV, dw.shape[1])


def kernel(x, hgrn_norm, hgrn_w_q, hgrn_w_f, hgrn_w_i, hgrn_w_g, hgrn_g_norm, hgrn_w_o, hgrn_lb_logits, mla_norm, mla_w_dq, mla_q_norm, mla_w_uq, mla_w_o, kv_in_norm, kv_w_dkv, kv_norm, kv_w_uk, kv_w_uv, mlp_norm, mlp_w_up, mlp_w_down, final_norm, loss_target, m_hgrn_norm, m_hgrn_w_q, m_hgrn_w_f, m_hgrn_w_i, m_hgrn_w_g, m_hgrn_g_norm, m_hgrn_w_o, m_hgrn_lb_logits, m_mla_norm, m_mla_w_dq, m_mla_q_norm, m_mla_w_uq, m_mla_w_o, m_kv_in_norm, m_kv_w_dkv, m_kv_norm, m_kv_w_uk, m_kv_w_uv, m_mlp_norm, m_mlp_w_up, m_mlp_w_down, m_final_norm, v_hgrn_norm, v_hgrn_w_q, v_hgrn_w_f, v_hgrn_w_i, v_hgrn_w_g, v_hgrn_g_norm, v_hgrn_w_o, v_hgrn_lb_logits, v_mla_norm, v_mla_w_dq, v_mla_q_norm, v_mla_w_uq, v_mla_w_o, v_kv_in_norm, v_kv_w_dkv, v_kv_norm, v_kv_w_uk, v_kv_w_uv, v_mlp_norm, v_mlp_w_up, v_mlp_w_down, v_final_norm):
    given = dict(locals())
    weight_names = ["hgrn_norm", "hgrn_w_q", "hgrn_w_f", "hgrn_w_i", "hgrn_w_g", "hgrn_g_norm", "hgrn_w_o",
                    "hgrn_lb_logits", "mla_norm", "mla_w_dq", "mla_q_norm", "mla_w_uq", "mla_w_o", "kv_in_norm",
                    "kv_w_dkv", "kv_norm", "kv_w_uk", "kv_w_uv", "mlp_norm", "mlp_w_up", "mlp_w_down", "final_norm"]
    me = 4 * lax.axis_index("x") + 2 * lax.axis_index("y") + lax.axis_index("c")
    xs, tgt = x[0], loss_target[0]
    seq, d_model = xs.shape
    n_heads, hd = MLA_HEADS, LANES

    big_local = {
        "hgrn_w_q": hgrn_w_q[0], "hgrn_w_f": hgrn_w_f[0], "hgrn_w_i": hgrn_w_i[0], "hgrn_w_g": hgrn_w_g[0],
        "hgrn_w_o": hgrn_w_o[0], "mla_w_dq": mla_w_dq[0], "mla_w_uq": mla_w_uq[0], "mla_w_o": mla_w_o[0],
        "kv_w_dkv": kv_w_dkv, "kv_w_uk": kv_w_uk, "kv_w_uv": kv_w_uv,
        "mlp_w_up0": mlp_w_up[0], "mlp_w_up1": mlp_w_up[1], "mlp_w_down0": mlp_w_down[0], "mlp_w_down1": mlp_w_down[1],
    }
    big_names = list(big_local)
    col_sharded = {"mla_w_uq", "kv_w_uk", "kv_w_uv", "mlp_w_up0", "mlp_w_up1"}
    vec_local = jnp.concatenate([hgrn_norm, hgrn_lb_logits], axis=0)
    first_names = ["hgrn_w_q", "hgrn_w_f", "hgrn_w_i", "hgrn_w_g"]
    later_names = {"mlp0": ["hgrn_w_o", "mlp_w_up0", "mlp_w_down0"],
                   "mla": ["kv_w_dkv", "kv_w_uk", "kv_w_uv", "mla_w_dq", "mla_w_uq", "mla_w_o"],
                   "mlp1": ["mlp_w_up1", "mlp_w_down1"]}

    def unshard(names, arrays):
        return {k: (_col_major(a) if k in col_sharded else _row_major(a)) for k, a in zip(names, arrays)}

    gathered = _exchange([big_local[k].astype(BF16) for k in first_names] + [vec_local], scatter=False, name="gather_first")
    gather_state, token = {}, jnp.zeros((8, LANES), F32)
    for tag, names in later_names.items():
        gather_state[tag], token = _exchange_start([(big_local[k] + token[0, 0]).astype(BF16) for k in names],
                                                   scatter=False, name=f"gather_{tag}_start")

    def gather_wait(tag, after):
        w.update(unshard(later_names[tag], _exchange_wait(gather_state[tag], after, name=f"gather_{tag}_wait")))

    w = unshard(first_names, gathered[:-1])
    vec_full = jnp.transpose(gathered[-1], (1, 0, 2)).reshape(3, d_model)
    hgrn_norm_full, lb_logits_full = vec_full[0:1], vec_full[1:3]
    t_c, t_s1, t_s2 = _rope_tables(seq)
    kv_lora = kv_w_uk.shape[0]

    xn0 = _rowcall(lambda a, g: (_rms(a, g),), [xs], [hgrn_norm_full], [(d_model, BF16)], [], name="hgrn_norm")[0]
    zq = _mm(xn0, w["hgrn_w_q"], mode="nn", after=token, name="hgrn_zq")
    zf = _mm(xn0, w["hgrn_w_f"], mode="nn", name="hgrn_zf")
    zi = _mm(xn0, w["hgrn_w_i"], mode="nn", name="hgrn_zi")
    zg = _mm(xn0, w["hgrn_w_g"], mode="nn", name="hgrn_zg")
    o_rec, states = _hgrn_fwd(zq, zf, zi, lb_logits_full, name="hgrn_fwd")
    mixed = _rowcall(_head_norm_gate, [o_rec, zg], [hgrn_g_norm], [(d_model, BF16)], [], name="hgrn_gate")[0]
    gather_wait("mlp0", mixed)
    h1 = _mm(mixed, w["hgrn_w_o"], mode="nn", add=xs, name="hgrn_out")
    h2, mlp0_saved = _mlp_fwd(h1, mlp_norm[0:1], w["mlp_w_up0"], w["mlp_w_down0"], "mlp0")
    gather_wait("mla", h2)
    w_uq3 = w["mla_w_uq"].reshape(-1, n_heads, MLA_NOPE + MLA_ROPE)
    w_uq_nope = w_uq3[:, :, :MLA_NOPE].reshape(-1, n_heads * hd)
    w_uq_rope = jnp.pad(w_uq3[:, :, MLA_NOPE:], ((0, 0), (0, 0), (0, hd - MLA_ROPE))).reshape(-1, n_heads * hd)
    w_dkv_pad = jnp.pad(w["kv_w_dkv"], ((0, 0), (0, kv_lora + hd - w["kv_w_dkv"].shape[1])))

    hn, xn2 = _rowcall(lambda a, g1, g2: (_rms(a, g1), _rms(a, g2)), [h2], [kv_in_norm[None, :], mla_norm],
                       [(d_model, BF16), (d_model, BF16)], [], name="kv_mla_norm")
    ckr = _mm(hn, w_dkv_pad, mode="nn", name="kv_down")

    def kv_latent(c_all, tc, ts1, ts2, g):
        return _rms(c_all[:, :kv_lora], g), _rope(c_all[:, kv_lora:], tc, ts1, ts2)

    c_kv, kr = _rowcall(kv_latent, [ckr, t_c, t_s1, t_s2], [kv_norm[None, :]], [(kv_lora, BF16), (hd, BF16)], [],
                        name="kv_latent")
    kn = _mm(c_kv, w["kv_w_uk"], mode="nn", out_dtype=BF16, name="kv_up_k")
    vv = _mm(c_kv, w["kv_w_uv"], mode="nn", out_dtype=BF16, name="kv_up_v")
    cq_pre = _mm(xn2, w["mla_w_dq"], mode="nn", name="q_down")
    c_q = _rowcall(lambda a, g: (_rms(a, g),), [cq_pre], [mla_q_norm], [(cq_pre.shape[1], BF16)], [], name="q_norm")[0]
    qn = _mm(c_q, w_uq_nope, mode="nn", out_dtype=BF16, scale=Q_PRESCALE, name="q_up_nope")
    qr_pre = _mm(c_q, w_uq_rope, mode="nn", name="q_up_rope")
    qr = _rowcall(lambda a, tc, ts1, ts2: (_rope_slabs(a, tc, ts1, ts2, False) * Q_PRESCALE,), [qr_pre, t_c, t_s1, t_s2], [],
                  [(n_heads * hd, BF16)], [], name="q_rope")[0]
    o_att, lse = _attn_fwd(qn, qr, kn, kr, vv, name="attn_fwd")
    h3 = _mm(o_att, w["mla_w_o"], mode="nn", add=h2, name="attn_out")
    gather_wait("mlp1", h3)
    h4, mlp1_saved = _mlp_fwd(h3, mlp_norm[1:2], w["mlp_w_up1"], w["mlp_w_down1"], "mlp1")
    dh4, g_final_norm, loss_part = _rowcall(_loss_head, [h4, tgt], [final_norm[None, :]], [(d_model, F32)],
                                            [d_model, LANES], name="loss_head")

    g = {}
    groups = {"mlp1": ["mlp_w_up1", "mlp_w_down1"],
              "mla": ["mla_w_o", "mla_w_uq", "mla_w_dq", "kv_w_uk", "kv_w_uv", "kv_w_dkv"],
              "mlp0": ["mlp_w_up0", "mlp_w_down0"],
              "hgrn_out": ["hgrn_w_o", "hgrn_w_g"],
              "hgrn_in": ["hgrn_w_q", "hgrn_w_f", "hgrn_w_i"]}
    scatter_state = {}

    def scatter_start(tag):
        scatter_state[tag], tok = _exchange_start(
            [(_col_terms if k in col_sharded else _row_terms)(g[k]) for k in groups[tag]], scatter=True,
            name=f"scatter_{tag}_start")
        return tok

    dh3, g_mlp_norm1, g["mlp_w_up1"], g["mlp_w_down1"] = _mlp_bwd(
        dh4, mlp1_saved, mlp_norm[1:2], w["mlp_w_up1"], w["mlp_w_down1"], "mlp1")
    d_oatt = _mm(dh3, w["mla_w_o"], mode="nt", out_dtype=BF16, after=scatter_start("mlp1"), name="attn_out_bwd_x")
    g["mla_w_o"] = _mm(o_att, dh3, mode="tn", name="attn_out_bwd_w")

    def head_delta(do, o):
        prod = do.astype(F32) * o.astype(F32)
        return (jnp.concatenate([jnp.broadcast_to(jnp.sum(prod[:, h * hd:(h + 1) * hd], axis=1, keepdims=True),
                                                  (prod.shape[0], hd)) for h in range(n_heads)], axis=1),)

    delta = _rowcall(head_delta, [d_oatt, o_att], [], [(n_heads * hd, F32)], [], name="attn_delta")[0]
    dqn, dqr, dkn, dvv, dkr = _attn_bwd(qn, qr, kn, kr, vv, d_oatt, lse, delta, name="attn_bwd")
    dqr_pre = _rowcall(lambda a, tc, ts1, ts2: (_rope_slabs(a, tc, ts1, ts2, True),), [dqr, t_c, t_s1, t_s2], [],
                       [(n_heads * hd, BF16)], [], name="q_rope_bwd")[0]
    dcq = _mm(dqn, w_uq_nope, mode="nt", name="q_up_nope_bwd_x")
    dcq = _mm(dqr_pre, w_uq_rope, mode="nt", add=dcq, name="q_up_rope_bwd_x")
    g_uq_nope = _mm(c_q, dqn, mode="tn", name="q_up_nope_bwd_w")
    g_uq_rope = _mm(c_q, dqr_pre, mode="tn", name="q_up_rope_bwd_w")
    q_lora = c_q.shape[1]
    g["mla_w_uq"] = jnp.concatenate([g_uq_nope.reshape(q_lora, n_heads, hd),
                                     g_uq_rope.reshape(q_lora, n_heads, hd)[:, :, :MLA_ROPE]], axis=2).reshape(q_lora, -1)
    dcq_pre, g_q_norm = _rowcall(lambda a, dy, gq: _rms_bwd(a, gq, dy), [cq_pre, dcq], [mla_q_norm],
                                 [(q_lora, BF16)], [q_lora], name="q_norm_bwd")
    dxn2 = _mm(dcq_pre, w["mla_w_dq"], mode="nt", name="q_down_bwd_x")
    g["mla_w_dq"] = _mm(xn2, dcq_pre, mode="tn", name="q_down_bwd_w")

    dc_kv = _mm(dkn, w["kv_w_uk"], mode="nt", name="kv_up_k_bwd_x")
    dc_kv = _mm(dvv, w["kv_w_uv"], mode="nt", add=dc_kv, name="kv_up_v_bwd_x")
    g["kv_w_uk"] = _mm(c_kv, dkn, mode="tn", name="kv_up_k_bwd_w")
    g["kv_w_uv"] = _mm(c_kv, dvv, mode="tn", name="kv_up_v_bwd_w")

    def kv_latent_bwd(c_all, dc, dkr_heads, tc, ts1, ts2, gk):
        dlat, dgk = _rms_bwd(c_all[:, :kv_lora], gk, dc)
        dkr_slab = dkr_heads[:, :hd]
        for h in range(1, n_heads):
            dkr_slab = dkr_slab + dkr_heads[:, h * hd:(h + 1) * hd]
        return jnp.concatenate([dlat, _rope_t(dkr_slab, tc, ts1, ts2)], axis=1), dgk

    dckr, g_kv_norm = _rowcall(kv_latent_bwd, [ckr, dc_kv, dkr, t_c, t_s1, t_s2], [kv_norm[None, :]],
                               [(kv_lora + hd, BF16)], [kv_lora], name="kv_latent_bwd")
    dhn = _mm(dckr, w_dkv_pad, mode="nt", name="kv_down_bwd_x")
    g["kv_w_dkv"] = _mm(hn, dckr, mode="tn", name="kv_down_bwd_w")[:, :kv_w_dkv.shape[1]]

    def kv_mla_norm_bwd(a, d1, d2, dres, g1, g2):
        dx1, dw1 = _rms_bwd(a, g1, d1)
        dx2, dw2 = _rms_bwd(a, g2, d2)
        return dx1 + dx2 + dres, dw1, dw2

    dh2, g_kv_in_norm, g_mla_norm = _rowcall(kv_mla_norm_bwd, [h2, dhn, dxn2, dh3], [kv_in_norm[None, :], mla_norm],
                                             [(d_model, F32)], [d_model, d_model], name="kv_mla_norm_bwd")
    dh1, g_mlp_norm0, g["mlp_w_up0"], g["mlp_w_down0"] = _mlp_bwd(
        dh2, mlp0_saved, mlp_norm[0:1], w["mlp_w_up0"], w["mlp_w_down0"], "mlp0", after=scatter_start("mla"))

    dmixed = _mm(dh1, w["hgrn_w_o"], mode="nt", after=scatter_start("mlp0"), name="hgrn_out_bwd_x")
    g["hgrn_w_o"] = _mm(mixed, dh1, mode="tn", name="hgrn_out_bwd_w")
    do_rec, dzg, g_g_norm = _rowcall(_head_norm_gate_bwd, [o_rec, zg, dmixed], [hgrn_g_norm],
                                     [(d_model, F32), (d_model, BF16)], [hd], name="hgrn_gate_bwd")
    g["hgrn_w_g"] = _mm(xn0, dzg, mode="tn", name="hgrn_w_g_bwd_w")
    dzq, dzf, dzi, g_lb = _hgrn_bwd(zq, zf, zi, lb_logits_full, states, do_rec, scatter_start("hgrn_out"),
                                    name="hgrn_bwd")
    dxn0 = _mm(dzg, w["hgrn_w_g"], mode="nt", name="hgrn_w_g_bwd_x")
    for nm, dz in (("hgrn_w_q", dzq), ("hgrn_w_f", dzf), ("hgrn_w_i", dzi)):
        dxn0 = _mm(dz, w[nm], mode="nt", add=dxn0, name=f"{nm}_bwd_x")
        g[nm] = _mm(xn0, dz, mode="tn", name=f"{nm}_bwd_w")

    def in_norm_bwd(a, dy, dres, gw):
        dx, dw = _rms_bwd(a, gw, dy)
        return dx + dres, dw

    grad_x, g_hgrn_norm = _rowcall(in_norm_bwd, [xs, dxn0, dh1], [hgrn_norm_full], [(d_model, F32)], [d_model],
                                   name="hgrn_norm_bwd")

    last = scatter_start("hgrn_in")
    small_parts = [g_hgrn_norm, g_lb, g_g_norm, g_mla_norm, g_q_norm, g_kv_in_norm, g_kv_norm, g_mlp_norm0,
                   g_mlp_norm1, g_final_norm, loss_part]
    small_sizes = [p.shape[1] for p in small_parts]
    small_terms = _exchange([jnp.concatenate(small_parts, axis=1)], scatter=False, name="gather_small")[0]
    small_sum = _sum_terms(small_terms, name="sum_small")
    offs = [0]
    for sz in small_sizes:
        offs.append(offs[-1] + sz)
    (s_hgrn_norm, s_lb, s_g_norm, s_mla_norm, s_q_norm, s_kv_in_norm, s_kv_norm, s_mlp_norm0, s_mlp_norm1, s_final_norm,
     s_loss) = [small_sum[:, a:b] for a, b in zip(offs[:-1], offs[1:])]
    shard = hgrn_norm.shape[1]
    g_lb_logits = _lb_logits_grad(lax.dynamic_slice_in_dim(s_lb, me * shard, shard, axis=1), hgrn_lb_logits,
                                  name="lb_logits_grad")
    loss = s_loss[0, 0]

    res = {}
    for tag, names in groups.items():
        for k, t in zip(names, _exchange_wait(scatter_state[tag], last, name=f"scatter_{tag}_wait")):
            if k.startswith("mlp_w_"):
                base, layer = k[:-1], int(k[-1])
                wk, mk, vk = given[base][layer], given["m_" + base][layer], given["v_" + base][layer]
            else:
                wk, mk, vk = given[k], given["m_" + k], given["v_" + k]
            shape = wk.shape
            wk, mk, vk = (a.reshape(shape[-2], shape[-1]) for a in (wk, mk, vk))
            upd = _adam(wk, t, mk, vk, name=f"adam_{k}")
            last = upd[0]
            res[k] = [o.reshape(shape) for o in upd]
    for base in ("mlp_w_up", "mlp_w_down"):
        res[base] = [jnp.stack([res[base + "0"][i], res[base + "1"][i]], axis=0) for i in range(4)]

    small_grads = {
        "hgrn_norm": lax.dynamic_slice_in_dim(s_hgrn_norm, me * shard, shard, axis=1),
        "hgrn_g_norm": s_g_norm, "hgrn_lb_logits": g_lb_logits, "mla_norm": s_mla_norm, "mla_q_norm": s_q_norm,
        "kv_in_norm": s_kv_in_norm, "kv_norm": s_kv_norm,
        "mlp_norm": jnp.concatenate([s_mlp_norm0, s_mlp_norm1], axis=0), "final_norm": s_final_norm,
    }
    small_names = list(small_grads)

    def flat(a):
        return a.reshape(1, -1)

    packed = [jnp.concatenate([flat(src[pre + k]) for k in small_names], axis=1)
              for src, pre in ((given, ""), (small_grads, ""), (given, "m_"), (given, "v_"))]
    small_out = _adam(packed[0], packed[1][None], packed[2], packed[3], name="adam_small")
    off = 0
    for k in small_names:
        size = given[k].size
        res[k] = [o[:, off:off + size].reshape(given[k].shape) for o in small_out]
        off += size

    outs = [loss, grad_x[None]]
    for i in range(4):
        outs += [res[k][i] for k in weight_names]
    return tuple(outs)
```

```python
import functools

import jax
import jax.numpy as jnp
from jax import lax
from jax.experimental import pallas as pl
from jax.experimental.pallas import tpu as pltpu

F32 = jnp.float32
BF16 = jnp.bfloat16

EPS = 1e-6
LANES = 128
N_DEV = 8
V7X_VMEM_LIMIT_BYTES = 56 << 20
MM_PIPELINE_BYTES = 30 << 20
MM_ROW_TILE = 512

HGRN_HEADS = 8
HGRN_CHUNK = 64
HGRN_SUB = 16
HGRN_HEADS_PER_STEP = 8
EXP_CLAMP = 80.0
MLA_HEADS = 16
MLA_NOPE = 128
MLA_ROPE = 64
ROPE_THETA = 10000.0
ATTN_SCALE = (MLA_NOPE + MLA_ROPE) ** -0.5

ADAM_LR = 0.001
ADAM_B1 = 0.9
ADAM_B2 = 0.999
ADAM_EPS = 1e-08
ADAM_WD = 0.01
ADAM_STEP = 10

_NN = ((1,), (0,))
_NT = ((1,), (1,))
_TN = ((0,), (0,))


def _params(*sem):
    return pltpu.CompilerParams(dimension_semantics=sem, vmem_limit_bytes=V7X_VMEM_LIMIT_BYTES)


def _dot(a, b, dims):
    return lax.dot_general(a.astype(BF16), b.astype(BF16), (dims, ((), ())), preferred_element_type=F32)


def _dot_f32(a, b, dims=_NN):
    return lax.dot_general(a, b, (dims, ((), ())), precision=lax.Precision.HIGHEST, preferred_element_type=F32)


def _sigmoid(x):
    return 1.0 / (1.0 + jnp.exp(-x))


def _rms(x, w):
    r = lax.rsqrt(jnp.mean(x * x, axis=-1, keepdims=True) + EPS)
    return x * r * w


def _rms_bwd(x, w, dy):
    r = lax.rsqrt(jnp.mean(x * x, axis=-1, keepdims=True) + EPS)
    xh = x * r
    dw = jnp.sum(dy * xh, axis=0, keepdims=True)
    dxh = dy * w
    dx = r * (dxh - xh * jnp.mean(dxh * xh, axis=-1, keepdims=True))
    return dx, dw


def _mm_tiles(m, n, k, a_bytes, b_bytes, out_tile_bytes):
    tm = min(m, MM_ROW_TILE)
    for tn in (n, 2048, 1024, 512, 256, LANES):
        if tn <= n and n % tn == 0:
            if 2 * (tm * k * a_bytes + k * tn * b_bytes + tm * tn * out_tile_bytes) <= MM_PIPELINE_BYTES:
                return tm, tn
    return tm, min(n, LANES)


def _mm(a, b, *, mode, name, out_dtype=F32, add=None, epilogue=None, aux=None, after=None, scale=None):
    if mode == "nn":
        (m, k), (k2, n) = a.shape, b.shape
    elif mode == "nt":
        (m, k), (n, k2) = a.shape, b.shape
    else:
        (k, m), (k2, n) = a.shape, b.shape
    assert k == k2, (name, a.shape, b.shape)
    tile_bytes = sum(x.dtype.itemsize for x in (add, aux) if x is not None)
    tile_bytes += 6 if epilogue == "relu2" else jnp.dtype(out_dtype).itemsize
    tm, tn = _mm_tiles(m, n, k, a.dtype.itemsize, b.dtype.itemsize, tile_bytes)
    assert m % tm == 0 and n % tn == 0, (name, m, n)
    dims = {"nn": _NN, "nt": _NT, "tn": _TN}[mode]
    a_spec = pl.BlockSpec((k, tm), lambda i, j: (0, i)) if mode == "tn" else pl.BlockSpec((tm, k), lambda i, j: (i, 0))
    b_spec = pl.BlockSpec((tn, k), lambda i, j: (j, 0)) if mode == "nt" else pl.BlockSpec((k, tn), lambda i, j: (0, j))
    o_spec = pl.BlockSpec((tm, tn), lambda i, j: (i, j))
    operands, in_specs = [a, b], [a_spec, b_spec]
    for extra in (add, aux):
        if extra is not None:
            assert extra.shape == (m, n), (name, extra.shape)
            operands.append(extra)
            in_specs.append(o_spec)
    n_in = len(operands)
    if after is not None:
        operands.append(after)
        in_specs.append(pl.BlockSpec(memory_space=pl.ANY))
    if epilogue == "relu2":
        out_shape = [jax.ShapeDtypeStruct((m, n), F32), jax.ShapeDtypeStruct((m, n), BF16)]
        out_specs = [o_spec, o_spec]
    else:
        out_shape = jax.ShapeDtypeStruct((m, n), out_dtype)
        out_specs = o_spec

    def body(*refs):
        acc = _dot(refs[0][...], refs[1][...], dims)
        extras, outs = refs[2:n_in], refs[len(operands):]
        if scale is not None:
            acc = acc * scale
        if add is not None:
            acc = acc + extras[0][...]
        if epilogue == "relu2":
            outs[0][...] = acc
            outs[1][...] = jnp.square(jnp.maximum(acc, 0.0)).astype(BF16)
        elif epilogue == "relu2_bwd":
            outs[0][...] = (acc * (2.0 * jnp.maximum(extras[-1][...], 0.0))).astype(out_dtype)
        else:
            outs[0][...] = acc.astype(out_dtype)

    return pl.pallas_call(
        body, name=name, grid=(m // tm, n // tn), in_specs=in_specs, out_specs=out_specs, out_shape=out_shape,
        compiler_params=_params("parallel", "parallel"),
    )(*operands)


def _rowcall(fn, rows, consts, outs, accs, *, name, tr=256):
    s = rows[0].shape[0]
    tr = min(tr, s)
    assert s % tr == 0
    n_out = len(outs)
    in_specs = [pl.BlockSpec((tr, r.shape[1]), lambda i: (i, 0)) for r in rows]
    in_specs += [pl.BlockSpec(c.shape, lambda i: (0, 0)) for c in consts]
    out_shape = [jax.ShapeDtypeStruct((s, w), dt) for w, dt in outs] + [jax.ShapeDtypeStruct((1, w), F32) for w in accs]
    out_specs = [pl.BlockSpec((tr, w), lambda i: (i, 0)) for w, _ in outs] + [pl.BlockSpec((1, w), lambda i: (0, 0)) for w in accs]
    n_in = len(rows) + len(consts)

    def body(*refs):
        res = fn(*[r[...] for r in refs[:n_in]])
        out_refs = refs[n_in:]
        for ref, val in zip(out_refs[:n_out], res[:n_out]):
            ref[...] = val.astype(ref.dtype)
        i = pl.program_id(0)
        for ref, val in zip(out_refs[n_out:], res[n_out:]):
            @pl.when(i == 0)
            def _(ref=ref, val=val):
                ref[...] = val

            @pl.when(i > 0)
            def _(ref=ref, val=val):
                ref[...] += val

    return pl.pallas_call(
        body, name=name, grid=(s // tr,), in_specs=in_specs, out_specs=out_specs, out_shape=out_shape,
        compiler_params=_params("arbitrary" if accs else "parallel"),
    )(*rows, *consts)


def _rope_tables(seq):
    half = MLA_ROPE // 2
    inv_freq = ROPE_THETA ** (-jnp.arange(half, dtype=F32) / half)
    ang = jnp.arange(seq, dtype=F32)[:, None] * inv_freq[None, :]
    cos, sin, zero = jnp.cos(ang), jnp.sin(ang), jnp.zeros((seq, half), F32)
    t_c = jnp.concatenate([cos, cos, zero, zero], axis=1)
    t_s1 = jnp.concatenate([-sin, zero, zero, zero], axis=1)
    t_s2 = jnp.concatenate([zero, sin, zero, zero], axis=1)
    return t_c, t_s1, t_s2


def _rope(slab, t_c, t_s1, t_s2):
    return slab * t_c + pltpu.roll(slab, 96, 1) * t_s1 + pltpu.roll(slab, 32, 1) * t_s2


def _rope_t(d, t_c, t_s1, t_s2):
    return d * t_c + pltpu.roll(d * t_s1, 32, 1) + pltpu.roll(d * t_s2, 96, 1)


def _lower_bound(logits):
    l0, l1 = logits[0:1, :], logits[1:2, :]
    mx = jnp.maximum(l0, l1)
    e0, e1 = jnp.exp(l0 - mx), jnp.exp(l1 - mx)
    return e0 / (e0 + e1)


def _tri(n, lower):
    row = lax.broadcasted_iota(jnp.int32, (n, n), 0)
    col = lax.broadcasted_iota(jnp.int32, (n, n), 1)
    return (row >= col) if lower else (row <= col)


def _hgrn_fwd(zq, zf, zi, lb_logits, *, name):
    s, d = zq.shape
    h_n, c, hp = d // LANES, HGRN_CHUNK, HGRN_HEADS_PER_STEP
    nc = s // c

    def body(zq_ref, zf_ref, zi_ref, lb_ref, o_ref, st_ref, state_sc, b_sc):
        @pl.when(pl.program_id(1) == 0)
        def _():
            state_sc[...] = jnp.zeros_like(state_sc)

        lower = _tri(c, True).astype(F32)
        hs = range(hp)
        sls = [slice(hh * LANES, (hh + 1) * LANES) for hh in hs]
        lb = [_lower_bound(lb_ref[:, sl]) for sl in sls]
        zq_v = [zq_ref[:, sl] for sl in sls]
        q = [z * _sigmoid(z) for z in zq_v]
        f = [lb[hh] + (1.0 - lb[hh]) * _sigmoid(zf_ref[:, sls[hh]]) for hh in hs]
        g = [jnp.log(x) for x in f]
        k = [1.0 - x for x in f]
        v = [zi_ref[:, sl] for sl in sls]
        b = [_dot_f32(lower, x) for x in g]
        s0t = [state_sc[hh] for hh in hs]
        for hh in hs:
            st_ref[hh] = s0t[hh]
            b_sc[hh] = b[hh]
        o_inter = [_dot(q[hh] * jnp.exp(b[hh]), s0t[hh], _NT) for hh in hs]
        scores = [[] for _ in hs]
        for i in range(c // HGRN_SUB):
            lo = i * HGRN_SUB
            for hh in hs:
                ref = b_sc[hh, lo - 1:lo, :] if i > 0 else jnp.zeros((1, LANES), F32)
                qt = q[hh][lo:lo + HGRN_SUB, :] * jnp.exp(b[hh][lo:lo + HGRN_SUB, :] - ref)
                dec = jnp.exp(jnp.minimum(ref - b[hh], EXP_CLAMP))
                scores[hh].append(_dot(qt, k[hh] * dec, _NT))
        a = [jnp.where(_tri(c, True), jnp.concatenate(sc, axis=0), 0.0) for sc in scores]
        for hh in hs:
            o_ref[:, sls[hh]] = o_inter[hh] + _dot(a[hh], v[hh], _NN)
        bl = [b_sc[hh, c - 1:c, :] for hh in hs]
        for hh in hs:
            state_sc[hh] = s0t[hh] * jnp.exp(bl[hh]) + _dot(v[hh], k[hh] * jnp.exp(bl[hh] - b[hh]), _TN)

    tile = pl.BlockSpec((c, hp * LANES), lambda h, i: (i, h))
    return pl.pallas_call(
        body, name=name, grid=(h_n // hp, nc),
        in_specs=[tile, tile, tile, pl.BlockSpec((2, hp * LANES), lambda h, i: (0, h))],
        out_specs=[tile, pl.BlockSpec((hp, None, LANES, LANES), lambda h, i: (h, i, 0, 0))],
        out_shape=[jax.ShapeDtypeStruct((s, d), F32), jax.ShapeDtypeStruct((h_n, nc, LANES, LANES), F32)],
        scratch_shapes=[pltpu.VMEM((hp, LANES, LANES), F32), pltpu.VMEM((hp, c, LANES), F32)],
        compiler_params=_params("parallel", "arbitrary"),
    )(zq, zf, zi, lb_logits)


def _hgrn_bwd(zq, zf, zi, lb_logits, states, do, after, *, name):
    s, d = zq.shape
    h_n, c, hp = d // LANES, HGRN_CHUNK, HGRN_HEADS_PER_STEP
    nc = s // c

    def body(zq_ref, zf_ref, zi_ref, lb_ref, st_ref, do_ref, _, dzq_ref, dzf_ref, dzi_ref, dlb_ref, dstate_sc, b_sc):
        @pl.when(pl.program_id(1) == 0)
        def _():
            dstate_sc[...] = jnp.zeros_like(dstate_sc)
            dlb_ref[...] = jnp.zeros_like(dlb_ref)

        lower, upper = _tri(c, True), _tri(c, False).astype(F32)
        lower_f = lower.astype(F32)
        last_row = lax.broadcasted_iota(jnp.int32, (c, LANES), 0) == c - 1
        hs = range(hp)
        sls = [slice(hh * LANES, (hh + 1) * LANES) for hh in hs]
        lb = [_lower_bound(lb_ref[:, sl]) for sl in sls]
        zq_v = [zq_ref[:, sl] for sl in sls]
        sq = [_sigmoid(z) for z in zq_v]
        q = [zq_v[hh] * sq[hh] for hh in hs]
        sf = [_sigmoid(zf_ref[:, sl]) for sl in sls]
        f = [lb[hh] + (1.0 - lb[hh]) * sf[hh] for hh in hs]
        g = [jnp.log(x) for x in f]
        k = [1.0 - x for x in f]
        v = [zi_ref[:, sl] for sl in sls]
        d_o = [do_ref[:, sl] for sl in sls]
        b = [_dot_f32(lower_f, x) for x in g]
        s0t = [st_ref[hh] for hh in hs]
        ds1t = [dstate_sc[hh] for hh in hs]
        for hh in hs:
            b_sc[hh] = b[hh]
        bl = [b_sc[hh, c - 1:c, :] for hh in hs]
        eb = [jnp.exp(x) for x in b]
        ebl = [jnp.exp(x) for x in bl]
        dec_end = [jnp.exp(bl[hh] - b[hh]) for hh in hs]
        da = [jnp.where(lower, _dot(d_o[hh], v[hh], _NT), 0.0) for hh in hs]
        dq_inter = [_dot(d_o[hh], s0t[hh], _NN) * eb[hh] for hh in hs]
        dk_state = [_dot(v[hh], ds1t[hh], _NN) * dec_end[hh] for hh in hs]
        dv_state = [_dot(k[hh] * dec_end[hh], ds1t[hh], _NT) for hh in hs]
        for hh in hs:
            dstate_sc[hh] = ds1t[hh] * ebl[hh] + _dot(d_o[hh], q[hh] * eb[hh], _TN)
        dk = list(dk_state)
        scores, dq_blocks = [[] for _ in hs], [[] for _ in hs]
        for i in range(c // HGRN_SUB):
            lo = i * HGRN_SUB
            for hh in hs:
                ref = b_sc[hh, lo - 1:lo, :] if i > 0 else jnp.zeros((1, LANES), F32)
                grow = jnp.exp(b[hh][lo:lo + HGRN_SUB, :] - ref)
                qt = q[hh][lo:lo + HGRN_SUB, :] * grow
                dec = jnp.exp(jnp.minimum(ref - b[hh], EXP_CLAMP))
                kd = k[hh] * dec
                scores[hh].append(_dot(qt, kd, _NT))
                da_i = da[hh][lo:lo + HGRN_SUB, :]
                dq_blocks[hh].append(_dot_f32(da_i, kd, _NN) * grow)
                dk[hh] = dk[hh] + _dot_f32(da_i, qt, _TN) * dec
        a = [jnp.where(lower, jnp.concatenate(sc, axis=0), 0.0) for sc in scores]
        dv = [_dot(a[hh], d_o[hh], _TN) + dv_state[hh] for hh in hs]
        dq = [dq_inter[hh] + jnp.concatenate(dq_blocks[hh], axis=0) for hh in hs]
        db_last = [jnp.sum(k[hh] * dk_state[hh], axis=0, keepdims=True)
                   + ebl[hh] * jnp.sum(s0t[hh] * ds1t[hh], axis=0, keepdims=True) for hh in hs]
        db = [q[hh] * dq[hh] - k[hh] * dk[hh] + jnp.where(last_row, db_last[hh], 0.0) for hh in hs]
        dg = [_dot_f32(upper, x) for x in db]
        df = [dg[hh] / f[hh] - dk[hh] for hh in hs]
        for hh in hs:
            sl = sls[hh]
            dzf_ref[:, sl] = (df[hh] * (1.0 - lb[hh]) * sf[hh] * (1.0 - sf[hh])).astype(BF16)
            dlb_ref[:, sl] += jnp.sum(df[hh] * (1.0 - sf[hh]), axis=0, keepdims=True)
            dzq_ref[:, sl] = (dq[hh] * sq[hh] * (1.0 + zq_v[hh] * (1.0 - sq[hh]))).astype(BF16)
            dzi_ref[:, sl] = dv[hh].astype(BF16)

    tile = pl.BlockSpec((c, hp * LANES), lambda h, i: (nc - 1 - i, h))
    out = jax.ShapeDtypeStruct((s, d), BF16)
    return pl.pallas_call(
        body, name=name, grid=(h_n // hp, nc),
        in_specs=[tile, tile, tile, pl.BlockSpec((2, hp * LANES), lambda h, i: (0, h)),
                  pl.BlockSpec((hp, None, LANES, LANES), lambda h, i: (h, nc - 1 - i, 0, 0)), tile,
                  pl.BlockSpec(memory_space=pl.ANY)],
        out_specs=[tile, tile, tile, pl.BlockSpec((1, hp * LANES), lambda h, i: (0, h))],
        out_shape=[out, out, out, jax.ShapeDtypeStruct((1, d), F32)],
        scratch_shapes=[pltpu.VMEM((hp, LANES, LANES), F32), pltpu.VMEM((hp, c, LANES), F32)],
        compiler_params=_params("parallel", "arbitrary"),
    )(zq, zf, zi, lb_logits, states, do, after)


LOG2E = 1.4426950408889634
LN2 = 0.6931471805599453
Q_PRESCALE = ATTN_SCALE * LOG2E


def _attn_tile(s):
    return min(1024, max(128, s // 2))


def _causal_pairs(n, q_major):
    pairs = [(i, j) for i in range(n) for j in range(i + 1)] if q_major else [(i, j) for j in range(n) for i in range(j, n)]
    return jnp.asarray([p[0] for p in pairs], jnp.int32), jnp.asarray([p[1] for p in pairs], jnp.int32)


def _scores(qn_ref, qr_ref, kn_ref, kr_ref):
    q = jnp.concatenate([qn_ref[...], qr_ref[...]], axis=1)
    k = jnp.concatenate([kn_ref[...], kr_ref[...]], axis=1)
    return q, k, _dot(q, k, _NT)


def _attn_fwd(qn, qr, kn, kr, v, *, name):
    s, t = qn.shape[0], _attn_tile(qn.shape[0])
    q_blk, k_blk = _causal_pairs(s // t, True)

    def body(qi_ref, kj_ref, qn_ref, qr_ref, kn_ref, kr_ref, v_ref, o_ref, lse_ref, m_sc, l_sc, acc_sc):
        p_id = pl.program_id(1)
        i, j = qi_ref[p_id], kj_ref[p_id]

        @pl.when(j == 0)
        def _():
            m_sc[...] = jnp.full_like(m_sc, -jnp.inf)
            l_sc[...] = jnp.zeros_like(l_sc)
            acc_sc[...] = jnp.zeros_like(acc_sc)

        def update(sc):
            m_prev = m_sc[...]
            m_new = jnp.maximum(m_prev, jnp.max(sc, axis=1, keepdims=True))
            alpha = jnp.exp2(m_prev - m_new)
            p = jnp.exp2(sc - m_new[:, :1])
            l_sc[...] = alpha * l_sc[...] + jnp.sum(p, axis=1, keepdims=True)
            acc_sc[...] = alpha * acc_sc[...] + _dot(p, v_ref[...], _NN)
            m_sc[...] = m_new

        @pl.when(j < i)
        def _():
            update(_scores(qn_ref, qr_ref, kn_ref, kr_ref)[2])

        @pl.when(j == i)
        def _():
            update(jnp.where(_tri(t, True), _scores(qn_ref, qr_ref, kn_ref, kr_ref)[2], -jnp.inf))
            o_ref[...] = (acc_sc[...] / l_sc[...]).astype(BF16)
            lse_ref[...] = m_sc[...] + jnp.log(l_sc[...]) * LOG2E

    q_spec = pl.BlockSpec((t, LANES), lambda h, p, qi, kj: (qi[p], h))
    k_spec = pl.BlockSpec((t, LANES), lambda h, p, qi, kj: (kj[p], h))
    kr_spec = pl.BlockSpec((t, LANES), lambda h, p, qi, kj: (kj[p], 0))
    stat = pltpu.VMEM((t, LANES), F32)
    return pl.pallas_call(
        body, name=name,
        grid_spec=pltpu.PrefetchScalarGridSpec(
            num_scalar_prefetch=2, grid=(MLA_HEADS, q_blk.shape[0]),
            in_specs=[q_spec, q_spec, k_spec, kr_spec, k_spec], out_specs=[q_spec, q_spec],
            scratch_shapes=[stat, stat, stat]),
        out_shape=[jax.ShapeDtypeStruct(qn.shape, BF16), jax.ShapeDtypeStruct(qn.shape, F32)],
        compiler_params=_params("parallel", "arbitrary"),
    )(q_blk, k_blk, qn, qr, kn, kr, v)


def _attn_bwd(qn, qr, kn, kr, v, do, lse, delta, *, name):
    s, t = qn.shape[0], _attn_tile(qn.shape[0])
    n = s // t
    q_blk, k_blk = _causal_pairs(n, False)

    def body(qi_ref, kj_ref, qn_ref, qr_ref, kn_ref, kr_ref, v_ref, do_ref, lse_ref, delta_ref,
             dqn_ref, dqr_ref, dkn_ref, dv_ref, dkr_ref, dk_sc, dv_sc):
        p_id = pl.program_id(1)
        i, j = qi_ref[p_id], kj_ref[p_id]

        @pl.when(p_id == 0)
        def _():
            dqn_ref[...] = jnp.zeros_like(dqn_ref)
            dqr_ref[...] = jnp.zeros_like(dqr_ref)

        @pl.when(i == j)
        def _():
            dk_sc[...] = jnp.zeros_like(dk_sc)
            dv_sc[...] = jnp.zeros_like(dv_sc)

        def accumulate(q, k, sc):
            p = jnp.exp2(sc - lse_ref[...][:, :1])
            d_o = do_ref[...]
            ds = (p * (_dot(d_o, v_ref[...], _NT) - delta_ref[...][:, :1])).astype(BF16)
            dv_sc[...] += _dot(p, d_o, _TN)
            dk_sc[...] += _dot(ds, q, _TN)
            dq = _dot(ds, k, _NN) * ATTN_SCALE
            rows = pl.ds(pl.multiple_of(i * t, t), t)
            dqn_ref[rows, :] += dq[:, :LANES]
            dqr_ref[rows, :] += dq[:, LANES:]

        @pl.when(j < i)
        def _():
            accumulate(*_scores(qn_ref, qr_ref, kn_ref, kr_ref))

        @pl.when(j == i)
        def _():
            q, k, sc = _scores(qn_ref, qr_ref, kn_ref, kr_ref)
            accumulate(q, k, jnp.where(_tri(t, True), sc, -jnp.inf))

        @pl.when(i == n - 1)
        def _():
            dkn_ref[...] = (dk_sc[:, :LANES] * LN2).astype(BF16)
            dkr_ref[...] = dk_sc[:, LANES:] * LN2
            dv_ref[...] = dv_sc[...].astype(BF16)

    q_spec = pl.BlockSpec((t, LANES), lambda h, p, qi, kj: (qi[p], h))
    k_spec = pl.BlockSpec((t, LANES), lambda h, p, qi, kj: (kj[p], h))
    kr_spec = pl.BlockSpec((t, LANES), lambda h, p, qi, kj: (kj[p], 0))
    head_spec = pl.BlockSpec((s, LANES), lambda h, p, qi, kj: (0, h))
    f32_out, bf16_out = jax.ShapeDtypeStruct(qn.shape, F32), jax.ShapeDtypeStruct(qn.shape, BF16)
    return pl.pallas_call(
        body, name=name,
        grid_spec=pltpu.PrefetchScalarGridSpec(
            num_scalar_prefetch=2, grid=(MLA_HEADS, q_blk.shape[0]),
            in_specs=[q_spec, q_spec, k_spec, kr_spec, k_spec, q_spec, q_spec, q_spec],
            out_specs=[head_spec, head_spec, k_spec, k_spec, k_spec],
            scratch_shapes=[pltpu.VMEM((t, 2 * LANES), F32), pltpu.VMEM((t, LANES), F32)]),
        out_shape=[f32_out, f32_out, bf16_out, bf16_out, f32_out],
        compiler_params=_params("parallel", "arbitrary"),
    )(q_blk, k_blk, qn, qr, kn, kr, v, do, lse, delta)


def _exchange(arrs, *, scatter, name):
    n = len(arrs)
    out_shape = [jax.ShapeDtypeStruct(a.shape if scatter else (N_DEV, *a.shape), a.dtype) for a in arrs]

    def body(*refs):
        ins, outs = refs[:n], refs[n:2 * n]
        send_sems, recv_sems, local_sems = refs[2 * n:]
        x, y, c = lax.axis_index("x"), lax.axis_index("y"), lax.axis_index("c")
        me = 4 * x + 2 * y + c
        copies = []
        for k in range(n):
            local = pltpu.make_async_copy(ins[k].at[me] if scatter else ins[k], outs[k].at[me], local_sems.at[k])
            local.start()
            copies.append(local)
            for d in range(1, N_DEV):
                px, py, pc = (x + (d >> 2)) % 2, (y + ((d >> 1) & 1)) % 2, (c + (d & 1)) % 2
                peer = 4 * px + 2 * py + pc
                remote = pltpu.make_async_remote_copy(
                    src_ref=ins[k].at[peer] if scatter else ins[k], dst_ref=outs[k].at[me],
                    send_sem=send_sems.at[k, d - 1], recv_sem=recv_sems.at[k, d - 1],
                    device_id=(px, py, pc), device_id_type=pl.DeviceIdType.MESH)
                remote.start()
                copies.append(remote)
        for cp in copies:
            cp.wait()

    any_spec = pl.BlockSpec(memory_space=pl.ANY)
    return pl.pallas_call(
        body, name=name, in_specs=[any_spec] * n, out_specs=[any_spec] * n, out_shape=out_shape,
        scratch_shapes=[pltpu.SemaphoreType.DMA((n, N_DEV - 1)), pltpu.SemaphoreType.DMA((n, N_DEV - 1)),
                        pltpu.SemaphoreType.DMA((n,))],
    )(*arrs)


def _peers(x, y, c):
    out = []
    for d in range(1, N_DEV):
        px, py, pc = (x + (d >> 2)) % 2, (y + ((d >> 1) & 1)) % 2, (c + (d & 1)) % 2
        out.append(((px, py, pc), 4 * px + 2 * py + pc))
    return out


def _exchange_copies(ins, lands, send_sems, recv_sems, scatter):
    x, y, c = lax.axis_index("x"), lax.axis_index("y"), lax.axis_index("c")
    me = 4 * x + 2 * y + c
    local, remote = [], []
    for k in range(len(ins)):
        local.append(pltpu.make_async_copy(ins[k].at[me] if scatter else ins[k], lands[k].at[me],
                                           recv_sems.at[k * N_DEV + N_DEV - 1]))
        for d, (coords, peer) in enumerate(_peers(x, y, c)):
            remote.append(pltpu.make_async_remote_copy(
                src_ref=ins[k].at[peer] if scatter else ins[k], dst_ref=lands[k].at[me],
                send_sem=send_sems.at[k * N_DEV + d], recv_sem=recv_sems.at[k * N_DEV + d],
                device_id=coords, device_id_type=pl.DeviceIdType.MESH))
    return local, remote


def _exchange_start(arrs, *, scatter, name, after=None):
    n = len(arrs)
    hbm = pl.BlockSpec(memory_space=pltpu.HBM)
    sem = pl.BlockSpec(memory_space=pltpu.SEMAPHORE)
    lands = [lax.empty(a.shape if scatter else (N_DEV, *a.shape), a.dtype) for a in arrs]

    def body(*refs):
        ins, land_refs = refs[:n], refs[n:2 * n]
        first_out = 2 * n + (after is not None)
        send_sems, recv_sems, token = refs[first_out], refs[first_out + 1], refs[-1]
        local, remote = _exchange_copies(ins, land_refs, send_sems, recv_sems, scatter)
        for cp in local + remote:
            cp.start()
        token[...] = jnp.zeros_like(token)

    operands = [pltpu.with_memory_space_constraint(a, pltpu.HBM) for a in list(arrs) + lands]
    behind = [] if after is None else [after]
    res = pl.pallas_call(
        body, name=name,
        out_shape=(pltpu.SemaphoreType.DMA((n * N_DEV,)), pltpu.SemaphoreType.DMA((n * N_DEV,)),
                   *[pltpu.HBM(o.shape, o.dtype) for o in operands], jax.ShapeDtypeStruct((8, LANES), F32)),
        in_specs=[hbm] * (2 * n) + [pl.BlockSpec(memory_space=pl.ANY)] * len(behind),
        out_specs=(sem, sem, *[hbm] * (2 * n), pl.BlockSpec(memory_space=pltpu.VMEM)),
        input_output_aliases={i: 2 + i for i in range(2 * n)},
        compiler_params=pltpu.CompilerParams(has_side_effects=pltpu.SideEffectType.DATAFLOW_SIDE_EFFECTING),
    )(*operands, *behind)
    return (res[0], res[1], list(res[2:2 + n]), list(res[2 + n:2 + 2 * n]), scatter), res[-1]


def _exchange_wait(state, after, *, name):
    send_sems, recv_sems, ins, lands, scatter = state
    n = len(ins)
    hbm = pl.BlockSpec(memory_space=pltpu.HBM)
    sem = pl.BlockSpec(memory_space=pltpu.SEMAPHORE)

    def body(*refs):
        in_refs, land_refs = refs[:n], refs[n:2 * n]
        local, remote = _exchange_copies(in_refs, land_refs, refs[2 * n], refs[2 * n + 1], scatter)
        for cp in local:
            cp.wait()
        for cp in remote:
            cp.wait_send()
            cp.wait_recv()

    res = pl.pallas_call(
        body, name=name, out_shape=tuple(pltpu.HBM(o.shape, o.dtype) for o in ins + lands),
        in_specs=[hbm] * (2 * n) + [sem, sem, pl.BlockSpec(memory_space=pl.ANY)], out_specs=tuple([hbm] * (2 * n)),
        input_output_aliases={i: i for i in range(2 * n)},
        compiler_params=pltpu.CompilerParams(has_side_effects=pltpu.SideEffectType.DATAFLOW_SIDE_EFFECTING),
    )(*ins, *lands, send_sems, recv_sems, after)
    return list(res[n:])


def _adam(w, terms, m, v, *, name):
    r, c = w.shape
    n = terms.shape[0]
    tr = min(r, 128)
    assert r % tr == 0

    def body(w_ref, t_ref, m_ref, v_ref, g_out, d_out, m_out, v_out):
        g = t_ref[0]
        for s in range(1, n):
            g = g + t_ref[s]
        m1 = ADAM_B1 * m_ref[...] + (1.0 - ADAM_B1) * g
        v1 = ADAM_B2 * v_ref[...] + (1.0 - ADAM_B2) * jnp.square(g)
        m_hat = m1 / (1.0 - ADAM_B1 ** ADAM_STEP)
        v_hat = v1 / (1.0 - ADAM_B2 ** ADAM_STEP)
        g_out[...] = g
        d_out[...] = -ADAM_LR * (m_hat / (jnp.sqrt(v_hat) + ADAM_EPS) + ADAM_WD * w_ref[...])
        m_out[...] = m1
        v_out[...] = v1

    spec = pl.BlockSpec((tr, c), lambda i: (i, 0))
    out = jax.ShapeDtypeStruct((r, c), F32)
    return pl.pallas_call(
        body, name=name, grid=(r // tr,),
        in_specs=[spec, pl.BlockSpec((n, tr, c), lambda i: (0, i, 0)), spec, spec], out_specs=[spec] * 4,
        out_shape=[out] * 4, compiler_params=_params("parallel"),
    )(w, terms, m, v)


def _sum_terms(terms, *, name):
    n, _, p = terms.shape

    def body(t_ref, o_ref):
        acc = t_ref[0]
        for s in range(1, n):
            acc = acc + t_ref[s]
        o_ref[...] = acc

    return pl.pallas_call(body, name=name, out_shape=jax.ShapeDtypeStruct((1, p), F32))(terms)


def _lb_logits_grad(dlb, logits, *, name):
    def body(dlb_ref, l_ref, o_ref):
        lb = _lower_bound(l_ref[...])
        d0 = dlb_ref[...] * lb * (1.0 - lb)
        o_ref[...] = jnp.concatenate([d0, -d0], axis=0)

    return pl.pallas_call(body, name=name, out_shape=jax.ShapeDtypeStruct(logits.shape, F32))(dlb, logits)


def _silu_grad(z):
    sg = _sigmoid(z)
    return sg * (1.0 + z * (1.0 - sg))


def _head_norm_gate(o, zg, gn):
    outs = []
    for h in range(HGRN_HEADS):
        sl = slice(h * LANES, (h + 1) * LANES)
        zg_h = zg[:, sl]
        outs.append(_rms(o[:, sl], gn) * (zg_h * _sigmoid(zg_h)))
    return (jnp.concatenate(outs, axis=1),)


def _head_norm_gate_bwd(o, zg, dm, gn):
    do_parts, dzg_parts, dgn = [], [], jnp.zeros((1, LANES), F32)
    for h in range(HGRN_HEADS):
        sl = slice(h * LANES, (h + 1) * LANES)
        o_h, zg_h, dm_h = o[:, sl], zg[:, sl], dm[:, sl]
        gate = zg_h * _sigmoid(zg_h)
        do_h, dgn_h = _rms_bwd(o_h, gn, dm_h * gate)
        dgn = dgn + dgn_h
        do_parts.append(do_h)
        dzg_parts.append(dm_h * _rms(o_h, gn) * _silu_grad(zg_h))
    return jnp.concatenate(do_parts, axis=1), jnp.concatenate(dzg_parts, axis=1), dgn


def _rope_slabs(x, t_c, t_s1, t_s2, transpose):
    fn = _rope_t if transpose else _rope
    return jnp.concatenate(
        [fn(x[:, h * LANES:(h + 1) * LANES], t_c, t_s1, t_s2) for h in range(x.shape[1] // LANES)], axis=1)


def _loss_head(h, tgt, w):
    d = h.shape[1]
    r = lax.rsqrt(jnp.mean(h * h, axis=-1, keepdims=True) + EPS)
    xh = h * r
    err = xh * w - tgt
    loss = 0.5 * jnp.sum(jnp.mean(err * err, axis=-1, keepdims=True), axis=0, keepdims=True)
    dy = err / d
    dxh = dy * w
    dh = r * (dxh - xh * jnp.mean(dxh * xh, axis=-1, keepdims=True))
    return dh, jnp.sum(dy * xh, axis=0, keepdims=True), jnp.broadcast_to(loss, (1, LANES))


def _mlp_fwd(h, norm, w_up, w_down, tag):
    d = h.shape[1]
    xn = _rowcall(lambda x, w: (_rms(x, w),), [h], [norm], [(d, BF16)], [], name=f"{tag}_norm")[0]
    u, act = _mm(xn, w_up, mode="nn", epilogue="relu2", name=f"{tag}_up")
    return _mm(act, w_down, mode="nn", add=h, name=f"{tag}_down"), (h, xn, u, act)


def _mlp_bwd(dh_out, saved, norm, w_up, w_down, tag, after=None):
    h, xn, u, act = saved
    d = h.shape[1]
    du = _mm(dh_out, w_down, mode="nt", epilogue="relu2_bwd", aux=u, out_dtype=BF16, after=after,
             name=f"{tag}_bwd_du")
    dw_down = _mm(act, dh_out, mode="tn", name=f"{tag}_bwd_wdown")
    dxn = _mm(du, w_up, mode="nt", name=f"{tag}_bwd_dxn")
    dw_up = _mm(xn, du, mode="tn", name=f"{tag}_bwd_wup")

    def norm_bwd(x, dy, dres, w):
        dx, dw = _rms_bwd(x, w, dy)
        return dx + dres, dw

    dh, dnorm = _rowcall(norm_bwd, [h, dxn, dh_out], [norm], [(d, F32)], [d], name=f"{tag}_bwd_norm")
    return dh, dnorm, dw_up, dw_down


def _row_major(g):
    return g.reshape(g.shape[0] * g.shape[1], g.shape[2])


def _col_major(g):
    return jnp.transpose(g, (1, 0, 2)).reshape(g.shape[1], g.shape[0] * g.shape[2])


def _col_terms(dw):
    k, n = dw.shape
    return jnp.transpose(dw.reshape(k, N_DEV, n // N_DEV), (1, 0, 2))


def _row_terms(dw):
    return dw.reshape(N_DEV, dw.shape[0] // N_DEV, dw.shape[1])


def kernel(x, hgrn_norm, hgrn_w_q, hgrn_w_f, hgrn_w_i, hgrn_w_g, hgrn_g_norm, hgrn_w_o, hgrn_lb_logits, mla_norm, mla_w_dq, mla_q_norm, mla_w_uq, mla_w_o, kv_in_norm, kv_w_dkv, kv_norm, kv_w_uk, kv_w_uv, mlp_norm, mlp_w_up, mlp_w_down, final_norm, loss_target, m_hgrn_norm, m_hgrn_w_q, m_hgrn_w_f, m_hgrn_w_i, m_hgrn_w_g, m_hgrn_g_norm, m_hgrn_w_o, m_hgrn_lb_logits, m_mla_norm, m_mla_w_dq, m_mla_q_norm, m_mla_w_uq, m_mla_w_o, m_kv_in_norm, m_kv_w_dkv, m_kv_norm, m_kv_w_uk, m_kv_w_uv, m_mlp_norm, m_mlp_w_up, m_mlp_w_down, m_final_norm, v_hgrn_norm, v_hgrn_w_q, v_hgrn_w_f, v_hgrn_w_i, v_hgrn_w_g, v_hgrn_g_norm, v_hgrn_w_o, v_hgrn_lb_logits, v_mla_norm, v_mla_w_dq, v_mla_q_norm, v_mla_w_uq, v_mla_w_o, v_kv_in_norm, v_kv_w_dkv, v_kv_norm, v_kv_w_uk, v_kv_w_uv, v_mlp_norm, v_mlp_w_up, v_mlp_w_down, v_final_norm):
    given = dict(locals())
    weight_names = ["hgrn_norm", "hgrn_w_q", "hgrn_w_f", "hgrn_w_i", "hgrn_w_g", "hgrn_g_norm", "hgrn_w_o",
                    "hgrn_lb_logits", "mla_norm", "mla_w_dq", "mla_q_norm", "mla_w_uq", "mla_w_o", "kv_in_norm",
                    "kv_w_dkv", "kv_norm", "kv_w_uk", "kv_w_uv", "mlp_norm", "mlp_w_up", "mlp_w_down", "final_norm"]
    me = 4 * lax.axis_index("x") + 2 * lax.axis_index("y") + lax.axis_index("c")
    xs, tgt = x[0], loss_target[0]
    seq, d_model = xs.shape
    n_heads, hd = MLA_HEADS, LANES

    big_local = {
        "hgrn_w_q": hgrn_w_q[0], "hgrn_w_f": hgrn_w_f[0], "hgrn_w_i": hgrn_w_i[0], "hgrn_w_g": hgrn_w_g[0],
        "hgrn_w_o": hgrn_w_o[0], "mla_w_dq": mla_w_dq[0], "mla_w_uq": mla_w_uq[0], "mla_w_o": mla_w_o[0],
        "kv_w_dkv": kv_w_dkv, "kv_w_uk": kv_w_uk, "kv_w_uv": kv_w_uv,
        "mlp_w_up0": mlp_w_up[0], "mlp_w_up1": mlp_w_up[1], "mlp_w_down0": mlp_w_down[0], "mlp_w_down1": mlp_w_down[1],
    }
    big_names = list(big_local)
    col_sharded = {"mla_w_uq", "kv_w_uk", "kv_w_uv", "mlp_w_up0", "mlp_w_up1"}
    vec_local = jnp.concatenate([hgrn_norm, hgrn_lb_logits], axis=0)
    first_names = ["hgrn_w_q", "hgrn_w_f", "hgrn_w_i", "hgrn_w_g"]
    later_names = {"mlp0": ["hgrn_w_o", "mlp_w_up0", "mlp_w_down0"],
                   "mla": ["kv_w_dkv", "kv_w_uk", "kv_w_uv", "mla_w_dq", "mla_w_uq", "mla_w_o"],
                   "mlp1": ["mlp_w_up1", "mlp_w_down1"]}

    def unshard(names, arrays):
        return {k: (_col_major(a) if k in col_sharded else _row_major(a)) for k, a in zip(names, arrays)}

    first_state, token = _exchange_start([big_local[k].astype(BF16) for k in first_names] + [vec_local], scatter=False,
                                         name="gather_first_start")
    gather_state = {}
    for tag, names in later_names.items():
        gather_state[tag], token = _exchange_start([big_local[k].astype(BF16) for k in names], scatter=False,
                                                   after=token, name=f"gather_{tag}_start")

    def gather_wait(tag, after):
        w.update(unshard(later_names[tag], _exchange_wait(gather_state[tag], after, name=f"gather_{tag}_wait")))

    gathered = _exchange_wait(first_state, token, name="gather_first_wait")
    w = unshard(first_names, gathered[:-1])
    vec_full = jnp.transpose(gathered[-1], (1, 0, 2)).reshape(3, d_model)
    hgrn_norm_full, lb_logits_full = vec_full[0:1], vec_full[1:3]
    t_c, t_s1, t_s2 = _rope_tables(seq)
    kv_lora = kv_w_uk.shape[0]

    xn0 = _rowcall(lambda a, g: (_rms(a, g),), [xs], [hgrn_norm_full], [(d_model, BF16)], [], name="hgrn_norm")[0]
    zq = _mm(xn0, w["hgrn_w_q"], mode="nn", name="hgrn_zq")
    zf = _mm(xn0, w["hgrn_w_f"], mode="nn", name="hgrn_zf")
    zi = _mm(xn0, w["hgrn_w_i"], mode="nn", name="hgrn_zi")
    zg = _mm(xn0, w["hgrn_w_g"], mode="nn", name="hgrn_zg")
    o_rec, states = _hgrn_fwd(zq, zf, zi, lb_logits_full, name="hgrn_fwd")
    mixed = _rowcall(_head_norm_gate, [o_rec, zg], [hgrn_g_norm], [(d_model, BF16)], [], name="hgrn_gate")[0]
    gather_wait("mlp0", mixed)
    h1 = _mm(mixed, w["hgrn_w_o"], mode="nn", add=xs, name="hgrn_out")
    h2, mlp0_saved = _mlp_fwd(h1, mlp_norm[0:1], w["mlp_w_up0"], w["mlp_w_down0"], "mlp0")
    gather_wait("mla", h2)
    w_uq3 = w["mla_w_uq"].reshape(-1, n_heads, MLA_NOPE + MLA_ROPE)
    w_uq_nope = w_uq3[:, :, :MLA_NOPE].reshape(-1, n_heads * hd)
    w_uq_rope = jnp.pad(w_uq3[:, :, MLA_NOPE:], ((0, 0), (0, 0), (0, hd - MLA_ROPE))).reshape(-1, n_heads * hd)
    w_dkv_pad = jnp.pad(w["kv_w_dkv"], ((0, 0), (0, kv_lora + hd - w["kv_w_dkv"].shape[1])))

    hn, xn2 = _rowcall(lambda a, g1, g2: (_rms(a, g1), _rms(a, g2)), [h2], [kv_in_norm[None, :], mla_norm],
                       [(d_model, BF16), (d_model, BF16)], [], name="kv_mla_norm")
    ckr = _mm(hn, w_dkv_pad, mode="nn", name="kv_down")

    def kv_latent(c_all, tc, ts1, ts2, g):
        return _rms(c_all[:, :kv_lora], g), _rope(c_all[:, kv_lora:], tc, ts1, ts2)

    c_kv, kr = _rowcall(kv_latent, [ckr, t_c, t_s1, t_s2], [kv_norm[None, :]], [(kv_lora, BF16), (hd, BF16)], [],
                        name="kv_latent")
    kn = _mm(c_kv, w["kv_w_uk"], mode="nn", out_dtype=BF16, name="kv_up_k")
    vv = _mm(c_kv, w["kv_w_uv"], mode="nn", out_dtype=BF16, name="kv_up_v")
    cq_pre = _mm(xn2, w["mla_w_dq"], mode="nn", name="q_down")
    c_q = _rowcall(lambda a, g: (_rms(a, g),), [cq_pre], [mla_q_norm], [(cq_pre.shape[1], BF16)], [], name="q_norm")[0]
    qn = _mm(c_q, w_uq_nope, mode="nn", out_dtype=BF16, scale=Q_PRESCALE, name="q_up_nope")
    qr_pre = _mm(c_q, w_uq_rope, mode="nn", name="q_up_rope")
    qr = _rowcall(lambda a, tc, ts1, ts2: (_rope_slabs(a, tc, ts1, ts2, False) * Q_PRESCALE,), [qr_pre, t_c, t_s1, t_s2], [],
                  [(n_heads * hd, BF16)], [], name="q_rope")[0]
    o_att, lse = _attn_fwd(qn, qr, kn, kr, vv, name="attn_fwd")
    h3 = _mm(o_att, w["mla_w_o"], mode="nn", add=h2, name="attn_out")
    gather_wait("mlp1", h3)
    h4, mlp1_saved = _mlp_fwd(h3, mlp_norm[1:2], w["mlp_w_up1"], w["mlp_w_down1"], "mlp1")
    dh4, g_final_norm, loss_part = _rowcall(_loss_head, [h4, tgt], [final_norm[None, :]], [(d_model, F32)],
                                            [d_model, LANES], name="loss_head")

    g = {}
    groups = {"mlp1": ["mlp_w_up1", "mlp_w_down1"],
              "mla": ["mla_w_o", "mla_w_uq", "mla_w_dq", "kv_w_uk", "kv_w_uv", "kv_w_dkv"],
              "mlp0": ["mlp_w_up0", "mlp_w_down0"],
              "hgrn_out": ["hgrn_w_o", "hgrn_w_g"],
              "hgrn_in": ["hgrn_w_q", "hgrn_w_f", "hgrn_w_i"]}
    scatter_state = {}

    def scatter_start(tag, after=None):
        scatter_state[tag], tok = _exchange_start(
            [(_col_terms if k in col_sharded else _row_terms)(g[k]) for k in groups[tag]], scatter=True, after=after,
            name=f"scatter_{tag}_start")
        return tok

    dh3, g_mlp_norm1, g["mlp_w_up1"], g["mlp_w_down1"] = _mlp_bwd(
        dh4, mlp1_saved, mlp_norm[1:2], w["mlp_w_up1"], w["mlp_w_down1"], "mlp1")
    d_oatt = _mm(dh3, w["mla_w_o"], mode="nt", out_dtype=BF16, after=scatter_start("mlp1"), name="attn_out_bwd_x")
    g["mla_w_o"] = _mm(o_att, dh3, mode="tn", name="attn_out_bwd_w")

    def head_delta(do, o):
        prod = do.astype(F32) * o.astype(F32)
        return (jnp.concatenate([jnp.broadcast_to(jnp.sum(prod[:, h * hd:(h + 1) * hd], axis=1, keepdims=True),
                                                  (prod.shape[0], hd)) for h in range(n_heads)], axis=1),)

    delta = _rowcall(head_delta, [d_oatt, o_att], [], [(n_heads * hd, F32)], [], name="attn_delta")[0]
    dqn, dqr, dkn, dvv, dkr = _attn_bwd(qn, qr, kn, kr, vv, d_oatt, lse, delta, name="attn_bwd")
    dqr_pre = _rowcall(lambda a, tc, ts1, ts2: (_rope_slabs(a, tc, ts1, ts2, True),), [dqr, t_c, t_s1, t_s2], [],
                       [(n_heads * hd, BF16)], [], name="q_rope_bwd")[0]
    dcq = _mm(dqn, w_uq_nope, mode="nt", name="q_up_nope_bwd_x")
    dcq = _mm(dqr_pre, w_uq_rope, mode="nt", add=dcq, name="q_up_rope_bwd_x")
    g_uq_nope = _mm(c_q, dqn, mode="tn", name="q_up_nope_bwd_w")
    g_uq_rope = _mm(c_q, dqr_pre, mode="tn", name="q_up_rope_bwd_w")
    q_lora = c_q.shape[1]
    g["mla_w_uq"] = jnp.concatenate([g_uq_nope.reshape(q_lora, n_heads, hd),
                                     g_uq_rope.reshape(q_lora, n_heads, hd)[:, :, :MLA_ROPE]], axis=2).reshape(q_lora, -1)
    dcq_pre, g_q_norm = _rowcall(lambda a, dy, gq: _rms_bwd(a, gq, dy), [cq_pre, dcq], [mla_q_norm],
                                 [(q_lora, BF16)], [q_lora], name="q_norm_bwd")
    dxn2 = _mm(dcq_pre, w["mla_w_dq"], mode="nt", name="q_down_bwd_x")
    g["mla_w_dq"] = _mm(xn2, dcq_pre, mode="tn", name="q_down_bwd_w")

    dc_kv = _mm(dkn, w["kv_w_uk"], mode="nt", name="kv_up_k_bwd_x")
    dc_kv = _mm(dvv, w["kv_w_uv"], mode="nt", add=dc_kv, name="kv_up_v_bwd_x")
    g["kv_w_uk"] = _mm(c_kv, dkn, mode="tn", name="kv_up_k_bwd_w")
    g["kv_w_uv"] = _mm(c_kv, dvv, mode="tn", name="kv_up_v_bwd_w")

    def kv_latent_bwd(c_all, dc, dkr_heads, tc, ts1, ts2, gk):
        dlat, dgk = _rms_bwd(c_all[:, :kv_lora], gk, dc)
        dkr_slab = dkr_heads[:, :hd]
        for h in range(1, n_heads):
            dkr_slab = dkr_slab + dkr_heads[:, h * hd:(h + 1) * hd]
        return jnp.concatenate([dlat, _rope_t(dkr_slab, tc, ts1, ts2)], axis=1), dgk

    dckr, g_kv_norm = _rowcall(kv_latent_bwd, [ckr, dc_kv, dkr, t_c, t_s1, t_s2], [kv_norm[None, :]],
                               [(kv_lora + hd, BF16)], [kv_lora], name="kv_latent_bwd")
    dhn = _mm(dckr, w_dkv_pad, mode="nt", name="kv_down_bwd_x")
    g["kv_w_dkv"] = _mm(hn, dckr, mode="tn", name="kv_down_bwd_w")[:, :kv_w_dkv.shape[1]]

    def kv_mla_norm_bwd(a, d1, d2, dres, g1, g2):
        dx1, dw1 = _rms_bwd(a, g1, d1)
        dx2, dw2 = _rms_bwd(a, g2, d2)
        return dx1 + dx2 + dres, dw1, dw2

    dh2, g_kv_in_norm, g_mla_norm = _rowcall(kv_mla_norm_bwd, [h2, dhn, dxn2, dh3], [kv_in_norm[None, :], mla_norm],
                                             [(d_model, F32)], [d_model, d_model], name="kv_mla_norm_bwd")
    dh1, g_mlp_norm0, g["mlp_w_up0"], g["mlp_w_down0"] = _mlp_bwd(
        dh2, mlp0_saved, mlp_norm[0:1], w["mlp_w_up0"], w["mlp_w_down0"], "mlp0", after=scatter_start("mla"))

    dmixed = _mm(dh1, w["hgrn_w_o"], mode="nt", after=scatter_start("mlp0"), name="hgrn_out_bwd_x")
    g["hgrn_w_o"] = _mm(mixed, dh1, mode="tn", name="hgrn_out_bwd_w")
    do_rec, dzg, g_g_norm = _rowcall(_head_norm_gate_bwd, [o_rec, zg, dmixed], [hgrn_g_norm],
                                     [(d_model, F32), (d_model, BF16)], [hd], name="hgrn_gate_bwd")
    g["hgrn_w_g"] = _mm(xn0, dzg, mode="tn", name="hgrn_w_g_bwd_w")
    dzq, dzf, dzi, g_lb = _hgrn_bwd(zq, zf, zi, lb_logits_full, states, do_rec, scatter_start("hgrn_out"),
                                    name="hgrn_bwd")
    dxn0 = _mm(dzg, w["hgrn_w_g"], mode="nt", name="hgrn_w_g_bwd_x")
    for nm, dz in (("hgrn_w_q", dzq), ("hgrn_w_f", dzf), ("hgrn_w_i", dzi)):
        dxn0 = _mm(dz, w[nm], mode="nt", add=dxn0, name=f"{nm}_bwd_x")
        g[nm] = _mm(xn0, dz, mode="tn", name=f"{nm}_bwd_w")

    def in_norm_bwd(a, dy, dres, gw):
        dx, dw = _rms_bwd(a, gw, dy)
        return dx + dres, dw

    grad_x, g_hgrn_norm = _rowcall(in_norm_bwd, [xs, dxn0, dh1], [hgrn_norm_full], [(d_model, F32)], [d_model],
                                   name="hgrn_norm_bwd")

    small_parts = [g_hgrn_norm, g_lb, g_g_norm, g_mla_norm, g_q_norm, g_kv_in_norm, g_kv_norm, g_mlp_norm0,
                   g_mlp_norm1, g_final_norm, loss_part]
    small_sizes = [p.shape[1] for p in small_parts]
    small_terms = _exchange([jnp.concatenate(small_parts, axis=1)], scatter=False, name="gather_small")[0]
    small_sum = _sum_terms(small_terms, name="sum_small")
    last = scatter_start("hgrn_in", after=small_sum)
    offs = [0]
    for sz in small_sizes:
        offs.append(offs[-1] + sz)
    (s_hgrn_norm, s_lb, s_g_norm, s_mla_norm, s_q_norm, s_kv_in_norm, s_kv_norm, s_mlp_norm0, s_mlp_norm1, s_final_norm,
     s_loss) = [small_sum[:, a:b] for a, b in zip(offs[:-1], offs[1:])]
    shard = hgrn_norm.shape[1]
    g_lb_logits = _lb_logits_grad(lax.dynamic_slice_in_dim(s_lb, me * shard, shard, axis=1), hgrn_lb_logits,
                                  name="lb_logits_grad")
    loss = s_loss[0, 0]

    res = {}
    for tag, names in groups.items():
        for k, t in zip(names, _exchange_wait(scatter_state[tag], last, name=f"scatter_{tag}_wait")):
            if k.startswith("mlp_w_"):
                base, layer = k[:-1], int(k[-1])
                wk, mk, vk = given[base][layer], given["m_" + base][layer], given["v_" + base][layer]
            else:
                wk, mk, vk = given[k], given["m_" + k], given["v_" + k]
            shape = wk.shape
            wk, mk, vk = (a.reshape(shape[-2], shape[-1]) for a in (wk, mk, vk))
            upd = _adam(wk, t, mk, vk, name=f"adam_{k}")
            last = upd[0]
            res[k] = [o.reshape(shape) for o in upd]
    for base in ("mlp_w_up", "mlp_w_down"):
        res[base] = [jnp.stack([res[base + "0"][i], res[base + "1"][i]], axis=0) for i in range(4)]

    small_grads = {
        "hgrn_norm": lax.dynamic_slice_in_dim(s_hgrn_norm, me * shard, shard, axis=1),
        "hgrn_g_norm": s_g_norm, "hgrn_lb_logits": g_lb_logits, "mla_norm": s_mla_norm, "mla_q_norm": s_q_norm,
        "kv_in_norm": s_kv_in_norm, "kv_norm": s_kv_norm,
        "mlp_norm": jnp.concatenate([s_mlp_norm0, s_mlp_norm1], axis=0), "final_norm": s_final_norm,
    }
    small_names = list(small_grads)

    def flat(a):
        return a.reshape(1, -1)

    packed = [jnp.concatenate([flat(src[pre + k]) for k in small_names], axis=1)
              for src, pre in ((given, ""), (small_grads, ""), (given, "m_"), (given, "v_"))]
    small_out = _adam(packed[0], packed[1][None], packed[2], packed[3], name="adam_small")
    off = 0
    for k in small_names:
        size = given[k].size
        res[k] = [o[:, off:off + size].reshape(given[k].shape) for o in small_out]
        off += size

    outs = [loss, grad_x[None]]
    for i in range(4):
        outs += [res[k][i] for k in weight_names]
    return tuple(outs)
```

```python
import functools

import jax
import jax.numpy as jnp
from jax import lax
from jax.experimental import pallas as pl
from jax.experimental.pallas import tpu as pltpu

F32 = jnp.float32
BF16 = jnp.bfloat16

EPS = 1e-6
LANES = 128
N_DEV = 8
V7X_VMEM_LIMIT_BYTES = 56 << 20
MM_PIPELINE_BYTES = 30 << 20
MM_ROW_TILE = 512
GRAD_WIRE_DTYPE = BF16

HGRN_HEADS = 8
HGRN_CHUNK = 64
HGRN_SUB = 16
HGRN_HEADS_PER_STEP = 8
EXP_CLAMP = 80.0
MLA_HEADS = 16
MLA_NOPE = 128
MLA_ROPE = 64
ROPE_THETA = 10000.0
ATTN_SCALE = (MLA_NOPE + MLA_ROPE) ** -0.5

ADAM_LR = 0.001
ADAM_B1 = 0.9
ADAM_B2 = 0.999
ADAM_EPS = 1e-08
ADAM_WD = 0.01
ADAM_STEP = 10

_NN = ((1,), (0,))
_NT = ((1,), (1,))
_TN = ((0,), (0,))


def _params(*sem):
    return pltpu.CompilerParams(dimension_semantics=sem, vmem_limit_bytes=V7X_VMEM_LIMIT_BYTES)


def _dot(a, b, dims):
    return lax.dot_general(a.astype(BF16), b.astype(BF16), (dims, ((), ())), preferred_element_type=F32)


def _dot_f32(a, b, dims=_NN):
    return lax.dot_general(a, b, (dims, ((), ())), precision=lax.Precision.HIGHEST, preferred_element_type=F32)


def _sigmoid(x):
    return 1.0 / (1.0 + jnp.exp(-x))


def _rms(x, w):
    r = lax.rsqrt(jnp.mean(x * x, axis=-1, keepdims=True) + EPS)
    return x * r * w


def _rms_bwd(x, w, dy):
    r = lax.rsqrt(jnp.mean(x * x, axis=-1, keepdims=True) + EPS)
    xh = x * r
    dw = jnp.sum(dy * xh, axis=0, keepdims=True)
    dxh = dy * w
    dx = r * (dxh - xh * jnp.mean(dxh * xh, axis=-1, keepdims=True))
    return dx, dw


def _mm_tiles(m, n, k, a_bytes, b_bytes, out_tile_bytes):
    tm = min(m, MM_ROW_TILE)
    for tn in (n, 2048, 1024, 512, 256, LANES):
        if tn <= n and n % tn == 0:
            if 2 * (tm * k * a_bytes + k * tn * b_bytes + tm * tn * out_tile_bytes) <= MM_PIPELINE_BYTES:
                return tm, tn
    return tm, min(n, LANES)


def _mm(a, b, *, mode, name, out_dtype=None, add=None, epilogue=None, aux=None, after=None, scale=None):
    if mode == "nn":
        (m, k), (k2, n) = a.shape, b.shape
    elif mode == "nt":
        (m, k), (n, k2) = a.shape, b.shape
    else:
        (k, m), (k2, n) = a.shape, b.shape
    assert k == k2, (name, a.shape, b.shape)
    if out_dtype is None:
        out_dtype = GRAD_WIRE_DTYPE if mode == "tn" else F32
    tile_bytes = sum(x.dtype.itemsize for x in (add, aux) if x is not None)
    tile_bytes += 6 if epilogue == "relu2" else jnp.dtype(out_dtype).itemsize
    tm, tn = _mm_tiles(m, n, k, a.dtype.itemsize, b.dtype.itemsize, tile_bytes)
    assert m % tm == 0 and n % tn == 0, (name, m, n)
    dims = {"nn": _NN, "nt": _NT, "tn": _TN}[mode]
    a_spec = pl.BlockSpec((k, tm), lambda i, j: (0, i)) if mode == "tn" else pl.BlockSpec((tm, k), lambda i, j: (i, 0))
    b_spec = pl.BlockSpec((tn, k), lambda i, j: (j, 0)) if mode == "nt" else pl.BlockSpec((k, tn), lambda i, j: (0, j))
    o_spec = pl.BlockSpec((tm, tn), lambda i, j: (i, j))
    operands, in_specs = [a, b], [a_spec, b_spec]
    for extra in (add, aux):
        if extra is not None:
            assert extra.shape == (m, n), (name, extra.shape)
            operands.append(extra)
            in_specs.append(o_spec)
    n_in = len(operands)
    if after is not None:
        operands.append(after)
        in_specs.append(pl.BlockSpec(memory_space=pl.ANY))
    if epilogue == "relu2":
        out_shape = [jax.ShapeDtypeStruct((m, n), F32), jax.ShapeDtypeStruct((m, n), BF16)]
        out_specs = [o_spec, o_spec]
    else:
        out_shape = jax.ShapeDtypeStruct((m, n), out_dtype)
        out_specs = o_spec

    def body(*refs):
        acc = _dot(refs[0][...], refs[1][...], dims)
        extras, outs = refs[2:n_in], refs[len(operands):]
        if scale is not None:
            acc = acc * scale
        if add is not None:
            acc = acc + extras[0][...]
        if epilogue == "relu2":
            outs[0][...] = acc
            outs[1][...] = jnp.square(jnp.maximum(acc, 0.0)).astype(BF16)
        elif epilogue == "relu2_bwd":
            outs[0][...] = (acc * (2.0 * jnp.maximum(extras[-1][...], 0.0))).astype(out_dtype)
        else:
            outs[0][...] = acc.astype(out_dtype)

    return pl.pallas_call(
        body, name=name, grid=(m // tm, n // tn), in_specs=in_specs, out_specs=out_specs, out_shape=out_shape,
        compiler_params=_params("parallel", "parallel"),
    )(*operands)


def _rowcall(fn, rows, consts, outs, accs, *, name, tr=256):
    s = rows[0].shape[0]
    tr = min(tr, s)
    assert s % tr == 0
    n_out = len(outs)
    in_specs = [pl.BlockSpec((tr, r.shape[1]), lambda i: (i, 0)) for r in rows]
    in_specs += [pl.BlockSpec(c.shape, lambda i: (0, 0)) for c in consts]
    out_shape = [jax.ShapeDtypeStruct((s, w), dt) for w, dt in outs] + [jax.ShapeDtypeStruct((1, w), F32) for w in accs]
    out_specs = [pl.BlockSpec((tr, w), lambda i: (i, 0)) for w, _ in outs] + [pl.BlockSpec((1, w), lambda i: (0, 0)) for w in accs]
    n_in = len(rows) + len(consts)

    def body(*refs):
        res = fn(*[r[...] for r in refs[:n_in]])
        out_refs = refs[n_in:]
        for ref, val in zip(out_refs[:n_out], res[:n_out]):
            ref[...] = val.astype(ref.dtype)
        i = pl.program_id(0)
        for ref, val in zip(out_refs[n_out:], res[n_out:]):
            @pl.when(i == 0)
            def _(ref=ref, val=val):
                ref[...] = val

            @pl.when(i > 0)
            def _(ref=ref, val=val):
                ref[...] += val

    return pl.pallas_call(
        body, name=name, grid=(s // tr,), in_specs=in_specs, out_specs=out_specs, out_shape=out_shape,
        compiler_params=_params("arbitrary" if accs else "parallel"),
    )(*rows, *consts)


def _rope_tables(seq):
    half = MLA_ROPE // 2
    inv_freq = ROPE_THETA ** (-jnp.arange(half, dtype=F32) / half)
    ang = jnp.arange(seq, dtype=F32)[:, None] * inv_freq[None, :]
    cos, sin, zero = jnp.cos(ang), jnp.sin(ang), jnp.zeros((seq, half), F32)
    t_c = jnp.concatenate([cos, cos, zero, zero], axis=1)
    t_s1 = jnp.concatenate([-sin, zero, zero, zero], axis=1)
    t_s2 = jnp.concatenate([zero, sin, zero, zero], axis=1)
    return t_c, t_s1, t_s2


def _rope(slab, t_c, t_s1, t_s2):
    return slab * t_c + pltpu.roll(slab, 96, 1) * t_s1 + pltpu.roll(slab, 32, 1) * t_s2


def _rope_t(d, t_c, t_s1, t_s2):
    return d * t_c + pltpu.roll(d * t_s1, 32, 1) + pltpu.roll(d * t_s2, 96, 1)


def _lower_bound(logits):
    l0, l1 = logits[0:1, :], logits[1:2, :]
    mx = jnp.maximum(l0, l1)
    e0, e1 = jnp.exp(l0 - mx), jnp.exp(l1 - mx)
    return e0 / (e0 + e1)


def _tri(n, lower):
    row = lax.broadcasted_iota(jnp.int32, (n, n), 0)
    col = lax.broadcasted_iota(jnp.int32, (n, n), 1)
    return (row >= col) if lower else (row <= col)


def _hgrn_fwd(zq, zf, zi, lb_logits, *, name):
    s, d = zq.shape
    h_n, c, hp = d // LANES, HGRN_CHUNK, HGRN_HEADS_PER_STEP
    nc = s // c

    def body(zq_ref, zf_ref, zi_ref, lb_ref, o_ref, st_ref, state_sc, b_sc):
        @pl.when(pl.program_id(1) == 0)
        def _():
            state_sc[...] = jnp.zeros_like(state_sc)

        lower = _tri(c, True).astype(F32)
        hs = range(hp)
        sls = [slice(hh * LANES, (hh + 1) * LANES) for hh in hs]
        lb = [_lower_bound(lb_ref[:, sl]) for sl in sls]
        zq_v = [zq_ref[:, sl] for sl in sls]
        q = [z * _sigmoid(z) for z in zq_v]
        f = [lb[hh] + (1.0 - lb[hh]) * _sigmoid(zf_ref[:, sls[hh]]) for hh in hs]
        g = [jnp.log(x) for x in f]
        k = [1.0 - x for x in f]
        v = [zi_ref[:, sl] for sl in sls]
        b = [_dot_f32(lower, x) for x in g]
        s0t = [state_sc[hh] for hh in hs]
        for hh in hs:
            st_ref[hh] = s0t[hh]
            b_sc[hh] = b[hh]
        o_inter = [_dot(q[hh] * jnp.exp(b[hh]), s0t[hh], _NT) for hh in hs]
        scores = [[] for _ in hs]
        for i in range(c // HGRN_SUB):
            lo = i * HGRN_SUB
            for hh in hs:
                ref = b_sc[hh, lo - 1:lo, :] if i > 0 else jnp.zeros((1, LANES), F32)
                qt = q[hh][lo:lo + HGRN_SUB, :] * jnp.exp(b[hh][lo:lo + HGRN_SUB, :] - ref)
                dec = jnp.exp(jnp.minimum(ref - b[hh], EXP_CLAMP))
                scores[hh].append(_dot(qt, k[hh] * dec, _NT))
        a = [jnp.where(_tri(c, True), jnp.concatenate(sc, axis=0), 0.0) for sc in scores]
        for hh in hs:
            o_ref[:, sls[hh]] = o_inter[hh] + _dot(a[hh], v[hh], _NN)
        bl = [b_sc[hh, c - 1:c, :] for hh in hs]
        for hh in hs:
            state_sc[hh] = s0t[hh] * jnp.exp(bl[hh]) + _dot(v[hh], k[hh] * jnp.exp(bl[hh] - b[hh]), _TN)

    tile = pl.BlockSpec((c, hp * LANES), lambda h, i: (i, h))
    return pl.pallas_call(
        body, name=name, grid=(h_n // hp, nc),
        in_specs=[tile, tile, tile, pl.BlockSpec((2, hp * LANES), lambda h, i: (0, h))],
        out_specs=[tile, pl.BlockSpec((hp, None, LANES, LANES), lambda h, i: (h, i, 0, 0))],
        out_shape=[jax.ShapeDtypeStruct((s, d), F32), jax.ShapeDtypeStruct((h_n, nc, LANES, LANES), F32)],
        scratch_shapes=[pltpu.VMEM((hp, LANES, LANES), F32), pltpu.VMEM((hp, c, LANES), F32)],
        compiler_params=_params("parallel", "arbitrary"),
    )(zq, zf, zi, lb_logits)


def _hgrn_bwd(zq, zf, zi, lb_logits, states, do, after, *, name):
    s, d = zq.shape
    h_n, c, hp = d // LANES, HGRN_CHUNK, HGRN_HEADS_PER_STEP
    nc = s // c

    def body(zq_ref, zf_ref, zi_ref, lb_ref, st_ref, do_ref, _, dzq_ref, dzf_ref, dzi_ref, dlb_ref, dstate_sc, b_sc):
        @pl.when(pl.program_id(1) == 0)
        def _():
            dstate_sc[...] = jnp.zeros_like(dstate_sc)
            dlb_ref[...] = jnp.zeros_like(dlb_ref)

        lower, upper = _tri(c, True), _tri(c, False).astype(F32)
        lower_f = lower.astype(F32)
        last_row = lax.broadcasted_iota(jnp.int32, (c, LANES), 0) == c - 1
        hs = range(hp)
        sls = [slice(hh * LANES, (hh + 1) * LANES) for hh in hs]
        lb = [_lower_bound(lb_ref[:, sl]) for sl in sls]
        zq_v = [zq_ref[:, sl] for sl in sls]
        sq = [_sigmoid(z) for z in zq_v]
        q = [zq_v[hh] * sq[hh] for hh in hs]
        sf = [_sigmoid(zf_ref[:, sl]) for sl in sls]
        f = [lb[hh] + (1.0 - lb[hh]) * sf[hh] for hh in hs]
        g = [jnp.log(x) for x in f]
        k = [1.0 - x for x in f]
        v = [zi_ref[:, sl] for sl in sls]
        d_o = [do_ref[:, sl] for sl in sls]
        b = [_dot_f32(lower_f, x) for x in g]
        s0t = [st_ref[hh] for hh in hs]
        ds1t = [dstate_sc[hh] for hh in hs]
        for hh in hs:
            b_sc[hh] = b[hh]
        bl = [b_sc[hh, c - 1:c, :] for hh in hs]
        eb = [jnp.exp(x) for x in b]
        ebl = [jnp.exp(x) for x in bl]
        dec_end = [jnp.exp(bl[hh] - b[hh]) for hh in hs]
        da = [jnp.where(lower, _dot(d_o[hh], v[hh], _NT), 0.0) for hh in hs]
        dq_inter = [_dot(d_o[hh], s0t[hh], _NN) * eb[hh] for hh in hs]
        dk_state = [_dot(v[hh], ds1t[hh], _NN) * dec_end[hh] for hh in hs]
        dv_state = [_dot(k[hh] * dec_end[hh], ds1t[hh], _NT) for hh in hs]
        for hh in hs:
            dstate_sc[hh] = ds1t[hh] * ebl[hh] + _dot(d_o[hh], q[hh] * eb[hh], _TN)
        dk = list(dk_state)
        scores, dq_blocks = [[] for _ in hs], [[] for _ in hs]
        for i in range(c // HGRN_SUB):
            lo = i * HGRN_SUB
            for hh in hs:
                ref = b_sc[hh, lo - 1:lo, :] if i > 0 else jnp.zeros((1, LANES), F32)
                grow = jnp.exp(b[hh][lo:lo + HGRN_SUB, :] - ref)
                qt = q[hh][lo:lo + HGRN_SUB, :] * grow
                dec = jnp.exp(jnp.minimum(ref - b[hh], EXP_CLAMP))
                kd = k[hh] * dec
                scores[hh].append(_dot(qt, kd, _NT))
                da_i = da[hh][lo:lo + HGRN_SUB, :]
                dq_blocks[hh].append(_dot_f32(da_i, kd, _NN) * grow)
                dk[hh] = dk[hh] + _dot_f32(da_i, qt, _TN) * dec
        a = [jnp.where(lower, jnp.concatenate(sc, axis=0), 0.0) for sc in scores]
        dv = [_dot(a[hh], d_o[hh], _TN) + dv_state[hh] for hh in hs]
        dq = [dq_inter[hh] + jnp.concatenate(dq_blocks[hh], axis=0) for hh in hs]
        db_last = [jnp.sum(k[hh] * dk_state[hh], axis=0, keepdims=True)
                   + ebl[hh] * jnp.sum(s0t[hh] * ds1t[hh], axis=0, keepdims=True) for hh in hs]
        db = [q[hh] * dq[hh] - k[hh] * dk[hh] + jnp.where(last_row, db_last[hh], 0.0) for hh in hs]
        dg = [_dot_f32(upper, x) for x in db]
        df = [dg[hh] / f[hh] - dk[hh] for hh in hs]
        for hh in hs:
            sl = sls[hh]
            dzf_ref[:, sl] = (df[hh] * (1.0 - lb[hh]) * sf[hh] * (1.0 - sf[hh])).astype(BF16)
            dlb_ref[:, sl] += jnp.sum(df[hh] * (1.0 - sf[hh]), axis=0, keepdims=True)
            dzq_ref[:, sl] = (dq[hh] * sq[hh] * (1.0 + zq_v[hh] * (1.0 - sq[hh]))).astype(BF16)
            dzi_ref[:, sl] = dv[hh].astype(BF16)

    tile = pl.BlockSpec((c, hp * LANES), lambda h, i: (nc - 1 - i, h))
    out = jax.ShapeDtypeStruct((s, d), BF16)
    return pl.pallas_call(
        body, name=name, grid=(h_n // hp, nc),
        in_specs=[tile, tile, tile, pl.BlockSpec((2, hp * LANES), lambda h, i: (0, h)),
                  pl.BlockSpec((hp, None, LANES, LANES), lambda h, i: (h, nc - 1 - i, 0, 0)), tile,
                  pl.BlockSpec(memory_space=pl.ANY)],
        out_specs=[tile, tile, tile, pl.BlockSpec((1, hp * LANES), lambda h, i: (0, h))],
        out_shape=[out, out, out, jax.ShapeDtypeStruct((1, d), F32)],
        scratch_shapes=[pltpu.VMEM((hp, LANES, LANES), F32), pltpu.VMEM((hp, c, LANES), F32)],
        compiler_params=_params("parallel", "arbitrary"),
    )(zq, zf, zi, lb_logits, states, do, after)


ATTN_SUB_ROWS = 256
LOG2E = 1.4426950408889634
LN2 = 0.6931471805599453
Q_PRESCALE = ATTN_SCALE * LOG2E


def _attn_tile(s):
    return min(1024, max(128, s // 2))


def _causal_pairs(n, q_major):
    pairs = [(i, j) for i in range(n) for j in range(i + 1)] if q_major else [(i, j) for j in range(n) for i in range(j, n)]
    return jnp.asarray([p[0] for p in pairs], jnp.int32), jnp.asarray([p[1] for p in pairs], jnp.int32)


def _sub_scores(qn_ref, qr_ref, k, r, sub, t, diagonal):
    q = jnp.concatenate([qn_ref[r:r + sub, :], qr_ref[r:r + sub, :]], axis=1)
    if not diagonal:
        return q, _dot(q, k, _NT)
    cols = r + sub
    keep = lax.broadcasted_iota(jnp.int32, (sub, cols), 1) <= r + lax.broadcasted_iota(jnp.int32, (sub, cols), 0)
    return q, jnp.where(keep, _dot(q, k[:cols], _NT), -jnp.inf)


def _attn_fwd(qn, qr, kn, kr, v, *, name):
    s, t = qn.shape[0], _attn_tile(qn.shape[0])
    sub = min(t, ATTN_SUB_ROWS)
    q_blk, k_blk = _causal_pairs(s // t, True)

    def body(qi_ref, kj_ref, qn_ref, qr_ref, kn_ref, kr_ref, v_ref, o_ref, lse_ref, m_sc, l_sc, acc_sc):
        p_id = pl.program_id(1)
        i, j = qi_ref[p_id], kj_ref[p_id]

        @pl.when(j == 0)
        def _():
            m_sc[...] = jnp.full_like(m_sc, -jnp.inf)
            l_sc[...] = jnp.zeros_like(l_sc)
            acc_sc[...] = jnp.zeros_like(acc_sc)

        def update(diagonal):
            k = jnp.concatenate([kn_ref[...], kr_ref[...]], axis=1)
            v = v_ref[...]
            starts = list(range(0, t, sub))
            scs = [_sub_scores(qn_ref, qr_ref, k, r, sub, t, diagonal)[1] for r in starts]
            ps, alphas = [], []
            for r, sc in zip(starts, scs):
                m_prev = m_sc[r:r + sub, :]
                m_new = jnp.maximum(m_prev, jnp.max(sc, axis=1, keepdims=True))
                alpha = jnp.exp2(m_prev - m_new)
                p = jnp.exp2(sc - m_new[:, :1])
                l_sc[r:r + sub, :] = alpha * l_sc[r:r + sub, :] + jnp.sum(p, axis=1, keepdims=True)
                m_sc[r:r + sub, :] = m_new
                ps.append(p)
                alphas.append(alpha)
            for r, p, alpha in zip(starts, ps, alphas):
                acc_sc[r:r + sub, :] = alpha * acc_sc[r:r + sub, :] + _dot(p, v[:p.shape[1]], _NN)

        @pl.when(j < i)
        def _():
            update(False)

        @pl.when(j == i)
        def _():
            update(True)
            o_ref[...] = (acc_sc[...] / l_sc[...]).astype(BF16)
            lse_ref[...] = m_sc[...] + jnp.log(l_sc[...]) * LOG2E

    q_spec = pl.BlockSpec((t, LANES), lambda h, p, qi, kj: (qi[p], h))
    k_spec = pl.BlockSpec((t, LANES), lambda h, p, qi, kj: (kj[p], h))
    kr_spec = pl.BlockSpec((t, LANES), lambda h, p, qi, kj: (kj[p], 0))
    stat = pltpu.VMEM((t, LANES), F32)
    return pl.pallas_call(
        body, name=name,
        grid_spec=pltpu.PrefetchScalarGridSpec(
            num_scalar_prefetch=2, grid=(MLA_HEADS, q_blk.shape[0]),
            in_specs=[q_spec, q_spec, k_spec, kr_spec, k_spec], out_specs=[q_spec, q_spec],
            scratch_shapes=[stat, stat, stat]),
        out_shape=[jax.ShapeDtypeStruct(qn.shape, BF16), jax.ShapeDtypeStruct(qn.shape, F32)],
        compiler_params=_params("parallel", "arbitrary"),
    )(q_blk, k_blk, qn, qr, kn, kr, v)


def _attn_bwd(qn, qr, kn, kr, v, do, lse, delta, *, name):
    s, t = qn.shape[0], _attn_tile(qn.shape[0])
    n, sub = s // t, min(t, ATTN_SUB_ROWS)
    q_blk, k_blk = _causal_pairs(n, False)

    def body(qi_ref, kj_ref, qn_ref, qr_ref, kn_ref, kr_ref, v_ref, do_ref, lse_ref, delta_ref,
             dqn_ref, dqr_ref, dkn_ref, dv_ref, dkr_ref, dk_sc, dv_sc):
        p_id = pl.program_id(1)
        i, j = qi_ref[p_id], kj_ref[p_id]

        @pl.when(p_id == 0)
        def _():
            dqn_ref[...] = jnp.zeros_like(dqn_ref)
            dqr_ref[...] = jnp.zeros_like(dqr_ref)

        @pl.when(i == j)
        def _():
            dk_sc[...] = jnp.zeros_like(dk_sc)
            dv_sc[...] = jnp.zeros_like(dv_sc)

        def accumulate(diagonal):
            k = jnp.concatenate([kn_ref[...], kr_ref[...]], axis=1)
            v = v_ref[...]
            starts = list(range(0, t, sub))
            qs, d_os, scs, dps = [], [], [], []
            for r in starts:
                q, sc = _sub_scores(qn_ref, qr_ref, k, r, sub, t, diagonal)
                d_o = do_ref[r:r + sub, :]
                qs.append(q)
                d_os.append(d_o)
                scs.append(sc)
                dps.append(_dot(d_o, v[:sc.shape[1]], _NT))
            ps, dss = [], []
            for r, sc, dp in zip(starts, scs, dps):
                p = jnp.exp2(sc - lse_ref[r:r + sub, :][:, :1])
                ps.append(p.astype(BF16))
                dss.append((p * (dp - delta_ref[r:r + sub, :][:, :1])).astype(BF16))
            for r, q, d_o, p, ds in zip(starts, qs, d_os, ps, dss):
                cols = p.shape[1]
                dv_sc[:cols, :] += _dot(p, d_o, _TN)
                dk_sc[:cols, :] += _dot(ds, q, _TN)
                dq = _dot(ds, k[:cols], _NN) * ATTN_SCALE
                rows = pl.ds(pl.multiple_of(i * t + r, sub), sub)
                dqn_ref[rows, :] += dq[:, :LANES]
                dqr_ref[rows, :] += dq[:, LANES:]

        @pl.when(j < i)
        def _():
            accumulate(False)

        @pl.when(j == i)
        def _():
            accumulate(True)

        @pl.when(i == n - 1)
        def _():
            dkn_ref[...] = (dk_sc[:, :LANES] * LN2).astype(BF16)
            dkr_ref[...] = dk_sc[:, LANES:] * LN2
            dv_ref[...] = dv_sc[...].astype(BF16)

    q_spec = pl.BlockSpec((t, LANES), lambda h, p, qi, kj: (qi[p], h))
    k_spec = pl.BlockSpec((t, LANES), lambda h, p, qi, kj: (kj[p], h))
    kr_spec = pl.BlockSpec((t, LANES), lambda h, p, qi, kj: (kj[p], 0))
    head_spec = pl.BlockSpec((s, LANES), lambda h, p, qi, kj: (0, h))
    f32_out, bf16_out = jax.ShapeDtypeStruct(qn.shape, F32), jax.ShapeDtypeStruct(qn.shape, BF16)
    return pl.pallas_call(
        body, name=name,
        grid_spec=pltpu.PrefetchScalarGridSpec(
            num_scalar_prefetch=2, grid=(MLA_HEADS, q_blk.shape[0]),
            in_specs=[q_spec, q_spec, k_spec, kr_spec, k_spec, q_spec, q_spec, q_spec],
            out_specs=[head_spec, head_spec, k_spec, k_spec, k_spec],
            scratch_shapes=[pltpu.VMEM((t, 2 * LANES), F32), pltpu.VMEM((t, LANES), F32)]),
        out_shape=[f32_out, f32_out, bf16_out, bf16_out, f32_out],
        compiler_params=_params("parallel", "arbitrary"),
    )(q_blk, k_blk, qn, qr, kn, kr, v, do, lse, delta)


def _exchange(arrs, *, scatter, name):
    n = len(arrs)
    out_shape = [jax.ShapeDtypeStruct(a.shape if scatter else (N_DEV, *a.shape), a.dtype) for a in arrs]

    def body(*refs):
        ins, outs = refs[:n], refs[n:2 * n]
        send_sems, recv_sems, local_sems = refs[2 * n:]
        x, y, c = lax.axis_index("x"), lax.axis_index("y"), lax.axis_index("c")
        me = 4 * x + 2 * y + c
        copies = []
        for k in range(n):
            local = pltpu.make_async_copy(ins[k].at[me] if scatter else ins[k], outs[k].at[me], local_sems.at[k])
            local.start()
            copies.append(local)
            for d in range(1, N_DEV):
                px, py, pc = (x + (d >> 2)) % 2, (y + ((d >> 1) & 1)) % 2, (c + (d & 1)) % 2
                peer = 4 * px + 2 * py + pc
                remote = pltpu.make_async_remote_copy(
                    src_ref=ins[k].at[peer] if scatter else ins[k], dst_ref=outs[k].at[me],
                    send_sem=send_sems.at[k, d - 1], recv_sem=recv_sems.at[k, d - 1],
                    device_id=(px, py, pc), device_id_type=pl.DeviceIdType.MESH)
                remote.start()
                copies.append(remote)
        for cp in copies:
            cp.wait()

    any_spec = pl.BlockSpec(memory_space=pl.ANY)
    return pl.pallas_call(
        body, name=name, in_specs=[any_spec] * n, out_specs=[any_spec] * n, out_shape=out_shape,
        scratch_shapes=[pltpu.SemaphoreType.DMA((n, N_DEV - 1)), pltpu.SemaphoreType.DMA((n, N_DEV - 1)),
                        pltpu.SemaphoreType.DMA((n,))],
    )(*arrs)


def _peers(x, y, c):
    out = []
    for d in range(1, N_DEV):
        px, py, pc = (x + (d >> 2)) % 2, (y + ((d >> 1) & 1)) % 2, (c + (d & 1)) % 2
        out.append(((px, py, pc), 4 * px + 2 * py + pc))
    return out


def _exchange_copies(ins, lands, send_sems, recv_sems, scatter):
    x, y, c = lax.axis_index("x"), lax.axis_index("y"), lax.axis_index("c")
    me = 4 * x + 2 * y + c
    local, remote = [], []
    for k in range(len(ins)):
        local.append(pltpu.make_async_copy(ins[k].at[me] if scatter else ins[k], lands[k].at[me],
                                           recv_sems.at[k * N_DEV + N_DEV - 1]))
        for d, (coords, peer) in enumerate(_peers(x, y, c)):
            remote.append(pltpu.make_async_remote_copy(
                src_ref=ins[k].at[peer] if scatter else ins[k], dst_ref=lands[k].at[me],
                send_sem=send_sems.at[k * N_DEV + d], recv_sem=recv_sems.at[k * N_DEV + d],
                device_id=coords, device_id_type=pl.DeviceIdType.MESH))
    return local, remote


def _exchange_start(arrs, *, scatter, name, after=None):
    n = len(arrs)
    hbm = pl.BlockSpec(memory_space=pltpu.HBM)
    sem = pl.BlockSpec(memory_space=pltpu.SEMAPHORE)
    lands = [lax.empty(a.shape if scatter else (N_DEV, *a.shape), a.dtype) for a in arrs]

    def body(*refs):
        ins, land_refs = refs[:n], refs[n:2 * n]
        first_out = 2 * n + (after is not None)
        send_sems, recv_sems, token = refs[first_out], refs[first_out + 1], refs[-1]
        local, remote = _exchange_copies(ins, land_refs, send_sems, recv_sems, scatter)
        for cp in local + remote:
            cp.start()
        token[...] = jnp.zeros_like(token)

    operands = [pltpu.with_memory_space_constraint(a, pltpu.HBM) for a in list(arrs) + lands]
    behind = [] if after is None else [after]
    res = pl.pallas_call(
        body, name=name,
        out_shape=(pltpu.SemaphoreType.DMA((n * N_DEV,)), pltpu.SemaphoreType.DMA((n * N_DEV,)),
                   *[pltpu.HBM(o.shape, o.dtype) for o in operands], jax.ShapeDtypeStruct((8, LANES), F32)),
        in_specs=[hbm] * (2 * n) + [pl.BlockSpec(memory_space=pl.ANY)] * len(behind),
        out_specs=(sem, sem, *[hbm] * (2 * n), pl.BlockSpec(memory_space=pltpu.VMEM)),
        input_output_aliases={i: 2 + i for i in range(2 * n)},
        compiler_params=pltpu.CompilerParams(has_side_effects=pltpu.SideEffectType.DATAFLOW_SIDE_EFFECTING),
    )(*operands, *behind)
    return (res[0], res[1], list(res[2:2 + n]), list(res[2 + n:2 + 2 * n]), scatter), res[-1]


def _exchange_wait(state, after, *, name):
    send_sems, recv_sems, ins, lands, scatter = state
    n = len(ins)
    hbm = pl.BlockSpec(memory_space=pltpu.HBM)
    sem = pl.BlockSpec(memory_space=pltpu.SEMAPHORE)

    def body(*refs):
        in_refs, land_refs = refs[:n], refs[n:2 * n]
        local, remote = _exchange_copies(in_refs, land_refs, refs[2 * n], refs[2 * n + 1], scatter)
        for cp in local:
            cp.wait()
        for cp in remote:
            cp.wait_send()
            cp.wait_recv()

    res = pl.pallas_call(
        body, name=name, out_shape=tuple(pltpu.HBM(o.shape, o.dtype) for o in ins + lands),
        in_specs=[hbm] * (2 * n) + [sem, sem, pl.BlockSpec(memory_space=pl.ANY)], out_specs=tuple([hbm] * (2 * n)),
        input_output_aliases={i: i for i in range(2 * n)},
        compiler_params=pltpu.CompilerParams(has_side_effects=pltpu.SideEffectType.DATAFLOW_SIDE_EFFECTING),
    )(*ins, *lands, send_sems, recv_sems, after)
    return list(res[n:])


def _adam(w, terms, m, v, *, name):
    r, c = w.shape
    n = terms.shape[0]
    tr = min(r, 128)
    assert r % tr == 0

    def body(w_ref, t_ref, m_ref, v_ref, g_out, d_out, m_out, v_out):
        g = t_ref[0].astype(F32)
        for s in range(1, n):
            g = g + t_ref[s].astype(F32)
        m1 = ADAM_B1 * m_ref[...] + (1.0 - ADAM_B1) * g
        v1 = ADAM_B2 * v_ref[...] + (1.0 - ADAM_B2) * jnp.square(g)
        m_hat = m1 / (1.0 - ADAM_B1 ** ADAM_STEP)
        v_hat = v1 / (1.0 - ADAM_B2 ** ADAM_STEP)
        g_out[...] = g
        d_out[...] = -ADAM_LR * (m_hat / (jnp.sqrt(v_hat) + ADAM_EPS) + ADAM_WD * w_ref[...])
        m_out[...] = m1
        v_out[...] = v1

    spec = pl.BlockSpec((tr, c), lambda i: (i, 0))
    out = jax.ShapeDtypeStruct((r, c), F32)
    return pl.pallas_call(
        body, name=name, grid=(r // tr,),
        in_specs=[spec, pl.BlockSpec((n, tr, c), lambda i: (0, i, 0)), spec, spec], out_specs=[spec] * 4,
        out_shape=[out] * 4, compiler_params=_params("parallel"),
    )(w, terms, m, v)


def _sum_terms(terms, *, name):
    n, _, p = terms.shape

    def body(t_ref, o_ref):
        acc = t_ref[0]
        for s in range(1, n):
            acc = acc + t_ref[s]
        o_ref[...] = acc

    return pl.pallas_call(body, name=name, out_shape=jax.ShapeDtypeStruct((1, p), F32))(terms)


def _lb_logits_grad(dlb, logits, *, name):
    def body(dlb_ref, l_ref, o_ref):
        lb = _lower_bound(l_ref[...])
        d0 = dlb_ref[...] * lb * (1.0 - lb)
        o_ref[...] = jnp.concatenate([d0, -d0], axis=0)

    return pl.pallas_call(body, name=name, out_shape=jax.ShapeDtypeStruct(logits.shape, F32))(dlb, logits)


def _silu_grad(z):
    sg = _sigmoid(z)
    return sg * (1.0 + z * (1.0 - sg))


def _head_norm_gate(o, zg, gn):
    outs = []
    for h in range(HGRN_HEADS):
        sl = slice(h * LANES, (h + 1) * LANES)
        zg_h = zg[:, sl]
        outs.append(_rms(o[:, sl], gn) * (zg_h * _sigmoid(zg_h)))
    return (jnp.concatenate(outs, axis=1),)


def _head_norm_gate_bwd(o, zg, dm, gn):
    do_parts, dzg_parts, dgn = [], [], jnp.zeros((1, LANES), F32)
    for h in range(HGRN_HEADS):
        sl = slice(h * LANES, (h + 1) * LANES)
        o_h, zg_h, dm_h = o[:, sl], zg[:, sl], dm[:, sl]
        gate = zg_h * _sigmoid(zg_h)
        do_h, dgn_h = _rms_bwd(o_h, gn, dm_h * gate)
        dgn = dgn + dgn_h
        do_parts.append(do_h)
        dzg_parts.append(dm_h * _rms(o_h, gn) * _silu_grad(zg_h))
    return jnp.concatenate(do_parts, axis=1), jnp.concatenate(dzg_parts, axis=1), dgn


def _rope_slabs(x, t_c, t_s1, t_s2, transpose):
    fn = _rope_t if transpose else _rope
    return jnp.concatenate(
        [fn(x[:, h * LANES:(h + 1) * LANES], t_c, t_s1, t_s2) for h in range(x.shape[1] // LANES)], axis=1)


def _loss_head(h, tgt, w):
    d = h.shape[1]
    r = lax.rsqrt(jnp.mean(h * h, axis=-1, keepdims=True) + EPS)
    xh = h * r
    err = xh * w - tgt
    loss = 0.5 * jnp.sum(jnp.mean(err * err, axis=-1, keepdims=True), axis=0, keepdims=True)
    dy = err / d
    dxh = dy * w
    dh = r * (dxh - xh * jnp.mean(dxh * xh, axis=-1, keepdims=True))
    return dh, jnp.sum(dy * xh, axis=0, keepdims=True), jnp.broadcast_to(loss, (1, LANES))


def _mlp_fwd(h, norm, w_up, w_down, tag):
    d = h.shape[1]
    xn = _rowcall(lambda x, w: (_rms(x, w),), [h], [norm], [(d, BF16)], [], name=f"{tag}_norm")[0]
    u, act = _mm(xn, w_up, mode="nn", epilogue="relu2", name=f"{tag}_up")
    return _mm(act, w_down, mode="nn", add=h, name=f"{tag}_down"), (h, xn, u, act)


def _mlp_bwd(dh_out, saved, norm, w_up, w_down, tag, after=None):
    h, xn, u, act = saved
    d = h.shape[1]
    du = _mm(dh_out, w_down, mode="nt", epilogue="relu2_bwd", aux=u, out_dtype=BF16, after=after,
             name=f"{tag}_bwd_du")
    dw_down = _mm(act, dh_out, mode="tn", name=f"{tag}_bwd_wdown")
    dxn = _mm(du, w_up, mode="nt", name=f"{tag}_bwd_dxn")
    dw_up = _mm(xn, du, mode="tn", name=f"{tag}_bwd_wup")

    def norm_bwd(x, dy, dres, w):
        dx, dw = _rms_bwd(x, w, dy)
        return dx + dres, dw

    dh, dnorm = _rowcall(norm_bwd, [h, dxn, dh_out], [norm], [(d, F32)], [d], name=f"{tag}_bwd_norm")
    return dh, dnorm, dw_up, dw_down


def _row_major(g):
    return g.reshape(g.shape[0] * g.shape[1], g.shape[2])


def _col_major(g):
    return jnp.transpose(g, (1, 0, 2)).reshape(g.shape[1], g.shape[0] * g.shape[2])


def _col_terms(dw):
    k, n = dw.shape
    return jnp.transpose(dw.reshape(k, N_DEV, n // N_DEV), (1, 0, 2))


def _row_terms(dw):
    return dw.reshape(N_DEV, dw.shape[0] // N_DEV, dw.shape[1])


def kernel(x, hgrn_norm, hgrn_w_q, hgrn_w_f, hgrn_w_i, hgrn_w_g, hgrn_g_norm, hgrn_w_o, hgrn_lb_logits, mla_norm, mla_w_dq, mla_q_norm, mla_w_uq, mla_w_o, kv_in_norm, kv_w_dkv, kv_norm, kv_w_uk, kv_w_uv, mlp_norm, mlp_w_up, mlp_w_down, final_norm, loss_target, m_hgrn_norm, m_hgrn_w_q, m_hgrn_w_f, m_hgrn_w_i, m_hgrn_w_g, m_hgrn_g_norm, m_hgrn_w_o, m_hgrn_lb_logits, m_mla_norm, m_mla_w_dq, m_mla_q_norm, m_mla_w_uq, m_mla_w_o, m_kv_in_norm, m_kv_w_dkv, m_kv_norm, m_kv_w_uk, m_kv_w_uv, m_mlp_norm, m_mlp_w_up, m_mlp_w_down, m_final_norm, v_hgrn_norm, v_hgrn_w_q, v_hgrn_w_f, v_hgrn_w_i, v_hgrn_w_g, v_hgrn_g_norm, v_hgrn_w_o, v_hgrn_lb_logits, v_mla_norm, v_mla_w_dq, v_mla_q_norm, v_mla_w_uq, v_mla_w_o, v_kv_in_norm, v_kv_w_dkv, v_kv_norm, v_kv_w_uk, v_kv_w_uv, v_mlp_norm, v_mlp_w_up, v_mlp_w_down, v_final_norm):
    given = dict(locals())
    weight_names = ["hgrn_norm", "hgrn_w_q", "hgrn_w_f", "hgrn_w_i", "hgrn_w_g", "hgrn_g_norm", "hgrn_w_o",
                    "hgrn_lb_logits", "mla_norm", "mla_w_dq", "mla_q_norm", "mla_w_uq", "mla_w_o", "kv_in_norm",
                    "kv_w_dkv", "kv_norm", "kv_w_uk", "kv_w_uv", "mlp_norm", "mlp_w_up", "mlp_w_down", "final_norm"]
    me = 4 * lax.axis_index("x") + 2 * lax.axis_index("y") + lax.axis_index("c")
    xs, tgt = x[0], loss_target[0]
    seq, d_model = xs.shape
    n_heads, hd = MLA_HEADS, LANES

    big_local = {
        "hgrn_w_q": hgrn_w_q[0], "hgrn_w_f": hgrn_w_f[0], "hgrn_w_i": hgrn_w_i[0], "hgrn_w_g": hgrn_w_g[0],
        "hgrn_w_o": hgrn_w_o[0], "mla_w_dq": mla_w_dq[0], "mla_w_uq": mla_w_uq[0], "mla_w_o": mla_w_o[0],
        "kv_w_dkv": kv_w_dkv, "kv_w_uk": kv_w_uk, "kv_w_uv": kv_w_uv,
        "mlp_w_up0": mlp_w_up[0], "mlp_w_up1": mlp_w_up[1], "mlp_w_down0": mlp_w_down[0], "mlp_w_down1": mlp_w_down[1],
    }
    big_names = list(big_local)
    col_sharded = {"mla_w_uq", "kv_w_uk", "kv_w_uv", "mlp_w_up0", "mlp_w_up1"}
    vec_local = jnp.concatenate([hgrn_norm, hgrn_lb_logits], axis=0)
    first_names = ["hgrn_w_q", "hgrn_w_f", "hgrn_w_i", "hgrn_w_g"]
    later_names = {"mlp0": ["hgrn_w_o", "mlp_w_up0", "mlp_w_down0"],
                   "mla": ["kv_w_dkv", "kv_w_uk", "kv_w_uv", "mla_w_dq", "mla_w_uq", "mla_w_o"],
                   "mlp1": ["mlp_w_up1", "mlp_w_down1"]}

    def unshard(names, arrays):
        return {k: (_col_major(a) if k in col_sharded else _row_major(a)) for k, a in zip(names, arrays)}

    first_state, token = _exchange_start([big_local[k].astype(BF16) for k in first_names] + [vec_local], scatter=False,
                                         name="gather_first_start")
    gather_state = {}
    for tag, names in later_names.items():
        gather_state[tag], token = _exchange_start([big_local[k].astype(BF16) for k in names], scatter=False,
                                                   after=token, name=f"gather_{tag}_start")

    def gather_wait(tag, after):
        w.update(unshard(later_names[tag], _exchange_wait(gather_state[tag], after, name=f"gather_{tag}_wait")))

    gathered = _exchange_wait(first_state, token, name="gather_first_wait")
    w = unshard(first_names, gathered[:-1])
    vec_full = jnp.transpose(gathered[-1], (1, 0, 2)).reshape(3, d_model)
    hgrn_norm_full, lb_logits_full = vec_full[0:1], vec_full[1:3]
    t_c, t_s1, t_s2 = _rope_tables(seq)
    kv_lora = kv_w_uk.shape[0]

    xn0 = _rowcall(lambda a, g: (_rms(a, g),), [xs], [hgrn_norm_full], [(d_model, BF16)], [], name="hgrn_norm")[0]
    zq = _mm(xn0, w["hgrn_w_q"], mode="nn", name="hgrn_zq")
    zf = _mm(xn0, w["hgrn_w_f"], mode="nn", name="hgrn_zf")
    zi = _mm(xn0, w["hgrn_w_i"], mode="nn", name="hgrn_zi")
    zg = _mm(xn0, w["hgrn_w_g"], mode="nn", name="hgrn_zg")
    o_rec, states = _hgrn_fwd(zq, zf, zi, lb_logits_full, name="hgrn_fwd")
    mixed = _rowcall(_head_norm_gate, [o_rec, zg], [hgrn_g_norm], [(d_model, BF16)], [], name="hgrn_gate")[0]
    gather_wait("mlp0", mixed)
    h1 = _mm(mixed, w["hgrn_w_o"], mode="nn", add=xs, name="hgrn_out")
    h2, mlp0_saved = _mlp_fwd(h1, mlp_norm[0:1], w["mlp_w_up0"], w["mlp_w_down0"], "mlp0")
    gather_wait("mla", h2)
    w_uq3 = w["mla_w_uq"].reshape(-1, n_heads, MLA_NOPE + MLA_ROPE)
    w_uq_nope = w_uq3[:, :, :MLA_NOPE].reshape(-1, n_heads * hd)
    w_uq_rope = jnp.pad(w_uq3[:, :, MLA_NOPE:], ((0, 0), (0, 0), (0, hd - MLA_ROPE))).reshape(-1, n_heads * hd)
    w_dkv_pad = jnp.pad(w["kv_w_dkv"], ((0, 0), (0, kv_lora + hd - w["kv_w_dkv"].shape[1])))

    hn, xn2 = _rowcall(lambda a, g1, g2: (_rms(a, g1), _rms(a, g2)), [h2], [kv_in_norm[None, :], mla_norm],
                       [(d_model, BF16), (d_model, BF16)], [], name="kv_mla_norm")
    ckr = _mm(hn, w_dkv_pad, mode="nn", name="kv_down")

    def kv_latent(c_all, tc, ts1, ts2, g):
        return _rms(c_all[:, :kv_lora], g), _rope(c_all[:, kv_lora:], tc, ts1, ts2)

    c_kv, kr = _rowcall(kv_latent, [ckr, t_c, t_s1, t_s2], [kv_norm[None, :]], [(kv_lora, BF16), (hd, BF16)], [],
                        name="kv_latent")
    kn = _mm(c_kv, w["kv_w_uk"], mode="nn", out_dtype=BF16, name="kv_up_k")
    vv = _mm(c_kv, w["kv_w_uv"], mode="nn", out_dtype=BF16, name="kv_up_v")
    cq_pre = _mm(xn2, w["mla_w_dq"], mode="nn", name="q_down")
    c_q = _rowcall(lambda a, g: (_rms(a, g),), [cq_pre], [mla_q_norm], [(cq_pre.shape[1], BF16)], [], name="q_norm")[0]
    qn = _mm(c_q, w_uq_nope, mode="nn", out_dtype=BF16, scale=Q_PRESCALE, name="q_up_nope")
    qr_pre = _mm(c_q, w_uq_rope, mode="nn", name="q_up_rope")
    qr = _rowcall(lambda a, tc, ts1, ts2: (_rope_slabs(a, tc, ts1, ts2, False) * Q_PRESCALE,), [qr_pre, t_c, t_s1, t_s2], [],
                  [(n_heads * hd, BF16)], [], name="q_rope")[0]
    o_att, lse = _attn_fwd(qn, qr, kn, kr, vv, name="attn_fwd")
    h3 = _mm(o_att, w["mla_w_o"], mode="nn", add=h2, name="attn_out")
    gather_wait("mlp1", h3)
    h4, mlp1_saved = _mlp_fwd(h3, mlp_norm[1:2], w["mlp_w_up1"], w["mlp_w_down1"], "mlp1")
    dh4, g_final_norm, loss_part = _rowcall(_loss_head, [h4, tgt], [final_norm[None, :]], [(d_model, F32)],
                                            [d_model, LANES], name="loss_head")

    g = {}
    groups = {"mlp1": ["mlp_w_up1", "mlp_w_down1"],
              "mla": ["mla_w_o", "mla_w_uq", "mla_w_dq", "kv_w_uk", "kv_w_uv", "kv_w_dkv"],
              "mlp0": ["mlp_w_up0", "mlp_w_down0"],
              "hgrn_out": ["hgrn_w_o", "hgrn_w_g"],
              "hgrn_in": ["hgrn_w_q", "hgrn_w_f", "hgrn_w_i"]}
    scatter_state = {}

    def scatter_start(tag, after=None):
        scatter_state[tag], tok = _exchange_start(
            [(_col_terms if k in col_sharded else _row_terms)(g[k]) for k in groups[tag]], scatter=True, after=after,
            name=f"scatter_{tag}_start")
        return tok

    dh3, g_mlp_norm1, g["mlp_w_up1"], g["mlp_w_down1"] = _mlp_bwd(
        dh4, mlp1_saved, mlp_norm[1:2], w["mlp_w_up1"], w["mlp_w_down1"], "mlp1")
    d_oatt = _mm(dh3, w["mla_w_o"], mode="nt", out_dtype=BF16, after=scatter_start("mlp1"), name="attn_out_bwd_x")
    g["mla_w_o"] = _mm(o_att, dh3, mode="tn", name="attn_out_bwd_w")

    def head_delta(do, o):
        prod = do.astype(F32) * o.astype(F32)
        return (jnp.concatenate([jnp.broadcast_to(jnp.sum(prod[:, h * hd:(h + 1) * hd], axis=1, keepdims=True),
                                                  (prod.shape[0], hd)) for h in range(n_heads)], axis=1),)

    delta = _rowcall(head_delta, [d_oatt, o_att], [], [(n_heads * hd, F32)], [], name="attn_delta")[0]
    dqn, dqr, dkn, dvv, dkr = _attn_bwd(qn, qr, kn, kr, vv, d_oatt, lse, delta, name="attn_bwd")
    dqr_pre = _rowcall(lambda a, tc, ts1, ts2: (_rope_slabs(a, tc, ts1, ts2, True),), [dqr, t_c, t_s1, t_s2], [],
                       [(n_heads * hd, BF16)], [], name="q_rope_bwd")[0]
    dcq = _mm(dqn, w_uq_nope, mode="nt", name="q_up_nope_bwd_x")
    dcq = _mm(dqr_pre, w_uq_rope, mode="nt", add=dcq, name="q_up_rope_bwd_x")
    g_uq_nope = _mm(c_q, dqn, mode="tn", name="q_up_nope_bwd_w")
    g_uq_rope = _mm(c_q, dqr_pre, mode="tn", name="q_up_rope_bwd_w")
    q_lora = c_q.shape[1]
    g["mla_w_uq"] = jnp.concatenate([g_uq_nope.reshape(q_lora, n_heads, hd),
                                     g_uq_rope.reshape(q_lora, n_heads, hd)[:, :, :MLA_ROPE]], axis=2).reshape(q_lora, -1)
    dcq_pre, g_q_norm = _rowcall(lambda a, dy, gq: _rms_bwd(a, gq, dy), [cq_pre, dcq], [mla_q_norm],
                                 [(q_lora, BF16)], [q_lora], name="q_norm_bwd")
    dxn2 = _mm(dcq_pre, w["mla_w_dq"], mode="nt", name="q_down_bwd_x")
    g["mla_w_dq"] = _mm(xn2, dcq_pre, mode="tn", name="q_down_bwd_w")

    dc_kv = _mm(dkn, w["kv_w_uk"], mode="nt", name="kv_up_k_bwd_x")
    dc_kv = _mm(dvv, w["kv_w_uv"], mode="nt", add=dc_kv, name="kv_up_v_bwd_x")
    g["kv_w_uk"] = _mm(c_kv, dkn, mode="tn", name="kv_up_k_bwd_w")
    g["kv_w_uv"] = _mm(c_kv, dvv, mode="tn", name="kv_up_v_bwd_w")

    def kv_latent_bwd(c_all, dc, dkr_heads, tc, ts1, ts2, gk):
        dlat, dgk = _rms_bwd(c_all[:, :kv_lora], gk, dc)
        dkr_slab = dkr_heads[:, :hd]
        for h in range(1, n_heads):
            dkr_slab = dkr_slab + dkr_heads[:, h * hd:(h + 1) * hd]
        return jnp.concatenate([dlat, _rope_t(dkr_slab, tc, ts1, ts2)], axis=1), dgk

    dckr, g_kv_norm = _rowcall(kv_latent_bwd, [ckr, dc_kv, dkr, t_c, t_s1, t_s2], [kv_norm[None, :]],
                               [(kv_lora + hd, BF16)], [kv_lora], name="kv_latent_bwd")
    dhn = _mm(dckr, w_dkv_pad, mode="nt", name="kv_down_bwd_x")
    g["kv_w_dkv"] = _mm(hn, dckr, mode="tn", name="kv_down_bwd_w")[:, :kv_w_dkv.shape[1]]

    def kv_mla_norm_bwd(a, d1, d2, dres, g1, g2):
        dx1, dw1 = _rms_bwd(a, g1, d1)
        dx2, dw2 = _rms_bwd(a, g2, d2)
        return dx1 + dx2 + dres, dw1, dw2

    dh2, g_kv_in_norm, g_mla_norm = _rowcall(kv_mla_norm_bwd, [h2, dhn, dxn2, dh3], [kv_in_norm[None, :], mla_norm],
                                             [(d_model, F32)], [d_model, d_model], name="kv_mla_norm_bwd")
    dh1, g_mlp_norm0, g["mlp_w_up0"], g["mlp_w_down0"] = _mlp_bwd(
        dh2, mlp0_saved, mlp_norm[0:1], w["mlp_w_up0"], w["mlp_w_down0"], "mlp0", after=scatter_start("mla"))

    dmixed = _mm(dh1, w["hgrn_w_o"], mode="nt", after=scatter_start("mlp0"), name="hgrn_out_bwd_x")
    g["hgrn_w_o"] = _mm(mixed, dh1, mode="tn", name="hgrn_out_bwd_w")
    do_rec, dzg, g_g_norm = _rowcall(_head_norm_gate_bwd, [o_rec, zg, dmixed], [hgrn_g_norm],
                                     [(d_model, F32), (d_model, BF16)], [hd], name="hgrn_gate_bwd")
    g["hgrn_w_g"] = _mm(xn0, dzg, mode="tn", name="hgrn_w_g_bwd_w")
    dzq, dzf, dzi, g_lb = _hgrn_bwd(zq, zf, zi, lb_logits_full, states, do_rec, scatter_start("hgrn_out"),
                                    name="hgrn_bwd")
    dxn0 = _mm(dzg, w["hgrn_w_g"], mode="nt", name="hgrn_w_g_bwd_x")
    for nm, dz in (("hgrn_w_q", dzq), ("hgrn_w_f", dzf), ("hgrn_w_i", dzi)):
        dxn0 = _mm(dz, w[nm], mode="nt", add=dxn0, name=f"{nm}_bwd_x")
        g[nm] = _mm(xn0, dz, mode="tn", name=f"{nm}_bwd_w")

    def in_norm_bwd(a, dy, dres, gw):
        dx, dw = _rms_bwd(a, gw, dy)
        return dx + dres, dw

    grad_x, g_hgrn_norm = _rowcall(in_norm_bwd, [xs, dxn0, dh1], [hgrn_norm_full], [(d_model, F32)], [d_model],
                                   name="hgrn_norm_bwd")

    small_parts = [g_hgrn_norm, g_lb, g_g_norm, g_mla_norm, g_q_norm, g_kv_in_norm, g_kv_norm, g_mlp_norm0,
                   g_mlp_norm1, g_final_norm, loss_part]
    small_sizes = [p.shape[1] for p in small_parts]
    small_terms = _exchange([jnp.concatenate(small_parts, axis=1)], scatter=False, name="gather_small")[0]
    small_sum = _sum_terms(small_terms, name="sum_small")
    last = scatter_start("hgrn_in", after=small_sum)
    offs = [0]
    for sz in small_sizes:
        offs.append(offs[-1] + sz)
    (s_hgrn_norm, s_lb, s_g_norm, s_mla_norm, s_q_norm, s_kv_in_norm, s_kv_norm, s_mlp_norm0, s_mlp_norm1, s_final_norm,
     s_loss) = [small_sum[:, a:b] for a, b in zip(offs[:-1], offs[1:])]
    shard = hgrn_norm.shape[1]
    g_lb_logits = _lb_logits_grad(lax.dynamic_slice_in_dim(s_lb, me * shard, shard, axis=1), hgrn_lb_logits,
                                  name="lb_logits_grad")
    loss = s_loss[0, 0]

    res = {}
    for tag, names in groups.items():
        for k, t in zip(names, _exchange_wait(scatter_state[tag], last, name=f"scatter_{tag}_wait")):
            if k.startswith("mlp_w_"):
                base, layer = k[:-1], int(k[-1])
                wk, mk, vk = given[base][layer], given["m_" + base][layer], given["v_" + base][layer]
            else:
                wk, mk, vk = given[k], given["m_" + k], given["v_" + k]
            shape = wk.shape
            wk, mk, vk = (a.reshape(shape[-2], shape[-1]) for a in (wk, mk, vk))
            upd = _adam(wk, t, mk, vk, name=f"adam_{k}")
            last = upd[0]
            res[k] = [o.reshape(shape) for o in upd]
    for base in ("mlp_w_up", "mlp_w_down"):
        res[base] = [jnp.stack([res[base + "0"][i], res[base + "1"][i]], axis=0) for i in range(4)]

    small_grads = {
        "hgrn_norm": lax.dynamic_slice_in_dim(s_hgrn_norm, me * shard, shard, axis=1),
        "hgrn_g_norm": s_g_norm, "hgrn_lb_logits": g_lb_logits, "mla_norm": s_mla_norm, "mla_q_norm": s_q_norm,
        "kv_in_norm": s_kv_in_norm, "kv_norm": s_kv_norm,
        "mlp_norm": jnp.concatenate([s_mlp_norm0, s_mlp_norm1], axis=0), "final_norm": s_final_norm,
    }
    small_names = list(small_grads)

    def flat(a):
        return a.reshape(1, -1)

    packed = [jnp.concatenate([flat(src[pre + k]) for k in small_names], axis=1)
              for src, pre in ((given, ""), (small_grads, ""), (given, "m_"), (given, "v_"))]
    small_out = _adam(packed[0], packed[1][None], packed[2], packed[3], name="adam_small")
    off = 0
    for k in small_names:
        size = given[k].size
        res[k] = [o[:, off:off + size].reshape(given[k].shape) for o in small_out]
        off += size

    outs = [loss, grad_x[None]]
    for i in range(4):
        outs += [res[k][i] for k in weight_names]
    return tuple(outs)
```

```python
import functools

import jax
import jax.numpy as jnp
from jax import lax
from jax.experimental import pallas as pl
from jax.experimental.pallas import tpu as pltpu

F32 = jnp.float32
BF16 = jnp.bfloat16

EPS = 1e-6
LANES = 128
N_DEV = 8
V7X_VMEM_LIMIT_BYTES = 56 << 20
MM_PIPELINE_BYTES = 30 << 20
MM_ROW_TILE = 512
GRAD_WIRE_DTYPE = BF16

HGRN_HEADS = 8
HGRN_CHUNK = 64
HGRN_SUB = 16
HGRN_HEADS_PER_STEP = 8
EXP_CLAMP = 80.0
MLA_HEADS = 16
MLA_NOPE = 128
MLA_ROPE = 64
ROPE_THETA = 10000.0
ATTN_SCALE = (MLA_NOPE + MLA_ROPE) ** -0.5

ADAM_LR = 0.001
ADAM_B1 = 0.9
ADAM_B2 = 0.999
ADAM_EPS = 1e-08
ADAM_WD = 0.01
ADAM_STEP = 10

_NN = ((1,), (0,))
_NT = ((1,), (1,))
_TN = ((0,), (0,))


def _params(*sem):
    return pltpu.CompilerParams(dimension_semantics=sem, vmem_limit_bytes=V7X_VMEM_LIMIT_BYTES)


def _dot(a, b, dims):
    return lax.dot_general(a.astype(BF16), b.astype(BF16), (dims, ((), ())), preferred_element_type=F32)


def _dot_f32(a, b, dims=_NN):
    return lax.dot_general(a, b, (dims, ((), ())), precision=lax.Precision.HIGHEST, preferred_element_type=F32)


def _sigmoid(x):
    return 1.0 / (1.0 + jnp.exp(-x))


def _rms(x, w):
    r = lax.rsqrt(jnp.mean(x * x, axis=-1, keepdims=True) + EPS)
    return x * r * w


def _rms_bwd(x, w, dy):
    r = lax.rsqrt(jnp.mean(x * x, axis=-1, keepdims=True) + EPS)
    xh = x * r
    dw = jnp.sum(dy * xh, axis=0, keepdims=True)
    dxh = dy * w
    dx = r * (dxh - xh * jnp.mean(dxh * xh, axis=-1, keepdims=True))
    return dx, dw


def _mm_tiles(m, n, k, a_bytes, b_bytes, out_tile_bytes):
    tm = min(m, MM_ROW_TILE)
    for tn in (n, 2048, 1024, 512, 256, LANES):
        if tn <= n and n % tn == 0:
            if 2 * (tm * k * a_bytes + k * tn * b_bytes + tm * tn * out_tile_bytes) <= MM_PIPELINE_BYTES:
                return tm, tn
    return tm, min(n, LANES)


def _mm(a, b, *, mode, name, out_dtype=None, add=None, epilogue=None, aux=None, after=None, scale=None):
    if mode == "nn":
        (m, k), (k2, n) = a.shape, b.shape
    elif mode == "nt":
        (m, k), (n, k2) = a.shape, b.shape
    else:
        (k, m), (k2, n) = a.shape, b.shape
    assert k == k2, (name, a.shape, b.shape)
    if out_dtype is None:
        out_dtype = GRAD_WIRE_DTYPE if mode == "tn" else F32
    tile_aux = () if epilogue == "rope" else (aux,)
    tile_bytes = sum(x.dtype.itemsize for x in (add, *tile_aux) if x is not None) + jnp.dtype(out_dtype).itemsize
    tm, tn = _mm_tiles(m, n, k, a.dtype.itemsize, b.dtype.itemsize, tile_bytes)
    assert m % tm == 0 and n % tn == 0, (name, m, n)
    dims = {"nn": _NN, "nt": _NT, "tn": _TN}[mode]
    a_spec = pl.BlockSpec((k, tm), lambda i, j: (0, i)) if mode == "tn" else pl.BlockSpec((tm, k), lambda i, j: (i, 0))
    b_spec = pl.BlockSpec((tn, k), lambda i, j: (j, 0)) if mode == "nt" else pl.BlockSpec((k, tn), lambda i, j: (0, j))
    o_spec = pl.BlockSpec((tm, tn), lambda i, j: (i, j))
    operands, in_specs = [a, b], [a_spec, b_spec]
    for extra in (add, *tile_aux):
        if extra is not None:
            assert extra.shape == (m, n), (name, extra.shape)
            operands.append(extra)
            in_specs.append(o_spec)
    if epilogue == "rope":
        operands += list(aux)
        in_specs += [pl.BlockSpec((tm, LANES), lambda i, j: (i, 0))] * 3
    n_in = len(operands)
    if after is not None:
        operands.append(after)
        in_specs.append(pl.BlockSpec(memory_space=pl.ANY))
    out_shape = jax.ShapeDtypeStruct((m, n), out_dtype)

    def body(*refs):
        acc = _dot(refs[0][...], refs[1][...], dims)
        extras, outs = refs[2:n_in], refs[len(operands):]
        if scale is not None:
            acc = acc * scale
        if add is not None:
            acc = acc + extras[0][...]
        if epilogue == "relu2":
            acc = jnp.square(jnp.maximum(acc, 0.0))
        elif epilogue == "relu2_bwd":
            acc = acc * (2.0 * jnp.sqrt(extras[-1][...].astype(F32)))
        elif epilogue == "rope":
            acc = _rope_slabs(acc, *[r[...] for r in extras[-3:]], False)
        outs[0][...] = acc.astype(out_dtype)

    return pl.pallas_call(
        body, name=name, grid=(m // tm, n // tn), in_specs=in_specs, out_specs=o_spec, out_shape=out_shape,
        compiler_params=_params("parallel", "parallel"),
    )(*operands)


def _rowcall(fn, rows, consts, outs, accs, *, name, tr=256):
    s = rows[0].shape[0]
    tr = min(tr, s)
    assert s % tr == 0
    n_out = len(outs)
    in_specs = [pl.BlockSpec((tr, r.shape[1]), lambda i: (i, 0)) for r in rows]
    in_specs += [pl.BlockSpec(c.shape, lambda i: (0, 0)) for c in consts]
    out_shape = [jax.ShapeDtypeStruct((s, w), dt) for w, dt in outs] + [jax.ShapeDtypeStruct((1, w), F32) for w in accs]
    out_specs = [pl.BlockSpec((tr, w), lambda i: (i, 0)) for w, _ in outs] + [pl.BlockSpec((1, w), lambda i: (0, 0)) for w in accs]
    n_in = len(rows) + len(consts)

    def body(*refs):
        res = fn(*[r[...] for r in refs[:n_in]])
        out_refs = refs[n_in:]
        for ref, val in zip(out_refs[:n_out], res[:n_out]):
            ref[...] = val.astype(ref.dtype)
        i = pl.program_id(0)
        for ref, val in zip(out_refs[n_out:], res[n_out:]):
            @pl.when(i == 0)
            def _(ref=ref, val=val):
                ref[...] = val

            @pl.when(i > 0)
            def _(ref=ref, val=val):
                ref[...] += val

    return pl.pallas_call(
        body, name=name, grid=(s // tr,), in_specs=in_specs, out_specs=out_specs, out_shape=out_shape,
        compiler_params=_params("arbitrary" if accs else "parallel"),
    )(*rows, *consts)


def _rope_tables(seq):
    half = MLA_ROPE // 2
    inv_freq = ROPE_THETA ** (-jnp.arange(half, dtype=F32) / half)
    ang = jnp.arange(seq, dtype=F32)[:, None] * inv_freq[None, :]
    cos, sin, zero = jnp.cos(ang), jnp.sin(ang), jnp.zeros((seq, half), F32)
    t_c = jnp.concatenate([cos, cos, zero, zero], axis=1)
    t_s1 = jnp.concatenate([-sin, zero, zero, zero], axis=1)
    t_s2 = jnp.concatenate([zero, sin, zero, zero], axis=1)
    return t_c, t_s1, t_s2


def _rope(slab, t_c, t_s1, t_s2):
    return slab * t_c + pltpu.roll(slab, 96, 1) * t_s1 + pltpu.roll(slab, 32, 1) * t_s2


def _rope_t(d, t_c, t_s1, t_s2):
    return d * t_c + pltpu.roll(d * t_s1, 32, 1) + pltpu.roll(d * t_s2, 96, 1)


def _lower_bound(logits):
    l0, l1 = logits[0:1, :], logits[1:2, :]
    mx = jnp.maximum(l0, l1)
    e0, e1 = jnp.exp(l0 - mx), jnp.exp(l1 - mx)
    return e0 / (e0 + e1)


def _tri(n, lower):
    row = lax.broadcasted_iota(jnp.int32, (n, n), 0)
    col = lax.broadcasted_iota(jnp.int32, (n, n), 1)
    return (row >= col) if lower else (row <= col)


def _hgrn_fwd(zq, zf, zi, lb_logits, *, name):
    s, d = zq.shape
    h_n, c, hp = d // LANES, HGRN_CHUNK, HGRN_HEADS_PER_STEP
    nc = s // c

    def body(zq_ref, zf_ref, zi_ref, lb_ref, o_ref, st_ref, state_sc, b_sc):
        @pl.when(pl.program_id(1) == 0)
        def _():
            state_sc[...] = jnp.zeros_like(state_sc)

        lower = _tri(c, True).astype(F32)
        hs = range(hp)
        sls = [slice(hh * LANES, (hh + 1) * LANES) for hh in hs]
        lb = [_lower_bound(lb_ref[:, sl]) for sl in sls]
        zq_v = [zq_ref[:, sl] for sl in sls]
        q = [z * _sigmoid(z) for z in zq_v]
        f = [lb[hh] + (1.0 - lb[hh]) * _sigmoid(zf_ref[:, sls[hh]]) for hh in hs]
        g = [jnp.log(x) for x in f]
        k = [1.0 - x for x in f]
        v = [zi_ref[:, sl] for sl in sls]
        b = [_dot_f32(lower, x) for x in g]
        s0t = [state_sc[hh] for hh in hs]
        for hh in hs:
            st_ref[hh] = s0t[hh]
            b_sc[hh] = b[hh]
        o_inter = [_dot(q[hh] * jnp.exp(b[hh]), s0t[hh], _NT) for hh in hs]
        scores = [[] for _ in hs]
        for i in range(c // HGRN_SUB):
            lo = i * HGRN_SUB
            for hh in hs:
                ref = b_sc[hh, lo - 1:lo, :] if i > 0 else jnp.zeros((1, LANES), F32)
                qt = q[hh][lo:lo + HGRN_SUB, :] * jnp.exp(b[hh][lo:lo + HGRN_SUB, :] - ref)
                dec = jnp.exp(jnp.minimum(ref - b[hh], EXP_CLAMP))
                scores[hh].append(_dot(qt, k[hh] * dec, _NT))
        a = [jnp.where(_tri(c, True), jnp.concatenate(sc, axis=0), 0.0) for sc in scores]
        for hh in hs:
            o_ref[:, sls[hh]] = o_inter[hh] + _dot(a[hh], v[hh], _NN)
        bl = [b_sc[hh, c - 1:c, :] for hh in hs]
        for hh in hs:
            state_sc[hh] = s0t[hh] * jnp.exp(bl[hh]) + _dot(v[hh], k[hh] * jnp.exp(bl[hh] - b[hh]), _TN)

    tile = pl.BlockSpec((c, hp * LANES), lambda h, i: (i, h))
    return pl.pallas_call(
        body, name=name, grid=(h_n // hp, nc),
        in_specs=[tile, tile, tile, pl.BlockSpec((2, hp * LANES), lambda h, i: (0, h))],
        out_specs=[tile, pl.BlockSpec((hp, None, LANES, LANES), lambda h, i: (h, i, 0, 0))],
        out_shape=[jax.ShapeDtypeStruct((s, d), F32), jax.ShapeDtypeStruct((h_n, nc, LANES, LANES), F32)],
        scratch_shapes=[pltpu.VMEM((hp, LANES, LANES), F32), pltpu.VMEM((hp, c, LANES), F32)],
        compiler_params=_params("parallel", "arbitrary"),
    )(zq, zf, zi, lb_logits)


def _hgrn_bwd(zq, zf, zi, lb_logits, states, do, after, *, name):
    s, d = zq.shape
    h_n, c, hp = d // LANES, HGRN_CHUNK, HGRN_HEADS_PER_STEP
    nc = s // c

    def body(zq_ref, zf_ref, zi_ref, lb_ref, st_ref, do_ref, _, dzq_ref, dzf_ref, dzi_ref, dlb_ref, dstate_sc, b_sc):
        @pl.when(pl.program_id(1) == 0)
        def _():
            dstate_sc[...] = jnp.zeros_like(dstate_sc)
            dlb_ref[...] = jnp.zeros_like(dlb_ref)

        lower, upper = _tri(c, True), _tri(c, False).astype(F32)
        lower_f = lower.astype(F32)
        last_row = lax.broadcasted_iota(jnp.int32, (c, LANES), 0) == c - 1
        hs = range(hp)
        sls = [slice(hh * LANES, (hh + 1) * LANES) for hh in hs]
        lb = [_lower_bound(lb_ref[:, sl]) for sl in sls]
        zq_v = [zq_ref[:, sl] for sl in sls]
        sq = [_sigmoid(z) for z in zq_v]
        q = [zq_v[hh] * sq[hh] for hh in hs]
        sf = [_sigmoid(zf_ref[:, sl]) for sl in sls]
        f = [lb[hh] + (1.0 - lb[hh]) * sf[hh] for hh in hs]
        g = [jnp.log(x) for x in f]
        k = [1.0 - x for x in f]
        v = [zi_ref[:, sl] for sl in sls]
        d_o = [do_ref[:, sl] for sl in sls]
        b = [_dot_f32(lower_f, x) for x in g]
        s0t = [st_ref[hh] for hh in hs]
        ds1t = [dstate_sc[hh] for hh in hs]
        for hh in hs:
            b_sc[hh] = b[hh]
        bl = [b_sc[hh, c - 1:c, :] for hh in hs]
        eb = [jnp.exp(x) for x in b]
        ebl = [jnp.exp(x) for x in bl]
        dec_end = [jnp.exp(bl[hh] - b[hh]) for hh in hs]
        da = [jnp.where(lower, _dot(d_o[hh], v[hh], _NT), 0.0) for hh in hs]
        dq_inter = [_dot(d_o[hh], s0t[hh], _NN) * eb[hh] for hh in hs]
        dk_state = [_dot(v[hh], ds1t[hh], _NN) * dec_end[hh] for hh in hs]
        dv_state = [_dot(k[hh] * dec_end[hh], ds1t[hh], _NT) for hh in hs]
        for hh in hs:
            dstate_sc[hh] = ds1t[hh] * ebl[hh] + _dot(d_o[hh], q[hh] * eb[hh], _TN)
        dk = list(dk_state)
        scores, dq_blocks = [[] for _ in hs], [[] for _ in hs]
        for i in range(c // HGRN_SUB):
            lo = i * HGRN_SUB
            for hh in hs:
                ref = b_sc[hh, lo - 1:lo, :] if i > 0 else jnp.zeros((1, LANES), F32)
                grow = jnp.exp(b[hh][lo:lo + HGRN_SUB, :] - ref)
                qt = q[hh][lo:lo + HGRN_SUB, :] * grow
                dec = jnp.exp(jnp.minimum(ref - b[hh], EXP_CLAMP))
                kd = k[hh] * dec
                scores[hh].append(_dot(qt, kd, _NT))
                da_i = da[hh][lo:lo + HGRN_SUB, :]
                dq_blocks[hh].append(_dot_f32(da_i, kd, _NN) * grow)
                dk[hh] = dk[hh] + _dot_f32(da_i, qt, _TN) * dec
        a = [jnp.where(lower, jnp.concatenate(sc, axis=0), 0.0) for sc in scores]
        dv = [_dot(a[hh], d_o[hh], _TN) + dv_state[hh] for hh in hs]
        dq = [dq_inter[hh] + jnp.concatenate(dq_blocks[hh], axis=0) for hh in hs]
        db_last = [jnp.sum(k[hh] * dk_state[hh], axis=0, keepdims=True)
                   + ebl[hh] * jnp.sum(s0t[hh] * ds1t[hh], axis=0, keepdims=True) for hh in hs]
        db = [q[hh] * dq[hh] - k[hh] * dk[hh] + jnp.where(last_row, db_last[hh], 0.0) for hh in hs]
        dg = [_dot_f32(upper, x) for x in db]
        df = [dg[hh] / f[hh] - dk[hh] for hh in hs]
        for hh in hs:
            sl = sls[hh]
            dzf_ref[:, sl] = (df[hh] * (1.0 - lb[hh]) * sf[hh] * (1.0 - sf[hh])).astype(BF16)
            dlb_ref[:, sl] += jnp.sum(df[hh] * (1.0 - sf[hh]), axis=0, keepdims=True)
            dzq_ref[:, sl] = (dq[hh] * sq[hh] * (1.0 + zq_v[hh] * (1.0 - sq[hh]))).astype(BF16)
            dzi_ref[:, sl] = dv[hh].astype(BF16)

    tile = pl.BlockSpec((c, hp * LANES), lambda h, i: (nc - 1 - i, h))
    out = jax.ShapeDtypeStruct((s, d), BF16)
    return pl.pallas_call(
        body, name=name, grid=(h_n // hp, nc),
        in_specs=[tile, tile, tile, pl.BlockSpec((2, hp * LANES), lambda h, i: (0, h)),
                  pl.BlockSpec((hp, None, LANES, LANES), lambda h, i: (h, nc - 1 - i, 0, 0)), tile,
                  pl.BlockSpec(memory_space=pl.ANY)],
        out_specs=[tile, tile, tile, pl.BlockSpec((1, hp * LANES), lambda h, i: (0, h))],
        out_shape=[out, out, out, jax.ShapeDtypeStruct((1, d), F32)],
        scratch_shapes=[pltpu.VMEM((hp, LANES, LANES), F32), pltpu.VMEM((hp, c, LANES), F32)],
        compiler_params=_params("parallel", "arbitrary"),
    )(zq, zf, zi, lb_logits, states, do, after)


ATTN_SUB_ROWS = 256
LOG2E = 1.4426950408889634
LN2 = 0.6931471805599453
Q_PRESCALE = ATTN_SCALE * LOG2E


def _attn_tile(s):
    return min(1024, max(128, s // 2))


def _causal_pairs(n, q_major):
    pairs = [(i, j) for i in range(n) for j in range(i + 1)] if q_major else [(i, j) for j in range(n) for i in range(j, n)]
    return jnp.asarray([p[0] for p in pairs], jnp.int32), jnp.asarray([p[1] for p in pairs], jnp.int32)


def _sub_scores(qn_ref, qr_ref, k, r, sub, t, diagonal):
    q = jnp.concatenate([qn_ref[r:r + sub, :], qr_ref[r:r + sub, :]], axis=1)
    if not diagonal:
        return q, _dot(q, k, _NT)
    cols = r + sub
    keep = lax.broadcasted_iota(jnp.int32, (sub, cols), 1) <= r + lax.broadcasted_iota(jnp.int32, (sub, cols), 0)
    return q, jnp.where(keep, _dot(q, k[:cols], _NT), -jnp.inf)


def _attn_fwd(qn, qr, kn, kr, v, *, name):
    s, t = qn.shape[0], _attn_tile(qn.shape[0])
    sub = min(t, ATTN_SUB_ROWS)
    q_blk, k_blk = _causal_pairs(s // t, True)

    def body(qi_ref, kj_ref, qn_ref, qr_ref, kn_ref, kr_ref, v_ref, o_ref, lse_ref, m_sc, l_sc, acc_sc):
        p_id = pl.program_id(1)
        i, j = qi_ref[p_id], kj_ref[p_id]

        @pl.when(j == 0)
        def _():
            m_sc[...] = jnp.full_like(m_sc, -jnp.inf)
            l_sc[...] = jnp.zeros_like(l_sc)
            acc_sc[...] = jnp.zeros_like(acc_sc)

        def update(diagonal):
            k = jnp.concatenate([kn_ref[...], kr_ref[...]], axis=1)
            v = v_ref[...]
            starts = list(range(0, t, sub))
            scs = [_sub_scores(qn_ref, qr_ref, k, r, sub, t, diagonal)[1] for r in starts]
            ps, alphas = [], []
            for r, sc in zip(starts, scs):
                m_prev = m_sc[r:r + sub, :]
                m_new = jnp.maximum(m_prev, jnp.max(sc, axis=1, keepdims=True))
                alpha = jnp.exp2(m_prev - m_new)
                p = jnp.exp2(sc - m_new[:, :1])
                l_sc[r:r + sub, :] = alpha * l_sc[r:r + sub, :] + jnp.sum(p, axis=1, keepdims=True)
                m_sc[r:r + sub, :] = m_new
                ps.append(p)
                alphas.append(alpha)
            for r, p, alpha in zip(starts, ps, alphas):
                acc_sc[r:r + sub, :] = alpha * acc_sc[r:r + sub, :] + _dot(p, v[:p.shape[1]], _NN)

        @pl.when(j < i)
        def _():
            update(False)

        @pl.when(j == i)
        def _():
            update(True)
            o_ref[...] = (acc_sc[...] / l_sc[...]).astype(BF16)
            lse_ref[...] = m_sc[...] + jnp.log(l_sc[...]) * LOG2E

    q_spec = pl.BlockSpec((t, LANES), lambda h, p, qi, kj: (qi[p], h))
    k_spec = pl.BlockSpec((t, LANES), lambda h, p, qi, kj: (kj[p], h))
    kr_spec = pl.BlockSpec((t, LANES), lambda h, p, qi, kj: (kj[p], 0))
    stat = pltpu.VMEM((t, LANES), F32)
    return pl.pallas_call(
        body, name=name,
        grid_spec=pltpu.PrefetchScalarGridSpec(
            num_scalar_prefetch=2, grid=(MLA_HEADS, q_blk.shape[0]),
            in_specs=[q_spec, q_spec, k_spec, kr_spec, k_spec], out_specs=[q_spec, q_spec],
            scratch_shapes=[stat, stat, stat]),
        out_shape=[jax.ShapeDtypeStruct(qn.shape, BF16), jax.ShapeDtypeStruct(qn.shape, F32)],
        compiler_params=_params("parallel", "arbitrary"),
    )(q_blk, k_blk, qn, qr, kn, kr, v)


def _attn_bwd(qn, qr, kn, kr, v, do, lse, delta, *, name):
    s, t = qn.shape[0], _attn_tile(qn.shape[0])
    n, sub = s // t, min(t, ATTN_SUB_ROWS)
    q_blk, k_blk = _causal_pairs(n, False)

    def body(qi_ref, kj_ref, qn_ref, qr_ref, kn_ref, kr_ref, v_ref, do_ref, lse_ref, delta_ref,
             dqn_ref, dqr_ref, dkn_ref, dv_ref, dkr_ref, dk_sc, dv_sc):
        p_id = pl.program_id(1)
        i, j = qi_ref[p_id], kj_ref[p_id]

        @pl.when(p_id == 0)
        def _():
            dqn_ref[...] = jnp.zeros_like(dqn_ref)
            dqr_ref[...] = jnp.zeros_like(dqr_ref)

        @pl.when(i == j)
        def _():
            dk_sc[...] = jnp.zeros_like(dk_sc)
            dv_sc[...] = jnp.zeros_like(dv_sc)

        def accumulate(diagonal):
            k = jnp.concatenate([kn_ref[...], kr_ref[...]], axis=1)
            v = v_ref[...]
            starts = list(range(0, t, sub))
            qs, d_os, scs, dps = [], [], [], []
            for r in starts:
                q, sc = _sub_scores(qn_ref, qr_ref, k, r, sub, t, diagonal)
                d_o = do_ref[r:r + sub, :]
                qs.append(q)
                d_os.append(d_o)
                scs.append(sc)
                dps.append(_dot(d_o, v[:sc.shape[1]], _NT))
            ps, dss = [], []
            for r, sc, dp in zip(starts, scs, dps):
                p = jnp.exp2(sc - lse_ref[r:r + sub, :][:, :1])
                ps.append(p.astype(BF16))
                dss.append((p * (dp - delta_ref[r:r + sub, :][:, :1])).astype(BF16))
            for r, q, d_o, p, ds in zip(starts, qs, d_os, ps, dss):
                cols = p.shape[1]
                dv_sc[:cols, :] += _dot(p, d_o, _TN)
                dk_sc[:cols, :] += _dot(ds, q, _TN)
                dq = _dot(ds, k[:cols], _NN) * ATTN_SCALE
                rows = pl.ds(pl.multiple_of(i * t + r, sub), sub)
                dqn_ref[rows, :] += dq[:, :LANES]
                dqr_ref[rows, :] += dq[:, LANES:]

        @pl.when(j < i)
        def _():
            accumulate(False)

        @pl.when(j == i)
        def _():
            accumulate(True)

        @pl.when(i == n - 1)
        def _():
            dkn_ref[...] = (dk_sc[:, :LANES] * LN2).astype(BF16)
            dkr_ref[...] = dk_sc[:, LANES:] * LN2
            dv_ref[...] = dv_sc[...].astype(BF16)

    q_spec = pl.BlockSpec((t, LANES), lambda h, p, qi, kj: (qi[p], h))
    k_spec = pl.BlockSpec((t, LANES), lambda h, p, qi, kj: (kj[p], h))
    kr_spec = pl.BlockSpec((t, LANES), lambda h, p, qi, kj: (kj[p], 0))
    head_spec = pl.BlockSpec((s, LANES), lambda h, p, qi, kj: (0, h))
    f32_out, bf16_out = jax.ShapeDtypeStruct(qn.shape, F32), jax.ShapeDtypeStruct(qn.shape, BF16)
    return pl.pallas_call(
        body, name=name,
        grid_spec=pltpu.PrefetchScalarGridSpec(
            num_scalar_prefetch=2, grid=(MLA_HEADS, q_blk.shape[0]),
            in_specs=[q_spec, q_spec, k_spec, kr_spec, k_spec, q_spec, q_spec, q_spec],
            out_specs=[head_spec, head_spec, k_spec, k_spec, k_spec],
            scratch_shapes=[pltpu.VMEM((t, 2 * LANES), F32), pltpu.VMEM((t, LANES), F32)]),
        out_shape=[f32_out, f32_out, bf16_out, bf16_out, f32_out],
        compiler_params=_params("parallel", "arbitrary"),
    )(q_blk, k_blk, qn, qr, kn, kr, v, do, lse, delta)


def _exchange(arrs, *, scatter, name):
    n = len(arrs)
    out_shape = [jax.ShapeDtypeStruct(a.shape if scatter else (N_DEV, *a.shape), a.dtype) for a in arrs]

    def body(*refs):
        ins, outs = refs[:n], refs[n:2 * n]
        send_sems, recv_sems, local_sems = refs[2 * n:]
        x, y, c = lax.axis_index("x"), lax.axis_index("y"), lax.axis_index("c")
        me = 4 * x + 2 * y + c
        copies = []
        for k in range(n):
            local = pltpu.make_async_copy(ins[k].at[me] if scatter else ins[k], outs[k].at[me], local_sems.at[k])
            local.start()
            copies.append(local)
            for d in range(1, N_DEV):
                px, py, pc = (x + (d >> 2)) % 2, (y + ((d >> 1) & 1)) % 2, (c + (d & 1)) % 2
                peer = 4 * px + 2 * py + pc
                remote = pltpu.make_async_remote_copy(
                    src_ref=ins[k].at[peer] if scatter else ins[k], dst_ref=outs[k].at[me],
                    send_sem=send_sems.at[k, d - 1], recv_sem=recv_sems.at[k, d - 1],
                    device_id=(px, py, pc), device_id_type=pl.DeviceIdType.MESH)
                remote.start()
                copies.append(remote)
        for cp in copies:
            cp.wait()

    any_spec = pl.BlockSpec(memory_space=pl.ANY)
    return pl.pallas_call(
        body, name=name, in_specs=[any_spec] * n, out_specs=[any_spec] * n, out_shape=out_shape,
        scratch_shapes=[pltpu.SemaphoreType.DMA((n, N_DEV - 1)), pltpu.SemaphoreType.DMA((n, N_DEV - 1)),
                        pltpu.SemaphoreType.DMA((n,))],
    )(*arrs)


def _peers(x, y, c):
    out = []
    for d in range(1, N_DEV):
        px, py, pc = (x + (d >> 2)) % 2, (y + ((d >> 1) & 1)) % 2, (c + (d & 1)) % 2
        out.append(((px, py, pc), 4 * px + 2 * py + pc))
    return out


def _exchange_copies(ins, lands, send_sems, recv_sems, scatter):
    x, y, c = lax.axis_index("x"), lax.axis_index("y"), lax.axis_index("c")
    me = 4 * x + 2 * y + c
    local, remote = [], []
    for k in range(len(ins)):
        local.append(pltpu.make_async_copy(ins[k].at[me] if scatter else ins[k], lands[k].at[me],
                                           recv_sems.at[k * N_DEV + N_DEV - 1]))
        for d, (coords, peer) in enumerate(_peers(x, y, c)):
            remote.append(pltpu.make_async_remote_copy(
                src_ref=ins[k].at[peer] if scatter else ins[k], dst_ref=lands[k].at[me],
                send_sem=send_sems.at[k * N_DEV + d], recv_sem=recv_sems.at[k * N_DEV + d],
                device_id=coords, device_id_type=pl.DeviceIdType.MESH))
    return local, remote


def _exchange_start(arrs, *, scatter, name, after=None):
    n = len(arrs)
    hbm = pl.BlockSpec(memory_space=pltpu.HBM)
    sem = pl.BlockSpec(memory_space=pltpu.SEMAPHORE)
    lands = [lax.empty(a.shape if scatter else (N_DEV, *a.shape), a.dtype) for a in arrs]

    def body(*refs):
        ins, land_refs = refs[:n], refs[n:2 * n]
        first_out = 2 * n + (after is not None)
        send_sems, recv_sems, token = refs[first_out], refs[first_out + 1], refs[-1]
        local, remote = _exchange_copies(ins, land_refs, send_sems, recv_sems, scatter)
        for cp in local + remote:
            cp.start()
        token[...] = jnp.zeros_like(token)

    operands = [pltpu.with_memory_space_constraint(a, pltpu.HBM) for a in list(arrs) + lands]
    behind = [] if after is None else [after]
    res = pl.pallas_call(
        body, name=name,
        out_shape=(pltpu.SemaphoreType.DMA((n * N_DEV,)), pltpu.SemaphoreType.DMA((n * N_DEV,)),
                   *[pltpu.HBM(o.shape, o.dtype) for o in operands], jax.ShapeDtypeStruct((8, LANES), F32)),
        in_specs=[hbm] * (2 * n) + [pl.BlockSpec(memory_space=pl.ANY)] * len(behind),
        out_specs=(sem, sem, *[hbm] * (2 * n), pl.BlockSpec(memory_space=pltpu.VMEM)),
        input_output_aliases={i: 2 + i for i in range(2 * n)},
        compiler_params=pltpu.CompilerParams(has_side_effects=pltpu.SideEffectType.DATAFLOW_SIDE_EFFECTING),
    )(*operands, *behind)
    return (res[0], res[1], list(res[2:2 + n]), list(res[2 + n:2 + 2 * n]), scatter), res[-1]


def _exchange_wait(state, after, *, name):
    send_sems, recv_sems, ins, lands, scatter = state
    n = len(ins)
    hbm = pl.BlockSpec(memory_space=pltpu.HBM)
    sem = pl.BlockSpec(memory_space=pltpu.SEMAPHORE)

    def body(*refs):
        in_refs, land_refs = refs[:n], refs[n:2 * n]
        local, remote = _exchange_copies(in_refs, land_refs, refs[2 * n], refs[2 * n + 1], scatter)
        for cp in local:
            cp.wait()
        for cp in remote:
            cp.wait_send()
            cp.wait_recv()

    res = pl.pallas_call(
        body, name=name, out_shape=tuple(pltpu.HBM(o.shape, o.dtype) for o in ins + lands),
        in_specs=[hbm] * (2 * n) + [sem, sem, pl.BlockSpec(memory_space=pl.ANY)], out_specs=tuple([hbm] * (2 * n)),
        input_output_aliases={i: i for i in range(2 * n)},
        compiler_params=pltpu.CompilerParams(has_side_effects=pltpu.SideEffectType.DATAFLOW_SIDE_EFFECTING),
    )(*ins, *lands, send_sems, recv_sems, after)
    return list(res[n:])


def _adam(w, terms, m, v, *, name):
    r, c = w.shape
    n = terms.shape[0]
    tr = min(r, 128)
    assert r % tr == 0

    def body(w_ref, t_ref, m_ref, v_ref, g_out, d_out, m_out, v_out):
        g = t_ref[0].astype(F32)
        for s in range(1, n):
            g = g + t_ref[s].astype(F32)
        m1 = ADAM_B1 * m_ref[...] + (1.0 - ADAM_B1) * g
        v1 = ADAM_B2 * v_ref[...] + (1.0 - ADAM_B2) * jnp.square(g)
        m_hat = m1 / (1.0 - ADAM_B1 ** ADAM_STEP)
        v_hat = v1 / (1.0 - ADAM_B2 ** ADAM_STEP)
        g_out[...] = g
        d_out[...] = -ADAM_LR * (m_hat / (jnp.sqrt(v_hat) + ADAM_EPS) + ADAM_WD * w_ref[...])
        m_out[...] = m1
        v_out[...] = v1

    spec = pl.BlockSpec((tr, c), lambda i: (i, 0))
    out = jax.ShapeDtypeStruct((r, c), F32)
    return pl.pallas_call(
        body, name=name, grid=(r // tr,),
        in_specs=[spec, pl.BlockSpec((n, tr, c), lambda i: (0, i, 0)), spec, spec], out_specs=[spec] * 4,
        out_shape=[out] * 4, compiler_params=_params("parallel"),
    )(w, terms, m, v)


def _sum_terms(terms, *, name):
    n, _, p = terms.shape

    def body(t_ref, o_ref):
        acc = t_ref[0]
        for s in range(1, n):
            acc = acc + t_ref[s]
        o_ref[...] = acc

    return pl.pallas_call(body, name=name, out_shape=jax.ShapeDtypeStruct((1, p), F32))(terms)


def _lb_logits_grad(dlb, logits, *, name):
    def body(dlb_ref, l_ref, o_ref):
        lb = _lower_bound(l_ref[...])
        d0 = dlb_ref[...] * lb * (1.0 - lb)
        o_ref[...] = jnp.concatenate([d0, -d0], axis=0)

    return pl.pallas_call(body, name=name, out_shape=jax.ShapeDtypeStruct(logits.shape, F32))(dlb, logits)


def _silu_grad(z):
    sg = _sigmoid(z)
    return sg * (1.0 + z * (1.0 - sg))


def _head_norm_gate(o, zg, gn):
    outs = []
    for h in range(HGRN_HEADS):
        sl = slice(h * LANES, (h + 1) * LANES)
        zg_h = zg[:, sl]
        outs.append(_rms(o[:, sl], gn) * (zg_h * _sigmoid(zg_h)))
    return (jnp.concatenate(outs, axis=1),)


def _head_norm_gate_bwd(o, zg, dm, gn):
    do_parts, dzg_parts, dgn = [], [], jnp.zeros((1, LANES), F32)
    for h in range(HGRN_HEADS):
        sl = slice(h * LANES, (h + 1) * LANES)
        o_h, zg_h, dm_h = o[:, sl], zg[:, sl], dm[:, sl]
        gate = zg_h * _sigmoid(zg_h)
        do_h, dgn_h = _rms_bwd(o_h, gn, dm_h * gate)
        dgn = dgn + dgn_h
        do_parts.append(do_h)
        dzg_parts.append(dm_h * _rms(o_h, gn) * _silu_grad(zg_h))
    return jnp.concatenate(do_parts, axis=1), jnp.concatenate(dzg_parts, axis=1), dgn


def _rope_slabs(x, t_c, t_s1, t_s2, transpose):
    fn = _rope_t if transpose else _rope
    return jnp.concatenate(
        [fn(x[:, h * LANES:(h + 1) * LANES], t_c, t_s1, t_s2) for h in range(x.shape[1] // LANES)], axis=1)


def _loss_head(h, tgt, w):
    d = h.shape[1]
    r = lax.rsqrt(jnp.mean(h * h, axis=-1, keepdims=True) + EPS)
    xh = h * r
    err = xh * w - tgt
    loss = 0.5 * jnp.sum(jnp.mean(err * err, axis=-1, keepdims=True), axis=0, keepdims=True)
    dy = err / d
    dxh = dy * w
    dh = r * (dxh - xh * jnp.mean(dxh * xh, axis=-1, keepdims=True))
    return dh, dh, jnp.sum(dy * xh, axis=0, keepdims=True), jnp.broadcast_to(loss, (1, LANES))


def _mlp_fwd(h, norm, w_up, w_down, tag):
    d = h.shape[1]
    xn = _rowcall(lambda x, w: (_rms(x, w),), [h], [norm], [(d, BF16)], [], name=f"{tag}_norm")[0]
    act = _mm(xn, w_up, mode="nn", epilogue="relu2", out_dtype=BF16, name=f"{tag}_up")
    return _mm(act, w_down, mode="nn", add=h, name=f"{tag}_down"), (h, xn, act)


def _mlp_bwd(dh_out, dh_out_bf, saved, norm, w_up, w_down, tag, after=None):
    h, xn, act = saved
    d = h.shape[1]
    du = _mm(dh_out_bf, w_down, mode="nt", epilogue="relu2_bwd", aux=act, out_dtype=BF16, after=after,
             name=f"{tag}_bwd_du")
    dw_down = _mm(act, dh_out_bf, mode="tn", name=f"{tag}_bwd_wdown")
    dxn = _mm(du, w_up, mode="nt", name=f"{tag}_bwd_dxn")
    dw_up = _mm(xn, du, mode="tn", name=f"{tag}_bwd_wup")

    def norm_bwd(x, dy, dres, w):
        dx, dw = _rms_bwd(x, w, dy)
        return dx + dres, dx + dres, dw

    dh, dh_bf, dnorm = _rowcall(norm_bwd, [h, dxn, dh_out], [norm], [(d, F32), (d, BF16)], [d], name=f"{tag}_bwd_norm")
    return dh, dh_bf, dnorm, dw_up, dw_down


def _row_major(g):
    return g.reshape(g.shape[0] * g.shape[1], g.shape[2])


def _col_major(g):
    return jnp.transpose(g, (1, 0, 2)).reshape(g.shape[1], g.shape[0] * g.shape[2])


def _col_terms(dw):
    k, n = dw.shape
    return jnp.transpose(dw.reshape(k, N_DEV, n // N_DEV), (1, 0, 2))


def _row_terms(dw):
    return dw.reshape(N_DEV, dw.shape[0] // N_DEV, dw.shape[1])


def kernel(x, hgrn_norm, hgrn_w_q, hgrn_w_f, hgrn_w_i, hgrn_w_g, hgrn_g_norm, hgrn_w_o, hgrn_lb_logits, mla_norm, mla_w_dq, mla_q_norm, mla_w_uq, mla_w_o, kv_in_norm, kv_w_dkv, kv_norm, kv_w_uk, kv_w_uv, mlp_norm, mlp_w_up, mlp_w_down, final_norm, loss_target, m_hgrn_norm, m_hgrn_w_q, m_hgrn_w_f, m_hgrn_w_i, m_hgrn_w_g, m_hgrn_g_norm, m_hgrn_w_o, m_hgrn_lb_logits, m_mla_norm, m_mla_w_dq, m_mla_q_norm, m_mla_w_uq, m_mla_w_o, m_kv_in_norm, m_kv_w_dkv, m_kv_norm, m_kv_w_uk, m_kv_w_uv, m_mlp_norm, m_mlp_w_up, m_mlp_w_down, m_final_norm, v_hgrn_norm, v_hgrn_w_q, v_hgrn_w_f, v_hgrn_w_i, v_hgrn_w_g, v_hgrn_g_norm, v_hgrn_w_o, v_hgrn_lb_logits, v_mla_norm, v_mla_w_dq, v_mla_q_norm, v_mla_w_uq, v_mla_w_o, v_kv_in_norm, v_kv_w_dkv, v_kv_norm, v_kv_w_uk, v_kv_w_uv, v_mlp_norm, v_mlp_w_up, v_mlp_w_down, v_final_norm):
    given = dict(locals())
    weight_names = ["hgrn_norm", "hgrn_w_q", "hgrn_w_f", "hgrn_w_i", "hgrn_w_g", "hgrn_g_norm", "hgrn_w_o",
                    "hgrn_lb_logits", "mla_norm", "mla_w_dq", "mla_q_norm", "mla_w_uq", "mla_w_o", "kv_in_norm",
                    "kv_w_dkv", "kv_norm", "kv_w_uk", "kv_w_uv", "mlp_norm", "mlp_w_up", "mlp_w_down", "final_norm"]
    me = 4 * lax.axis_index("x") + 2 * lax.axis_index("y") + lax.axis_index("c")
    xs, tgt = x[0], loss_target[0]
    seq, d_model = xs.shape
    n_heads, hd = MLA_HEADS, LANES

    big_local = {
        "hgrn_w_q": hgrn_w_q[0], "hgrn_w_f": hgrn_w_f[0], "hgrn_w_i": hgrn_w_i[0], "hgrn_w_g": hgrn_w_g[0],
        "hgrn_w_o": hgrn_w_o[0], "mla_w_dq": mla_w_dq[0], "mla_w_uq": mla_w_uq[0], "mla_w_o": mla_w_o[0],
        "kv_w_dkv": kv_w_dkv, "kv_w_uk": kv_w_uk, "kv_w_uv": kv_w_uv,
        "mlp_w_up0": mlp_w_up[0], "mlp_w_up1": mlp_w_up[1], "mlp_w_down0": mlp_w_down[0], "mlp_w_down1": mlp_w_down[1],
    }
    big_names = list(big_local)
    col_sharded = {"mla_w_uq", "kv_w_uk", "kv_w_uv", "mlp_w_up0", "mlp_w_up1"}
    vec_local = jnp.concatenate([hgrn_norm, hgrn_lb_logits], axis=0)
    first_names = ["hgrn_w_q", "hgrn_w_f", "hgrn_w_i", "hgrn_w_g"]
    later_names = {"mlp0": ["hgrn_w_o", "mlp_w_up0", "mlp_w_down0"],
                   "mla": ["kv_w_dkv", "kv_w_uk", "kv_w_uv", "mla_w_dq", "mla_w_uq", "mla_w_o"],
                   "mlp1": ["mlp_w_up1", "mlp_w_down1"]}

    def unshard(names, arrays):
        return {k: (_col_major(a) if k in col_sharded else _row_major(a)) for k, a in zip(names, arrays)}

    first_state, token = _exchange_start([big_local[k].astype(BF16) for k in first_names] + [vec_local], scatter=False,
                                         name="gather_first_start")
    gather_state = {}
    for tag, names in later_names.items():
        gather_state[tag], token = _exchange_start([big_local[k].astype(BF16) for k in names], scatter=False,
                                                   after=token, name=f"gather_{tag}_start")

    def gather_wait(tag, after):
        w.update(unshard(later_names[tag], _exchange_wait(gather_state[tag], after, name=f"gather_{tag}_wait")))

    gathered = _exchange_wait(first_state, token, name="gather_first_wait")
    w = unshard(first_names, gathered[:-1])
    vec_full = jnp.transpose(gathered[-1], (1, 0, 2)).reshape(3, d_model)
    hgrn_norm_full, lb_logits_full = vec_full[0:1], vec_full[1:3]
    t_c, t_s1, t_s2 = _rope_tables(seq)
    kv_lora = kv_w_uk.shape[0]

    xn0 = _rowcall(lambda a, g: (_rms(a, g),), [xs], [hgrn_norm_full], [(d_model, BF16)], [], name="hgrn_norm")[0]
    zq = _mm(xn0, w["hgrn_w_q"], mode="nn", name="hgrn_zq")
    zf = _mm(xn0, w["hgrn_w_f"], mode="nn", name="hgrn_zf")
    zi = _mm(xn0, w["hgrn_w_i"], mode="nn", name="hgrn_zi")
    zg = _mm(xn0, w["hgrn_w_g"], mode="nn", name="hgrn_zg")
    o_rec, states = _hgrn_fwd(zq, zf, zi, lb_logits_full, name="hgrn_fwd")
    mixed = _rowcall(_head_norm_gate, [o_rec, zg], [hgrn_g_norm], [(d_model, BF16)], [], name="hgrn_gate")[0]
    gather_wait("mlp0", mixed)
    h1 = _mm(mixed, w["hgrn_w_o"], mode="nn", add=xs, name="hgrn_out")
    h2, mlp0_saved = _mlp_fwd(h1, mlp_norm[0:1], w["mlp_w_up0"], w["mlp_w_down0"], "mlp0")
    gather_wait("mla", h2)
    w_uq3 = w["mla_w_uq"].reshape(-1, n_heads, MLA_NOPE + MLA_ROPE)
    w_uq_nope = w_uq3[:, :, :MLA_NOPE].reshape(-1, n_heads * hd)
    w_uq_rope = jnp.pad(w_uq3[:, :, MLA_NOPE:], ((0, 0), (0, 0), (0, hd - MLA_ROPE))).reshape(-1, n_heads * hd)
    w_dkv_pad = jnp.pad(w["kv_w_dkv"], ((0, 0), (0, kv_lora + hd - w["kv_w_dkv"].shape[1])))

    hn, xn2 = _rowcall(lambda a, g1, g2: (_rms(a, g1), _rms(a, g2)), [h2], [kv_in_norm[None, :], mla_norm],
                       [(d_model, BF16), (d_model, BF16)], [], name="kv_mla_norm")
    ckr = _mm(hn, w_dkv_pad, mode="nn", name="kv_down")

    def kv_latent(c_all, tc, ts1, ts2, g):
        return _rms(c_all[:, :kv_lora], g), _rope(c_all[:, kv_lora:], tc, ts1, ts2)

    c_kv, kr = _rowcall(kv_latent, [ckr, t_c, t_s1, t_s2], [kv_norm[None, :]], [(kv_lora, BF16), (hd, BF16)], [],
                        name="kv_latent")
    kn = _mm(c_kv, w["kv_w_uk"], mode="nn", out_dtype=BF16, name="kv_up_k")
    vv = _mm(c_kv, w["kv_w_uv"], mode="nn", out_dtype=BF16, name="kv_up_v")
    cq_pre = _mm(xn2, w["mla_w_dq"], mode="nn", name="q_down")
    c_q = _rowcall(lambda a, g: (_rms(a, g),), [cq_pre], [mla_q_norm], [(cq_pre.shape[1], BF16)], [], name="q_norm")[0]
    qn = _mm(c_q, w_uq_nope, mode="nn", out_dtype=BF16, scale=Q_PRESCALE, name="q_up_nope")
    qr = _mm(c_q, w_uq_rope, mode="nn", out_dtype=BF16, scale=Q_PRESCALE, epilogue="rope", aux=(t_c, t_s1, t_s2),
             name="q_up_rope")
    o_att, lse = _attn_fwd(qn, qr, kn, kr, vv, name="attn_fwd")
    h3 = _mm(o_att, w["mla_w_o"], mode="nn", add=h2, name="attn_out")
    gather_wait("mlp1", h3)
    h4, mlp1_saved = _mlp_fwd(h3, mlp_norm[1:2], w["mlp_w_up1"], w["mlp_w_down1"], "mlp1")
    dh4, dh4_bf, g_final_norm, loss_part = _rowcall(_loss_head, [h4, tgt], [final_norm[None, :]],
                                                    [(d_model, F32), (d_model, BF16)], [d_model, LANES], name="loss_head")

    g = {}
    groups = {"mlp1": ["mlp_w_up1", "mlp_w_down1"],
              "mla": ["mla_w_o", "mla_w_uq", "mla_w_dq", "kv_w_uk", "kv_w_uv", "kv_w_dkv"],
              "mlp0": ["mlp_w_up0", "mlp_w_down0"],
              "hgrn_out": ["hgrn_w_o", "hgrn_w_g"],
              "hgrn_in": ["hgrn_w_q", "hgrn_w_f", "hgrn_w_i"]}
    scatter_state = {}

    def scatter_start(tag, after=None):
        scatter_state[tag], tok = _exchange_start(
            [(_col_terms if k in col_sharded else _row_terms)(g[k]) for k in groups[tag]], scatter=True, after=after,
            name=f"scatter_{tag}_start")
        return tok

    dh3, dh3_bf, g_mlp_norm1, g["mlp_w_up1"], g["mlp_w_down1"] = _mlp_bwd(
        dh4, dh4_bf, mlp1_saved, mlp_norm[1:2], w["mlp_w_up1"], w["mlp_w_down1"], "mlp1")
    d_oatt = _mm(dh3_bf, w["mla_w_o"], mode="nt", out_dtype=BF16, after=scatter_start("mlp1"), name="attn_out_bwd_x")
    g["mla_w_o"] = _mm(o_att, dh3_bf, mode="tn", name="attn_out_bwd_w")

    def head_delta(do, o):
        prod = do.astype(F32) * o.astype(F32)
        return (jnp.concatenate([jnp.broadcast_to(jnp.sum(prod[:, h * hd:(h + 1) * hd], axis=1, keepdims=True),
                                                  (prod.shape[0], hd)) for h in range(n_heads)], axis=1),)

    delta = _rowcall(head_delta, [d_oatt, o_att], [], [(n_heads * hd, F32)], [], name="attn_delta")[0]
    dqn, dqr, dkn, dvv, dkr = _attn_bwd(qn, qr, kn, kr, vv, d_oatt, lse, delta, name="attn_bwd")
    dqr_pre = _rowcall(lambda a, tc, ts1, ts2: (_rope_slabs(a, tc, ts1, ts2, True),), [dqr, t_c, t_s1, t_s2], [],
                       [(n_heads * hd, BF16)], [], name="q_rope_bwd")[0]
    dcq = _mm(dqn, w_uq_nope, mode="nt", name="q_up_nope_bwd_x")
    dcq = _mm(dqr_pre, w_uq_rope, mode="nt", add=dcq, name="q_up_rope_bwd_x")
    g_uq_nope = _mm(c_q, dqn, mode="tn", name="q_up_nope_bwd_w")
    g_uq_rope = _mm(c_q, dqr_pre, mode="tn", name="q_up_rope_bwd_w")
    q_lora = c_q.shape[1]
    g["mla_w_uq"] = jnp.concatenate([g_uq_nope.reshape(q_lora, n_heads, hd),
                                     g_uq_rope.reshape(q_lora, n_heads, hd)[:, :, :MLA_ROPE]], axis=2).reshape(q_lora, -1)
    dcq_pre, g_q_norm = _rowcall(lambda a, dy, gq: _rms_bwd(a, gq, dy), [cq_pre, dcq], [mla_q_norm],
                                 [(q_lora, BF16)], [q_lora], name="q_norm_bwd")
    dxn2 = _mm(dcq_pre, w["mla_w_dq"], mode="nt", name="q_down_bwd_x")
    g["mla_w_dq"] = _mm(xn2, dcq_pre, mode="tn", name="q_down_bwd_w")

    dc_kv = _mm(dkn, w["kv_w_uk"], mode="nt", name="kv_up_k_bwd_x")
    dc_kv = _mm(dvv, w["kv_w_uv"], mode="nt", add=dc_kv, name="kv_up_v_bwd_x")
    g["kv_w_uk"] = _mm(c_kv, dkn, mode="tn", name="kv_up_k_bwd_w")
    g["kv_w_uv"] = _mm(c_kv, dvv, mode="tn", name="kv_up_v_bwd_w")

    def kv_latent_bwd(c_all, dc, dkr_heads, tc, ts1, ts2, gk):
        dlat, dgk = _rms_bwd(c_all[:, :kv_lora], gk, dc)
        dkr_slab = dkr_heads[:, :hd]
        for h in range(1, n_heads):
            dkr_slab = dkr_slab + dkr_heads[:, h * hd:(h + 1) * hd]
        return jnp.concatenate([dlat, _rope_t(dkr_slab, tc, ts1, ts2)], axis=1), dgk

    dckr, g_kv_norm = _rowcall(kv_latent_bwd, [ckr, dc_kv, dkr, t_c, t_s1, t_s2], [kv_norm[None, :]],
                               [(kv_lora + hd, BF16)], [kv_lora], name="kv_latent_bwd")
    dhn = _mm(dckr, w_dkv_pad, mode="nt", name="kv_down_bwd_x")
    g["kv_w_dkv"] = _mm(hn, dckr, mode="tn", name="kv_down_bwd_w")[:, :kv_w_dkv.shape[1]]

    def kv_mla_norm_bwd(a, d1, d2, dres, g1, g2):
        dx1, dw1 = _rms_bwd(a, g1, d1)
        dx2, dw2 = _rms_bwd(a, g2, d2)
        return dx1 + dx2 + dres, dx1 + dx2 + dres, dw1, dw2

    dh2, dh2_bf, g_kv_in_norm, g_mla_norm = _rowcall(
        kv_mla_norm_bwd, [h2, dhn, dxn2, dh3], [kv_in_norm[None, :], mla_norm], [(d_model, F32), (d_model, BF16)],
        [d_model, d_model], name="kv_mla_norm_bwd")
    dh1, dh1_bf, g_mlp_norm0, g["mlp_w_up0"], g["mlp_w_down0"] = _mlp_bwd(
        dh2, dh2_bf, mlp0_saved, mlp_norm[0:1], w["mlp_w_up0"], w["mlp_w_down0"], "mlp0", after=scatter_start("mla"))

    dmixed = _mm(dh1_bf, w["hgrn_w_o"], mode="nt", after=scatter_start("mlp0"), name="hgrn_out_bwd_x")
    g["hgrn_w_o"] = _mm(mixed, dh1_bf, mode="tn", name="hgrn_out_bwd_w")
    do_rec, dzg, g_g_norm = _rowcall(_head_norm_gate_bwd, [o_rec, zg, dmixed], [hgrn_g_norm],
                                     [(d_model, F32), (d_model, BF16)], [hd], name="hgrn_gate_bwd")
    g["hgrn_w_g"] = _mm(xn0, dzg, mode="tn", name="hgrn_w_g_bwd_w")
    dzq, dzf, dzi, g_lb = _hgrn_bwd(zq, zf, zi, lb_logits_full, states, do_rec, scatter_start("hgrn_out"),
                                    name="hgrn_bwd")
    dxn0 = _mm(dzg, w["hgrn_w_g"], mode="nt", name="hgrn_w_g_bwd_x")
    for nm, dz in (("hgrn_w_q", dzq), ("hgrn_w_f", dzf), ("hgrn_w_i", dzi)):
        dxn0 = _mm(dz, w[nm], mode="nt", add=dxn0, name=f"{nm}_bwd_x")
        g[nm] = _mm(xn0, dz, mode="tn", name=f"{nm}_bwd_w")

    def in_norm_bwd(a, dy, dres, gw):
        dx, dw = _rms_bwd(a, gw, dy)
        return dx + dres, dw

    grad_x, g_hgrn_norm = _rowcall(in_norm_bwd, [xs, dxn0, dh1], [hgrn_norm_full], [(d_model, F32)], [d_model],
                                   name="hgrn_norm_bwd")

    small_parts = [g_hgrn_norm, g_lb, g_g_norm, g_mla_norm, g_q_norm, g_kv_in_norm, g_kv_norm, g_mlp_norm0,
                   g_mlp_norm1, g_final_norm, loss_part]
    small_sizes = [p.shape[1] for p in small_parts]
    small_terms = _exchange([jnp.concatenate(small_parts, axis=1)], scatter=False, name="gather_small")[0]
    small_sum = _sum_terms(small_terms, name="sum_small")
    last = scatter_start("hgrn_in", after=small_sum)
    offs = [0]
    for sz in small_sizes:
        offs.append(offs[-1] + sz)
    (s_hgrn_norm, s_lb, s_g_norm, s_mla_norm, s_q_norm, s_kv_in_norm, s_kv_norm, s_mlp_norm0, s_mlp_norm1, s_final_norm,
     s_loss) = [small_sum[:, a:b] for a, b in zip(offs[:-1], offs[1:])]
    shard = hgrn_norm.shape[1]
    g_lb_logits = _lb_logits_grad(lax.dynamic_slice_in_dim(s_lb, me * shard, shard, axis=1), hgrn_lb_logits,
                                  name="lb_logits_grad")
    loss = s_loss[0, 0]

    res = {}
    for tag, names in groups.items():
        for k, t in zip(names, _exchange_wait(scatter_state[tag], last, name=f"scatter_{tag}_wait")):
            if k.startswith("mlp_w_"):
                base, layer = k[:-1], int(k[-1])
                wk, mk, vk = given[base][layer], given["m_" + base][layer], given["v_" + base][layer]
            else:
                wk, mk, vk = given[k], given["m_" + k], given["v_" + k]
            shape = wk.shape
            wk, mk, vk = (a.reshape(shape[-2], shape[-1]) for a in (wk, mk, vk))
            upd = _adam(wk, t, mk, vk, name=f"adam_{k}")
            last = upd[0]
            res[k] = [o.reshape(shape) for o in upd]
    for base in ("mlp_w_up", "mlp_w_down"):
        res[base] = [jnp.stack([res[base + "0"][i], res[base + "1"][i]], axis=0) for i in range(4)]

    small_grads = {
        "hgrn_norm": lax.dynamic_slice_in_dim(s_hgrn_norm, me * shard, shard, axis=1),
        "hgrn_g_norm": s_g_norm, "hgrn_lb_logits": g_lb_logits, "mla_norm": s_mla_norm, "mla_q_norm": s_q_norm,
        "kv_in_norm": s_kv_in_norm, "kv_norm": s_kv_norm,
        "mlp_norm": jnp.concatenate([s_mlp_norm0, s_mlp_norm1], axis=0), "final_norm": s_final_norm,
    }
    small_names = list(small_grads)

    def flat(a):
        return a.reshape(1, -1)

    packed = [jnp.concatenate([flat(src[pre + k]) for k in small_names], axis=1)
              for src, pre in ((given, ""), (small_grads, ""), (given, "m_"), (given, "v_"))]
    small_out = _adam(packed[0], packed[1][None], packed[2], packed[3], name="adam_small")
    off = 0
    for k in small_names:
        size = given[k].size
        res[k] = [o[:, off:off + size].reshape(given[k].shape) for o in small_out]
        off += size

    outs = [loss, grad_x[None]]
    for i in range(4):
        outs += [res[k][i] for k in weight_names]
    return tuple(outs)
```

```python
import functools

import jax
import jax.numpy as jnp
from jax import lax
from jax.experimental import pallas as pl
from jax.experimental.pallas import tpu as pltpu

F32 = jnp.float32
BF16 = jnp.bfloat16

EPS = 1e-6
LANES = 128
N_DEV = 8
V7X_VMEM_LIMIT_BYTES = 56 << 20
MM_PIPELINE_BYTES = 30 << 20
MM_ROW_TILE = 512
GRAD_WIRE_DTYPE = BF16

HGRN_HEADS = 8
HGRN_CHUNK = 64
HGRN_SUB = 16
HGRN_HEADS_PER_STEP = 8
EXP_CLAMP = 80.0
MLA_HEADS = 16
MLA_NOPE = 128
MLA_ROPE = 64
ROPE_THETA = 10000.0
ATTN_SCALE = (MLA_NOPE + MLA_ROPE) ** -0.5

ADAM_LR = 0.001
ADAM_B1 = 0.9
ADAM_B2 = 0.999
ADAM_EPS = 1e-08
ADAM_WD = 0.01
ADAM_STEP = 10

_NN = ((1,), (0,))
_NT = ((1,), (1,))
_TN = ((0,), (0,))


def _params(*sem):
    return pltpu.CompilerParams(dimension_semantics=sem, vmem_limit_bytes=V7X_VMEM_LIMIT_BYTES)


def _dot(a, b, dims):
    return lax.dot_general(a.astype(BF16), b.astype(BF16), (dims, ((), ())), preferred_element_type=F32)


def _dot_f32(a, b, dims=_NN):
    return lax.dot_general(a, b, (dims, ((), ())), precision=lax.Precision.HIGHEST, preferred_element_type=F32)


def _sigmoid(x):
    return 1.0 / (1.0 + jnp.exp(-x))


def _rms(x, w):
    r = lax.rsqrt(jnp.mean(x * x, axis=-1, keepdims=True) + EPS)
    return x * r * w


def _rms_bwd(x, w, dy):
    r = lax.rsqrt(jnp.mean(x * x, axis=-1, keepdims=True) + EPS)
    xh = x * r
    dw = jnp.sum(dy * xh, axis=0, keepdims=True)
    dxh = dy * w
    dx = r * (dxh - xh * jnp.mean(dxh * xh, axis=-1, keepdims=True))
    return dx, dw


def _mm_tiles(m, n, k, a_bytes, b_bytes, out_tile_bytes):
    tm = min(m, MM_ROW_TILE)
    for tn in (n, 2048, 1024, 512, 256, LANES):
        if tn <= n and n % tn == 0:
            if 2 * (tm * k * a_bytes + k * tn * b_bytes + tm * tn * out_tile_bytes) <= MM_PIPELINE_BYTES:
                return tm, tn
    return tm, min(n, LANES)


def _mm(a, b, *, mode, name, out_dtype=None, add=None, relu2_of=None, after=None):
    if mode == "nn":
        (m, k), (k2, n) = a.shape, b.shape
    elif mode == "nt":
        (m, k), (n, k2) = a.shape, b.shape
    else:
        (k, m), (k2, n) = a.shape, b.shape
    assert k == k2, (name, a.shape, b.shape)
    if out_dtype is None:
        out_dtype = GRAD_WIRE_DTYPE if mode == "tn" else F32
    tile_bytes = sum(x.dtype.itemsize for x in (add, relu2_of) if x is not None) + jnp.dtype(out_dtype).itemsize
    tm, tn = _mm_tiles(m, n, k, a.dtype.itemsize, b.dtype.itemsize, tile_bytes)
    assert m % tm == 0 and n % tn == 0, (name, m, n)
    dims = {"nn": _NN, "nt": _NT, "tn": _TN}[mode]
    a_spec = pl.BlockSpec((k, tm), lambda i, j: (0, i)) if mode == "tn" else pl.BlockSpec((tm, k), lambda i, j: (i, 0))
    b_spec = pl.BlockSpec((tn, k), lambda i, j: (j, 0)) if mode == "nt" else pl.BlockSpec((k, tn), lambda i, j: (0, j))
    o_spec = pl.BlockSpec((tm, tn), lambda i, j: (i, j))
    operands, in_specs = [a, b], [a_spec, b_spec]
    for extra in (add, relu2_of):
        if extra is not None:
            assert extra.shape == (m, n), (name, extra.shape)
            operands.append(extra)
            in_specs.append(o_spec)
    n_in = len(operands)
    if after is not None:
        operands.append(after)
        in_specs.append(pl.BlockSpec(memory_space=pl.ANY))
    out_shape = jax.ShapeDtypeStruct((m, n), out_dtype)

    def body(*refs):
        acc = _dot(refs[0][...], refs[1][...], dims)
        extras, outs = refs[2:n_in], refs[len(operands):]
        if add is not None:
            acc = acc + extras[0][...]
        if relu2_of is not None:
            acc = acc * (2.0 * jnp.sqrt(extras[-1][...].astype(F32)))
        outs[0][...] = acc.astype(out_dtype)

    return pl.pallas_call(
        body, name=name, grid=(m // tm, n // tn), in_specs=in_specs, out_specs=o_spec, out_shape=out_shape,
        compiler_params=_params("parallel", "parallel"),
    )(*operands)


def _rowcall(fn, rows, consts, outs, accs, *, name, tr=256):
    s = rows[0].shape[0]
    tr = min(tr, s)
    assert s % tr == 0
    n_out = len(outs)
    accs = [(1, a) if isinstance(a, int) else a for a in accs]
    in_specs = [pl.BlockSpec((tr, r.shape[1]), lambda i: (i, 0)) for r in rows]
    in_specs += [pl.BlockSpec(c.shape, lambda i: (0, 0)) for c in consts]
    out_shape = [jax.ShapeDtypeStruct((s, w), dt) for w, dt in outs] + [jax.ShapeDtypeStruct(a, F32) for a in accs]
    out_specs = [pl.BlockSpec((tr, w), lambda i: (i, 0)) for w, _ in outs] + [pl.BlockSpec(a, lambda i: (0, 0)) for a in accs]
    n_in = len(rows) + len(consts)

    def body(*refs):
        res = fn(*[r[...] for r in refs[:n_in]])
        out_refs = refs[n_in:]
        for ref, val in zip(out_refs[:n_out], res[:n_out]):
            ref[...] = val.astype(ref.dtype)
        i = pl.program_id(0)
        for ref, val in zip(out_refs[n_out:], res[n_out:]):
            @pl.when(i == 0)
            def _(ref=ref, val=val):
                ref[...] = val

            @pl.when(i > 0)
            def _(ref=ref, val=val):
                ref[...] += val

    return pl.pallas_call(
        body, name=name, grid=(s // tr,), in_specs=in_specs, out_specs=out_specs, out_shape=out_shape,
        compiler_params=_params("arbitrary" if accs else "parallel"),
    )(*rows, *consts)


def _rope_tables(seq):
    half = MLA_ROPE // 2
    inv_freq = ROPE_THETA ** (-jnp.arange(half, dtype=F32) / half)
    ang = jnp.arange(seq, dtype=F32)[:, None] * inv_freq[None, :]
    cos, sin, zero = jnp.cos(ang), jnp.sin(ang), jnp.zeros((seq, half), F32)
    t_c = jnp.concatenate([cos, cos, zero, zero], axis=1)
    t_s1 = jnp.concatenate([-sin, zero, zero, zero], axis=1)
    t_s2 = jnp.concatenate([zero, sin, zero, zero], axis=1)
    return t_c, t_s1, t_s2


def _rope(slab, t_c, t_s1, t_s2):
    return slab * t_c + pltpu.roll(slab, 96, 1) * t_s1 + pltpu.roll(slab, 32, 1) * t_s2


def _rope_t(d, t_c, t_s1, t_s2):
    return d * t_c + pltpu.roll(d * t_s1, 32, 1) + pltpu.roll(d * t_s2, 96, 1)


def _lower_bound(logits):
    l0, l1 = logits[0:1, :], logits[1:2, :]
    mx = jnp.maximum(l0, l1)
    e0, e1 = jnp.exp(l0 - mx), jnp.exp(l1 - mx)
    return e0 / (e0 + e1)


def _tri(n, lower):
    row = lax.broadcasted_iota(jnp.int32, (n, n), 0)
    col = lax.broadcasted_iota(jnp.int32, (n, n), 1)
    return (row >= col) if lower else (row <= col)


def _hgrn_fwd(zq, zf, zi, lb_logits, *, name):
    s, d = zq.shape
    h_n, c, hp = d // LANES, HGRN_CHUNK, HGRN_HEADS_PER_STEP
    nc = s // c

    def body(zq_ref, zf_ref, zi_ref, lb_ref, o_ref, st_ref, state_sc, b_sc):
        @pl.when(pl.program_id(1) == 0)
        def _():
            state_sc[...] = jnp.zeros_like(state_sc)

        lower = _tri(c, True).astype(F32)
        hs = range(hp)
        sls = [slice(hh * LANES, (hh + 1) * LANES) for hh in hs]
        lb = [_lower_bound(lb_ref[:, sl]) for sl in sls]
        zq_v = [zq_ref[:, sl] for sl in sls]
        q = [z * _sigmoid(z) for z in zq_v]
        f = [lb[hh] + (1.0 - lb[hh]) * _sigmoid(zf_ref[:, sls[hh]]) for hh in hs]
        g = [jnp.log(x) for x in f]
        k = [1.0 - x for x in f]
        v = [zi_ref[:, sl] for sl in sls]
        b = [_dot_f32(lower, x) for x in g]
        s0t = [state_sc[hh] for hh in hs]
        for hh in hs:
            st_ref[hh] = s0t[hh]
            b_sc[hh] = b[hh]
        o_inter = [_dot(q[hh] * jnp.exp(b[hh]), s0t[hh], _NT) for hh in hs]
        scores = [[] for _ in hs]
        for i in range(c // HGRN_SUB):
            lo = i * HGRN_SUB
            for hh in hs:
                ref = b_sc[hh, lo - 1:lo, :] if i > 0 else jnp.zeros((1, LANES), F32)
                qt = q[hh][lo:lo + HGRN_SUB, :] * jnp.exp(b[hh][lo:lo + HGRN_SUB, :] - ref)
                dec = jnp.exp(jnp.minimum(ref - b[hh], EXP_CLAMP))
                scores[hh].append(_dot(qt, k[hh] * dec, _NT))
        a = [jnp.where(_tri(c, True), jnp.concatenate(sc, axis=0), 0.0) for sc in scores]
        for hh in hs:
            o_ref[:, sls[hh]] = o_inter[hh] + _dot(a[hh], v[hh], _NN)
        bl = [b_sc[hh, c - 1:c, :] for hh in hs]
        for hh in hs:
            state_sc[hh] = s0t[hh] * jnp.exp(bl[hh]) + _dot(v[hh], k[hh] * jnp.exp(bl[hh] - b[hh]), _TN)

    tile = pl.BlockSpec((c, hp * LANES), lambda h, i: (i, h))
    return pl.pallas_call(
        body, name=name, grid=(h_n // hp, nc),
        in_specs=[tile, tile, tile, pl.BlockSpec((2, hp * LANES), lambda h, i: (0, h))],
        out_specs=[tile, pl.BlockSpec((hp, None, LANES, LANES), lambda h, i: (h, i, 0, 0))],
        out_shape=[jax.ShapeDtypeStruct((s, d), F32), jax.ShapeDtypeStruct((h_n, nc, LANES, LANES), F32)],
        scratch_shapes=[pltpu.VMEM((hp, LANES, LANES), F32), pltpu.VMEM((hp, c, LANES), F32)],
        compiler_params=_params("parallel", "arbitrary"),
    )(zq, zf, zi, lb_logits)


def _hgrn_bwd(zq, zf, zi, lb_logits, states, do, after, *, name):
    s, d = zq.shape
    h_n, c, hp = d // LANES, HGRN_CHUNK, HGRN_HEADS_PER_STEP
    nc = s // c

    def body(zq_ref, zf_ref, zi_ref, lb_ref, st_ref, do_ref, _, dzq_ref, dzf_ref, dzi_ref, dlb_ref, dstate_sc, b_sc):
        @pl.when(pl.program_id(1) == 0)
        def _():
            dstate_sc[...] = jnp.zeros_like(dstate_sc)
            dlb_ref[...] = jnp.zeros_like(dlb_ref)

        lower, upper = _tri(c, True), _tri(c, False).astype(F32)
        lower_f = lower.astype(F32)
        last_row = lax.broadcasted_iota(jnp.int32, (c, LANES), 0) == c - 1
        hs = range(hp)
        sls = [slice(hh * LANES, (hh + 1) * LANES) for hh in hs]
        lb = [_lower_bound(lb_ref[:, sl]) for sl in sls]
        zq_v = [zq_ref[:, sl] for sl in sls]
        sq = [_sigmoid(z) for z in zq_v]
        q = [zq_v[hh] * sq[hh] for hh in hs]
        sf = [_sigmoid(zf_ref[:, sl]) for sl in sls]
        f = [lb[hh] + (1.0 - lb[hh]) * sf[hh] for hh in hs]
        g = [jnp.log(x) for x in f]
        k = [1.0 - x for x in f]
        v = [zi_ref[:, sl] for sl in sls]
        d_o = [do_ref[:, sl] for sl in sls]
        b = [_dot_f32(lower_f, x) for x in g]
        s0t = [st_ref[hh] for hh in hs]
        ds1t = [dstate_sc[hh] for hh in hs]
        for hh in hs:
            b_sc[hh] = b[hh]
        bl = [b_sc[hh, c - 1:c, :] for hh in hs]
        eb = [jnp.exp(x) for x in b]
        ebl = [jnp.exp(x) for x in bl]
        dec_end = [jnp.exp(bl[hh] - b[hh]) for hh in hs]
        da = [jnp.where(lower, _dot(d_o[hh], v[hh], _NT), 0.0) for hh in hs]
        dq_inter = [_dot(d_o[hh], s0t[hh], _NN) * eb[hh] for hh in hs]
        dk_state = [_dot(v[hh], ds1t[hh], _NN) * dec_end[hh] for hh in hs]
        dv_state = [_dot(k[hh] * dec_end[hh], ds1t[hh], _NT) for hh in hs]
        for hh in hs:
            dstate_sc[hh] = ds1t[hh] * ebl[hh] + _dot(d_o[hh], q[hh] * eb[hh], _TN)
        dk = list(dk_state)
        scores, dq_blocks = [[] for _ in hs], [[] for _ in hs]
        for i in range(c // HGRN_SUB):
            lo = i * HGRN_SUB
            for hh in hs:
                ref = b_sc[hh, lo - 1:lo, :] if i > 0 else jnp.zeros((1, LANES), F32)
                grow = jnp.exp(b[hh][lo:lo + HGRN_SUB, :] - ref)
                qt = q[hh][lo:lo + HGRN_SUB, :] * grow
                dec = jnp.exp(jnp.minimum(ref - b[hh], EXP_CLAMP))
                kd = k[hh] * dec
                scores[hh].append(_dot(qt, kd, _NT))
                da_i = da[hh][lo:lo + HGRN_SUB, :]
                dq_blocks[hh].append(_dot_f32(da_i, kd, _NN) * grow)
                dk[hh] = dk[hh] + _dot_f32(da_i, qt, _TN) * dec
        a = [jnp.where(lower, jnp.concatenate(sc, axis=0), 0.0) for sc in scores]
        dv = [_dot(a[hh], d_o[hh], _TN) + dv_state[hh] for hh in hs]
        dq = [dq_inter[hh] + jnp.concatenate(dq_blocks[hh], axis=0) for hh in hs]
        db_last = [jnp.sum(k[hh] * dk_state[hh], axis=0, keepdims=True)
                   + ebl[hh] * jnp.sum(s0t[hh] * ds1t[hh], axis=0, keepdims=True) for hh in hs]
        db = [q[hh] * dq[hh] - k[hh] * dk[hh] + jnp.where(last_row, db_last[hh], 0.0) for hh in hs]
        dg = [_dot_f32(upper, x) for x in db]
        df = [dg[hh] / f[hh] - dk[hh] for hh in hs]
        for hh in hs:
            sl = sls[hh]
            dzf_ref[:, sl] = (df[hh] * (1.0 - lb[hh]) * sf[hh] * (1.0 - sf[hh])).astype(BF16)
            dlb_ref[:, sl] += jnp.sum(df[hh] * (1.0 - sf[hh]), axis=0, keepdims=True)
            dzq_ref[:, sl] = (dq[hh] * sq[hh] * (1.0 + zq_v[hh] * (1.0 - sq[hh]))).astype(BF16)
            dzi_ref[:, sl] = dv[hh].astype(BF16)

    tile = pl.BlockSpec((c, hp * LANES), lambda h, i: (nc - 1 - i, h))
    out = jax.ShapeDtypeStruct((s, d), BF16)
    return pl.pallas_call(
        body, name=name, grid=(h_n // hp, nc),
        in_specs=[tile, tile, tile, pl.BlockSpec((2, hp * LANES), lambda h, i: (0, h)),
                  pl.BlockSpec((hp, None, LANES, LANES), lambda h, i: (h, nc - 1 - i, 0, 0)), tile,
                  pl.BlockSpec(memory_space=pl.ANY)],
        out_specs=[tile, tile, tile, pl.BlockSpec((1, hp * LANES), lambda h, i: (0, h))],
        out_shape=[out, out, out, jax.ShapeDtypeStruct((1, d), F32)],
        scratch_shapes=[pltpu.VMEM((hp, LANES, LANES), F32), pltpu.VMEM((hp, c, LANES), F32)],
        compiler_params=_params("parallel", "arbitrary"),
    )(zq, zf, zi, lb_logits, states, do, after)


ATTN_SUB_ROWS = 256
LOG2E = 1.4426950408889634
LN2 = 0.6931471805599453
Q_PRESCALE = ATTN_SCALE * LOG2E


def _attn_tile(s):
    return min(1024, max(128, s // 2))


def _causal_pairs(n, q_major):
    pairs = [(i, j) for i in range(n) for j in range(i + 1)] if q_major else [(i, j) for j in range(n) for i in range(j, n)]
    return jnp.asarray([p[0] for p in pairs], jnp.int32), jnp.asarray([p[1] for p in pairs], jnp.int32)


def _sub_scores(qn_ref, qr_ref, k, r, sub, t, diagonal):
    q = jnp.concatenate([qn_ref[r:r + sub, :], qr_ref[r:r + sub, :]], axis=1)
    if not diagonal:
        return q, _dot(q, k, _NT)
    cols = r + sub
    keep = lax.broadcasted_iota(jnp.int32, (sub, cols), 1) <= r + lax.broadcasted_iota(jnp.int32, (sub, cols), 0)
    return q, jnp.where(keep, _dot(q, k[:cols], _NT), -jnp.inf)


def _attn_fwd(qn, qr, kn, kr, v, *, name):
    s, t = qn.shape[0], _attn_tile(qn.shape[0])
    sub = min(t, ATTN_SUB_ROWS)
    q_blk, k_blk = _causal_pairs(s // t, True)

    def body(qi_ref, kj_ref, qn_ref, qr_ref, kn_ref, kr_ref, v_ref, o_ref, lse_ref, m_sc, l_sc, acc_sc):
        p_id = pl.program_id(1)
        i, j = qi_ref[p_id], kj_ref[p_id]

        @pl.when(j == 0)
        def _():
            m_sc[...] = jnp.full_like(m_sc, -jnp.inf)
            l_sc[...] = jnp.zeros_like(l_sc)
            acc_sc[...] = jnp.zeros_like(acc_sc)

        def update(diagonal):
            k = jnp.concatenate([kn_ref[...], kr_ref[...]], axis=1)
            v = v_ref[...]
            starts = list(range(0, t, sub))
            scs = [_sub_scores(qn_ref, qr_ref, k, r, sub, t, diagonal)[1] for r in starts]
            ps, alphas = [], []
            for r, sc in zip(starts, scs):
                m_prev = m_sc[r:r + sub, :]
                m_new = jnp.maximum(m_prev, jnp.max(sc, axis=1, keepdims=True))
                alpha = jnp.exp2(m_prev - m_new)
                p = jnp.exp2(sc - m_new[:, :1])
                l_sc[r:r + sub, :] = alpha * l_sc[r:r + sub, :] + jnp.sum(p, axis=1, keepdims=True)
                m_sc[r:r + sub, :] = m_new
                ps.append(p)
                alphas.append(alpha)
            for r, p, alpha in zip(starts, ps, alphas):
                acc_sc[r:r + sub, :] = alpha * acc_sc[r:r + sub, :] + _dot(p, v[:p.shape[1]], _NN)

        @pl.when(j < i)
        def _():
            update(False)

        @pl.when(j == i)
        def _():
            update(True)
            o_ref[...] = (acc_sc[...] / l_sc[...]).astype(BF16)
            lse_ref[...] = m_sc[...] + jnp.log(l_sc[...]) * LOG2E

    q_spec = pl.BlockSpec((t, LANES), lambda h, p, qi, kj: (qi[p], h))
    k_spec = pl.BlockSpec((t, LANES), lambda h, p, qi, kj: (kj[p], h))
    kr_spec = pl.BlockSpec((t, LANES), lambda h, p, qi, kj: (kj[p], 0))
    stat = pltpu.VMEM((t, LANES), F32)
    return pl.pallas_call(
        body, name=name,
        grid_spec=pltpu.PrefetchScalarGridSpec(
            num_scalar_prefetch=2, grid=(MLA_HEADS, q_blk.shape[0]),
            in_specs=[q_spec, q_spec, k_spec, kr_spec, k_spec], out_specs=[q_spec, q_spec],
            scratch_shapes=[stat, stat, stat]),
        out_shape=[jax.ShapeDtypeStruct(qn.shape, BF16), jax.ShapeDtypeStruct(qn.shape, F32)],
        compiler_params=_params("parallel", "arbitrary"),
    )(q_blk, k_blk, qn, qr, kn, kr, v)


def _attn_bwd(qn, qr, kn, kr, v, do, lse, delta, *, name):
    s, t = qn.shape[0], _attn_tile(qn.shape[0])
    n, sub = s // t, min(t, ATTN_SUB_ROWS)
    q_blk, k_blk = _causal_pairs(n, False)

    def body(qi_ref, kj_ref, qn_ref, qr_ref, kn_ref, kr_ref, v_ref, do_ref, lse_ref, delta_ref,
             dqn_ref, dqr_ref, dkn_ref, dv_ref, dkr_ref, dk_sc, dv_sc):
        p_id = pl.program_id(1)
        i, j = qi_ref[p_id], kj_ref[p_id]

        @pl.when(p_id == 0)
        def _():
            dqn_ref[...] = jnp.zeros_like(dqn_ref)
            dqr_ref[...] = jnp.zeros_like(dqr_ref)

        @pl.when(i == j)
        def _():
            dk_sc[...] = jnp.zeros_like(dk_sc)
            dv_sc[...] = jnp.zeros_like(dv_sc)

        def accumulate(diagonal):
            k = jnp.concatenate([kn_ref[...], kr_ref[...]], axis=1)
            v = v_ref[...]
            starts = list(range(0, t, sub))
            qs, d_os, scs, dps = [], [], [], []
            for r in starts:
                q, sc = _sub_scores(qn_ref, qr_ref, k, r, sub, t, diagonal)
                d_o = do_ref[r:r + sub, :]
                qs.append(q)
                d_os.append(d_o)
                scs.append(sc)
                dps.append(_dot(d_o, v[:sc.shape[1]], _NT))
            ps, dss = [], []
            for r, sc, dp in zip(starts, scs, dps):
                p = jnp.exp2(sc - lse_ref[r:r + sub, :][:, :1])
                ps.append(p.astype(BF16))
                dss.append((p * (dp - delta_ref[r:r + sub, :][:, :1])).astype(BF16))
            for r, q, d_o, p, ds in zip(starts, qs, d_os, ps, dss):
                cols = p.shape[1]
                dv_sc[:cols, :] += _dot(p, d_o, _TN)
                dk_sc[:cols, :] += _dot(ds, q, _TN)
                dq = _dot(ds, k[:cols], _NN) * ATTN_SCALE
                rows = pl.ds(pl.multiple_of(i * t + r, sub), sub)
                dqn_ref[rows, :] += dq[:, :LANES]
                dqr_ref[rows, :] += dq[:, LANES:]

        @pl.when(j < i)
        def _():
            accumulate(False)

        @pl.when(j == i)
        def _():
            accumulate(True)

        @pl.when(i == n - 1)
        def _():
            dkn_ref[...] = (dk_sc[:, :LANES] * LN2).astype(BF16)
            dkr_ref[...] = dk_sc[:, LANES:] * LN2
            dv_ref[...] = dv_sc[...].astype(BF16)

    q_spec = pl.BlockSpec((t, LANES), lambda h, p, qi, kj: (qi[p], h))
    k_spec = pl.BlockSpec((t, LANES), lambda h, p, qi, kj: (kj[p], h))
    kr_spec = pl.BlockSpec((t, LANES), lambda h, p, qi, kj: (kj[p], 0))
    head_spec = pl.BlockSpec((s, LANES), lambda h, p, qi, kj: (0, h))
    f32_out, bf16_out = jax.ShapeDtypeStruct(qn.shape, F32), jax.ShapeDtypeStruct(qn.shape, BF16)
    return pl.pallas_call(
        body, name=name,
        grid_spec=pltpu.PrefetchScalarGridSpec(
            num_scalar_prefetch=2, grid=(MLA_HEADS, q_blk.shape[0]),
            in_specs=[q_spec, q_spec, k_spec, kr_spec, k_spec, q_spec, q_spec, q_spec],
            out_specs=[head_spec, head_spec, k_spec, k_spec, k_spec],
            scratch_shapes=[pltpu.VMEM((t, 2 * LANES), F32), pltpu.VMEM((t, LANES), F32)]),
        out_shape=[f32_out, f32_out, bf16_out, bf16_out, f32_out],
        compiler_params=_params("parallel", "arbitrary"),
    )(q_blk, k_blk, qn, qr, kn, kr, v, do, lse, delta)


def _exchange(arrs, *, scatter, name):
    n = len(arrs)
    out_shape = [jax.ShapeDtypeStruct(a.shape if scatter else (N_DEV, *a.shape), a.dtype) for a in arrs]

    def body(*refs):
        ins, outs = refs[:n], refs[n:2 * n]
        send_sems, recv_sems, local_sems = refs[2 * n:]
        x, y, c = lax.axis_index("x"), lax.axis_index("y"), lax.axis_index("c")
        me = 4 * x + 2 * y + c
        copies = []
        for k in range(n):
            local = pltpu.make_async_copy(ins[k].at[me] if scatter else ins[k], outs[k].at[me], local_sems.at[k])
            local.start()
            copies.append(local)
            for d in range(1, N_DEV):
                px, py, pc = (x + (d >> 2)) % 2, (y + ((d >> 1) & 1)) % 2, (c + (d & 1)) % 2
                peer = 4 * px + 2 * py + pc
                remote = pltpu.make_async_remote_copy(
                    src_ref=ins[k].at[peer] if scatter else ins[k], dst_ref=outs[k].at[me],
                    send_sem=send_sems.at[k, d - 1], recv_sem=recv_sems.at[k, d - 1],
                    device_id=(px, py, pc), device_id_type=pl.DeviceIdType.MESH)
                remote.start()
                copies.append(remote)
        for cp in copies:
            cp.wait()

    any_spec = pl.BlockSpec(memory_space=pl.ANY)
    return pl.pallas_call(
        body, name=name, in_specs=[any_spec] * n, out_specs=[any_spec] * n, out_shape=out_shape,
        scratch_shapes=[pltpu.SemaphoreType.DMA((n, N_DEV - 1)), pltpu.SemaphoreType.DMA((n, N_DEV - 1)),
                        pltpu.SemaphoreType.DMA((n,))],
    )(*arrs)


def _peers(x, y, c):
    out = []
    for d in range(1, N_DEV):
        px, py, pc = (x + (d >> 2)) % 2, (y + ((d >> 1) & 1)) % 2, (c + (d & 1)) % 2
        out.append(((px, py, pc), 4 * px + 2 * py + pc))
    return out


def _exchange_copies(ins, lands, send_sems, recv_sems, scatter):
    x, y, c = lax.axis_index("x"), lax.axis_index("y"), lax.axis_index("c")
    me = 4 * x + 2 * y + c
    local, remote = [], []
    for k in range(len(ins)):
        local.append(pltpu.make_async_copy(ins[k].at[me] if scatter else ins[k], lands[k].at[me],
                                           recv_sems.at[k * N_DEV + N_DEV - 1]))
        for d, (coords, peer) in enumerate(_peers(x, y, c)):
            remote.append(pltpu.make_async_remote_copy(
                src_ref=ins[k].at[peer] if scatter else ins[k], dst_ref=lands[k].at[me],
                send_sem=send_sems.at[k * N_DEV + d], recv_sem=recv_sems.at[k * N_DEV + d],
                device_id=coords, device_id_type=pl.DeviceIdType.MESH))
    return local, remote


def _exchange_start(arrs, *, scatter, name, after=None):
    n = len(arrs)
    hbm = pl.BlockSpec(memory_space=pltpu.HBM)
    sem = pl.BlockSpec(memory_space=pltpu.SEMAPHORE)
    lands = [lax.empty(a.shape if scatter else (N_DEV, *a.shape), a.dtype) for a in arrs]

    def body(*refs):
        ins, land_refs = refs[:n], refs[n:2 * n]
        first_out = 2 * n + (after is not None)
        send_sems, recv_sems, token = refs[first_out], refs[first_out + 1], refs[-1]
        local, remote = _exchange_copies(ins, land_refs, send_sems, recv_sems, scatter)
        for cp in local + remote:
            cp.start()
        token[...] = jnp.zeros_like(token)

    operands = [pltpu.with_memory_space_constraint(a, pltpu.HBM) for a in list(arrs) + lands]
    behind = [] if after is None else [after]
    res = pl.pallas_call(
        body, name=name,
        out_shape=(pltpu.SemaphoreType.DMA((n * N_DEV,)), pltpu.SemaphoreType.DMA((n * N_DEV,)),
                   *[pltpu.HBM(o.shape, o.dtype) for o in operands], jax.ShapeDtypeStruct((8, LANES), F32)),
        in_specs=[hbm] * (2 * n) + [pl.BlockSpec(memory_space=pl.ANY)] * len(behind),
        out_specs=(sem, sem, *[hbm] * (2 * n), pl.BlockSpec(memory_space=pltpu.VMEM)),
        input_output_aliases={i: 2 + i for i in range(2 * n)},
        compiler_params=pltpu.CompilerParams(has_side_effects=pltpu.SideEffectType.DATAFLOW_SIDE_EFFECTING),
    )(*operands, *behind)
    return (res[0], res[1], list(res[2:2 + n]), list(res[2 + n:2 + 2 * n]), scatter), res[-1]


def _exchange_wait(state, after, *, name):
    send_sems, recv_sems, ins, lands, scatter = state
    n = len(ins)
    hbm = pl.BlockSpec(memory_space=pltpu.HBM)
    sem = pl.BlockSpec(memory_space=pltpu.SEMAPHORE)

    def body(*refs):
        in_refs, land_refs = refs[:n], refs[n:2 * n]
        local, remote = _exchange_copies(in_refs, land_refs, refs[2 * n], refs[2 * n + 1], scatter)
        for cp in local:
            cp.wait()
        for cp in remote:
            cp.wait_send()
            cp.wait_recv()

    res = pl.pallas_call(
        body, name=name, out_shape=tuple(pltpu.HBM(o.shape, o.dtype) for o in ins + lands),
        in_specs=[hbm] * (2 * n) + [sem, sem, pl.BlockSpec(memory_space=pl.ANY)], out_specs=tuple([hbm] * (2 * n)),
        input_output_aliases={i: i for i in range(2 * n)},
        compiler_params=pltpu.CompilerParams(has_side_effects=pltpu.SideEffectType.DATAFLOW_SIDE_EFFECTING),
    )(*ins, *lands, send_sems, recv_sems, after)
    return list(res[n:])


def _adam(w, terms, m, v, *, name):
    r, c = w.shape
    n = terms.shape[0]
    tr = min(r, 128)
    assert r % tr == 0

    def body(w_ref, t_ref, m_ref, v_ref, g_out, d_out, m_out, v_out):
        g = t_ref[0].astype(F32)
        for s in range(1, n):
            g = g + t_ref[s].astype(F32)
        m1 = ADAM_B1 * m_ref[...] + (1.0 - ADAM_B1) * g
        v1 = ADAM_B2 * v_ref[...] + (1.0 - ADAM_B2) * jnp.square(g)
        m_hat = m1 / (1.0 - ADAM_B1 ** ADAM_STEP)
        v_hat = v1 / (1.0 - ADAM_B2 ** ADAM_STEP)
        g_out[...] = g
        d_out[...] = -ADAM_LR * (m_hat / (jnp.sqrt(v_hat) + ADAM_EPS) + ADAM_WD * w_ref[...])
        m_out[...] = m1
        v_out[...] = v1

    spec = pl.BlockSpec((tr, c), lambda i: (i, 0))
    out = jax.ShapeDtypeStruct((r, c), F32)
    return pl.pallas_call(
        body, name=name, grid=(r // tr,),
        in_specs=[spec, pl.BlockSpec((n, tr, c), lambda i: (0, i, 0)), spec, spec], out_specs=[spec] * 4,
        out_shape=[out] * 4, compiler_params=_params("parallel"),
    )(w, terms, m, v)


def _sum_terms(terms, *, name):
    n, _, p = terms.shape

    def body(t_ref, o_ref):
        acc = t_ref[0]
        for s in range(1, n):
            acc = acc + t_ref[s]
        o_ref[...] = acc

    return pl.pallas_call(body, name=name, out_shape=jax.ShapeDtypeStruct((1, p), F32))(terms)


def _lb_logits_grad(dlb, logits, *, name):
    def body(dlb_ref, l_ref, o_ref):
        lb = _lower_bound(l_ref[...])
        d0 = dlb_ref[...] * lb * (1.0 - lb)
        o_ref[...] = jnp.concatenate([d0, -d0], axis=0)

    return pl.pallas_call(body, name=name, out_shape=jax.ShapeDtypeStruct(logits.shape, F32))(dlb, logits)


def _silu_grad(z):
    sg = _sigmoid(z)
    return sg * (1.0 + z * (1.0 - sg))


def _head_norm_gate(o, zg, gn):
    outs = []
    for h in range(HGRN_HEADS):
        sl = slice(h * LANES, (h + 1) * LANES)
        zg_h = zg[:, sl]
        outs.append(_rms(o[:, sl], gn) * (zg_h * _sigmoid(zg_h)))
    return (jnp.concatenate(outs, axis=1),)


def _head_norm_gate_bwd(o, zg, dm, gn):
    do_parts, dzg_parts, dgn = [], [], jnp.zeros((1, LANES), F32)
    for h in range(HGRN_HEADS):
        sl = slice(h * LANES, (h + 1) * LANES)
        o_h, zg_h, dm_h = o[:, sl], zg[:, sl], dm[:, sl]
        gate = zg_h * _sigmoid(zg_h)
        do_h, dgn_h = _rms_bwd(o_h, gn, dm_h * gate)
        dgn = dgn + dgn_h
        do_parts.append(do_h)
        dzg_parts.append(dm_h * _rms(o_h, gn) * _silu_grad(zg_h))
    return jnp.concatenate(do_parts, axis=1), jnp.concatenate(dzg_parts, axis=1), dgn


def _rope_slabs(x, t_c, t_s1, t_s2, transpose):
    fn = _rope_t if transpose else _rope
    return jnp.concatenate(
        [fn(x[:, h * LANES:(h + 1) * LANES], t_c, t_s1, t_s2) for h in range(x.shape[1] // LANES)], axis=1)


def _loss_head(h, tgt, w):
    d = h.shape[1]
    r = lax.rsqrt(jnp.mean(h * h, axis=-1, keepdims=True) + EPS)
    xh = h * r
    err = xh * w - tgt
    loss = 0.5 * jnp.sum(jnp.mean(err * err, axis=-1, keepdims=True), axis=0, keepdims=True)
    dy = err / d
    dxh = dy * w
    dh = r * (dxh - xh * jnp.mean(dxh * xh, axis=-1, keepdims=True))
    return dh, dh, jnp.sum(dy * xh, axis=0, keepdims=True), jnp.broadcast_to(loss, (1, LANES))


def _mlp_fwd(h, norm, w_up, w_down, tag):
    d = h.shape[1]

    def up(x, g, wu):
        x_n = _rms(x, g).astype(BF16)
        return x_n, jnp.square(jnp.maximum(_dot(x_n, wu, _NN), 0.0))

    xn, act = _rowcall(up, [h], [norm, w_up], [(d, BF16), (w_up.shape[1], BF16)], [], tr=512, name=f"{tag}_up")
    return _mm(act, w_down, mode="nn", add=h, name=f"{tag}_down"), (h, xn, act)


def _mlp_bwd(dh_out, dh_out_bf, saved, norm, w_up, w_down, tag, after=None):
    h, xn, act = saved
    d = h.shape[1]
    du = _mm(dh_out_bf, w_down, mode="nt", relu2_of=act, out_dtype=BF16, after=after, name=f"{tag}_bwd_du")
    dw_down = _mm(act, dh_out_bf, mode="tn", name=f"{tag}_bwd_wdown")
    dw_up = _mm(xn, du, mode="tn", name=f"{tag}_bwd_wup")

    def up_norm_bwd(x, d_u, dres, g, wu):
        dx, dw = _rms_bwd(x, g, _dot(d_u, wu, _NT))
        return dx + dres, dx + dres, dw

    dh, dh_bf, dnorm = _rowcall(up_norm_bwd, [h, du, dh_out], [norm, w_up], [(d, F32), (d, BF16)], [d], tr=512,
                                name=f"{tag}_bwd_dxn")
    return dh, dh_bf, dnorm, dw_up, dw_down


def _row_major(g):
    return g.reshape(g.shape[0] * g.shape[1], g.shape[2])


def _col_major(g):
    return jnp.transpose(g, (1, 0, 2)).reshape(g.shape[1], g.shape[0] * g.shape[2])


def _col_terms(dw):
    k, n = dw.shape
    return jnp.transpose(dw.reshape(k, N_DEV, n // N_DEV), (1, 0, 2))


def _row_terms(dw):
    return dw.reshape(N_DEV, dw.shape[0] // N_DEV, dw.shape[1])


def kernel(x, hgrn_norm, hgrn_w_q, hgrn_w_f, hgrn_w_i, hgrn_w_g, hgrn_g_norm, hgrn_w_o, hgrn_lb_logits, mla_norm, mla_w_dq, mla_q_norm, mla_w_uq, mla_w_o, kv_in_norm, kv_w_dkv, kv_norm, kv_w_uk, kv_w_uv, mlp_norm, mlp_w_up, mlp_w_down, final_norm, loss_target, m_hgrn_norm, m_hgrn_w_q, m_hgrn_w_f, m_hgrn_w_i, m_hgrn_w_g, m_hgrn_g_norm, m_hgrn_w_o, m_hgrn_lb_logits, m_mla_norm, m_mla_w_dq, m_mla_q_norm, m_mla_w_uq, m_mla_w_o, m_kv_in_norm, m_kv_w_dkv, m_kv_norm, m_kv_w_uk, m_kv_w_uv, m_mlp_norm, m_mlp_w_up, m_mlp_w_down, m_final_norm, v_hgrn_norm, v_hgrn_w_q, v_hgrn_w_f, v_hgrn_w_i, v_hgrn_w_g, v_hgrn_g_norm, v_hgrn_w_o, v_hgrn_lb_logits, v_mla_norm, v_mla_w_dq, v_mla_q_norm, v_mla_w_uq, v_mla_w_o, v_kv_in_norm, v_kv_w_dkv, v_kv_norm, v_kv_w_uk, v_kv_w_uv, v_mlp_norm, v_mlp_w_up, v_mlp_w_down, v_final_norm):
    given = dict(locals())
    weight_names = ["hgrn_norm", "hgrn_w_q", "hgrn_w_f", "hgrn_w_i", "hgrn_w_g", "hgrn_g_norm", "hgrn_w_o",
                    "hgrn_lb_logits", "mla_norm", "mla_w_dq", "mla_q_norm", "mla_w_uq", "mla_w_o", "kv_in_norm",
                    "kv_w_dkv", "kv_norm", "kv_w_uk", "kv_w_uv", "mlp_norm", "mlp_w_up", "mlp_w_down", "final_norm"]
    me = 4 * lax.axis_index("x") + 2 * lax.axis_index("y") + lax.axis_index("c")
    xs, tgt = x[0], loss_target[0]
    seq, d_model = xs.shape
    n_heads, hd = MLA_HEADS, LANES

    big_local = {
        "hgrn_w_q": hgrn_w_q[0], "hgrn_w_f": hgrn_w_f[0], "hgrn_w_i": hgrn_w_i[0], "hgrn_w_g": hgrn_w_g[0],
        "hgrn_w_o": hgrn_w_o[0], "mla_w_dq": mla_w_dq[0], "mla_w_uq": mla_w_uq[0], "mla_w_o": mla_w_o[0],
        "kv_w_dkv": kv_w_dkv, "kv_w_uk": kv_w_uk, "kv_w_uv": kv_w_uv,
        "mlp_w_up0": mlp_w_up[0], "mlp_w_up1": mlp_w_up[1], "mlp_w_down0": mlp_w_down[0], "mlp_w_down1": mlp_w_down[1],
    }
    big_names = list(big_local)
    col_sharded = {"mla_w_uq", "kv_w_uk", "kv_w_uv", "mlp_w_up0", "mlp_w_up1"}
    vec_local = jnp.concatenate([hgrn_norm, hgrn_lb_logits], axis=0)
    first_names = ["hgrn_w_q", "hgrn_w_f", "hgrn_w_i", "hgrn_w_g"]
    later_names = {"mlp0": ["hgrn_w_o", "mlp_w_up0", "mlp_w_down0"],
                   "mla": ["kv_w_dkv", "kv_w_uk", "kv_w_uv", "mla_w_dq", "mla_w_uq", "mla_w_o"],
                   "mlp1": ["mlp_w_up1", "mlp_w_down1"]}

    def unshard(names, arrays):
        return {k: (_col_major(a) if k in col_sharded else _row_major(a)) for k, a in zip(names, arrays)}

    first_state, token = _exchange_start([big_local[k].astype(BF16) for k in first_names] + [vec_local], scatter=False,
                                         name="gather_first_start")
    gather_state = {}
    for tag, names in later_names.items():
        gather_state[tag], token = _exchange_start([big_local[k].astype(BF16) for k in names], scatter=False,
                                                   after=token, name=f"gather_{tag}_start")

    def gather_wait(tag, after):
        w.update(unshard(later_names[tag], _exchange_wait(gather_state[tag], after, name=f"gather_{tag}_wait")))

    gathered = _exchange_wait(first_state, token, name="gather_first_wait")
    w = unshard(first_names, gathered[:-1])
    vec_full = jnp.transpose(gathered[-1], (1, 0, 2)).reshape(3, d_model)
    hgrn_norm_full, lb_logits_full = vec_full[0:1], vec_full[1:3]
    t_c, t_s1, t_s2 = _rope_tables(seq)
    kv_lora = kv_w_uk.shape[0]

    def hgrn_proj(a, g, *weights):
        xn = _rms(a, g).astype(BF16)
        return (xn, *[_dot(xn, wt, _NN) for wt in weights])

    xn0, zq, zf, zi, zg = _rowcall(hgrn_proj, [xs], [hgrn_norm_full] + [w[k] for k in first_names],
                                   [(d_model, BF16)] + [(d_model, F32)] * 4, [], tr=512, name="hgrn_proj")
    o_rec, states = _hgrn_fwd(zq, zf, zi, lb_logits_full, name="hgrn_fwd")
    mixed = _rowcall(_head_norm_gate, [o_rec, zg], [hgrn_g_norm], [(d_model, BF16)], [], name="hgrn_gate")[0]
    gather_wait("mlp0", mixed)
    h1 = _mm(mixed, w["hgrn_w_o"], mode="nn", add=xs, name="hgrn_out")
    h2, mlp0_saved = _mlp_fwd(h1, mlp_norm[0:1], w["mlp_w_up0"], w["mlp_w_down0"], "mlp0")
    gather_wait("mla", h2)
    w_uq3 = w["mla_w_uq"].reshape(-1, n_heads, MLA_NOPE + MLA_ROPE)
    w_uq_nope = w_uq3[:, :, :MLA_NOPE].reshape(-1, n_heads * hd)
    w_uq_rope = jnp.pad(w_uq3[:, :, MLA_NOPE:], ((0, 0), (0, 0), (0, hd - MLA_ROPE))).reshape(-1, n_heads * hd)
    w_dkv_pad = jnp.pad(w["kv_w_dkv"], ((0, 0), (0, kv_lora + hd - w["kv_w_dkv"].shape[1])))

    q_lora, qk_cols = w["mla_w_dq"].shape[1], n_heads * hd

    def mla_qkv(a, tc, ts1, ts2, g_kv_in, g_mla, g_q, g_kv, wdq, wn, wr, wdkv, wuk, wuv):
        h_n, x_n = _rms(a, g_kv_in).astype(BF16), _rms(a, g_mla).astype(BF16)
        cq = _dot(x_n, wdq, _NN)
        cq_n = _rms(cq, g_q).astype(BF16)
        q_nope = _dot(cq_n, wn, _NN) * Q_PRESCALE
        q_rope = _rope_slabs(_dot(cq_n, wr, _NN) * Q_PRESCALE, tc, ts1, ts2, False)
        c_all = _dot(h_n, wdkv, _NN)
        lat = _rms(c_all[:, :kv_lora], g_kv).astype(BF16)
        return (h_n, x_n, cq, cq_n, q_nope, q_rope, c_all, lat, _rope(c_all[:, kv_lora:], tc, ts1, ts2),
                _dot(lat, wuk, _NN), _dot(lat, wuv, _NN))

    hn, xn2, cq_pre, c_q, qn, qr, ckr, c_kv, kr, kn, vv = _rowcall(
        mla_qkv, [h2, t_c, t_s1, t_s2],
        [kv_in_norm[None, :], mla_norm, mla_q_norm, kv_norm[None, :], w["mla_w_dq"], w_uq_nope, w_uq_rope, w_dkv_pad,
         w["kv_w_uk"], w["kv_w_uv"]],
        [(d_model, BF16), (d_model, BF16), (q_lora, F32), (q_lora, BF16), (qk_cols, BF16), (qk_cols, BF16),
         (kv_lora + hd, F32), (kv_lora, BF16), (hd, BF16), (qk_cols, BF16), (qk_cols, BF16)], [], name="mla_qkv")
    o_att, lse = _attn_fwd(qn, qr, kn, kr, vv, name="attn_fwd")
    h3 = _mm(o_att, w["mla_w_o"], mode="nn", add=h2, name="attn_out")
    gather_wait("mlp1", h3)
    h4, mlp1_saved = _mlp_fwd(h3, mlp_norm[1:2], w["mlp_w_up1"], w["mlp_w_down1"], "mlp1")
    dh4, dh4_bf, g_final_norm, loss_part = _rowcall(_loss_head, [h4, tgt], [final_norm[None, :]],
                                                    [(d_model, F32), (d_model, BF16)], [d_model, LANES], name="loss_head")

    g = {}
    groups = {"mlp1": ["mlp_w_up1", "mlp_w_down1"],
              "mla": ["mla_w_o", "mla_w_uq", "mla_w_dq", "kv_w_uk", "kv_w_uv", "kv_w_dkv"],
              "mlp0": ["mlp_w_up0", "mlp_w_down0"],
              "hgrn_out": ["hgrn_w_o", "hgrn_w_g"],
              "hgrn_in": ["hgrn_w_q", "hgrn_w_f", "hgrn_w_i"]}
    scatter_state = {}

    def scatter_start(tag, after=None):
        scatter_state[tag], tok = _exchange_start(
            [(_col_terms if k in col_sharded else _row_terms)(g[k]) for k in groups[tag]], scatter=True, after=after,
            name=f"scatter_{tag}_start")
        return tok

    dh3, dh3_bf, g_mlp_norm1, g["mlp_w_up1"], g["mlp_w_down1"] = _mlp_bwd(
        dh4, dh4_bf, mlp1_saved, mlp_norm[1:2], w["mlp_w_up1"], w["mlp_w_down1"], "mlp1")
    d_oatt = _mm(dh3_bf, w["mla_w_o"], mode="nt", out_dtype=BF16, after=scatter_start("mlp1"), name="attn_out_bwd_x")
    g["mla_w_o"] = _mm(o_att, dh3_bf, mode="tn", name="attn_out_bwd_w")

    def head_delta(do, o):
        prod = do.astype(F32) * o.astype(F32)
        return (jnp.concatenate([jnp.broadcast_to(jnp.sum(prod[:, h * hd:(h + 1) * hd], axis=1, keepdims=True),
                                                  (prod.shape[0], hd)) for h in range(n_heads)], axis=1),)

    delta = _rowcall(head_delta, [d_oatt, o_att], [], [(n_heads * hd, F32)], [], name="attn_delta")[0]
    dqn, dqr, dkn, dvv, dkr = _attn_bwd(qn, qr, kn, kr, vv, d_oatt, lse, delta, name="attn_bwd")

    def q_path_bwd(cq, cq_n, x_n, d_qn, d_qr, tc, ts1, ts2, g_q, wdq, wn, wr):
        d_qn, d_qr = d_qn.astype(BF16), _rope_slabs(d_qr, tc, ts1, ts2, True).astype(BF16)
        d_cq, d_gq = _rms_bwd(cq, g_q, _dot(d_qn, wn, _NT) + _dot(d_qr, wr, _NT))
        d_cq = d_cq.astype(BF16)
        return _dot(d_cq, wdq, _NT), d_gq, _dot(x_n, d_cq, _TN), _dot(cq_n, d_qn, _TN), _dot(cq_n, d_qr, _TN)

    dxn2, g_q_norm, g_dq, g_uq_nope, g_uq_rope = _rowcall(
        q_path_bwd, [cq_pre, c_q, xn2, dqn, dqr, t_c, t_s1, t_s2], [mla_q_norm, w["mla_w_dq"], w_uq_nope, w_uq_rope],
        [(d_model, F32)], [q_lora, (d_model, q_lora), (q_lora, qk_cols), (q_lora, qk_cols)], name="mla_q_bwd")
    g["mla_w_dq"] = g_dq.astype(GRAD_WIRE_DTYPE)
    g["mla_w_uq"] = jnp.concatenate([g_uq_nope.reshape(q_lora, n_heads, hd),
                                     g_uq_rope.reshape(q_lora, n_heads, hd)[:, :, :MLA_ROPE]],
                                    axis=2).reshape(q_lora, -1).astype(GRAD_WIRE_DTYPE)

    def kv_path_bwd(c_all, lat, h_n, d_kn, d_v, d_kr_heads, tc, ts1, ts2, g_kv, wdkv, wuk, wuv):
        d_lat, d_gkv = _rms_bwd(c_all[:, :kv_lora], g_kv, _dot(d_kn, wuk, _NT) + _dot(d_v, wuv, _NT))
        d_kr = d_kr_heads[:, :hd]
        for h in range(1, n_heads):
            d_kr = d_kr + d_kr_heads[:, h * hd:(h + 1) * hd]
        d_all = jnp.concatenate([d_lat, _rope_t(d_kr, tc, ts1, ts2)], axis=1).astype(BF16)
        return _dot(d_all, wdkv, _NT), d_gkv, _dot(h_n, d_all, _TN), _dot(lat, d_kn, _TN), _dot(lat, d_v, _TN)

    dhn, g_kv_norm, g_dkv, g_uk, g_uv = _rowcall(
        kv_path_bwd, [ckr, c_kv, hn, dkn, dvv, dkr, t_c, t_s1, t_s2], [kv_norm[None, :], w_dkv_pad, w["kv_w_uk"], w["kv_w_uv"]],
        [(d_model, F32)], [kv_lora, (d_model, kv_lora + hd), (kv_lora, qk_cols), (kv_lora, qk_cols)], name="mla_kv_bwd")
    g["kv_w_dkv"] = g_dkv[:, :kv_w_dkv.shape[1]].astype(GRAD_WIRE_DTYPE)
    g["kv_w_uk"], g["kv_w_uv"] = g_uk.astype(GRAD_WIRE_DTYPE), g_uv.astype(GRAD_WIRE_DTYPE)

    def kv_mla_norm_bwd(a, d1, d2, dres, g1, g2):
        dx1, dw1 = _rms_bwd(a, g1, d1)
        dx2, dw2 = _rms_bwd(a, g2, d2)
        return dx1 + dx2 + dres, dx1 + dx2 + dres, dw1, dw2

    dh2, dh2_bf, g_kv_in_norm, g_mla_norm = _rowcall(
        kv_mla_norm_bwd, [h2, dhn, dxn2, dh3], [kv_in_norm[None, :], mla_norm], [(d_model, F32), (d_model, BF16)],
        [d_model, d_model], name="kv_mla_norm_bwd")
    dh1, dh1_bf, g_mlp_norm0, g["mlp_w_up0"], g["mlp_w_down0"] = _mlp_bwd(
        dh2, dh2_bf, mlp0_saved, mlp_norm[0:1], w["mlp_w_up0"], w["mlp_w_down0"], "mlp0", after=scatter_start("mla"))

    dmixed = _mm(dh1_bf, w["hgrn_w_o"], mode="nt", after=scatter_start("mlp0"), name="hgrn_out_bwd_x")
    g["hgrn_w_o"] = _mm(mixed, dh1_bf, mode="tn", name="hgrn_out_bwd_w")
    do_rec, dzg, g_g_norm = _rowcall(_head_norm_gate_bwd, [o_rec, zg, dmixed], [hgrn_g_norm],
                                     [(d_model, F32), (d_model, BF16)], [hd], name="hgrn_gate_bwd")
    g["hgrn_w_g"] = _mm(xn0, dzg, mode="tn", name="hgrn_w_g_bwd_w")
    dzq, dzf, dzi, g_lb = _hgrn_bwd(zq, zf, zi, lb_logits_full, states, do_rec, scatter_start("hgrn_out"),
                                    name="hgrn_bwd")
    for nm, dz in (("hgrn_w_q", dzq), ("hgrn_w_f", dzf), ("hgrn_w_i", dzi)):
        g[nm] = _mm(xn0, dz, mode="tn", name=f"{nm}_bwd_w")

    def hgrn_proj_bwd(a, dres, *rest):
        dzs, gw, weights = rest[:4], rest[4], rest[5:]
        dxn = _dot(dzs[0], weights[0], _NT)
        for dz, wt in zip(dzs[1:], weights[1:]):
            dxn = dxn + _dot(dz, wt, _NT)
        dx, dw = _rms_bwd(a, gw, dxn)
        return dx + dres, dw

    grad_x, g_hgrn_norm = _rowcall(hgrn_proj_bwd, [xs, dh1, dzq, dzf, dzi, dzg],
                                   [hgrn_norm_full] + [w[k] for k in first_names], [(d_model, F32)], [d_model],
                                   tr=512, name="hgrn_proj_bwd")

    small_parts = [g_hgrn_norm, g_lb, g_g_norm, g_mla_norm, g_q_norm, g_kv_in_norm, g_kv_norm, g_mlp_norm0,
                   g_mlp_norm1, g_final_norm, loss_part]
    small_sizes = [p.shape[1] for p in small_parts]
    small_terms = _exchange([jnp.concatenate(small_parts, axis=1)], scatter=False, name="gather_small")[0]
    small_sum = _sum_terms(small_terms, name="sum_small")
    last = scatter_start("hgrn_in", after=small_sum)
    offs = [0]
    for sz in small_sizes:
        offs.append(offs[-1] + sz)
    (s_hgrn_norm, s_lb, s_g_norm, s_mla_norm, s_q_norm, s_kv_in_norm, s_kv_norm, s_mlp_norm0, s_mlp_norm1, s_final_norm,
     s_loss) = [small_sum[:, a:b] for a, b in zip(offs[:-1], offs[1:])]
    shard = hgrn_norm.shape[1]
    g_lb_logits = _lb_logits_grad(lax.dynamic_slice_in_dim(s_lb, me * shard, shard, axis=1), hgrn_lb_logits,
                                  name="lb_logits_grad")
    loss = s_loss[0, 0]

    res = {}
    for tag, names in groups.items():
        for k, t in zip(names, _exchange_wait(scatter_state[tag], last, name=f"scatter_{tag}_wait")):
            if k.startswith("mlp_w_"):
                base, layer = k[:-1], int(k[-1])
                wk, mk, vk = given[base][layer], given["m_" + base][layer], given["v_" + base][layer]
            else:
                wk, mk, vk = given[k], given["m_" + k], given["v_" + k]
            shape = wk.shape
            wk, mk, vk = (a.reshape(shape[-2], shape[-1]) for a in (wk, mk, vk))
            upd = _adam(wk, t, mk, vk, name=f"adam_{k}")
            last = upd[0]
            res[k] = [o.reshape(shape) for o in upd]
    for base in ("mlp_w_up", "mlp_w_down"):
        res[base] = [jnp.stack([res[base + "0"][i], res[base + "1"][i]], axis=0) for i in range(4)]

    small_grads = {
        "hgrn_norm": lax.dynamic_slice_in_dim(s_hgrn_norm, me * shard, shard, axis=1),
        "hgrn_g_norm": s_g_norm, "hgrn_lb_logits": g_lb_logits, "mla_norm": s_mla_norm, "mla_q_norm": s_q_norm,
        "kv_in_norm": s_kv_in_norm, "kv_norm": s_kv_norm,
        "mlp_norm": jnp.concatenate([s_mlp_norm0, s_mlp_norm1], axis=0), "final_norm": s_final_norm,
    }
    small_names = list(small_grads)

    def flat(a):
        return a.reshape(1, -1)

    packed = [jnp.concatenate([flat(src[pre + k]) for k in small_names], axis=1)
              for src, pre in ((given, ""), (small_grads, ""), (given, "m_"), (given, "v_"))]
    small_out = _adam(packed[0], packed[1][None], packed[2], packed[3], name="adam_small")
    off = 0
    for k in small_names:
        size = given[k].size
        res[k] = [o[:, off:off + size].reshape(given[k].shape) for o in small_out]
        off += size

    outs = [loss, grad_x[None]]
    for i in range(4):
        outs += [res[k][i] for k in weight_names]
    return tuple(outs)
```

```python
import functools

import jax
import jax.numpy as jnp
from jax import lax
from jax.experimental import pallas as pl
from jax.experimental.pallas import tpu as pltpu

F32 = jnp.float32
BF16 = jnp.bfloat16

EPS = 1e-6
LANES = 128
N_DEV = 8
V7X_VMEM_LIMIT_BYTES = 56 << 20
MM_PIPELINE_BYTES = 30 << 20
MM_ROW_TILE = 512
GRAD_WIRE_DTYPE = BF16

HGRN_HEADS = 8
HGRN_CHUNK = 64
HGRN_SUB = 16
HGRN_HEADS_PER_STEP = 8
EXP_CLAMP = 80.0
MLA_HEADS = 16
MLA_NOPE = 128
MLA_ROPE = 64
ROPE_THETA = 10000.0
ATTN_SCALE = (MLA_NOPE + MLA_ROPE) ** -0.5

ADAM_LR = 0.001
ADAM_B1 = 0.9
ADAM_B2 = 0.999
ADAM_EPS = 1e-08
ADAM_WD = 0.01
ADAM_STEP = 10

_NN = ((1,), (0,))
_NT = ((1,), (1,))
_TN = ((0,), (0,))


def _params(*sem):
    return pltpu.CompilerParams(dimension_semantics=sem, vmem_limit_bytes=V7X_VMEM_LIMIT_BYTES)


def _dot(a, b, dims):
    return lax.dot_general(a.astype(BF16), b.astype(BF16), (dims, ((), ())), preferred_element_type=F32)


def _dot_f32(a, b, dims=_NN):
    return lax.dot_general(a, b, (dims, ((), ())), precision=lax.Precision.HIGH, preferred_element_type=F32)


def _sigmoid(x):
    return 1.0 / (1.0 + jnp.exp(-x))


def _rms(x, w):
    r = lax.rsqrt(jnp.mean(x * x, axis=-1, keepdims=True) + EPS)
    return x * r * w


def _rms_bwd(x, w, dy):
    r = lax.rsqrt(jnp.mean(x * x, axis=-1, keepdims=True) + EPS)
    xh = x * r
    dw = jnp.sum(dy * xh, axis=0, keepdims=True)
    dxh = dy * w
    dx = r * (dxh - xh * jnp.mean(dxh * xh, axis=-1, keepdims=True))
    return dx, dw


def _mm_tiles(m, n, k, a_bytes, b_bytes, out_tile_bytes):
    tm = min(m, MM_ROW_TILE)
    for tn in (n, 2048, 1024, 512, 256, LANES):
        if tn <= n and n % tn == 0:
            if 2 * (tm * k * a_bytes + k * tn * b_bytes + tm * tn * out_tile_bytes) <= MM_PIPELINE_BYTES:
                return tm, tn
    return tm, min(n, LANES)


def _mm(a, b, *, mode, name, out_dtype=None, add=None, relu2_of=None, after=None):
    if mode == "nn":
        (m, k), (k2, n) = a.shape, b.shape
    elif mode == "nt":
        (m, k), (n, k2) = a.shape, b.shape
    else:
        (k, m), (k2, n) = a.shape, b.shape
    assert k == k2, (name, a.shape, b.shape)
    if out_dtype is None:
        out_dtype = GRAD_WIRE_DTYPE if mode == "tn" else F32
    tile_bytes = sum(x.dtype.itemsize for x in (add, relu2_of) if x is not None) + jnp.dtype(out_dtype).itemsize
    tm, tn = _mm_tiles(m, n, k, a.dtype.itemsize, b.dtype.itemsize, tile_bytes)
    assert m % tm == 0 and n % tn == 0, (name, m, n)
    dims = {"nn": _NN, "nt": _NT, "tn": _TN}[mode]
    a_spec = pl.BlockSpec((k, tm), lambda i, j: (0, i)) if mode == "tn" else pl.BlockSpec((tm, k), lambda i, j: (i, 0))
    b_spec = pl.BlockSpec((tn, k), lambda i, j: (j, 0)) if mode == "nt" else pl.BlockSpec((k, tn), lambda i, j: (0, j))
    o_spec = pl.BlockSpec((tm, tn), lambda i, j: (i, j))
    operands, in_specs = [a, b], [a_spec, b_spec]
    for extra in (add, relu2_of):
        if extra is not None:
            assert extra.shape == (m, n), (name, extra.shape)
            operands.append(extra)
            in_specs.append(o_spec)
    n_in = len(operands)
    if after is not None:
        operands.append(after)
        in_specs.append(pl.BlockSpec(memory_space=pl.ANY))
    out_shape = jax.ShapeDtypeStruct((m, n), out_dtype)

    def body(*refs):
        acc = _dot(refs[0][...], refs[1][...], dims)
        extras, outs = refs[2:n_in], refs[len(operands):]
        if add is not None:
            acc = acc + extras[0][...]
        if relu2_of is not None:
            acc = acc * (2.0 * jnp.sqrt(extras[-1][...].astype(F32)))
        outs[0][...] = acc.astype(out_dtype)

    return pl.pallas_call(
        body, name=name, grid=(m // tm, n // tn), in_specs=in_specs, out_specs=o_spec, out_shape=out_shape,
        compiler_params=_params("parallel", "parallel"),
    )(*operands)


def _rowcall(fn, rows, consts, outs, accs, *, name, tr=256, after=None):
    s = rows[0].shape[0]
    tr = min(tr, s)
    assert s % tr == 0
    n_out = len(outs)
    accs = [(1, a) if isinstance(a, int) else a for a in accs]
    in_specs = [pl.BlockSpec((tr, r.shape[1]), lambda i: (i, 0)) for r in rows]
    in_specs += [pl.BlockSpec(c.shape, lambda i: (0, 0)) for c in consts]
    out_shape = [jax.ShapeDtypeStruct((s, w), dt) for w, dt in outs] + [jax.ShapeDtypeStruct(a, F32) for a in accs]
    out_specs = [pl.BlockSpec((tr, w), lambda i: (i, 0)) for w, _ in outs] + [pl.BlockSpec(a, lambda i: (0, 0)) for a in accs]
    n_in = len(rows) + len(consts)

    def body(*refs):
        res = fn(*[r[...] for r in refs[:n_in]])
        out_refs = refs[n_in + (after is not None):]
        for ref, val in zip(out_refs[:n_out], res[:n_out]):
            ref[...] = val.astype(ref.dtype)
        i = pl.program_id(0)
        for ref, val in zip(out_refs[n_out:], res[n_out:]):
            @pl.when(i == 0)
            def _(ref=ref, val=val):
                ref[...] = val

            @pl.when(i > 0)
            def _(ref=ref, val=val):
                ref[...] += val

    behind = [] if after is None else [after]
    return pl.pallas_call(
        body, name=name, grid=(s // tr,), in_specs=in_specs + [pl.BlockSpec(memory_space=pl.ANY)] * len(behind),
        out_specs=out_specs, out_shape=out_shape, compiler_params=_params("arbitrary" if accs else "parallel"),
    )(*rows, *consts, *behind)


def _rope_tables(seq):
    half = MLA_ROPE // 2
    inv_freq = ROPE_THETA ** (-jnp.arange(half, dtype=F32) / half)
    ang = jnp.arange(seq, dtype=F32)[:, None] * inv_freq[None, :]
    cos, sin, zero = jnp.cos(ang), jnp.sin(ang), jnp.zeros((seq, half), F32)
    t_c = jnp.concatenate([cos, cos, zero, zero], axis=1)
    t_s1 = jnp.concatenate([-sin, zero, zero, zero], axis=1)
    t_s2 = jnp.concatenate([zero, sin, zero, zero], axis=1)
    return t_c, t_s1, t_s2


def _rope(slab, t_c, t_s1, t_s2):
    return slab * t_c + pltpu.roll(slab, 96, 1) * t_s1 + pltpu.roll(slab, 32, 1) * t_s2


def _rope_t(d, t_c, t_s1, t_s2):
    return d * t_c + pltpu.roll(d * t_s1, 32, 1) + pltpu.roll(d * t_s2, 96, 1)


def _lower_bound(logits):
    l0, l1 = logits[0:1, :], logits[1:2, :]
    mx = jnp.maximum(l0, l1)
    e0, e1 = jnp.exp(l0 - mx), jnp.exp(l1 - mx)
    return e0 / (e0 + e1)


def _tri(n, lower):
    row = lax.broadcasted_iota(jnp.int32, (n, n), 0)
    col = lax.broadcasted_iota(jnp.int32, (n, n), 1)
    return (row >= col) if lower else (row <= col)


def _hgrn_fwd(zq, zf, zi, lb_logits, *, name):
    s, d = zq.shape
    h_n, c, hp = d // LANES, HGRN_CHUNK, HGRN_HEADS_PER_STEP
    nc = s // c

    def body(zq_ref, zf_ref, zi_ref, lb_ref, o_ref, st_ref, state_sc, b_sc):
        @pl.when(pl.program_id(1) == 0)
        def _():
            state_sc[...] = jnp.zeros_like(state_sc)

        lower = _tri(c, True).astype(F32)
        hs = range(hp)
        sls = [slice(hh * LANES, (hh + 1) * LANES) for hh in hs]
        lb = [_lower_bound(lb_ref[:, sl]) for sl in sls]
        zq_v = [zq_ref[:, sl] for sl in sls]
        q = [z * _sigmoid(z) for z in zq_v]
        f = [lb[hh] + (1.0 - lb[hh]) * _sigmoid(zf_ref[:, sls[hh]]) for hh in hs]
        g = [jnp.log(x) for x in f]
        k = [1.0 - x for x in f]
        v = [zi_ref[:, sl] for sl in sls]
        b = [_dot_f32(lower, x) for x in g]
        s0t = [state_sc[hh] for hh in hs]
        for hh in hs:
            st_ref[hh] = s0t[hh]
            b_sc[hh] = b[hh]
        o_inter = [_dot(q[hh] * jnp.exp(b[hh]), s0t[hh], _NT) for hh in hs]
        scores = [[] for _ in hs]
        for i in range(c // HGRN_SUB):
            lo = i * HGRN_SUB
            for hh in hs:
                ref = b_sc[hh, lo - 1:lo, :] if i > 0 else jnp.zeros((1, LANES), F32)
                qt = q[hh][lo:lo + HGRN_SUB, :] * jnp.exp(b[hh][lo:lo + HGRN_SUB, :] - ref)
                dec = jnp.exp(jnp.minimum(ref - b[hh], EXP_CLAMP))
                scores[hh].append(_dot(qt, k[hh] * dec, _NT))
        a = [jnp.where(_tri(c, True), jnp.concatenate(sc, axis=0), 0.0) for sc in scores]
        for hh in hs:
            o_ref[:, sls[hh]] = o_inter[hh] + _dot(a[hh], v[hh], _NN)
        bl = [b_sc[hh, c - 1:c, :] for hh in hs]
        for hh in hs:
            state_sc[hh] = s0t[hh] * jnp.exp(bl[hh]) + _dot(v[hh], k[hh] * jnp.exp(bl[hh] - b[hh]), _TN)

    tile = pl.BlockSpec((c, hp * LANES), lambda h, i: (i, h))
    return pl.pallas_call(
        body, name=name, grid=(h_n // hp, nc),
        in_specs=[tile, tile, tile, pl.BlockSpec((2, hp * LANES), lambda h, i: (0, h))],
        out_specs=[tile, pl.BlockSpec((hp, None, LANES, LANES), lambda h, i: (h, i, 0, 0))],
        out_shape=[jax.ShapeDtypeStruct((s, d), F32), jax.ShapeDtypeStruct((h_n, nc, LANES, LANES), F32)],
        scratch_shapes=[pltpu.VMEM((hp, LANES, LANES), F32), pltpu.VMEM((hp, c, LANES), F32)],
        compiler_params=_params("parallel", "arbitrary"),
    )(zq, zf, zi, lb_logits)


def _hgrn_bwd(zq, zf, zi, lb_logits, states, do, after, *, name):
    s, d = zq.shape
    h_n, c, hp = d // LANES, HGRN_CHUNK, HGRN_HEADS_PER_STEP
    nc = s // c

    def body(zq_ref, zf_ref, zi_ref, lb_ref, st_ref, do_ref, _, dzq_ref, dzf_ref, dzi_ref, dlb_ref, dstate_sc, b_sc):
        @pl.when(pl.program_id(1) == 0)
        def _():
            dstate_sc[...] = jnp.zeros_like(dstate_sc)
            dlb_ref[...] = jnp.zeros_like(dlb_ref)

        lower, upper = _tri(c, True), _tri(c, False).astype(F32)
        lower_f = lower.astype(F32)
        last_row = lax.broadcasted_iota(jnp.int32, (c, LANES), 0) == c - 1
        hs = range(hp)
        sls = [slice(hh * LANES, (hh + 1) * LANES) for hh in hs]
        lb = [_lower_bound(lb_ref[:, sl]) for sl in sls]
        zq_v = [zq_ref[:, sl] for sl in sls]
        sq = [_sigmoid(z) for z in zq_v]
        q = [zq_v[hh] * sq[hh] for hh in hs]
        sf = [_sigmoid(zf_ref[:, sl]) for sl in sls]
        f = [lb[hh] + (1.0 - lb[hh]) * sf[hh] for hh in hs]
        g = [jnp.log(x) for x in f]
        k = [1.0 - x for x in f]
        v = [zi_ref[:, sl] for sl in sls]
        d_o = [do_ref[:, sl] for sl in sls]
        b = [_dot_f32(lower_f, x) for x in g]
        s0t = [st_ref[hh] for hh in hs]
        ds1t = [dstate_sc[hh] for hh in hs]
        for hh in hs:
            b_sc[hh] = b[hh]
        bl = [b_sc[hh, c - 1:c, :] for hh in hs]
        eb = [jnp.exp(x) for x in b]
        ebl = [jnp.exp(x) for x in bl]
        dec_end = [jnp.exp(bl[hh] - b[hh]) for hh in hs]
        da = [jnp.where(lower, _dot(d_o[hh], v[hh], _NT), 0.0) for hh in hs]
        dq_inter = [_dot(d_o[hh], s0t[hh], _NN) * eb[hh] for hh in hs]
        dk_state = [_dot(v[hh], ds1t[hh], _NN) * dec_end[hh] for hh in hs]
        dv_state = [_dot(k[hh] * dec_end[hh], ds1t[hh], _NT) for hh in hs]
        for hh in hs:
            dstate_sc[hh] = ds1t[hh] * ebl[hh] + _dot(d_o[hh], q[hh] * eb[hh], _TN)
        dk = list(dk_state)
        scores, dq_blocks = [[] for _ in hs], [[] for _ in hs]
        for i in range(c // HGRN_SUB):
            lo = i * HGRN_SUB
            for hh in hs:
                ref = b_sc[hh, lo - 1:lo, :] if i > 0 else jnp.zeros((1, LANES), F32)
                grow = jnp.exp(b[hh][lo:lo + HGRN_SUB, :] - ref)
                qt = q[hh][lo:lo + HGRN_SUB, :] * grow
                dec = jnp.exp(jnp.minimum(ref - b[hh], EXP_CLAMP))
                kd = k[hh] * dec
                scores[hh].append(_dot(qt, kd, _NT))
                da_i = da[hh][lo:lo + HGRN_SUB, :]
                dq_blocks[hh].append(_dot_f32(da_i, kd, _NN) * grow)
                dk[hh] = dk[hh] + _dot_f32(da_i, qt, _TN) * dec
        a = [jnp.where(lower, jnp.concatenate(sc, axis=0), 0.0) for sc in scores]
        dv = [_dot(a[hh], d_o[hh], _TN) + dv_state[hh] for hh in hs]
        dq = [dq_inter[hh] + jnp.concatenate(dq_blocks[hh], axis=0) for hh in hs]
        db_last = [jnp.sum(k[hh] * dk_state[hh], axis=0, keepdims=True)
                   + ebl[hh] * jnp.sum(s0t[hh] * ds1t[hh], axis=0, keepdims=True) for hh in hs]
        db = [q[hh] * dq[hh] - k[hh] * dk[hh] + jnp.where(last_row, db_last[hh], 0.0) for hh in hs]
        dg = [_dot_f32(upper, x) for x in db]
        df = [dg[hh] / f[hh] - dk[hh] for hh in hs]
        for hh in hs:
            sl = sls[hh]
            dzf_ref[:, sl] = (df[hh] * (1.0 - lb[hh]) * sf[hh] * (1.0 - sf[hh])).astype(BF16)
            dlb_ref[:, sl] += jnp.sum(df[hh] * (1.0 - sf[hh]), axis=0, keepdims=True)
            dzq_ref[:, sl] = (dq[hh] * sq[hh] * (1.0 + zq_v[hh] * (1.0 - sq[hh]))).astype(BF16)
            dzi_ref[:, sl] = dv[hh].astype(BF16)

    tile = pl.BlockSpec((c, hp * LANES), lambda h, i: (nc - 1 - i, h))
    out = jax.ShapeDtypeStruct((s, d), BF16)
    return pl.pallas_call(
        body, name=name, grid=(h_n // hp, nc),
        in_specs=[tile, tile, tile, pl.BlockSpec((2, hp * LANES), lambda h, i: (0, h)),
                  pl.BlockSpec((hp, None, LANES, LANES), lambda h, i: (h, nc - 1 - i, 0, 0)), tile,
                  pl.BlockSpec(memory_space=pl.ANY)],
        out_specs=[tile, tile, tile, pl.BlockSpec((1, hp * LANES), lambda h, i: (0, h))],
        out_shape=[out, out, out, jax.ShapeDtypeStruct((1, d), F32)],
        scratch_shapes=[pltpu.VMEM((hp, LANES, LANES), F32), pltpu.VMEM((hp, c, LANES), F32)],
        compiler_params=_params("parallel", "arbitrary"),
    )(zq, zf, zi, lb_logits, states, do, after)


ATTN_SUB_ROWS = 256
LOG2E = 1.4426950408889634
LN2 = 0.6931471805599453
Q_PRESCALE = ATTN_SCALE * LOG2E


def _attn_tile(s):
    return min(1024, max(128, s // 2))


def _causal_pairs(n, q_major):
    pairs = [(i, j) for i in range(n) for j in range(i + 1)] if q_major else [(i, j) for j in range(n) for i in range(j, n)]
    return jnp.asarray([p[0] for p in pairs], jnp.int32), jnp.asarray([p[1] for p in pairs], jnp.int32)


def _sub_scores(qn_ref, qr_ref, k, r, sub, t, diagonal):
    q = jnp.concatenate([qn_ref[r:r + sub, :], qr_ref[r:r + sub, :]], axis=1)
    if not diagonal:
        return q, _dot(q, k, _NT)
    cols = r + sub
    keep = lax.broadcasted_iota(jnp.int32, (sub, cols), 1) <= r + lax.broadcasted_iota(jnp.int32, (sub, cols), 0)
    return q, jnp.where(keep, _dot(q, k[:cols], _NT), -jnp.inf)


def _attn_fwd(qn, qr, kn, kr, v, *, name):
    s, t = qn.shape[0], _attn_tile(qn.shape[0])
    sub = min(t, ATTN_SUB_ROWS)
    q_blk, k_blk = _causal_pairs(s // t, True)

    def body(qi_ref, kj_ref, qn_ref, qr_ref, kn_ref, kr_ref, v_ref, o_ref, lse_ref, m_sc, l_sc, acc_sc):
        p_id = pl.program_id(1)
        i, j = qi_ref[p_id], kj_ref[p_id]

        @pl.when(j == 0)
        def _():
            m_sc[...] = jnp.full_like(m_sc, -jnp.inf)
            l_sc[...] = jnp.zeros_like(l_sc)
            acc_sc[...] = jnp.zeros_like(acc_sc)

        def update(diagonal):
            k = jnp.concatenate([kn_ref[...], kr_ref[...]], axis=1)
            v = v_ref[...]
            starts = list(range(0, t, sub))
            scs = [_sub_scores(qn_ref, qr_ref, k, r, sub, t, diagonal)[1] for r in starts]
            ps, alphas = [], []
            for r, sc in zip(starts, scs):
                m_prev = m_sc[r:r + sub, :]
                m_new = jnp.maximum(m_prev, jnp.max(sc, axis=1, keepdims=True))
                alpha = jnp.exp2(m_prev - m_new)
                p = jnp.exp2(sc - m_new[:, :1])
                l_sc[r:r + sub, :] = alpha * l_sc[r:r + sub, :] + jnp.sum(p, axis=1, keepdims=True)
                m_sc[r:r + sub, :] = m_new
                ps.append(p)
                alphas.append(alpha)
            for r, p, alpha in zip(starts, ps, alphas):
                acc_sc[r:r + sub, :] = alpha * acc_sc[r:r + sub, :] + _dot(p, v[:p.shape[1]], _NN)

        @pl.when(j < i)
        def _():
            update(False)

        @pl.when(j == i)
        def _():
            update(True)
            o_ref[...] = (acc_sc[...] / l_sc[...]).astype(BF16)
            lse_ref[...] = m_sc[...] + jnp.log(l_sc[...]) * LOG2E

    q_spec = pl.BlockSpec((t, LANES), lambda h, p, qi, kj: (qi[p], h))
    k_spec = pl.BlockSpec((t, LANES), lambda h, p, qi, kj: (kj[p], h))
    kr_spec = pl.BlockSpec((t, LANES), lambda h, p, qi, kj: (kj[p], 0))
    stat = pltpu.VMEM((t, LANES), F32)
    return pl.pallas_call(
        body, name=name,
        grid_spec=pltpu.PrefetchScalarGridSpec(
            num_scalar_prefetch=2, grid=(MLA_HEADS, q_blk.shape[0]),
            in_specs=[q_spec, q_spec, k_spec, kr_spec, k_spec], out_specs=[q_spec, q_spec],
            scratch_shapes=[stat, stat, stat]),
        out_shape=[jax.ShapeDtypeStruct(qn.shape, BF16), jax.ShapeDtypeStruct(qn.shape, F32)],
        compiler_params=_params("parallel", "arbitrary"),
    )(q_blk, k_blk, qn, qr, kn, kr, v)


def _attn_bwd(qn, qr, kn, kr, v, do, lse, delta, *, name):
    s, t = qn.shape[0], _attn_tile(qn.shape[0])
    n, sub = s // t, min(t, ATTN_SUB_ROWS)
    q_blk, k_blk = _causal_pairs(n, False)

    def body(qi_ref, kj_ref, qn_ref, qr_ref, kn_ref, kr_ref, v_ref, do_ref, lse_ref, delta_ref,
             dqn_ref, dqr_ref, dkn_ref, dv_ref, dkr_ref, dk_sc, dv_sc):
        p_id = pl.program_id(1)
        i, j = qi_ref[p_id], kj_ref[p_id]

        @pl.when(p_id == 0)
        def _():
            dqn_ref[...] = jnp.zeros_like(dqn_ref)
            dqr_ref[...] = jnp.zeros_like(dqr_ref)

        @pl.when(i == j)
        def _():
            dk_sc[...] = jnp.zeros_like(dk_sc)
            dv_sc[...] = jnp.zeros_like(dv_sc)

        def accumulate(diagonal):
            k = jnp.concatenate([kn_ref[...], kr_ref[...]], axis=1)
            v = v_ref[...]
            starts = list(range(0, t, sub))
            qs, d_os, scs, dps = [], [], [], []
            for r in starts:
                q, sc = _sub_scores(qn_ref, qr_ref, k, r, sub, t, diagonal)
                d_o = do_ref[r:r + sub, :]
                qs.append(q)
                d_os.append(d_o)
                scs.append(sc)
                dps.append(_dot(d_o, v[:sc.shape[1]], _NT))
            ps, dss = [], []
            for r, sc, dp in zip(starts, scs, dps):
                p = jnp.exp2(sc - lse_ref[r:r + sub, :][:, :1])
                ps.append(p.astype(BF16))
                dss.append((p * (dp - delta_ref[r:r + sub, :][:, :1])).astype(BF16))
            for r, q, d_o, p, ds in zip(starts, qs, d_os, ps, dss):
                cols = p.shape[1]
                dv_sc[:cols, :] += _dot(p, d_o, _TN)
                dk_sc[:cols, :] += _dot(ds, q, _TN)
                dq = _dot(ds, k[:cols], _NN) * ATTN_SCALE
                rows = pl.ds(pl.multiple_of(i * t + r, sub), sub)
                dqn_ref[rows, :] += dq[:, :LANES]
                dqr_ref[rows, :] += dq[:, LANES:]

        @pl.when(j < i)
        def _():
            accumulate(False)

        @pl.when(j == i)
        def _():
            accumulate(True)

        @pl.when(i == n - 1)
        def _():
            dkn_ref[...] = (dk_sc[:, :LANES] * LN2).astype(BF16)
            dkr_ref[...] = dk_sc[:, LANES:] * LN2
            dv_ref[...] = dv_sc[...].astype(BF16)

    q_spec = pl.BlockSpec((t, LANES), lambda h, p, qi, kj: (qi[p], h))
    k_spec = pl.BlockSpec((t, LANES), lambda h, p, qi, kj: (kj[p], h))
    kr_spec = pl.BlockSpec((t, LANES), lambda h, p, qi, kj: (kj[p], 0))
    head_spec = pl.BlockSpec((s, LANES), lambda h, p, qi, kj: (0, h))
    f32_out, bf16_out = jax.ShapeDtypeStruct(qn.shape, F32), jax.ShapeDtypeStruct(qn.shape, BF16)
    return pl.pallas_call(
        body, name=name,
        grid_spec=pltpu.PrefetchScalarGridSpec(
            num_scalar_prefetch=2, grid=(MLA_HEADS, q_blk.shape[0]),
            in_specs=[q_spec, q_spec, k_spec, kr_spec, k_spec, q_spec, q_spec, q_spec],
            out_specs=[head_spec, head_spec, k_spec, k_spec, k_spec],
            scratch_shapes=[pltpu.VMEM((t, 2 * LANES), F32), pltpu.VMEM((t, LANES), F32)]),
        out_shape=[f32_out, f32_out, bf16_out, bf16_out, f32_out],
        compiler_params=_params("parallel", "arbitrary"),
    )(q_blk, k_blk, qn, qr, kn, kr, v, do, lse, delta)


def _exchange(arrs, *, scatter, name):
    n = len(arrs)
    out_shape = [jax.ShapeDtypeStruct(a.shape if scatter else (N_DEV, *a.shape), a.dtype) for a in arrs]

    def body(*refs):
        ins, outs = refs[:n], refs[n:2 * n]
        send_sems, recv_sems, local_sems = refs[2 * n:]
        x, y, c = lax.axis_index("x"), lax.axis_index("y"), lax.axis_index("c")
        me = 4 * x + 2 * y + c
        copies = []
        for k in range(n):
            local = pltpu.make_async_copy(ins[k].at[me] if scatter else ins[k], outs[k].at[me], local_sems.at[k])
            local.start()
            copies.append(local)
            for d in range(1, N_DEV):
                px, py, pc = (x + (d >> 2)) % 2, (y + ((d >> 1) & 1)) % 2, (c + (d & 1)) % 2
                peer = 4 * px + 2 * py + pc
                remote = pltpu.make_async_remote_copy(
                    src_ref=ins[k].at[peer] if scatter else ins[k], dst_ref=outs[k].at[me],
                    send_sem=send_sems.at[k, d - 1], recv_sem=recv_sems.at[k, d - 1],
                    device_id=(px, py, pc), device_id_type=pl.DeviceIdType.MESH)
                remote.start()
                copies.append(remote)
        for cp in copies:
            cp.wait()

    any_spec = pl.BlockSpec(memory_space=pl.ANY)
    return pl.pallas_call(
        body, name=name, in_specs=[any_spec] * n, out_specs=[any_spec] * n, out_shape=out_shape,
        scratch_shapes=[pltpu.SemaphoreType.DMA((n, N_DEV - 1)), pltpu.SemaphoreType.DMA((n, N_DEV - 1)),
                        pltpu.SemaphoreType.DMA((n,))],
    )(*arrs)


def _peers(x, y, c):
    out = []
    for d in range(1, N_DEV):
        px, py, pc = (x + (d >> 2)) % 2, (y + ((d >> 1) & 1)) % 2, (c + (d & 1)) % 2
        out.append(((px, py, pc), 4 * px + 2 * py + pc))
    return out


def _exchange_copies(ins, lands, send_sems, recv_sems, scatter):
    x, y, c = lax.axis_index("x"), lax.axis_index("y"), lax.axis_index("c")
    me = 4 * x + 2 * y + c
    local, remote = [], []
    for k in range(len(ins)):
        local.append(pltpu.make_async_copy(ins[k].at[me] if scatter else ins[k], lands[k].at[me],
                                           recv_sems.at[k * N_DEV + N_DEV - 1]))
        for d, (coords, peer) in enumerate(_peers(x, y, c)):
            remote.append(pltpu.make_async_remote_copy(
                src_ref=ins[k].at[peer] if scatter else ins[k], dst_ref=lands[k].at[me],
                send_sem=send_sems.at[k * N_DEV + d], recv_sem=recv_sems.at[k * N_DEV + d],
                device_id=coords, device_id_type=pl.DeviceIdType.MESH))
    return local, remote


def _exchange_start(arrs, *, scatter, name, after=None):
    n = len(arrs)
    hbm = pl.BlockSpec(memory_space=pltpu.HBM)
    sem = pl.BlockSpec(memory_space=pltpu.SEMAPHORE)
    lands = [lax.empty(a.shape if scatter else (N_DEV, *a.shape), a.dtype) for a in arrs]

    def body(*refs):
        ins, land_refs = refs[:n], refs[n:2 * n]
        first_out = 2 * n + (after is not None)
        send_sems, recv_sems, token = refs[first_out], refs[first_out + 1], refs[-1]
        local, remote = _exchange_copies(ins, land_refs, send_sems, recv_sems, scatter)
        for cp in local + remote:
            cp.start()
        token[...] = jnp.zeros_like(token)

    operands = [pltpu.with_memory_space_constraint(a, pltpu.HBM) for a in list(arrs) + lands]
    behind = [] if after is None else [after]
    res = pl.pallas_call(
        body, name=name,
        out_shape=(pltpu.SemaphoreType.DMA((n * N_DEV,)), pltpu.SemaphoreType.DMA((n * N_DEV,)),
                   *[pltpu.HBM(o.shape, o.dtype) for o in operands], jax.ShapeDtypeStruct((8, LANES), F32)),
        in_specs=[hbm] * (2 * n) + [pl.BlockSpec(memory_space=pl.ANY)] * len(behind),
        out_specs=(sem, sem, *[hbm] * (2 * n), pl.BlockSpec(memory_space=pltpu.VMEM)),
        input_output_aliases={i: 2 + i for i in range(2 * n)},
        compiler_params=pltpu.CompilerParams(has_side_effects=pltpu.SideEffectType.DATAFLOW_SIDE_EFFECTING),
    )(*operands, *behind)
    return (res[0], res[1], list(res[2:2 + n]), list(res[2 + n:2 + 2 * n]), scatter), res[-1]


def _exchange_wait(state, after, *, name):
    send_sems, recv_sems, ins, lands, scatter = state
    n = len(ins)
    hbm = pl.BlockSpec(memory_space=pltpu.HBM)
    sem = pl.BlockSpec(memory_space=pltpu.SEMAPHORE)

    def body(*refs):
        in_refs, land_refs = refs[:n], refs[n:2 * n]
        local, remote = _exchange_copies(in_refs, land_refs, refs[2 * n], refs[2 * n + 1], scatter)
        for cp in local:
            cp.wait()
        for cp in remote:
            cp.wait_send()
            cp.wait_recv()

    res = pl.pallas_call(
        body, name=name, out_shape=tuple(pltpu.HBM(o.shape, o.dtype) for o in ins + lands),
        in_specs=[hbm] * (2 * n) + [sem, sem, pl.BlockSpec(memory_space=pl.ANY)], out_specs=tuple([hbm] * (2 * n)),
        input_output_aliases={i: i for i in range(2 * n)},
        compiler_params=pltpu.CompilerParams(has_side_effects=pltpu.SideEffectType.DATAFLOW_SIDE_EFFECTING),
    )(*ins, *lands, send_sems, recv_sems, after)
    return list(res[n:])


def _adam(w, terms, m, v, *, name):
    r, c = w.shape
    n = terms.shape[0]
    tr = min(r, 128)
    assert r % tr == 0

    def body(w_ref, t_ref, m_ref, v_ref, g_out, d_out, m_out, v_out):
        g = t_ref[0].astype(F32)
        for s in range(1, n):
            g = g + t_ref[s].astype(F32)
        m1 = ADAM_B1 * m_ref[...] + (1.0 - ADAM_B1) * g
        v1 = ADAM_B2 * v_ref[...] + (1.0 - ADAM_B2) * jnp.square(g)
        m_hat = m1 / (1.0 - ADAM_B1 ** ADAM_STEP)
        v_hat = v1 / (1.0 - ADAM_B2 ** ADAM_STEP)
        g_out[...] = g
        d_out[...] = -ADAM_LR * (m_hat / (jnp.sqrt(v_hat) + ADAM_EPS) + ADAM_WD * w_ref[...])
        m_out[...] = m1
        v_out[...] = v1

    spec = pl.BlockSpec((tr, c), lambda i: (i, 0))
    out = jax.ShapeDtypeStruct((r, c), F32)
    return pl.pallas_call(
        body, name=name, grid=(r // tr,),
        in_specs=[spec, pl.BlockSpec((n, tr, c), lambda i: (0, i, 0)), spec, spec], out_specs=[spec] * 4,
        out_shape=[out] * 4, compiler_params=_params("parallel"),
    )(w, terms, m, v)


def _sum_terms(terms, *, name):
    n, _, p = terms.shape

    def body(t_ref, o_ref):
        acc = t_ref[0]
        for s in range(1, n):
            acc = acc + t_ref[s]
        o_ref[...] = acc

    return pl.pallas_call(body, name=name, out_shape=jax.ShapeDtypeStruct((1, p), F32))(terms)


def _lb_logits_grad(dlb, logits, *, name):
    def body(dlb_ref, l_ref, o_ref):
        lb = _lower_bound(l_ref[...])
        d0 = dlb_ref[...] * lb * (1.0 - lb)
        o_ref[...] = jnp.concatenate([d0, -d0], axis=0)

    return pl.pallas_call(body, name=name, out_shape=jax.ShapeDtypeStruct(logits.shape, F32))(dlb, logits)


def _silu_grad(z):
    sg = _sigmoid(z)
    return sg * (1.0 + z * (1.0 - sg))


def _head_norm_gate(o, zg, gn):
    outs = []
    for h in range(HGRN_HEADS):
        sl = slice(h * LANES, (h + 1) * LANES)
        zg_h = zg[:, sl]
        outs.append(_rms(o[:, sl], gn) * (zg_h * _sigmoid(zg_h)))
    return (jnp.concatenate(outs, axis=1),)


def _head_norm_gate_bwd(o, zg, dm, gn):
    do_parts, dzg_parts, dgn = [], [], jnp.zeros((1, LANES), F32)
    for h in range(HGRN_HEADS):
        sl = slice(h * LANES, (h + 1) * LANES)
        o_h, zg_h, dm_h = o[:, sl], zg[:, sl], dm[:, sl]
        gate = zg_h * _sigmoid(zg_h)
        do_h, dgn_h = _rms_bwd(o_h, gn, dm_h * gate)
        dgn = dgn + dgn_h
        do_parts.append(do_h)
        dzg_parts.append(dm_h * _rms(o_h, gn) * _silu_grad(zg_h))
    return jnp.concatenate(do_parts, axis=1), jnp.concatenate(dzg_parts, axis=1), dgn


def _rope_slabs(x, t_c, t_s1, t_s2, transpose):
    fn = _rope_t if transpose else _rope
    return jnp.concatenate(
        [fn(x[:, h * LANES:(h + 1) * LANES], t_c, t_s1, t_s2) for h in range(x.shape[1] // LANES)], axis=1)


def _loss_head(h, tgt, w):
    d = h.shape[1]
    r = lax.rsqrt(jnp.mean(h * h, axis=-1, keepdims=True) + EPS)
    xh = h * r
    err = xh * w - tgt
    loss = 0.5 * jnp.sum(jnp.mean(err * err, axis=-1, keepdims=True), axis=0, keepdims=True)
    dy = err / d
    dxh = dy * w
    dh = r * (dxh - xh * jnp.mean(dxh * xh, axis=-1, keepdims=True))
    return dh, dh, jnp.sum(dy * xh, axis=0, keepdims=True), jnp.broadcast_to(loss, (1, LANES))


def _mlp_fwd(h, norm, w_up, w_down, tag, loss_head=None):
    d = h.shape[1]

    def up(x, g, wu):
        x_n = _rms(x, g).astype(BF16)
        return x_n, jnp.square(jnp.maximum(_dot(x_n, wu, _NN), 0.0))

    xn, act = _rowcall(up, [h], [norm, w_up], [(d, BF16), (w_up.shape[1], BF16)], [], tr=512, name=f"{tag}_up")
    if loss_head is None:
        return _mm(act, w_down, mode="nn", add=h, name=f"{tag}_down"), (h, xn, act)
    tgt, final_norm = loss_head

    def down_and_loss(a, res, t, wd, g):
        return _loss_head(res + _dot(a, wd, _NN), t, g)

    return _rowcall(down_and_loss, [act, h, tgt], [w_down, final_norm], [(d, F32), (d, BF16)], [d, LANES],
                    name=f"{tag}_down_loss"), (h, xn, act)


def _mlp_bwd(dh_out, dh_out_bf, saved, norm, w_up, w_down, tag, after=None):
    h, xn, act = saved
    d = h.shape[1]
    du = _mm(dh_out_bf, w_down, mode="nt", relu2_of=act, out_dtype=BF16, after=after, name=f"{tag}_bwd_du")
    dw_down = _mm(act, dh_out_bf, mode="tn", name=f"{tag}_bwd_wdown")
    dw_up = _mm(xn, du, mode="tn", name=f"{tag}_bwd_wup")

    def up_norm_bwd(x, d_u, dres, g, wu):
        dx, dw = _rms_bwd(x, g, _dot(d_u, wu, _NT))
        return dx + dres, dx + dres, dw

    dh, dh_bf, dnorm = _rowcall(up_norm_bwd, [h, du, dh_out], [norm, w_up], [(d, F32), (d, BF16)], [d], tr=512,
                                name=f"{tag}_bwd_dxn")
    return dh, dh_bf, dnorm, dw_up, dw_down


def _row_major(g):
    return g.reshape(g.shape[0] * g.shape[1], g.shape[2])


def _col_major(g):
    return jnp.transpose(g, (1, 0, 2)).reshape(g.shape[1], g.shape[0] * g.shape[2])


def _col_terms(dw):
    k, n = dw.shape
    return jnp.transpose(dw.reshape(k, N_DEV, n // N_DEV), (1, 0, 2))


def _row_terms(dw):
    return dw.reshape(N_DEV, dw.shape[0] // N_DEV, dw.shape[1])


def kernel(x, hgrn_norm, hgrn_w_q, hgrn_w_f, hgrn_w_i, hgrn_w_g, hgrn_g_norm, hgrn_w_o, hgrn_lb_logits, mla_norm, mla_w_dq, mla_q_norm, mla_w_uq, mla_w_o, kv_in_norm, kv_w_dkv, kv_norm, kv_w_uk, kv_w_uv, mlp_norm, mlp_w_up, mlp_w_down, final_norm, loss_target, m_hgrn_norm, m_hgrn_w_q, m_hgrn_w_f, m_hgrn_w_i, m_hgrn_w_g, m_hgrn_g_norm, m_hgrn_w_o, m_hgrn_lb_logits, m_mla_norm, m_mla_w_dq, m_mla_q_norm, m_mla_w_uq, m_mla_w_o, m_kv_in_norm, m_kv_w_dkv, m_kv_norm, m_kv_w_uk, m_kv_w_uv, m_mlp_norm, m_mlp_w_up, m_mlp_w_down, m_final_norm, v_hgrn_norm, v_hgrn_w_q, v_hgrn_w_f, v_hgrn_w_i, v_hgrn_w_g, v_hgrn_g_norm, v_hgrn_w_o, v_hgrn_lb_logits, v_mla_norm, v_mla_w_dq, v_mla_q_norm, v_mla_w_uq, v_mla_w_o, v_kv_in_norm, v_kv_w_dkv, v_kv_norm, v_kv_w_uk, v_kv_w_uv, v_mlp_norm, v_mlp_w_up, v_mlp_w_down, v_final_norm):
    given = dict(locals())
    weight_names = ["hgrn_norm", "hgrn_w_q", "hgrn_w_f", "hgrn_w_i", "hgrn_w_g", "hgrn_g_norm", "hgrn_w_o",
                    "hgrn_lb_logits", "mla_norm", "mla_w_dq", "mla_q_norm", "mla_w_uq", "mla_w_o", "kv_in_norm",
                    "kv_w_dkv", "kv_norm", "kv_w_uk", "kv_w_uv", "mlp_norm", "mlp_w_up", "mlp_w_down", "final_norm"]
    me = 4 * lax.axis_index("x") + 2 * lax.axis_index("y") + lax.axis_index("c")
    xs, tgt = x[0], loss_target[0]
    seq, d_model = xs.shape
    n_heads, hd = MLA_HEADS, LANES

    big_local = {
        "hgrn_w_q": hgrn_w_q[0], "hgrn_w_f": hgrn_w_f[0], "hgrn_w_i": hgrn_w_i[0], "hgrn_w_g": hgrn_w_g[0],
        "hgrn_w_o": hgrn_w_o[0], "mla_w_dq": mla_w_dq[0], "mla_w_uq": mla_w_uq[0], "mla_w_o": mla_w_o[0],
        "kv_w_dkv": kv_w_dkv, "kv_w_uk": kv_w_uk, "kv_w_uv": kv_w_uv,
        "mlp_w_up0": mlp_w_up[0], "mlp_w_up1": mlp_w_up[1], "mlp_w_down0": mlp_w_down[0], "mlp_w_down1": mlp_w_down[1],
    }
    big_names = list(big_local)
    col_sharded = {"mla_w_uq", "kv_w_uk", "kv_w_uv", "mlp_w_up0", "mlp_w_up1"}
    vec_local = jnp.concatenate([hgrn_norm, hgrn_lb_logits], axis=0)
    first_names = ["hgrn_w_q", "hgrn_w_f", "hgrn_w_i", "hgrn_w_g"]
    later_names = {"mlp0": ["hgrn_w_o", "mlp_w_up0", "mlp_w_down0"],
                   "mla": ["kv_w_dkv", "kv_w_uk", "kv_w_uv", "mla_w_dq", "mla_w_uq", "mla_w_o"],
                   "mlp1": ["mlp_w_up1", "mlp_w_down1"]}

    def unshard(names, arrays):
        return {k: (_col_major(a) if k in col_sharded else _row_major(a)) for k, a in zip(names, arrays)}

    first_state, token = _exchange_start([big_local[k].astype(BF16) for k in first_names] + [vec_local], scatter=False,
                                         name="gather_first_start")
    gather_state = {}
    for tag, names in later_names.items():
        gather_state[tag], token = _exchange_start([big_local[k].astype(BF16) for k in names], scatter=False,
                                                   after=token, name=f"gather_{tag}_start")

    def gather_wait(tag, after):
        w.update(unshard(later_names[tag], _exchange_wait(gather_state[tag], after, name=f"gather_{tag}_wait")))

    gathered = _exchange_wait(first_state, token, name="gather_first_wait")
    w = unshard(first_names, gathered[:-1])
    vec_full = jnp.transpose(gathered[-1], (1, 0, 2)).reshape(3, d_model)
    hgrn_norm_full, lb_logits_full = vec_full[0:1], vec_full[1:3]
    t_c, t_s1, t_s2 = _rope_tables(seq)
    kv_lora = kv_w_uk.shape[0]

    def hgrn_proj(a, g, *weights):
        xn = _rms(a, g).astype(BF16)
        return (xn, *[_dot(xn, wt, _NN) for wt in weights])

    xn0, zq, zf, zi, zg = _rowcall(hgrn_proj, [xs], [hgrn_norm_full] + [w[k] for k in first_names],
                                   [(d_model, BF16)] + [(d_model, F32)] * 4, [], tr=512, name="hgrn_proj")
    o_rec, states = _hgrn_fwd(zq, zf, zi, lb_logits_full, name="hgrn_fwd")
    gather_wait("mlp0", o_rec)

    def gate_out(o, z, res, gn, wo):
        m = _head_norm_gate(o, z, gn)[0].astype(BF16)
        return m, res + _dot(m, wo, _NN)

    mixed, h1 = _rowcall(gate_out, [o_rec, zg, xs], [hgrn_g_norm, w["hgrn_w_o"]], [(d_model, BF16), (d_model, F32)], [],
                         name="hgrn_gate_out")
    h2, mlp0_saved = _mlp_fwd(h1, mlp_norm[0:1], w["mlp_w_up0"], w["mlp_w_down0"], "mlp0")
    gather_wait("mla", h2)
    w_uq3 = w["mla_w_uq"].reshape(-1, n_heads, MLA_NOPE + MLA_ROPE)
    w_uq_nope = w_uq3[:, :, :MLA_NOPE].reshape(-1, n_heads * hd)
    w_uq_rope = jnp.pad(w_uq3[:, :, MLA_NOPE:], ((0, 0), (0, 0), (0, hd - MLA_ROPE))).reshape(-1, n_heads * hd)
    w_dkv_pad = jnp.pad(w["kv_w_dkv"], ((0, 0), (0, kv_lora + hd - w["kv_w_dkv"].shape[1])))

    q_lora, qk_cols = w["mla_w_dq"].shape[1], n_heads * hd

    def mla_qkv(a, tc, ts1, ts2, g_kv_in, g_mla, g_q, g_kv, wdq, wn, wr, wdkv, wuk, wuv):
        h_n, x_n = _rms(a, g_kv_in).astype(BF16), _rms(a, g_mla).astype(BF16)
        cq = _dot(x_n, wdq, _NN)
        cq_n = _rms(cq, g_q).astype(BF16)
        q_nope = _dot(cq_n, wn, _NN) * Q_PRESCALE
        q_rope = _rope_slabs(_dot(cq_n, wr, _NN) * Q_PRESCALE, tc, ts1, ts2, False)
        c_all = _dot(h_n, wdkv, _NN)
        lat = _rms(c_all[:, :kv_lora], g_kv).astype(BF16)
        return (h_n, x_n, cq, cq_n, q_nope, q_rope, c_all, lat, _rope(c_all[:, kv_lora:], tc, ts1, ts2),
                _dot(lat, wuk, _NN), _dot(lat, wuv, _NN))

    hn, xn2, cq_pre, c_q, qn, qr, ckr, c_kv, kr, kn, vv = _rowcall(
        mla_qkv, [h2, t_c, t_s1, t_s2],
        [kv_in_norm[None, :], mla_norm, mla_q_norm, kv_norm[None, :], w["mla_w_dq"], w_uq_nope, w_uq_rope, w_dkv_pad,
         w["kv_w_uk"], w["kv_w_uv"]],
        [(d_model, BF16), (d_model, BF16), (q_lora, F32), (q_lora, BF16), (qk_cols, BF16), (qk_cols, BF16),
         (kv_lora + hd, F32), (kv_lora, BF16), (hd, BF16), (qk_cols, BF16), (qk_cols, BF16)], [], name="mla_qkv")
    o_att, lse = _attn_fwd(qn, qr, kn, kr, vv, name="attn_fwd")
    h3 = _mm(o_att, w["mla_w_o"], mode="nn", add=h2, name="attn_out")
    gather_wait("mlp1", h3)
    (dh4, dh4_bf, g_final_norm, loss_part), mlp1_saved = _mlp_fwd(
        h3, mlp_norm[1:2], w["mlp_w_up1"], w["mlp_w_down1"], "mlp1", loss_head=(tgt, final_norm[None, :]))

    g = {}
    groups = {"mlp1": ["mlp_w_up1", "mlp_w_down1"],
              "mla": ["mla_w_o", "mla_w_uq", "mla_w_dq", "kv_w_uk", "kv_w_uv", "kv_w_dkv"],
              "mlp0": ["mlp_w_up0", "mlp_w_down0"],
              "hgrn_out": ["hgrn_w_o", "hgrn_w_g"],
              "hgrn_in": ["hgrn_w_q", "hgrn_w_f", "hgrn_w_i"]}
    scatter_state = {}

    def scatter_start(tag, after=None):
        scatter_state[tag], tok = _exchange_start(
            [(_col_terms if k in col_sharded else _row_terms)(g[k]) for k in groups[tag]], scatter=True, after=after,
            name=f"scatter_{tag}_start")
        return tok

    dh3, dh3_bf, g_mlp_norm1, g["mlp_w_up1"], g["mlp_w_down1"] = _mlp_bwd(
        dh4, dh4_bf, mlp1_saved, mlp_norm[1:2], w["mlp_w_up1"], w["mlp_w_down1"], "mlp1")
    def attn_out_bwd(dres, o, wo):
        d_o = _dot(dres, wo, _NT).astype(BF16)
        prod = d_o.astype(F32) * o.astype(F32)
        return d_o, jnp.concatenate([jnp.broadcast_to(jnp.sum(prod[:, h * hd:(h + 1) * hd], axis=1, keepdims=True),
                                                      (prod.shape[0], hd)) for h in range(n_heads)], axis=1)

    d_oatt, delta = _rowcall(attn_out_bwd, [dh3_bf, o_att], [w["mla_w_o"]], [(qk_cols, BF16), (qk_cols, F32)], [],
                             after=scatter_start("mlp1"), name="attn_out_bwd_x")
    g["mla_w_o"] = _mm(o_att, dh3_bf, mode="tn", name="attn_out_bwd_w")
    dqn, dqr, dkn, dvv, dkr = _attn_bwd(qn, qr, kn, kr, vv, d_oatt, lse, delta, name="attn_bwd")

    def q_path_bwd(cq, cq_n, x_n, d_qn, d_qr, tc, ts1, ts2, g_q, wdq, wn, wr):
        d_qn, d_qr = d_qn.astype(BF16), _rope_slabs(d_qr, tc, ts1, ts2, True).astype(BF16)
        d_cq, d_gq = _rms_bwd(cq, g_q, _dot(d_qn, wn, _NT) + _dot(d_qr, wr, _NT))
        d_cq = d_cq.astype(BF16)
        return _dot(d_cq, wdq, _NT), d_gq, _dot(x_n, d_cq, _TN), _dot(cq_n, d_qn, _TN), _dot(cq_n, d_qr, _TN)

    dxn2, g_q_norm, g_dq, g_uq_nope, g_uq_rope = _rowcall(
        q_path_bwd, [cq_pre, c_q, xn2, dqn, dqr, t_c, t_s1, t_s2], [mla_q_norm, w["mla_w_dq"], w_uq_nope, w_uq_rope],
        [(d_model, F32)], [q_lora, (d_model, q_lora), (q_lora, qk_cols), (q_lora, qk_cols)], name="mla_q_bwd")
    g["mla_w_dq"] = g_dq.astype(GRAD_WIRE_DTYPE)
    g["mla_w_uq"] = jnp.concatenate([g_uq_nope.reshape(q_lora, n_heads, hd),
                                     g_uq_rope.reshape(q_lora, n_heads, hd)[:, :, :MLA_ROPE]],
                                    axis=2).reshape(q_lora, -1).astype(GRAD_WIRE_DTYPE)

    def kv_path_bwd(c_all, lat, h_n, d_kn, d_v, d_kr_heads, tc, ts1, ts2, a, d_xn2, dres,
                    g_kv, g_kv_in, g_mla, wdkv, wuk, wuv):
        d_lat, d_gkv = _rms_bwd(c_all[:, :kv_lora], g_kv, _dot(d_kn, wuk, _NT) + _dot(d_v, wuv, _NT))
        d_kr = d_kr_heads[:, :hd]
        for h in range(1, n_heads):
            d_kr = d_kr + d_kr_heads[:, h * hd:(h + 1) * hd]
        d_all = jnp.concatenate([d_lat, _rope_t(d_kr, tc, ts1, ts2)], axis=1).astype(BF16)
        dx1, d_gkv_in = _rms_bwd(a, g_kv_in, _dot(d_all, wdkv, _NT))
        dx2, d_gmla = _rms_bwd(a, g_mla, d_xn2)
        d_a = dx1 + dx2 + dres
        return (d_a, d_a, d_gkv, d_gkv_in, d_gmla, _dot(h_n, d_all, _TN), _dot(lat, d_kn, _TN), _dot(lat, d_v, _TN))

    dh2, dh2_bf, g_kv_norm, g_kv_in_norm, g_mla_norm, g_dkv, g_uk, g_uv = _rowcall(
        kv_path_bwd, [ckr, c_kv, hn, dkn, dvv, dkr, t_c, t_s1, t_s2, h2, dxn2, dh3],
        [kv_norm[None, :], kv_in_norm[None, :], mla_norm, w_dkv_pad, w["kv_w_uk"], w["kv_w_uv"]],
        [(d_model, F32), (d_model, BF16)],
        [kv_lora, d_model, d_model, (d_model, kv_lora + hd), (kv_lora, qk_cols), (kv_lora, qk_cols)], name="mla_kv_bwd")
    g["kv_w_dkv"] = g_dkv[:, :kv_w_dkv.shape[1]].astype(GRAD_WIRE_DTYPE)
    g["kv_w_uk"], g["kv_w_uv"] = g_uk.astype(GRAD_WIRE_DTYPE), g_uv.astype(GRAD_WIRE_DTYPE)
    dh1, dh1_bf, g_mlp_norm0, g["mlp_w_up0"], g["mlp_w_down0"] = _mlp_bwd(
        dh2, dh2_bf, mlp0_saved, mlp_norm[0:1], w["mlp_w_up0"], w["mlp_w_down0"], "mlp0", after=scatter_start("mla"))

    g["hgrn_w_o"] = _mm(mixed, dh1_bf, mode="tn", after=scatter_start("mlp0"), name="hgrn_out_bwd_w")
    do_rec, dzg, g_g_norm = _rowcall(
        lambda dres, o, z, wo, gn: _head_norm_gate_bwd(o, z, _dot(dres, wo, _NT), gn), [dh1_bf, o_rec, zg],
        [w["hgrn_w_o"], hgrn_g_norm], [(d_model, F32), (d_model, BF16)], [hd], name="hgrn_gate_out_bwd")
    g["hgrn_w_g"] = _mm(xn0, dzg, mode="tn", name="hgrn_w_g_bwd_w")
    dzq, dzf, dzi, g_lb = _hgrn_bwd(zq, zf, zi, lb_logits_full, states, do_rec, scatter_start("hgrn_out"),
                                    name="hgrn_bwd")
    for nm, dz in (("hgrn_w_q", dzq), ("hgrn_w_f", dzf), ("hgrn_w_i", dzi)):
        g[nm] = _mm(xn0, dz, mode="tn", name=f"{nm}_bwd_w")

    def hgrn_proj_bwd(a, dres, *rest):
        dzs, gw, weights = rest[:4], rest[4], rest[5:]
        dxn = _dot(dzs[0], weights[0], _NT)
        for dz, wt in zip(dzs[1:], weights[1:]):
            dxn = dxn + _dot(dz, wt, _NT)
        dx, dw = _rms_bwd(a, gw, dxn)
        return dx + dres, dw

    grad_x, g_hgrn_norm = _rowcall(hgrn_proj_bwd, [xs, dh1, dzq, dzf, dzi, dzg],
                                   [hgrn_norm_full] + [w[k] for k in first_names], [(d_model, F32)], [d_model],
                                   tr=512, name="hgrn_proj_bwd")

    small_parts = [g_hgrn_norm, g_lb, g_g_norm, g_mla_norm, g_q_norm, g_kv_in_norm, g_kv_norm, g_mlp_norm0,
                   g_mlp_norm1, g_final_norm, loss_part]
    small_sizes = [p.shape[1] for p in small_parts]
    small_terms = _exchange([jnp.concatenate(small_parts, axis=1)], scatter=False, name="gather_small")[0]
    small_sum = _sum_terms(small_terms, name="sum_small")
    last = scatter_start("hgrn_in", after=small_sum)
    offs = [0]
    for sz in small_sizes:
        offs.append(offs[-1] + sz)
    (s_hgrn_norm, s_lb, s_g_norm, s_mla_norm, s_q_norm, s_kv_in_norm, s_kv_norm, s_mlp_norm0, s_mlp_norm1, s_final_norm,
     s_loss) = [small_sum[:, a:b] for a, b in zip(offs[:-1], offs[1:])]
    shard = hgrn_norm.shape[1]
    g_lb_logits = _lb_logits_grad(lax.dynamic_slice_in_dim(s_lb, me * shard, shard, axis=1), hgrn_lb_logits,
                                  name="lb_logits_grad")
    loss = s_loss[0, 0]

    res = {}
    for tag, names in groups.items():
        for k, t in zip(names, _exchange_wait(scatter_state[tag], last, name=f"scatter_{tag}_wait")):
            if k.startswith("mlp_w_"):
                base, layer = k[:-1], int(k[-1])
                wk, mk, vk = given[base][layer], given["m_" + base][layer], given["v_" + base][layer]
            else:
                wk, mk, vk = given[k], given["m_" + k], given["v_" + k]
            shape = wk.shape
            wk, mk, vk = (a.reshape(shape[-2], shape[-1]) for a in (wk, mk, vk))
            upd = _adam(wk, t, mk, vk, name=f"adam_{k}")
            last = upd[0]
            res[k] = [o.reshape(shape) for o in upd]
    for base in ("mlp_w_up", "mlp_w_down"):
        res[base] = [jnp.stack([res[base + "0"][i], res[base + "1"][i]], axis=0) for i in range(4)]

    small_grads = {
        "hgrn_norm": lax.dynamic_slice_in_dim(s_hgrn_norm, me * shard, shard, axis=1),
        "hgrn_g_norm": s_g_norm, "hgrn_lb_logits": g_lb_logits, "mla_norm": s_mla_norm, "mla_q_norm": s_q_norm,
        "kv_in_norm": s_kv_in_norm, "kv_norm": s_kv_norm,
        "mlp_norm": jnp.concatenate([s_mlp_norm0, s_mlp_norm1], axis=0), "final_norm": s_final_norm,
    }
    small_names = list(small_grads)

    def flat(a):
        return a.reshape(1, -1)

    packed = [jnp.concatenate([flat(src[pre + k]) for k in small_names], axis=1)
              for src, pre in ((given, ""), (small_grads, ""), (given, "m_"), (given, "v_"))]
    small_out = _adam(packed[0], packed[1][None], packed[2], packed[3], name="adam_small")
    off = 0
    for k in small_names:
        size = given[k].size
        res[k] = [o[:, off:off + size].reshape(given[k].shape) for o in small_out]
        off += size

    outs = [loss, grad_x[None]]
    for i in range(4):
        outs += [res[k][i] for k in weight_names]
    return tuple(outs)
```

```python
import functools

import jax
import jax.numpy as jnp
from jax import lax
from jax.experimental import pallas as pl
from jax.experimental.pallas import tpu as pltpu

F32 = jnp.float32
BF16 = jnp.bfloat16

EPS = 1e-6
LANES = 128
N_DEV = 8
V7X_VMEM_LIMIT_BYTES = 56 << 20
MM_PIPELINE_BYTES = 30 << 20
MM_ROW_TILE = 512
GRAD_WIRE_DTYPE = BF16

HGRN_HEADS = 8
HGRN_CHUNK = 64
HGRN_SUB = 16
HGRN_HEADS_PER_STEP = 8
EXP_CLAMP = 80.0
MLA_HEADS = 16
MLA_NOPE = 128
MLA_ROPE = 64
ROPE_THETA = 10000.0
ATTN_SCALE = (MLA_NOPE + MLA_ROPE) ** -0.5

ADAM_LR = 0.001
ADAM_B1 = 0.9
ADAM_B2 = 0.999
ADAM_EPS = 1e-08
ADAM_WD = 0.01
ADAM_STEP = 10

_NN = ((1,), (0,))
_NT = ((1,), (1,))
_TN = ((0,), (0,))


def _params(*sem):
    return pltpu.CompilerParams(dimension_semantics=sem, vmem_limit_bytes=V7X_VMEM_LIMIT_BYTES)


def _dot(a, b, dims):
    return lax.dot_general(a.astype(BF16), b.astype(BF16), (dims, ((), ())), preferred_element_type=F32)


def _dot_f32(a, b, dims=_NN):
    return lax.dot_general(a, b, (dims, ((), ())), precision=lax.Precision.HIGH, preferred_element_type=F32)


def _sigmoid(x):
    return 1.0 / (1.0 + jnp.exp(-x))


def _rms(x, w):
    r = lax.rsqrt(jnp.mean(x * x, axis=-1, keepdims=True) + EPS)
    return x * r * w


def _rms_bwd(x, w, dy):
    r = lax.rsqrt(jnp.mean(x * x, axis=-1, keepdims=True) + EPS)
    xh = x * r
    dw = jnp.sum(dy * xh, axis=0, keepdims=True)
    dxh = dy * w
    dx = r * (dxh - xh * jnp.mean(dxh * xh, axis=-1, keepdims=True))
    return dx, dw


def _mm_tiles(m, n, k, a_bytes, b_bytes, out_tile_bytes):
    tm = min(m, MM_ROW_TILE)
    for tn in (n, 2048, 1024, 512, 256, LANES):
        if tn <= n and n % tn == 0:
            if 2 * (tm * k * a_bytes + k * tn * b_bytes + tm * tn * out_tile_bytes) <= MM_PIPELINE_BYTES:
                return tm, tn
    return tm, min(n, LANES)


def _mm(a, b, *, mode, name, out_dtype=None, add=None, relu2_of=None, after=None):
    if mode == "nn":
        (m, k), (k2, n) = a.shape, b.shape
    elif mode == "nt":
        (m, k), (n, k2) = a.shape, b.shape
    else:
        (k, m), (k2, n) = a.shape, b.shape
    assert k == k2, (name, a.shape, b.shape)
    if out_dtype is None:
        out_dtype = GRAD_WIRE_DTYPE if mode == "tn" else F32
    tile_bytes = sum(x.dtype.itemsize for x in (add, relu2_of) if x is not None) + jnp.dtype(out_dtype).itemsize
    tm, tn = _mm_tiles(m, n, k, a.dtype.itemsize, b.dtype.itemsize, tile_bytes)
    assert m % tm == 0 and n % tn == 0, (name, m, n)
    dims = {"nn": _NN, "nt": _NT, "tn": _TN}[mode]
    a_spec = pl.BlockSpec((k, tm), lambda i, j: (0, i)) if mode == "tn" else pl.BlockSpec((tm, k), lambda i, j: (i, 0))
    b_spec = pl.BlockSpec((tn, k), lambda i, j: (j, 0)) if mode == "nt" else pl.BlockSpec((k, tn), lambda i, j: (0, j))
    o_spec = pl.BlockSpec((tm, tn), lambda i, j: (i, j))
    operands, in_specs = [a, b], [a_spec, b_spec]
    for extra in (add, relu2_of):
        if extra is not None:
            assert extra.shape == (m, n), (name, extra.shape)
            operands.append(extra)
            in_specs.append(o_spec)
    n_in = len(operands)
    if after is not None:
        operands.append(after)
        in_specs.append(pl.BlockSpec(memory_space=pl.ANY))
    out_shape = jax.ShapeDtypeStruct((m, n), out_dtype)

    def body(*refs):
        acc = _dot(refs[0][...], refs[1][...], dims)
        extras, outs = refs[2:n_in], refs[len(operands):]
        if add is not None:
            acc = acc + extras[0][...]
        if relu2_of is not None:
            acc = acc * (2.0 * jnp.sqrt(extras[-1][...].astype(F32)))
        outs[0][...] = acc.astype(out_dtype)

    return pl.pallas_call(
        body, name=name, grid=(m // tm, n // tn), in_specs=in_specs, out_specs=o_spec, out_shape=out_shape,
        compiler_params=_params("parallel", "parallel"),
    )(*operands)


def _rowcall(fn, rows, consts, outs, accs, *, name, tr=256, after=None):
    s = rows[0].shape[0]
    tr = min(tr, s)
    assert s % tr == 0
    n_out = len(outs)
    accs = [(1, a) if isinstance(a, int) else a for a in accs]
    in_specs = [pl.BlockSpec((tr, r.shape[1]), lambda i: (i, 0)) for r in rows]
    in_specs += [pl.BlockSpec(c.shape, lambda i: (0, 0)) for c in consts]
    out_shape = [jax.ShapeDtypeStruct((s, w), dt) for w, dt in outs] + [jax.ShapeDtypeStruct(a, F32) for a in accs]
    out_specs = [pl.BlockSpec((tr, w), lambda i: (i, 0)) for w, _ in outs] + [pl.BlockSpec(a, lambda i: (0, 0)) for a in accs]
    n_in = len(rows) + len(consts)

    def body(*refs):
        res = fn(*[r[...] for r in refs[:n_in]])
        out_refs = refs[n_in + (after is not None):]
        for ref, val in zip(out_refs[:n_out], res[:n_out]):
            ref[...] = val.astype(ref.dtype)
        i = pl.program_id(0)
        for ref, val in zip(out_refs[n_out:], res[n_out:]):
            @pl.when(i == 0)
            def _(ref=ref, val=val):
                ref[...] = val

            @pl.when(i > 0)
            def _(ref=ref, val=val):
                ref[...] += val

    behind = [] if after is None else [after]
    return pl.pallas_call(
        body, name=name, grid=(s // tr,), in_specs=in_specs + [pl.BlockSpec(memory_space=pl.ANY)] * len(behind),
        out_specs=out_specs, out_shape=out_shape, compiler_params=_params("arbitrary" if accs else "parallel"),
    )(*rows, *consts, *behind)


def _rope_tables(seq):
    half = MLA_ROPE // 2
    inv_freq = ROPE_THETA ** (-jnp.arange(half, dtype=F32) / half)
    ang = jnp.arange(seq, dtype=F32)[:, None] * inv_freq[None, :]
    cos, sin, zero = jnp.cos(ang), jnp.sin(ang), jnp.zeros((seq, half), F32)
    t_c = jnp.concatenate([cos, cos, zero, zero], axis=1)
    t_s1 = jnp.concatenate([-sin, zero, zero, zero], axis=1)
    t_s2 = jnp.concatenate([zero, sin, zero, zero], axis=1)
    return t_c, t_s1, t_s2


def _rope(slab, t_c, t_s1, t_s2):
    return slab * t_c + pltpu.roll(slab, 96, 1) * t_s1 + pltpu.roll(slab, 32, 1) * t_s2


def _rope_t(d, t_c, t_s1, t_s2):
    return d * t_c + pltpu.roll(d * t_s1, 32, 1) + pltpu.roll(d * t_s2, 96, 1)


def _lower_bound(logits):
    l0, l1 = logits[0:1, :], logits[1:2, :]
    mx = jnp.maximum(l0, l1)
    e0, e1 = jnp.exp(l0 - mx), jnp.exp(l1 - mx)
    return e0 / (e0 + e1)


def _tri(n, lower):
    row = lax.broadcasted_iota(jnp.int32, (n, n), 0)
    col = lax.broadcasted_iota(jnp.int32, (n, n), 1)
    return (row >= col) if lower else (row <= col)


def _hgrn_fwd(zq, zf, zi, lb_logits, *, name):
    s, d = zq.shape
    h_n, c, hp = d // LANES, HGRN_CHUNK, HGRN_HEADS_PER_STEP
    nc = s // c

    def body(zq_ref, zf_ref, zi_ref, lb_ref, o_ref, st_ref, state_sc, b_sc):
        @pl.when(pl.program_id(1) == 0)
        def _():
            state_sc[...] = jnp.zeros_like(state_sc)

        lower = _tri(c, True).astype(F32)
        hs = range(hp)
        sls = [slice(hh * LANES, (hh + 1) * LANES) for hh in hs]
        lb = [_lower_bound(lb_ref[:, sl]) for sl in sls]
        zq_v = [zq_ref[:, sl] for sl in sls]
        q = [z * _sigmoid(z) for z in zq_v]
        f = [lb[hh] + (1.0 - lb[hh]) * _sigmoid(zf_ref[:, sls[hh]]) for hh in hs]
        g = [jnp.log(x) for x in f]
        k = [1.0 - x for x in f]
        v = [zi_ref[:, sl] for sl in sls]
        b = [_dot_f32(lower, x) for x in g]
        s0t = [state_sc[hh] for hh in hs]
        for hh in hs:
            st_ref[hh] = s0t[hh]
            b_sc[hh] = b[hh]
        o_inter = [_dot(q[hh] * jnp.exp(b[hh]), s0t[hh], _NT) for hh in hs]
        scores = [[] for _ in hs]
        for i in range(c // HGRN_SUB):
            lo = i * HGRN_SUB
            for hh in hs:
                ref = b_sc[hh, lo - 1:lo, :] if i > 0 else jnp.zeros((1, LANES), F32)
                qt = q[hh][lo:lo + HGRN_SUB, :] * jnp.exp(b[hh][lo:lo + HGRN_SUB, :] - ref)
                dec = jnp.exp(jnp.minimum(ref - b[hh], EXP_CLAMP))
                scores[hh].append(_dot(qt, k[hh] * dec, _NT))
        a = [jnp.where(_tri(c, True), jnp.concatenate(sc, axis=0), 0.0) for sc in scores]
        for hh in hs:
            o_ref[:, sls[hh]] = o_inter[hh] + _dot(a[hh], v[hh], _NN)
        bl = [b_sc[hh, c - 1:c, :] for hh in hs]
        for hh in hs:
            state_sc[hh] = s0t[hh] * jnp.exp(bl[hh]) + _dot(v[hh], k[hh] * jnp.exp(bl[hh] - b[hh]), _TN)

    tile = pl.BlockSpec((c, hp * LANES), lambda h, i: (i, h))
    return pl.pallas_call(
        body, name=name, grid=(h_n // hp, nc),
        in_specs=[tile, tile, tile, pl.BlockSpec((2, hp * LANES), lambda h, i: (0, h))],
        out_specs=[tile, pl.BlockSpec((hp, None, LANES, LANES), lambda h, i: (h, i, 0, 0))],
        out_shape=[jax.ShapeDtypeStruct((s, d), F32), jax.ShapeDtypeStruct((h_n, nc, LANES, LANES), F32)],
        scratch_shapes=[pltpu.VMEM((hp, LANES, LANES), F32), pltpu.VMEM((hp, c, LANES), F32)],
        compiler_params=_params("parallel", "arbitrary"),
    )(zq, zf, zi, lb_logits)


def _hgrn_bwd(zq, zf, zi, lb_logits, states, do, after, *, name):
    s, d = zq.shape
    h_n, c, hp = d // LANES, HGRN_CHUNK, HGRN_HEADS_PER_STEP
    nc = s // c

    def body(zq_ref, zf_ref, zi_ref, lb_ref, st_ref, do_ref, _, dzq_ref, dzf_ref, dzi_ref, dlb_ref, dstate_sc, b_sc):
        @pl.when(pl.program_id(1) == 0)
        def _():
            dstate_sc[...] = jnp.zeros_like(dstate_sc)
            dlb_ref[...] = jnp.zeros_like(dlb_ref)

        lower, upper = _tri(c, True), _tri(c, False).astype(F32)
        lower_f = lower.astype(F32)
        last_row = lax.broadcasted_iota(jnp.int32, (c, LANES), 0) == c - 1
        hs = range(hp)
        sls = [slice(hh * LANES, (hh + 1) * LANES) for hh in hs]
        lb = [_lower_bound(lb_ref[:, sl]) for sl in sls]
        zq_v = [zq_ref[:, sl] for sl in sls]
        sq = [_sigmoid(z) for z in zq_v]
        q = [zq_v[hh] * sq[hh] for hh in hs]
        sf = [_sigmoid(zf_ref[:, sl]) for sl in sls]
        f = [lb[hh] + (1.0 - lb[hh]) * sf[hh] for hh in hs]
        g = [jnp.log(x) for x in f]
        k = [1.0 - x for x in f]
        v = [zi_ref[:, sl] for sl in sls]
        d_o = [do_ref[:, sl] for sl in sls]
        b = [_dot_f32(lower_f, x) for x in g]
        s0t = [st_ref[hh] for hh in hs]
        ds1t = [dstate_sc[hh] for hh in hs]
        for hh in hs:
            b_sc[hh] = b[hh]
        bl = [b_sc[hh, c - 1:c, :] for hh in hs]
        eb = [jnp.exp(x) for x in b]
        ebl = [jnp.exp(x) for x in bl]
        dec_end = [jnp.exp(bl[hh] - b[hh]) for hh in hs]
        da = [jnp.where(lower, _dot(d_o[hh], v[hh], _NT), 0.0) for hh in hs]
        dq_inter = [_dot(d_o[hh], s0t[hh], _NN) * eb[hh] for hh in hs]
        dk_state = [_dot(v[hh], ds1t[hh], _NN) * dec_end[hh] for hh in hs]
        dv_state = [_dot(k[hh] * dec_end[hh], ds1t[hh], _NT) for hh in hs]
        for hh in hs:
            dstate_sc[hh] = ds1t[hh] * ebl[hh] + _dot(d_o[hh], q[hh] * eb[hh], _TN)
        dk = list(dk_state)
        scores, dq_blocks = [[] for _ in hs], [[] for _ in hs]
        for i in range(c // HGRN_SUB):
            lo = i * HGRN_SUB
            for hh in hs:
                ref = b_sc[hh, lo - 1:lo, :] if i > 0 else jnp.zeros((1, LANES), F32)
                grow = jnp.exp(b[hh][lo:lo + HGRN_SUB, :] - ref)
                qt = q[hh][lo:lo + HGRN_SUB, :] * grow
                dec = jnp.exp(jnp.minimum(ref - b[hh], EXP_CLAMP))
                kd = k[hh] * dec
                scores[hh].append(_dot(qt, kd, _NT))
                da_i = da[hh][lo:lo + HGRN_SUB, :]
                dq_blocks[hh].append(_dot_f32(da_i, kd, _NN) * grow)
                dk[hh] = dk[hh] + _dot_f32(da_i, qt, _TN) * dec
        a = [jnp.where(lower, jnp.concatenate(sc, axis=0), 0.0) for sc in scores]
        dv = [_dot(a[hh], d_o[hh], _TN) + dv_state[hh] for hh in hs]
        dq = [dq_inter[hh] + jnp.concatenate(dq_blocks[hh], axis=0) for hh in hs]
        db_last = [jnp.sum(k[hh] * dk_state[hh], axis=0, keepdims=True)
                   + ebl[hh] * jnp.sum(s0t[hh] * ds1t[hh], axis=0, keepdims=True) for hh in hs]
        db = [q[hh] * dq[hh] - k[hh] * dk[hh] + jnp.where(last_row, db_last[hh], 0.0) for hh in hs]
        dg = [_dot_f32(upper, x) for x in db]
        df = [dg[hh] / f[hh] - dk[hh] for hh in hs]
        for hh in hs:
            sl = sls[hh]
            dzf_ref[:, sl] = (df[hh] * (1.0 - lb[hh]) * sf[hh] * (1.0 - sf[hh])).astype(BF16)
            dlb_ref[:, sl] += jnp.sum(df[hh] * (1.0 - sf[hh]), axis=0, keepdims=True)
            dzq_ref[:, sl] = (dq[hh] * sq[hh] * (1.0 + zq_v[hh] * (1.0 - sq[hh]))).astype(BF16)
            dzi_ref[:, sl] = dv[hh].astype(BF16)

    tile = pl.BlockSpec((c, hp * LANES), lambda h, i: (nc - 1 - i, h))
    out = jax.ShapeDtypeStruct((s, d), BF16)
    return pl.pallas_call(
        body, name=name, grid=(h_n // hp, nc),
        in_specs=[tile, tile, tile, pl.BlockSpec((2, hp * LANES), lambda h, i: (0, h)),
                  pl.BlockSpec((hp, None, LANES, LANES), lambda h, i: (h, nc - 1 - i, 0, 0)), tile,
                  pl.BlockSpec(memory_space=pl.ANY)],
        out_specs=[tile, tile, tile, pl.BlockSpec((1, hp * LANES), lambda h, i: (0, h))],
        out_shape=[out, out, out, jax.ShapeDtypeStruct((1, d), F32)],
        scratch_shapes=[pltpu.VMEM((hp, LANES, LANES), F32), pltpu.VMEM((hp, c, LANES), F32)],
        compiler_params=_params("parallel", "arbitrary"),
    )(zq, zf, zi, lb_logits, states, do, after)


ATTN_SUB_ROWS = 256
LOG2E = 1.4426950408889634
LN2 = 0.6931471805599453
Q_PRESCALE = ATTN_SCALE * LOG2E


def _attn_tile(s):
    return min(1024, max(128, s // 2))


def _causal_pairs(n, q_major):
    pairs = [(i, j) for i in range(n) for j in range(i + 1)] if q_major else [(i, j) for j in range(n) for i in range(j, n)]
    return jnp.asarray([p[0] for p in pairs], jnp.int32), jnp.asarray([p[1] for p in pairs], jnp.int32)


def _sub_scores(qn_ref, qr_ref, k, r, sub, t, diagonal):
    q = jnp.concatenate([qn_ref[r:r + sub, :], qr_ref[r:r + sub, :]], axis=1)
    if not diagonal:
        return q, _dot(q, k, _NT)
    cols = r + sub
    keep = lax.broadcasted_iota(jnp.int32, (sub, cols), 1) <= r + lax.broadcasted_iota(jnp.int32, (sub, cols), 0)
    return q, jnp.where(keep, _dot(q, k[:cols], _NT), -jnp.inf)


def _attn_fwd(qn, qr, kn, kr, v, *, name):
    s, t = qn.shape[0], _attn_tile(qn.shape[0])
    sub = min(t, ATTN_SUB_ROWS)
    q_blk, k_blk = _causal_pairs(s // t, True)

    def body(qi_ref, kj_ref, qn_ref, qr_ref, kn_ref, kr_ref, v_ref, o_ref, lse_ref, m_sc, l_sc, acc_sc):
        p_id = pl.program_id(1)
        i, j = qi_ref[p_id], kj_ref[p_id]

        @pl.when(j == 0)
        def _():
            m_sc[...] = jnp.full_like(m_sc, -jnp.inf)
            l_sc[...] = jnp.zeros_like(l_sc)
            acc_sc[...] = jnp.zeros_like(acc_sc)

        def update(diagonal):
            k = jnp.concatenate([kn_ref[...], kr_ref[...]], axis=1)
            v = v_ref[...]
            starts = list(range(0, t, sub))
            scs = [_sub_scores(qn_ref, qr_ref, k, r, sub, t, diagonal)[1] for r in starts]
            ps, alphas = [], []
            for r, sc in zip(starts, scs):
                m_prev = m_sc[r:r + sub, :]
                m_new = jnp.maximum(m_prev, jnp.max(sc, axis=1, keepdims=True))
                alpha = jnp.exp2(m_prev - m_new)
                p = jnp.exp2(sc - m_new[:, :1])
                l_sc[r:r + sub, :] = alpha * l_sc[r:r + sub, :] + jnp.sum(p, axis=1, keepdims=True)
                m_sc[r:r + sub, :] = m_new
                ps.append(p)
                alphas.append(alpha)
            for r, p, alpha in zip(starts, ps, alphas):
                acc_sc[r:r + sub, :] = alpha * acc_sc[r:r + sub, :] + _dot(p, v[:p.shape[1]], _NN)

        @pl.when(j < i)
        def _():
            update(False)

        @pl.when(j == i)
        def _():
            update(True)
            o_ref[...] = (acc_sc[...] / l_sc[...]).astype(BF16)
            lse_ref[...] = m_sc[...] + jnp.log(l_sc[...]) * LOG2E

    q_spec = pl.BlockSpec((t, LANES), lambda h, p, qi, kj: (qi[p], h))
    k_spec = pl.BlockSpec((t, LANES), lambda h, p, qi, kj: (kj[p], h))
    kr_spec = pl.BlockSpec((t, LANES), lambda h, p, qi, kj: (kj[p], 0))
    stat = pltpu.VMEM((t, LANES), F32)
    return pl.pallas_call(
        body, name=name,
        grid_spec=pltpu.PrefetchScalarGridSpec(
            num_scalar_prefetch=2, grid=(MLA_HEADS, q_blk.shape[0]),
            in_specs=[q_spec, q_spec, k_spec, kr_spec, k_spec], out_specs=[q_spec, q_spec],
            scratch_shapes=[stat, stat, stat]),
        out_shape=[jax.ShapeDtypeStruct(qn.shape, BF16), jax.ShapeDtypeStruct(qn.shape, F32)],
        compiler_params=_params("parallel", "arbitrary"),
    )(q_blk, k_blk, qn, qr, kn, kr, v)


def _attn_bwd(qn, qr, kn, kr, v, do, lse, delta, *, name):
    s, t = qn.shape[0], _attn_tile(qn.shape[0])
    n, sub = s // t, min(t, ATTN_SUB_ROWS)
    q_blk, k_blk = _causal_pairs(n, False)

    def body(qi_ref, kj_ref, qn_ref, qr_ref, kn_ref, kr_ref, v_ref, do_ref, lse_ref, delta_ref,
             dqn_ref, dqr_ref, dkn_ref, dv_ref, dkr_ref, dk_sc, dv_sc):
        p_id = pl.program_id(1)
        i, j = qi_ref[p_id], kj_ref[p_id]

        @pl.when(p_id == 0)
        def _():
            dqn_ref[...] = jnp.zeros_like(dqn_ref)
            dqr_ref[...] = jnp.zeros_like(dqr_ref)

        @pl.when(i == j)
        def _():
            dk_sc[...] = jnp.zeros_like(dk_sc)
            dv_sc[...] = jnp.zeros_like(dv_sc)

        def accumulate(diagonal):
            k = jnp.concatenate([kn_ref[...], kr_ref[...]], axis=1)
            v = v_ref[...]
            starts = list(range(0, t, sub))
            qs, d_os, scs, dps = [], [], [], []
            for r in starts:
                q, sc = _sub_scores(qn_ref, qr_ref, k, r, sub, t, diagonal)
                d_o = do_ref[r:r + sub, :]
                qs.append(q)
                d_os.append(d_o)
                scs.append(sc)
                dps.append(_dot(d_o, v[:sc.shape[1]], _NT))
            ps, dss = [], []
            for r, sc, dp in zip(starts, scs, dps):
                p = jnp.exp2(sc - lse_ref[r:r + sub, :][:, :1])
                ps.append(p.astype(BF16))
                dss.append((p * (dp - delta_ref[r:r + sub, :][:, :1])).astype(BF16))
            for r, q, d_o, p, ds in zip(starts, qs, d_os, ps, dss):
                cols = p.shape[1]
                dv_sc[:cols, :] += _dot(p, d_o, _TN)
                dk_sc[:cols, :] += _dot(ds, q, _TN)
                dq = _dot(ds, k[:cols], _NN) * ATTN_SCALE
                rows = pl.ds(pl.multiple_of(i * t + r, sub), sub)
                dqn_ref[rows, :] += dq[:, :LANES]
                dqr_ref[rows, :] += dq[:, LANES:]

        @pl.when(j < i)
        def _():
            accumulate(False)

        @pl.when(j == i)
        def _():
            accumulate(True)

        @pl.when(i == n - 1)
        def _():
            dkn_ref[...] = (dk_sc[:, :LANES] * LN2).astype(BF16)
            dkr_ref[...] = dk_sc[:, LANES:] * LN2
            dv_ref[...] = dv_sc[...].astype(BF16)

    q_spec = pl.BlockSpec((t, LANES), lambda h, p, qi, kj: (qi[p], h))
    k_spec = pl.BlockSpec((t, LANES), lambda h, p, qi, kj: (kj[p], h))
    kr_spec = pl.BlockSpec((t, LANES), lambda h, p, qi, kj: (kj[p], 0))
    head_spec = pl.BlockSpec((s, LANES), lambda h, p, qi, kj: (0, h))
    f32_out, bf16_out = jax.ShapeDtypeStruct(qn.shape, F32), jax.ShapeDtypeStruct(qn.shape, BF16)
    return pl.pallas_call(
        body, name=name,
        grid_spec=pltpu.PrefetchScalarGridSpec(
            num_scalar_prefetch=2, grid=(MLA_HEADS, q_blk.shape[0]),
            in_specs=[q_spec, q_spec, k_spec, kr_spec, k_spec, q_spec, q_spec, q_spec],
            out_specs=[head_spec, head_spec, k_spec, k_spec, k_spec],
            scratch_shapes=[pltpu.VMEM((t, 2 * LANES), F32), pltpu.VMEM((t, LANES), F32)]),
        out_shape=[f32_out, f32_out, bf16_out, bf16_out, f32_out],
        compiler_params=_params("parallel", "arbitrary"),
    )(q_blk, k_blk, qn, qr, kn, kr, v, do, lse, delta)


def _exchange(arrs, *, scatter, name):
    n = len(arrs)
    out_shape = [jax.ShapeDtypeStruct(a.shape if scatter else (N_DEV, *a.shape), a.dtype) for a in arrs]

    def body(*refs):
        ins, outs = refs[:n], refs[n:2 * n]
        send_sems, recv_sems, local_sems = refs[2 * n:]
        x, y, c = lax.axis_index("x"), lax.axis_index("y"), lax.axis_index("c")
        me = 4 * x + 2 * y + c
        copies = []
        for k in range(n):
            local = pltpu.make_async_copy(ins[k].at[me] if scatter else ins[k], outs[k].at[me], local_sems.at[k])
            local.start()
            copies.append(local)
            for d in range(1, N_DEV):
                px, py, pc = (x + (d >> 2)) % 2, (y + ((d >> 1) & 1)) % 2, (c + (d & 1)) % 2
                peer = 4 * px + 2 * py + pc
                remote = pltpu.make_async_remote_copy(
                    src_ref=ins[k].at[peer] if scatter else ins[k], dst_ref=outs[k].at[me],
                    send_sem=send_sems.at[k, d - 1], recv_sem=recv_sems.at[k, d - 1],
                    device_id=(px, py, pc), device_id_type=pl.DeviceIdType.MESH)
                remote.start()
                copies.append(remote)
        for cp in copies:
            cp.wait()

    any_spec = pl.BlockSpec(memory_space=pl.ANY)
    return pl.pallas_call(
        body, name=name, in_specs=[any_spec] * n, out_specs=[any_spec] * n, out_shape=out_shape,
        scratch_shapes=[pltpu.SemaphoreType.DMA((n, N_DEV - 1)), pltpu.SemaphoreType.DMA((n, N_DEV - 1)),
                        pltpu.SemaphoreType.DMA((n,))],
    )(*arrs)


def _peers(x, y, c):
    out = []
    for d in range(1, N_DEV):
        px, py, pc = (x + (d >> 2)) % 2, (y + ((d >> 1) & 1)) % 2, (c + (d & 1)) % 2
        out.append(((px, py, pc), 4 * px + 2 * py + pc))
    return out


def _exchange_copies(ins, lands, send_sems, recv_sems, scatter):
    x, y, c = lax.axis_index("x"), lax.axis_index("y"), lax.axis_index("c")
    me = 4 * x + 2 * y + c
    local, remote = [], []
    for k in range(len(ins)):
        local.append(pltpu.make_async_copy(ins[k].at[me] if scatter else ins[k], lands[k].at[me],
                                           recv_sems.at[k * N_DEV + N_DEV - 1]))
        for d, (coords, peer) in enumerate(_peers(x, y, c)):
            remote.append(pltpu.make_async_remote_copy(
                src_ref=ins[k].at[peer] if scatter else ins[k], dst_ref=lands[k].at[me],
                send_sem=send_sems.at[k * N_DEV + d], recv_sem=recv_sems.at[k * N_DEV + d],
                device_id=coords, device_id_type=pl.DeviceIdType.MESH))
    return local, remote


def _exchange_start(arrs, *, scatter, name, after=None):
    n = len(arrs)
    hbm = pl.BlockSpec(memory_space=pltpu.HBM)
    sem = pl.BlockSpec(memory_space=pltpu.SEMAPHORE)
    lands = [lax.empty(a.shape if scatter else (N_DEV, *a.shape), a.dtype) for a in arrs]

    def body(*refs):
        ins, land_refs = refs[:n], refs[n:2 * n]
        first_out = 2 * n + (after is not None)
        send_sems, recv_sems, token = refs[first_out], refs[first_out + 1], refs[-1]
        local, remote = _exchange_copies(ins, land_refs, send_sems, recv_sems, scatter)
        for cp in local + remote:
            cp.start()
        token[...] = jnp.zeros_like(token)

    operands = [pltpu.with_memory_space_constraint(a, pltpu.HBM) for a in list(arrs) + lands]
    behind = [] if after is None else [after]
    res = pl.pallas_call(
        body, name=name,
        out_shape=(pltpu.SemaphoreType.DMA((n * N_DEV,)), pltpu.SemaphoreType.DMA((n * N_DEV,)),
                   *[pltpu.HBM(o.shape, o.dtype) for o in operands], jax.ShapeDtypeStruct((8, LANES), F32)),
        in_specs=[hbm] * (2 * n) + [pl.BlockSpec(memory_space=pl.ANY)] * len(behind),
        out_specs=(sem, sem, *[hbm] * (2 * n), pl.BlockSpec(memory_space=pltpu.VMEM)),
        input_output_aliases={i: 2 + i for i in range(2 * n)},
        compiler_params=pltpu.CompilerParams(has_side_effects=pltpu.SideEffectType.DATAFLOW_SIDE_EFFECTING),
    )(*operands, *behind)
    return (res[0], res[1], list(res[2:2 + n]), list(res[2 + n:2 + 2 * n]), scatter), res[-1]


def _exchange_wait(state, after, *, name):
    send_sems, recv_sems, ins, lands, scatter = state
    n = len(ins)
    hbm = pl.BlockSpec(memory_space=pltpu.HBM)
    sem = pl.BlockSpec(memory_space=pltpu.SEMAPHORE)

    def body(*refs):
        in_refs, land_refs = refs[:n], refs[n:2 * n]
        local, remote = _exchange_copies(in_refs, land_refs, refs[2 * n], refs[2 * n + 1], scatter)
        for cp in local:
            cp.wait()
        for cp in remote:
            cp.wait_send()
            cp.wait_recv()

    res = pl.pallas_call(
        body, name=name, out_shape=tuple(pltpu.HBM(o.shape, o.dtype) for o in ins + lands),
        in_specs=[hbm] * (2 * n) + [sem, sem, pl.BlockSpec(memory_space=pl.ANY)], out_specs=tuple([hbm] * (2 * n)),
        input_output_aliases={i: i for i in range(2 * n)},
        compiler_params=pltpu.CompilerParams(has_side_effects=pltpu.SideEffectType.DATAFLOW_SIDE_EFFECTING),
    )(*ins, *lands, send_sems, recv_sems, after)
    return list(res[n:])


def _adam(w, terms, m, v, *, name):
    r, c = w.shape
    n = terms.shape[0]
    tr = min(r, 128)
    assert r % tr == 0

    def body(w_ref, t_ref, m_ref, v_ref, g_out, d_out, m_out, v_out):
        g = t_ref[0].astype(F32)
        for s in range(1, n):
            g = g + t_ref[s].astype(F32)
        m1 = ADAM_B1 * m_ref[...] + (1.0 - ADAM_B1) * g
        v1 = ADAM_B2 * v_ref[...] + (1.0 - ADAM_B2) * jnp.square(g)
        m_hat = m1 / (1.0 - ADAM_B1 ** ADAM_STEP)
        v_hat = v1 / (1.0 - ADAM_B2 ** ADAM_STEP)
        g_out[...] = g
        d_out[...] = -ADAM_LR * (m_hat / (jnp.sqrt(v_hat) + ADAM_EPS) + ADAM_WD * w_ref[...])
        m_out[...] = m1
        v_out[...] = v1

    spec = pl.BlockSpec((tr, c), lambda i: (i, 0))
    out = jax.ShapeDtypeStruct((r, c), F32)
    return pl.pallas_call(
        body, name=name, grid=(r // tr,),
        in_specs=[spec, pl.BlockSpec((n, tr, c), lambda i: (0, i, 0)), spec, spec], out_specs=[spec] * 4,
        out_shape=[out] * 4, compiler_params=_params("parallel"),
    )(w, terms, m, v)


def _sum_terms(terms, *, name):
    n, _, p = terms.shape

    def body(t_ref, o_ref):
        acc = t_ref[0]
        for s in range(1, n):
            acc = acc + t_ref[s]
        o_ref[...] = acc

    return pl.pallas_call(body, name=name, out_shape=jax.ShapeDtypeStruct((1, p), F32))(terms)


def _lb_logits_grad(dlb, logits, *, name):
    def body(dlb_ref, l_ref, o_ref):
        lb = _lower_bound(l_ref[...])
        d0 = dlb_ref[...] * lb * (1.0 - lb)
        o_ref[...] = jnp.concatenate([d0, -d0], axis=0)

    return pl.pallas_call(body, name=name, out_shape=jax.ShapeDtypeStruct(logits.shape, F32))(dlb, logits)


def _silu_grad(z):
    sg = _sigmoid(z)
    return sg * (1.0 + z * (1.0 - sg))


def _head_norm_gate(o, zg, gn):
    outs = []
    for h in range(HGRN_HEADS):
        sl = slice(h * LANES, (h + 1) * LANES)
        zg_h = zg[:, sl]
        outs.append(_rms(o[:, sl], gn) * (zg_h * _sigmoid(zg_h)))
    return (jnp.concatenate(outs, axis=1),)


def _head_norm_gate_bwd(o, zg, dm, gn):
    do_parts, dzg_parts, dgn = [], [], jnp.zeros((1, LANES), F32)
    for h in range(HGRN_HEADS):
        sl = slice(h * LANES, (h + 1) * LANES)
        o_h, zg_h, dm_h = o[:, sl], zg[:, sl], dm[:, sl]
        gate = zg_h * _sigmoid(zg_h)
        do_h, dgn_h = _rms_bwd(o_h, gn, dm_h * gate)
        dgn = dgn + dgn_h
        do_parts.append(do_h)
        dzg_parts.append(dm_h * _rms(o_h, gn) * _silu_grad(zg_h))
    return jnp.concatenate(do_parts, axis=1), jnp.concatenate(dzg_parts, axis=1), dgn


def _rope_slabs(x, t_c, t_s1, t_s2, transpose):
    fn = _rope_t if transpose else _rope
    return jnp.concatenate(
        [fn(x[:, h * LANES:(h + 1) * LANES], t_c, t_s1, t_s2) for h in range(x.shape[1] // LANES)], axis=1)


def _loss_head(h, tgt, w):
    d = h.shape[1]
    r = lax.rsqrt(jnp.mean(h * h, axis=-1, keepdims=True) + EPS)
    xh = h * r
    err = xh * w - tgt
    loss = 0.5 * jnp.sum(jnp.mean(err * err, axis=-1, keepdims=True), axis=0, keepdims=True)
    dy = err / d
    dxh = dy * w
    dh = r * (dxh - xh * jnp.mean(dxh * xh, axis=-1, keepdims=True))
    return dh, dh, jnp.sum(dy * xh, axis=0, keepdims=True), jnp.broadcast_to(loss, (1, LANES))


def _mlp_fwd(h, norm, w_up, w_down, tag, loss_head=None):
    d = h.shape[1]

    def up(x, g, wu):
        x_n = _rms(x, g).astype(BF16)
        return x_n, jnp.square(jnp.maximum(_dot(x_n, wu, _NN), 0.0))

    xn, act = _rowcall(up, [h], [norm, w_up], [(d, BF16), (w_up.shape[1], BF16)], [], tr=512, name=f"{tag}_up")
    if callable(w_down):
        w_down = w_down(act)
    if loss_head is None:
        return _mm(act, w_down, mode="nn", add=h, name=f"{tag}_down"), (h, xn, act)
    tgt, final_norm = loss_head

    def down_and_loss(a, res, t, wd, g):
        return _loss_head(res + _dot(a, wd, _NN), t, g)

    return _rowcall(down_and_loss, [act, h, tgt], [w_down, final_norm], [(d, F32), (d, BF16)], [d, LANES],
                    name=f"{tag}_down_loss"), (h, xn, act)


def _mlp_bwd(dh_out, dh_out_bf, saved, norm, w_up, w_down, tag, after=None):
    h, xn, act = saved
    d = h.shape[1]
    du = _mm(dh_out_bf, w_down, mode="nt", relu2_of=act, out_dtype=BF16, after=after, name=f"{tag}_bwd_du")
    dw_down = _mm(act, dh_out_bf, mode="tn", name=f"{tag}_bwd_wdown")
    dw_up = _mm(xn, du, mode="tn", name=f"{tag}_bwd_wup")

    def up_norm_bwd(x, d_u, dres, g, wu):
        dx, dw = _rms_bwd(x, g, _dot(d_u, wu, _NT))
        return dx + dres, dx + dres, dw

    dh, dh_bf, dnorm = _rowcall(up_norm_bwd, [h, du, dh_out], [norm, w_up], [(d, F32), (d, BF16)], [d], tr=512,
                                name=f"{tag}_bwd_dxn")
    return dh, dh_bf, dnorm, dw_up, dw_down


def _row_major(g):
    return g.reshape(g.shape[0] * g.shape[1], g.shape[2])


def _col_major(g):
    return jnp.transpose(g, (1, 0, 2)).reshape(g.shape[1], g.shape[0] * g.shape[2])


def _col_terms(dw):
    k, n = dw.shape
    return jnp.transpose(dw.reshape(k, N_DEV, n // N_DEV), (1, 0, 2))


def _row_terms(dw):
    return dw.reshape(N_DEV, dw.shape[0] // N_DEV, dw.shape[1])


def kernel(x, hgrn_norm, hgrn_w_q, hgrn_w_f, hgrn_w_i, hgrn_w_g, hgrn_g_norm, hgrn_w_o, hgrn_lb_logits, mla_norm, mla_w_dq, mla_q_norm, mla_w_uq, mla_w_o, kv_in_norm, kv_w_dkv, kv_norm, kv_w_uk, kv_w_uv, mlp_norm, mlp_w_up, mlp_w_down, final_norm, loss_target, m_hgrn_norm, m_hgrn_w_q, m_hgrn_w_f, m_hgrn_w_i, m_hgrn_w_g, m_hgrn_g_norm, m_hgrn_w_o, m_hgrn_lb_logits, m_mla_norm, m_mla_w_dq, m_mla_q_norm, m_mla_w_uq, m_mla_w_o, m_kv_in_norm, m_kv_w_dkv, m_kv_norm, m_kv_w_uk, m_kv_w_uv, m_mlp_norm, m_mlp_w_up, m_mlp_w_down, m_final_norm, v_hgrn_norm, v_hgrn_w_q, v_hgrn_w_f, v_hgrn_w_i, v_hgrn_w_g, v_hgrn_g_norm, v_hgrn_w_o, v_hgrn_lb_logits, v_mla_norm, v_mla_w_dq, v_mla_q_norm, v_mla_w_uq, v_mla_w_o, v_kv_in_norm, v_kv_w_dkv, v_kv_norm, v_kv_w_uk, v_kv_w_uv, v_mlp_norm, v_mlp_w_up, v_mlp_w_down, v_final_norm):
    given = dict(locals())
    weight_names = ["hgrn_norm", "hgrn_w_q", "hgrn_w_f", "hgrn_w_i", "hgrn_w_g", "hgrn_g_norm", "hgrn_w_o",
                    "hgrn_lb_logits", "mla_norm", "mla_w_dq", "mla_q_norm", "mla_w_uq", "mla_w_o", "kv_in_norm",
                    "kv_w_dkv", "kv_norm", "kv_w_uk", "kv_w_uv", "mlp_norm", "mlp_w_up", "mlp_w_down", "final_norm"]
    me = 4 * lax.axis_index("x") + 2 * lax.axis_index("y") + lax.axis_index("c")
    xs, tgt = x[0], loss_target[0]
    seq, d_model = xs.shape
    n_heads, hd = MLA_HEADS, LANES

    big_local = {
        "hgrn_w_q": hgrn_w_q[0], "hgrn_w_f": hgrn_w_f[0], "hgrn_w_i": hgrn_w_i[0], "hgrn_w_g": hgrn_w_g[0],
        "hgrn_w_o": hgrn_w_o[0], "mla_w_dq": mla_w_dq[0], "mla_w_uq": mla_w_uq[0], "mla_w_o": mla_w_o[0],
        "kv_w_dkv": kv_w_dkv, "kv_w_uk": kv_w_uk, "kv_w_uv": kv_w_uv,
        "mlp_w_up0": mlp_w_up[0], "mlp_w_up1": mlp_w_up[1], "mlp_w_down0": mlp_w_down[0], "mlp_w_down1": mlp_w_down[1],
    }
    big_names = list(big_local)
    col_sharded = {"mla_w_uq", "kv_w_uk", "kv_w_uv", "mlp_w_up0", "mlp_w_up1"}
    vec_local = jnp.concatenate([hgrn_norm, hgrn_lb_logits], axis=0)
    first_names = ["hgrn_w_q", "hgrn_w_f", "hgrn_w_i"]
    proj_names = first_names + ["hgrn_w_g"]
    later_names = {"hgrn_o": ["hgrn_w_g", "hgrn_w_o"], "up0": ["mlp_w_up0"], "down0": ["mlp_w_down0"],
                   "mla": ["kv_w_dkv", "kv_w_uk", "kv_w_uv", "mla_w_dq", "mla_w_uq", "mla_w_o"],
                   "mlp1": ["mlp_w_up1", "mlp_w_down1"]}

    def unshard(names, arrays):
        return {k: (_col_major(a) if k in col_sharded else _row_major(a)) for k, a in zip(names, arrays)}

    first_state, token = _exchange_start([big_local[k].astype(BF16) for k in first_names] + [vec_local], scatter=False,
                                         name="gather_first_start")
    gather_state = {}
    for tag, names in later_names.items():
        gather_state[tag], token = _exchange_start([big_local[k].astype(BF16) for k in names], scatter=False,
                                                   after=token, name=f"gather_{tag}_start")

    def gather_wait(tag, after):
        w.update(unshard(later_names[tag], _exchange_wait(gather_state[tag], after, name=f"gather_{tag}_wait")))
        return [w[k] for k in later_names[tag]]

    gathered = _exchange_wait(first_state, token, name="gather_first_wait")
    w = unshard(first_names, gathered[:-1])
    vec_full = jnp.transpose(gathered[-1], (1, 0, 2)).reshape(3, d_model)
    hgrn_norm_full, lb_logits_full = vec_full[0:1], vec_full[1:3]
    t_c, t_s1, t_s2 = _rope_tables(seq)
    kv_lora = kv_w_uk.shape[0]

    def hgrn_proj(a, g, *weights):
        xn = _rms(a, g).astype(BF16)
        return (xn, *[_dot(xn, wt, _NN) for wt in weights])

    xn0, zq, zf, zi = _rowcall(hgrn_proj, [xs], [hgrn_norm_full] + [w[k] for k in first_names],
                               [(d_model, BF16)] + [(d_model, F32)] * 3, [], tr=512, name="hgrn_proj")
    o_rec, states = _hgrn_fwd(zq, zf, zi, lb_logits_full, name="hgrn_fwd")
    gather_wait("hgrn_o", o_rec)

    def gate_out(o, x_n, res, gn, wg, wo):
        z = _dot(x_n, wg, _NN)
        m = _head_norm_gate(o, z, gn)[0].astype(BF16)
        return z, m, res + _dot(m, wo, _NN)

    zg, mixed, h1 = _rowcall(gate_out, [o_rec, xn0, xs], [hgrn_g_norm, w["hgrn_w_g"], w["hgrn_w_o"]],
                             [(d_model, F32), (d_model, BF16), (d_model, F32)], [], name="hgrn_gate_out")
    h2, mlp0_saved = _mlp_fwd(h1, mlp_norm[0:1], gather_wait("up0", h1)[0], lambda act: gather_wait("down0", act)[0],
                              "mlp0")
    gather_wait("mla", h2)
    w_uq3 = w["mla_w_uq"].reshape(-1, n_heads, MLA_NOPE + MLA_ROPE)
    w_uq_nope = w_uq3[:, :, :MLA_NOPE].reshape(-1, n_heads * hd)
    w_uq_rope = jnp.pad(w_uq3[:, :, MLA_NOPE:], ((0, 0), (0, 0), (0, hd - MLA_ROPE))).reshape(-1, n_heads * hd)
    w_dkv_pad = jnp.pad(w["kv_w_dkv"], ((0, 0), (0, kv_lora + hd - w["kv_w_dkv"].shape[1])))

    q_lora, qk_cols = w["mla_w_dq"].shape[1], n_heads * hd

    def mla_qkv(a, tc, ts1, ts2, g_kv_in, g_mla, g_q, g_kv, wdq, wn, wr, wdkv, wuk, wuv):
        h_n, x_n = _rms(a, g_kv_in).astype(BF16), _rms(a, g_mla).astype(BF16)
        cq = _dot(x_n, wdq, _NN)
        cq_n = _rms(cq, g_q).astype(BF16)
        q_nope = _dot(cq_n, wn, _NN) * Q_PRESCALE
        q_rope = _rope_slabs(_dot(cq_n, wr, _NN) * Q_PRESCALE, tc, ts1, ts2, False)
        c_all = _dot(h_n, wdkv, _NN)
        lat = _rms(c_all[:, :kv_lora], g_kv).astype(BF16)
        return (h_n, x_n, cq, cq_n, q_nope, q_rope, c_all, lat, _rope(c_all[:, kv_lora:], tc, ts1, ts2),
                _dot(lat, wuk, _NN), _dot(lat, wuv, _NN))

    hn, xn2, cq_pre, c_q, qn, qr, ckr, c_kv, kr, kn, vv = _rowcall(
        mla_qkv, [h2, t_c, t_s1, t_s2],
        [kv_in_norm[None, :], mla_norm, mla_q_norm, kv_norm[None, :], w["mla_w_dq"], w_uq_nope, w_uq_rope, w_dkv_pad,
         w["kv_w_uk"], w["kv_w_uv"]],
        [(d_model, BF16), (d_model, BF16), (q_lora, F32), (q_lora, BF16), (qk_cols, BF16), (qk_cols, BF16),
         (kv_lora + hd, F32), (kv_lora, BF16), (hd, BF16), (qk_cols, BF16), (qk_cols, BF16)], [], name="mla_qkv")
    o_att, lse = _attn_fwd(qn, qr, kn, kr, vv, name="attn_fwd")
    h3 = _mm(o_att, w["mla_w_o"], mode="nn", add=h2, name="attn_out")
    gather_wait("mlp1", h3)
    (dh4, dh4_bf, g_final_norm, loss_part), mlp1_saved = _mlp_fwd(
        h3, mlp_norm[1:2], w["mlp_w_up1"], w["mlp_w_down1"], "mlp1", loss_head=(tgt, final_norm[None, :]))

    g = {}
    groups = {"mlp1": ["mlp_w_up1", "mlp_w_down1"],
              "mla": ["mla_w_o", "mla_w_uq", "mla_w_dq", "kv_w_uk", "kv_w_uv", "kv_w_dkv"],
              "mlp0": ["mlp_w_up0", "mlp_w_down0"],
              "hgrn_out": ["hgrn_w_o", "hgrn_w_g"],
              "hgrn_in": ["hgrn_w_q", "hgrn_w_f", "hgrn_w_i"]}
    scatter_state = {}

    def scatter_start(tag, after=None):
        scatter_state[tag], tok = _exchange_start(
            [(_col_terms if k in col_sharded else _row_terms)(g[k]) for k in groups[tag]], scatter=True, after=after,
            name=f"scatter_{tag}_start")
        return tok

    dh3, dh3_bf, g_mlp_norm1, g["mlp_w_up1"], g["mlp_w_down1"] = _mlp_bwd(
        dh4, dh4_bf, mlp1_saved, mlp_norm[1:2], w["mlp_w_up1"], w["mlp_w_down1"], "mlp1")
    def attn_out_bwd(dres, o, wo):
        d_o = _dot(dres, wo, _NT).astype(BF16)
        prod = d_o.astype(F32) * o.astype(F32)
        return d_o, jnp.concatenate([jnp.broadcast_to(jnp.sum(prod[:, h * hd:(h + 1) * hd], axis=1, keepdims=True),
                                                      (prod.shape[0], hd)) for h in range(n_heads)], axis=1)

    d_oatt, delta = _rowcall(attn_out_bwd, [dh3_bf, o_att], [w["mla_w_o"]], [(qk_cols, BF16), (qk_cols, F32)], [],
                             after=scatter_start("mlp1"), name="attn_out_bwd_x")
    g["mla_w_o"] = _mm(o_att, dh3_bf, mode="tn", name="attn_out_bwd_w")
    dqn, dqr, dkn, dvv, dkr = _attn_bwd(qn, qr, kn, kr, vv, d_oatt, lse, delta, name="attn_bwd")

    def q_path_bwd(cq, cq_n, x_n, d_qn, d_qr, tc, ts1, ts2, g_q, wdq, wn, wr):
        d_qn, d_qr = d_qn.astype(BF16), _rope_slabs(d_qr, tc, ts1, ts2, True).astype(BF16)
        d_cq, d_gq = _rms_bwd(cq, g_q, _dot(d_qn, wn, _NT) + _dot(d_qr, wr, _NT))
        d_cq = d_cq.astype(BF16)
        return _dot(d_cq, wdq, _NT), d_gq, _dot(x_n, d_cq, _TN), _dot(cq_n, d_qn, _TN), _dot(cq_n, d_qr, _TN)

    dxn2, g_q_norm, g_dq, g_uq_nope, g_uq_rope = _rowcall(
        q_path_bwd, [cq_pre, c_q, xn2, dqn, dqr, t_c, t_s1, t_s2], [mla_q_norm, w["mla_w_dq"], w_uq_nope, w_uq_rope],
        [(d_model, F32)], [q_lora, (d_model, q_lora), (q_lora, qk_cols), (q_lora, qk_cols)], name="mla_q_bwd")
    g["mla_w_dq"] = g_dq.astype(GRAD_WIRE_DTYPE)
    g["mla_w_uq"] = jnp.concatenate([g_uq_nope.reshape(q_lora, n_heads, hd),
                                     g_uq_rope.reshape(q_lora, n_heads, hd)[:, :, :MLA_ROPE]],
                                    axis=2).reshape(q_lora, -1).astype(GRAD_WIRE_DTYPE)

    def kv_path_bwd(c_all, lat, h_n, d_kn, d_v, d_kr_heads, tc, ts1, ts2, a, d_xn2, dres,
                    g_kv, g_kv_in, g_mla, wdkv, wuk, wuv):
        d_lat, d_gkv = _rms_bwd(c_all[:, :kv_lora], g_kv, _dot(d_kn, wuk, _NT) + _dot(d_v, wuv, _NT))
        d_kr = d_kr_heads[:, :hd]
        for h in range(1, n_heads):
            d_kr = d_kr + d_kr_heads[:, h * hd:(h + 1) * hd]
        d_all = jnp.concatenate([d_lat, _rope_t(d_kr, tc, ts1, ts2)], axis=1).astype(BF16)
        dx1, d_gkv_in = _rms_bwd(a, g_kv_in, _dot(d_all, wdkv, _NT))
        dx2, d_gmla = _rms_bwd(a, g_mla, d_xn2)
        d_a = dx1 + dx2 + dres
        return (d_a, d_a, d_gkv, d_gkv_in, d_gmla, _dot(h_n, d_all, _TN), _dot(lat, d_kn, _TN), _dot(lat, d_v, _TN))

    dh2, dh2_bf, g_kv_norm, g_kv_in_norm, g_mla_norm, g_dkv, g_uk, g_uv = _rowcall(
        kv_path_bwd, [ckr, c_kv, hn, dkn, dvv, dkr, t_c, t_s1, t_s2, h2, dxn2, dh3],
        [kv_norm[None, :], kv_in_norm[None, :], mla_norm, w_dkv_pad, w["kv_w_uk"], w["kv_w_uv"]],
        [(d_model, F32), (d_model, BF16)],
        [kv_lora, d_model, d_model, (d_model, kv_lora + hd), (kv_lora, qk_cols), (kv_lora, qk_cols)], name="mla_kv_bwd")
    g["kv_w_dkv"] = g_dkv[:, :kv_w_dkv.shape[1]].astype(GRAD_WIRE_DTYPE)
    g["kv_w_uk"], g["kv_w_uv"] = g_uk.astype(GRAD_WIRE_DTYPE), g_uv.astype(GRAD_WIRE_DTYPE)
    dh1, dh1_bf, g_mlp_norm0, g["mlp_w_up0"], g["mlp_w_down0"] = _mlp_bwd(
        dh2, dh2_bf, mlp0_saved, mlp_norm[0:1], w["mlp_w_up0"], w["mlp_w_down0"], "mlp0", after=scatter_start("mla"))

    g["hgrn_w_o"] = _mm(mixed, dh1_bf, mode="tn", after=scatter_start("mlp0"), name="hgrn_out_bwd_w")
    do_rec, dzg, g_g_norm = _rowcall(
        lambda dres, o, z, wo, gn: _head_norm_gate_bwd(o, z, _dot(dres, wo, _NT), gn), [dh1_bf, o_rec, zg],
        [w["hgrn_w_o"], hgrn_g_norm], [(d_model, F32), (d_model, BF16)], [hd], name="hgrn_gate_out_bwd")
    g["hgrn_w_g"] = _mm(xn0, dzg, mode="tn", name="hgrn_w_g_bwd_w")
    dzq, dzf, dzi, g_lb = _hgrn_bwd(zq, zf, zi, lb_logits_full, states, do_rec, scatter_start("hgrn_out"),
                                    name="hgrn_bwd")
    for nm, dz in (("hgrn_w_q", dzq), ("hgrn_w_f", dzf), ("hgrn_w_i", dzi)):
        g[nm] = _mm(xn0, dz, mode="tn", name=f"{nm}_bwd_w")

    def hgrn_proj_bwd(a, dres, *rest):
        dzs, gw, weights = rest[:4], rest[4], rest[5:]
        dxn = _dot(dzs[0], weights[0], _NT)
        for dz, wt in zip(dzs[1:], weights[1:]):
            dxn = dxn + _dot(dz, wt, _NT)
        dx, dw = _rms_bwd(a, gw, dxn)
        return dx + dres, dw

    grad_x, g_hgrn_norm = _rowcall(hgrn_proj_bwd, [xs, dh1, dzq, dzf, dzi, dzg],
                                   [hgrn_norm_full] + [w[k] for k in proj_names], [(d_model, F32)], [d_model],
                                   tr=512, name="hgrn_proj_bwd")

    small_parts = [g_hgrn_norm, g_lb, g_g_norm, g_mla_norm, g_q_norm, g_kv_in_norm, g_kv_norm, g_mlp_norm0,
                   g_mlp_norm1, g_final_norm, loss_part]
    small_sizes = [p.shape[1] for p in small_parts]
    small_terms = _exchange([jnp.concatenate(small_parts, axis=1)], scatter=False, name="gather_small")[0]
    small_sum = _sum_terms(small_terms, name="sum_small")
    last = scatter_start("hgrn_in", after=small_sum)
    offs = [0]
    for sz in small_sizes:
        offs.append(offs[-1] + sz)
    (s_hgrn_norm, s_lb, s_g_norm, s_mla_norm, s_q_norm, s_kv_in_norm, s_kv_norm, s_mlp_norm0, s_mlp_norm1, s_final_norm,
     s_loss) = [small_sum[:, a:b] for a, b in zip(offs[:-1], offs[1:])]
    shard = hgrn_norm.shape[1]
    g_lb_logits = _lb_logits_grad(lax.dynamic_slice_in_dim(s_lb, me * shard, shard, axis=1), hgrn_lb_logits,
                                  name="lb_logits_grad")
    loss = s_loss[0, 0]

    res = {}
    for tag, names in groups.items():
        for k, t in zip(names, _exchange_wait(scatter_state[tag], last, name=f"scatter_{tag}_wait")):
            if k.startswith("mlp_w_"):
                base, layer = k[:-1], int(k[-1])
                wk, mk, vk = given[base][layer], given["m_" + base][layer], given["v_" + base][layer]
            else:
                wk, mk, vk = given[k], given["m_" + k], given["v_" + k]
            shape = wk.shape
            wk, mk, vk = (a.reshape(shape[-2], shape[-1]) for a in (wk, mk, vk))
            upd = _adam(wk, t, mk, vk, name=f"adam_{k}")
            last = upd[0]
            res[k] = [o.reshape(shape) for o in upd]
    for base in ("mlp_w_up", "mlp_w_down"):
        res[base] = [jnp.stack([res[base + "0"][i], res[base + "1"][i]], axis=0) for i in range(4)]

    small_grads = {
        "hgrn_norm": lax.dynamic_slice_in_dim(s_hgrn_norm, me * shard, shard, axis=1),
        "hgrn_g_norm": s_g_norm, "hgrn_lb_logits": g_lb_logits, "mla_norm": s_mla_norm, "mla_q_norm": s_q_norm,
        "kv_in_norm": s_kv_in_norm, "kv_norm": s_kv_norm,
        "mlp_norm": jnp.concatenate([s_mlp_norm0, s_mlp_norm1], axis=0), "final_norm": s_final_norm,
    }
    small_names = list(small_grads)

    def flat(a):
        return a.reshape(1, -1)

    packed = [jnp.concatenate([flat(src[pre + k]) for k in small_names], axis=1)
              for src, pre in ((given, ""), (small_grads, ""), (given, "m_"), (given, "v_"))]
    small_out = _adam(packed[0], packed[1][None], packed[2], packed[3], name="adam_small")
    off = 0
    for k in small_names:
        size = given[k].size
        res[k] = [o[:, off:off + size].reshape(given[k].shape) for o in small_out]
        off += size

    outs = [loss, grad_x[None]]
    for i in range(4):
        outs += [res[k][i] for k in weight_names]
    return tuple(outs)
```

```python
import functools

import jax
import jax.numpy as jnp
from jax import lax
from jax.experimental import pallas as pl
from jax.experimental.pallas import tpu as pltpu

F32 = jnp.float32
BF16 = jnp.bfloat16

EPS = 1e-6
LANES = 128
N_DEV = 8
V7X_VMEM_LIMIT_BYTES = 56 << 20
MM_PIPELINE_BYTES = 30 << 20
MM_ROW_TILE = 512
GRAD_WIRE_DTYPE = BF16

HGRN_HEADS = 8
HGRN_CHUNK = 64
HGRN_SUB = 16
HGRN_HEADS_PER_STEP = 8
EXP_CLAMP = 80.0
MLA_HEADS = 16
MLA_NOPE = 128
MLA_ROPE = 64
ROPE_THETA = 10000.0
ATTN_SCALE = (MLA_NOPE + MLA_ROPE) ** -0.5

ADAM_LR = 0.001
ADAM_B1 = 0.9
ADAM_B2 = 0.999
ADAM_EPS = 1e-08
ADAM_WD = 0.01
ADAM_STEP = 10

_NN = ((1,), (0,))
_NT = ((1,), (1,))
_TN = ((0,), (0,))


def _params(*sem):
    return pltpu.CompilerParams(dimension_semantics=sem, vmem_limit_bytes=V7X_VMEM_LIMIT_BYTES)


def _dot(a, b, dims):
    return lax.dot_general(a.astype(BF16), b.astype(BF16), (dims, ((), ())), preferred_element_type=F32)


def _dot_f32(a, b, dims=_NN):
    return lax.dot_general(a, b, (dims, ((), ())), precision=lax.Precision.HIGH, preferred_element_type=F32)


def _sigmoid(x):
    return 1.0 / (1.0 + jnp.exp(-x))


def _rms(x, w):
    r = lax.rsqrt(jnp.mean(x * x, axis=-1, keepdims=True) + EPS)
    return x * r * w


def _rms_bwd(x, w, dy):
    r = lax.rsqrt(jnp.mean(x * x, axis=-1, keepdims=True) + EPS)
    xh = x * r
    dw = jnp.sum(dy * xh, axis=0, keepdims=True)
    dxh = dy * w
    dx = r * (dxh - xh * jnp.mean(dxh * xh, axis=-1, keepdims=True))
    return dx, dw


def _mm_tiles(m, n, k, a_bytes, b_bytes, out_tile_bytes):
    tm = min(m, MM_ROW_TILE)
    for tn in (n, 2048, 1024, 512, 256, LANES):
        if tn <= n and n % tn == 0:
            if 2 * (tm * k * a_bytes + k * tn * b_bytes + tm * tn * out_tile_bytes) <= MM_PIPELINE_BYTES:
                return tm, tn
    return tm, min(n, LANES)


def _mm(a, b, *, mode, name, out_dtype=None, add=None, relu2_of=None, after=None, col_shards=None):
    if mode == "nn":
        (m, k), (k2, n) = a.shape, b.shape
    elif mode == "nt":
        (m, k), (n, k2) = a.shape, b.shape
    else:
        (k, m), (k2, n) = a.shape, b.shape
    assert k == k2, (name, a.shape, b.shape)
    if out_dtype is None:
        out_dtype = GRAD_WIRE_DTYPE if mode == "tn" else F32
    tile_bytes = sum(x.dtype.itemsize for x in (add, relu2_of) if x is not None) + jnp.dtype(out_dtype).itemsize
    tm, tn = _mm_tiles(m, n, k, a.dtype.itemsize, b.dtype.itemsize, tile_bytes)
    if col_shards is not None:
        assert add is None and relu2_of is None
        tn = n // col_shards
    assert m % tm == 0 and n % tn == 0, (name, m, n)
    dims = {"nn": _NN, "nt": _NT, "tn": _TN}[mode]
    a_spec = pl.BlockSpec((k, tm), lambda i, j: (0, i)) if mode == "tn" else pl.BlockSpec((tm, k), lambda i, j: (i, 0))
    b_spec = pl.BlockSpec((tn, k), lambda i, j: (j, 0)) if mode == "nt" else pl.BlockSpec((k, tn), lambda i, j: (0, j))
    o_spec = pl.BlockSpec((tm, tn), lambda i, j: (i, j))
    operands, in_specs = [a, b], [a_spec, b_spec]
    for extra in (add, relu2_of):
        if extra is not None:
            assert extra.shape == (m, n), (name, extra.shape)
            operands.append(extra)
            in_specs.append(o_spec)
    n_in = len(operands)
    if after is not None:
        operands.append(after)
        in_specs.append(pl.BlockSpec(memory_space=pl.ANY))
    out_shape = jax.ShapeDtypeStruct((m, n), out_dtype)
    if col_shards is not None:
        out_shape = jax.ShapeDtypeStruct((col_shards, m, tn), out_dtype)
        o_spec = pl.BlockSpec((None, tm, tn), lambda i, j: (j, i, 0))

    def body(*refs):
        acc = _dot(refs[0][...], refs[1][...], dims)
        extras, outs = refs[2:n_in], refs[len(operands):]
        if add is not None:
            acc = acc + extras[0][...]
        if relu2_of is not None:
            acc = acc * (2.0 * jnp.sqrt(extras[-1][...].astype(F32)))
        outs[0][...] = acc.astype(out_dtype)

    return pl.pallas_call(
        body, name=name, grid=(m // tm, n // tn), in_specs=in_specs, out_specs=o_spec, out_shape=out_shape,
        compiler_params=_params("parallel", "parallel"),
    )(*operands)


def _rowcall(fn, rows, consts, outs, accs, *, name, tr=256, after=None):
    s = rows[0].shape[0]
    tr = min(tr, s)
    assert s % tr == 0
    n_out = len(outs)
    accs = [(1, a) if isinstance(a, int) else a for a in accs]
    in_specs = [pl.BlockSpec((tr, r.shape[1]), lambda i: (i, 0)) for r in rows]
    in_specs += [pl.BlockSpec(c.shape, lambda i, nd=c.ndim: (0,) * nd) for c in consts]
    out_shape = [jax.ShapeDtypeStruct((s, w), dt) for w, dt in outs] + [jax.ShapeDtypeStruct(a, F32) for a in accs]
    out_specs = [pl.BlockSpec((tr, w), lambda i: (i, 0)) for w, _ in outs] + [pl.BlockSpec(a, lambda i: (0, 0)) for a in accs]
    n_in = len(rows) + len(consts)

    def body(*refs):
        res = fn(*[r[...] for r in refs[:n_in]])
        out_refs = refs[n_in + (after is not None):]
        for ref, val in zip(out_refs[:n_out], res[:n_out]):
            ref[...] = val.astype(ref.dtype)
        i = pl.program_id(0)
        for ref, val in zip(out_refs[n_out:], res[n_out:]):
            @pl.when(i == 0)
            def _(ref=ref, val=val):
                ref[...] = val

            @pl.when(i > 0)
            def _(ref=ref, val=val):
                ref[...] += val

    behind = [] if after is None else [after]
    return pl.pallas_call(
        body, name=name, grid=(s // tr,), in_specs=in_specs + [pl.BlockSpec(memory_space=pl.ANY)] * len(behind),
        out_specs=out_specs, out_shape=out_shape, compiler_params=_params("arbitrary" if accs else "parallel"),
    )(*rows, *consts, *behind)


def _rope_tables(seq):
    half = MLA_ROPE // 2
    inv_freq = ROPE_THETA ** (-jnp.arange(half, dtype=F32) / half)
    ang = jnp.arange(seq, dtype=F32)[:, None] * inv_freq[None, :]
    cos, sin, zero = jnp.cos(ang), jnp.sin(ang), jnp.zeros((seq, half), F32)
    t_c = jnp.concatenate([cos, cos, zero, zero], axis=1)
    t_s1 = jnp.concatenate([-sin, zero, zero, zero], axis=1)
    t_s2 = jnp.concatenate([zero, sin, zero, zero], axis=1)
    return t_c, t_s1, t_s2


def _rope(slab, t_c, t_s1, t_s2):
    return slab * t_c + pltpu.roll(slab, 96, 1) * t_s1 + pltpu.roll(slab, 32, 1) * t_s2


def _rope_t(d, t_c, t_s1, t_s2):
    return d * t_c + pltpu.roll(d * t_s1, 32, 1) + pltpu.roll(d * t_s2, 96, 1)


def _lower_bound(logits):
    l0, l1 = logits[0:1, :], logits[1:2, :]
    mx = jnp.maximum(l0, l1)
    e0, e1 = jnp.exp(l0 - mx), jnp.exp(l1 - mx)
    return e0 / (e0 + e1)


def _tri(n, lower):
    row = lax.broadcasted_iota(jnp.int32, (n, n), 0)
    col = lax.broadcasted_iota(jnp.int32, (n, n), 1)
    return (row >= col) if lower else (row <= col)


def _hgrn_fwd(zq, zf, zi, lb_logits, *, name):
    s, d = zq.shape
    h_n, c, hp = d // LANES, HGRN_CHUNK, HGRN_HEADS_PER_STEP
    nc = s // c

    def body(zq_ref, zf_ref, zi_ref, lb_ref, o_ref, st_ref, state_sc, b_sc):
        @pl.when(pl.program_id(1) == 0)
        def _():
            state_sc[...] = jnp.zeros_like(state_sc)

        lower = _tri(c, True).astype(F32)
        hs = range(hp)
        sls = [slice(hh * LANES, (hh + 1) * LANES) for hh in hs]
        lb = [_lower_bound(lb_ref[:, sl]) for sl in sls]
        zq_v = [zq_ref[:, sl] for sl in sls]
        q = [z * _sigmoid(z) for z in zq_v]
        f = [lb[hh] + (1.0 - lb[hh]) * _sigmoid(zf_ref[:, sls[hh]]) for hh in hs]
        g = [jnp.log(x) for x in f]
        k = [1.0 - x for x in f]
        v = [zi_ref[:, sl] for sl in sls]
        b = [_dot_f32(lower, x) for x in g]
        s0t = [state_sc[hh] for hh in hs]
        for hh in hs:
            st_ref[hh] = s0t[hh]
            b_sc[hh] = b[hh]
        o_inter = [_dot(q[hh] * jnp.exp(b[hh]), s0t[hh], _NT) for hh in hs]
        scores = [[] for _ in hs]
        for i in range(c // HGRN_SUB):
            lo = i * HGRN_SUB
            for hh in hs:
                ref = b_sc[hh, lo - 1:lo, :] if i > 0 else jnp.zeros((1, LANES), F32)
                qt = q[hh][lo:lo + HGRN_SUB, :] * jnp.exp(b[hh][lo:lo + HGRN_SUB, :] - ref)
                dec = jnp.exp(jnp.minimum(ref - b[hh], EXP_CLAMP))
                scores[hh].append(_dot(qt, k[hh] * dec, _NT))
        a = [jnp.where(_tri(c, True), jnp.concatenate(sc, axis=0), 0.0) for sc in scores]
        for hh in hs:
            o_ref[:, sls[hh]] = o_inter[hh] + _dot(a[hh], v[hh], _NN)
        bl = [b_sc[hh, c - 1:c, :] for hh in hs]
        for hh in hs:
            state_sc[hh] = s0t[hh] * jnp.exp(bl[hh]) + _dot(v[hh], k[hh] * jnp.exp(bl[hh] - b[hh]), _TN)

    tile = pl.BlockSpec((c, hp * LANES), lambda h, i: (i, h))
    return pl.pallas_call(
        body, name=name, grid=(h_n // hp, nc),
        in_specs=[tile, tile, tile, pl.BlockSpec((2, hp * LANES), lambda h, i: (0, h))],
        out_specs=[tile, pl.BlockSpec((hp, None, LANES, LANES), lambda h, i: (h, i, 0, 0))],
        out_shape=[jax.ShapeDtypeStruct((s, d), F32), jax.ShapeDtypeStruct((h_n, nc, LANES, LANES), F32)],
        scratch_shapes=[pltpu.VMEM((hp, LANES, LANES), F32), pltpu.VMEM((hp, c, LANES), F32)],
        compiler_params=_params("parallel", "arbitrary"),
    )(zq, zf, zi, lb_logits)


def _hgrn_bwd(zq, zf, zi, lb_logits, states, do, after, *, name):
    s, d = zq.shape
    h_n, c, hp = d // LANES, HGRN_CHUNK, HGRN_HEADS_PER_STEP
    nc = s // c

    def body(zq_ref, zf_ref, zi_ref, lb_ref, st_ref, do_ref, _, dzq_ref, dzf_ref, dzi_ref, dlb_ref, dstate_sc, b_sc):
        @pl.when(pl.program_id(1) == 0)
        def _():
            dstate_sc[...] = jnp.zeros_like(dstate_sc)
            dlb_ref[...] = jnp.zeros_like(dlb_ref)

        lower, upper = _tri(c, True), _tri(c, False).astype(F32)
        lower_f = lower.astype(F32)
        last_row = lax.broadcasted_iota(jnp.int32, (c, LANES), 0) == c - 1
        hs = range(hp)
        sls = [slice(hh * LANES, (hh + 1) * LANES) for hh in hs]
        lb = [_lower_bound(lb_ref[:, sl]) for sl in sls]
        zq_v = [zq_ref[:, sl] for sl in sls]
        sq = [_sigmoid(z) for z in zq_v]
        q = [zq_v[hh] * sq[hh] for hh in hs]
        sf = [_sigmoid(zf_ref[:, sl]) for sl in sls]
        f = [lb[hh] + (1.0 - lb[hh]) * sf[hh] for hh in hs]
        g = [jnp.log(x) for x in f]
        k = [1.0 - x for x in f]
        v = [zi_ref[:, sl] for sl in sls]
        d_o = [do_ref[:, sl] for sl in sls]
        b = [_dot_f32(lower_f, x) for x in g]
        s0t = [st_ref[hh] for hh in hs]
        ds1t = [dstate_sc[hh] for hh in hs]
        for hh in hs:
            b_sc[hh] = b[hh]
        bl = [b_sc[hh, c - 1:c, :] for hh in hs]
        eb = [jnp.exp(x) for x in b]
        ebl = [jnp.exp(x) for x in bl]
        dec_end = [jnp.exp(bl[hh] - b[hh]) for hh in hs]
        da = [jnp.where(lower, _dot(d_o[hh], v[hh], _NT), 0.0) for hh in hs]
        dq_inter = [_dot(d_o[hh], s0t[hh], _NN) * eb[hh] for hh in hs]
        dk_state = [_dot(v[hh], ds1t[hh], _NN) * dec_end[hh] for hh in hs]
        dv_state = [_dot(k[hh] * dec_end[hh], ds1t[hh], _NT) for hh in hs]
        for hh in hs:
            dstate_sc[hh] = ds1t[hh] * ebl[hh] + _dot(d_o[hh], q[hh] * eb[hh], _TN)
        dk = list(dk_state)
        scores, dq_blocks = [[] for _ in hs], [[] for _ in hs]
        for i in range(c // HGRN_SUB):
            lo = i * HGRN_SUB
            for hh in hs:
                ref = b_sc[hh, lo - 1:lo, :] if i > 0 else jnp.zeros((1, LANES), F32)
                grow = jnp.exp(b[hh][lo:lo + HGRN_SUB, :] - ref)
                qt = q[hh][lo:lo + HGRN_SUB, :] * grow
                dec = jnp.exp(jnp.minimum(ref - b[hh], EXP_CLAMP))
                kd = k[hh] * dec
                scores[hh].append(_dot(qt, kd, _NT))
                da_i = da[hh][lo:lo + HGRN_SUB, :]
                dq_blocks[hh].append(_dot_f32(da_i, kd, _NN) * grow)
                dk[hh] = dk[hh] + _dot_f32(da_i, qt, _TN) * dec
        a = [jnp.where(lower, jnp.concatenate(sc, axis=0), 0.0) for sc in scores]
        dv = [_dot(a[hh], d_o[hh], _TN) + dv_state[hh] for hh in hs]
        dq = [dq_inter[hh] + jnp.concatenate(dq_blocks[hh], axis=0) for hh in hs]
        db_last = [jnp.sum(k[hh] * dk_state[hh], axis=0, keepdims=True)
                   + ebl[hh] * jnp.sum(s0t[hh] * ds1t[hh], axis=0, keepdims=True) for hh in hs]
        db = [q[hh] * dq[hh] - k[hh] * dk[hh] + jnp.where(last_row, db_last[hh], 0.0) for hh in hs]
        dg = [_dot_f32(upper, x) for x in db]
        df = [dg[hh] / f[hh] - dk[hh] for hh in hs]
        for hh in hs:
            sl = sls[hh]
            dzf_ref[:, sl] = (df[hh] * (1.0 - lb[hh]) * sf[hh] * (1.0 - sf[hh])).astype(BF16)
            dlb_ref[:, sl] += jnp.sum(df[hh] * (1.0 - sf[hh]), axis=0, keepdims=True)
            dzq_ref[:, sl] = (dq[hh] * sq[hh] * (1.0 + zq_v[hh] * (1.0 - sq[hh]))).astype(BF16)
            dzi_ref[:, sl] = dv[hh].astype(BF16)

    tile = pl.BlockSpec((c, hp * LANES), lambda h, i: (nc - 1 - i, h))
    out = jax.ShapeDtypeStruct((s, d), BF16)
    return pl.pallas_call(
        body, name=name, grid=(h_n // hp, nc),
        in_specs=[tile, tile, tile, pl.BlockSpec((2, hp * LANES), lambda h, i: (0, h)),
                  pl.BlockSpec((hp, None, LANES, LANES), lambda h, i: (h, nc - 1 - i, 0, 0)), tile,
                  pl.BlockSpec(memory_space=pl.ANY)],
        out_specs=[tile, tile, tile, pl.BlockSpec((1, hp * LANES), lambda h, i: (0, h))],
        out_shape=[out, out, out, jax.ShapeDtypeStruct((1, d), F32)],
        scratch_shapes=[pltpu.VMEM((hp, LANES, LANES), F32), pltpu.VMEM((hp, c, LANES), F32)],
        compiler_params=_params("parallel", "arbitrary"),
    )(zq, zf, zi, lb_logits, states, do, after)


ATTN_SUB_ROWS = 256
LOG2E = 1.4426950408889634
LN2 = 0.6931471805599453
Q_PRESCALE = ATTN_SCALE * LOG2E


def _attn_tile(s):
    return min(1024, max(128, s // 2))


def _causal_pairs(n, q_major):
    pairs = [(i, j) for i in range(n) for j in range(i + 1)] if q_major else [(i, j) for j in range(n) for i in range(j, n)]
    return jnp.asarray([p[0] for p in pairs], jnp.int32), jnp.asarray([p[1] for p in pairs], jnp.int32)


def _sub_scores(qn_ref, qr_ref, k, r, sub, t, diagonal):
    q = jnp.concatenate([qn_ref[r:r + sub, :], qr_ref[r:r + sub, :]], axis=1)
    if not diagonal:
        return q, _dot(q, k, _NT)
    cols = r + sub
    keep = lax.broadcasted_iota(jnp.int32, (sub, cols), 1) <= r + lax.broadcasted_iota(jnp.int32, (sub, cols), 0)
    return q, jnp.where(keep, _dot(q, k[:cols], _NT), -jnp.inf)


def _attn_fwd(qn, qr, kn, kr, v, *, name):
    s, t = qn.shape[0], _attn_tile(qn.shape[0])
    sub = min(t, ATTN_SUB_ROWS)
    q_blk, k_blk = _causal_pairs(s // t, True)

    def body(qi_ref, kj_ref, qn_ref, qr_ref, kn_ref, kr_ref, v_ref, o_ref, lse_ref, m_sc, l_sc, acc_sc):
        p_id = pl.program_id(1)
        i, j = qi_ref[p_id], kj_ref[p_id]

        @pl.when(j == 0)
        def _():
            m_sc[...] = jnp.full_like(m_sc, -jnp.inf)
            l_sc[...] = jnp.zeros_like(l_sc)
            acc_sc[...] = jnp.zeros_like(acc_sc)

        def update(diagonal):
            k = jnp.concatenate([kn_ref[...], kr_ref[...]], axis=1)
            v = v_ref[...]
            starts = list(range(0, t, sub))
            scs = [_sub_scores(qn_ref, qr_ref, k, r, sub, t, diagonal)[1] for r in starts]
            ps, alphas = [], []
            for r, sc in zip(starts, scs):
                m_prev = m_sc[r:r + sub, :]
                m_new = jnp.maximum(m_prev, jnp.max(sc, axis=1, keepdims=True))
                alpha = jnp.exp2(m_prev - m_new)
                p = jnp.exp2(sc - m_new[:, :1])
                l_sc[r:r + sub, :] = alpha * l_sc[r:r + sub, :] + jnp.sum(p, axis=1, keepdims=True)
                m_sc[r:r + sub, :] = m_new
                ps.append(p)
                alphas.append(alpha)
            for r, p, alpha in zip(starts, ps, alphas):
                acc_sc[r:r + sub, :] = alpha * acc_sc[r:r + sub, :] + _dot(p, v[:p.shape[1]], _NN)

        @pl.when(j < i)
        def _():
            update(False)

        @pl.when(j == i)
        def _():
            update(True)
            o_ref[...] = (acc_sc[...] / l_sc[...]).astype(BF16)
            lse_ref[...] = m_sc[...] + jnp.log(l_sc[...]) * LOG2E

    q_spec = pl.BlockSpec((t, LANES), lambda h, p, qi, kj: (qi[p], h))
    k_spec = pl.BlockSpec((t, LANES), lambda h, p, qi, kj: (kj[p], h))
    kr_spec = pl.BlockSpec((t, LANES), lambda h, p, qi, kj: (kj[p], 0))
    stat = pltpu.VMEM((t, LANES), F32)
    return pl.pallas_call(
        body, name=name,
        grid_spec=pltpu.PrefetchScalarGridSpec(
            num_scalar_prefetch=2, grid=(MLA_HEADS, q_blk.shape[0]),
            in_specs=[q_spec, q_spec, k_spec, kr_spec, k_spec], out_specs=[q_spec, q_spec],
            scratch_shapes=[stat, stat, stat]),
        out_shape=[jax.ShapeDtypeStruct(qn.shape, BF16), jax.ShapeDtypeStruct(qn.shape, F32)],
        compiler_params=_params("parallel", "arbitrary"),
    )(q_blk, k_blk, qn, qr, kn, kr, v)


def _attn_bwd(qn, qr, kn, kr, v, do, lse, delta, *, name):
    s, t = qn.shape[0], _attn_tile(qn.shape[0])
    n, sub = s // t, min(t, ATTN_SUB_ROWS)
    q_blk, k_blk = _causal_pairs(n, False)

    def body(qi_ref, kj_ref, qn_ref, qr_ref, kn_ref, kr_ref, v_ref, do_ref, lse_ref, delta_ref,
             dqn_ref, dqr_ref, dkn_ref, dv_ref, dkr_ref, dk_sc, dv_sc):
        p_id = pl.program_id(1)
        i, j = qi_ref[p_id], kj_ref[p_id]

        @pl.when(p_id == 0)
        def _():
            dqn_ref[...] = jnp.zeros_like(dqn_ref)
            dqr_ref[...] = jnp.zeros_like(dqr_ref)

        @pl.when(i == j)
        def _():
            dk_sc[...] = jnp.zeros_like(dk_sc)
            dv_sc[...] = jnp.zeros_like(dv_sc)

        def accumulate(diagonal):
            k = jnp.concatenate([kn_ref[...], kr_ref[...]], axis=1)
            v = v_ref[...]
            starts = list(range(0, t, sub))
            qs, d_os, scs, dps = [], [], [], []
            for r in starts:
                q, sc = _sub_scores(qn_ref, qr_ref, k, r, sub, t, diagonal)
                d_o = do_ref[r:r + sub, :]
                qs.append(q)
                d_os.append(d_o)
                scs.append(sc)
                dps.append(_dot(d_o, v[:sc.shape[1]], _NT))
            ps, dss = [], []
            for r, sc, dp in zip(starts, scs, dps):
                p = jnp.exp2(sc - lse_ref[r:r + sub, :][:, :1])
                ps.append(p.astype(BF16))
                dss.append((p * (dp - delta_ref[r:r + sub, :][:, :1])).astype(BF16))
            for r, q, d_o, p, ds in zip(starts, qs, d_os, ps, dss):
                cols = p.shape[1]
                dv_sc[:cols, :] += _dot(p, d_o, _TN)
                dk_sc[:cols, :] += _dot(ds, q, _TN)
                dq = _dot(ds, k[:cols], _NN) * ATTN_SCALE
                rows = pl.ds(pl.multiple_of(i * t + r, sub), sub)
                dqn_ref[rows, :] += dq[:, :LANES]
                dqr_ref[rows, :] += dq[:, LANES:]

        @pl.when(j < i)
        def _():
            accumulate(False)

        @pl.when(j == i)
        def _():
            accumulate(True)

        @pl.when(i == n - 1)
        def _():
            dkn_ref[...] = (dk_sc[:, :LANES] * LN2).astype(BF16)
            dkr_ref[...] = dk_sc[:, LANES:] * LN2
            dv_ref[...] = dv_sc[...].astype(BF16)

    q_spec = pl.BlockSpec((t, LANES), lambda h, p, qi, kj: (qi[p], h))
    k_spec = pl.BlockSpec((t, LANES), lambda h, p, qi, kj: (kj[p], h))
    kr_spec = pl.BlockSpec((t, LANES), lambda h, p, qi, kj: (kj[p], 0))
    head_spec = pl.BlockSpec((s, LANES), lambda h, p, qi, kj: (0, h))
    f32_out, bf16_out = jax.ShapeDtypeStruct(qn.shape, F32), jax.ShapeDtypeStruct(qn.shape, BF16)
    return pl.pallas_call(
        body, name=name,
        grid_spec=pltpu.PrefetchScalarGridSpec(
            num_scalar_prefetch=2, grid=(MLA_HEADS, q_blk.shape[0]),
            in_specs=[q_spec, q_spec, k_spec, kr_spec, k_spec, q_spec, q_spec, q_spec],
            out_specs=[head_spec, head_spec, k_spec, k_spec, k_spec],
            scratch_shapes=[pltpu.VMEM((t, 2 * LANES), F32), pltpu.VMEM((t, LANES), F32)]),
        out_shape=[f32_out, f32_out, bf16_out, bf16_out, f32_out],
        compiler_params=_params("parallel", "arbitrary"),
    )(q_blk, k_blk, qn, qr, kn, kr, v, do, lse, delta)


def _exchange(arrs, *, scatter, name):
    n = len(arrs)
    out_shape = [jax.ShapeDtypeStruct(a.shape if scatter else (N_DEV, *a.shape), a.dtype) for a in arrs]

    def body(*refs):
        ins, outs = refs[:n], refs[n:2 * n]
        send_sems, recv_sems, local_sems = refs[2 * n:]
        x, y, c = lax.axis_index("x"), lax.axis_index("y"), lax.axis_index("c")
        me = 4 * x + 2 * y + c
        copies = []
        for k in range(n):
            local = pltpu.make_async_copy(ins[k].at[me] if scatter else ins[k], outs[k].at[me], local_sems.at[k])
            local.start()
            copies.append(local)
            for d in range(1, N_DEV):
                px, py, pc = (x + (d >> 2)) % 2, (y + ((d >> 1) & 1)) % 2, (c + (d & 1)) % 2
                peer = 4 * px + 2 * py + pc
                remote = pltpu.make_async_remote_copy(
                    src_ref=ins[k].at[peer] if scatter else ins[k], dst_ref=outs[k].at[me],
                    send_sem=send_sems.at[k, d - 1], recv_sem=recv_sems.at[k, d - 1],
                    device_id=(px, py, pc), device_id_type=pl.DeviceIdType.MESH)
                remote.start()
                copies.append(remote)
        for cp in copies:
            cp.wait()

    any_spec = pl.BlockSpec(memory_space=pl.ANY)
    return pl.pallas_call(
        body, name=name, in_specs=[any_spec] * n, out_specs=[any_spec] * n, out_shape=out_shape,
        scratch_shapes=[pltpu.SemaphoreType.DMA((n, N_DEV - 1)), pltpu.SemaphoreType.DMA((n, N_DEV - 1)),
                        pltpu.SemaphoreType.DMA((n,))],
    )(*arrs)


def _peers(x, y, c):
    out = []
    for d in range(1, N_DEV):
        px, py, pc = (x + (d >> 2)) % 2, (y + ((d >> 1) & 1)) % 2, (c + (d & 1)) % 2
        out.append(((px, py, pc), 4 * px + 2 * py + pc))
    return out


def _exchange_copies(ins, lands, send_sems, recv_sems, scatter):
    x, y, c = lax.axis_index("x"), lax.axis_index("y"), lax.axis_index("c")
    me = 4 * x + 2 * y + c
    local, remote = [], []
    for k in range(len(ins)):
        local.append(pltpu.make_async_copy(ins[k].at[me] if scatter else ins[k], lands[k].at[me],
                                           recv_sems.at[k * N_DEV + N_DEV - 1]))
        for d, (coords, peer) in enumerate(_peers(x, y, c)):
            remote.append(pltpu.make_async_remote_copy(
                src_ref=ins[k].at[peer] if scatter else ins[k], dst_ref=lands[k].at[me],
                send_sem=send_sems.at[k * N_DEV + d], recv_sem=recv_sems.at[k * N_DEV + d],
                device_id=coords, device_id_type=pl.DeviceIdType.MESH))
    return local, remote


def _exchange_start(arrs, *, scatter, name, after=None):
    n = len(arrs)
    hbm = pl.BlockSpec(memory_space=pltpu.HBM)
    sem = pl.BlockSpec(memory_space=pltpu.SEMAPHORE)
    lands = [lax.empty(a.shape if scatter else (N_DEV, *a.shape), a.dtype) for a in arrs]

    def body(*refs):
        ins, land_refs = refs[:n], refs[n:2 * n]
        first_out = 2 * n + (after is not None)
        send_sems, recv_sems, token = refs[first_out], refs[first_out + 1], refs[-1]
        local, remote = _exchange_copies(ins, land_refs, send_sems, recv_sems, scatter)
        for cp in local + remote:
            cp.start()
        token[...] = jnp.zeros_like(token)

    operands = [pltpu.with_memory_space_constraint(a, pltpu.HBM) for a in list(arrs) + lands]
    behind = [] if after is None else [after]
    res = pl.pallas_call(
        body, name=name,
        out_shape=(pltpu.SemaphoreType.DMA((n * N_DEV,)), pltpu.SemaphoreType.DMA((n * N_DEV,)),
                   *[pltpu.HBM(o.shape, o.dtype) for o in operands], jax.ShapeDtypeStruct((8, LANES), F32)),
        in_specs=[hbm] * (2 * n) + [pl.BlockSpec(memory_space=pl.ANY)] * len(behind),
        out_specs=(sem, sem, *[hbm] * (2 * n), pl.BlockSpec(memory_space=pltpu.VMEM)),
        input_output_aliases={i: 2 + i for i in range(2 * n)},
        compiler_params=pltpu.CompilerParams(has_side_effects=pltpu.SideEffectType.DATAFLOW_SIDE_EFFECTING),
    )(*operands, *behind)
    return (res[0], res[1], list(res[2:2 + n]), list(res[2 + n:2 + 2 * n]), scatter), res[-1]


def _exchange_wait(state, after, *, name):
    send_sems, recv_sems, ins, lands, scatter = state
    n = len(ins)
    hbm = pl.BlockSpec(memory_space=pltpu.HBM)
    sem = pl.BlockSpec(memory_space=pltpu.SEMAPHORE)

    def body(*refs):
        in_refs, land_refs = refs[:n], refs[n:2 * n]
        local, remote = _exchange_copies(in_refs, land_refs, refs[2 * n], refs[2 * n + 1], scatter)
        for cp in local:
            cp.wait()
        for cp in remote:
            cp.wait_send()
            cp.wait_recv()

    res = pl.pallas_call(
        body, name=name, out_shape=tuple(pltpu.HBM(o.shape, o.dtype) for o in ins + lands),
        in_specs=[hbm] * (2 * n) + [sem, sem, pl.BlockSpec(memory_space=pl.ANY)], out_specs=tuple([hbm] * (2 * n)),
        input_output_aliases={i: i for i in range(2 * n)},
        compiler_params=pltpu.CompilerParams(has_side_effects=pltpu.SideEffectType.DATAFLOW_SIDE_EFFECTING),
    )(*ins, *lands, send_sems, recv_sems, after)
    return list(res[n:])


def _adam(w, terms, m, v, *, name):
    r, c = w.shape
    n = terms.shape[0]
    tr = min(r, 128)
    assert r % tr == 0

    def body(w_ref, t_ref, m_ref, v_ref, g_out, d_out, m_out, v_out):
        g = t_ref[0].astype(F32)
        for s in range(1, n):
            g = g + t_ref[s].astype(F32)
        m1 = ADAM_B1 * m_ref[...] + (1.0 - ADAM_B1) * g
        v1 = ADAM_B2 * v_ref[...] + (1.0 - ADAM_B2) * jnp.square(g)
        m_hat = m1 / (1.0 - ADAM_B1 ** ADAM_STEP)
        v_hat = v1 / (1.0 - ADAM_B2 ** ADAM_STEP)
        g_out[...] = g
        d_out[...] = -ADAM_LR * (m_hat / (jnp.sqrt(v_hat) + ADAM_EPS) + ADAM_WD * w_ref[...])
        m_out[...] = m1
        v_out[...] = v1

    spec = pl.BlockSpec((tr, c), lambda i: (i, 0))
    out = jax.ShapeDtypeStruct((r, c), F32)
    return pl.pallas_call(
        body, name=name, grid=(r // tr,),
        in_specs=[spec, pl.BlockSpec((n, tr, c), lambda i: (0, i, 0)), spec, spec], out_specs=[spec] * 4,
        out_shape=[out] * 4, compiler_params=_params("parallel"),
    )(w, terms, m, v)


def _sum_terms(terms, *, name):
    n, _, p = terms.shape

    def body(t_ref, o_ref):
        acc = t_ref[0]
        for s in range(1, n):
            acc = acc + t_ref[s]
        o_ref[...] = acc

    return pl.pallas_call(body, name=name, out_shape=jax.ShapeDtypeStruct((1, p), F32))(terms)


def _lb_logits_grad(dlb, logits, *, name):
    def body(dlb_ref, l_ref, o_ref):
        lb = _lower_bound(l_ref[...])
        d0 = dlb_ref[...] * lb * (1.0 - lb)
        o_ref[...] = jnp.concatenate([d0, -d0], axis=0)

    return pl.pallas_call(body, name=name, out_shape=jax.ShapeDtypeStruct(logits.shape, F32))(dlb, logits)


def _silu_grad(z):
    sg = _sigmoid(z)
    return sg * (1.0 + z * (1.0 - sg))


def _head_norm_gate(o, zg, gn):
    outs = []
    for h in range(HGRN_HEADS):
        sl = slice(h * LANES, (h + 1) * LANES)
        zg_h = zg[:, sl]
        outs.append(_rms(o[:, sl], gn) * (zg_h * _sigmoid(zg_h)))
    return (jnp.concatenate(outs, axis=1),)


def _head_norm_gate_bwd(o, zg, dm, gn):
    do_parts, dzg_parts, dgn = [], [], jnp.zeros((1, LANES), F32)
    for h in range(HGRN_HEADS):
        sl = slice(h * LANES, (h + 1) * LANES)
        o_h, zg_h, dm_h = o[:, sl], zg[:, sl], dm[:, sl]
        gate = zg_h * _sigmoid(zg_h)
        do_h, dgn_h = _rms_bwd(o_h, gn, dm_h * gate)
        dgn = dgn + dgn_h
        do_parts.append(do_h)
        dzg_parts.append(dm_h * _rms(o_h, gn) * _silu_grad(zg_h))
    return jnp.concatenate(do_parts, axis=1), jnp.concatenate(dzg_parts, axis=1), dgn


def _rope_slabs(x, t_c, t_s1, t_s2, transpose):
    fn = _rope_t if transpose else _rope
    return jnp.concatenate(
        [fn(x[:, h * LANES:(h + 1) * LANES], t_c, t_s1, t_s2) for h in range(x.shape[1] // LANES)], axis=1)


def _loss_head(h, tgt, w):
    d = h.shape[1]
    r = lax.rsqrt(jnp.mean(h * h, axis=-1, keepdims=True) + EPS)
    xh = h * r
    err = xh * w - tgt
    loss = 0.5 * jnp.sum(jnp.mean(err * err, axis=-1, keepdims=True), axis=0, keepdims=True)
    dy = err / d
    dxh = dy * w
    dh = r * (dxh - xh * jnp.mean(dxh * xh, axis=-1, keepdims=True))
    return dh, dh, jnp.sum(dy * xh, axis=0, keepdims=True), jnp.broadcast_to(loss, (1, LANES))


def _mlp_fwd(h, norm, w_up, w_down, tag, loss_head=None):
    d = h.shape[1]

    def up(x, g, wu):
        x_n = _rms(x, g).astype(BF16)
        return x_n, jnp.concatenate([jnp.square(jnp.maximum(_dot(x_n, wu[j], _NN), 0.0)) for j in range(wu.shape[0])],
                                    axis=1)

    xn, act = _rowcall(up, [h], [norm, w_up], [(d, BF16), (w_up.shape[0] * w_up.shape[2], BF16)], [], tr=512,
                       name=f"{tag}_up")
    if callable(w_down):
        w_down = w_down(act)
    if loss_head is None:
        return _mm(act, w_down, mode="nn", add=h, name=f"{tag}_down"), (h, xn, act)
    tgt, final_norm = loss_head

    def down_and_loss(a, res, t, wd, g):
        return _loss_head(res + _dot(a, wd, _NN), t, g)

    return _rowcall(down_and_loss, [act, h, tgt], [w_down, final_norm], [(d, F32), (d, BF16)], [d, LANES],
                    name=f"{tag}_down_loss"), (h, xn, act)


def _mlp_bwd(dh_out, dh_out_bf, saved, norm, w_up, w_down, tag, after=None):
    h, xn, act = saved
    d = h.shape[1]
    du = _mm(dh_out_bf, w_down, mode="nt", relu2_of=act, out_dtype=BF16, after=after, name=f"{tag}_bwd_du")
    dw_down = _mm(act, dh_out_bf, mode="tn", name=f"{tag}_bwd_wdown")
    dw_up = _mm(xn, du, mode="tn", col_shards=w_up.shape[0], name=f"{tag}_bwd_wup")

    def up_norm_bwd(x, d_u, dres, g, wu):
        cols = wu.shape[2]
        dxn = _dot(d_u[:, :cols], wu[0], _NT)
        for j in range(1, wu.shape[0]):
            dxn = dxn + _dot(d_u[:, j * cols:(j + 1) * cols], wu[j], _NT)
        dx, dw = _rms_bwd(x, g, dxn)
        return dx + dres, dx + dres, dw

    dh, dh_bf, dnorm = _rowcall(up_norm_bwd, [h, du, dh_out], [norm, w_up], [(d, F32), (d, BF16)], [d], tr=512,
                                name=f"{tag}_bwd_dxn")
    return dh, dh_bf, dnorm, dw_up, dw_down


def _row_major(g):
    return g.reshape(g.shape[0] * g.shape[1], g.shape[2])


def _col_major(g):
    return jnp.transpose(g, (1, 0, 2)).reshape(g.shape[1], g.shape[0] * g.shape[2])


def _col_terms(dw):
    k, n = dw.shape
    return jnp.transpose(dw.reshape(k, N_DEV, n // N_DEV), (1, 0, 2))


def _row_terms(dw):
    return dw.reshape(N_DEV, dw.shape[0] // N_DEV, dw.shape[1])


def kernel(x, hgrn_norm, hgrn_w_q, hgrn_w_f, hgrn_w_i, hgrn_w_g, hgrn_g_norm, hgrn_w_o, hgrn_lb_logits, mla_norm, mla_w_dq, mla_q_norm, mla_w_uq, mla_w_o, kv_in_norm, kv_w_dkv, kv_norm, kv_w_uk, kv_w_uv, mlp_norm, mlp_w_up, mlp_w_down, final_norm, loss_target, m_hgrn_norm, m_hgrn_w_q, m_hgrn_w_f, m_hgrn_w_i, m_hgrn_w_g, m_hgrn_g_norm, m_hgrn_w_o, m_hgrn_lb_logits, m_mla_norm, m_mla_w_dq, m_mla_q_norm, m_mla_w_uq, m_mla_w_o, m_kv_in_norm, m_kv_w_dkv, m_kv_norm, m_kv_w_uk, m_kv_w_uv, m_mlp_norm, m_mlp_w_up, m_mlp_w_down, m_final_norm, v_hgrn_norm, v_hgrn_w_q, v_hgrn_w_f, v_hgrn_w_i, v_hgrn_w_g, v_hgrn_g_norm, v_hgrn_w_o, v_hgrn_lb_logits, v_mla_norm, v_mla_w_dq, v_mla_q_norm, v_mla_w_uq, v_mla_w_o, v_kv_in_norm, v_kv_w_dkv, v_kv_norm, v_kv_w_uk, v_kv_w_uv, v_mlp_norm, v_mlp_w_up, v_mlp_w_down, v_final_norm):
    given = dict(locals())
    weight_names = ["hgrn_norm", "hgrn_w_q", "hgrn_w_f", "hgrn_w_i", "hgrn_w_g", "hgrn_g_norm", "hgrn_w_o",
                    "hgrn_lb_logits", "mla_norm", "mla_w_dq", "mla_q_norm", "mla_w_uq", "mla_w_o", "kv_in_norm",
                    "kv_w_dkv", "kv_norm", "kv_w_uk", "kv_w_uv", "mlp_norm", "mlp_w_up", "mlp_w_down", "final_norm"]
    me = 4 * lax.axis_index("x") + 2 * lax.axis_index("y") + lax.axis_index("c")
    xs, tgt = x[0], loss_target[0]
    seq, d_model = xs.shape
    n_heads, hd = MLA_HEADS, LANES

    big_local = {
        "hgrn_w_q": hgrn_w_q[0], "hgrn_w_f": hgrn_w_f[0], "hgrn_w_i": hgrn_w_i[0], "hgrn_w_g": hgrn_w_g[0],
        "hgrn_w_o": hgrn_w_o[0], "mla_w_dq": mla_w_dq[0], "mla_w_uq": mla_w_uq[0], "mla_w_o": mla_w_o[0],
        "kv_w_dkv": kv_w_dkv, "kv_w_uk": kv_w_uk, "kv_w_uv": kv_w_uv,
        "mlp_w_up0": mlp_w_up[0], "mlp_w_up1": mlp_w_up[1], "mlp_w_down0": mlp_w_down[0], "mlp_w_down1": mlp_w_down[1],
    }
    big_names = list(big_local)
    col_sharded = {"mla_w_uq", "kv_w_uk", "kv_w_uv"}
    shard_major = {"mlp_w_up0", "mlp_w_up1"}
    vec_local = jnp.concatenate([hgrn_norm, hgrn_lb_logits], axis=0)
    first_names = ["hgrn_w_q", "hgrn_w_f", "hgrn_w_i"]
    proj_names = first_names + ["hgrn_w_g"]
    later_names = {"hgrn_o": ["hgrn_w_g", "hgrn_w_o"], "up0": ["mlp_w_up0"], "down0": ["mlp_w_down0"],
                   "mla": ["kv_w_dkv", "kv_w_uk", "kv_w_uv", "mla_w_dq", "mla_w_uq", "mla_w_o"],
                   "mlp1": ["mlp_w_up1", "mlp_w_down1"]}

    def unshard(names, arrays):
        return {k: (a if k in shard_major else _col_major(a) if k in col_sharded else _row_major(a))
                for k, a in zip(names, arrays)}

    first_state, token = _exchange_start([big_local[k].astype(BF16) for k in first_names] + [vec_local], scatter=False,
                                         name="gather_first_start")
    gather_state = {}
    for tag, names in later_names.items():
        gather_state[tag], token = _exchange_start([big_local[k].astype(BF16) for k in names], scatter=False,
                                                   after=token, name=f"gather_{tag}_start")

    def gather_wait(tag, after):
        w.update(unshard(later_names[tag], _exchange_wait(gather_state[tag], after, name=f"gather_{tag}_wait")))
        return [w[k] for k in later_names[tag]]

    gathered = _exchange_wait(first_state, token, name="gather_first_wait")
    w = unshard(first_names, gathered[:-1])
    vec_full = jnp.transpose(gathered[-1], (1, 0, 2)).reshape(3, d_model)
    hgrn_norm_full, lb_logits_full = vec_full[0:1], vec_full[1:3]
    t_c, t_s1, t_s2 = _rope_tables(seq)
    kv_lora = kv_w_uk.shape[0]

    def hgrn_proj(a, g, *weights):
        xn = _rms(a, g).astype(BF16)
        return (xn, *[_dot(xn, wt, _NN) for wt in weights])

    xn0, zq, zf, zi = _rowcall(hgrn_proj, [xs], [hgrn_norm_full] + [w[k] for k in first_names],
                               [(d_model, BF16)] + [(d_model, F32)] * 3, [], tr=512, name="hgrn_proj")
    o_rec, states = _hgrn_fwd(zq, zf, zi, lb_logits_full, name="hgrn_fwd")
    gather_wait("hgrn_o", o_rec)

    def gate_out(o, x_n, res, gn, wg, wo):
        z = _dot(x_n, wg, _NN)
        m = _head_norm_gate(o, z, gn)[0].astype(BF16)
        return z, m, res + _dot(m, wo, _NN)

    zg, mixed, h1 = _rowcall(gate_out, [o_rec, xn0, xs], [hgrn_g_norm, w["hgrn_w_g"], w["hgrn_w_o"]],
                             [(d_model, F32), (d_model, BF16), (d_model, F32)], [], name="hgrn_gate_out")
    h2, mlp0_saved = _mlp_fwd(h1, mlp_norm[0:1], gather_wait("up0", h1)[0], lambda act: gather_wait("down0", act)[0],
                              "mlp0")
    gather_wait("mla", h2)
    w_uq3 = w["mla_w_uq"].reshape(-1, n_heads, MLA_NOPE + MLA_ROPE)
    w_uq_nope = w_uq3[:, :, :MLA_NOPE].reshape(-1, n_heads * hd)
    w_uq_rope = jnp.pad(w_uq3[:, :, MLA_NOPE:], ((0, 0), (0, 0), (0, hd - MLA_ROPE))).reshape(-1, n_heads * hd)
    w_dkv_pad = jnp.pad(w["kv_w_dkv"], ((0, 0), (0, kv_lora + hd - w["kv_w_dkv"].shape[1])))

    q_lora, qk_cols = w["mla_w_dq"].shape[1], n_heads * hd

    def mla_qkv(a, tc, ts1, ts2, g_kv_in, g_mla, g_q, g_kv, wdq, wn, wr, wdkv, wuk, wuv):
        h_n, x_n = _rms(a, g_kv_in).astype(BF16), _rms(a, g_mla).astype(BF16)
        cq = _dot(x_n, wdq, _NN)
        cq_n = _rms(cq, g_q).astype(BF16)
        q_nope = _dot(cq_n, wn, _NN) * Q_PRESCALE
        q_rope = _rope_slabs(_dot(cq_n, wr, _NN) * Q_PRESCALE, tc, ts1, ts2, False)
        c_all = _dot(h_n, wdkv, _NN)
        lat = _rms(c_all[:, :kv_lora], g_kv).astype(BF16)
        return (h_n, x_n, cq, cq_n, q_nope, q_rope, c_all, lat, _rope(c_all[:, kv_lora:], tc, ts1, ts2),
                _dot(lat, wuk, _NN), _dot(lat, wuv, _NN))

    hn, xn2, cq_pre, c_q, qn, qr, ckr, c_kv, kr, kn, vv = _rowcall(
        mla_qkv, [h2, t_c, t_s1, t_s2],
        [kv_in_norm[None, :], mla_norm, mla_q_norm, kv_norm[None, :], w["mla_w_dq"], w_uq_nope, w_uq_rope, w_dkv_pad,
         w["kv_w_uk"], w["kv_w_uv"]],
        [(d_model, BF16), (d_model, BF16), (q_lora, F32), (q_lora, BF16), (qk_cols, BF16), (qk_cols, BF16),
         (kv_lora + hd, F32), (kv_lora, BF16), (hd, BF16), (qk_cols, BF16), (qk_cols, BF16)], [], tr=512, name="mla_qkv")
    o_att, lse = _attn_fwd(qn, qr, kn, kr, vv, name="attn_fwd")
    h3 = _mm(o_att, w["mla_w_o"], mode="nn", add=h2, name="attn_out")
    gather_wait("mlp1", h3)
    (dh4, dh4_bf, g_final_norm, loss_part), mlp1_saved = _mlp_fwd(
        h3, mlp_norm[1:2], w["mlp_w_up1"], w["mlp_w_down1"], "mlp1", loss_head=(tgt, final_norm[None, :]))

    g = {}
    groups = {"mlp1": ["mlp_w_up1", "mlp_w_down1"],
              "mla": ["mla_w_o", "mla_w_uq", "mla_w_dq", "kv_w_uk", "kv_w_uv", "kv_w_dkv"],
              "mlp0": ["mlp_w_up0", "mlp_w_down0"],
              "hgrn_out": ["hgrn_w_o", "hgrn_w_g"],
              "hgrn_in": ["hgrn_w_q", "hgrn_w_f", "hgrn_w_i"]}
    scatter_state = {}

    def scatter_start(tag, after=None):
        scatter_state[tag], tok = _exchange_start(
            [g[k] if k in shard_major else (_col_terms if k in col_sharded else _row_terms)(g[k]) for k in groups[tag]],
            scatter=True, after=after,
            name=f"scatter_{tag}_start")
        return tok

    dh3, dh3_bf, g_mlp_norm1, g["mlp_w_up1"], g["mlp_w_down1"] = _mlp_bwd(
        dh4, dh4_bf, mlp1_saved, mlp_norm[1:2], w["mlp_w_up1"], w["mlp_w_down1"], "mlp1")
    def attn_out_bwd(dres, o, wo):
        d_o = _dot(dres, wo, _NT).astype(BF16)
        prod = d_o.astype(F32) * o.astype(F32)
        return d_o, jnp.concatenate([jnp.broadcast_to(jnp.sum(prod[:, h * hd:(h + 1) * hd], axis=1, keepdims=True),
                                                      (prod.shape[0], hd)) for h in range(n_heads)], axis=1)

    d_oatt, delta = _rowcall(attn_out_bwd, [dh3_bf, o_att], [w["mla_w_o"]], [(qk_cols, BF16), (qk_cols, F32)], [],
                             after=scatter_start("mlp1"), name="attn_out_bwd_x")
    g["mla_w_o"] = _mm(o_att, dh3_bf, mode="tn", name="attn_out_bwd_w")
    dqn, dqr, dkn, dvv, dkr = _attn_bwd(qn, qr, kn, kr, vv, d_oatt, lse, delta, name="attn_bwd")

    def q_path_bwd(cq, cq_n, x_n, d_qn, d_qr, tc, ts1, ts2, g_q, wdq, wn, wr):
        d_qn, d_qr = d_qn.astype(BF16), _rope_slabs(d_qr, tc, ts1, ts2, True).astype(BF16)
        d_cq, d_gq = _rms_bwd(cq, g_q, _dot(d_qn, wn, _NT) + _dot(d_qr, wr, _NT))
        d_cq = d_cq.astype(BF16)
        return _dot(d_cq, wdq, _NT), d_gq, _dot(x_n, d_cq, _TN), _dot(cq_n, d_qn, _TN), _dot(cq_n, d_qr, _TN)

    dxn2, g_q_norm, g_dq, g_uq_nope, g_uq_rope = _rowcall(
        q_path_bwd, [cq_pre, c_q, xn2, dqn, dqr, t_c, t_s1, t_s2], [mla_q_norm, w["mla_w_dq"], w_uq_nope, w_uq_rope],
        [(d_model, F32)], [q_lora, (d_model, q_lora), (q_lora, qk_cols), (q_lora, qk_cols)], tr=512, name="mla_q_bwd")
    g["mla_w_dq"] = g_dq.astype(GRAD_WIRE_DTYPE)
    g["mla_w_uq"] = jnp.concatenate([g_uq_nope.reshape(q_lora, n_heads, hd),
                                     g_uq_rope.reshape(q_lora, n_heads, hd)[:, :, :MLA_ROPE]],
                                    axis=2).reshape(q_lora, -1).astype(GRAD_WIRE_DTYPE)

    def kv_path_bwd(c_all, lat, h_n, d_kn, d_v, d_kr_heads, tc, ts1, ts2, a, d_xn2, dres,
                    g_kv, g_kv_in, g_mla, wdkv, wuk, wuv):
        d_lat, d_gkv = _rms_bwd(c_all[:, :kv_lora], g_kv, _dot(d_kn, wuk, _NT) + _dot(d_v, wuv, _NT))
        d_kr = d_kr_heads[:, :hd]
        for h in range(1, n_heads):
            d_kr = d_kr + d_kr_heads[:, h * hd:(h + 1) * hd]
        d_all = jnp.concatenate([d_lat, _rope_t(d_kr, tc, ts1, ts2)], axis=1).astype(BF16)
        dx1, d_gkv_in = _rms_bwd(a, g_kv_in, _dot(d_all, wdkv, _NT))
        dx2, d_gmla = _rms_bwd(a, g_mla, d_xn2)
        d_a = dx1 + dx2 + dres
        return (d_a, d_a, d_gkv, d_gkv_in, d_gmla, _dot(h_n, d_all, _TN), _dot(lat, d_kn, _TN), _dot(lat, d_v, _TN))

    dh2, dh2_bf, g_kv_norm, g_kv_in_norm, g_mla_norm, g_dkv, g_uk, g_uv = _rowcall(
        kv_path_bwd, [ckr, c_kv, hn, dkn, dvv, dkr, t_c, t_s1, t_s2, h2, dxn2, dh3],
        [kv_norm[None, :], kv_in_norm[None, :], mla_norm, w_dkv_pad, w["kv_w_uk"], w["kv_w_uv"]],
        [(d_model, F32), (d_model, BF16)],
        [kv_lora, d_model, d_model, (d_model, kv_lora + hd), (kv_lora, qk_cols), (kv_lora, qk_cols)], name="mla_kv_bwd")
    g["kv_w_dkv"] = g_dkv[:, :kv_w_dkv.shape[1]].astype(GRAD_WIRE_DTYPE)
    g["kv_w_uk"], g["kv_w_uv"] = g_uk.astype(GRAD_WIRE_DTYPE), g_uv.astype(GRAD_WIRE_DTYPE)
    dh1, dh1_bf, g_mlp_norm0, g["mlp_w_up0"], g["mlp_w_down0"] = _mlp_bwd(
        dh2, dh2_bf, mlp0_saved, mlp_norm[0:1], w["mlp_w_up0"], w["mlp_w_down0"], "mlp0", after=scatter_start("mla"))

    g["hgrn_w_o"] = _mm(mixed, dh1_bf, mode="tn", after=scatter_start("mlp0"), name="hgrn_out_bwd_w")
    do_rec, dzg, g_g_norm = _rowcall(
        lambda dres, o, z, wo, gn: _head_norm_gate_bwd(o, z, _dot(dres, wo, _NT), gn), [dh1_bf, o_rec, zg],
        [w["hgrn_w_o"], hgrn_g_norm], [(d_model, F32), (d_model, BF16)], [hd], name="hgrn_gate_out_bwd")
    g["hgrn_w_g"] = _mm(xn0, dzg, mode="tn", name="hgrn_w_g_bwd_w")
    dzq, dzf, dzi, g_lb = _hgrn_bwd(zq, zf, zi, lb_logits_full, states, do_rec, scatter_start("hgrn_out"),
                                    name="hgrn_bwd")
    for nm, dz in (("hgrn_w_q", dzq), ("hgrn_w_f", dzf), ("hgrn_w_i", dzi)):
        g[nm] = _mm(xn0, dz, mode="tn", name=f"{nm}_bwd_w")

    def hgrn_proj_bwd(a, dres, *rest):
        dzs, gw, weights = rest[:4], rest[4], rest[5:]
        dxn = _dot(dzs[0], weights[0], _NT)
        for dz, wt in zip(dzs[1:], weights[1:]):
            dxn = dxn + _dot(dz, wt, _NT)
        dx, dw = _rms_bwd(a, gw, dxn)
        return dx + dres, dw

    grad_x, g_hgrn_norm = _rowcall(hgrn_proj_bwd, [xs, dh1, dzq, dzf, dzi, dzg],
                                   [hgrn_norm_full] + [w[k] for k in proj_names], [(d_model, F32)], [d_model],
                                   tr=512, name="hgrn_proj_bwd")

    small_parts = [g_hgrn_norm, g_lb, g_g_norm, g_mla_norm, g_q_norm, g_kv_in_norm, g_kv_norm, g_mlp_norm0,
                   g_mlp_norm1, g_final_norm, loss_part]
    small_sizes = [p.shape[1] for p in small_parts]
    small_terms = _exchange([jnp.concatenate(small_parts, axis=1)], scatter=False, name="gather_small")[0]
    small_sum = _sum_terms(small_terms, name="sum_small")
    last = scatter_start("hgrn_in", after=small_sum)
    offs = [0]
    for sz in small_sizes:
        offs.append(offs[-1] + sz)
    (s_hgrn_norm, s_lb, s_g_norm, s_mla_norm, s_q_norm, s_kv_in_norm, s_kv_norm, s_mlp_norm0, s_mlp_norm1, s_final_norm,
     s_loss) = [small_sum[:, a:b] for a, b in zip(offs[:-1], offs[1:])]
    shard = hgrn_norm.shape[1]
    g_lb_logits = _lb_logits_grad(lax.dynamic_slice_in_dim(s_lb, me * shard, shard, axis=1), hgrn_lb_logits,
                                  name="lb_logits_grad")
    loss = s_loss[0, 0]

    res = {}
    for tag, names in groups.items():
        for k, t in zip(names, _exchange_wait(scatter_state[tag], last, name=f"scatter_{tag}_wait")):
            if k.startswith("mlp_w_"):
                base, layer = k[:-1], int(k[-1])
                wk, mk, vk = given[base][layer], given["m_" + base][layer], given["v_" + base][layer]
            else:
                wk, mk, vk = given[k], given["m_" + k], given["v_" + k]
            shape = wk.shape
            wk, mk, vk = (a.reshape(shape[-2], shape[-1]) for a in (wk, mk, vk))
            upd = _adam(wk, t, mk, vk, name=f"adam_{k}")
            last = upd[0]
            res[k] = [o.reshape(shape) for o in upd]
    for base in ("mlp_w_up", "mlp_w_down"):
        res[base] = [jnp.stack([res[base + "0"][i], res[base + "1"][i]], axis=0) for i in range(4)]

    small_grads = {
        "hgrn_norm": lax.dynamic_slice_in_dim(s_hgrn_norm, me * shard, shard, axis=1),
        "hgrn_g_norm": s_g_norm, "hgrn_lb_logits": g_lb_logits, "mla_norm": s_mla_norm, "mla_q_norm": s_q_norm,
        "kv_in_norm": s_kv_in_norm, "kv_norm": s_kv_norm,
        "mlp_norm": jnp.concatenate([s_mlp_norm0, s_mlp_norm1], axis=0), "final_norm": s_final_norm,
    }
    small_names = list(small_grads)

    def flat(a):
        return a.reshape(1, -1)

    packed = [jnp.concatenate([flat(src[pre + k]) for k in small_names], axis=1)
              for src, pre in ((given, ""), (small_grads, ""), (given, "m_"), (given, "v_"))]
    small_out = _adam(packed[0], packed[1][None], packed[2], packed[3], name="adam_small")
    off = 0
    for k in small_names:
        size = given[k].size
        res[k] = [o[:, off:off + size].reshape(given[k].shape) for o in small_out]
        off += size

    outs = [loss, grad_x[None]]
    for i in range(4):
        outs += [res[k][i] for k in weight_names]
    return tuple(outs)
```

```python
import functools

import jax
import jax.numpy as jnp
from jax import lax
from jax.experimental import pallas as pl
from jax.experimental.pallas import tpu as pltpu

F32 = jnp.float32
BF16 = jnp.bfloat16

EPS = 1e-6
LANES = 128
N_DEV = 8
V7X_VMEM_LIMIT_BYTES = 56 << 20
MM_PIPELINE_BYTES = 30 << 20
MM_ROW_TILE = 512
GRAD_WIRE_DTYPE = BF16

HGRN_HEADS = 8
HGRN_CHUNK = 64
HGRN_SUB = 16
HGRN_HEADS_PER_STEP = 8
EXP_CLAMP = 80.0
MLA_HEADS = 16
MLA_NOPE = 128
MLA_ROPE = 64
ROPE_THETA = 10000.0
ATTN_SCALE = (MLA_NOPE + MLA_ROPE) ** -0.5

ADAM_LR = 0.001
ADAM_B1 = 0.9
ADAM_B2 = 0.999
ADAM_EPS = 1e-08
ADAM_WD = 0.01
ADAM_STEP = 10

_NN = ((1,), (0,))
_NT = ((1,), (1,))
_TN = ((0,), (0,))


def _params(*sem):
    return pltpu.CompilerParams(dimension_semantics=sem, vmem_limit_bytes=V7X_VMEM_LIMIT_BYTES)


def _dot(a, b, dims):
    return lax.dot_general(a.astype(BF16), b.astype(BF16), (dims, ((), ())), preferred_element_type=F32)


def _dot_f32(a, b, dims=_NN):
    return lax.dot_general(a, b, (dims, ((), ())), precision=lax.Precision.HIGH, preferred_element_type=F32)


def _sigmoid(x):
    return 1.0 / (1.0 + jnp.exp(-x))


def _rms(x, w):
    r = lax.rsqrt(jnp.mean(x * x, axis=-1, keepdims=True) + EPS)
    return x * r * w


def _rms_bwd(x, w, dy):
    r = lax.rsqrt(jnp.mean(x * x, axis=-1, keepdims=True) + EPS)
    xh = x * r
    dw = jnp.sum(dy * xh, axis=0, keepdims=True)
    dxh = dy * w
    dx = r * (dxh - xh * jnp.mean(dxh * xh, axis=-1, keepdims=True))
    return dx, dw


def _mm_tiles(m, n, k, a_bytes, b_bytes, out_tile_bytes):
    tm = min(m, MM_ROW_TILE)
    for tn in (n, 2048, 1024, 512, 256, LANES):
        if tn <= n and n % tn == 0:
            if 2 * (tm * k * a_bytes + k * tn * b_bytes + tm * tn * out_tile_bytes) <= MM_PIPELINE_BYTES:
                return tm, tn
    return tm, min(n, LANES)


def _mm(a, b, *, mode, name, out_dtype=None, add=None, relu2_of=None, after=None, col_shards=None):
    if mode == "nn":
        (m, k), (k2, n) = a.shape, b.shape
    elif mode == "nt":
        (m, k), (n, k2) = a.shape, b.shape
    else:
        (k, m), (k2, n) = a.shape, b.shape
    assert k == k2, (name, a.shape, b.shape)
    if out_dtype is None:
        out_dtype = GRAD_WIRE_DTYPE if mode == "tn" else F32
    tile_bytes = sum(x.dtype.itemsize for x in (add, relu2_of) if x is not None) + jnp.dtype(out_dtype).itemsize
    tm, tn = _mm_tiles(m, n, k, a.dtype.itemsize, b.dtype.itemsize, tile_bytes)
    if col_shards is not None:
        assert add is None and relu2_of is None
        tn = n // col_shards
    assert m % tm == 0 and n % tn == 0, (name, m, n)
    dims = {"nn": _NN, "nt": _NT, "tn": _TN}[mode]
    a_spec = pl.BlockSpec((k, tm), lambda i, j: (0, i)) if mode == "tn" else pl.BlockSpec((tm, k), lambda i, j: (i, 0))
    b_spec = pl.BlockSpec((tn, k), lambda i, j: (j, 0)) if mode == "nt" else pl.BlockSpec((k, tn), lambda i, j: (0, j))
    o_spec = pl.BlockSpec((tm, tn), lambda i, j: (i, j))
    operands, in_specs = [a, b], [a_spec, b_spec]
    for extra in (add, relu2_of):
        if extra is not None:
            assert extra.shape == (m, n), (name, extra.shape)
            operands.append(extra)
            in_specs.append(o_spec)
    n_in = len(operands)
    if after is not None:
        operands.append(after)
        in_specs.append(pl.BlockSpec(memory_space=pl.ANY))
    out_shape = jax.ShapeDtypeStruct((m, n), out_dtype)
    if col_shards is not None:
        out_shape = jax.ShapeDtypeStruct((col_shards, m, tn), out_dtype)
        o_spec = pl.BlockSpec((None, tm, tn), lambda i, j: (j, i, 0))

    def body(*refs):
        acc = _dot(refs[0][...], refs[1][...], dims)
        extras, outs = refs[2:n_in], refs[len(operands):]
        if add is not None:
            acc = acc + extras[0][...]
        if relu2_of is not None:
            acc = acc * (2.0 * jnp.sqrt(extras[-1][...].astype(F32)))
        outs[0][...] = acc.astype(out_dtype)

    return pl.pallas_call(
        body, name=name, grid=(m // tm, n // tn), in_specs=in_specs, out_specs=o_spec, out_shape=out_shape,
        compiler_params=_params("parallel", "parallel"),
    )(*operands)


def _rowcall(fn, rows, consts, outs, accs, *, name, tr=256, after=None):
    s = rows[0].shape[0]
    tr = min(tr, s)
    assert s % tr == 0
    n_out = len(outs)
    accs = [(1, a) if isinstance(a, int) else a for a in accs]
    in_specs = [pl.BlockSpec((tr, r.shape[1]), lambda i: (i, 0)) for r in rows]
    in_specs += [pl.BlockSpec(c.shape, lambda i, nd=c.ndim: (0,) * nd) for c in consts]
    out_shape = [jax.ShapeDtypeStruct((s, w), dt) for w, dt in outs] + [jax.ShapeDtypeStruct(a, F32) for a in accs]
    out_specs = [pl.BlockSpec((tr, w), lambda i: (i, 0)) for w, _ in outs] + [pl.BlockSpec(a, lambda i: (0, 0)) for a in accs]
    n_in = len(rows) + len(consts)

    def body(*refs):
        res = fn(*[r[...] for r in refs[:n_in]])
        out_refs = refs[n_in + (after is not None):]
        for ref, val in zip(out_refs[:n_out], res[:n_out]):
            ref[...] = val.astype(ref.dtype)
        i = pl.program_id(0)
        for ref, val in zip(out_refs[n_out:], res[n_out:]):
            @pl.when(i == 0)
            def _(ref=ref, val=val):
                ref[...] = val

            @pl.when(i > 0)
            def _(ref=ref, val=val):
                ref[...] += val

    behind = [] if after is None else [after]
    return pl.pallas_call(
        body, name=name, grid=(s // tr,), in_specs=in_specs + [pl.BlockSpec(memory_space=pl.ANY)] * len(behind),
        out_specs=out_specs, out_shape=out_shape, compiler_params=_params("arbitrary" if accs else "parallel"),
    )(*rows, *consts, *behind)


def _rope_tables(seq):
    half = MLA_ROPE // 2
    inv_freq = ROPE_THETA ** (-jnp.arange(half, dtype=F32) / half)
    ang = jnp.arange(seq, dtype=F32)[:, None] * inv_freq[None, :]
    cos, sin, zero = jnp.cos(ang), jnp.sin(ang), jnp.zeros((seq, half), F32)
    t_c = jnp.concatenate([cos, cos, zero, zero], axis=1)
    t_s1 = jnp.concatenate([-sin, zero, zero, zero], axis=1)
    t_s2 = jnp.concatenate([zero, sin, zero, zero], axis=1)
    return t_c, t_s1, t_s2


def _rope(slab, t_c, t_s1, t_s2):
    return slab * t_c + pltpu.roll(slab, 96, 1) * t_s1 + pltpu.roll(slab, 32, 1) * t_s2


def _rope_t(d, t_c, t_s1, t_s2):
    return d * t_c + pltpu.roll(d * t_s1, 32, 1) + pltpu.roll(d * t_s2, 96, 1)


def _lower_bound(logits):
    l0, l1 = logits[0:1, :], logits[1:2, :]
    mx = jnp.maximum(l0, l1)
    e0, e1 = jnp.exp(l0 - mx), jnp.exp(l1 - mx)
    return e0 / (e0 + e1)


def _tri(n, lower):
    row = lax.broadcasted_iota(jnp.int32, (n, n), 0)
    col = lax.broadcasted_iota(jnp.int32, (n, n), 1)
    return (row >= col) if lower else (row <= col)


def _hgrn_fwd(zq, zf, zi, lb_logits, *, name):
    s, d = zq.shape
    h_n, c, hp = d // LANES, HGRN_CHUNK, HGRN_HEADS_PER_STEP
    nc = s // c

    def body(zq_ref, zf_ref, zi_ref, lb_ref, o_ref, st_ref, state_sc, b_sc):
        @pl.when(pl.program_id(1) == 0)
        def _():
            state_sc[...] = jnp.zeros_like(state_sc)

        lower = _tri(c, True).astype(F32)
        hs = range(hp)
        sls = [slice(hh * LANES, (hh + 1) * LANES) for hh in hs]
        lb = [_lower_bound(lb_ref[:, sl]) for sl in sls]
        zq_v = [zq_ref[:, sl] for sl in sls]
        q = [z * _sigmoid(z) for z in zq_v]
        f = [lb[hh] + (1.0 - lb[hh]) * _sigmoid(zf_ref[:, sls[hh]]) for hh in hs]
        g = [jnp.log(x) for x in f]
        k = [1.0 - x for x in f]
        v = [zi_ref[:, sl] for sl in sls]
        b = [_dot_f32(lower, x) for x in g]
        s0t = [state_sc[hh] for hh in hs]
        for hh in hs:
            st_ref[hh] = s0t[hh]
            b_sc[hh] = b[hh]
        o_inter = [_dot(q[hh] * jnp.exp(b[hh]), s0t[hh], _NT) for hh in hs]
        scores = [[] for _ in hs]
        for i in range(c // HGRN_SUB):
            lo = i * HGRN_SUB
            for hh in hs:
                ref = b_sc[hh, lo - 1:lo, :] if i > 0 else jnp.zeros((1, LANES), F32)
                qt = q[hh][lo:lo + HGRN_SUB, :] * jnp.exp(b[hh][lo:lo + HGRN_SUB, :] - ref)
                dec = jnp.exp(jnp.minimum(ref - b[hh], EXP_CLAMP))
                scores[hh].append(_dot(qt, k[hh] * dec, _NT))
        a = [jnp.where(_tri(c, True), jnp.concatenate(sc, axis=0), 0.0) for sc in scores]
        for hh in hs:
            o_ref[:, sls[hh]] = o_inter[hh] + _dot(a[hh], v[hh], _NN)
        bl = [b_sc[hh, c - 1:c, :] for hh in hs]
        for hh in hs:
            state_sc[hh] = s0t[hh] * jnp.exp(bl[hh]) + _dot(v[hh], k[hh] * jnp.exp(bl[hh] - b[hh]), _TN)

    tile = pl.BlockSpec((c, hp * LANES), lambda h, i: (i, h))
    return pl.pallas_call(
        body, name=name, grid=(h_n // hp, nc),
        in_specs=[tile, tile, tile, pl.BlockSpec((2, hp * LANES), lambda h, i: (0, h))],
        out_specs=[tile, pl.BlockSpec((hp, None, LANES, LANES), lambda h, i: (h, i, 0, 0))],
        out_shape=[jax.ShapeDtypeStruct((s, d), F32), jax.ShapeDtypeStruct((h_n, nc, LANES, LANES), F32)],
        scratch_shapes=[pltpu.VMEM((hp, LANES, LANES), F32), pltpu.VMEM((hp, c, LANES), F32)],
        compiler_params=_params("parallel", "arbitrary"),
    )(zq, zf, zi, lb_logits)


def _hgrn_bwd(zq, zf, zi, lb_logits, states, do, after, *, name):
    s, d = zq.shape
    h_n, c, hp = d // LANES, HGRN_CHUNK, HGRN_HEADS_PER_STEP
    nc = s // c

    def body(zq_ref, zf_ref, zi_ref, lb_ref, st_ref, do_ref, _, dzq_ref, dzf_ref, dzi_ref, dlb_ref, dstate_sc, b_sc):
        @pl.when(pl.program_id(1) == 0)
        def _():
            dstate_sc[...] = jnp.zeros_like(dstate_sc)
            dlb_ref[...] = jnp.zeros_like(dlb_ref)

        lower, upper = _tri(c, True), _tri(c, False).astype(F32)
        lower_f = lower.astype(F32)
        last_row = lax.broadcasted_iota(jnp.int32, (c, LANES), 0) == c - 1
        hs = range(hp)
        sls = [slice(hh * LANES, (hh + 1) * LANES) for hh in hs]
        lb = [_lower_bound(lb_ref[:, sl]) for sl in sls]
        zq_v = [zq_ref[:, sl] for sl in sls]
        sq = [_sigmoid(z) for z in zq_v]
        q = [zq_v[hh] * sq[hh] for hh in hs]
        sf = [_sigmoid(zf_ref[:, sl]) for sl in sls]
        f = [lb[hh] + (1.0 - lb[hh]) * sf[hh] for hh in hs]
        g = [jnp.log(x) for x in f]
        k = [1.0 - x for x in f]
        v = [zi_ref[:, sl] for sl in sls]
        d_o = [do_ref[:, sl] for sl in sls]
        b = [_dot_f32(lower_f, x) for x in g]
        s0t = [st_ref[hh] for hh in hs]
        ds1t = [dstate_sc[hh] for hh in hs]
        for hh in hs:
            b_sc[hh] = b[hh]
        bl = [b_sc[hh, c - 1:c, :] for hh in hs]
        eb = [jnp.exp(x) for x in b]
        ebl = [jnp.exp(x) for x in bl]
        dec_end = [jnp.exp(bl[hh] - b[hh]) for hh in hs]
        da = [jnp.where(lower, _dot(d_o[hh], v[hh], _NT), 0.0) for hh in hs]
        dq_inter = [_dot(d_o[hh], s0t[hh], _NN) * eb[hh] for hh in hs]
        dk_state = [_dot(v[hh], ds1t[hh], _NN) * dec_end[hh] for hh in hs]
        dv_state = [_dot(k[hh] * dec_end[hh], ds1t[hh], _NT) for hh in hs]
        for hh in hs:
            dstate_sc[hh] = ds1t[hh] * ebl[hh] + _dot(d_o[hh], q[hh] * eb[hh], _TN)
        dk = list(dk_state)
        scores, dq_blocks = [[] for _ in hs], [[] for _ in hs]
        for i in range(c // HGRN_SUB):
            lo = i * HGRN_SUB
            for hh in hs:
                ref = b_sc[hh, lo - 1:lo, :] if i > 0 else jnp.zeros((1, LANES), F32)
                grow = jnp.exp(b[hh][lo:lo + HGRN_SUB, :] - ref)
                qt = q[hh][lo:lo + HGRN_SUB, :] * grow
                dec = jnp.exp(jnp.minimum(ref - b[hh], EXP_CLAMP))
                kd = k[hh] * dec
                scores[hh].append(_dot(qt, kd, _NT))
                da_i = da[hh][lo:lo + HGRN_SUB, :]
                dq_blocks[hh].append(_dot_f32(da_i, kd, _NN) * grow)
                dk[hh] = dk[hh] + _dot_f32(da_i, qt, _TN) * dec
        a = [jnp.where(lower, jnp.concatenate(sc, axis=0), 0.0) for sc in scores]
        dv = [_dot(a[hh], d_o[hh], _TN) + dv_state[hh] for hh in hs]
        dq = [dq_inter[hh] + jnp.concatenate(dq_blocks[hh], axis=0) for hh in hs]
        db_last = [jnp.sum(k[hh] * dk_state[hh], axis=0, keepdims=True)
                   + ebl[hh] * jnp.sum(s0t[hh] * ds1t[hh], axis=0, keepdims=True) for hh in hs]
        db = [q[hh] * dq[hh] - k[hh] * dk[hh] + jnp.where(last_row, db_last[hh], 0.0) for hh in hs]
        dg = [_dot_f32(upper, x) for x in db]
        df = [dg[hh] / f[hh] - dk[hh] for hh in hs]
        for hh in hs:
            sl = sls[hh]
            dzf_ref[:, sl] = (df[hh] * (1.0 - lb[hh]) * sf[hh] * (1.0 - sf[hh])).astype(BF16)
            dlb_ref[:, sl] += jnp.sum(df[hh] * (1.0 - sf[hh]), axis=0, keepdims=True)
            dzq_ref[:, sl] = (dq[hh] * sq[hh] * (1.0 + zq_v[hh] * (1.0 - sq[hh]))).astype(BF16)
            dzi_ref[:, sl] = dv[hh].astype(BF16)

    tile = pl.BlockSpec((c, hp * LANES), lambda h, i: (nc - 1 - i, h))
    out = jax.ShapeDtypeStruct((s, d), BF16)
    return pl.pallas_call(
        body, name=name, grid=(h_n // hp, nc),
        in_specs=[tile, tile, tile, pl.BlockSpec((2, hp * LANES), lambda h, i: (0, h)),
                  pl.BlockSpec((hp, None, LANES, LANES), lambda h, i: (h, nc - 1 - i, 0, 0)), tile,
                  pl.BlockSpec(memory_space=pl.ANY)],
        out_specs=[tile, tile, tile, pl.BlockSpec((1, hp * LANES), lambda h, i: (0, h))],
        out_shape=[out, out, out, jax.ShapeDtypeStruct((1, d), F32)],
        scratch_shapes=[pltpu.VMEM((hp, LANES, LANES), F32), pltpu.VMEM((hp, c, LANES), F32)],
        compiler_params=_params("parallel", "arbitrary"),
    )(zq, zf, zi, lb_logits, states, do, after)


ATTN_SUB_ROWS = 256
LOG2E = 1.4426950408889634
LN2 = 0.6931471805599453
Q_PRESCALE = ATTN_SCALE * LOG2E


def _attn_tile(s):
    return min(1024, max(128, s // 2))


def _causal_pairs(n, q_major):
    pairs = [(i, j) for i in range(n) for j in range(i + 1)] if q_major else [(i, j) for j in range(n) for i in range(j, n)]
    return jnp.asarray([p[0] for p in pairs], jnp.int32), jnp.asarray([p[1] for p in pairs], jnp.int32)


def _sub_scores(qn_ref, qr_ref, k, r, sub, t, diagonal):
    q = jnp.concatenate([qn_ref[r:r + sub, :], qr_ref[r:r + sub, :]], axis=1)
    if not diagonal:
        return q, _dot(q, k, _NT)
    cols = r + sub
    keep = lax.broadcasted_iota(jnp.int32, (sub, cols), 1) <= r + lax.broadcasted_iota(jnp.int32, (sub, cols), 0)
    return q, jnp.where(keep, _dot(q, k[:cols], _NT), -jnp.inf)


def _attn_fwd(qn, qr, kn, kr, v, *, name):
    s, t = qn.shape[0], _attn_tile(qn.shape[0])
    sub = min(t, ATTN_SUB_ROWS)
    q_blk, k_blk = _causal_pairs(s // t, True)

    def body(qi_ref, kj_ref, qn_ref, qr_ref, kn_ref, kr_ref, v_ref, o_ref, lse_ref, m_sc, l_sc, acc_sc):
        p_id = pl.program_id(1)
        i, j = qi_ref[p_id], kj_ref[p_id]

        @pl.when(j == 0)
        def _():
            m_sc[...] = jnp.full_like(m_sc, -jnp.inf)
            l_sc[...] = jnp.zeros_like(l_sc)
            acc_sc[...] = jnp.zeros_like(acc_sc)

        def update(diagonal):
            k = jnp.concatenate([kn_ref[...], kr_ref[...]], axis=1)
            v = v_ref[...]
            starts = list(range(0, t, sub))
            scs = [_sub_scores(qn_ref, qr_ref, k, r, sub, t, diagonal)[1] for r in starts]
            ps, alphas = [], []
            for r, sc in zip(starts, scs):
                m_prev = m_sc[r:r + sub, :]
                m_new = jnp.maximum(m_prev, jnp.max(sc, axis=1, keepdims=True))
                alpha = jnp.exp2(m_prev - m_new)
                p = jnp.exp2(sc - m_new[:, :1])
                l_sc[r:r + sub, :] = alpha * l_sc[r:r + sub, :] + jnp.sum(p, axis=1, keepdims=True)
                m_sc[r:r + sub, :] = m_new
                ps.append(p)
                alphas.append(alpha)
            for r, p, alpha in zip(starts, ps, alphas):
                acc_sc[r:r + sub, :] = alpha * acc_sc[r:r + sub, :] + _dot(p, v[:p.shape[1]], _NN)

        @pl.when(j < i)
        def _():
            update(False)

        @pl.when(j == i)
        def _():
            update(True)
            o_ref[...] = (acc_sc[...] / l_sc[...]).astype(BF16)
            lse_ref[...] = m_sc[...] + jnp.log(l_sc[...]) * LOG2E

    q_spec = pl.BlockSpec((t, LANES), lambda h, p, qi, kj: (qi[p], h))
    k_spec = pl.BlockSpec((t, LANES), lambda h, p, qi, kj: (kj[p], h))
    kr_spec = pl.BlockSpec((t, LANES), lambda h, p, qi, kj: (kj[p], 0))
    stat = pltpu.VMEM((t, LANES), F32)
    return pl.pallas_call(
        body, name=name,
        grid_spec=pltpu.PrefetchScalarGridSpec(
            num_scalar_prefetch=2, grid=(MLA_HEADS, q_blk.shape[0]),
            in_specs=[q_spec, q_spec, k_spec, kr_spec, k_spec], out_specs=[q_spec, q_spec],
            scratch_shapes=[stat, stat, stat]),
        out_shape=[jax.ShapeDtypeStruct(qn.shape, BF16), jax.ShapeDtypeStruct(qn.shape, F32)],
        compiler_params=_params("parallel", "arbitrary"),
    )(q_blk, k_blk, qn, qr, kn, kr, v)


def _attn_bwd(qn, qr, kn, kr, v, do, lse, delta, *, name):
    s, t = qn.shape[0], _attn_tile(qn.shape[0])
    n, sub = s // t, min(t, ATTN_SUB_ROWS)
    q_blk, k_blk = _causal_pairs(n, False)

    def body(qi_ref, kj_ref, qn_ref, qr_ref, kn_ref, kr_ref, v_ref, do_ref, lse_ref, delta_ref,
             dqn_ref, dqr_ref, dkn_ref, dv_ref, dkr_ref, dk_sc, dv_sc):
        p_id = pl.program_id(1)
        i, j = qi_ref[p_id], kj_ref[p_id]

        @pl.when(p_id == 0)
        def _():
            dqn_ref[...] = jnp.zeros_like(dqn_ref)
            dqr_ref[...] = jnp.zeros_like(dqr_ref)

        @pl.when(i == j)
        def _():
            dk_sc[...] = jnp.zeros_like(dk_sc)
            dv_sc[...] = jnp.zeros_like(dv_sc)

        def accumulate(diagonal):
            k = jnp.concatenate([kn_ref[...], kr_ref[...]], axis=1)
            v = v_ref[...]
            starts = list(range(0, t, sub))
            qs, d_os, scs, dps = [], [], [], []
            for r in starts:
                q, sc = _sub_scores(qn_ref, qr_ref, k, r, sub, t, diagonal)
                d_o = do_ref[r:r + sub, :]
                qs.append(q)
                d_os.append(d_o)
                scs.append(sc)
                dps.append(_dot(d_o, v[:sc.shape[1]], _NT))
            ps, dss = [], []
            for r, sc, dp in zip(starts, scs, dps):
                p = jnp.exp2(sc - lse_ref[r:r + sub, :][:, :1])
                ps.append(p.astype(BF16))
                dss.append((p * (dp - delta_ref[r:r + sub, :][:, :1])).astype(BF16))
            for r, q, d_o, p, ds in zip(starts, qs, d_os, ps, dss):
                cols = p.shape[1]
                dv_sc[:cols, :] += _dot(p, d_o, _TN)
                dk_sc[:cols, :] += _dot(ds, q, _TN)
                dq = _dot(ds, k[:cols], _NN) * ATTN_SCALE
                rows = pl.ds(pl.multiple_of(i * t + r, sub), sub)
                dqn_ref[rows, :] += dq[:, :LANES]
                dqr_ref[rows, :] += dq[:, LANES:]

        @pl.when(j < i)
        def _():
            accumulate(False)

        @pl.when(j == i)
        def _():
            accumulate(True)

        @pl.when(i == n - 1)
        def _():
            dkn_ref[...] = (dk_sc[:, :LANES] * LN2).astype(BF16)
            dkr_ref[...] = dk_sc[:, LANES:] * LN2
            dv_ref[...] = dv_sc[...].astype(BF16)

    q_spec = pl.BlockSpec((t, LANES), lambda h, p, qi, kj: (qi[p], h))
    k_spec = pl.BlockSpec((t, LANES), lambda h, p, qi, kj: (kj[p], h))
    kr_spec = pl.BlockSpec((t, LANES), lambda h, p, qi, kj: (kj[p], 0))
    head_spec = pl.BlockSpec((s, LANES), lambda h, p, qi, kj: (0, h))
    f32_out, bf16_out = jax.ShapeDtypeStruct(qn.shape, F32), jax.ShapeDtypeStruct(qn.shape, BF16)
    return pl.pallas_call(
        body, name=name,
        grid_spec=pltpu.PrefetchScalarGridSpec(
            num_scalar_prefetch=2, grid=(MLA_HEADS, q_blk.shape[0]),
            in_specs=[q_spec, q_spec, k_spec, kr_spec, k_spec, q_spec, q_spec, q_spec],
            out_specs=[head_spec, head_spec, k_spec, k_spec, k_spec],
            scratch_shapes=[pltpu.VMEM((t, 2 * LANES), F32), pltpu.VMEM((t, LANES), F32)]),
        out_shape=[f32_out, f32_out, bf16_out, bf16_out, f32_out],
        compiler_params=_params("parallel", "arbitrary"),
    )(q_blk, k_blk, qn, qr, kn, kr, v, do, lse, delta)


def _exchange(arrs, *, scatter, name):
    n = len(arrs)
    out_shape = [jax.ShapeDtypeStruct(a.shape if scatter else (N_DEV, *a.shape), a.dtype) for a in arrs]

    def body(*refs):
        ins, outs = refs[:n], refs[n:2 * n]
        send_sems, recv_sems, local_sems = refs[2 * n:]
        x, y, c = lax.axis_index("x"), lax.axis_index("y"), lax.axis_index("c")
        me = 4 * x + 2 * y + c
        copies = []
        for k in range(n):
            local = pltpu.make_async_copy(ins[k].at[me] if scatter else ins[k], outs[k].at[me], local_sems.at[k])
            local.start()
            copies.append(local)
            for d in range(1, N_DEV):
                px, py, pc = (x + (d >> 2)) % 2, (y + ((d >> 1) & 1)) % 2, (c + (d & 1)) % 2
                peer = 4 * px + 2 * py + pc
                remote = pltpu.make_async_remote_copy(
                    src_ref=ins[k].at[peer] if scatter else ins[k], dst_ref=outs[k].at[me],
                    send_sem=send_sems.at[k, d - 1], recv_sem=recv_sems.at[k, d - 1],
                    device_id=(px, py, pc), device_id_type=pl.DeviceIdType.MESH)
                remote.start()
                copies.append(remote)
        for cp in copies:
            cp.wait()

    any_spec = pl.BlockSpec(memory_space=pl.ANY)
    return pl.pallas_call(
        body, name=name, in_specs=[any_spec] * n, out_specs=[any_spec] * n, out_shape=out_shape,
        scratch_shapes=[pltpu.SemaphoreType.DMA((n, N_DEV - 1)), pltpu.SemaphoreType.DMA((n, N_DEV - 1)),
                        pltpu.SemaphoreType.DMA((n,))],
    )(*arrs)


def _peers(x, y, c):
    out = []
    for d in range(1, N_DEV):
        px, py, pc = (x + (d >> 2)) % 2, (y + ((d >> 1) & 1)) % 2, (c + (d & 1)) % 2
        out.append(((px, py, pc), 4 * px + 2 * py + pc))
    return out


def _exchange_copies(ins, lands, send_sems, recv_sems, scatter):
    x, y, c = lax.axis_index("x"), lax.axis_index("y"), lax.axis_index("c")
    me = 4 * x + 2 * y + c
    local, remote = [], []
    for k in range(len(ins)):
        local.append(pltpu.make_async_copy(ins[k].at[me] if scatter else ins[k], lands[k].at[me],
                                           recv_sems.at[k * N_DEV + N_DEV - 1]))
        for d, (coords, peer) in enumerate(_peers(x, y, c)):
            remote.append(pltpu.make_async_remote_copy(
                src_ref=ins[k].at[peer] if scatter else ins[k], dst_ref=lands[k].at[me],
                send_sem=send_sems.at[k * N_DEV + d], recv_sem=recv_sems.at[k * N_DEV + d],
                device_id=coords, device_id_type=pl.DeviceIdType.MESH))
    return local, remote


def _exchange_start(arrs, *, scatter, name, after=None):
    n = len(arrs)
    hbm = pl.BlockSpec(memory_space=pltpu.HBM)
    sem = pl.BlockSpec(memory_space=pltpu.SEMAPHORE)
    lands = [lax.empty(a.shape if scatter else (N_DEV, *a.shape), a.dtype) for a in arrs]

    def body(*refs):
        ins, land_refs = refs[:n], refs[n:2 * n]
        first_out = 2 * n + (after is not None)
        send_sems, recv_sems, token = refs[first_out], refs[first_out + 1], refs[-1]
        local, remote = _exchange_copies(ins, land_refs, send_sems, recv_sems, scatter)
        for cp in local + remote:
            cp.start()
        token[...] = jnp.zeros_like(token)

    operands = [pltpu.with_memory_space_constraint(a, pltpu.HBM) for a in list(arrs) + lands]
    behind = [] if after is None else [after]
    res = pl.pallas_call(
        body, name=name,
        out_shape=(pltpu.SemaphoreType.DMA((n * N_DEV,)), pltpu.SemaphoreType.DMA((n * N_DEV,)),
                   *[pltpu.HBM(o.shape, o.dtype) for o in operands], jax.ShapeDtypeStruct((8, LANES), F32)),
        in_specs=[hbm] * (2 * n) + [pl.BlockSpec(memory_space=pl.ANY)] * len(behind),
        out_specs=(sem, sem, *[hbm] * (2 * n), pl.BlockSpec(memory_space=pltpu.VMEM)),
        input_output_aliases={i: 2 + i for i in range(2 * n)},
        compiler_params=pltpu.CompilerParams(has_side_effects=pltpu.SideEffectType.DATAFLOW_SIDE_EFFECTING),
    )(*operands, *behind)
    return (res[0], res[1], list(res[2:2 + n]), list(res[2 + n:2 + 2 * n]), scatter), res[-1]


def _exchange_wait(state, after, *, name):
    send_sems, recv_sems, ins, lands, scatter = state
    n = len(ins)
    hbm = pl.BlockSpec(memory_space=pltpu.HBM)
    sem = pl.BlockSpec(memory_space=pltpu.SEMAPHORE)

    def body(*refs):
        in_refs, land_refs = refs[:n], refs[n:2 * n]
        local, remote = _exchange_copies(in_refs, land_refs, refs[2 * n], refs[2 * n + 1], scatter)
        for cp in local:
            cp.wait()
        for cp in remote:
            cp.wait_send()
            cp.wait_recv()

    res = pl.pallas_call(
        body, name=name, out_shape=tuple(pltpu.HBM(o.shape, o.dtype) for o in ins + lands),
        in_specs=[hbm] * (2 * n) + [sem, sem, pl.BlockSpec(memory_space=pl.ANY)], out_specs=tuple([hbm] * (2 * n)),
        input_output_aliases={i: i for i in range(2 * n)},
        compiler_params=pltpu.CompilerParams(has_side_effects=pltpu.SideEffectType.DATAFLOW_SIDE_EFFECTING),
    )(*ins, *lands, send_sems, recv_sems, after)
    return list(res[n:])


def _adam(w, terms, m, v, *, name):
    n_layers, r, c = w.shape
    tr = min(r, 128)
    assert r % tr == 0 and len(terms) == n_layers
    steps = r // tr

    def body(w_ref, *rest):
        t_refs, (m_ref, v_ref, g_out, d_out, m_out, v_out) = rest[:n_layers], rest[n_layers:]
        for layer, t_ref in enumerate(t_refs):
            @pl.when(pl.program_id(0) == layer)
            def _(t_ref=t_ref):
                g = t_ref[0].astype(F32)
                for s in range(1, t_ref.shape[0]):
                    g = g + t_ref[s].astype(F32)
                m1 = ADAM_B1 * m_ref[...] + (1.0 - ADAM_B1) * g
                v1 = ADAM_B2 * v_ref[...] + (1.0 - ADAM_B2) * jnp.square(g)
                m_hat = m1 / (1.0 - ADAM_B1 ** ADAM_STEP)
                v_hat = v1 / (1.0 - ADAM_B2 ** ADAM_STEP)
                g_out[...] = g
                d_out[...] = -ADAM_LR * (m_hat / (jnp.sqrt(v_hat) + ADAM_EPS) + ADAM_WD * w_ref[...])
                m_out[...] = m1
                v_out[...] = v1

    def term_spec(layer, t):
        return pl.BlockSpec((t.shape[0], tr, c),
                            lambda l, i: (0, jnp.where(l == layer, i, jnp.where(l < layer, 0, steps - 1)), 0))

    spec = pl.BlockSpec((None, tr, c), lambda l, i: (l, i, 0))
    out = jax.ShapeDtypeStruct(w.shape, F32)
    return pl.pallas_call(
        body, name=name, grid=(n_layers, steps),
        in_specs=[spec] + [term_spec(layer, t) for layer, t in enumerate(terms)] + [spec, spec], out_specs=[spec] * 4,
        out_shape=[out] * 4, compiler_params=_params("arbitrary", "arbitrary"),
    )(w, *terms, m, v)


def _sum_terms(terms, *, name):
    n, _, p = terms.shape

    def body(t_ref, o_ref):
        acc = t_ref[0]
        for s in range(1, n):
            acc = acc + t_ref[s]
        o_ref[...] = acc

    return pl.pallas_call(body, name=name, out_shape=jax.ShapeDtypeStruct((1, p), F32))(terms)


def _lb_logits_grad(dlb, logits, *, name):
    def body(dlb_ref, l_ref, o_ref):
        lb = _lower_bound(l_ref[...])
        d0 = dlb_ref[...] * lb * (1.0 - lb)
        o_ref[...] = jnp.concatenate([d0, -d0], axis=0)

    return pl.pallas_call(body, name=name, out_shape=jax.ShapeDtypeStruct(logits.shape, F32))(dlb, logits)


def _silu_grad(z):
    sg = _sigmoid(z)
    return sg * (1.0 + z * (1.0 - sg))


def _head_norm_gate(o, zg, gn):
    outs = []
    for h in range(HGRN_HEADS):
        sl = slice(h * LANES, (h + 1) * LANES)
        zg_h = zg[:, sl]
        outs.append(_rms(o[:, sl], gn) * (zg_h * _sigmoid(zg_h)))
    return (jnp.concatenate(outs, axis=1),)


def _head_norm_gate_bwd(o, zg, dm, gn):
    do_parts, dzg_parts, dgn = [], [], jnp.zeros((1, LANES), F32)
    for h in range(HGRN_HEADS):
        sl = slice(h * LANES, (h + 1) * LANES)
        o_h, zg_h, dm_h = o[:, sl], zg[:, sl], dm[:, sl]
        gate = zg_h * _sigmoid(zg_h)
        do_h, dgn_h = _rms_bwd(o_h, gn, dm_h * gate)
        dgn = dgn + dgn_h
        do_parts.append(do_h)
        dzg_parts.append(dm_h * _rms(o_h, gn) * _silu_grad(zg_h))
    return jnp.concatenate(do_parts, axis=1), jnp.concatenate(dzg_parts, axis=1), dgn


def _rope_slabs(x, t_c, t_s1, t_s2, transpose):
    fn = _rope_t if transpose else _rope
    return jnp.concatenate(
        [fn(x[:, h * LANES:(h + 1) * LANES], t_c, t_s1, t_s2) for h in range(x.shape[1] // LANES)], axis=1)


def _loss_head(h, tgt, w):
    d = h.shape[1]
    r = lax.rsqrt(jnp.mean(h * h, axis=-1, keepdims=True) + EPS)
    xh = h * r
    err = xh * w - tgt
    loss = 0.5 * jnp.sum(jnp.mean(err * err, axis=-1, keepdims=True), axis=0, keepdims=True)
    dy = err / d
    dxh = dy * w
    dh = r * (dxh - xh * jnp.mean(dxh * xh, axis=-1, keepdims=True))
    return dh, dh, jnp.sum(dy * xh, axis=0, keepdims=True), jnp.broadcast_to(loss, (1, LANES))


def _mlp_fwd(h, norm, w_up, w_down, tag, loss_head=None):
    d = h.shape[1]

    def up(x, g, wu):
        x_n = _rms(x, g).astype(BF16)
        return x_n, jnp.concatenate([jnp.square(jnp.maximum(_dot(x_n, wu[j], _NN), 0.0)) for j in range(wu.shape[0])],
                                    axis=1)

    xn, act = _rowcall(up, [h], [norm, w_up], [(d, BF16), (w_up.shape[0] * w_up.shape[2], BF16)], [], tr=512,
                       name=f"{tag}_up")
    if callable(w_down):
        w_down = w_down(act)
    if loss_head is None:
        return _mm(act, w_down, mode="nn", add=h, name=f"{tag}_down"), (h, xn, act)
    tgt, final_norm = loss_head

    def down_and_loss(a, res, t, wd, g):
        return _loss_head(res + _dot(a, wd, _NN), t, g)

    return _rowcall(down_and_loss, [act, h, tgt], [w_down, final_norm], [(d, F32), (d, BF16)], [d, LANES],
                    name=f"{tag}_down_loss"), (h, xn, act)


def _mlp_bwd(dh_out, dh_out_bf, saved, norm, w_up, w_down, tag, after=None):
    h, xn, act = saved
    d = h.shape[1]
    du = _mm(dh_out_bf, w_down, mode="nt", relu2_of=act, out_dtype=BF16, after=after, name=f"{tag}_bwd_du")
    dw_down = _mm(act, dh_out_bf, mode="tn", name=f"{tag}_bwd_wdown")
    dw_up = _mm(xn, du, mode="tn", col_shards=w_up.shape[0], name=f"{tag}_bwd_wup")

    def up_norm_bwd(x, d_u, dres, g, wu):
        cols = wu.shape[2]
        dxn = _dot(d_u[:, :cols], wu[0], _NT)
        for j in range(1, wu.shape[0]):
            dxn = dxn + _dot(d_u[:, j * cols:(j + 1) * cols], wu[j], _NT)
        dx, dw = _rms_bwd(x, g, dxn)
        return dx + dres, dx + dres, dw

    dh, dh_bf, dnorm = _rowcall(up_norm_bwd, [h, du, dh_out], [norm, w_up], [(d, F32), (d, BF16)], [d], tr=512,
                                name=f"{tag}_bwd_dxn")
    return dh, dh_bf, dnorm, dw_up, dw_down


def _row_major(g):
    return g.reshape(g.shape[0] * g.shape[1], g.shape[2])


def _col_major(g):
    return jnp.transpose(g, (1, 0, 2)).reshape(g.shape[1], g.shape[0] * g.shape[2])


def _col_terms(dw):
    k, n = dw.shape
    return jnp.transpose(dw.reshape(k, N_DEV, n // N_DEV), (1, 0, 2))


def _row_terms(dw):
    return dw.reshape(N_DEV, dw.shape[0] // N_DEV, dw.shape[1])


def kernel(x, hgrn_norm, hgrn_w_q, hgrn_w_f, hgrn_w_i, hgrn_w_g, hgrn_g_norm, hgrn_w_o, hgrn_lb_logits, mla_norm, mla_w_dq, mla_q_norm, mla_w_uq, mla_w_o, kv_in_norm, kv_w_dkv, kv_norm, kv_w_uk, kv_w_uv, mlp_norm, mlp_w_up, mlp_w_down, final_norm, loss_target, m_hgrn_norm, m_hgrn_w_q, m_hgrn_w_f, m_hgrn_w_i, m_hgrn_w_g, m_hgrn_g_norm, m_hgrn_w_o, m_hgrn_lb_logits, m_mla_norm, m_mla_w_dq, m_mla_q_norm, m_mla_w_uq, m_mla_w_o, m_kv_in_norm, m_kv_w_dkv, m_kv_norm, m_kv_w_uk, m_kv_w_uv, m_mlp_norm, m_mlp_w_up, m_mlp_w_down, m_final_norm, v_hgrn_norm, v_hgrn_w_q, v_hgrn_w_f, v_hgrn_w_i, v_hgrn_w_g, v_hgrn_g_norm, v_hgrn_w_o, v_hgrn_lb_logits, v_mla_norm, v_mla_w_dq, v_mla_q_norm, v_mla_w_uq, v_mla_w_o, v_kv_in_norm, v_kv_w_dkv, v_kv_norm, v_kv_w_uk, v_kv_w_uv, v_mlp_norm, v_mlp_w_up, v_mlp_w_down, v_final_norm):
    given = dict(locals())
    weight_names = ["hgrn_norm", "hgrn_w_q", "hgrn_w_f", "hgrn_w_i", "hgrn_w_g", "hgrn_g_norm", "hgrn_w_o",
                    "hgrn_lb_logits", "mla_norm", "mla_w_dq", "mla_q_norm", "mla_w_uq", "mla_w_o", "kv_in_norm",
                    "kv_w_dkv", "kv_norm", "kv_w_uk", "kv_w_uv", "mlp_norm", "mlp_w_up", "mlp_w_down", "final_norm"]
    me = 4 * lax.axis_index("x") + 2 * lax.axis_index("y") + lax.axis_index("c")
    xs, tgt = x[0], loss_target[0]
    seq, d_model = xs.shape
    n_heads, hd = MLA_HEADS, LANES

    big_local = {
        "hgrn_w_q": hgrn_w_q[0], "hgrn_w_f": hgrn_w_f[0], "hgrn_w_i": hgrn_w_i[0], "hgrn_w_g": hgrn_w_g[0],
        "hgrn_w_o": hgrn_w_o[0], "mla_w_dq": mla_w_dq[0], "mla_w_uq": mla_w_uq[0], "mla_w_o": mla_w_o[0],
        "kv_w_dkv": kv_w_dkv, "kv_w_uk": kv_w_uk, "kv_w_uv": kv_w_uv,
        "mlp_w_up0": mlp_w_up[0], "mlp_w_up1": mlp_w_up[1], "mlp_w_down0": mlp_w_down[0], "mlp_w_down1": mlp_w_down[1],
    }
    big_names = list(big_local)
    col_sharded = {"mla_w_uq", "kv_w_uk", "kv_w_uv"}
    shard_major = {"mlp_w_up0", "mlp_w_up1"}
    vec_local = jnp.concatenate([hgrn_norm, hgrn_lb_logits], axis=0)
    first_names = ["hgrn_w_q", "hgrn_w_f", "hgrn_w_i"]
    proj_names = first_names + ["hgrn_w_g"]
    later_names = {"hgrn_o": ["hgrn_w_g", "hgrn_w_o"], "up0": ["mlp_w_up0"], "down0": ["mlp_w_down0"],
                   "mla": ["kv_w_dkv", "kv_w_uk", "kv_w_uv", "mla_w_dq", "mla_w_uq", "mla_w_o"],
                   "mlp1": ["mlp_w_up1", "mlp_w_down1"]}

    def unshard(names, arrays):
        return {k: (a if k in shard_major else _col_major(a) if k in col_sharded else _row_major(a))
                for k, a in zip(names, arrays)}

    first_state, token = _exchange_start([big_local[k].astype(BF16) for k in first_names] + [vec_local], scatter=False,
                                         name="gather_first_start")
    gather_state = {}
    for tag, names in later_names.items():
        gather_state[tag], token = _exchange_start([big_local[k].astype(BF16) for k in names], scatter=False,
                                                   after=token, name=f"gather_{tag}_start")

    def gather_wait(tag, after):
        w.update(unshard(later_names[tag], _exchange_wait(gather_state[tag], after, name=f"gather_{tag}_wait")))
        return [w[k] for k in later_names[tag]]

    gathered = _exchange_wait(first_state, token, name="gather_first_wait")
    w = unshard(first_names, gathered[:-1])
    vec_full = jnp.transpose(gathered[-1], (1, 0, 2)).reshape(3, d_model)
    hgrn_norm_full, lb_logits_full = vec_full[0:1], vec_full[1:3]
    t_c, t_s1, t_s2 = _rope_tables(seq)
    kv_lora = kv_w_uk.shape[0]

    def hgrn_proj(a, g, *weights):
        xn = _rms(a, g).astype(BF16)
        return (xn, *[_dot(xn, wt, _NN) for wt in weights])

    xn0, zq, zf, zi = _rowcall(hgrn_proj, [xs], [hgrn_norm_full] + [w[k] for k in first_names],
                               [(d_model, BF16)] + [(d_model, F32)] * 3, [], tr=512, name="hgrn_proj")
    o_rec, states = _hgrn_fwd(zq, zf, zi, lb_logits_full, name="hgrn_fwd")
    gather_wait("hgrn_o", o_rec)

    def gate_out(o, x_n, res, gn, wg, wo):
        z = _dot(x_n, wg, _NN)
        m = _head_norm_gate(o, z, gn)[0].astype(BF16)
        return z, m, res + _dot(m, wo, _NN)

    zg, mixed, h1 = _rowcall(gate_out, [o_rec, xn0, xs], [hgrn_g_norm, w["hgrn_w_g"], w["hgrn_w_o"]],
                             [(d_model, F32), (d_model, BF16), (d_model, F32)], [], name="hgrn_gate_out")
    h2, mlp0_saved = _mlp_fwd(h1, mlp_norm[0:1], gather_wait("up0", h1)[0], lambda act: gather_wait("down0", act)[0],
                              "mlp0")
    gather_wait("mla", h2)
    w_uq3 = w["mla_w_uq"].reshape(-1, n_heads, MLA_NOPE + MLA_ROPE)
    w_uq_nope = w_uq3[:, :, :MLA_NOPE].reshape(-1, n_heads * hd)
    w_uq_rope = jnp.pad(w_uq3[:, :, MLA_NOPE:], ((0, 0), (0, 0), (0, hd - MLA_ROPE))).reshape(-1, n_heads * hd)
    w_dkv_pad = jnp.pad(w["kv_w_dkv"], ((0, 0), (0, kv_lora + hd - w["kv_w_dkv"].shape[1])))

    q_lora, qk_cols = w["mla_w_dq"].shape[1], n_heads * hd

    def mla_qkv(a, tc, ts1, ts2, g_kv_in, g_mla, g_q, g_kv, wdq, wn, wr, wdkv, wuk, wuv):
        h_n, x_n = _rms(a, g_kv_in).astype(BF16), _rms(a, g_mla).astype(BF16)
        cq = _dot(x_n, wdq, _NN)
        cq_n = _rms(cq, g_q).astype(BF16)
        q_nope = _dot(cq_n, wn, _NN) * Q_PRESCALE
        q_rope = _rope_slabs(_dot(cq_n, wr, _NN) * Q_PRESCALE, tc, ts1, ts2, False)
        c_all = _dot(h_n, wdkv, _NN)
        lat = _rms(c_all[:, :kv_lora], g_kv).astype(BF16)
        return (h_n, x_n, cq, cq_n, q_nope, q_rope, c_all, lat, _rope(c_all[:, kv_lora:], tc, ts1, ts2),
                _dot(lat, wuk, _NN), _dot(lat, wuv, _NN))

    hn, xn2, cq_pre, c_q, qn, qr, ckr, c_kv, kr, kn, vv = _rowcall(
        mla_qkv, [h2, t_c, t_s1, t_s2],
        [kv_in_norm[None, :], mla_norm, mla_q_norm, kv_norm[None, :], w["mla_w_dq"], w_uq_nope, w_uq_rope, w_dkv_pad,
         w["kv_w_uk"], w["kv_w_uv"]],
        [(d_model, BF16), (d_model, BF16), (q_lora, F32), (q_lora, BF16), (qk_cols, BF16), (qk_cols, BF16),
         (kv_lora + hd, F32), (kv_lora, BF16), (hd, BF16), (qk_cols, BF16), (qk_cols, BF16)], [], tr=512, name="mla_qkv")
    o_att, lse = _attn_fwd(qn, qr, kn, kr, vv, name="attn_fwd")
    h3 = _mm(o_att, w["mla_w_o"], mode="nn", add=h2, name="attn_out")
    gather_wait("mlp1", h3)
    (dh4, dh4_bf, g_final_norm, loss_part), mlp1_saved = _mlp_fwd(
        h3, mlp_norm[1:2], w["mlp_w_up1"], w["mlp_w_down1"], "mlp1", loss_head=(tgt, final_norm[None, :]))

    g = {}
    groups = {"mlp1": ["mlp_w_up1", "mlp_w_down1"],
              "mla": ["mla_w_o", "mla_w_uq", "mla_w_dq", "kv_w_uk", "kv_w_uv", "kv_w_dkv"],
              "mlp0": ["mlp_w_up0", "mlp_w_down0"],
              "hgrn_out": ["hgrn_w_o", "hgrn_w_g"],
              "hgrn_in": ["hgrn_w_q", "hgrn_w_f", "hgrn_w_i"]}
    scatter_state = {}

    def scatter_start(tag, after=None):
        scatter_state[tag], tok = _exchange_start(
            [g[k] if k in shard_major else (_col_terms if k in col_sharded else _row_terms)(g[k]) for k in groups[tag]],
            scatter=True, after=after,
            name=f"scatter_{tag}_start")
        return tok

    dh3, dh3_bf, g_mlp_norm1, g["mlp_w_up1"], g["mlp_w_down1"] = _mlp_bwd(
        dh4, dh4_bf, mlp1_saved, mlp_norm[1:2], w["mlp_w_up1"], w["mlp_w_down1"], "mlp1")
    def attn_out_bwd(dres, o, wo):
        d_o = _dot(dres, wo, _NT).astype(BF16)
        prod = d_o.astype(F32) * o.astype(F32)
        return d_o, jnp.concatenate([jnp.broadcast_to(jnp.sum(prod[:, h * hd:(h + 1) * hd], axis=1, keepdims=True),
                                                      (prod.shape[0], hd)) for h in range(n_heads)], axis=1)

    d_oatt, delta = _rowcall(attn_out_bwd, [dh3_bf, o_att], [w["mla_w_o"]], [(qk_cols, BF16), (qk_cols, F32)], [],
                             after=scatter_start("mlp1"), name="attn_out_bwd_x")
    g["mla_w_o"] = _mm(o_att, dh3_bf, mode="tn", name="attn_out_bwd_w")
    dqn, dqr, dkn, dvv, dkr = _attn_bwd(qn, qr, kn, kr, vv, d_oatt, lse, delta, name="attn_bwd")

    def q_path_bwd(cq, cq_n, x_n, d_qn, d_qr, tc, ts1, ts2, g_q, wdq, wn, wr):
        d_qn, d_qr = d_qn.astype(BF16), _rope_slabs(d_qr, tc, ts1, ts2, True).astype(BF16)
        d_cq, d_gq = _rms_bwd(cq, g_q, _dot(d_qn, wn, _NT) + _dot(d_qr, wr, _NT))
        d_cq = d_cq.astype(BF16)
        return _dot(d_cq, wdq, _NT), d_gq, _dot(x_n, d_cq, _TN), _dot(cq_n, d_qn, _TN), _dot(cq_n, d_qr, _TN)

    dxn2, g_q_norm, g_dq, g_uq_nope, g_uq_rope = _rowcall(
        q_path_bwd, [cq_pre, c_q, xn2, dqn, dqr, t_c, t_s1, t_s2], [mla_q_norm, w["mla_w_dq"], w_uq_nope, w_uq_rope],
        [(d_model, F32)], [q_lora, (d_model, q_lora), (q_lora, qk_cols), (q_lora, qk_cols)], tr=512, name="mla_q_bwd")
    g["mla_w_dq"] = g_dq.astype(GRAD_WIRE_DTYPE)
    g["mla_w_uq"] = jnp.concatenate([g_uq_nope.reshape(q_lora, n_heads, hd),
                                     g_uq_rope.reshape(q_lora, n_heads, hd)[:, :, :MLA_ROPE]],
                                    axis=2).reshape(q_lora, -1).astype(GRAD_WIRE_DTYPE)

    def kv_path_bwd(c_all, lat, h_n, d_kn, d_v, d_kr_heads, tc, ts1, ts2, a, d_xn2, dres,
                    g_kv, g_kv_in, g_mla, wdkv, wuk, wuv):
        d_lat, d_gkv = _rms_bwd(c_all[:, :kv_lora], g_kv, _dot(d_kn, wuk, _NT) + _dot(d_v, wuv, _NT))
        d_kr = d_kr_heads[:, :hd]
        for h in range(1, n_heads):
            d_kr = d_kr + d_kr_heads[:, h * hd:(h + 1) * hd]
        d_all = jnp.concatenate([d_lat, _rope_t(d_kr, tc, ts1, ts2)], axis=1).astype(BF16)
        dx1, d_gkv_in = _rms_bwd(a, g_kv_in, _dot(d_all, wdkv, _NT))
        dx2, d_gmla = _rms_bwd(a, g_mla, d_xn2)
        d_a = dx1 + dx2 + dres
        return (d_a, d_a, d_gkv, d_gkv_in, d_gmla, _dot(h_n, d_all, _TN), _dot(lat, d_kn, _TN), _dot(lat, d_v, _TN))

    dh2, dh2_bf, g_kv_norm, g_kv_in_norm, g_mla_norm, g_dkv, g_uk, g_uv = _rowcall(
        kv_path_bwd, [ckr, c_kv, hn, dkn, dvv, dkr, t_c, t_s1, t_s2, h2, dxn2, dh3],
        [kv_norm[None, :], kv_in_norm[None, :], mla_norm, w_dkv_pad, w["kv_w_uk"], w["kv_w_uv"]],
        [(d_model, F32), (d_model, BF16)],
        [kv_lora, d_model, d_model, (d_model, kv_lora + hd), (kv_lora, qk_cols), (kv_lora, qk_cols)], name="mla_kv_bwd")
    g["kv_w_dkv"] = g_dkv[:, :kv_w_dkv.shape[1]].astype(GRAD_WIRE_DTYPE)
    g["kv_w_uk"], g["kv_w_uv"] = g_uk.astype(GRAD_WIRE_DTYPE), g_uv.astype(GRAD_WIRE_DTYPE)
    dh1, dh1_bf, g_mlp_norm0, g["mlp_w_up0"], g["mlp_w_down0"] = _mlp_bwd(
        dh2, dh2_bf, mlp0_saved, mlp_norm[0:1], w["mlp_w_up0"], w["mlp_w_down0"], "mlp0", after=scatter_start("mla"))

    g["hgrn_w_o"] = _mm(mixed, dh1_bf, mode="tn", after=scatter_start("mlp0"), name="hgrn_out_bwd_w")
    do_rec, dzg, g_g_norm = _rowcall(
        lambda dres, o, z, wo, gn: _head_norm_gate_bwd(o, z, _dot(dres, wo, _NT), gn), [dh1_bf, o_rec, zg],
        [w["hgrn_w_o"], hgrn_g_norm], [(d_model, F32), (d_model, BF16)], [hd], name="hgrn_gate_out_bwd")
    g["hgrn_w_g"] = _mm(xn0, dzg, mode="tn", name="hgrn_w_g_bwd_w")
    dzq, dzf, dzi, g_lb = _hgrn_bwd(zq, zf, zi, lb_logits_full, states, do_rec, scatter_start("hgrn_out"),
                                    name="hgrn_bwd")
    for nm, dz in (("hgrn_w_q", dzq), ("hgrn_w_f", dzf), ("hgrn_w_i", dzi)):
        g[nm] = _mm(xn0, dz, mode="tn", name=f"{nm}_bwd_w")

    def hgrn_proj_bwd(a, dres, *rest):
        dzs, gw, weights = rest[:4], rest[4], rest[5:]
        dxn = _dot(dzs[0], weights[0], _NT)
        for dz, wt in zip(dzs[1:], weights[1:]):
            dxn = dxn + _dot(dz, wt, _NT)
        dx, dw = _rms_bwd(a, gw, dxn)
        return dx + dres, dw

    grad_x, g_hgrn_norm = _rowcall(hgrn_proj_bwd, [xs, dh1, dzq, dzf, dzi, dzg],
                                   [hgrn_norm_full] + [w[k] for k in proj_names], [(d_model, F32)], [d_model],
                                   tr=512, name="hgrn_proj_bwd")

    small_parts = [g_hgrn_norm, g_lb, g_g_norm, g_mla_norm, g_q_norm, g_kv_in_norm, g_kv_norm, g_mlp_norm0,
                   g_mlp_norm1, g_final_norm, loss_part]
    small_sizes = [p.shape[1] for p in small_parts]
    small_terms = _exchange([jnp.concatenate(small_parts, axis=1)], scatter=False, name="gather_small")[0]
    small_sum = _sum_terms(small_terms, name="sum_small")
    last = scatter_start("hgrn_in", after=small_sum)
    offs = [0]
    for sz in small_sizes:
        offs.append(offs[-1] + sz)
    (s_hgrn_norm, s_lb, s_g_norm, s_mla_norm, s_q_norm, s_kv_in_norm, s_kv_norm, s_mlp_norm0, s_mlp_norm1, s_final_norm,
     s_loss) = [small_sum[:, a:b] for a, b in zip(offs[:-1], offs[1:])]
    shard = hgrn_norm.shape[1]
    g_lb_logits = _lb_logits_grad(lax.dynamic_slice_in_dim(s_lb, me * shard, shard, axis=1), hgrn_lb_logits,
                                  name="lb_logits_grad")
    loss = s_loss[0, 0]

    res, layer_terms = {}, {}

    def update(k, term_list):
        shape = given[k].shape
        as_layers = (len(term_list), shape[-2], shape[-1])
        upd = _adam(given[k].reshape(as_layers), term_list, given["m_" + k].reshape(as_layers),
                    given["v_" + k].reshape(as_layers), name=f"adam_{k}")
        res[k] = [o.reshape(shape) for o in upd]
        return upd[0]

    for tag, names in groups.items():
        for k, t in zip(names, _exchange_wait(scatter_state[tag], last, name=f"scatter_{tag}_wait")):
            if k.startswith("mlp_w_"):
                layer_terms.setdefault(k[:-1], {})[int(k[-1])] = t
                if len(layer_terms[k[:-1]]) == 2:
                    last = update(k[:-1], [layer_terms[k[:-1]][0], layer_terms[k[:-1]][1]])
            else:
                last = update(k, [t])

    small_grads = {
        "hgrn_norm": lax.dynamic_slice_in_dim(s_hgrn_norm, me * shard, shard, axis=1),
        "hgrn_g_norm": s_g_norm, "hgrn_lb_logits": g_lb_logits, "mla_norm": s_mla_norm, "mla_q_norm": s_q_norm,
        "kv_in_norm": s_kv_in_norm, "kv_norm": s_kv_norm,
        "mlp_norm": jnp.concatenate([s_mlp_norm0, s_mlp_norm1], axis=0), "final_norm": s_final_norm,
    }
    small_names = list(small_grads)

    def flat(a):
        return a.reshape(1, -1)

    packed = [jnp.concatenate([flat(src[pre + k]) for k in small_names], axis=1)
              for src, pre in ((given, ""), (small_grads, ""), (given, "m_"), (given, "v_"))]
    small_out = _adam(packed[0][None], [packed[1][None]], packed[2][None], packed[3][None], name="adam_small")
    off = 0
    for k in small_names:
        size = given[k].size
        res[k] = [o[0, :, off:off + size].reshape(given[k].shape) for o in small_out]
        off += size

    outs = [loss, grad_x[None]]
    for i in range(4):
        outs += [res[k][i] for k in weight_names]
    return tuple(outs)
```

```python
import functools

import jax
import jax.numpy as jnp
from jax import lax
from jax.experimental import pallas as pl
from jax.experimental.pallas import tpu as pltpu

F32 = jnp.float32
BF16 = jnp.bfloat16

EPS = 1e-6
LANES = 128
N_DEV = 8
V7X_VMEM_LIMIT_BYTES = 56 << 20
MM_PIPELINE_BYTES = 30 << 20
MM_ROW_TILE = 512
GRAD_WIRE_DTYPE = BF16

HGRN_HEADS = 8
HGRN_CHUNK = 64
HGRN_SUB = 16
HGRN_HEADS_PER_STEP = 8
HGRN_CHUNKS_PER_STEP = 2
EXP_CLAMP = 80.0
MLA_HEADS = 16
MLA_NOPE = 128
MLA_ROPE = 64
ROPE_THETA = 10000.0
ATTN_SCALE = (MLA_NOPE + MLA_ROPE) ** -0.5

ADAM_LR = 0.001
ADAM_B1 = 0.9
ADAM_B2 = 0.999
ADAM_EPS = 1e-08
ADAM_WD = 0.01
ADAM_STEP = 10

_NN = ((1,), (0,))
_NT = ((1,), (1,))
_TN = ((0,), (0,))


def _params(*sem):
    return pltpu.CompilerParams(dimension_semantics=sem, vmem_limit_bytes=V7X_VMEM_LIMIT_BYTES)


def _dot(a, b, dims):
    return lax.dot_general(a.astype(BF16), b.astype(BF16), (dims, ((), ())), preferred_element_type=F32)


def _dot_f32(a, b, dims=_NN):
    return lax.dot_general(a, b, (dims, ((), ())), precision=lax.Precision.HIGH, preferred_element_type=F32)


def _sigmoid(x):
    return 1.0 / (1.0 + jnp.exp(-x))


def _rms(x, w):
    r = lax.rsqrt(jnp.mean(x * x, axis=-1, keepdims=True) + EPS)
    return x * r * w


def _rms_bwd(x, w, dy):
    r = lax.rsqrt(jnp.mean(x * x, axis=-1, keepdims=True) + EPS)
    xh = x * r
    dw = jnp.sum(dy * xh, axis=0, keepdims=True)
    dxh = dy * w
    dx = r * (dxh - xh * jnp.mean(dxh * xh, axis=-1, keepdims=True))
    return dx, dw


def _mm_tiles(m, n, k, a_bytes, b_bytes, out_tile_bytes):
    tm = min(m, MM_ROW_TILE)
    for tn in (n, 2048, 1024, 512, 256, LANES):
        if tn <= n and n % tn == 0:
            if 2 * (tm * k * a_bytes + k * tn * b_bytes + tm * tn * out_tile_bytes) <= MM_PIPELINE_BYTES:
                return tm, tn
    return tm, min(n, LANES)


def _mm(a, b, *, mode, name, out_dtype=None, add=None, relu2_of=None, after=None, col_shards=None):
    if mode == "nn":
        (m, k), (k2, n) = a.shape, b.shape
    elif mode == "nt":
        (m, k), (n, k2) = a.shape, b.shape
    else:
        (k, m), (k2, n) = a.shape, b.shape
    assert k == k2, (name, a.shape, b.shape)
    if out_dtype is None:
        out_dtype = GRAD_WIRE_DTYPE if mode == "tn" else F32
    tile_bytes = sum(x.dtype.itemsize for x in (add, relu2_of) if x is not None) + jnp.dtype(out_dtype).itemsize
    tm, tn = _mm_tiles(m, n, k, a.dtype.itemsize, b.dtype.itemsize, tile_bytes)
    if col_shards is not None:
        assert add is None and relu2_of is None
        tn = n // col_shards
    assert m % tm == 0 and n % tn == 0, (name, m, n)
    dims = {"nn": _NN, "nt": _NT, "tn": _TN}[mode]
    a_spec = pl.BlockSpec((k, tm), lambda i, j: (0, i)) if mode == "tn" else pl.BlockSpec((tm, k), lambda i, j: (i, 0))
    b_spec = pl.BlockSpec((tn, k), lambda i, j: (j, 0)) if mode == "nt" else pl.BlockSpec((k, tn), lambda i, j: (0, j))
    o_spec = pl.BlockSpec((tm, tn), lambda i, j: (i, j))
    operands, in_specs = [a, b], [a_spec, b_spec]
    for extra in (add, relu2_of):
        if extra is not None:
            assert extra.shape == (m, n), (name, extra.shape)
            operands.append(extra)
            in_specs.append(o_spec)
    n_in = len(operands)
    if after is not None:
        operands.append(after)
        in_specs.append(pl.BlockSpec(memory_space=pl.ANY))
    out_shape = jax.ShapeDtypeStruct((m, n), out_dtype)
    if col_shards is not None:
        out_shape = jax.ShapeDtypeStruct((col_shards, m, tn), out_dtype)
        o_spec = pl.BlockSpec((None, tm, tn), lambda i, j: (j, i, 0))

    def body(*refs):
        acc = _dot(refs[0][...], refs[1][...], dims)
        extras, outs = refs[2:n_in], refs[len(operands):]
        if add is not None:
            acc = acc + extras[0][...]
        if relu2_of is not None:
            acc = acc * (2.0 * jnp.sqrt(extras[-1][...].astype(F32)))
        outs[0][...] = acc.astype(out_dtype)

    return pl.pallas_call(
        body, name=name, grid=(m // tm, n // tn), in_specs=in_specs, out_specs=o_spec, out_shape=out_shape,
        compiler_params=_params("parallel", "parallel"),
    )(*operands)


def _rowcall(fn, rows, consts, outs, accs, *, name, tr=256, after=None):
    s = rows[0].shape[0]
    tr = min(tr, s)
    assert s % tr == 0
    n_out = len(outs)
    accs = [(1, a) if isinstance(a, int) else a for a in accs]
    in_specs = [pl.BlockSpec((tr, r.shape[1]), lambda i: (i, 0)) for r in rows]
    in_specs += [pl.BlockSpec(c.shape, lambda i, nd=c.ndim: (0,) * nd) for c in consts]
    out_shape = [jax.ShapeDtypeStruct((s, w), dt) for w, dt in outs] + [jax.ShapeDtypeStruct(a, F32) for a in accs]
    out_specs = [pl.BlockSpec((tr, w), lambda i: (i, 0)) for w, _ in outs] + [pl.BlockSpec(a, lambda i: (0, 0)) for a in accs]
    n_in = len(rows) + len(consts)

    def body(*refs):
        res = fn(*[r[...] for r in refs[:n_in]])
        out_refs = refs[n_in + (after is not None):]
        for ref, val in zip(out_refs[:n_out], res[:n_out]):
            ref[...] = val.astype(ref.dtype)
        i = pl.program_id(0)
        for ref, val in zip(out_refs[n_out:], res[n_out:]):
            @pl.when(i == 0)
            def _(ref=ref, val=val):
                ref[...] = val

            @pl.when(i > 0)
            def _(ref=ref, val=val):
                ref[...] += val

    behind = [] if after is None else [after]
    return pl.pallas_call(
        body, name=name, grid=(s // tr,), in_specs=in_specs + [pl.BlockSpec(memory_space=pl.ANY)] * len(behind),
        out_specs=out_specs, out_shape=out_shape, compiler_params=_params("arbitrary" if accs else "parallel"),
    )(*rows, *consts, *behind)


def _rope_tables(seq):
    half = MLA_ROPE // 2
    inv_freq = ROPE_THETA ** (-jnp.arange(half, dtype=F32) / half)
    ang = jnp.arange(seq, dtype=F32)[:, None] * inv_freq[None, :]
    cos, sin, zero = jnp.cos(ang), jnp.sin(ang), jnp.zeros((seq, half), F32)
    t_c = jnp.concatenate([cos, cos, zero, zero], axis=1)
    t_s1 = jnp.concatenate([-sin, zero, zero, zero], axis=1)
    t_s2 = jnp.concatenate([zero, sin, zero, zero], axis=1)
    return t_c, t_s1, t_s2


def _rope(slab, t_c, t_s1, t_s2):
    return slab * t_c + pltpu.roll(slab, 96, 1) * t_s1 + pltpu.roll(slab, 32, 1) * t_s2


def _rope_t(d, t_c, t_s1, t_s2):
    return d * t_c + pltpu.roll(d * t_s1, 32, 1) + pltpu.roll(d * t_s2, 96, 1)


def _lower_bound(logits):
    l0, l1 = logits[0:1, :], logits[1:2, :]
    mx = jnp.maximum(l0, l1)
    e0, e1 = jnp.exp(l0 - mx), jnp.exp(l1 - mx)
    return e0 / (e0 + e1)


def _tri(n, lower):
    row = lax.broadcasted_iota(jnp.int32, (n, n), 0)
    col = lax.broadcasted_iota(jnp.int32, (n, n), 1)
    return (row >= col) if lower else (row <= col)


def _hgrn_fwd(zq, zf, zi, lb_logits, *, name):
    s, d = zq.shape
    h_n, c, hp, cps = d // LANES, HGRN_CHUNK, HGRN_HEADS_PER_STEP, HGRN_CHUNKS_PER_STEP
    nc = s // c

    def body(zq_ref, zf_ref, zi_ref, lb_ref, o_ref, st_ref, state_sc, b_sc):
        @pl.when(pl.program_id(1) == 0)
        def _():
            state_sc[...] = jnp.zeros_like(state_sc)

        lower = _tri(c, True)
        lower_f = lower.astype(F32)
        hs, pairs = range(hp), [(cc, hh) for cc in range(cps) for hh in range(hp)]
        sls = [slice(hh * LANES, (hh + 1) * LANES) for hh in hs]
        rws = [slice(cc * c, (cc + 1) * c) for cc in range(cps)]
        lb = [_lower_bound(lb_ref[:, sl]) for sl in sls]
        zq_v = {p: zq_ref[rws[p[0]], sls[p[1]]] for p in pairs}
        q = {p: zq_v[p] * _sigmoid(zq_v[p]) for p in pairs}
        f = {p: lb[p[1]] + (1.0 - lb[p[1]]) * _sigmoid(zf_ref[rws[p[0]], sls[p[1]]]) for p in pairs}
        k = {p: 1.0 - f[p] for p in pairs}
        v = {p: zi_ref[rws[p[0]], sls[p[1]]] for p in pairs}
        b = {p: _dot_f32(lower_f, jnp.log(f[p])) for p in pairs}
        for p in pairs:
            b_sc[p[0], p[1]] = b[p]
        qe = {p: q[p] * jnp.exp(b[p]) for p in pairs}
        scores = {p: [] for p in pairs}
        for i in range(c // HGRN_SUB):
            lo = i * HGRN_SUB
            for p in pairs:
                ref = b_sc[p[0], p[1], lo - 1:lo, :] if i > 0 else jnp.zeros((1, LANES), F32)
                qt = q[p][lo:lo + HGRN_SUB, :] * jnp.exp(b[p][lo:lo + HGRN_SUB, :] - ref)
                dec = jnp.exp(jnp.minimum(ref - b[p], EXP_CLAMP))
                scores[p].append(_dot(qt, k[p] * dec, _NT))
        o_intra = {p: _dot(jnp.where(lower, jnp.concatenate(scores[p], axis=0), 0.0), v[p], _NN) for p in pairs}
        bl = {p: b_sc[p[0], p[1], c - 1:c, :] for p in pairs}
        k_end = {p: k[p] * jnp.exp(bl[p] - b[p]) for p in pairs}
        state = [state_sc[hh] for hh in hs]
        for cc in range(cps):
            for hh in hs:
                st_ref[hh, cc] = state[hh]
                o_ref[rws[cc], sls[hh]] = _dot(qe[cc, hh], state[hh], _NT) + o_intra[cc, hh]
            state = [state[hh] * jnp.exp(bl[cc, hh]) + _dot(v[cc, hh], k_end[cc, hh], _TN) for hh in hs]
        for hh in hs:
            state_sc[hh] = state[hh]

    tile = pl.BlockSpec((cps * c, hp * LANES), lambda h, i: (i, h))
    return pl.pallas_call(
        body, name=name, grid=(h_n // hp, nc // cps),
        in_specs=[tile, tile, tile, pl.BlockSpec((2, hp * LANES), lambda h, i: (0, h))],
        out_specs=[tile, pl.BlockSpec((hp, cps, LANES, LANES), lambda h, i: (h, i, 0, 0))],
        out_shape=[jax.ShapeDtypeStruct((s, d), F32), jax.ShapeDtypeStruct((h_n, nc, LANES, LANES), F32)],
        scratch_shapes=[pltpu.VMEM((hp, LANES, LANES), F32), pltpu.VMEM((cps, hp, c, LANES), F32)],
        compiler_params=_params("parallel", "arbitrary"),
    )(zq, zf, zi, lb_logits)


def _hgrn_bwd(zq, zf, zi, lb_logits, states, do, after, *, name):
    s, d = zq.shape
    h_n, c, hp, cps = d // LANES, HGRN_CHUNK, HGRN_HEADS_PER_STEP, HGRN_CHUNKS_PER_STEP
    nc = s // c
    n_steps = nc // cps

    def body(zq_ref, zf_ref, zi_ref, lb_ref, st_ref, do_ref, _, dzq_ref, dzf_ref, dzi_ref, dlb_ref, dstate_sc, b_sc):
        @pl.when(pl.program_id(1) == 0)
        def _():
            dstate_sc[...] = jnp.zeros_like(dstate_sc)
            dlb_ref[...] = jnp.zeros_like(dlb_ref)

        lower, upper = _tri(c, True), _tri(c, False).astype(F32)
        lower_f = lower.astype(F32)
        last_row = lax.broadcasted_iota(jnp.int32, (c, LANES), 0) == c - 1
        hs, pairs = range(hp), [(cc, hh) for cc in range(cps) for hh in range(hp)]
        sls = [slice(hh * LANES, (hh + 1) * LANES) for hh in hs]
        rws = [slice(cc * c, (cc + 1) * c) for cc in range(cps)]
        lb = [_lower_bound(lb_ref[:, sl]) for sl in sls]
        zq_v = {p: zq_ref[rws[p[0]], sls[p[1]]] for p in pairs}
        sq = {p: _sigmoid(zq_v[p]) for p in pairs}
        q = {p: zq_v[p] * sq[p] for p in pairs}
        sf = {p: _sigmoid(zf_ref[rws[p[0]], sls[p[1]]]) for p in pairs}
        f = {p: lb[p[1]] + (1.0 - lb[p[1]]) * sf[p] for p in pairs}
        k = {p: 1.0 - f[p] for p in pairs}
        v = {p: zi_ref[rws[p[0]], sls[p[1]]] for p in pairs}
        d_o = {p: do_ref[rws[p[0]], sls[p[1]]] for p in pairs}
        b = {p: _dot_f32(lower_f, jnp.log(f[p])) for p in pairs}
        s0t = {p: st_ref[p[1], p[0]] for p in pairs}
        for p in pairs:
            b_sc[p[0], p[1]] = b[p]
        bl = {p: b_sc[p[0], p[1], c - 1:c, :] for p in pairs}
        eb = {p: jnp.exp(b[p]) for p in pairs}
        ebl = {p: jnp.exp(bl[p]) for p in pairs}
        dec_end = {p: jnp.exp(bl[p] - b[p]) for p in pairs}
        da = {p: jnp.where(lower, _dot(d_o[p], v[p], _NT), 0.0) for p in pairs}
        dq = {p: _dot(d_o[p], s0t[p], _NN) * eb[p] for p in pairs}
        dstate_in = {p: _dot(d_o[p], q[p] * eb[p], _TN) for p in pairs}
        dk_intra = {p: jnp.zeros((c, LANES), F32) for p in pairs}
        scores, dq_blocks = {p: [] for p in pairs}, {p: [] for p in pairs}
        for i in range(c // HGRN_SUB):
            lo = i * HGRN_SUB
            for p in pairs:
                ref = b_sc[p[0], p[1], lo - 1:lo, :] if i > 0 else jnp.zeros((1, LANES), F32)
                grow = jnp.exp(b[p][lo:lo + HGRN_SUB, :] - ref)
                qt = q[p][lo:lo + HGRN_SUB, :] * grow
                dec = jnp.exp(jnp.minimum(ref - b[p], EXP_CLAMP))
                kd = k[p] * dec
                scores[p].append(_dot(qt, kd, _NT))
                da_i = da[p][lo:lo + HGRN_SUB, :]
                dq_blocks[p].append(_dot_f32(da_i, kd, _NN) * grow)
                dk_intra[p] = dk_intra[p] + _dot_f32(da_i, qt, _TN) * dec
        dv_intra = {p: _dot(jnp.where(lower, jnp.concatenate(scores[p], axis=0), 0.0), d_o[p], _TN) for p in pairs}
        dq = {p: dq[p] + jnp.concatenate(dq_blocks[p], axis=0) for p in pairs}
        q_dq = {p: q[p] * dq[p] for p in pairs}
        for p in pairs:
            dzq_ref[rws[p[0]], sls[p[1]]] = (dq[p] * sq[p] * (1.0 + zq_v[p] * (1.0 - sq[p]))).astype(BF16)
        dstate = [dstate_sc[hh] for hh in hs]
        for cc in reversed(range(cps)):
            ps = [(cc, hh) for hh in hs]
            dk_state = [_dot(v[p], dstate[p[1]], _NN) * dec_end[p] for p in ps]
            dv = [dv_intra[p] + _dot(k[p] * dec_end[p], dstate[p[1]], _NT) for p in ps]
            dk = [dk_intra[p] + dk_state[p[1]] for p in ps]
            db_last = [jnp.sum(k[p] * dk_state[p[1]], axis=0, keepdims=True)
                       + ebl[p] * jnp.sum(s0t[p] * dstate[p[1]], axis=0, keepdims=True) for p in ps]
            db = [q_dq[p] - k[p] * dk[p[1]] + jnp.where(last_row, db_last[p[1]], 0.0) for p in ps]
            df = [_dot_f32(upper, db[p[1]]) / f[p] - dk[p[1]] for p in ps]
            for p in ps:
                hh = p[1]
                dzf_ref[rws[cc], sls[hh]] = (df[hh] * (1.0 - lb[hh]) * sf[p] * (1.0 - sf[p])).astype(BF16)
                dlb_ref[:, sls[hh]] += jnp.sum(df[hh] * (1.0 - sf[p]), axis=0, keepdims=True)
                dzi_ref[rws[cc], sls[hh]] = dv[hh].astype(BF16)
            dstate = [dstate[p[1]] * ebl[p] + dstate_in[p] for p in ps]
        for hh in hs:
            dstate_sc[hh] = dstate[hh]

    tile = pl.BlockSpec((cps * c, hp * LANES), lambda h, i: (n_steps - 1 - i, h))
    out = jax.ShapeDtypeStruct((s, d), BF16)
    return pl.pallas_call(
        body, name=name, grid=(h_n // hp, n_steps),
        in_specs=[tile, tile, tile, pl.BlockSpec((2, hp * LANES), lambda h, i: (0, h)),
                  pl.BlockSpec((hp, cps, LANES, LANES), lambda h, i: (h, n_steps - 1 - i, 0, 0)), tile,
                  pl.BlockSpec(memory_space=pl.ANY)],
        out_specs=[tile, tile, tile, pl.BlockSpec((1, hp * LANES), lambda h, i: (0, h))],
        out_shape=[out, out, out, jax.ShapeDtypeStruct((1, d), F32)],
        scratch_shapes=[pltpu.VMEM((hp, LANES, LANES), F32), pltpu.VMEM((cps, hp, c, LANES), F32)],
        compiler_params=_params("parallel", "arbitrary"),
    )(zq, zf, zi, lb_logits, states, do, after)


ATTN_SUB_ROWS = 256
LOG2E = 1.4426950408889634
LN2 = 0.6931471805599453
Q_PRESCALE = ATTN_SCALE * LOG2E


def _attn_tile(s):
    return min(1024, max(128, s // 2))


def _causal_pairs(n, q_major):
    pairs = [(i, j) for i in range(n) for j in range(i + 1)] if q_major else [(i, j) for j in range(n) for i in range(j, n)]
    return jnp.asarray([p[0] for p in pairs], jnp.int32), jnp.asarray([p[1] for p in pairs], jnp.int32)


def _sub_scores(qn_ref, qr_ref, k, r, sub, t, diagonal):
    q = jnp.concatenate([qn_ref[r:r + sub, :], qr_ref[r:r + sub, :]], axis=1)
    if not diagonal:
        return q, _dot(q, k, _NT)
    cols = r + sub
    keep = lax.broadcasted_iota(jnp.int32, (sub, cols), 1) <= r + lax.broadcasted_iota(jnp.int32, (sub, cols), 0)
    return q, jnp.where(keep, _dot(q, k[:cols], _NT), -jnp.inf)


def _attn_fwd(qn, qr, kn, kr, v, *, name):
    s, t = qn.shape[0], _attn_tile(qn.shape[0])
    sub = min(t, ATTN_SUB_ROWS)
    q_blk, k_blk = _causal_pairs(s // t, True)

    def body(qi_ref, kj_ref, qn_ref, qr_ref, kn_ref, kr_ref, v_ref, o_ref, lse_ref, m_sc, l_sc, acc_sc):
        p_id = pl.program_id(1)
        i, j = qi_ref[p_id], kj_ref[p_id]

        @pl.when(j == 0)
        def _():
            m_sc[...] = jnp.full_like(m_sc, -jnp.inf)
            l_sc[...] = jnp.zeros_like(l_sc)
            acc_sc[...] = jnp.zeros_like(acc_sc)

        def update(diagonal):
            k = jnp.concatenate([kn_ref[...], kr_ref[...]], axis=1)
            v = v_ref[...]
            starts = list(range(0, t, sub))
            scs = [_sub_scores(qn_ref, qr_ref, k, r, sub, t, diagonal)[1] for r in starts]
            ps, alphas = [], []
            for r, sc in zip(starts, scs):
                m_prev = m_sc[r:r + sub, :]
                m_new = jnp.maximum(m_prev, jnp.max(sc, axis=1, keepdims=True))
                alpha = jnp.exp2(m_prev - m_new)
                p = jnp.exp2(sc - m_new[:, :1])
                l_sc[r:r + sub, :] = alpha * l_sc[r:r + sub, :] + jnp.sum(p, axis=1, keepdims=True)
                m_sc[r:r + sub, :] = m_new
                ps.append(p)
                alphas.append(alpha)
            for r, p, alpha in zip(starts, ps, alphas):
                acc_sc[r:r + sub, :] = alpha * acc_sc[r:r + sub, :] + _dot(p, v[:p.shape[1]], _NN)

        @pl.when(j < i)
        def _():
            update(False)

        @pl.when(j == i)
        def _():
            update(True)
            o_ref[...] = (acc_sc[...] / l_sc[...]).astype(BF16)
            lse_ref[...] = m_sc[...] + jnp.log(l_sc[...]) * LOG2E

    q_spec = pl.BlockSpec((t, LANES), lambda h, p, qi, kj: (qi[p], h))
    k_spec = pl.BlockSpec((t, LANES), lambda h, p, qi, kj: (kj[p], h))
    kr_spec = pl.BlockSpec((t, LANES), lambda h, p, qi, kj: (kj[p], 0))
    stat = pltpu.VMEM((t, LANES), F32)
    return pl.pallas_call(
        body, name=name,
        grid_spec=pltpu.PrefetchScalarGridSpec(
            num_scalar_prefetch=2, grid=(MLA_HEADS, q_blk.shape[0]),
            in_specs=[q_spec, q_spec, k_spec, kr_spec, k_spec], out_specs=[q_spec, q_spec],
            scratch_shapes=[stat, stat, stat]),
        out_shape=[jax.ShapeDtypeStruct(qn.shape, BF16), jax.ShapeDtypeStruct(qn.shape, F32)],
        compiler_params=_params("parallel", "arbitrary"),
    )(q_blk, k_blk, qn, qr, kn, kr, v)


def _attn_bwd(qn, qr, kn, kr, v, do, lse, delta, *, name):
    s, t = qn.shape[0], _attn_tile(qn.shape[0])
    n, sub = s // t, min(t, ATTN_SUB_ROWS)
    q_blk, k_blk = _causal_pairs(n, False)

    def body(qi_ref, kj_ref, qn_ref, qr_ref, kn_ref, kr_ref, v_ref, do_ref, lse_ref, delta_ref,
             dqn_ref, dqr_ref, dkn_ref, dv_ref, dkr_ref, dk_sc, dv_sc):
        p_id = pl.program_id(1)
        i, j = qi_ref[p_id], kj_ref[p_id]

        @pl.when(p_id == 0)
        def _():
            dqn_ref[...] = jnp.zeros_like(dqn_ref)
            dqr_ref[...] = jnp.zeros_like(dqr_ref)

        @pl.when(i == j)
        def _():
            dk_sc[...] = jnp.zeros_like(dk_sc)
            dv_sc[...] = jnp.zeros_like(dv_sc)

        def accumulate(diagonal):
            k = jnp.concatenate([kn_ref[...], kr_ref[...]], axis=1)
            v = v_ref[...]
            starts = list(range(0, t, sub))
            qs, d_os, scs, dps = [], [], [], []
            for r in starts:
                q, sc = _sub_scores(qn_ref, qr_ref, k, r, sub, t, diagonal)
                d_o = do_ref[r:r + sub, :]
                qs.append(q)
                d_os.append(d_o)
                scs.append(sc)
                dps.append(_dot(d_o, v[:sc.shape[1]], _NT))
            ps, dss = [], []
            for r, sc, dp in zip(starts, scs, dps):
                p = jnp.exp2(sc - lse_ref[r:r + sub, :][:, :1])
                ps.append(p.astype(BF16))
                dss.append((p * (dp - delta_ref[r:r + sub, :][:, :1])).astype(BF16))
            for r, q, d_o, p, ds in zip(starts, qs, d_os, ps, dss):
                cols = p.shape[1]
                dv_sc[:cols, :] += _dot(p, d_o, _TN)
                dk_sc[:cols, :] += _dot(ds, q, _TN)
                dq = _dot(ds, k[:cols], _NN) * ATTN_SCALE
                rows = pl.ds(pl.multiple_of(i * t + r, sub), sub)
                dqn_ref[rows, :] += dq[:, :LANES]
                dqr_ref[rows, :] += dq[:, LANES:]

        @pl.when(j < i)
        def _():
            accumulate(False)

        @pl.when(j == i)
        def _():
            accumulate(True)

        @pl.when(i == n - 1)
        def _():
            dkn_ref[...] = (dk_sc[:, :LANES] * LN2).astype(BF16)
            dkr_ref[...] = dk_sc[:, LANES:] * LN2
            dv_ref[...] = dv_sc[...].astype(BF16)

    q_spec = pl.BlockSpec((t, LANES), lambda h, p, qi, kj: (qi[p], h))
    k_spec = pl.BlockSpec((t, LANES), lambda h, p, qi, kj: (kj[p], h))
    kr_spec = pl.BlockSpec((t, LANES), lambda h, p, qi, kj: (kj[p], 0))
    head_spec = pl.BlockSpec((s, LANES), lambda h, p, qi, kj: (0, h))
    f32_out, bf16_out = jax.ShapeDtypeStruct(qn.shape, F32), jax.ShapeDtypeStruct(qn.shape, BF16)
    return pl.pallas_call(
        body, name=name,
        grid_spec=pltpu.PrefetchScalarGridSpec(
            num_scalar_prefetch=2, grid=(MLA_HEADS, q_blk.shape[0]),
            in_specs=[q_spec, q_spec, k_spec, kr_spec, k_spec, q_spec, q_spec, q_spec],
            out_specs=[head_spec, head_spec, k_spec, k_spec, k_spec],
            scratch_shapes=[pltpu.VMEM((t, 2 * LANES), F32), pltpu.VMEM((t, LANES), F32)]),
        out_shape=[f32_out, f32_out, bf16_out, bf16_out, f32_out],
        compiler_params=_params("parallel", "arbitrary"),
    )(q_blk, k_blk, qn, qr, kn, kr, v, do, lse, delta)


def _exchange(arrs, *, scatter, name):
    n = len(arrs)
    out_shape = [jax.ShapeDtypeStruct(a.shape if scatter else (N_DEV, *a.shape), a.dtype) for a in arrs]

    def body(*refs):
        ins, outs = refs[:n], refs[n:2 * n]
        send_sems, recv_sems, local_sems = refs[2 * n:]
        x, y, c = lax.axis_index("x"), lax.axis_index("y"), lax.axis_index("c")
        me = 4 * x + 2 * y + c
        copies = []
        for k in range(n):
            local = pltpu.make_async_copy(ins[k].at[me] if scatter else ins[k], outs[k].at[me], local_sems.at[k])
            local.start()
            copies.append(local)
            for d in range(1, N_DEV):
                px, py, pc = (x + (d >> 2)) % 2, (y + ((d >> 1) & 1)) % 2, (c + (d & 1)) % 2
                peer = 4 * px + 2 * py + pc
                remote = pltpu.make_async_remote_copy(
                    src_ref=ins[k].at[peer] if scatter else ins[k], dst_ref=outs[k].at[me],
                    send_sem=send_sems.at[k, d - 1], recv_sem=recv_sems.at[k, d - 1],
                    device_id=(px, py, pc), device_id_type=pl.DeviceIdType.MESH)
                remote.start()
                copies.append(remote)
        for cp in copies:
            cp.wait()

    any_spec = pl.BlockSpec(memory_space=pl.ANY)
    return pl.pallas_call(
        body, name=name, in_specs=[any_spec] * n, out_specs=[any_spec] * n, out_shape=out_shape,
        scratch_shapes=[pltpu.SemaphoreType.DMA((n, N_DEV - 1)), pltpu.SemaphoreType.DMA((n, N_DEV - 1)),
                        pltpu.SemaphoreType.DMA((n,))],
    )(*arrs)


def _peers(x, y, c):
    out = []
    for d in range(1, N_DEV):
        px, py, pc = (x + (d >> 2)) % 2, (y + ((d >> 1) & 1)) % 2, (c + (d & 1)) % 2
        out.append(((px, py, pc), 4 * px + 2 * py + pc))
    return out


def _exchange_copies(ins, lands, send_sems, recv_sems, scatter):
    x, y, c = lax.axis_index("x"), lax.axis_index("y"), lax.axis_index("c")
    me = 4 * x + 2 * y + c
    local, remote = [], []
    for k in range(len(ins)):
        local.append(pltpu.make_async_copy(ins[k].at[me] if scatter else ins[k], lands[k].at[me],
                                           recv_sems.at[k * N_DEV + N_DEV - 1]))
        for d, (coords, peer) in enumerate(_peers(x, y, c)):
            remote.append(pltpu.make_async_remote_copy(
                src_ref=ins[k].at[peer] if scatter else ins[k], dst_ref=lands[k].at[me],
                send_sem=send_sems.at[k * N_DEV + d], recv_sem=recv_sems.at[k * N_DEV + d],
                device_id=coords, device_id_type=pl.DeviceIdType.MESH))
    return local, remote


def _exchange_start(arrs, *, scatter, name, after=None):
    n = len(arrs)
    hbm = pl.BlockSpec(memory_space=pltpu.HBM)
    sem = pl.BlockSpec(memory_space=pltpu.SEMAPHORE)
    lands = [lax.empty(a.shape if scatter else (N_DEV, *a.shape), a.dtype) for a in arrs]

    def body(*refs):
        ins, land_refs = refs[:n], refs[n:2 * n]
        first_out = 2 * n + (after is not None)
        send_sems, recv_sems, token = refs[first_out], refs[first_out + 1], refs[-1]
        local, remote = _exchange_copies(ins, land_refs, send_sems, recv_sems, scatter)
        for cp in local + remote:
            cp.start()
        token[...] = jnp.zeros_like(token)

    operands = [pltpu.with_memory_space_constraint(a, pltpu.HBM) for a in list(arrs) + lands]
    behind = [] if after is None else [after]
    res = pl.pallas_call(
        body, name=name,
        out_shape=(pltpu.SemaphoreType.DMA((n * N_DEV,)), pltpu.SemaphoreType.DMA((n * N_DEV,)),
                   *[pltpu.HBM(o.shape, o.dtype) for o in operands], jax.ShapeDtypeStruct((8, LANES), F32)),
        in_specs=[hbm] * (2 * n) + [pl.BlockSpec(memory_space=pl.ANY)] * len(behind),
        out_specs=(sem, sem, *[hbm] * (2 * n), pl.BlockSpec(memory_space=pltpu.VMEM)),
        input_output_aliases={i: 2 + i for i in range(2 * n)},
        compiler_params=pltpu.CompilerParams(has_side_effects=pltpu.SideEffectType.DATAFLOW_SIDE_EFFECTING),
    )(*operands, *behind)
    return (res[0], res[1], list(res[2:2 + n]), list(res[2 + n:2 + 2 * n]), scatter), res[-1]


def _exchange_wait(state, after, *, name):
    send_sems, recv_sems, ins, lands, scatter = state
    n = len(ins)
    hbm = pl.BlockSpec(memory_space=pltpu.HBM)
    sem = pl.BlockSpec(memory_space=pltpu.SEMAPHORE)

    def body(*refs):
        in_refs, land_refs = refs[:n], refs[n:2 * n]
        local, remote = _exchange_copies(in_refs, land_refs, refs[2 * n], refs[2 * n + 1], scatter)
        for cp in local:
            cp.wait()
        for cp in remote:
            cp.wait_send()
            cp.wait_recv()

    res = pl.pallas_call(
        body, name=name, out_shape=tuple(pltpu.HBM(o.shape, o.dtype) for o in ins + lands),
        in_specs=[hbm] * (2 * n) + [sem, sem, pl.BlockSpec(memory_space=pl.ANY)], out_specs=tuple([hbm] * (2 * n)),
        input_output_aliases={i: i for i in range(2 * n)},
        compiler_params=pltpu.CompilerParams(has_side_effects=pltpu.SideEffectType.DATAFLOW_SIDE_EFFECTING),
    )(*ins, *lands, send_sems, recv_sems, after)
    return list(res[n:])


def _adam(w, terms, m, v, *, name):
    n_layers, r, c = w.shape
    tr = min(r, 128)
    assert r % tr == 0 and len(terms) == n_layers
    steps = r // tr

    def body(w_ref, *rest):
        t_refs, (m_ref, v_ref, g_out, d_out, m_out, v_out) = rest[:n_layers], rest[n_layers:]
        for layer, t_ref in enumerate(t_refs):
            @pl.when(pl.program_id(0) == layer)
            def _(t_ref=t_ref):
                g = t_ref[0].astype(F32)
                for s in range(1, t_ref.shape[0]):
                    g = g + t_ref[s].astype(F32)
                m1 = ADAM_B1 * m_ref[...] + (1.0 - ADAM_B1) * g
                v1 = ADAM_B2 * v_ref[...] + (1.0 - ADAM_B2) * jnp.square(g)
                m_hat = m1 / (1.0 - ADAM_B1 ** ADAM_STEP)
                v_hat = v1 / (1.0 - ADAM_B2 ** ADAM_STEP)
                g_out[...] = g
                d_out[...] = -ADAM_LR * (m_hat / (jnp.sqrt(v_hat) + ADAM_EPS) + ADAM_WD * w_ref[...])
                m_out[...] = m1
                v_out[...] = v1

    def term_spec(layer, t):
        return pl.BlockSpec((t.shape[0], tr, c),
                            lambda l, i: (0, jnp.where(l == layer, i, jnp.where(l < layer, 0, steps - 1)), 0))

    spec = pl.BlockSpec((None, tr, c), lambda l, i: (l, i, 0))
    out = jax.ShapeDtypeStruct(w.shape, F32)
    return pl.pallas_call(
        body, name=name, grid=(n_layers, steps),
        in_specs=[spec] + [term_spec(layer, t) for layer, t in enumerate(terms)] + [spec, spec], out_specs=[spec] * 4,
        out_shape=[out] * 4, compiler_params=_params("arbitrary", "arbitrary"),
    )(w, *terms, m, v)


def _sum_terms(terms, *, name):
    n, _, p = terms.shape

    def body(t_ref, o_ref):
        acc = t_ref[0]
        for s in range(1, n):
            acc = acc + t_ref[s]
        o_ref[...] = acc

    return pl.pallas_call(body, name=name, out_shape=jax.ShapeDtypeStruct((1, p), F32))(terms)


def _lb_logits_grad(dlb, logits, *, name):
    def body(dlb_ref, l_ref, o_ref):
        lb = _lower_bound(l_ref[...])
        d0 = dlb_ref[...] * lb * (1.0 - lb)
        o_ref[...] = jnp.concatenate([d0, -d0], axis=0)

    return pl.pallas_call(body, name=name, out_shape=jax.ShapeDtypeStruct(logits.shape, F32))(dlb, logits)


def _silu_grad(z):
    sg = _sigmoid(z)
    return sg * (1.0 + z * (1.0 - sg))


def _head_norm_gate(o, zg, gn):
    outs = []
    for h in range(HGRN_HEADS):
        sl = slice(h * LANES, (h + 1) * LANES)
        zg_h = zg[:, sl]
        outs.append(_rms(o[:, sl], gn) * (zg_h * _sigmoid(zg_h)))
    return (jnp.concatenate(outs, axis=1),)


def _head_norm_gate_bwd(o, zg, dm, gn):
    do_parts, dzg_parts, dgn = [], [], jnp.zeros((1, LANES), F32)
    for h in range(HGRN_HEADS):
        sl = slice(h * LANES, (h + 1) * LANES)
        o_h, zg_h, dm_h = o[:, sl], zg[:, sl], dm[:, sl]
        gate = zg_h * _sigmoid(zg_h)
        do_h, dgn_h = _rms_bwd(o_h, gn, dm_h * gate)
        dgn = dgn + dgn_h
        do_parts.append(do_h)
        dzg_parts.append(dm_h * _rms(o_h, gn) * _silu_grad(zg_h))
    return jnp.concatenate(do_parts, axis=1), jnp.concatenate(dzg_parts, axis=1), dgn


def _rope_slabs(x, t_c, t_s1, t_s2, transpose):
    fn = _rope_t if transpose else _rope
    return jnp.concatenate(
        [fn(x[:, h * LANES:(h + 1) * LANES], t_c, t_s1, t_s2) for h in range(x.shape[1] // LANES)], axis=1)


def _loss_head(h, tgt, w):
    d = h.shape[1]
    r = lax.rsqrt(jnp.mean(h * h, axis=-1, keepdims=True) + EPS)
    xh = h * r
    err = xh * w - tgt
    loss = 0.5 * jnp.sum(jnp.mean(err * err, axis=-1, keepdims=True), axis=0, keepdims=True)
    dy = err / d
    dxh = dy * w
    dh = r * (dxh - xh * jnp.mean(dxh * xh, axis=-1, keepdims=True))
    return dh, dh, jnp.sum(dy * xh, axis=0, keepdims=True), jnp.broadcast_to(loss, (1, LANES))


def _mlp_fwd(h, norm, w_up, w_down, tag, loss_head=None):
    d = h.shape[1]

    def up(x, g, wu):
        x_n = _rms(x, g).astype(BF16)
        return x_n, jnp.concatenate([jnp.square(jnp.maximum(_dot(x_n, wu[j], _NN), 0.0)) for j in range(wu.shape[0])],
                                    axis=1)

    xn, act = _rowcall(up, [h], [norm, w_up], [(d, BF16), (w_up.shape[0] * w_up.shape[2], BF16)], [], tr=512,
                       name=f"{tag}_up")
    if callable(w_down):
        w_down = w_down(act)
    if loss_head is None:
        return _mm(act, w_down, mode="nn", add=h, name=f"{tag}_down"), (h, xn, act)
    tgt, final_norm = loss_head

    def down_and_loss(a, res, t, wd, g):
        return _loss_head(res + _dot(a, wd, _NN), t, g)

    return _rowcall(down_and_loss, [act, h, tgt], [w_down, final_norm], [(d, F32), (d, BF16)], [d, LANES],
                    name=f"{tag}_down_loss"), (h, xn, act)


def _mlp_bwd(dh_out, dh_out_bf, saved, norm, w_up, w_down, tag, after=None):
    h, xn, act = saved
    d = h.shape[1]
    du = _mm(dh_out_bf, w_down, mode="nt", relu2_of=act, out_dtype=BF16, after=after, name=f"{tag}_bwd_du")
    dw_down = _mm(act, dh_out_bf, mode="tn", name=f"{tag}_bwd_wdown")
    dw_up = _mm(xn, du, mode="tn", col_shards=w_up.shape[0], name=f"{tag}_bwd_wup")

    def up_norm_bwd(x, d_u, dres, g, wu):
        cols = wu.shape[2]
        dxn = _dot(d_u[:, :cols], wu[0], _NT)
        for j in range(1, wu.shape[0]):
            dxn = dxn + _dot(d_u[:, j * cols:(j + 1) * cols], wu[j], _NT)
        dx, dw = _rms_bwd(x, g, dxn)
        return dx + dres, dx + dres, dw

    dh, dh_bf, dnorm = _rowcall(up_norm_bwd, [h, du, dh_out], [norm, w_up], [(d, F32), (d, BF16)], [d], tr=512,
                                name=f"{tag}_bwd_dxn")
    return dh, dh_bf, dnorm, dw_up, dw_down


def _row_major(g):
    return g.reshape(g.shape[0] * g.shape[1], g.shape[2])


def _col_major(g):
    return jnp.transpose(g, (1, 0, 2)).reshape(g.shape[1], g.shape[0] * g.shape[2])


def _col_terms(dw):
    k, n = dw.shape
    return jnp.transpose(dw.reshape(k, N_DEV, n // N_DEV), (1, 0, 2))


def _row_terms(dw):
    return dw.reshape(N_DEV, dw.shape[0] // N_DEV, dw.shape[1])


def kernel(x, hgrn_norm, hgrn_w_q, hgrn_w_f, hgrn_w_i, hgrn_w_g, hgrn_g_norm, hgrn_w_o, hgrn_lb_logits, mla_norm, mla_w_dq, mla_q_norm, mla_w_uq, mla_w_o, kv_in_norm, kv_w_dkv, kv_norm, kv_w_uk, kv_w_uv, mlp_norm, mlp_w_up, mlp_w_down, final_norm, loss_target, m_hgrn_norm, m_hgrn_w_q, m_hgrn_w_f, m_hgrn_w_i, m_hgrn_w_g, m_hgrn_g_norm, m_hgrn_w_o, m_hgrn_lb_logits, m_mla_norm, m_mla_w_dq, m_mla_q_norm, m_mla_w_uq, m_mla_w_o, m_kv_in_norm, m_kv_w_dkv, m_kv_norm, m_kv_w_uk, m_kv_w_uv, m_mlp_norm, m_mlp_w_up, m_mlp_w_down, m_final_norm, v_hgrn_norm, v_hgrn_w_q, v_hgrn_w_f, v_hgrn_w_i, v_hgrn_w_g, v_hgrn_g_norm, v_hgrn_w_o, v_hgrn_lb_logits, v_mla_norm, v_mla_w_dq, v_mla_q_norm, v_mla_w_uq, v_mla_w_o, v_kv_in_norm, v_kv_w_dkv, v_kv_norm, v_kv_w_uk, v_kv_w_uv, v_mlp_norm, v_mlp_w_up, v_mlp_w_down, v_final_norm):
    given = dict(locals())
    weight_names = ["hgrn_norm", "hgrn_w_q", "hgrn_w_f", "hgrn_w_i", "hgrn_w_g", "hgrn_g_norm", "hgrn_w_o",
                    "hgrn_lb_logits", "mla_norm", "mla_w_dq", "mla_q_norm", "mla_w_uq", "mla_w_o", "kv_in_norm",
                    "kv_w_dkv", "kv_norm", "kv_w_uk", "kv_w_uv", "mlp_norm", "mlp_w_up", "mlp_w_down", "final_norm"]
    me = 4 * lax.axis_index("x") + 2 * lax.axis_index("y") + lax.axis_index("c")
    xs, tgt = x[0], loss_target[0]
    seq, d_model = xs.shape
    n_heads, hd = MLA_HEADS, LANES

    big_local = {
        "hgrn_w_q": hgrn_w_q[0], "hgrn_w_f": hgrn_w_f[0], "hgrn_w_i": hgrn_w_i[0], "hgrn_w_g": hgrn_w_g[0],
        "hgrn_w_o": hgrn_w_o[0], "mla_w_dq": mla_w_dq[0], "mla_w_uq": mla_w_uq[0], "mla_w_o": mla_w_o[0],
        "kv_w_dkv": kv_w_dkv, "kv_w_uk": kv_w_uk, "kv_w_uv": kv_w_uv,
        "mlp_w_up0": mlp_w_up[0], "mlp_w_up1": mlp_w_up[1], "mlp_w_down0": mlp_w_down[0], "mlp_w_down1": mlp_w_down[1],
    }
    big_names = list(big_local)
    col_sharded = {"mla_w_uq", "kv_w_uk", "kv_w_uv"}
    shard_major = {"mlp_w_up0", "mlp_w_up1"}
    vec_local = jnp.concatenate([hgrn_norm, hgrn_lb_logits], axis=0)
    first_names = ["hgrn_w_q", "hgrn_w_f", "hgrn_w_i"]
    proj_names = first_names + ["hgrn_w_g"]
    later_names = {"hgrn_o": ["hgrn_w_g", "hgrn_w_o"], "up0": ["mlp_w_up0"], "down0": ["mlp_w_down0"],
                   "mla": ["kv_w_dkv", "kv_w_uk", "kv_w_uv", "mla_w_dq", "mla_w_uq", "mla_w_o"],
                   "mlp1": ["mlp_w_up1", "mlp_w_down1"]}

    def unshard(names, arrays):
        return {k: (a if k in shard_major else _col_major(a) if k in col_sharded else _row_major(a))
                for k, a in zip(names, arrays)}

    first_state, token = _exchange_start([big_local[k].astype(BF16) for k in first_names] + [vec_local], scatter=False,
                                         name="gather_first_start")
    gather_state = {}
    for tag, names in later_names.items():
        gather_state[tag], token = _exchange_start([big_local[k].astype(BF16) for k in names], scatter=False,
                                                   after=token, name=f"gather_{tag}_start")

    def gather_wait(tag, after):
        w.update(unshard(later_names[tag], _exchange_wait(gather_state[tag], after, name=f"gather_{tag}_wait")))
        return [w[k] for k in later_names[tag]]

    gathered = _exchange_wait(first_state, token, name="gather_first_wait")
    w = unshard(first_names, gathered[:-1])
    vec_full = jnp.transpose(gathered[-1], (1, 0, 2)).reshape(3, d_model)
    hgrn_norm_full, lb_logits_full = vec_full[0:1], vec_full[1:3]
    t_c, t_s1, t_s2 = _rope_tables(seq)
    kv_lora = kv_w_uk.shape[0]

    def hgrn_proj(a, g, *weights):
        xn = _rms(a, g).astype(BF16)
        return (xn, *[_dot(xn, wt, _NN) for wt in weights])

    xn0, zq, zf, zi = _rowcall(hgrn_proj, [xs], [hgrn_norm_full] + [w[k] for k in first_names],
                               [(d_model, BF16)] + [(d_model, F32)] * 3, [], tr=512, name="hgrn_proj")
    o_rec, states = _hgrn_fwd(zq, zf, zi, lb_logits_full, name="hgrn_fwd")
    gather_wait("hgrn_o", o_rec)

    def gate_out(o, x_n, res, gn, wg, wo):
        z = _dot(x_n, wg, _NN)
        m = _head_norm_gate(o, z, gn)[0].astype(BF16)
        return z, m, res + _dot(m, wo, _NN)

    zg, mixed, h1 = _rowcall(gate_out, [o_rec, xn0, xs], [hgrn_g_norm, w["hgrn_w_g"], w["hgrn_w_o"]],
                             [(d_model, F32), (d_model, BF16), (d_model, F32)], [], name="hgrn_gate_out")
    h2, mlp0_saved = _mlp_fwd(h1, mlp_norm[0:1], gather_wait("up0", h1)[0], lambda act: gather_wait("down0", act)[0],
                              "mlp0")
    gather_wait("mla", h2)
    w_uq3 = w["mla_w_uq"].reshape(-1, n_heads, MLA_NOPE + MLA_ROPE)
    w_uq_nope = w_uq3[:, :, :MLA_NOPE].reshape(-1, n_heads * hd)
    w_uq_rope = jnp.pad(w_uq3[:, :, MLA_NOPE:], ((0, 0), (0, 0), (0, hd - MLA_ROPE))).reshape(-1, n_heads * hd)
    w_dkv_pad = jnp.pad(w["kv_w_dkv"], ((0, 0), (0, kv_lora + hd - w["kv_w_dkv"].shape[1])))

    q_lora, qk_cols = w["mla_w_dq"].shape[1], n_heads * hd

    def mla_qkv(a, tc, ts1, ts2, g_kv_in, g_mla, g_q, g_kv, wdq, wn, wr, wdkv, wuk, wuv):
        h_n, x_n = _rms(a, g_kv_in).astype(BF16), _rms(a, g_mla).astype(BF16)
        cq = _dot(x_n, wdq, _NN)
        cq_n = _rms(cq, g_q).astype(BF16)
        q_nope = _dot(cq_n, wn, _NN) * Q_PRESCALE
        q_rope = _rope_slabs(_dot(cq_n, wr, _NN) * Q_PRESCALE, tc, ts1, ts2, False)
        c_all = _dot(h_n, wdkv, _NN)
        lat = _rms(c_all[:, :kv_lora], g_kv).astype(BF16)
        return (h_n, x_n, cq, cq_n, q_nope, q_rope, c_all, lat, _rope(c_all[:, kv_lora:], tc, ts1, ts2),
                _dot(lat, wuk, _NN), _dot(lat, wuv, _NN))

    hn, xn2, cq_pre, c_q, qn, qr, ckr, c_kv, kr, kn, vv = _rowcall(
        mla_qkv, [h2, t_c, t_s1, t_s2],
        [kv_in_norm[None, :], mla_norm, mla_q_norm, kv_norm[None, :], w["mla_w_dq"], w_uq_nope, w_uq_rope, w_dkv_pad,
         w["kv_w_uk"], w["kv_w_uv"]],
        [(d_model, BF16), (d_model, BF16), (q_lora, F32), (q_lora, BF16), (qk_cols, BF16), (qk_cols, BF16),
         (kv_lora + hd, F32), (kv_lora, BF16), (hd, BF16), (qk_cols, BF16), (qk_cols, BF16)], [], tr=512, name="mla_qkv")
    o_att, lse = _attn_fwd(qn, qr, kn, kr, vv, name="attn_fwd")
    h3 = _mm(o_att, w["mla_w_o"], mode="nn", add=h2, name="attn_out")
    gather_wait("mlp1", h3)
    (dh4, dh4_bf, g_final_norm, loss_part), mlp1_saved = _mlp_fwd(
        h3, mlp_norm[1:2], w["mlp_w_up1"], w["mlp_w_down1"], "mlp1", loss_head=(tgt, final_norm[None, :]))

    g = {}
    groups = {"mlp1": ["mlp_w_up1", "mlp_w_down1"],
              "mla": ["mla_w_o", "mla_w_uq", "mla_w_dq", "kv_w_uk", "kv_w_uv", "kv_w_dkv"],
              "mlp0": ["mlp_w_up0", "mlp_w_down0"],
              "hgrn_out": ["hgrn_w_o", "hgrn_w_g"],
              "hgrn_in": ["hgrn_w_q", "hgrn_w_f", "hgrn_w_i"]}
    scatter_state = {}

    def scatter_start(tag, after=None):
        scatter_state[tag], tok = _exchange_start(
            [g[k] if k in shard_major else (_col_terms if k in col_sharded else _row_terms)(g[k]) for k in groups[tag]],
            scatter=True, after=after,
            name=f"scatter_{tag}_start")
        return tok

    dh3, dh3_bf, g_mlp_norm1, g["mlp_w_up1"], g["mlp_w_down1"] = _mlp_bwd(
        dh4, dh4_bf, mlp1_saved, mlp_norm[1:2], w["mlp_w_up1"], w["mlp_w_down1"], "mlp1")
    def attn_out_bwd(dres, o, wo):
        d_o = _dot(dres, wo, _NT).astype(BF16)
        prod = d_o.astype(F32) * o.astype(F32)
        return d_o, jnp.concatenate([jnp.broadcast_to(jnp.sum(prod[:, h * hd:(h + 1) * hd], axis=1, keepdims=True),
                                                      (prod.shape[0], hd)) for h in range(n_heads)], axis=1)

    d_oatt, delta = _rowcall(attn_out_bwd, [dh3_bf, o_att], [w["mla_w_o"]], [(qk_cols, BF16), (qk_cols, F32)], [],
                             after=scatter_start("mlp1"), name="attn_out_bwd_x")
    g["mla_w_o"] = _mm(o_att, dh3_bf, mode="tn", name="attn_out_bwd_w")
    dqn, dqr, dkn, dvv, dkr = _attn_bwd(qn, qr, kn, kr, vv, d_oatt, lse, delta, name="attn_bwd")

    def q_path_bwd(cq, cq_n, x_n, d_qn, d_qr, tc, ts1, ts2, g_q, wdq, wn, wr):
        d_qn, d_qr = d_qn.astype(BF16), _rope_slabs(d_qr, tc, ts1, ts2, True).astype(BF16)
        d_cq, d_gq = _rms_bwd(cq, g_q, _dot(d_qn, wn, _NT) + _dot(d_qr, wr, _NT))
        d_cq = d_cq.astype(BF16)
        return _dot(d_cq, wdq, _NT), d_gq, _dot(x_n, d_cq, _TN), _dot(cq_n, d_qn, _TN), _dot(cq_n, d_qr, _TN)

    dxn2, g_q_norm, g_dq, g_uq_nope, g_uq_rope = _rowcall(
        q_path_bwd, [cq_pre, c_q, xn2, dqn, dqr, t_c, t_s1, t_s2], [mla_q_norm, w["mla_w_dq"], w_uq_nope, w_uq_rope],
        [(d_model, F32)], [q_lora, (d_model, q_lora), (q_lora, qk_cols), (q_lora, qk_cols)], tr=512, name="mla_q_bwd")
    g["mla_w_dq"] = g_dq.astype(GRAD_WIRE_DTYPE)
    g["mla_w_uq"] = jnp.concatenate([g_uq_nope.reshape(q_lora, n_heads, hd),
                                     g_uq_rope.reshape(q_lora, n_heads, hd)[:, :, :MLA_ROPE]],
                                    axis=2).reshape(q_lora, -1).astype(GRAD_WIRE_DTYPE)

    def kv_path_bwd(c_all, lat, h_n, d_kn, d_v, d_kr_heads, tc, ts1, ts2, a, d_xn2, dres,
                    g_kv, g_kv_in, g_mla, wdkv, wuk, wuv):
        d_lat, d_gkv = _rms_bwd(c_all[:, :kv_lora], g_kv, _dot(d_kn, wuk, _NT) + _dot(d_v, wuv, _NT))
        d_kr = d_kr_heads[:, :hd]
        for h in range(1, n_heads):
            d_kr = d_kr + d_kr_heads[:, h * hd:(h + 1) * hd]
        d_all = jnp.concatenate([d_lat, _rope_t(d_kr, tc, ts1, ts2)], axis=1).astype(BF16)
        dx1, d_gkv_in = _rms_bwd(a, g_kv_in, _dot(d_all, wdkv, _NT))
        dx2, d_gmla = _rms_bwd(a, g_mla, d_xn2)
        d_a = dx1 + dx2 + dres
        return (d_a, d_a, d_gkv, d_gkv_in, d_gmla, _dot(h_n, d_all, _TN), _dot(lat, d_kn, _TN), _dot(lat, d_v, _TN))

    dh2, dh2_bf, g_kv_norm, g_kv_in_norm, g_mla_norm, g_dkv, g_uk, g_uv = _rowcall(
        kv_path_bwd, [ckr, c_kv, hn, dkn, dvv, dkr, t_c, t_s1, t_s2, h2, dxn2, dh3],
        [kv_norm[None, :], kv_in_norm[None, :], mla_norm, w_dkv_pad, w["kv_w_uk"], w["kv_w_uv"]],
        [(d_model, F32), (d_model, BF16)],
        [kv_lora, d_model, d_model, (d_model, kv_lora + hd), (kv_lora, qk_cols), (kv_lora, qk_cols)], name="mla_kv_bwd")
    g["kv_w_dkv"] = g_dkv[:, :kv_w_dkv.shape[1]].astype(GRAD_WIRE_DTYPE)
    g["kv_w_uk"], g["kv_w_uv"] = g_uk.astype(GRAD_WIRE_DTYPE), g_uv.astype(GRAD_WIRE_DTYPE)
    dh1, dh1_bf, g_mlp_norm0, g["mlp_w_up0"], g["mlp_w_down0"] = _mlp_bwd(
        dh2, dh2_bf, mlp0_saved, mlp_norm[0:1], w["mlp_w_up0"], w["mlp_w_down0"], "mlp0", after=scatter_start("mla"))

    g["hgrn_w_o"] = _mm(mixed, dh1_bf, mode="tn", after=scatter_start("mlp0"), name="hgrn_out_bwd_w")
    do_rec, dzg, g_g_norm = _rowcall(
        lambda dres, o, z, wo, gn: _head_norm_gate_bwd(o, z, _dot(dres, wo, _NT), gn), [dh1_bf, o_rec, zg],
        [w["hgrn_w_o"], hgrn_g_norm], [(d_model, F32), (d_model, BF16)], [hd], name="hgrn_gate_out_bwd")
    g["hgrn_w_g"] = _mm(xn0, dzg, mode="tn", name="hgrn_w_g_bwd_w")
    dzq, dzf, dzi, g_lb = _hgrn_bwd(zq, zf, zi, lb_logits_full, states, do_rec, scatter_start("hgrn_out"),
                                    name="hgrn_bwd")
    for nm, dz in (("hgrn_w_q", dzq), ("hgrn_w_f", dzf), ("hgrn_w_i", dzi)):
        g[nm] = _mm(xn0, dz, mode="tn", name=f"{nm}_bwd_w")

    def hgrn_proj_bwd(a, dres, *rest):
        dzs, gw, weights = rest[:4], rest[4], rest[5:]
        dxn = _dot(dzs[0], weights[0], _NT)
        for dz, wt in zip(dzs[1:], weights[1:]):
            dxn = dxn + _dot(dz, wt, _NT)
        dx, dw = _rms_bwd(a, gw, dxn)
        return dx + dres, dw

    grad_x, g_hgrn_norm = _rowcall(hgrn_proj_bwd, [xs, dh1, dzq, dzf, dzi, dzg],
                                   [hgrn_norm_full] + [w[k] for k in proj_names], [(d_model, F32)], [d_model],
                                   tr=512, name="hgrn_proj_bwd")

    small_parts = [g_hgrn_norm, g_lb, g_g_norm, g_mla_norm, g_q_norm, g_kv_in_norm, g_kv_norm, g_mlp_norm0,
                   g_mlp_norm1, g_final_norm, loss_part]
    small_sizes = [p.shape[1] for p in small_parts]
    small_terms = _exchange([jnp.concatenate(small_parts, axis=1)], scatter=False, name="gather_small")[0]
    small_sum = _sum_terms(small_terms, name="sum_small")
    last = scatter_start("hgrn_in", after=small_sum)
    offs = [0]
    for sz in small_sizes:
        offs.append(offs[-1] + sz)
    (s_hgrn_norm, s_lb, s_g_norm, s_mla_norm, s_q_norm, s_kv_in_norm, s_kv_norm, s_mlp_norm0, s_mlp_norm1, s_final_norm,
     s_loss) = [small_sum[:, a:b] for a, b in zip(offs[:-1], offs[1:])]
    shard = hgrn_norm.shape[1]
    g_lb_logits = _lb_logits_grad(lax.dynamic_slice_in_dim(s_lb, me * shard, shard, axis=1), hgrn_lb_logits,
                                  name="lb_logits_grad")
    loss = s_loss[0, 0]

    res, layer_terms = {}, {}

    def update(k, term_list):
        shape = given[k].shape
        as_layers = (len(term_list), shape[-2], shape[-1])
        upd = _adam(given[k].reshape(as_layers), term_list, given["m_" + k].reshape(as_layers),
                    given["v_" + k].reshape(as_layers), name=f"adam_{k}")
        res[k] = [o.reshape(shape) for o in upd]
        return upd[0]

    for tag, names in groups.items():
        for k, t in zip(names, _exchange_wait(scatter_state[tag], last, name=f"scatter_{tag}_wait")):
            if k.startswith("mlp_w_"):
                layer_terms.setdefault(k[:-1], {})[int(k[-1])] = t
                if len(layer_terms[k[:-1]]) == 2:
                    last = update(k[:-1], [layer_terms[k[:-1]][0], layer_terms[k[:-1]][1]])
            else:
                last = update(k, [t])

    small_grads = {
        "hgrn_norm": lax.dynamic_slice_in_dim(s_hgrn_norm, me * shard, shard, axis=1),
        "hgrn_g_norm": s_g_norm, "hgrn_lb_logits": g_lb_logits, "mla_norm": s_mla_norm, "mla_q_norm": s_q_norm,
        "kv_in_norm": s_kv_in_norm, "kv_norm": s_kv_norm,
        "mlp_norm": jnp.concatenate([s_mlp_norm0, s_mlp_norm1], axis=0), "final_norm": s_final_norm,
    }
    small_names = list(small_grads)

    def flat(a):
        return a.reshape(1, -1)

    packed = [jnp.concatenate([flat(src[pre + k]) for k in small_names], axis=1)
              for src, pre in ((given, ""), (small_grads, ""), (given, "m_"), (given, "v_"))]
    small_out = _adam(packed[0][None], [packed[1][None]], packed[2][None], packed[3][None], name="adam_small")
    off = 0
    for k in small_names:
        size = given[k].size
        res[k] = [o[0, :, off:off + size].reshape(given[k].shape) for o in small_out]
        off += size

    outs = [loss, grad_x[None]]
    for i in range(4):
        outs += [res[k][i] for k in weight_names]
    return tuple(outs)
```

```python
import functools

import jax
import jax.numpy as jnp
from jax import lax
from jax.experimental import pallas as pl
from jax.experimental.pallas import tpu as pltpu

F32 = jnp.float32
BF16 = jnp.bfloat16

EPS = 1e-6
LANES = 128
N_DEV = 8
V7X_VMEM_LIMIT_BYTES = 56 << 20
MM_PIPELINE_BYTES = 30 << 20
MM_ROW_TILE = 512
GRAD_WIRE_DTYPE = BF16

HGRN_HEADS = 8
HGRN_CHUNK = 64
HGRN_SUB = 16
HGRN_HEADS_PER_STEP = 8
HGRN_CHUNKS_PER_STEP = 2
EXP_CLAMP = 80.0
MLA_HEADS = 16
MLA_NOPE = 128
MLA_ROPE = 64
ROPE_THETA = 10000.0
ATTN_SCALE = (MLA_NOPE + MLA_ROPE) ** -0.5

ADAM_LR = 0.001
ADAM_B1 = 0.9
ADAM_B2 = 0.999
ADAM_EPS = 1e-08
ADAM_WD = 0.01
ADAM_STEP = 10

_NN = ((1,), (0,))
_NT = ((1,), (1,))
_TN = ((0,), (0,))


def _params(*sem):
    return pltpu.CompilerParams(dimension_semantics=sem, vmem_limit_bytes=V7X_VMEM_LIMIT_BYTES)


def _dot(a, b, dims):
    return lax.dot_general(a.astype(BF16), b.astype(BF16), (dims, ((), ())), preferred_element_type=F32)


def _dot_f32(a, b, dims=_NN):
    return lax.dot_general(a, b, (dims, ((), ())), precision=lax.Precision.HIGH, preferred_element_type=F32)


def _sigmoid(x):
    return 1.0 / (1.0 + jnp.exp(-x))


def _rms(x, w):
    r = lax.rsqrt(jnp.mean(x * x, axis=-1, keepdims=True) + EPS)
    return x * r * w


def _rms_bwd(x, w, dy):
    r = lax.rsqrt(jnp.mean(x * x, axis=-1, keepdims=True) + EPS)
    xh = x * r
    dw = jnp.sum(dy * xh, axis=0, keepdims=True)
    dxh = dy * w
    dx = r * (dxh - xh * jnp.mean(dxh * xh, axis=-1, keepdims=True))
    return dx, dw


def _mm_tiles(m, n, k, a_bytes, b_bytes, out_tile_bytes):
    tm = min(m, MM_ROW_TILE)
    for tn in (n, 2048, 1024, 512, 256, LANES):
        if tn <= n and n % tn == 0:
            if 2 * (tm * k * a_bytes + k * tn * b_bytes + tm * tn * out_tile_bytes) <= MM_PIPELINE_BYTES:
                return tm, tn
    return tm, min(n, LANES)


def _mm(a, b, *, mode, name, out_dtype=None, add=None, relu2_of=None, after=None, col_shards=None):
    if mode == "nn":
        (m, k), (k2, n) = a.shape, b.shape
    elif mode == "nt":
        (m, k), (n, k2) = a.shape, b.shape
    else:
        (k, m), (k2, n) = a.shape, b.shape
    assert k == k2, (name, a.shape, b.shape)
    if out_dtype is None:
        out_dtype = GRAD_WIRE_DTYPE if mode == "tn" else F32
    tile_bytes = sum(x.dtype.itemsize for x in (add, relu2_of) if x is not None) + jnp.dtype(out_dtype).itemsize
    tm, tn = _mm_tiles(m, n, k, a.dtype.itemsize, b.dtype.itemsize, tile_bytes)
    if col_shards is not None:
        assert add is None and relu2_of is None
        tn = n // col_shards
    assert m % tm == 0 and n % tn == 0, (name, m, n)
    dims = {"nn": _NN, "nt": _NT, "tn": _TN}[mode]
    a_spec = pl.BlockSpec((k, tm), lambda i, j: (0, i)) if mode == "tn" else pl.BlockSpec((tm, k), lambda i, j: (i, 0))
    b_spec = pl.BlockSpec((tn, k), lambda i, j: (j, 0)) if mode == "nt" else pl.BlockSpec((k, tn), lambda i, j: (0, j))
    o_spec = pl.BlockSpec((tm, tn), lambda i, j: (i, j))
    operands, in_specs = [a, b], [a_spec, b_spec]
    for extra in (add, relu2_of):
        if extra is not None:
            assert extra.shape == (m, n), (name, extra.shape)
            operands.append(extra)
            in_specs.append(o_spec)
    n_in = len(operands)
    if after is not None:
        operands.append(after)
        in_specs.append(pl.BlockSpec(memory_space=pl.ANY))
    out_shape = jax.ShapeDtypeStruct((m, n), out_dtype)
    if col_shards is not None:
        out_shape = jax.ShapeDtypeStruct((col_shards, m, tn), out_dtype)
        o_spec = pl.BlockSpec((None, tm, tn), lambda i, j: (j, i, 0))

    def body(*refs):
        acc = _dot(refs[0][...], refs[1][...], dims)
        extras, outs = refs[2:n_in], refs[len(operands):]
        if add is not None:
            acc = acc + extras[0][...]
        if relu2_of is not None:
            acc = acc * (2.0 * jnp.sqrt(extras[-1][...].astype(F32)))
        outs[0][...] = acc.astype(out_dtype)

    return pl.pallas_call(
        body, name=name, grid=(m // tm, n // tn), in_specs=in_specs, out_specs=o_spec, out_shape=out_shape,
        compiler_params=_params("parallel", "parallel"),
    )(*operands)


def _rowcall(fn, rows, consts, outs, accs, *, name, tr=256, after=None):
    s = rows[0].shape[0]
    tr = min(tr, s)
    assert s % tr == 0
    n_out = len(outs)
    accs = [(1, a) if isinstance(a, int) else a for a in accs]
    in_specs = [pl.BlockSpec((tr, r.shape[1]), lambda i: (i, 0)) for r in rows]
    in_specs += [pl.BlockSpec(c.shape, lambda i, nd=c.ndim: (0,) * nd) for c in consts]
    out_shape = [jax.ShapeDtypeStruct((s, w), dt) for w, dt in outs] + [jax.ShapeDtypeStruct(a, F32) for a in accs]
    out_specs = [pl.BlockSpec((tr, w), lambda i: (i, 0)) for w, _ in outs] + [pl.BlockSpec(a, lambda i: (0, 0)) for a in accs]
    n_in = len(rows) + len(consts)

    def body(*refs):
        res = fn(*[r[...] for r in refs[:n_in]])
        out_refs = refs[n_in + (after is not None):]
        for ref, val in zip(out_refs[:n_out], res[:n_out]):
            ref[...] = val.astype(ref.dtype)
        i = pl.program_id(0)
        for ref, val in zip(out_refs[n_out:], res[n_out:]):
            @pl.when(i == 0)
            def _(ref=ref, val=val):
                ref[...] = val

            @pl.when(i > 0)
            def _(ref=ref, val=val):
                ref[...] += val

    behind = [] if after is None else [after]
    return pl.pallas_call(
        body, name=name, grid=(s // tr,), in_specs=in_specs + [pl.BlockSpec(memory_space=pl.ANY)] * len(behind),
        out_specs=out_specs, out_shape=out_shape, compiler_params=_params("arbitrary" if accs else "parallel"),
    )(*rows, *consts, *behind)


def _rope_tables(seq):
    half = MLA_ROPE // 2
    inv_freq = ROPE_THETA ** (-jnp.arange(half, dtype=F32) / half)
    ang = jnp.arange(seq, dtype=F32)[:, None] * inv_freq[None, :]
    cos, sin, zero = jnp.cos(ang), jnp.sin(ang), jnp.zeros((seq, half), F32)
    t_c = jnp.concatenate([cos, cos, zero, zero], axis=1)
    t_s1 = jnp.concatenate([-sin, zero, zero, zero], axis=1)
    t_s2 = jnp.concatenate([zero, sin, zero, zero], axis=1)
    return t_c, t_s1, t_s2


def _rope(slab, t_c, t_s1, t_s2):
    return slab * t_c + pltpu.roll(slab, 96, 1) * t_s1 + pltpu.roll(slab, 32, 1) * t_s2


def _rope_t(d, t_c, t_s1, t_s2):
    return d * t_c + pltpu.roll(d * t_s1, 32, 1) + pltpu.roll(d * t_s2, 96, 1)


def _lower_bound(logits):
    l0, l1 = logits[0:1, :], logits[1:2, :]
    mx = jnp.maximum(l0, l1)
    e0, e1 = jnp.exp(l0 - mx), jnp.exp(l1 - mx)
    return e0 / (e0 + e1)


def _tri(n, lower):
    row = lax.broadcasted_iota(jnp.int32, (n, n), 0)
    col = lax.broadcasted_iota(jnp.int32, (n, n), 1)
    return (row >= col) if lower else (row <= col)


def _hgrn_fwd(zq, zf, zi, lb_logits, *, name):
    s, d = zq.shape
    h_n, c, hp, cps = d // LANES, HGRN_CHUNK, HGRN_HEADS_PER_STEP, HGRN_CHUNKS_PER_STEP
    nc = s // c

    def body(zq_ref, zf_ref, zi_ref, lb_ref, o_ref, st_ref, state_sc, b_sc):
        @pl.when(pl.program_id(1) == 0)
        def _():
            state_sc[...] = jnp.zeros_like(state_sc)

        lower = _tri(c, True)
        lower_f = lower.astype(F32)
        hs, pairs = range(hp), [(cc, hh) for cc in range(cps) for hh in range(hp)]
        sls = [slice(hh * LANES, (hh + 1) * LANES) for hh in hs]
        rws = [slice(cc * c, (cc + 1) * c) for cc in range(cps)]
        lb = [_lower_bound(lb_ref[:, sl]) for sl in sls]
        zq_v = {p: zq_ref[rws[p[0]], sls[p[1]]] for p in pairs}
        q = {p: zq_v[p] * _sigmoid(zq_v[p]) for p in pairs}
        f = {p: lb[p[1]] + (1.0 - lb[p[1]]) * _sigmoid(zf_ref[rws[p[0]], sls[p[1]]]) for p in pairs}
        k = {p: 1.0 - f[p] for p in pairs}
        v = {p: zi_ref[rws[p[0]], sls[p[1]]] for p in pairs}
        b = {p: _dot_f32(lower_f, jnp.log(f[p])) for p in pairs}
        for p in pairs:
            b_sc[p[0], p[1]] = b[p]
        qe = {p: q[p] * jnp.exp(b[p]) for p in pairs}
        scores = {p: [] for p in pairs}
        for i in range(c // HGRN_SUB):
            lo = i * HGRN_SUB
            for p in pairs:
                ref = b_sc[p[0], p[1], lo - 1:lo, :] if i > 0 else jnp.zeros((1, LANES), F32)
                qt = q[p][lo:lo + HGRN_SUB, :] * jnp.exp(b[p][lo:lo + HGRN_SUB, :] - ref)
                dec = jnp.exp(jnp.minimum(ref - b[p], EXP_CLAMP))
                scores[p].append(_dot(qt, k[p] * dec, _NT))
        o_intra = {p: _dot(jnp.where(lower, jnp.concatenate(scores[p], axis=0), 0.0), v[p], _NN) for p in pairs}
        bl = {p: b_sc[p[0], p[1], c - 1:c, :] for p in pairs}
        k_end = {p: k[p] * jnp.exp(bl[p] - b[p]) for p in pairs}
        state = [state_sc[hh] for hh in hs]
        for cc in range(cps):
            for hh in hs:
                st_ref[hh, cc] = state[hh]
                o_ref[rws[cc], sls[hh]] = _dot(qe[cc, hh], state[hh], _NT) + o_intra[cc, hh]
            state = [state[hh] * jnp.exp(bl[cc, hh]) + _dot(v[cc, hh], k_end[cc, hh], _TN) for hh in hs]
        for hh in hs:
            state_sc[hh] = state[hh]

    tile = pl.BlockSpec((cps * c, hp * LANES), lambda h, i: (i, h))
    return pl.pallas_call(
        body, name=name, grid=(h_n // hp, nc // cps),
        in_specs=[tile, tile, tile, pl.BlockSpec((2, hp * LANES), lambda h, i: (0, h))],
        out_specs=[tile, pl.BlockSpec((hp, cps, LANES, LANES), lambda h, i: (h, i, 0, 0))],
        out_shape=[jax.ShapeDtypeStruct((s, d), F32), jax.ShapeDtypeStruct((h_n, nc, LANES, LANES), F32)],
        scratch_shapes=[pltpu.VMEM((hp, LANES, LANES), F32), pltpu.VMEM((cps, hp, c, LANES), F32)],
        compiler_params=_params("parallel", "arbitrary"),
    )(zq, zf, zi, lb_logits)


def _hgrn_bwd(zq, zf, zi, lb_logits, states, do, after, *, name):
    s, d = zq.shape
    h_n, c, hp, cps = d // LANES, HGRN_CHUNK, HGRN_HEADS_PER_STEP, HGRN_CHUNKS_PER_STEP
    nc = s // c
    n_steps = nc // cps

    def body(zq_ref, zf_ref, zi_ref, lb_ref, st_ref, do_ref, _, dzq_ref, dzf_ref, dzi_ref, dlb_ref, dstate_sc, b_sc):
        @pl.when(pl.program_id(1) == 0)
        def _():
            dstate_sc[...] = jnp.zeros_like(dstate_sc)
            dlb_ref[...] = jnp.zeros_like(dlb_ref)

        lower, upper = _tri(c, True), _tri(c, False).astype(F32)
        lower_f = lower.astype(F32)
        last_row = lax.broadcasted_iota(jnp.int32, (c, LANES), 0) == c - 1
        hs, pairs = range(hp), [(cc, hh) for cc in range(cps) for hh in range(hp)]
        sls = [slice(hh * LANES, (hh + 1) * LANES) for hh in hs]
        rws = [slice(cc * c, (cc + 1) * c) for cc in range(cps)]
        lb = [_lower_bound(lb_ref[:, sl]) for sl in sls]
        zq_v = {p: zq_ref[rws[p[0]], sls[p[1]]] for p in pairs}
        sq = {p: _sigmoid(zq_v[p]) for p in pairs}
        q = {p: zq_v[p] * sq[p] for p in pairs}
        sf = {p: _sigmoid(zf_ref[rws[p[0]], sls[p[1]]]) for p in pairs}
        f = {p: lb[p[1]] + (1.0 - lb[p[1]]) * sf[p] for p in pairs}
        k = {p: 1.0 - f[p] for p in pairs}
        v = {p: zi_ref[rws[p[0]], sls[p[1]]] for p in pairs}
        d_o = {p: do_ref[rws[p[0]], sls[p[1]]] for p in pairs}
        b = {p: _dot_f32(lower_f, jnp.log(f[p])) for p in pairs}
        s0t = {p: st_ref[p[1], p[0]] for p in pairs}
        for p in pairs:
            b_sc[p[0], p[1]] = b[p]
        bl = {p: b_sc[p[0], p[1], c - 1:c, :] for p in pairs}
        eb = {p: jnp.exp(b[p]) for p in pairs}
        ebl = {p: jnp.exp(bl[p]) for p in pairs}
        dec_end = {p: jnp.exp(bl[p] - b[p]) for p in pairs}
        da = {p: jnp.where(lower, _dot(d_o[p], v[p], _NT), 0.0) for p in pairs}
        dq = {p: _dot(d_o[p], s0t[p], _NN) * eb[p] for p in pairs}
        dstate_in = {p: _dot(d_o[p], q[p] * eb[p], _TN) for p in pairs}
        dk_intra = {p: jnp.zeros((c, LANES), F32) for p in pairs}
        scores, dq_blocks = {p: [] for p in pairs}, {p: [] for p in pairs}
        for i in range(c // HGRN_SUB):
            lo = i * HGRN_SUB
            for p in pairs:
                ref = b_sc[p[0], p[1], lo - 1:lo, :] if i > 0 else jnp.zeros((1, LANES), F32)
                grow = jnp.exp(b[p][lo:lo + HGRN_SUB, :] - ref)
                qt = q[p][lo:lo + HGRN_SUB, :] * grow
                dec = jnp.exp(jnp.minimum(ref - b[p], EXP_CLAMP))
                kd = k[p] * dec
                scores[p].append(_dot(qt, kd, _NT))
                da_i = da[p][lo:lo + HGRN_SUB, :]
                dq_blocks[p].append(_dot_f32(da_i, kd, _NN) * grow)
                dk_intra[p] = dk_intra[p] + _dot_f32(da_i, qt, _TN) * dec
        dv_intra = {p: _dot(jnp.where(lower, jnp.concatenate(scores[p], axis=0), 0.0), d_o[p], _TN) for p in pairs}
        dq = {p: dq[p] + jnp.concatenate(dq_blocks[p], axis=0) for p in pairs}
        q_dq = {p: q[p] * dq[p] for p in pairs}
        for p in pairs:
            dzq_ref[rws[p[0]], sls[p[1]]] = (dq[p] * sq[p] * (1.0 + zq_v[p] * (1.0 - sq[p]))).astype(BF16)
        dstate = [dstate_sc[hh] for hh in hs]
        for cc in reversed(range(cps)):
            ps = [(cc, hh) for hh in hs]
            dk_state = [_dot(v[p], dstate[p[1]], _NN) * dec_end[p] for p in ps]
            dv = [dv_intra[p] + _dot(k[p] * dec_end[p], dstate[p[1]], _NT) for p in ps]
            dk = [dk_intra[p] + dk_state[p[1]] for p in ps]
            db_last = [jnp.sum(k[p] * dk_state[p[1]], axis=0, keepdims=True)
                       + ebl[p] * jnp.sum(s0t[p] * dstate[p[1]], axis=0, keepdims=True) for p in ps]
            db = [q_dq[p] - k[p] * dk[p[1]] + jnp.where(last_row, db_last[p[1]], 0.0) for p in ps]
            df = [_dot_f32(upper, db[p[1]]) / f[p] - dk[p[1]] for p in ps]
            for p in ps:
                hh = p[1]
                dzf_ref[rws[cc], sls[hh]] = (df[hh] * (1.0 - lb[hh]) * sf[p] * (1.0 - sf[p])).astype(BF16)
                dlb_ref[:, sls[hh]] += jnp.sum(df[hh] * (1.0 - sf[p]), axis=0, keepdims=True)
                dzi_ref[rws[cc], sls[hh]] = dv[hh].astype(BF16)
            dstate = [dstate[p[1]] * ebl[p] + dstate_in[p] for p in ps]
        for hh in hs:
            dstate_sc[hh] = dstate[hh]

    tile = pl.BlockSpec((cps * c, hp * LANES), lambda h, i: (n_steps - 1 - i, h))
    out = jax.ShapeDtypeStruct((s, d), BF16)
    return pl.pallas_call(
        body, name=name, grid=(h_n // hp, n_steps),
        in_specs=[tile, tile, tile, pl.BlockSpec((2, hp * LANES), lambda h, i: (0, h)),
                  pl.BlockSpec((hp, cps, LANES, LANES), lambda h, i: (h, n_steps - 1 - i, 0, 0)), tile,
                  pl.BlockSpec(memory_space=pl.ANY)],
        out_specs=[tile, tile, tile, pl.BlockSpec((1, hp * LANES), lambda h, i: (0, h))],
        out_shape=[out, out, out, jax.ShapeDtypeStruct((1, d), F32)],
        scratch_shapes=[pltpu.VMEM((hp, LANES, LANES), F32), pltpu.VMEM((cps, hp, c, LANES), F32)],
        compiler_params=_params("parallel", "arbitrary"),
    )(zq, zf, zi, lb_logits, states, do, after)


ATTN_SUB_ROWS = 256
LOG2E = 1.4426950408889634
LN2 = 0.6931471805599453
Q_PRESCALE = ATTN_SCALE * LOG2E


def _attn_tile(s):
    return min(1024, max(128, s // 2))


def _causal_pairs(n, q_major):
    pairs = [(i, j) for i in range(n) for j in range(i + 1)] if q_major else [(i, j) for j in range(n) for i in range(j, n)]
    return jnp.asarray([p[0] for p in pairs], jnp.int32), jnp.asarray([p[1] for p in pairs], jnp.int32)


def _sub_scores(qn_ref, qr_ref, k, r, sub, t, diagonal):
    q = jnp.concatenate([qn_ref[r:r + sub, :], qr_ref[r:r + sub, :]], axis=1)
    if not diagonal:
        return q, _dot(q, k, _NT)
    cols = r + sub
    keep = lax.broadcasted_iota(jnp.int32, (sub, cols), 1) <= r + lax.broadcasted_iota(jnp.int32, (sub, cols), 0)
    return q, jnp.where(keep, _dot(q, k[:cols], _NT), -jnp.inf)


def _attn_fwd(qn, qr, kn, kr, v, *, name):
    s, t = qn.shape[0], _attn_tile(qn.shape[0])
    sub = min(t, ATTN_SUB_ROWS)
    q_blk, k_blk = _causal_pairs(s // t, True)

    def body(qi_ref, kj_ref, qn_ref, qr_ref, kn_ref, kr_ref, v_ref, o_ref, lse_ref, m_sc, l_sc, acc_sc):
        p_id = pl.program_id(1)
        i, j = qi_ref[p_id], kj_ref[p_id]

        @pl.when(j == 0)
        def _():
            m_sc[...] = jnp.full_like(m_sc, -jnp.inf)
            l_sc[...] = jnp.zeros_like(l_sc)
            acc_sc[...] = jnp.zeros_like(acc_sc)

        def update(diagonal):
            k = jnp.concatenate([kn_ref[...], kr_ref[...]], axis=1)
            v = v_ref[...]
            starts = list(range(0, t, sub))
            scs = [_sub_scores(qn_ref, qr_ref, k, r, sub, t, diagonal)[1] for r in starts]
            ps, alphas = [], []
            for r, sc in zip(starts, scs):
                m_prev = m_sc[r:r + sub, :]
                m_new = jnp.maximum(m_prev, jnp.max(sc, axis=1, keepdims=True))
                alpha = jnp.exp2(m_prev - m_new)
                p = jnp.exp2(sc - m_new[:, :1])
                l_sc[r:r + sub, :] = alpha * l_sc[r:r + sub, :] + jnp.sum(p, axis=1, keepdims=True)
                m_sc[r:r + sub, :] = m_new
                ps.append(p)
                alphas.append(alpha)
            for r, p, alpha in zip(starts, ps, alphas):
                acc_sc[r:r + sub, :] = alpha * acc_sc[r:r + sub, :] + _dot(p, v[:p.shape[1]], _NN)

        @pl.when(j < i)
        def _():
            update(False)

        @pl.when(j == i)
        def _():
            update(True)
            o_ref[...] = (acc_sc[...] / l_sc[...]).astype(BF16)
            lse_ref[...] = m_sc[...] + jnp.log(l_sc[...]) * LOG2E

    q_spec = pl.BlockSpec((t, LANES), lambda h, p, qi, kj: (qi[p], h))
    k_spec = pl.BlockSpec((t, LANES), lambda h, p, qi, kj: (kj[p], h))
    kr_spec = pl.BlockSpec((t, LANES), lambda h, p, qi, kj: (kj[p], 0))
    stat = pltpu.VMEM((t, LANES), F32)
    return pl.pallas_call(
        body, name=name,
        grid_spec=pltpu.PrefetchScalarGridSpec(
            num_scalar_prefetch=2, grid=(MLA_HEADS, q_blk.shape[0]),
            in_specs=[q_spec, q_spec, k_spec, kr_spec, k_spec], out_specs=[q_spec, q_spec],
            scratch_shapes=[stat, stat, stat]),
        out_shape=[jax.ShapeDtypeStruct(qn.shape, BF16), jax.ShapeDtypeStruct(qn.shape, F32)],
        compiler_params=_params("parallel", "arbitrary"),
    )(q_blk, k_blk, qn, qr, kn, kr, v)


def _attn_bwd(qn, qr, kn, kr, v, do, lse, delta, *, name):
    s, t = qn.shape[0], _attn_tile(qn.shape[0])
    n, sub = s // t, min(t, ATTN_SUB_ROWS)
    q_blk, k_blk = _causal_pairs(n, False)

    def body(qi_ref, kj_ref, qn_ref, qr_ref, kn_ref, kr_ref, v_ref, do_ref, lse_ref, delta_ref,
             dqn_ref, dqr_ref, dkn_ref, dv_ref, dkr_ref, dk_sc, dv_sc):
        p_id = pl.program_id(1)
        i, j = qi_ref[p_id], kj_ref[p_id]

        @pl.when(p_id == 0)
        def _():
            dqn_ref[...] = jnp.zeros_like(dqn_ref)
            dqr_ref[...] = jnp.zeros_like(dqr_ref)

        @pl.when(i == j)
        def _():
            dk_sc[...] = jnp.zeros_like(dk_sc)
            dv_sc[...] = jnp.zeros_like(dv_sc)

        def accumulate(diagonal):
            k = jnp.concatenate([kn_ref[...], kr_ref[...]], axis=1)
            v = v_ref[...]
            starts = list(range(0, t, sub))
            qs, d_os, scs, dps = [], [], [], []
            for r in starts:
                q, sc = _sub_scores(qn_ref, qr_ref, k, r, sub, t, diagonal)
                d_o = do_ref[r:r + sub, :]
                qs.append(q)
                d_os.append(d_o)
                scs.append(sc)
                dps.append(_dot(d_o, v[:sc.shape[1]], _NT))
            ps, dss = [], []
            for r, sc, dp in zip(starts, scs, dps):
                p = jnp.exp2(sc - lse_ref[r:r + sub, :][:, :1])
                ps.append(p.astype(BF16))
                dss.append((p * (dp - delta_ref[r:r + sub, :][:, :1])).astype(BF16))
            for r, q, d_o, p, ds in zip(starts, qs, d_os, ps, dss):
                cols = p.shape[1]
                dv_sc[:cols, :] += _dot(p, d_o, _TN)
                dk_sc[:cols, :] += _dot(ds, q, _TN)
                dq = _dot(ds, k[:cols], _NN) * ATTN_SCALE
                rows = pl.ds(pl.multiple_of(i * t + r, sub), sub)
                dqn_ref[rows, :] += dq[:, :LANES]
                dqr_ref[rows, :] += dq[:, LANES:]

        @pl.when(j < i)
        def _():
            accumulate(False)

        @pl.when(j == i)
        def _():
            accumulate(True)

        @pl.when(i == n - 1)
        def _():
            dkn_ref[...] = (dk_sc[:, :LANES] * LN2).astype(BF16)
            dkr_ref[...] = dk_sc[:, LANES:] * LN2
            dv_ref[...] = dv_sc[...].astype(BF16)

    q_spec = pl.BlockSpec((t, LANES), lambda h, p, qi, kj: (qi[p], h))
    k_spec = pl.BlockSpec((t, LANES), lambda h, p, qi, kj: (kj[p], h))
    kr_spec = pl.BlockSpec((t, LANES), lambda h, p, qi, kj: (kj[p], 0))
    head_spec = pl.BlockSpec((s, LANES), lambda h, p, qi, kj: (0, h))
    f32_out, bf16_out = jax.ShapeDtypeStruct(qn.shape, F32), jax.ShapeDtypeStruct(qn.shape, BF16)
    return pl.pallas_call(
        body, name=name,
        grid_spec=pltpu.PrefetchScalarGridSpec(
            num_scalar_prefetch=2, grid=(MLA_HEADS, q_blk.shape[0]),
            in_specs=[q_spec, q_spec, k_spec, kr_spec, k_spec, q_spec, q_spec, q_spec],
            out_specs=[head_spec, head_spec, k_spec, k_spec, k_spec],
            scratch_shapes=[pltpu.VMEM((t, 2 * LANES), F32), pltpu.VMEM((t, LANES), F32)]),
        out_shape=[f32_out, f32_out, bf16_out, bf16_out, f32_out],
        compiler_params=_params("parallel", "arbitrary"),
    )(q_blk, k_blk, qn, qr, kn, kr, v, do, lse, delta)


def _exchange(arrs, *, scatter, name):
    n = len(arrs)
    out_shape = [jax.ShapeDtypeStruct(a.shape if scatter else (N_DEV, *a.shape), a.dtype) for a in arrs]

    def body(*refs):
        ins, outs = refs[:n], refs[n:2 * n]
        send_sems, recv_sems, local_sems = refs[2 * n:]
        x, y, c = lax.axis_index("x"), lax.axis_index("y"), lax.axis_index("c")
        me = 4 * x + 2 * y + c
        copies = []
        for k in range(n):
            local = pltpu.make_async_copy(ins[k].at[me] if scatter else ins[k], outs[k].at[me], local_sems.at[k])
            local.start()
            copies.append(local)
            for d in range(1, N_DEV):
                px, py, pc = (x + (d >> 2)) % 2, (y + ((d >> 1) & 1)) % 2, (c + (d & 1)) % 2
                peer = 4 * px + 2 * py + pc
                remote = pltpu.make_async_remote_copy(
                    src_ref=ins[k].at[peer] if scatter else ins[k], dst_ref=outs[k].at[me],
                    send_sem=send_sems.at[k, d - 1], recv_sem=recv_sems.at[k, d - 1],
                    device_id=(px, py, pc), device_id_type=pl.DeviceIdType.MESH)
                remote.start()
                copies.append(remote)
        for cp in copies:
            cp.wait()

    any_spec = pl.BlockSpec(memory_space=pl.ANY)
    return pl.pallas_call(
        body, name=name, in_specs=[any_spec] * n, out_specs=[any_spec] * n, out_shape=out_shape,
        scratch_shapes=[pltpu.SemaphoreType.DMA((n, N_DEV - 1)), pltpu.SemaphoreType.DMA((n, N_DEV - 1)),
                        pltpu.SemaphoreType.DMA((n,))],
    )(*arrs)


def _peers(x, y, c):
    out = []
    for d in range(1, N_DEV):
        px, py, pc = (x + (d >> 2)) % 2, (y + ((d >> 1) & 1)) % 2, (c + (d & 1)) % 2
        out.append(((px, py, pc), 4 * px + 2 * py + pc))
    return out


def _exchange_copies(ins, lands, send_sems, recv_sems, scatter):
    x, y, c = lax.axis_index("x"), lax.axis_index("y"), lax.axis_index("c")
    me = 4 * x + 2 * y + c
    local, remote = [], []
    for k in range(len(ins)):
        local.append(pltpu.make_async_copy(ins[k].at[me] if scatter else ins[k], lands[k].at[me],
                                           recv_sems.at[k * N_DEV + N_DEV - 1]))
        for d, (coords, peer) in enumerate(_peers(x, y, c)):
            remote.append(pltpu.make_async_remote_copy(
                src_ref=ins[k].at[peer] if scatter else ins[k], dst_ref=lands[k].at[me],
                send_sem=send_sems.at[k * N_DEV + d], recv_sem=recv_sems.at[k * N_DEV + d],
                device_id=coords, device_id_type=pl.DeviceIdType.MESH))
    return local, remote


def _exchange_start(groups, *, scatter, name, after=None):
    arrs = [a for grp in groups for a in grp]
    n, n_grp = len(arrs), len(groups)
    offsets = [sum(len(grp) for grp in groups[:i]) for i in range(n_grp + 1)]
    hbm = pl.BlockSpec(memory_space=pltpu.HBM)
    sem = pl.BlockSpec(memory_space=pltpu.SEMAPHORE)
    lands = [lax.empty(a.shape if scatter else (N_DEV, *a.shape), a.dtype) for a in arrs]

    def body(*refs):
        ins, land_refs = refs[:n], refs[n:2 * n]
        first_out = 2 * n + (after is not None)
        for i in range(n_grp):
            lo, hi = offsets[i], offsets[i + 1]
            local, remote = _exchange_copies(ins[lo:hi], land_refs[lo:hi], refs[first_out + 2 * i],
                                             refs[first_out + 2 * i + 1], scatter)
            for cp in local + remote:
                cp.start()
        refs[-1][...] = jnp.zeros_like(refs[-1])

    operands = [pltpu.with_memory_space_constraint(a, pltpu.HBM) for a in arrs + lands]
    behind = [] if after is None else [after]
    sems = [pltpu.SemaphoreType.DMA((len(grp) * N_DEV,)) for grp in groups for _ in range(2)]
    res = pl.pallas_call(
        body, name=name,
        out_shape=(*sems, *[pltpu.HBM(o.shape, o.dtype) for o in operands], jax.ShapeDtypeStruct((8, LANES), F32)),
        in_specs=[hbm] * (2 * n) + [pl.BlockSpec(memory_space=pl.ANY)] * len(behind),
        out_specs=(*[sem] * (2 * n_grp), *[hbm] * (2 * n), pl.BlockSpec(memory_space=pltpu.VMEM)),
        input_output_aliases={i: 2 * n_grp + i for i in range(2 * n)},
        compiler_params=pltpu.CompilerParams(has_side_effects=pltpu.SideEffectType.DATAFLOW_SIDE_EFFECTING),
    )(*operands, *behind)
    thru_in, thru_land = res[2 * n_grp:2 * n_grp + n], res[2 * n_grp + n:2 * n_grp + 2 * n]
    states = [(res[2 * i], res[2 * i + 1], list(thru_in[offsets[i]:offsets[i + 1]]),
               list(thru_land[offsets[i]:offsets[i + 1]]), scatter) for i in range(n_grp)]
    return states, res[-1]


def _exchange_wait(state, after, *, name):
    send_sems, recv_sems, ins, lands, scatter = state
    n = len(ins)
    hbm = pl.BlockSpec(memory_space=pltpu.HBM)
    sem = pl.BlockSpec(memory_space=pltpu.SEMAPHORE)

    def body(*refs):
        in_refs, land_refs = refs[:n], refs[n:2 * n]
        local, remote = _exchange_copies(in_refs, land_refs, refs[2 * n], refs[2 * n + 1], scatter)
        for cp in local:
            cp.wait()
        for cp in remote:
            cp.wait_send()
            cp.wait_recv()

    res = pl.pallas_call(
        body, name=name, out_shape=tuple(pltpu.HBM(o.shape, o.dtype) for o in ins + lands),
        in_specs=[hbm] * (2 * n) + [sem, sem, pl.BlockSpec(memory_space=pl.ANY)], out_specs=tuple([hbm] * (2 * n)),
        input_output_aliases={i: i for i in range(2 * n)},
        compiler_params=pltpu.CompilerParams(has_side_effects=pltpu.SideEffectType.DATAFLOW_SIDE_EFFECTING),
    )(*ins, *lands, send_sems, recv_sems, after)
    return list(res[n:])


def _adam(w, terms, m, v, *, name):
    n_layers, r, c = w.shape
    tr = min(r, 128)
    assert r % tr == 0 and len(terms) == n_layers
    steps = r // tr

    def body(w_ref, *rest):
        t_refs, (m_ref, v_ref, g_out, d_out, m_out, v_out) = rest[:n_layers], rest[n_layers:]
        for layer, t_ref in enumerate(t_refs):
            @pl.when(pl.program_id(0) == layer)
            def _(t_ref=t_ref):
                g = t_ref[0].astype(F32)
                for s in range(1, t_ref.shape[0]):
                    g = g + t_ref[s].astype(F32)
                m1 = ADAM_B1 * m_ref[...] + (1.0 - ADAM_B1) * g
                v1 = ADAM_B2 * v_ref[...] + (1.0 - ADAM_B2) * jnp.square(g)
                m_hat = m1 / (1.0 - ADAM_B1 ** ADAM_STEP)
                v_hat = v1 / (1.0 - ADAM_B2 ** ADAM_STEP)
                g_out[...] = g
                d_out[...] = -ADAM_LR * (m_hat / (jnp.sqrt(v_hat) + ADAM_EPS) + ADAM_WD * w_ref[...])
                m_out[...] = m1
                v_out[...] = v1

    def term_spec(layer, t):
        return pl.BlockSpec((t.shape[0], tr, c),
                            lambda l, i: (0, jnp.where(l == layer, i, jnp.where(l < layer, 0, steps - 1)), 0))

    spec = pl.BlockSpec((None, tr, c), lambda l, i: (l, i, 0))
    out = jax.ShapeDtypeStruct(w.shape, F32)
    return pl.pallas_call(
        body, name=name, grid=(n_layers, steps),
        in_specs=[spec] + [term_spec(layer, t) for layer, t in enumerate(terms)] + [spec, spec], out_specs=[spec] * 4,
        out_shape=[out] * 4, compiler_params=_params("arbitrary", "arbitrary"),
    )(w, *terms, m, v)


def _sum_terms(terms, *, name):
    n, _, p = terms.shape

    def body(t_ref, o_ref):
        acc = t_ref[0]
        for s in range(1, n):
            acc = acc + t_ref[s]
        o_ref[...] = acc

    return pl.pallas_call(body, name=name, out_shape=jax.ShapeDtypeStruct((1, p), F32))(terms)


def _lb_logits_grad(dlb, logits, *, name):
    def body(dlb_ref, l_ref, o_ref):
        lb = _lower_bound(l_ref[...])
        d0 = dlb_ref[...] * lb * (1.0 - lb)
        o_ref[...] = jnp.concatenate([d0, -d0], axis=0)

    return pl.pallas_call(body, name=name, out_shape=jax.ShapeDtypeStruct(logits.shape, F32))(dlb, logits)


def _silu_grad(z):
    sg = _sigmoid(z)
    return sg * (1.0 + z * (1.0 - sg))


def _head_norm_gate(o, zg, gn):
    outs = []
    for h in range(HGRN_HEADS):
        sl = slice(h * LANES, (h + 1) * LANES)
        zg_h = zg[:, sl]
        outs.append(_rms(o[:, sl], gn) * (zg_h * _sigmoid(zg_h)))
    return (jnp.concatenate(outs, axis=1),)


def _head_norm_gate_bwd(o, zg, dm, gn):
    do_parts, dzg_parts, dgn = [], [], jnp.zeros((1, LANES), F32)
    for h in range(HGRN_HEADS):
        sl = slice(h * LANES, (h + 1) * LANES)
        o_h, zg_h, dm_h = o[:, sl], zg[:, sl], dm[:, sl]
        gate = zg_h * _sigmoid(zg_h)
        do_h, dgn_h = _rms_bwd(o_h, gn, dm_h * gate)
        dgn = dgn + dgn_h
        do_parts.append(do_h)
        dzg_parts.append(dm_h * _rms(o_h, gn) * _silu_grad(zg_h))
    return jnp.concatenate(do_parts, axis=1), jnp.concatenate(dzg_parts, axis=1), dgn


def _rope_slabs(x, t_c, t_s1, t_s2, transpose):
    fn = _rope_t if transpose else _rope
    return jnp.concatenate(
        [fn(x[:, h * LANES:(h + 1) * LANES], t_c, t_s1, t_s2) for h in range(x.shape[1] // LANES)], axis=1)


def _loss_head(h, tgt, w):
    d = h.shape[1]
    r = lax.rsqrt(jnp.mean(h * h, axis=-1, keepdims=True) + EPS)
    xh = h * r
    err = xh * w - tgt
    loss = 0.5 * jnp.sum(jnp.mean(err * err, axis=-1, keepdims=True), axis=0, keepdims=True)
    dy = err / d
    dxh = dy * w
    dh = r * (dxh - xh * jnp.mean(dxh * xh, axis=-1, keepdims=True))
    return dh, dh, jnp.sum(dy * xh, axis=0, keepdims=True), jnp.broadcast_to(loss, (1, LANES))


def _mlp_fwd(h, norm, w_up, w_down, tag, loss_head=None):
    d = h.shape[1]

    def up(x, g, wu):
        x_n = _rms(x, g).astype(BF16)
        return x_n, jnp.concatenate([jnp.square(jnp.maximum(_dot(x_n, wu[j], _NN), 0.0)) for j in range(wu.shape[0])],
                                    axis=1)

    xn, act = _rowcall(up, [h], [norm, w_up], [(d, BF16), (w_up.shape[0] * w_up.shape[2], BF16)], [], tr=512,
                       name=f"{tag}_up")
    if callable(w_down):
        w_down = w_down(act)
    if loss_head is None:
        return _mm(act, w_down, mode="nn", add=h, name=f"{tag}_down"), (h, xn, act)
    tgt, final_norm = loss_head

    def down_and_loss(a, res, t, wd, g):
        return _loss_head(res + _dot(a, wd, _NN), t, g)

    return _rowcall(down_and_loss, [act, h, tgt], [w_down, final_norm], [(d, F32), (d, BF16)], [d, LANES],
                    name=f"{tag}_down_loss"), (h, xn, act)


def _mlp_bwd(dh_out, dh_out_bf, saved, norm, w_up, w_down, tag, after=None):
    h, xn, act = saved
    d = h.shape[1]
    du = _mm(dh_out_bf, w_down, mode="nt", relu2_of=act, out_dtype=BF16, after=after, name=f"{tag}_bwd_du")
    dw_down = _mm(act, dh_out_bf, mode="tn", name=f"{tag}_bwd_wdown")
    dw_up = _mm(xn, du, mode="tn", col_shards=w_up.shape[0], name=f"{tag}_bwd_wup")

    def up_norm_bwd(x, d_u, dres, g, wu):
        cols = wu.shape[2]
        dxn = _dot(d_u[:, :cols], wu[0], _NT)
        for j in range(1, wu.shape[0]):
            dxn = dxn + _dot(d_u[:, j * cols:(j + 1) * cols], wu[j], _NT)
        dx, dw = _rms_bwd(x, g, dxn)
        return dx + dres, dx + dres, dw

    dh, dh_bf, dnorm = _rowcall(up_norm_bwd, [h, du, dh_out], [norm, w_up], [(d, F32), (d, BF16)], [d], tr=512,
                                name=f"{tag}_bwd_dxn")
    return dh, dh_bf, dnorm, dw_up, dw_down


def _row_major(g):
    return g.reshape(g.shape[0] * g.shape[1], g.shape[2])


def _col_major(g):
    return jnp.transpose(g, (1, 0, 2)).reshape(g.shape[1], g.shape[0] * g.shape[2])


def _col_terms(dw):
    k, n = dw.shape
    return jnp.transpose(dw.reshape(k, N_DEV, n // N_DEV), (1, 0, 2))


def _row_terms(dw):
    return dw.reshape(N_DEV, dw.shape[0] // N_DEV, dw.shape[1])


def kernel(x, hgrn_norm, hgrn_w_q, hgrn_w_f, hgrn_w_i, hgrn_w_g, hgrn_g_norm, hgrn_w_o, hgrn_lb_logits, mla_norm, mla_w_dq, mla_q_norm, mla_w_uq, mla_w_o, kv_in_norm, kv_w_dkv, kv_norm, kv_w_uk, kv_w_uv, mlp_norm, mlp_w_up, mlp_w_down, final_norm, loss_target, m_hgrn_norm, m_hgrn_w_q, m_hgrn_w_f, m_hgrn_w_i, m_hgrn_w_g, m_hgrn_g_norm, m_hgrn_w_o, m_hgrn_lb_logits, m_mla_norm, m_mla_w_dq, m_mla_q_norm, m_mla_w_uq, m_mla_w_o, m_kv_in_norm, m_kv_w_dkv, m_kv_norm, m_kv_w_uk, m_kv_w_uv, m_mlp_norm, m_mlp_w_up, m_mlp_w_down, m_final_norm, v_hgrn_norm, v_hgrn_w_q, v_hgrn_w_f, v_hgrn_w_i, v_hgrn_w_g, v_hgrn_g_norm, v_hgrn_w_o, v_hgrn_lb_logits, v_mla_norm, v_mla_w_dq, v_mla_q_norm, v_mla_w_uq, v_mla_w_o, v_kv_in_norm, v_kv_w_dkv, v_kv_norm, v_kv_w_uk, v_kv_w_uv, v_mlp_norm, v_mlp_w_up, v_mlp_w_down, v_final_norm):
    given = dict(locals())
    weight_names = ["hgrn_norm", "hgrn_w_q", "hgrn_w_f", "hgrn_w_i", "hgrn_w_g", "hgrn_g_norm", "hgrn_w_o",
                    "hgrn_lb_logits", "mla_norm", "mla_w_dq", "mla_q_norm", "mla_w_uq", "mla_w_o", "kv_in_norm",
                    "kv_w_dkv", "kv_norm", "kv_w_uk", "kv_w_uv", "mlp_norm", "mlp_w_up", "mlp_w_down", "final_norm"]
    me = 4 * lax.axis_index("x") + 2 * lax.axis_index("y") + lax.axis_index("c")
    xs, tgt = x[0], loss_target[0]
    seq, d_model = xs.shape
    n_heads, hd = MLA_HEADS, LANES

    big_local = {
        "hgrn_w_q": hgrn_w_q[0], "hgrn_w_f": hgrn_w_f[0], "hgrn_w_i": hgrn_w_i[0], "hgrn_w_g": hgrn_w_g[0],
        "hgrn_w_o": hgrn_w_o[0], "mla_w_dq": mla_w_dq[0], "mla_w_uq": mla_w_uq[0], "mla_w_o": mla_w_o[0],
        "kv_w_dkv": kv_w_dkv, "kv_w_uk": kv_w_uk, "kv_w_uv": kv_w_uv,
        "mlp_w_up0": mlp_w_up[0], "mlp_w_up1": mlp_w_up[1], "mlp_w_down0": mlp_w_down[0], "mlp_w_down1": mlp_w_down[1],
    }
    big_names = list(big_local)
    col_sharded = {"mla_w_uq", "kv_w_uk", "kv_w_uv"}
    shard_major = {"mlp_w_up0", "mlp_w_up1"}
    vec_local = jnp.concatenate([hgrn_norm, hgrn_lb_logits], axis=0)
    first_names = ["hgrn_w_q", "hgrn_w_f", "hgrn_w_i"]
    proj_names = first_names + ["hgrn_w_g"]
    later_names = {"hgrn_o": ["hgrn_w_g", "hgrn_w_o"], "up0": ["mlp_w_up0"], "down0": ["mlp_w_down0"],
                   "mla": ["kv_w_dkv", "kv_w_uk", "kv_w_uv", "mla_w_dq", "mla_w_uq", "mla_w_o"],
                   "mlp1": ["mlp_w_up1", "mlp_w_down1"]}

    def unshard(names, arrays):
        return {k: (a if k in shard_major else _col_major(a) if k in col_sharded else _row_major(a))
                for k, a in zip(names, arrays)}

    gather_states, token = _exchange_start(
        [[big_local[k].astype(BF16) for k in first_names] + [vec_local]]
        + [[big_local[k].astype(BF16) for k in names] for names in later_names.values()], scatter=False, name="gather_start")
    first_state, gather_state = gather_states[0], dict(zip(later_names, gather_states[1:]))

    def gather_wait(tag, after):
        w.update(unshard(later_names[tag], _exchange_wait(gather_state[tag], after, name=f"gather_{tag}_wait")))
        return [w[k] for k in later_names[tag]]

    gathered = _exchange_wait(first_state, token, name="gather_first_wait")
    w = unshard(first_names, gathered[:-1])
    vec_full = jnp.transpose(gathered[-1], (1, 0, 2)).reshape(3, d_model)
    hgrn_norm_full, lb_logits_full = vec_full[0:1], vec_full[1:3]
    t_c, t_s1, t_s2 = _rope_tables(seq)
    kv_lora = kv_w_uk.shape[0]

    def hgrn_proj(a, g, *weights):
        xn = _rms(a, g).astype(BF16)
        return (xn, *[_dot(xn, wt, _NN) for wt in weights])

    xn0, zq, zf, zi = _rowcall(hgrn_proj, [xs], [hgrn_norm_full] + [w[k] for k in first_names],
                               [(d_model, BF16)] + [(d_model, F32)] * 3, [], tr=512, name="hgrn_proj")
    o_rec, states = _hgrn_fwd(zq, zf, zi, lb_logits_full, name="hgrn_fwd")
    gather_wait("hgrn_o", o_rec)

    def gate_out(o, x_n, res, gn, wg, wo):
        z = _dot(x_n, wg, _NN)
        m = _head_norm_gate(o, z, gn)[0].astype(BF16)
        return z, m, res + _dot(m, wo, _NN)

    zg, mixed, h1 = _rowcall(gate_out, [o_rec, xn0, xs], [hgrn_g_norm, w["hgrn_w_g"], w["hgrn_w_o"]],
                             [(d_model, F32), (d_model, BF16), (d_model, F32)], [], name="hgrn_gate_out")
    h2, mlp0_saved = _mlp_fwd(h1, mlp_norm[0:1], gather_wait("up0", h1)[0], lambda act: gather_wait("down0", act)[0],
                              "mlp0")
    gather_wait("mla", h2)
    w_uq3 = w["mla_w_uq"].reshape(-1, n_heads, MLA_NOPE + MLA_ROPE)
    w_uq_nope = w_uq3[:, :, :MLA_NOPE].reshape(-1, n_heads * hd)
    w_uq_rope = jnp.pad(w_uq3[:, :, MLA_NOPE:], ((0, 0), (0, 0), (0, hd - MLA_ROPE))).reshape(-1, n_heads * hd)
    w_dkv_pad = jnp.pad(w["kv_w_dkv"], ((0, 0), (0, kv_lora + hd - w["kv_w_dkv"].shape[1])))

    q_lora, qk_cols = w["mla_w_dq"].shape[1], n_heads * hd

    def mla_qkv(a, tc, ts1, ts2, g_kv_in, g_mla, g_q, g_kv, wdq, wn, wr, wdkv, wuk, wuv):
        h_n, x_n = _rms(a, g_kv_in).astype(BF16), _rms(a, g_mla).astype(BF16)
        cq = _dot(x_n, wdq, _NN)
        cq_n = _rms(cq, g_q).astype(BF16)
        q_nope = _dot(cq_n, wn, _NN) * Q_PRESCALE
        q_rope = _rope_slabs(_dot(cq_n, wr, _NN) * Q_PRESCALE, tc, ts1, ts2, False)
        c_all = _dot(h_n, wdkv, _NN)
        lat = _rms(c_all[:, :kv_lora], g_kv).astype(BF16)
        return (h_n, x_n, cq, cq_n, q_nope, q_rope, c_all, lat, _rope(c_all[:, kv_lora:], tc, ts1, ts2),
                _dot(lat, wuk, _NN), _dot(lat, wuv, _NN))

    hn, xn2, cq_pre, c_q, qn, qr, ckr, c_kv, kr, kn, vv = _rowcall(
        mla_qkv, [h2, t_c, t_s1, t_s2],
        [kv_in_norm[None, :], mla_norm, mla_q_norm, kv_norm[None, :], w["mla_w_dq"], w_uq_nope, w_uq_rope, w_dkv_pad,
         w["kv_w_uk"], w["kv_w_uv"]],
        [(d_model, BF16), (d_model, BF16), (q_lora, F32), (q_lora, BF16), (qk_cols, BF16), (qk_cols, BF16),
         (kv_lora + hd, F32), (kv_lora, BF16), (hd, BF16), (qk_cols, BF16), (qk_cols, BF16)], [], tr=512, name="mla_qkv")
    o_att, lse = _attn_fwd(qn, qr, kn, kr, vv, name="attn_fwd")
    h3 = _mm(o_att, w["mla_w_o"], mode="nn", add=h2, name="attn_out")
    gather_wait("mlp1", h3)
    (dh4, dh4_bf, g_final_norm, loss_part), mlp1_saved = _mlp_fwd(
        h3, mlp_norm[1:2], w["mlp_w_up1"], w["mlp_w_down1"], "mlp1", loss_head=(tgt, final_norm[None, :]))

    g = {}
    groups = {"mlp1": ["mlp_w_up1", "mlp_w_down1"],
              "mla": ["mla_w_o", "mla_w_uq", "mla_w_dq", "kv_w_uk", "kv_w_uv", "kv_w_dkv"],
              "mlp0": ["mlp_w_up0", "mlp_w_down0"],
              "hgrn_out": ["hgrn_w_o", "hgrn_w_g"],
              "hgrn_in": ["hgrn_w_q", "hgrn_w_f", "hgrn_w_i"]}
    scatter_state = {}

    def scatter_start(tag, after=None):
        (scatter_state[tag],), tok = _exchange_start(
            [[g[k] if k in shard_major else (_col_terms if k in col_sharded else _row_terms)(g[k]) for k in groups[tag]]],
            scatter=True, after=after, name=f"scatter_{tag}_start")
        return tok

    dh3, dh3_bf, g_mlp_norm1, g["mlp_w_up1"], g["mlp_w_down1"] = _mlp_bwd(
        dh4, dh4_bf, mlp1_saved, mlp_norm[1:2], w["mlp_w_up1"], w["mlp_w_down1"], "mlp1")
    def attn_out_bwd(dres, o, wo):
        d_o = _dot(dres, wo, _NT).astype(BF16)
        prod = d_o.astype(F32) * o.astype(F32)
        return d_o, jnp.concatenate([jnp.broadcast_to(jnp.sum(prod[:, h * hd:(h + 1) * hd], axis=1, keepdims=True),
                                                      (prod.shape[0], hd)) for h in range(n_heads)], axis=1)

    d_oatt, delta = _rowcall(attn_out_bwd, [dh3_bf, o_att], [w["mla_w_o"]], [(qk_cols, BF16), (qk_cols, F32)], [],
                             after=scatter_start("mlp1"), name="attn_out_bwd_x")
    g["mla_w_o"] = _mm(o_att, dh3_bf, mode="tn", name="attn_out_bwd_w")
    dqn, dqr, dkn, dvv, dkr = _attn_bwd(qn, qr, kn, kr, vv, d_oatt, lse, delta, name="attn_bwd")

    def q_path_bwd(cq, cq_n, x_n, d_qn, d_qr, tc, ts1, ts2, g_q, wdq, wn, wr):
        d_qn, d_qr = d_qn.astype(BF16), _rope_slabs(d_qr, tc, ts1, ts2, True).astype(BF16)
        d_cq, d_gq = _rms_bwd(cq, g_q, _dot(d_qn, wn, _NT) + _dot(d_qr, wr, _NT))
        d_cq = d_cq.astype(BF16)
        return _dot(d_cq, wdq, _NT), d_gq, _dot(x_n, d_cq, _TN), _dot(cq_n, d_qn, _TN), _dot(cq_n, d_qr, _TN)

    dxn2, g_q_norm, g_dq, g_uq_nope, g_uq_rope = _rowcall(
        q_path_bwd, [cq_pre, c_q, xn2, dqn, dqr, t_c, t_s1, t_s2], [mla_q_norm, w["mla_w_dq"], w_uq_nope, w_uq_rope],
        [(d_model, F32)], [q_lora, (d_model, q_lora), (q_lora, qk_cols), (q_lora, qk_cols)], tr=512, name="mla_q_bwd")
    g["mla_w_dq"] = g_dq.astype(GRAD_WIRE_DTYPE)
    g["mla_w_uq"] = jnp.concatenate([g_uq_nope.reshape(q_lora, n_heads, hd),
                                     g_uq_rope.reshape(q_lora, n_heads, hd)[:, :, :MLA_ROPE]],
                                    axis=2).reshape(q_lora, -1).astype(GRAD_WIRE_DTYPE)

    def kv_path_bwd(c_all, lat, h_n, d_kn, d_v, d_kr_heads, tc, ts1, ts2, a, d_xn2, dres,
                    g_kv, g_kv_in, g_mla, wdkv, wuk, wuv):
        d_lat, d_gkv = _rms_bwd(c_all[:, :kv_lora], g_kv, _dot(d_kn, wuk, _NT) + _dot(d_v, wuv, _NT))
        d_kr = d_kr_heads[:, :hd]
        for h in range(1, n_heads):
            d_kr = d_kr + d_kr_heads[:, h * hd:(h + 1) * hd]
        d_all = jnp.concatenate([d_lat, _rope_t(d_kr, tc, ts1, ts2)], axis=1).astype(BF16)
        dx1, d_gkv_in = _rms_bwd(a, g_kv_in, _dot(d_all, wdkv, _NT))
        dx2, d_gmla = _rms_bwd(a, g_mla, d_xn2)
        d_a = dx1 + dx2 + dres
        return (d_a, d_a, d_gkv, d_gkv_in, d_gmla, _dot(h_n, d_all, _TN), _dot(lat, d_kn, _TN), _dot(lat, d_v, _TN))

    dh2, dh2_bf, g_kv_norm, g_kv_in_norm, g_mla_norm, g_dkv, g_uk, g_uv = _rowcall(
        kv_path_bwd, [ckr, c_kv, hn, dkn, dvv, dkr, t_c, t_s1, t_s2, h2, dxn2, dh3],
        [kv_norm[None, :], kv_in_norm[None, :], mla_norm, w_dkv_pad, w["kv_w_uk"], w["kv_w_uv"]],
        [(d_model, F32), (d_model, BF16)],
        [kv_lora, d_model, d_model, (d_model, kv_lora + hd), (kv_lora, qk_cols), (kv_lora, qk_cols)], name="mla_kv_bwd")
    g["kv_w_dkv"] = g_dkv[:, :kv_w_dkv.shape[1]].astype(GRAD_WIRE_DTYPE)
    g["kv_w_uk"], g["kv_w_uv"] = g_uk.astype(GRAD_WIRE_DTYPE), g_uv.astype(GRAD_WIRE_DTYPE)
    dh1, dh1_bf, g_mlp_norm0, g["mlp_w_up0"], g["mlp_w_down0"] = _mlp_bwd(
        dh2, dh2_bf, mlp0_saved, mlp_norm[0:1], w["mlp_w_up0"], w["mlp_w_down0"], "mlp0", after=scatter_start("mla"))

    g["hgrn_w_o"] = _mm(mixed, dh1_bf, mode="tn", after=scatter_start("mlp0"), name="hgrn_out_bwd_w")
    do_rec, dzg, g_g_norm = _rowcall(
        lambda dres, o, z, wo, gn: _head_norm_gate_bwd(o, z, _dot(dres, wo, _NT), gn), [dh1_bf, o_rec, zg],
        [w["hgrn_w_o"], hgrn_g_norm], [(d_model, F32), (d_model, BF16)], [hd], name="hgrn_gate_out_bwd")
    g["hgrn_w_g"] = _mm(xn0, dzg, mode="tn", name="hgrn_w_g_bwd_w")
    dzq, dzf, dzi, g_lb = _hgrn_bwd(zq, zf, zi, lb_logits_full, states, do_rec, scatter_start("hgrn_out"),
                                    name="hgrn_bwd")
    for nm, dz in (("hgrn_w_q", dzq), ("hgrn_w_f", dzf), ("hgrn_w_i", dzi)):
        g[nm] = _mm(xn0, dz, mode="tn", name=f"{nm}_bwd_w")

    def hgrn_proj_bwd(a, dres, *rest):
        dzs, gw, weights = rest[:4], rest[4], rest[5:]
        dxn = _dot(dzs[0], weights[0], _NT)
        for dz, wt in zip(dzs[1:], weights[1:]):
            dxn = dxn + _dot(dz, wt, _NT)
        dx, dw = _rms_bwd(a, gw, dxn)
        return dx + dres, dw

    grad_x, g_hgrn_norm = _rowcall(hgrn_proj_bwd, [xs, dh1, dzq, dzf, dzi, dzg],
                                   [hgrn_norm_full] + [w[k] for k in proj_names], [(d_model, F32)], [d_model],
                                   tr=512, name="hgrn_proj_bwd")

    small_parts = [g_hgrn_norm, g_lb, g_g_norm, g_mla_norm, g_q_norm, g_kv_in_norm, g_kv_norm, g_mlp_norm0,
                   g_mlp_norm1, g_final_norm, loss_part]
    small_sizes = [p.shape[1] for p in small_parts]
    small_terms = _exchange([jnp.concatenate(small_parts, axis=1)], scatter=False, name="gather_small")[0]
    small_sum = _sum_terms(small_terms, name="sum_small")
    last = scatter_start("hgrn_in", after=small_sum)
    offs = [0]
    for sz in small_sizes:
        offs.append(offs[-1] + sz)
    (s_hgrn_norm, s_lb, s_g_norm, s_mla_norm, s_q_norm, s_kv_in_norm, s_kv_norm, s_mlp_norm0, s_mlp_norm1, s_final_norm,
     s_loss) = [small_sum[:, a:b] for a, b in zip(offs[:-1], offs[1:])]
    shard = hgrn_norm.shape[1]
    g_lb_logits = _lb_logits_grad(lax.dynamic_slice_in_dim(s_lb, me * shard, shard, axis=1), hgrn_lb_logits,
                                  name="lb_logits_grad")
    loss = s_loss[0, 0]

    res, layer_terms = {}, {}

    def update(k, term_list):
        shape = given[k].shape
        as_layers = (len(term_list), shape[-2], shape[-1])
        upd = _adam(given[k].reshape(as_layers), term_list, given["m_" + k].reshape(as_layers),
                    given["v_" + k].reshape(as_layers), name=f"adam_{k}")
        res[k] = [o.reshape(shape) for o in upd]
        return upd[0]

    for tag, names in groups.items():
        for k, t in zip(names, _exchange_wait(scatter_state[tag], last, name=f"scatter_{tag}_wait")):
            if k.startswith("mlp_w_"):
                layer_terms.setdefault(k[:-1], {})[int(k[-1])] = t
                if len(layer_terms[k[:-1]]) == 2:
                    last = update(k[:-1], [layer_terms[k[:-1]][0], layer_terms[k[:-1]][1]])
            else:
                last = update(k, [t])

    small_grads = {
        "hgrn_norm": lax.dynamic_slice_in_dim(s_hgrn_norm, me * shard, shard, axis=1),
        "hgrn_g_norm": s_g_norm, "hgrn_lb_logits": g_lb_logits, "mla_norm": s_mla_norm, "mla_q_norm": s_q_norm,
        "kv_in_norm": s_kv_in_norm, "kv_norm": s_kv_norm,
        "mlp_norm": jnp.concatenate([s_mlp_norm0, s_mlp_norm1], axis=0), "final_norm": s_final_norm,
    }
    small_names = list(small_grads)

    def flat(a):
        return a.reshape(1, -1)

    packed = [jnp.concatenate([flat(src[pre + k]) for k in small_names], axis=1)
              for src, pre in ((given, ""), (small_grads, ""), (given, "m_"), (given, "v_"))]
    small_out = _adam(packed[0][None], [packed[1][None]], packed[2][None], packed[3][None], name="adam_small")
    off = 0
    for k in small_names:
        size = given[k].size
        res[k] = [o[0, :, off:off + size].reshape(given[k].shape) for o in small_out]
        off += size

    outs = [loss, grad_x[None]]
    for i in range(4):
        outs += [res[k][i] for k in weight_names]
    return tuple(outs)
```

```python
import functools

import jax
import jax.numpy as jnp
from jax import lax
from jax.experimental import pallas as pl
from jax.experimental.pallas import tpu as pltpu

F32 = jnp.float32
BF16 = jnp.bfloat16

EPS = 1e-6
LANES = 128
N_DEV = 8
V7X_VMEM_LIMIT_BYTES = 56 << 20
MM_PIPELINE_BYTES = 30 << 20
MM_ROW_TILE = 512
GRAD_WIRE_DTYPE = BF16

HGRN_HEADS = 8
HGRN_CHUNK = 64
HGRN_SUB = 16
HGRN_HEADS_PER_STEP = 8
HGRN_CHUNKS_PER_STEP = 4
EXP_CLAMP = 80.0
MLA_HEADS = 16
MLA_NOPE = 128
MLA_ROPE = 64
ROPE_THETA = 10000.0
ATTN_SCALE = (MLA_NOPE + MLA_ROPE) ** -0.5

ADAM_LR = 0.001
ADAM_B1 = 0.9
ADAM_B2 = 0.999
ADAM_EPS = 1e-08
ADAM_WD = 0.01
ADAM_STEP = 10

_NN = ((1,), (0,))
_NT = ((1,), (1,))
_TN = ((0,), (0,))


def _params(*sem):
    return pltpu.CompilerParams(dimension_semantics=sem, vmem_limit_bytes=V7X_VMEM_LIMIT_BYTES)


def _dot(a, b, dims):
    return lax.dot_general(a.astype(BF16), b.astype(BF16), (dims, ((), ())), preferred_element_type=F32)


def _dot_f32(a, b, dims=_NN):
    return lax.dot_general(a, b, (dims, ((), ())), precision=lax.Precision.HIGH, preferred_element_type=F32)


def _sigmoid(x):
    return 1.0 / (1.0 + jnp.exp(-x))


def _rms(x, w):
    r = lax.rsqrt(jnp.mean(x * x, axis=-1, keepdims=True) + EPS)
    return x * r * w


def _rms_bwd(x, w, dy):
    r = lax.rsqrt(jnp.mean(x * x, axis=-1, keepdims=True) + EPS)
    xh = x * r
    dw = jnp.sum(dy * xh, axis=0, keepdims=True)
    dxh = dy * w
    dx = r * (dxh - xh * jnp.mean(dxh * xh, axis=-1, keepdims=True))
    return dx, dw


def _mm_tiles(m, n, k, a_bytes, b_bytes, out_tile_bytes):
    tm = min(m, MM_ROW_TILE)
    for tn in (n, 2048, 1024, 512, 256, LANES):
        if tn <= n and n % tn == 0:
            if 2 * (tm * k * a_bytes + k * tn * b_bytes + tm * tn * out_tile_bytes) <= MM_PIPELINE_BYTES:
                return tm, tn
    return tm, min(n, LANES)


def _mm(a, b, *, mode, name, out_dtype=None, add=None, relu2_of=None, after=None, col_shards=None):
    if mode == "nn":
        (m, k), (k2, n) = a.shape, b.shape
    elif mode == "nt":
        (m, k), (n, k2) = a.shape, b.shape
    else:
        (k, m), (k2, n) = a.shape, b.shape
    assert k == k2, (name, a.shape, b.shape)
    if out_dtype is None:
        out_dtype = GRAD_WIRE_DTYPE if mode == "tn" else F32
    tile_bytes = sum(x.dtype.itemsize for x in (add, relu2_of) if x is not None) + jnp.dtype(out_dtype).itemsize
    tm, tn = _mm_tiles(m, n, k, a.dtype.itemsize, b.dtype.itemsize, tile_bytes)
    if col_shards is not None:
        assert add is None and relu2_of is None
        tn = n // col_shards
    assert m % tm == 0 and n % tn == 0, (name, m, n)
    dims = {"nn": _NN, "nt": _NT, "tn": _TN}[mode]
    a_spec = pl.BlockSpec((k, tm), lambda i, j: (0, i)) if mode == "tn" else pl.BlockSpec((tm, k), lambda i, j: (i, 0))
    b_spec = pl.BlockSpec((tn, k), lambda i, j: (j, 0)) if mode == "nt" else pl.BlockSpec((k, tn), lambda i, j: (0, j))
    o_spec = pl.BlockSpec((tm, tn), lambda i, j: (i, j))
    operands, in_specs = [a, b], [a_spec, b_spec]
    for extra in (add, relu2_of):
        if extra is not None:
            assert extra.shape == (m, n), (name, extra.shape)
            operands.append(extra)
            in_specs.append(o_spec)
    n_in = len(operands)
    if after is not None:
        operands.append(after)
        in_specs.append(pl.BlockSpec(memory_space=pl.ANY))
    out_shape = jax.ShapeDtypeStruct((m, n), out_dtype)
    if col_shards is not None:
        out_shape = jax.ShapeDtypeStruct((col_shards, m, tn), out_dtype)
        o_spec = pl.BlockSpec((None, tm, tn), lambda i, j: (j, i, 0))

    def body(*refs):
        acc = _dot(refs[0][...], refs[1][...], dims)
        extras, outs = refs[2:n_in], refs[len(operands):]
        if add is not None:
            acc = acc + extras[0][...]
        if relu2_of is not None:
            acc = acc * (2.0 * jnp.sqrt(extras[-1][...].astype(F32)))
        outs[0][...] = acc.astype(out_dtype)

    return pl.pallas_call(
        body, name=name, grid=(m // tm, n // tn), in_specs=in_specs, out_specs=o_spec, out_shape=out_shape,
        compiler_params=_params("parallel", "parallel"),
    )(*operands)


def _rowcall(fn, rows, consts, outs, accs, *, name, tr=256, after=None):
    s = rows[0].shape[0]
    tr = min(tr, s)
    assert s % tr == 0
    n_out = len(outs)
    accs = [(1, a) if isinstance(a, int) else a for a in accs]
    in_specs = [pl.BlockSpec((tr, r.shape[1]), lambda i: (i, 0)) for r in rows]
    in_specs += [pl.BlockSpec(c.shape, lambda i, nd=c.ndim: (0,) * nd) for c in consts]
    out_shape = [jax.ShapeDtypeStruct((s, w), dt) for w, dt in outs] + [jax.ShapeDtypeStruct(a, F32) for a in accs]
    out_specs = [pl.BlockSpec((tr, w), lambda i: (i, 0)) for w, _ in outs] + [pl.BlockSpec(a, lambda i: (0, 0)) for a in accs]
    n_in = len(rows) + len(consts)

    def body(*refs):
        res = fn(*[r[...] for r in refs[:n_in]])
        out_refs = refs[n_in + (after is not None):]
        for ref, val in zip(out_refs[:n_out], res[:n_out]):
            ref[...] = val.astype(ref.dtype)
        i = pl.program_id(0)
        for ref, val in zip(out_refs[n_out:], res[n_out:]):
            @pl.when(i == 0)
            def _(ref=ref, val=val):
                ref[...] = val

            @pl.when(i > 0)
            def _(ref=ref, val=val):
                ref[...] += val

    behind = [] if after is None else [after]
    return pl.pallas_call(
        body, name=name, grid=(s // tr,), in_specs=in_specs + [pl.BlockSpec(memory_space=pl.ANY)] * len(behind),
        out_specs=out_specs, out_shape=out_shape, compiler_params=_params("arbitrary" if accs else "parallel"),
    )(*rows, *consts, *behind)


def _rope_tables(seq):
    half = MLA_ROPE // 2
    inv_freq = ROPE_THETA ** (-jnp.arange(half, dtype=F32) / half)
    ang = jnp.arange(seq, dtype=F32)[:, None] * inv_freq[None, :]
    cos, sin, zero = jnp.cos(ang), jnp.sin(ang), jnp.zeros((seq, half), F32)
    t_c = jnp.concatenate([cos, cos, zero, zero], axis=1)
    t_s1 = jnp.concatenate([-sin, zero, zero, zero], axis=1)
    t_s2 = jnp.concatenate([zero, sin, zero, zero], axis=1)
    return t_c, t_s1, t_s2


def _rope(slab, t_c, t_s1, t_s2):
    return slab * t_c + pltpu.roll(slab, 96, 1) * t_s1 + pltpu.roll(slab, 32, 1) * t_s2


def _rope_t(d, t_c, t_s1, t_s2):
    return d * t_c + pltpu.roll(d * t_s1, 32, 1) + pltpu.roll(d * t_s2, 96, 1)


def _lower_bound(logits):
    l0, l1 = logits[0:1, :], logits[1:2, :]
    mx = jnp.maximum(l0, l1)
    e0, e1 = jnp.exp(l0 - mx), jnp.exp(l1 - mx)
    return e0 / (e0 + e1)


def _tri(n, lower):
    row = lax.broadcasted_iota(jnp.int32, (n, n), 0)
    col = lax.broadcasted_iota(jnp.int32, (n, n), 1)
    return (row >= col) if lower else (row <= col)


def _hgrn_fwd(zq, zf, zi, lb_logits, *, name):
    s, d = zq.shape
    h_n, c, hp, cps = d // LANES, HGRN_CHUNK, HGRN_HEADS_PER_STEP, HGRN_CHUNKS_PER_STEP
    nc = s // c

    def body(zq_ref, zf_ref, zi_ref, lb_ref, o_ref, st_ref, state_sc, b_sc):
        @pl.when(pl.program_id(1) == 0)
        def _():
            state_sc[...] = jnp.zeros_like(state_sc)

        lower = _tri(c, True)
        lower_f = lower.astype(F32)
        hs, pairs = range(hp), [(cc, hh) for cc in range(cps) for hh in range(hp)]
        sls = [slice(hh * LANES, (hh + 1) * LANES) for hh in hs]
        rws = [slice(cc * c, (cc + 1) * c) for cc in range(cps)]
        lb = [_lower_bound(lb_ref[:, sl]) for sl in sls]
        zq_v = {p: zq_ref[rws[p[0]], sls[p[1]]] for p in pairs}
        q = {p: zq_v[p] * _sigmoid(zq_v[p]) for p in pairs}
        f = {p: lb[p[1]] + (1.0 - lb[p[1]]) * _sigmoid(zf_ref[rws[p[0]], sls[p[1]]]) for p in pairs}
        k = {p: 1.0 - f[p] for p in pairs}
        v = {p: zi_ref[rws[p[0]], sls[p[1]]] for p in pairs}
        b = {p: _dot_f32(lower_f, jnp.log(f[p])) for p in pairs}
        for p in pairs:
            b_sc[p[0], p[1]] = b[p]
        qe = {p: q[p] * jnp.exp(b[p]) for p in pairs}
        scores = {p: [] for p in pairs}
        for i in range(c // HGRN_SUB):
            lo = i * HGRN_SUB
            for p in pairs:
                ref = b_sc[p[0], p[1], lo - 1:lo, :] if i > 0 else jnp.zeros((1, LANES), F32)
                qt = q[p][lo:lo + HGRN_SUB, :] * jnp.exp(b[p][lo:lo + HGRN_SUB, :] - ref)
                dec = jnp.exp(jnp.minimum(ref - b[p], EXP_CLAMP))
                scores[p].append(_dot(qt, k[p] * dec, _NT))
        o_intra = {p: _dot(jnp.where(lower, jnp.concatenate(scores[p], axis=0), 0.0), v[p], _NN) for p in pairs}
        bl = {p: b_sc[p[0], p[1], c - 1:c, :] for p in pairs}
        k_end = {p: k[p] * jnp.exp(bl[p] - b[p]) for p in pairs}
        state = [state_sc[hh] for hh in hs]
        for cc in range(cps):
            for hh in hs:
                st_ref[hh, cc] = state[hh]
                o_ref[rws[cc], sls[hh]] = _dot(qe[cc, hh], state[hh], _NT) + o_intra[cc, hh]
            state = [state[hh] * jnp.exp(bl[cc, hh]) + _dot(v[cc, hh], k_end[cc, hh], _TN) for hh in hs]
        for hh in hs:
            state_sc[hh] = state[hh]

    tile = pl.BlockSpec((cps * c, hp * LANES), lambda h, i: (i, h))
    return pl.pallas_call(
        body, name=name, grid=(h_n // hp, nc // cps),
        in_specs=[tile, tile, tile, pl.BlockSpec((2, hp * LANES), lambda h, i: (0, h))],
        out_specs=[tile, pl.BlockSpec((hp, cps, LANES, LANES), lambda h, i: (h, i, 0, 0))],
        out_shape=[jax.ShapeDtypeStruct((s, d), F32), jax.ShapeDtypeStruct((h_n, nc, LANES, LANES), F32)],
        scratch_shapes=[pltpu.VMEM((hp, LANES, LANES), F32), pltpu.VMEM((cps, hp, c, LANES), F32)],
        compiler_params=_params("parallel", "arbitrary"),
    )(zq, zf, zi, lb_logits)


def _hgrn_bwd(zq, zf, zi, lb_logits, states, do, after, *, name):
    s, d = zq.shape
    h_n, c, hp, cps = d // LANES, HGRN_CHUNK, HGRN_HEADS_PER_STEP, HGRN_CHUNKS_PER_STEP
    nc = s // c
    n_steps = nc // cps

    def body(zq_ref, zf_ref, zi_ref, lb_ref, st_ref, do_ref, _, dzq_ref, dzf_ref, dzi_ref, dlb_ref, dstate_sc, b_sc):
        @pl.when(pl.program_id(1) == 0)
        def _():
            dstate_sc[...] = jnp.zeros_like(dstate_sc)
            dlb_ref[...] = jnp.zeros_like(dlb_ref)

        lower, upper = _tri(c, True), _tri(c, False).astype(F32)
        lower_f = lower.astype(F32)
        last_row = lax.broadcasted_iota(jnp.int32, (c, LANES), 0) == c - 1
        hs, pairs = range(hp), [(cc, hh) for cc in range(cps) for hh in range(hp)]
        sls = [slice(hh * LANES, (hh + 1) * LANES) for hh in hs]
        rws = [slice(cc * c, (cc + 1) * c) for cc in range(cps)]
        lb = [_lower_bound(lb_ref[:, sl]) for sl in sls]
        zq_v = {p: zq_ref[rws[p[0]], sls[p[1]]] for p in pairs}
        sq = {p: _sigmoid(zq_v[p]) for p in pairs}
        q = {p: zq_v[p] * sq[p] for p in pairs}
        sf = {p: _sigmoid(zf_ref[rws[p[0]], sls[p[1]]]) for p in pairs}
        f = {p: lb[p[1]] + (1.0 - lb[p[1]]) * sf[p] for p in pairs}
        k = {p: 1.0 - f[p] for p in pairs}
        v = {p: zi_ref[rws[p[0]], sls[p[1]]] for p in pairs}
        d_o = {p: do_ref[rws[p[0]], sls[p[1]]] for p in pairs}
        b = {p: _dot_f32(lower_f, jnp.log(f[p])) for p in pairs}
        s0t = {p: st_ref[p[1], p[0]] for p in pairs}
        for p in pairs:
            b_sc[p[0], p[1]] = b[p]
        bl = {p: b_sc[p[0], p[1], c - 1:c, :] for p in pairs}
        eb = {p: jnp.exp(b[p]) for p in pairs}
        ebl = {p: jnp.exp(bl[p]) for p in pairs}
        dec_end = {p: jnp.exp(bl[p] - b[p]) for p in pairs}
        da = {p: jnp.where(lower, _dot(d_o[p], v[p], _NT), 0.0) for p in pairs}
        dq = {p: _dot(d_o[p], s0t[p], _NN) * eb[p] for p in pairs}
        dstate_in = {p: _dot(d_o[p], q[p] * eb[p], _TN) for p in pairs}
        dk_intra = {p: jnp.zeros((c, LANES), F32) for p in pairs}
        scores, dq_blocks = {p: [] for p in pairs}, {p: [] for p in pairs}
        for i in range(c // HGRN_SUB):
            lo = i * HGRN_SUB
            for p in pairs:
                ref = b_sc[p[0], p[1], lo - 1:lo, :] if i > 0 else jnp.zeros((1, LANES), F32)
                grow = jnp.exp(b[p][lo:lo + HGRN_SUB, :] - ref)
                qt = q[p][lo:lo + HGRN_SUB, :] * grow
                dec = jnp.exp(jnp.minimum(ref - b[p], EXP_CLAMP))
                kd = k[p] * dec
                scores[p].append(_dot(qt, kd, _NT))
                da_i = da[p][lo:lo + HGRN_SUB, :]
                dq_blocks[p].append(_dot_f32(da_i, kd, _NN) * grow)
                dk_intra[p] = dk_intra[p] + _dot_f32(da_i, qt, _TN) * dec
        dv_intra = {p: _dot(jnp.where(lower, jnp.concatenate(scores[p], axis=0), 0.0), d_o[p], _TN) for p in pairs}
        dq = {p: dq[p] + jnp.concatenate(dq_blocks[p], axis=0) for p in pairs}
        q_dq = {p: q[p] * dq[p] for p in pairs}
        for p in pairs:
            dzq_ref[rws[p[0]], sls[p[1]]] = (dq[p] * sq[p] * (1.0 + zq_v[p] * (1.0 - sq[p]))).astype(BF16)
        dstate = [dstate_sc[hh] for hh in hs]
        for cc in reversed(range(cps)):
            ps = [(cc, hh) for hh in hs]
            dk_state = [_dot(v[p], dstate[p[1]], _NN) * dec_end[p] for p in ps]
            dv = [dv_intra[p] + _dot(k[p] * dec_end[p], dstate[p[1]], _NT) for p in ps]
            dk = [dk_intra[p] + dk_state[p[1]] for p in ps]
            db_last = [jnp.sum(k[p] * dk_state[p[1]], axis=0, keepdims=True)
                       + ebl[p] * jnp.sum(s0t[p] * dstate[p[1]], axis=0, keepdims=True) for p in ps]
            db = [q_dq[p] - k[p] * dk[p[1]] + jnp.where(last_row, db_last[p[1]], 0.0) for p in ps]
            df = [_dot_f32(upper, db[p[1]]) / f[p] - dk[p[1]] for p in ps]
            for p in ps:
                hh = p[1]
                dzf_ref[rws[cc], sls[hh]] = (df[hh] * (1.0 - lb[hh]) * sf[p] * (1.0 - sf[p])).astype(BF16)
                dlb_ref[:, sls[hh]] += jnp.sum(df[hh] * (1.0 - sf[p]), axis=0, keepdims=True)
                dzi_ref[rws[cc], sls[hh]] = dv[hh].astype(BF16)
            dstate = [dstate[p[1]] * ebl[p] + dstate_in[p] for p in ps]
        for hh in hs:
            dstate_sc[hh] = dstate[hh]

    tile = pl.BlockSpec((cps * c, hp * LANES), lambda h, i: (n_steps - 1 - i, h))
    out = jax.ShapeDtypeStruct((s, d), BF16)
    return pl.pallas_call(
        body, name=name, grid=(h_n // hp, n_steps),
        in_specs=[tile, tile, tile, pl.BlockSpec((2, hp * LANES), lambda h, i: (0, h)),
                  pl.BlockSpec((hp, cps, LANES, LANES), lambda h, i: (h, n_steps - 1 - i, 0, 0)), tile,
                  pl.BlockSpec(memory_space=pl.ANY)],
        out_specs=[tile, tile, tile, pl.BlockSpec((1, hp * LANES), lambda h, i: (0, h))],
        out_shape=[out, out, out, jax.ShapeDtypeStruct((1, d), F32)],
        scratch_shapes=[pltpu.VMEM((hp, LANES, LANES), F32), pltpu.VMEM((cps, hp, c, LANES), F32)],
        compiler_params=_params("parallel", "arbitrary"),
    )(zq, zf, zi, lb_logits, states, do, after)


ATTN_SUB_ROWS = 256
LOG2E = 1.4426950408889634
LN2 = 0.6931471805599453
Q_PRESCALE = ATTN_SCALE * LOG2E


def _attn_tile(s):
    return min(1024, max(128, s // 2))


def _causal_pairs(n, q_major):
    pairs = [(i, j) for i in range(n) for j in range(i + 1)] if q_major else [(i, j) for j in range(n) for i in range(j, n)]
    return jnp.asarray([p[0] for p in pairs], jnp.int32), jnp.asarray([p[1] for p in pairs], jnp.int32)


def _sub_scores(qn_ref, qr_ref, k, r, sub, t, diagonal):
    q = jnp.concatenate([qn_ref[r:r + sub, :], qr_ref[r:r + sub, :]], axis=1)
    if not diagonal:
        return q, _dot(q, k, _NT)
    cols = r + sub
    keep = lax.broadcasted_iota(jnp.int32, (sub, cols), 1) <= r + lax.broadcasted_iota(jnp.int32, (sub, cols), 0)
    return q, jnp.where(keep, _dot(q, k[:cols], _NT), -jnp.inf)


def _attn_fwd(qn, qr, kn, kr, v, *, name):
    s, t = qn.shape[0], _attn_tile(qn.shape[0])
    sub = min(t, ATTN_SUB_ROWS)
    q_blk, k_blk = _causal_pairs(s // t, True)

    def body(qi_ref, kj_ref, qn_ref, qr_ref, kn_ref, kr_ref, v_ref, o_ref, lse_ref, m_sc, l_sc, acc_sc):
        p_id = pl.program_id(1)
        i, j = qi_ref[p_id], kj_ref[p_id]

        @pl.when(j == 0)
        def _():
            m_sc[...] = jnp.full_like(m_sc, -jnp.inf)
            l_sc[...] = jnp.zeros_like(l_sc)
            acc_sc[...] = jnp.zeros_like(acc_sc)

        def update(diagonal):
            k = jnp.concatenate([kn_ref[...], kr_ref[...]], axis=1)
            v = v_ref[...]
            starts = list(range(0, t, sub))
            scs = [_sub_scores(qn_ref, qr_ref, k, r, sub, t, diagonal)[1] for r in starts]
            ps, alphas = [], []
            for r, sc in zip(starts, scs):
                m_prev = m_sc[r:r + sub, :]
                m_new = jnp.maximum(m_prev, jnp.max(sc, axis=1, keepdims=True))
                alpha = jnp.exp2(m_prev - m_new)
                p = jnp.exp2(sc - m_new[:, :1])
                l_sc[r:r + sub, :] = alpha * l_sc[r:r + sub, :] + jnp.sum(p, axis=1, keepdims=True)
                m_sc[r:r + sub, :] = m_new
                ps.append(p)
                alphas.append(alpha)
            for r, p, alpha in zip(starts, ps, alphas):
                acc_sc[r:r + sub, :] = alpha * acc_sc[r:r + sub, :] + _dot(p, v[:p.shape[1]], _NN)

        @pl.when(j < i)
        def _():
            update(False)

        @pl.when(j == i)
        def _():
            update(True)
            o_ref[...] = (acc_sc[...] / l_sc[...]).astype(BF16)
            lse_ref[...] = m_sc[...] + jnp.log(l_sc[...]) * LOG2E

    q_spec = pl.BlockSpec((t, LANES), lambda h, p, qi, kj: (qi[p], h))
    k_spec = pl.BlockSpec((t, LANES), lambda h, p, qi, kj: (kj[p], h))
    kr_spec = pl.BlockSpec((t, LANES), lambda h, p, qi, kj: (kj[p], 0))
    stat = pltpu.VMEM((t, LANES), F32)
    return pl.pallas_call(
        body, name=name,
        grid_spec=pltpu.PrefetchScalarGridSpec(
            num_scalar_prefetch=2, grid=(MLA_HEADS, q_blk.shape[0]),
            in_specs=[q_spec, q_spec, k_spec, kr_spec, k_spec], out_specs=[q_spec, q_spec],
            scratch_shapes=[stat, stat, stat]),
        out_shape=[jax.ShapeDtypeStruct(qn.shape, BF16), jax.ShapeDtypeStruct(qn.shape, F32)],
        compiler_params=_params("parallel", "arbitrary"),
    )(q_blk, k_blk, qn, qr, kn, kr, v)


def _attn_bwd(qn, qr, kn, kr, v, do, lse, delta, *, name):
    s, t = qn.shape[0], _attn_tile(qn.shape[0])
    n, sub = s // t, min(t, ATTN_SUB_ROWS)
    q_blk, k_blk = _causal_pairs(n, False)

    def body(qi_ref, kj_ref, qn_ref, qr_ref, kn_ref, kr_ref, v_ref, do_ref, lse_ref, delta_ref,
             dqn_ref, dqr_ref, dkn_ref, dv_ref, dkr_ref, dk_sc, dv_sc):
        p_id = pl.program_id(1)
        i, j = qi_ref[p_id], kj_ref[p_id]

        @pl.when(p_id == 0)
        def _():
            dqn_ref[...] = jnp.zeros_like(dqn_ref)
            dqr_ref[...] = jnp.zeros_like(dqr_ref)

        @pl.when(i == j)
        def _():
            dk_sc[...] = jnp.zeros_like(dk_sc)
            dv_sc[...] = jnp.zeros_like(dv_sc)

        def accumulate(diagonal):
            k = jnp.concatenate([kn_ref[...], kr_ref[...]], axis=1)
            v = v_ref[...]
            starts = list(range(0, t, sub))
            qs, d_os, scs, dps = [], [], [], []
            for r in starts:
                q, sc = _sub_scores(qn_ref, qr_ref, k, r, sub, t, diagonal)
                d_o = do_ref[r:r + sub, :]
                qs.append(q)
                d_os.append(d_o)
                scs.append(sc)
                dps.append(_dot(d_o, v[:sc.shape[1]], _NT))
            ps, dss = [], []
            for r, sc, dp in zip(starts, scs, dps):
                p = jnp.exp2(sc - lse_ref[r:r + sub, :][:, :1])
                ps.append(p.astype(BF16))
                dss.append((p * (dp - delta_ref[r:r + sub, :][:, :1])).astype(BF16))
            for r, q, d_o, p, ds in zip(starts, qs, d_os, ps, dss):
                cols = p.shape[1]
                dv_sc[:cols, :] += _dot(p, d_o, _TN)
                dk_sc[:cols, :] += _dot(ds, q, _TN)
                dq = _dot(ds, k[:cols], _NN) * ATTN_SCALE
                rows = pl.ds(pl.multiple_of(i * t + r, sub), sub)
                dqn_ref[rows, :] += dq[:, :LANES]
                dqr_ref[rows, :] += dq[:, LANES:]

        @pl.when(j < i)
        def _():
            accumulate(False)

        @pl.when(j == i)
        def _():
            accumulate(True)

        @pl.when(i == n - 1)
        def _():
            dkn_ref[...] = (dk_sc[:, :LANES] * LN2).astype(BF16)
            dkr_ref[...] = dk_sc[:, LANES:] * LN2
            dv_ref[...] = dv_sc[...].astype(BF16)

    q_spec = pl.BlockSpec((t, LANES), lambda h, p, qi, kj: (qi[p], h))
    k_spec = pl.BlockSpec((t, LANES), lambda h, p, qi, kj: (kj[p], h))
    kr_spec = pl.BlockSpec((t, LANES), lambda h, p, qi, kj: (kj[p], 0))
    head_spec = pl.BlockSpec((s, LANES), lambda h, p, qi, kj: (0, h))
    f32_out, bf16_out = jax.ShapeDtypeStruct(qn.shape, F32), jax.ShapeDtypeStruct(qn.shape, BF16)
    return pl.pallas_call(
        body, name=name,
        grid_spec=pltpu.PrefetchScalarGridSpec(
            num_scalar_prefetch=2, grid=(MLA_HEADS, q_blk.shape[0]),
            in_specs=[q_spec, q_spec, k_spec, kr_spec, k_spec, q_spec, q_spec, q_spec],
            out_specs=[head_spec, head_spec, k_spec, k_spec, k_spec],
            scratch_shapes=[pltpu.VMEM((t, 2 * LANES), F32), pltpu.VMEM((t, LANES), F32)]),
        out_shape=[f32_out, f32_out, bf16_out, bf16_out, f32_out],
        compiler_params=_params("parallel", "arbitrary"),
    )(q_blk, k_blk, qn, qr, kn, kr, v, do, lse, delta)


def _exchange(arrs, *, scatter, name):
    n = len(arrs)
    out_shape = [jax.ShapeDtypeStruct(a.shape if scatter else (N_DEV, *a.shape), a.dtype) for a in arrs]

    def body(*refs):
        ins, outs = refs[:n], refs[n:2 * n]
        send_sems, recv_sems, local_sems = refs[2 * n:]
        x, y, c = lax.axis_index("x"), lax.axis_index("y"), lax.axis_index("c")
        me = 4 * x + 2 * y + c
        copies = []
        for k in range(n):
            local = pltpu.make_async_copy(ins[k].at[me] if scatter else ins[k], outs[k].at[me], local_sems.at[k])
            local.start()
            copies.append(local)
            for d in range(1, N_DEV):
                px, py, pc = (x + (d >> 2)) % 2, (y + ((d >> 1) & 1)) % 2, (c + (d & 1)) % 2
                peer = 4 * px + 2 * py + pc
                remote = pltpu.make_async_remote_copy(
                    src_ref=ins[k].at[peer] if scatter else ins[k], dst_ref=outs[k].at[me],
                    send_sem=send_sems.at[k, d - 1], recv_sem=recv_sems.at[k, d - 1],
                    device_id=(px, py, pc), device_id_type=pl.DeviceIdType.MESH)
                remote.start()
                copies.append(remote)
        for cp in copies:
            cp.wait()

    any_spec = pl.BlockSpec(memory_space=pl.ANY)
    return pl.pallas_call(
        body, name=name, in_specs=[any_spec] * n, out_specs=[any_spec] * n, out_shape=out_shape,
        scratch_shapes=[pltpu.SemaphoreType.DMA((n, N_DEV - 1)), pltpu.SemaphoreType.DMA((n, N_DEV - 1)),
                        pltpu.SemaphoreType.DMA((n,))],
    )(*arrs)


def _peers(x, y, c):
    out = []
    for d in range(1, N_DEV):
        px, py, pc = (x + (d >> 2)) % 2, (y + ((d >> 1) & 1)) % 2, (c + (d & 1)) % 2
        out.append(((px, py, pc), 4 * px + 2 * py + pc))
    return out


def _exchange_copies(ins, lands, send_sems, recv_sems, scatter):
    x, y, c = lax.axis_index("x"), lax.axis_index("y"), lax.axis_index("c")
    me = 4 * x + 2 * y + c
    local, remote = [], []
    for k in range(len(ins)):
        local.append(pltpu.make_async_copy(ins[k].at[me] if scatter else ins[k], lands[k].at[me],
                                           recv_sems.at[k * N_DEV + N_DEV - 1]))
        for d, (coords, peer) in enumerate(_peers(x, y, c)):
            remote.append(pltpu.make_async_remote_copy(
                src_ref=ins[k].at[peer] if scatter else ins[k], dst_ref=lands[k].at[me],
                send_sem=send_sems.at[k * N_DEV + d], recv_sem=recv_sems.at[k * N_DEV + d],
                device_id=coords, device_id_type=pl.DeviceIdType.MESH))
    return local, remote


def _exchange_start(arrs, *, scatter, name, after=None):
    n = len(arrs)
    hbm = pl.BlockSpec(memory_space=pltpu.HBM)
    sem = pl.BlockSpec(memory_space=pltpu.SEMAPHORE)
    lands = [lax.empty(a.shape if scatter else (N_DEV, *a.shape), a.dtype) for a in arrs]

    def body(*refs):
        ins, land_refs = refs[:n], refs[n:2 * n]
        first_out = 2 * n + (after is not None)
        send_sems, recv_sems, token = refs[first_out], refs[first_out + 1], refs[-1]
        local, remote = _exchange_copies(ins, land_refs, send_sems, recv_sems, scatter)
        for cp in local + remote:
            cp.start()
        token[...] = jnp.zeros_like(token)

    operands = [pltpu.with_memory_space_constraint(a, pltpu.HBM) for a in list(arrs) + lands]
    behind = [] if after is None else [after]
    res = pl.pallas_call(
        body, name=name,
        out_shape=(pltpu.SemaphoreType.DMA((n * N_DEV,)), pltpu.SemaphoreType.DMA((n * N_DEV,)),
                   *[pltpu.HBM(o.shape, o.dtype) for o in operands], jax.ShapeDtypeStruct((8, LANES), F32)),
        in_specs=[hbm] * (2 * n) + [pl.BlockSpec(memory_space=pl.ANY)] * len(behind),
        out_specs=(sem, sem, *[hbm] * (2 * n), pl.BlockSpec(memory_space=pltpu.VMEM)),
        input_output_aliases={i: 2 + i for i in range(2 * n)},
        compiler_params=pltpu.CompilerParams(has_side_effects=pltpu.SideEffectType.DATAFLOW_SIDE_EFFECTING),
    )(*operands, *behind)
    return (res[0], res[1], list(res[2:2 + n]), list(res[2 + n:2 + 2 * n]), scatter), res[-1]


def _exchange_wait(state, after, *, name):
    send_sems, recv_sems, ins, lands, scatter = state
    n = len(ins)
    hbm = pl.BlockSpec(memory_space=pltpu.HBM)
    sem = pl.BlockSpec(memory_space=pltpu.SEMAPHORE)

    def body(*refs):
        in_refs, land_refs = refs[:n], refs[n:2 * n]
        local, remote = _exchange_copies(in_refs, land_refs, refs[2 * n], refs[2 * n + 1], scatter)
        for cp in local:
            cp.wait()
        for cp in remote:
            cp.wait_send()
            cp.wait_recv()

    res = pl.pallas_call(
        body, name=name, out_shape=tuple(pltpu.HBM(o.shape, o.dtype) for o in ins + lands),
        in_specs=[hbm] * (2 * n) + [sem, sem, pl.BlockSpec(memory_space=pl.ANY)], out_specs=tuple([hbm] * (2 * n)),
        input_output_aliases={i: i for i in range(2 * n)},
        compiler_params=pltpu.CompilerParams(has_side_effects=pltpu.SideEffectType.DATAFLOW_SIDE_EFFECTING),
    )(*ins, *lands, send_sems, recv_sems, after)
    return list(res[n:])


def _adam(w, terms, m, v, *, name):
    n_layers, r, c = w.shape
    tr = min(r, 128)
    assert r % tr == 0 and len(terms) == n_layers
    steps = r // tr

    def body(w_ref, *rest):
        t_refs, (m_ref, v_ref, g_out, d_out, m_out, v_out) = rest[:n_layers], rest[n_layers:]
        for layer, t_ref in enumerate(t_refs):
            @pl.when(pl.program_id(0) == layer)
            def _(t_ref=t_ref):
                g = t_ref[0].astype(F32)
                for s in range(1, t_ref.shape[0]):
                    g = g + t_ref[s].astype(F32)
                m1 = ADAM_B1 * m_ref[...] + (1.0 - ADAM_B1) * g
                v1 = ADAM_B2 * v_ref[...] + (1.0 - ADAM_B2) * jnp.square(g)
                m_hat = m1 / (1.0 - ADAM_B1 ** ADAM_STEP)
                v_hat = v1 / (1.0 - ADAM_B2 ** ADAM_STEP)
                g_out[...] = g
                d_out[...] = -ADAM_LR * (m_hat / (jnp.sqrt(v_hat) + ADAM_EPS) + ADAM_WD * w_ref[...])
                m_out[...] = m1
                v_out[...] = v1

    def term_spec(layer, t):
        return pl.BlockSpec((t.shape[0], tr, c),
                            lambda l, i: (0, jnp.where(l == layer, i, jnp.where(l < layer, 0, steps - 1)), 0))

    spec = pl.BlockSpec((None, tr, c), lambda l, i: (l, i, 0))
    out = jax.ShapeDtypeStruct(w.shape, F32)
    return pl.pallas_call(
        body, name=name, grid=(n_layers, steps),
        in_specs=[spec] + [term_spec(layer, t) for layer, t in enumerate(terms)] + [spec, spec], out_specs=[spec] * 4,
        out_shape=[out] * 4, compiler_params=_params("arbitrary", "arbitrary"),
    )(w, *terms, m, v)


def _sum_terms(terms, *, name):
    n, _, p = terms.shape

    def body(t_ref, o_ref):
        acc = t_ref[0]
        for s in range(1, n):
            acc = acc + t_ref[s]
        o_ref[...] = acc

    return pl.pallas_call(body, name=name, out_shape=jax.ShapeDtypeStruct((1, p), F32))(terms)


def _lb_logits_grad(dlb, logits, *, name):
    def body(dlb_ref, l_ref, o_ref):
        lb = _lower_bound(l_ref[...])
        d0 = dlb_ref[...] * lb * (1.0 - lb)
        o_ref[...] = jnp.concatenate([d0, -d0], axis=0)

    return pl.pallas_call(body, name=name, out_shape=jax.ShapeDtypeStruct(logits.shape, F32))(dlb, logits)


def _silu_grad(z):
    sg = _sigmoid(z)
    return sg * (1.0 + z * (1.0 - sg))


def _head_norm_gate(o, zg, gn):
    outs = []
    for h in range(HGRN_HEADS):
        sl = slice(h * LANES, (h + 1) * LANES)
        zg_h = zg[:, sl]
        outs.append(_rms(o[:, sl], gn) * (zg_h * _sigmoid(zg_h)))
    return (jnp.concatenate(outs, axis=1),)


def _head_norm_gate_bwd(o, zg, dm, gn):
    do_parts, dzg_parts, dgn = [], [], jnp.zeros((1, LANES), F32)
    for h in range(HGRN_HEADS):
        sl = slice(h * LANES, (h + 1) * LANES)
        o_h, zg_h, dm_h = o[:, sl], zg[:, sl], dm[:, sl]
        gate = zg_h * _sigmoid(zg_h)
        do_h, dgn_h = _rms_bwd(o_h, gn, dm_h * gate)
        dgn = dgn + dgn_h
        do_parts.append(do_h)
        dzg_parts.append(dm_h * _rms(o_h, gn) * _silu_grad(zg_h))
    return jnp.concatenate(do_parts, axis=1), jnp.concatenate(dzg_parts, axis=1), dgn


def _rope_slabs(x, t_c, t_s1, t_s2, transpose):
    fn = _rope_t if transpose else _rope
    return jnp.concatenate(
        [fn(x[:, h * LANES:(h + 1) * LANES], t_c, t_s1, t_s2) for h in range(x.shape[1] // LANES)], axis=1)


def _loss_head(h, tgt, w):
    d = h.shape[1]
    r = lax.rsqrt(jnp.mean(h * h, axis=-1, keepdims=True) + EPS)
    xh = h * r
    err = xh * w - tgt
    loss = 0.5 * jnp.sum(jnp.mean(err * err, axis=-1, keepdims=True), axis=0, keepdims=True)
    dy = err / d
    dxh = dy * w
    dh = r * (dxh - xh * jnp.mean(dxh * xh, axis=-1, keepdims=True))
    return dh, dh, jnp.sum(dy * xh, axis=0, keepdims=True), jnp.broadcast_to(loss, (1, LANES))


def _mlp_fwd(h, norm, w_up, w_down, tag, loss_head=None):
    d = h.shape[1]

    def up(x, g, wu):
        x_n = _rms(x, g).astype(BF16)
        return x_n, jnp.concatenate([jnp.square(jnp.maximum(_dot(x_n, wu[j], _NN), 0.0)) for j in range(wu.shape[0])],
                                    axis=1)

    xn, act = _rowcall(up, [h], [norm, w_up], [(d, BF16), (w_up.shape[0] * w_up.shape[2], BF16)], [], tr=512,
                       name=f"{tag}_up")
    if callable(w_down):
        w_down = w_down(act)
    if loss_head is None:
        return _mm(act, w_down, mode="nn", add=h, name=f"{tag}_down"), (h, xn, act)
    tgt, final_norm = loss_head

    def down_and_loss(a, res, t, wd, g):
        return _loss_head(res + _dot(a, wd, _NN), t, g)

    return _rowcall(down_and_loss, [act, h, tgt], [w_down, final_norm], [(d, F32), (d, BF16)], [d, LANES],
                    name=f"{tag}_down_loss"), (h, xn, act)


def _mlp_bwd(dh_out, dh_out_bf, saved, norm, w_up, w_down, tag, after=None):
    h, xn, act = saved
    d = h.shape[1]
    du = _mm(dh_out_bf, w_down, mode="nt", relu2_of=act, out_dtype=BF16, after=after, name=f"{tag}_bwd_du")
    dw_down = _mm(act, dh_out_bf, mode="tn", name=f"{tag}_bwd_wdown")
    dw_up = _mm(xn, du, mode="tn", col_shards=w_up.shape[0], name=f"{tag}_bwd_wup")

    def up_norm_bwd(x, d_u, dres, g, wu):
        cols = wu.shape[2]
        dxn = _dot(d_u[:, :cols], wu[0], _NT)
        for j in range(1, wu.shape[0]):
            dxn = dxn + _dot(d_u[:, j * cols:(j + 1) * cols], wu[j], _NT)
        dx, dw = _rms_bwd(x, g, dxn)
        return dx + dres, dx + dres, dw

    dh, dh_bf, dnorm = _rowcall(up_norm_bwd, [h, du, dh_out], [norm, w_up], [(d, F32), (d, BF16)], [d], tr=512,
                                name=f"{tag}_bwd_dxn")
    return dh, dh_bf, dnorm, dw_up, dw_down


def _row_major(g):
    return g.reshape(g.shape[0] * g.shape[1], g.shape[2])


def _col_major(g):
    return jnp.transpose(g, (1, 0, 2)).reshape(g.shape[1], g.shape[0] * g.shape[2])


def _col_terms(dw):
    k, n = dw.shape
    return jnp.transpose(dw.reshape(k, N_DEV, n // N_DEV), (1, 0, 2))


def _row_terms(dw):
    return dw.reshape(N_DEV, dw.shape[0] // N_DEV, dw.shape[1])


def kernel(x, hgrn_norm, hgrn_w_q, hgrn_w_f, hgrn_w_i, hgrn_w_g, hgrn_g_norm, hgrn_w_o, hgrn_lb_logits, mla_norm, mla_w_dq, mla_q_norm, mla_w_uq, mla_w_o, kv_in_norm, kv_w_dkv, kv_norm, kv_w_uk, kv_w_uv, mlp_norm, mlp_w_up, mlp_w_down, final_norm, loss_target, m_hgrn_norm, m_hgrn_w_q, m_hgrn_w_f, m_hgrn_w_i, m_hgrn_w_g, m_hgrn_g_norm, m_hgrn_w_o, m_hgrn_lb_logits, m_mla_norm, m_mla_w_dq, m_mla_q_norm, m_mla_w_uq, m_mla_w_o, m_kv_in_norm, m_kv_w_dkv, m_kv_norm, m_kv_w_uk, m_kv_w_uv, m_mlp_norm, m_mlp_w_up, m_mlp_w_down, m_final_norm, v_hgrn_norm, v_hgrn_w_q, v_hgrn_w_f, v_hgrn_w_i, v_hgrn_w_g, v_hgrn_g_norm, v_hgrn_w_o, v_hgrn_lb_logits, v_mla_norm, v_mla_w_dq, v_mla_q_norm, v_mla_w_uq, v_mla_w_o, v_kv_in_norm, v_kv_w_dkv, v_kv_norm, v_kv_w_uk, v_kv_w_uv, v_mlp_norm, v_mlp_w_up, v_mlp_w_down, v_final_norm):
    given = dict(locals())
    weight_names = ["hgrn_norm", "hgrn_w_q", "hgrn_w_f", "hgrn_w_i", "hgrn_w_g", "hgrn_g_norm", "hgrn_w_o",
                    "hgrn_lb_logits", "mla_norm", "mla_w_dq", "mla_q_norm", "mla_w_uq", "mla_w_o", "kv_in_norm",
                    "kv_w_dkv", "kv_norm", "kv_w_uk", "kv_w_uv", "mlp_norm", "mlp_w_up", "mlp_w_down", "final_norm"]
    me = 4 * lax.axis_index("x") + 2 * lax.axis_index("y") + lax.axis_index("c")
    xs, tgt = x[0], loss_target[0]
    seq, d_model = xs.shape
    n_heads, hd = MLA_HEADS, LANES

    big_local = {
        "hgrn_w_q": hgrn_w_q[0], "hgrn_w_f": hgrn_w_f[0], "hgrn_w_i": hgrn_w_i[0], "hgrn_w_g": hgrn_w_g[0],
        "hgrn_w_o": hgrn_w_o[0], "mla_w_dq": mla_w_dq[0], "mla_w_uq": mla_w_uq[0], "mla_w_o": mla_w_o[0],
        "kv_w_dkv": kv_w_dkv, "kv_w_uk": kv_w_uk, "kv_w_uv": kv_w_uv,
        "mlp_w_up0": mlp_w_up[0], "mlp_w_up1": mlp_w_up[1], "mlp_w_down0": mlp_w_down[0], "mlp_w_down1": mlp_w_down[1],
    }
    big_names = list(big_local)
    col_sharded = {"mla_w_uq", "kv_w_uk", "kv_w_uv"}
    shard_major = {"mlp_w_up0", "mlp_w_up1"}
    vec_local = jnp.concatenate([hgrn_norm, hgrn_lb_logits], axis=0)
    first_names = ["hgrn_w_q", "hgrn_w_f", "hgrn_w_i"]
    proj_names = first_names + ["hgrn_w_g"]
    later_names = {"hgrn_o": ["hgrn_w_g", "hgrn_w_o"], "up0": ["mlp_w_up0"], "down0": ["mlp_w_down0"],
                   "mla": ["kv_w_dkv", "kv_w_uk", "kv_w_uv", "mla_w_dq", "mla_w_uq", "mla_w_o"],
                   "mlp1": ["mlp_w_up1", "mlp_w_down1"]}

    def unshard(names, arrays):
        return {k: (a if k in shard_major else _col_major(a) if k in col_sharded else _row_major(a))
                for k, a in zip(names, arrays)}

    first_state, token = _exchange_start([big_local[k].astype(BF16) for k in first_names] + [vec_local], scatter=False,
                                         name="gather_first_start")
    gather_state = {}
    for tag, names in later_names.items():
        gather_state[tag], token = _exchange_start([big_local[k].astype(BF16) for k in names], scatter=False,
                                                   after=token, name=f"gather_{tag}_start")

    def gather_wait(tag, after):
        w.update(unshard(later_names[tag], _exchange_wait(gather_state[tag], after, name=f"gather_{tag}_wait")))
        return [w[k] for k in later_names[tag]]

    gathered = _exchange_wait(first_state, token, name="gather_first_wait")
    w = unshard(first_names, gathered[:-1])
    vec_full = jnp.transpose(gathered[-1], (1, 0, 2)).reshape(3, d_model)
    hgrn_norm_full, lb_logits_full = vec_full[0:1], vec_full[1:3]
    t_c, t_s1, t_s2 = _rope_tables(seq)
    kv_lora = kv_w_uk.shape[0]

    def hgrn_proj(a, g, *weights):
        xn = _rms(a, g).astype(BF16)
        return (xn, *[_dot(xn, wt, _NN) for wt in weights])

    xn0, zq, zf, zi = _rowcall(hgrn_proj, [xs], [hgrn_norm_full] + [w[k] for k in first_names],
                               [(d_model, BF16)] + [(d_model, F32)] * 3, [], tr=512, name="hgrn_proj")
    o_rec, states = _hgrn_fwd(zq, zf, zi, lb_logits_full, name="hgrn_fwd")
    gather_wait("hgrn_o", o_rec)

    def gate_out(o, x_n, res, gn, wg, wo):
        z = _dot(x_n, wg, _NN)
        m = _head_norm_gate(o, z, gn)[0].astype(BF16)
        return z, m, res + _dot(m, wo, _NN)

    zg, mixed, h1 = _rowcall(gate_out, [o_rec, xn0, xs], [hgrn_g_norm, w["hgrn_w_g"], w["hgrn_w_o"]],
                             [(d_model, F32), (d_model, BF16), (d_model, F32)], [], name="hgrn_gate_out")
    h2, mlp0_saved = _mlp_fwd(h1, mlp_norm[0:1], gather_wait("up0", h1)[0], lambda act: gather_wait("down0", act)[0],
                              "mlp0")
    gather_wait("mla", h2)
    w_uq3 = w["mla_w_uq"].reshape(-1, n_heads, MLA_NOPE + MLA_ROPE)
    w_uq_nope = w_uq3[:, :, :MLA_NOPE].reshape(-1, n_heads * hd)
    w_uq_rope = jnp.pad(w_uq3[:, :, MLA_NOPE:], ((0, 0), (0, 0), (0, hd - MLA_ROPE))).reshape(-1, n_heads * hd)
    w_dkv_pad = jnp.pad(w["kv_w_dkv"], ((0, 0), (0, kv_lora + hd - w["kv_w_dkv"].shape[1])))

    q_lora, qk_cols = w["mla_w_dq"].shape[1], n_heads * hd

    def mla_qkv(a, tc, ts1, ts2, g_kv_in, g_mla, g_q, g_kv, wdq, wn, wr, wdkv, wuk, wuv):
        h_n, x_n = _rms(a, g_kv_in).astype(BF16), _rms(a, g_mla).astype(BF16)
        cq = _dot(x_n, wdq, _NN)
        cq_n = _rms(cq, g_q).astype(BF16)
        q_nope = _dot(cq_n, wn, _NN) * Q_PRESCALE
        q_rope = _rope_slabs(_dot(cq_n, wr, _NN) * Q_PRESCALE, tc, ts1, ts2, False)
        c_all = _dot(h_n, wdkv, _NN)
        lat = _rms(c_all[:, :kv_lora], g_kv).astype(BF16)
        return (h_n, x_n, cq, cq_n, q_nope, q_rope, c_all, lat, _rope(c_all[:, kv_lora:], tc, ts1, ts2),
                _dot(lat, wuk, _NN), _dot(lat, wuv, _NN))

    hn, xn2, cq_pre, c_q, qn, qr, ckr, c_kv, kr, kn, vv = _rowcall(
        mla_qkv, [h2, t_c, t_s1, t_s2],
        [kv_in_norm[None, :], mla_norm, mla_q_norm, kv_norm[None, :], w["mla_w_dq"], w_uq_nope, w_uq_rope, w_dkv_pad,
         w["kv_w_uk"], w["kv_w_uv"]],
        [(d_model, BF16), (d_model, BF16), (q_lora, F32), (q_lora, BF16), (qk_cols, BF16), (qk_cols, BF16),
         (kv_lora + hd, F32), (kv_lora, BF16), (hd, BF16), (qk_cols, BF16), (qk_cols, BF16)], [], tr=512, name="mla_qkv")
    o_att, lse = _attn_fwd(qn, qr, kn, kr, vv, name="attn_fwd")
    h3 = _mm(o_att, w["mla_w_o"], mode="nn", add=h2, name="attn_out")
    gather_wait("mlp1", h3)
    (dh4, dh4_bf, g_final_norm, loss_part), mlp1_saved = _mlp_fwd(
        h3, mlp_norm[1:2], w["mlp_w_up1"], w["mlp_w_down1"], "mlp1", loss_head=(tgt, final_norm[None, :]))

    g = {}
    groups = {"mlp1": ["mlp_w_up1", "mlp_w_down1"],
              "mla": ["mla_w_o", "mla_w_uq", "mla_w_dq", "kv_w_uk", "kv_w_uv", "kv_w_dkv"],
              "mlp0": ["mlp_w_up0", "mlp_w_down0"],
              "hgrn_out": ["hgrn_w_o", "hgrn_w_g"],
              "hgrn_in": ["hgrn_w_q", "hgrn_w_f", "hgrn_w_i"]}
    scatter_state = {}

    def scatter_start(tag, after=None):
        scatter_state[tag], tok = _exchange_start(
            [g[k] if k in shard_major else (_col_terms if k in col_sharded else _row_terms)(g[k]) for k in groups[tag]],
            scatter=True, after=after,
            name=f"scatter_{tag}_start")
        return tok

    dh3, dh3_bf, g_mlp_norm1, g["mlp_w_up1"], g["mlp_w_down1"] = _mlp_bwd(
        dh4, dh4_bf, mlp1_saved, mlp_norm[1:2], w["mlp_w_up1"], w["mlp_w_down1"], "mlp1")
    def attn_out_bwd(dres, o, wo):
        d_o = _dot(dres, wo, _NT).astype(BF16)
        prod = d_o.astype(F32) * o.astype(F32)
        return d_o, jnp.concatenate([jnp.broadcast_to(jnp.sum(prod[:, h * hd:(h + 1) * hd], axis=1, keepdims=True),
                                                      (prod.shape[0], hd)) for h in range(n_heads)], axis=1)

    d_oatt, delta = _rowcall(attn_out_bwd, [dh3_bf, o_att], [w["mla_w_o"]], [(qk_cols, BF16), (qk_cols, F32)], [],
                             after=scatter_start("mlp1"), name="attn_out_bwd_x")
    g["mla_w_o"] = _mm(o_att, dh3_bf, mode="tn", name="attn_out_bwd_w")
    dqn, dqr, dkn, dvv, dkr = _attn_bwd(qn, qr, kn, kr, vv, d_oatt, lse, delta, name="attn_bwd")

    def q_path_bwd(cq, cq_n, x_n, d_qn, d_qr, tc, ts1, ts2, g_q, wdq, wn, wr):
        d_qn, d_qr = d_qn.astype(BF16), _rope_slabs(d_qr, tc, ts1, ts2, True).astype(BF16)
        d_cq, d_gq = _rms_bwd(cq, g_q, _dot(d_qn, wn, _NT) + _dot(d_qr, wr, _NT))
        d_cq = d_cq.astype(BF16)
        return _dot(d_cq, wdq, _NT), d_gq, _dot(x_n, d_cq, _TN), _dot(cq_n, d_qn, _TN), _dot(cq_n, d_qr, _TN)

    dxn2, g_q_norm, g_dq, g_uq_nope, g_uq_rope = _rowcall(
        q_path_bwd, [cq_pre, c_q, xn2, dqn, dqr, t_c, t_s1, t_s2], [mla_q_norm, w["mla_w_dq"], w_uq_nope, w_uq_rope],
        [(d_model, F32)], [q_lora, (d_model, q_lora), (q_lora, qk_cols), (q_lora, qk_cols)], tr=512, name="mla_q_bwd")
    g["mla_w_dq"] = g_dq.astype(GRAD_WIRE_DTYPE)
    g["mla_w_uq"] = jnp.concatenate([g_uq_nope.reshape(q_lora, n_heads, hd),
                                     g_uq_rope.reshape(q_lora, n_heads, hd)[:, :, :MLA_ROPE]],
                                    axis=2).reshape(q_lora, -1).astype(GRAD_WIRE_DTYPE)

    def kv_path_bwd(c_all, lat, h_n, d_kn, d_v, d_kr_heads, tc, ts1, ts2, a, d_xn2, dres,
                    g_kv, g_kv_in, g_mla, wdkv, wuk, wuv):
        d_lat, d_gkv = _rms_bwd(c_all[:, :kv_lora], g_kv, _dot(d_kn, wuk, _NT) + _dot(d_v, wuv, _NT))
        d_kr = d_kr_heads[:, :hd]
        for h in range(1, n_heads):
            d_kr = d_kr + d_kr_heads[:, h * hd:(h + 1) * hd]
        d_all = jnp.concatenate([d_lat, _rope_t(d_kr, tc, ts1, ts2)], axis=1).astype(BF16)
        dx1, d_gkv_in = _rms_bwd(a, g_kv_in, _dot(d_all, wdkv, _NT))
        dx2, d_gmla = _rms_bwd(a, g_mla, d_xn2)
        d_a = dx1 + dx2 + dres
        return (d_a, d_a, d_gkv, d_gkv_in, d_gmla, _dot(h_n, d_all, _TN), _dot(lat, d_kn, _TN), _dot(lat, d_v, _TN))

    dh2, dh2_bf, g_kv_norm, g_kv_in_norm, g_mla_norm, g_dkv, g_uk, g_uv = _rowcall(
        kv_path_bwd, [ckr, c_kv, hn, dkn, dvv, dkr, t_c, t_s1, t_s2, h2, dxn2, dh3],
        [kv_norm[None, :], kv_in_norm[None, :], mla_norm, w_dkv_pad, w["kv_w_uk"], w["kv_w_uv"]],
        [(d_model, F32), (d_model, BF16)],
        [kv_lora, d_model, d_model, (d_model, kv_lora + hd), (kv_lora, qk_cols), (kv_lora, qk_cols)], name="mla_kv_bwd")
    g["kv_w_dkv"] = g_dkv[:, :kv_w_dkv.shape[1]].astype(GRAD_WIRE_DTYPE)
    g["kv_w_uk"], g["kv_w_uv"] = g_uk.astype(GRAD_WIRE_DTYPE), g_uv.astype(GRAD_WIRE_DTYPE)
    dh1, dh1_bf, g_mlp_norm0, g["mlp_w_up0"], g["mlp_w_down0"] = _mlp_bwd(
        dh2, dh2_bf, mlp0_saved, mlp_norm[0:1], w["mlp_w_up0"], w["mlp_w_down0"], "mlp0", after=scatter_start("mla"))

    g["hgrn_w_o"] = _mm(mixed, dh1_bf, mode="tn", after=scatter_start("mlp0"), name="hgrn_out_bwd_w")
    do_rec, dzg, g_g_norm = _rowcall(
        lambda dres, o, z, wo, gn: _head_norm_gate_bwd(o, z, _dot(dres, wo, _NT), gn), [dh1_bf, o_rec, zg],
        [w["hgrn_w_o"], hgrn_g_norm], [(d_model, F32), (d_model, BF16)], [hd], name="hgrn_gate_out_bwd")
    g["hgrn_w_g"] = _mm(xn0, dzg, mode="tn", name="hgrn_w_g_bwd_w")
    dzq, dzf, dzi, g_lb = _hgrn_bwd(zq, zf, zi, lb_logits_full, states, do_rec, scatter_start("hgrn_out"),
                                    name="hgrn_bwd")
    for nm, dz in (("hgrn_w_q", dzq), ("hgrn_w_f", dzf), ("hgrn_w_i", dzi)):
        g[nm] = _mm(xn0, dz, mode="tn", name=f"{nm}_bwd_w")

    def hgrn_proj_bwd(a, dres, *rest):
        dzs, gw, weights = rest[:4], rest[4], rest[5:]
        dxn = _dot(dzs[0], weights[0], _NT)
        for dz, wt in zip(dzs[1:], weights[1:]):
            dxn = dxn + _dot(dz, wt, _NT)
        dx, dw = _rms_bwd(a, gw, dxn)
        return dx + dres, dw

    grad_x, g_hgrn_norm = _rowcall(hgrn_proj_bwd, [xs, dh1, dzq, dzf, dzi, dzg],
                                   [hgrn_norm_full] + [w[k] for k in proj_names], [(d_model, F32)], [d_model],
                                   tr=512, name="hgrn_proj_bwd")

    small_parts = [g_hgrn_norm, g_lb, g_g_norm, g_mla_norm, g_q_norm, g_kv_in_norm, g_kv_norm, g_mlp_norm0,
                   g_mlp_norm1, g_final_norm, loss_part]
    small_sizes = [p.shape[1] for p in small_parts]
    small_terms = _exchange([jnp.concatenate(small_parts, axis=1)], scatter=False, name="gather_small")[0]
    small_sum = _sum_terms(small_terms, name="sum_small")
    last = scatter_start("hgrn_in", after=small_sum)
    offs = [0]
    for sz in small_sizes:
        offs.append(offs[-1] + sz)
    (s_hgrn_norm, s_lb, s_g_norm, s_mla_norm, s_q_norm, s_kv_in_norm, s_kv_norm, s_mlp_norm0, s_mlp_norm1, s_final_norm,
     s_loss) = [small_sum[:, a:b] for a, b in zip(offs[:-1], offs[1:])]
    shard = hgrn_norm.shape[1]
    g_lb_logits = _lb_logits_grad(lax.dynamic_slice_in_dim(s_lb, me * shard, shard, axis=1), hgrn_lb_logits,
                                  name="lb_logits_grad")
    loss = s_loss[0, 0]

    res, layer_terms = {}, {}

    def update(k, term_list):
        shape = given[k].shape
        as_layers = (len(term_list), shape[-2], shape[-1])
        upd = _adam(given[k].reshape(as_layers), term_list, given["m_" + k].reshape(as_layers),
                    given["v_" + k].reshape(as_layers), name=f"adam_{k}")
        res[k] = [o.reshape(shape) for o in upd]
        return upd[0]

    for tag, names in groups.items():
        for k, t in zip(names, _exchange_wait(scatter_state[tag], last, name=f"scatter_{tag}_wait")):
            if k.startswith("mlp_w_"):
                layer_terms.setdefault(k[:-1], {})[int(k[-1])] = t
                if len(layer_terms[k[:-1]]) == 2:
                    last = update(k[:-1], [layer_terms[k[:-1]][0], layer_terms[k[:-1]][1]])
            else:
                last = update(k, [t])

    small_grads = {
        "hgrn_norm": lax.dynamic_slice_in_dim(s_hgrn_norm, me * shard, shard, axis=1),
        "hgrn_g_norm": s_g_norm, "hgrn_lb_logits": g_lb_logits, "mla_norm": s_mla_norm, "mla_q_norm": s_q_norm,
        "kv_in_norm": s_kv_in_norm, "kv_norm": s_kv_norm,
        "mlp_norm": jnp.concatenate([s_mlp_norm0, s_mlp_norm1], axis=0), "final_norm": s_final_norm,
    }
    small_names = list(small_grads)

    def flat(a):
        return a.reshape(1, -1)

    packed = [jnp.concatenate([flat(src[pre + k]) for k in small_names], axis=1)
              for src, pre in ((given, ""), (small_grads, ""), (given, "m_"), (given, "v_"))]
    small_out = _adam(packed[0][None], [packed[1][None]], packed[2][None], packed[3][None], name="adam_small")
    off = 0
    for k in small_names:
        size = given[k].size
        res[k] = [o[0, :, off:off + size].reshape(given[k].shape) for o in small_out]
        off += size

    outs = [loss, grad_x[None]]
    for i in range(4):
        outs += [res[k][i] for k in weight_names]
    return tuple(outs)
```

```python
import functools

import jax
import jax.numpy as jnp
from jax import lax
from jax.experimental import pallas as pl
from jax.experimental.pallas import tpu as pltpu

F32 = jnp.float32
BF16 = jnp.bfloat16

EPS = 1e-6
LANES = 128
N_DEV = 8
V7X_VMEM_LIMIT_BYTES = 56 << 20
MM_PIPELINE_BYTES = 30 << 20
MM_ROW_TILE = 512
GRAD_WIRE_DTYPE = BF16

HGRN_HEADS = 8
HGRN_CHUNK = 64
HGRN_SUB = 16
HGRN_HEADS_PER_STEP = 8
HGRN_CHUNKS_PER_STEP = 2
EXP_CLAMP = 80.0
MLA_HEADS = 16
MLA_NOPE = 128
MLA_ROPE = 64
ROPE_THETA = 10000.0
ATTN_SCALE = (MLA_NOPE + MLA_ROPE) ** -0.5

ADAM_LR = 0.001
ADAM_B1 = 0.9
ADAM_B2 = 0.999
ADAM_EPS = 1e-08
ADAM_WD = 0.01
ADAM_STEP = 10

_NN = ((1,), (0,))
_NT = ((1,), (1,))
_TN = ((0,), (0,))


def _params(*sem):
    return pltpu.CompilerParams(dimension_semantics=sem, vmem_limit_bytes=V7X_VMEM_LIMIT_BYTES)


def _dot(a, b, dims):
    return lax.dot_general(a.astype(BF16), b.astype(BF16), (dims, ((), ())), preferred_element_type=F32)


def _dot_f32(a, b, dims=_NN):
    return lax.dot_general(a, b, (dims, ((), ())), precision=lax.Precision.HIGH, preferred_element_type=F32)


def _sigmoid(x):
    return 1.0 / (1.0 + jnp.exp(-x))


def _rms(x, w):
    r = lax.rsqrt(jnp.mean(x * x, axis=-1, keepdims=True) + EPS)
    return x * r * w


def _rms_bwd(x, w, dy):
    r = lax.rsqrt(jnp.mean(x * x, axis=-1, keepdims=True) + EPS)
    xh = x * r
    dw = jnp.sum(dy * xh, axis=0, keepdims=True)
    dxh = dy * w
    dx = r * (dxh - xh * jnp.mean(dxh * xh, axis=-1, keepdims=True))
    return dx, dw


def _mm_tiles(m, n, k, a_bytes, b_bytes, out_tile_bytes):
    tm = min(m, MM_ROW_TILE)
    for tn in (n, 2048, 1024, 512, 256, LANES):
        if tn <= n and n % tn == 0:
            if 2 * (tm * k * a_bytes + k * tn * b_bytes + tm * tn * out_tile_bytes) <= MM_PIPELINE_BYTES:
                return tm, tn
    return tm, min(n, LANES)


def _mm(a, b, *, mode, name, out_dtype=None, add=None, relu2_of=None, after=None, col_shards=None):
    if mode == "nn":
        (m, k), (k2, n) = a.shape, b.shape
    elif mode == "nt":
        (m, k), (n, k2) = a.shape, b.shape
    else:
        (k, m), (k2, n) = a.shape, b.shape
    assert k == k2, (name, a.shape, b.shape)
    if out_dtype is None:
        out_dtype = GRAD_WIRE_DTYPE if mode == "tn" else F32
    tile_bytes = sum(x.dtype.itemsize for x in (add, relu2_of) if x is not None) + jnp.dtype(out_dtype).itemsize
    tm, tn = _mm_tiles(m, n, k, a.dtype.itemsize, b.dtype.itemsize, tile_bytes)
    if col_shards is not None:
        assert add is None and relu2_of is None
        tn = n // col_shards
    assert m % tm == 0 and n % tn == 0, (name, m, n)
    dims = {"nn": _NN, "nt": _NT, "tn": _TN}[mode]
    a_spec = pl.BlockSpec((k, tm), lambda i, j: (0, i)) if mode == "tn" else pl.BlockSpec((tm, k), lambda i, j: (i, 0))
    b_spec = pl.BlockSpec((tn, k), lambda i, j: (j, 0)) if mode == "nt" else pl.BlockSpec((k, tn), lambda i, j: (0, j))
    o_spec = pl.BlockSpec((tm, tn), lambda i, j: (i, j))
    operands, in_specs = [a, b], [a_spec, b_spec]
    for extra in (add, relu2_of):
        if extra is not None:
            assert extra.shape == (m, n), (name, extra.shape)
            operands.append(extra)
            in_specs.append(o_spec)
    n_in = len(operands)
    if after is not None:
        operands.append(after)
        in_specs.append(pl.BlockSpec(memory_space=pl.ANY))
    out_shape = jax.ShapeDtypeStruct((m, n), out_dtype)
    if col_shards is not None:
        out_shape = jax.ShapeDtypeStruct((col_shards, m, tn), out_dtype)
        o_spec = pl.BlockSpec((None, tm, tn), lambda i, j: (j, i, 0))

    def body(*refs):
        acc = _dot(refs[0][...], refs[1][...], dims)
        extras, outs = refs[2:n_in], refs[len(operands):]
        if add is not None:
            acc = acc + extras[0][...]
        if relu2_of is not None:
            acc = acc * (2.0 * jnp.sqrt(extras[-1][...].astype(F32)))
        outs[0][...] = acc.astype(out_dtype)

    return pl.pallas_call(
        body, name=name, grid=(m // tm, n // tn), in_specs=in_specs, out_specs=o_spec, out_shape=out_shape,
        compiler_params=_params("parallel", "parallel"),
    )(*operands)


def _rowcall(fn, rows, consts, outs, accs, *, name, tr=256, after=None):
    s = rows[0].shape[0]
    tr = min(tr, s)
    assert s % tr == 0
    n_out = len(outs)
    accs = [(1, a) if isinstance(a, int) else a for a in accs]
    in_specs = [pl.BlockSpec((tr, r.shape[1]), lambda i: (i, 0)) for r in rows]
    in_specs += [pl.BlockSpec(c.shape, lambda i, nd=c.ndim: (0,) * nd) for c in consts]
    out_shape = [jax.ShapeDtypeStruct((s, w), dt) for w, dt in outs] + [jax.ShapeDtypeStruct(a, F32) for a in accs]
    out_specs = [pl.BlockSpec((tr, w), lambda i: (i, 0)) for w, _ in outs] + [pl.BlockSpec(a, lambda i: (0, 0)) for a in accs]
    n_in = len(rows) + len(consts)

    def body(*refs):
        res = fn(*[r[...] for r in refs[:n_in]])
        out_refs = refs[n_in + (after is not None):]
        for ref, val in zip(out_refs[:n_out], res[:n_out]):
            ref[...] = val.astype(ref.dtype)
        i = pl.program_id(0)
        for ref, val in zip(out_refs[n_out:], res[n_out:]):
            @pl.when(i == 0)
            def _(ref=ref, val=val):
                ref[...] = val

            @pl.when(i > 0)
            def _(ref=ref, val=val):
                ref[...] += val

    behind = [] if after is None else [after]
    return pl.pallas_call(
        body, name=name, grid=(s // tr,), in_specs=in_specs + [pl.BlockSpec(memory_space=pl.ANY)] * len(behind),
        out_specs=out_specs, out_shape=out_shape, compiler_params=_params("arbitrary" if accs else "parallel"),
    )(*rows, *consts, *behind)


def _rope_tables(seq):
    half = MLA_ROPE // 2
    inv_freq = ROPE_THETA ** (-jnp.arange(half, dtype=F32) / half)
    ang = jnp.arange(seq, dtype=F32)[:, None] * inv_freq[None, :]
    cos, sin, zero = jnp.cos(ang), jnp.sin(ang), jnp.zeros((seq, half), F32)
    t_c = jnp.concatenate([cos, cos, zero, zero], axis=1)
    t_s1 = jnp.concatenate([-sin, zero, zero, zero], axis=1)
    t_s2 = jnp.concatenate([zero, sin, zero, zero], axis=1)
    return t_c, t_s1, t_s2


def _rope(slab, t_c, t_s1, t_s2):
    return slab * t_c + pltpu.roll(slab, 96, 1) * t_s1 + pltpu.roll(slab, 32, 1) * t_s2


def _rope_t(d, t_c, t_s1, t_s2):
    return d * t_c + pltpu.roll(d * t_s1, 32, 1) + pltpu.roll(d * t_s2, 96, 1)


def _lower_bound(logits):
    l0, l1 = logits[0:1, :], logits[1:2, :]
    mx = jnp.maximum(l0, l1)
    e0, e1 = jnp.exp(l0 - mx), jnp.exp(l1 - mx)
    return e0 / (e0 + e1)


def _tri(n, lower):
    row = lax.broadcasted_iota(jnp.int32, (n, n), 0)
    col = lax.broadcasted_iota(jnp.int32, (n, n), 1)
    return (row >= col) if lower else (row <= col)


def _hgrn_fwd(zq, zf, zi, lb_logits, *, name):
    s, d = zq.shape
    h_n, c, hp, cps = d // LANES, HGRN_CHUNK, HGRN_HEADS_PER_STEP, HGRN_CHUNKS_PER_STEP
    nc = s // c

    def body(zq_ref, zf_ref, zi_ref, lb_ref, o_ref, st_ref, state_sc, b_sc):
        @pl.when(pl.program_id(1) == 0)
        def _():
            state_sc[...] = jnp.zeros_like(state_sc)

        lower = _tri(c, True)
        lower_f = lower.astype(F32)
        hs, pairs = range(hp), [(cc, hh) for cc in range(cps) for hh in range(hp)]
        sls = [slice(hh * LANES, (hh + 1) * LANES) for hh in hs]
        rws = [slice(cc * c, (cc + 1) * c) for cc in range(cps)]
        lb = [_lower_bound(lb_ref[:, sl]) for sl in sls]
        zq_v = {p: zq_ref[rws[p[0]], sls[p[1]]] for p in pairs}
        q = {p: zq_v[p] * _sigmoid(zq_v[p]) for p in pairs}
        f = {p: lb[p[1]] + (1.0 - lb[p[1]]) * _sigmoid(zf_ref[rws[p[0]], sls[p[1]]]) for p in pairs}
        k = {p: 1.0 - f[p] for p in pairs}
        v = {p: zi_ref[rws[p[0]], sls[p[1]]] for p in pairs}
        b = {p: _dot_f32(lower_f, jnp.log(f[p])) for p in pairs}
        for p in pairs:
            b_sc[p[0], p[1]] = b[p]
        qe = {p: q[p] * jnp.exp(b[p]) for p in pairs}
        scores = {p: [] for p in pairs}
        for i in range(c // HGRN_SUB):
            lo = i * HGRN_SUB
            for p in pairs:
                ref = b_sc[p[0], p[1], lo - 1:lo, :] if i > 0 else jnp.zeros((1, LANES), F32)
                qt = q[p][lo:lo + HGRN_SUB, :] * jnp.exp(b[p][lo:lo + HGRN_SUB, :] - ref)
                dec = jnp.exp(jnp.minimum(ref - b[p], EXP_CLAMP))
                scores[p].append(_dot(qt, k[p] * dec, _NT))
        o_intra = {p: _dot(jnp.where(lower, jnp.concatenate(scores[p], axis=0), 0.0), v[p], _NN) for p in pairs}
        bl = {p: b_sc[p[0], p[1], c - 1:c, :] for p in pairs}
        k_end = {p: k[p] * jnp.exp(bl[p] - b[p]) for p in pairs}
        state = [state_sc[hh] for hh in hs]
        for cc in range(cps):
            for hh in hs:
                st_ref[hh, cc] = state[hh]
                o_ref[rws[cc], sls[hh]] = _dot(qe[cc, hh], state[hh], _NT) + o_intra[cc, hh]
            state = [state[hh] * jnp.exp(bl[cc, hh]) + _dot(v[cc, hh], k_end[cc, hh], _TN) for hh in hs]
        for hh in hs:
            state_sc[hh] = state[hh]

    tile = pl.BlockSpec((cps * c, hp * LANES), lambda h, i: (i, h))
    return pl.pallas_call(
        body, name=name, grid=(h_n // hp, nc // cps),
        in_specs=[tile, tile, tile, pl.BlockSpec((2, hp * LANES), lambda h, i: (0, h))],
        out_specs=[tile, pl.BlockSpec((hp, cps, LANES, LANES), lambda h, i: (h, i, 0, 0))],
        out_shape=[jax.ShapeDtypeStruct((s, d), F32), jax.ShapeDtypeStruct((h_n, nc, LANES, LANES), F32)],
        scratch_shapes=[pltpu.VMEM((hp, LANES, LANES), F32), pltpu.VMEM((cps, hp, c, LANES), F32)],
        compiler_params=_params("parallel", "arbitrary"),
    )(zq, zf, zi, lb_logits)


def _hgrn_bwd(zq, zf, zi, lb_logits, states, do, after, *, name):
    s, d = zq.shape
    h_n, c, hp, cps = d // LANES, HGRN_CHUNK, HGRN_HEADS_PER_STEP, HGRN_CHUNKS_PER_STEP
    nc = s // c
    n_steps = nc // cps

    def body(zq_ref, zf_ref, zi_ref, lb_ref, st_ref, do_ref, _, dzq_ref, dzf_ref, dzi_ref, dlb_ref, dstate_sc, b_sc):
        @pl.when(pl.program_id(1) == 0)
        def _():
            dstate_sc[...] = jnp.zeros_like(dstate_sc)
            dlb_ref[...] = jnp.zeros_like(dlb_ref)

        lower, upper = _tri(c, True), _tri(c, False).astype(F32)
        lower_f = lower.astype(F32)
        last_row = lax.broadcasted_iota(jnp.int32, (c, LANES), 0) == c - 1
        hs, pairs = range(hp), [(cc, hh) for cc in range(cps) for hh in range(hp)]
        sls = [slice(hh * LANES, (hh + 1) * LANES) for hh in hs]
        rws = [slice(cc * c, (cc + 1) * c) for cc in range(cps)]
        lb = [_lower_bound(lb_ref[:, sl]) for sl in sls]
        zq_v = {p: zq_ref[rws[p[0]], sls[p[1]]] for p in pairs}
        sq = {p: _sigmoid(zq_v[p]) for p in pairs}
        q = {p: zq_v[p] * sq[p] for p in pairs}
        sf = {p: _sigmoid(zf_ref[rws[p[0]], sls[p[1]]]) for p in pairs}
        f = {p: lb[p[1]] + (1.0 - lb[p[1]]) * sf[p] for p in pairs}
        k = {p: 1.0 - f[p] for p in pairs}
        v = {p: zi_ref[rws[p[0]], sls[p[1]]] for p in pairs}
        d_o = {p: do_ref[rws[p[0]], sls[p[1]]] for p in pairs}
        b = {p: _dot_f32(lower_f, jnp.log(f[p])) for p in pairs}
        s0t = {p: st_ref[p[1], p[0]] for p in pairs}
        for p in pairs:
            b_sc[p[0], p[1]] = b[p]
        bl = {p: b_sc[p[0], p[1], c - 1:c, :] for p in pairs}
        eb = {p: jnp.exp(b[p]) for p in pairs}
        ebl = {p: jnp.exp(bl[p]) for p in pairs}
        dec_end = {p: jnp.exp(bl[p] - b[p]) for p in pairs}
        da = {p: jnp.where(lower, _dot(d_o[p], v[p], _NT), 0.0) for p in pairs}
        dq = {p: _dot(d_o[p], s0t[p], _NN) * eb[p] for p in pairs}
        dstate_in = {p: _dot(d_o[p], q[p] * eb[p], _TN) for p in pairs}
        dk_intra = {p: jnp.zeros((c, LANES), F32) for p in pairs}
        scores, dq_blocks = {p: [] for p in pairs}, {p: [] for p in pairs}
        for i in range(c // HGRN_SUB):
            lo = i * HGRN_SUB
            for p in pairs:
                ref = b_sc[p[0], p[1], lo - 1:lo, :] if i > 0 else jnp.zeros((1, LANES), F32)
                grow = jnp.exp(b[p][lo:lo + HGRN_SUB, :] - ref)
                qt = q[p][lo:lo + HGRN_SUB, :] * grow
                dec = jnp.exp(jnp.minimum(ref - b[p], EXP_CLAMP))
                kd = k[p] * dec
                scores[p].append(_dot(qt, kd, _NT))
                da_i = da[p][lo:lo + HGRN_SUB, :]
                dq_blocks[p].append(_dot_f32(da_i, kd, _NN) * grow)
                dk_intra[p] = dk_intra[p] + _dot_f32(da_i, qt, _TN) * dec
        dv_intra = {p: _dot(jnp.where(lower, jnp.concatenate(scores[p], axis=0), 0.0), d_o[p], _TN) for p in pairs}
        dq = {p: dq[p] + jnp.concatenate(dq_blocks[p], axis=0) for p in pairs}
        q_dq = {p: q[p] * dq[p] for p in pairs}
        for p in pairs:
            dzq_ref[rws[p[0]], sls[p[1]]] = (dq[p] * sq[p] * (1.0 + zq_v[p] * (1.0 - sq[p]))).astype(BF16)
        dstate = [dstate_sc[hh] for hh in hs]
        for cc in reversed(range(cps)):
            ps = [(cc, hh) for hh in hs]
            dk_state = [_dot(v[p], dstate[p[1]], _NN) * dec_end[p] for p in ps]
            dv = [dv_intra[p] + _dot(k[p] * dec_end[p], dstate[p[1]], _NT) for p in ps]
            dk = [dk_intra[p] + dk_state[p[1]] for p in ps]
            db_last = [jnp.sum(k[p] * dk_state[p[1]], axis=0, keepdims=True)
                       + ebl[p] * jnp.sum(s0t[p] * dstate[p[1]], axis=0, keepdims=True) for p in ps]
            db = [q_dq[p] - k[p] * dk[p[1]] + jnp.where(last_row, db_last[p[1]], 0.0) for p in ps]
            df = [_dot_f32(upper, db[p[1]]) / f[p] - dk[p[1]] for p in ps]
            for p in ps:
                hh = p[1]
                dzf_ref[rws[cc], sls[hh]] = (df[hh] * (1.0 - lb[hh]) * sf[p] * (1.0 - sf[p])).astype(BF16)
                dlb_ref[:, sls[hh]] += jnp.sum(df[hh] * (1.0 - sf[p]), axis=0, keepdims=True)
                dzi_ref[rws[cc], sls[hh]] = dv[hh].astype(BF16)
            dstate = [dstate[p[1]] * ebl[p] + dstate_in[p] for p in ps]
        for hh in hs:
            dstate_sc[hh] = dstate[hh]

    tile = pl.BlockSpec((cps * c, hp * LANES), lambda h, i: (n_steps - 1 - i, h))
    out = jax.ShapeDtypeStruct((s, d), BF16)
    return pl.pallas_call(
        body, name=name, grid=(h_n // hp, n_steps),
        in_specs=[tile, tile, tile, pl.BlockSpec((2, hp * LANES), lambda h, i: (0, h)),
                  pl.BlockSpec((hp, cps, LANES, LANES), lambda h, i: (h, n_steps - 1 - i, 0, 0)), tile,
                  pl.BlockSpec(memory_space=pl.ANY)],
        out_specs=[tile, tile, tile, pl.BlockSpec((1, hp * LANES), lambda h, i: (0, h))],
        out_shape=[out, out, out, jax.ShapeDtypeStruct((1, d), F32)],
        scratch_shapes=[pltpu.VMEM((hp, LANES, LANES), F32), pltpu.VMEM((cps, hp, c, LANES), F32)],
        compiler_params=_params("parallel", "arbitrary"),
    )(zq, zf, zi, lb_logits, states, do, after)


ATTN_SUB_ROWS = 256
LOG2E = 1.4426950408889634
LN2 = 0.6931471805599453
Q_PRESCALE = ATTN_SCALE * LOG2E


def _attn_tile(s):
    return min(1024, max(128, s // 2))


def _causal_pairs(n, q_major):
    pairs = [(i, j) for i in range(n) for j in range(i + 1)] if q_major else [(i, j) for j in range(n) for i in range(j, n)]
    return jnp.asarray([p[0] for p in pairs], jnp.int32), jnp.asarray([p[1] for p in pairs], jnp.int32)


def _sub_scores(qn_ref, qr_ref, k, r, sub, t, diagonal):
    q = jnp.concatenate([qn_ref[r:r + sub, :], qr_ref[r:r + sub, :]], axis=1)
    if not diagonal:
        return q, _dot(q, k, _NT)
    cols = r + sub
    keep = lax.broadcasted_iota(jnp.int32, (sub, cols), 1) <= r + lax.broadcasted_iota(jnp.int32, (sub, cols), 0)
    return q, jnp.where(keep, _dot(q, k[:cols], _NT), -jnp.inf)


def _attn_fwd(qn, qr, kn, kr, v, *, name):
    s, t = qn.shape[0], _attn_tile(qn.shape[0])
    sub = min(t, ATTN_SUB_ROWS)
    q_blk, k_blk = _causal_pairs(s // t, True)

    def body(qi_ref, kj_ref, qn_ref, qr_ref, kn_ref, kr_ref, v_ref, o_ref, lse_ref, m_sc, l_sc, acc_sc):
        p_id = pl.program_id(1)
        i, j = qi_ref[p_id], kj_ref[p_id]

        @pl.when(j == 0)
        def _():
            m_sc[...] = jnp.full_like(m_sc, -jnp.inf)
            l_sc[...] = jnp.zeros_like(l_sc)
            acc_sc[...] = jnp.zeros_like(acc_sc)

        def update(diagonal):
            k = jnp.concatenate([kn_ref[...], kr_ref[...]], axis=1)
            v = v_ref[...]
            starts = list(range(0, t, sub))
            scs = [_sub_scores(qn_ref, qr_ref, k, r, sub, t, diagonal)[1] for r in starts]
            ps, alphas = [], []
            for r, sc in zip(starts, scs):
                m_prev = m_sc[r:r + sub, :]
                m_new = jnp.maximum(m_prev, jnp.max(sc, axis=1, keepdims=True))
                alpha = jnp.exp2(m_prev - m_new)
                p = jnp.exp2(sc - m_new[:, :1])
                l_sc[r:r + sub, :] = alpha * l_sc[r:r + sub, :] + jnp.sum(p, axis=1, keepdims=True)
                m_sc[r:r + sub, :] = m_new
                ps.append(p)
                alphas.append(alpha)
            for r, p, alpha in zip(starts, ps, alphas):
                acc_sc[r:r + sub, :] = alpha * acc_sc[r:r + sub, :] + _dot(p, v[:p.shape[1]], _NN)

        @pl.when(j < i)
        def _():
            update(False)

        @pl.when(j == i)
        def _():
            update(True)
            o_ref[...] = (acc_sc[...] / l_sc[...]).astype(BF16)
            lse_ref[...] = m_sc[...] + jnp.log(l_sc[...]) * LOG2E

    q_spec = pl.BlockSpec((t, LANES), lambda h, p, qi, kj: (qi[p], h))
    k_spec = pl.BlockSpec((t, LANES), lambda h, p, qi, kj: (kj[p], h))
    kr_spec = pl.BlockSpec((t, LANES), lambda h, p, qi, kj: (kj[p], 0))
    stat = pltpu.VMEM((t, LANES), F32)
    return pl.pallas_call(
        body, name=name,
        grid_spec=pltpu.PrefetchScalarGridSpec(
            num_scalar_prefetch=2, grid=(MLA_HEADS, q_blk.shape[0]),
            in_specs=[q_spec, q_spec, k_spec, kr_spec, k_spec], out_specs=[q_spec, q_spec],
            scratch_shapes=[stat, stat, stat]),
        out_shape=[jax.ShapeDtypeStruct(qn.shape, BF16), jax.ShapeDtypeStruct(qn.shape, F32)],
        compiler_params=_params("parallel", "arbitrary"),
    )(q_blk, k_blk, qn, qr, kn, kr, v)


def _attn_bwd(qn, qr, kn, kr, v, do, lse, delta, *, name):
    s, t = qn.shape[0], _attn_tile(qn.shape[0])
    n, sub = s // t, min(t, ATTN_SUB_ROWS)
    q_blk, k_blk = _causal_pairs(n, False)

    def body(qi_ref, kj_ref, qn_ref, qr_ref, kn_ref, kr_ref, v_ref, do_ref, lse_ref, delta_ref,
             dqn_ref, dqr_ref, dkn_ref, dv_ref, dkr_ref, dk_sc, dv_sc):
        p_id = pl.program_id(1)
        i, j = qi_ref[p_id], kj_ref[p_id]

        @pl.when(p_id == 0)
        def _():
            dqn_ref[...] = jnp.zeros_like(dqn_ref)
            dqr_ref[...] = jnp.zeros_like(dqr_ref)

        @pl.when(i == j)
        def _():
            dk_sc[...] = jnp.zeros_like(dk_sc)
            dv_sc[...] = jnp.zeros_like(dv_sc)

        def accumulate(diagonal):
            k = jnp.concatenate([kn_ref[...], kr_ref[...]], axis=1)
            v = v_ref[...]
            starts = list(range(0, t, sub))
            qs, d_os, scs, dps = [], [], [], []
            for r in starts:
                q, sc = _sub_scores(qn_ref, qr_ref, k, r, sub, t, diagonal)
                d_o = do_ref[r:r + sub, :]
                qs.append(q)
                d_os.append(d_o)
                scs.append(sc)
                dps.append(_dot(d_o, v[:sc.shape[1]], _NT))
            ps, dss = [], []
            for r, sc, dp in zip(starts, scs, dps):
                p = jnp.exp2(sc - lse_ref[r:r + sub, :][:, :1])
                ps.append(p.astype(BF16))
                dss.append((p * (dp - delta_ref[r:r + sub, :][:, :1])).astype(BF16))
            for r, q, d_o, p, ds in zip(starts, qs, d_os, ps, dss):
                cols = p.shape[1]
                dv_sc[:cols, :] += _dot(p, d_o, _TN)
                dk_sc[:cols, :] += _dot(ds, q, _TN)
                dq = _dot(ds, k[:cols], _NN) * ATTN_SCALE
                rows = pl.ds(pl.multiple_of(i * t + r, sub), sub)
                dqn_ref[rows, :] += dq[:, :LANES]
                dqr_ref[rows, :] += dq[:, LANES:]

        @pl.when(j < i)
        def _():
            accumulate(False)

        @pl.when(j == i)
        def _():
            accumulate(True)

        @pl.when(i == n - 1)
        def _():
            dkn_ref[...] = (dk_sc[:, :LANES] * LN2).astype(BF16)
            dkr_ref[...] = dk_sc[:, LANES:] * LN2
            dv_ref[...] = dv_sc[...].astype(BF16)

    q_spec = pl.BlockSpec((t, LANES), lambda h, p, qi, kj: (qi[p], h))
    k_spec = pl.BlockSpec((t, LANES), lambda h, p, qi, kj: (kj[p], h))
    kr_spec = pl.BlockSpec((t, LANES), lambda h, p, qi, kj: (kj[p], 0))
    head_spec = pl.BlockSpec((s, LANES), lambda h, p, qi, kj: (0, h))
    f32_out, bf16_out = jax.ShapeDtypeStruct(qn.shape, F32), jax.ShapeDtypeStruct(qn.shape, BF16)
    return pl.pallas_call(
        body, name=name,
        grid_spec=pltpu.PrefetchScalarGridSpec(
            num_scalar_prefetch=2, grid=(MLA_HEADS, q_blk.shape[0]),
            in_specs=[q_spec, q_spec, k_spec, kr_spec, k_spec, q_spec, q_spec, q_spec],
            out_specs=[head_spec, head_spec, k_spec, k_spec, k_spec],
            scratch_shapes=[pltpu.VMEM((t, 2 * LANES), F32), pltpu.VMEM((t, LANES), F32)]),
        out_shape=[f32_out, f32_out, bf16_out, bf16_out, f32_out],
        compiler_params=_params("parallel", "arbitrary"),
    )(q_blk, k_blk, qn, qr, kn, kr, v, do, lse, delta)


def _exchange(arrs, *, scatter, name):
    n = len(arrs)
    out_shape = [jax.ShapeDtypeStruct(a.shape if scatter else (N_DEV, *a.shape), a.dtype) for a in arrs]

    def body(*refs):
        ins, outs = refs[:n], refs[n:2 * n]
        send_sems, recv_sems, local_sems = refs[2 * n:]
        x, y, c = lax.axis_index("x"), lax.axis_index("y"), lax.axis_index("c")
        me = 4 * x + 2 * y + c
        copies = []
        for k in range(n):
            local = pltpu.make_async_copy(ins[k].at[me] if scatter else ins[k], outs[k].at[me], local_sems.at[k])
            local.start()
            copies.append(local)
            for d in range(1, N_DEV):
                px, py, pc = (x + (d >> 2)) % 2, (y + ((d >> 1) & 1)) % 2, (c + (d & 1)) % 2
                peer = 4 * px + 2 * py + pc
                remote = pltpu.make_async_remote_copy(
                    src_ref=ins[k].at[peer] if scatter else ins[k], dst_ref=outs[k].at[me],
                    send_sem=send_sems.at[k, d - 1], recv_sem=recv_sems.at[k, d - 1],
                    device_id=(px, py, pc), device_id_type=pl.DeviceIdType.MESH)
                remote.start()
                copies.append(remote)
        for cp in copies:
            cp.wait()

    any_spec = pl.BlockSpec(memory_space=pl.ANY)
    return pl.pallas_call(
        body, name=name, in_specs=[any_spec] * n, out_specs=[any_spec] * n, out_shape=out_shape,
        scratch_shapes=[pltpu.SemaphoreType.DMA((n, N_DEV - 1)), pltpu.SemaphoreType.DMA((n, N_DEV - 1)),
                        pltpu.SemaphoreType.DMA((n,))],
    )(*arrs)


def _peers(x, y, c):
    out = []
    for d in range(1, N_DEV):
        px, py, pc = (x + (d >> 2)) % 2, (y + ((d >> 1) & 1)) % 2, (c + (d & 1)) % 2
        out.append(((px, py, pc), 4 * px + 2 * py + pc))
    return out


CHIP_LEVEL_PEERS = (1, 2, 4, 6)


def _exchange_copies(ins, lands, send_sems, recv_sems, scatter, chip_level=False):
    x, y, c = lax.axis_index("x"), lax.axis_index("y"), lax.axis_index("c")
    me = 4 * x + 2 * y + c
    local, remote = [], []
    for k in range(len(ins)):
        local.append(pltpu.make_async_copy(ins[k].at[me] if scatter else ins[k], lands[k].at[me],
                                           recv_sems.at[k * N_DEV + N_DEV - 1]))
        for d, (coords, peer) in enumerate(_peers(x, y, c)):
            if chip_level and d + 1 not in CHIP_LEVEL_PEERS:
                continue
            remote.append(pltpu.make_async_remote_copy(
                src_ref=ins[k].at[peer] if scatter else ins[k], dst_ref=lands[k].at[me],
                send_sem=send_sems.at[k * N_DEV + d], recv_sem=recv_sems.at[k * N_DEV + d],
                device_id=coords, device_id_type=pl.DeviceIdType.MESH))
    return local, remote


def _exchange_start(arrs, *, scatter, name, after=None, chip_level=False):
    n = len(arrs)
    hbm = pl.BlockSpec(memory_space=pltpu.HBM)
    sem = pl.BlockSpec(memory_space=pltpu.SEMAPHORE)
    lands = [lax.empty(a.shape if scatter else (N_DEV, *a.shape), a.dtype) for a in arrs]

    def body(*refs):
        ins, land_refs = refs[:n], refs[n:2 * n]
        first_out = 2 * n + (after is not None)
        send_sems, recv_sems, token = refs[first_out], refs[first_out + 1], refs[-1]
        local, remote = _exchange_copies(ins, land_refs, send_sems, recv_sems, scatter, chip_level)
        for cp in local + remote:
            cp.start()
        token[...] = jnp.zeros_like(token)

    operands = [pltpu.with_memory_space_constraint(a, pltpu.HBM) for a in list(arrs) + lands]
    behind = [] if after is None else [after]
    res = pl.pallas_call(
        body, name=name,
        out_shape=(pltpu.SemaphoreType.DMA((n * N_DEV,)), pltpu.SemaphoreType.DMA((n * N_DEV,)),
                   *[pltpu.HBM(o.shape, o.dtype) for o in operands], jax.ShapeDtypeStruct((8, LANES), F32)),
        in_specs=[hbm] * (2 * n) + [pl.BlockSpec(memory_space=pl.ANY)] * len(behind),
        out_specs=(sem, sem, *[hbm] * (2 * n), pl.BlockSpec(memory_space=pltpu.VMEM)),
        input_output_aliases={i: 2 + i for i in range(2 * n)},
        compiler_params=pltpu.CompilerParams(has_side_effects=pltpu.SideEffectType.DATAFLOW_SIDE_EFFECTING),
    )(*operands, *behind)
    return (res[0], res[1], list(res[2:2 + n]), list(res[2 + n:2 + 2 * n]), scatter, chip_level), res[-1]


def _exchange_wait(state, after, *, name):
    send_sems, recv_sems, ins, lands, scatter, chip_level = state
    n = len(ins)
    hbm = pl.BlockSpec(memory_space=pltpu.HBM)
    sem = pl.BlockSpec(memory_space=pltpu.SEMAPHORE)

    def body(*refs):
        in_refs, land_refs = refs[:n], refs[n:2 * n]
        local, remote = _exchange_copies(in_refs, land_refs, refs[2 * n], refs[2 * n + 1], scatter, chip_level)
        for cp in local:
            cp.wait()
        for cp in remote:
            cp.wait_send()
            cp.wait_recv()

    res = pl.pallas_call(
        body, name=name, out_shape=tuple(pltpu.HBM(o.shape, o.dtype) for o in ins + lands),
        in_specs=[hbm] * (2 * n) + [sem, sem, pl.BlockSpec(memory_space=pl.ANY)], out_specs=tuple([hbm] * (2 * n)),
        input_output_aliases={i: i for i in range(2 * n)},
        compiler_params=pltpu.CompilerParams(has_side_effects=pltpu.SideEffectType.DATAFLOW_SIDE_EFFECTING),
    )(*ins, *lands, send_sems, recv_sems, after)
    return list(res[n:])


def _chip_forward(lands, *, name):
    n = len(lands)

    def body(*refs):
        ins, outs, send_sems, recv_sems = refs[:n], refs[n:2 * n], refs[2 * n], refs[2 * n + 1]
        x, y, c = lax.axis_index("x"), lax.axis_index("y"), lax.axis_index("c")
        copies = []
        for k in range(n):
            for j, (dx, dy) in enumerate(((0, 1), (1, 0), (1, 1))):
                held = 4 * ((x + dx) % 2) + 2 * ((y + dy) % 2) + c
                cp = pltpu.make_async_remote_copy(
                    src_ref=ins[k].at[held], dst_ref=outs[k].at[held], send_sem=send_sems.at[k, j],
                    recv_sem=recv_sems.at[k, j], device_id=(x, y, 1 - c), device_id_type=pl.DeviceIdType.MESH)
                cp.start()
                copies.append(cp)
        for cp in copies:
            cp.wait()

    any_spec = pl.BlockSpec(memory_space=pl.ANY)
    return pl.pallas_call(
        body, name=name, in_specs=[any_spec] * n, out_specs=[any_spec] * n,
        out_shape=[jax.ShapeDtypeStruct(a.shape, a.dtype) for a in lands], input_output_aliases={k: k for k in range(n)},
        scratch_shapes=[pltpu.SemaphoreType.DMA((n, 3)), pltpu.SemaphoreType.DMA((n, 3))],
    )(*lands)


def _adam(w, terms, m, v, *, name):
    n_layers, r, c = w.shape
    tr = min(r, 128)
    assert r % tr == 0 and len(terms) == n_layers
    steps = r // tr

    def body(w_ref, *rest):
        t_refs, (m_ref, v_ref, g_out, d_out, m_out, v_out) = rest[:n_layers], rest[n_layers:]
        for layer, t_ref in enumerate(t_refs):
            @pl.when(pl.program_id(0) == layer)
            def _(t_ref=t_ref):
                g = t_ref[0].astype(F32)
                for s in range(1, t_ref.shape[0]):
                    g = g + t_ref[s].astype(F32)
                m1 = ADAM_B1 * m_ref[...] + (1.0 - ADAM_B1) * g
                v1 = ADAM_B2 * v_ref[...] + (1.0 - ADAM_B2) * jnp.square(g)
                m_hat = m1 / (1.0 - ADAM_B1 ** ADAM_STEP)
                v_hat = v1 / (1.0 - ADAM_B2 ** ADAM_STEP)
                g_out[...] = g
                d_out[...] = -ADAM_LR * (m_hat / (jnp.sqrt(v_hat) + ADAM_EPS) + ADAM_WD * w_ref[...])
                m_out[...] = m1
                v_out[...] = v1

    def term_spec(layer, t):
        return pl.BlockSpec((t.shape[0], tr, c),
                            lambda l, i: (0, jnp.where(l == layer, i, jnp.where(l < layer, 0, steps - 1)), 0))

    spec = pl.BlockSpec((None, tr, c), lambda l, i: (l, i, 0))
    out = jax.ShapeDtypeStruct(w.shape, F32)
    return pl.pallas_call(
        body, name=name, grid=(n_layers, steps),
        in_specs=[spec] + [term_spec(layer, t) for layer, t in enumerate(terms)] + [spec, spec], out_specs=[spec] * 4,
        out_shape=[out] * 4, compiler_params=_params("arbitrary", "arbitrary"),
    )(w, *terms, m, v)


def _sum_terms(terms, *, name):
    n, _, p = terms.shape

    def body(t_ref, o_ref):
        acc = t_ref[0]
        for s in range(1, n):
            acc = acc + t_ref[s]
        o_ref[...] = acc

    return pl.pallas_call(body, name=name, out_shape=jax.ShapeDtypeStruct((1, p), F32))(terms)


def _lb_logits_grad(dlb, logits, *, name):
    def body(dlb_ref, l_ref, o_ref):
        lb = _lower_bound(l_ref[...])
        d0 = dlb_ref[...] * lb * (1.0 - lb)
        o_ref[...] = jnp.concatenate([d0, -d0], axis=0)

    return pl.pallas_call(body, name=name, out_shape=jax.ShapeDtypeStruct(logits.shape, F32))(dlb, logits)


def _silu_grad(z):
    sg = _sigmoid(z)
    return sg * (1.0 + z * (1.0 - sg))


def _head_norm_gate(o, zg, gn):
    outs = []
    for h in range(HGRN_HEADS):
        sl = slice(h * LANES, (h + 1) * LANES)
        zg_h = zg[:, sl]
        outs.append(_rms(o[:, sl], gn) * (zg_h * _sigmoid(zg_h)))
    return (jnp.concatenate(outs, axis=1),)


def _head_norm_gate_bwd(o, zg, dm, gn):
    do_parts, dzg_parts, dgn = [], [], jnp.zeros((1, LANES), F32)
    for h in range(HGRN_HEADS):
        sl = slice(h * LANES, (h + 1) * LANES)
        o_h, zg_h, dm_h = o[:, sl], zg[:, sl], dm[:, sl]
        gate = zg_h * _sigmoid(zg_h)
        do_h, dgn_h = _rms_bwd(o_h, gn, dm_h * gate)
        dgn = dgn + dgn_h
        do_parts.append(do_h)
        dzg_parts.append(dm_h * _rms(o_h, gn) * _silu_grad(zg_h))
    return jnp.concatenate(do_parts, axis=1), jnp.concatenate(dzg_parts, axis=1), dgn


def _rope_slabs(x, t_c, t_s1, t_s2, transpose):
    fn = _rope_t if transpose else _rope
    return jnp.concatenate(
        [fn(x[:, h * LANES:(h + 1) * LANES], t_c, t_s1, t_s2) for h in range(x.shape[1] // LANES)], axis=1)


def _loss_head(h, tgt, w):
    d = h.shape[1]
    r = lax.rsqrt(jnp.mean(h * h, axis=-1, keepdims=True) + EPS)
    xh = h * r
    err = xh * w - tgt
    loss = 0.5 * jnp.sum(jnp.mean(err * err, axis=-1, keepdims=True), axis=0, keepdims=True)
    dy = err / d
    dxh = dy * w
    dh = r * (dxh - xh * jnp.mean(dxh * xh, axis=-1, keepdims=True))
    return dh, dh, jnp.sum(dy * xh, axis=0, keepdims=True), jnp.broadcast_to(loss, (1, LANES))


def _mlp_fwd(h, norm, w_up, w_down, tag, loss_head=None):
    d = h.shape[1]

    def up(x, g, wu):
        x_n = _rms(x, g).astype(BF16)
        return x_n, jnp.concatenate([jnp.square(jnp.maximum(_dot(x_n, wu[j], _NN), 0.0)) for j in range(wu.shape[0])],
                                    axis=1)

    xn, act = _rowcall(up, [h], [norm, w_up], [(d, BF16), (w_up.shape[0] * w_up.shape[2], BF16)], [], tr=512,
                       name=f"{tag}_up")
    if callable(w_down):
        w_down = w_down(act)
    if loss_head is None:
        return _mm(act, w_down, mode="nn", add=h, name=f"{tag}_down"), (h, xn, act)
    tgt, final_norm = loss_head

    def down_and_loss(a, res, t, wd, g):
        return _loss_head(res + _dot(a, wd, _NN), t, g)

    return _rowcall(down_and_loss, [act, h, tgt], [w_down, final_norm], [(d, F32), (d, BF16)], [d, LANES],
                    name=f"{tag}_down_loss"), (h, xn, act)


def _mlp_bwd(dh_out, dh_out_bf, saved, norm, w_up, w_down, tag, after=None):
    h, xn, act = saved
    d = h.shape[1]
    du = _mm(dh_out_bf, w_down, mode="nt", relu2_of=act, out_dtype=BF16, after=after, name=f"{tag}_bwd_du")
    dw_down = _mm(act, dh_out_bf, mode="tn", name=f"{tag}_bwd_wdown")
    dw_up = _mm(xn, du, mode="tn", col_shards=w_up.shape[0], name=f"{tag}_bwd_wup")

    def up_norm_bwd(x, d_u, dres, g, wu):
        cols = wu.shape[2]
        dxn = _dot(d_u[:, :cols], wu[0], _NT)
        for j in range(1, wu.shape[0]):
            dxn = dxn + _dot(d_u[:, j * cols:(j + 1) * cols], wu[j], _NT)
        dx, dw = _rms_bwd(x, g, dxn)
        return dx + dres, dx + dres, dw

    dh, dh_bf, dnorm = _rowcall(up_norm_bwd, [h, du, dh_out], [norm, w_up], [(d, F32), (d, BF16)], [d], tr=512,
                                name=f"{tag}_bwd_dxn")
    return dh, dh_bf, dnorm, dw_up, dw_down


def _row_major(g):
    return g.reshape(g.shape[0] * g.shape[1], g.shape[2])


def _col_major(g):
    return jnp.transpose(g, (1, 0, 2)).reshape(g.shape[1], g.shape[0] * g.shape[2])


def _col_terms(dw):
    k, n = dw.shape
    return jnp.transpose(dw.reshape(k, N_DEV, n // N_DEV), (1, 0, 2))


def _row_terms(dw):
    return dw.reshape(N_DEV, dw.shape[0] // N_DEV, dw.shape[1])


def kernel(x, hgrn_norm, hgrn_w_q, hgrn_w_f, hgrn_w_i, hgrn_w_g, hgrn_g_norm, hgrn_w_o, hgrn_lb_logits, mla_norm, mla_w_dq, mla_q_norm, mla_w_uq, mla_w_o, kv_in_norm, kv_w_dkv, kv_norm, kv_w_uk, kv_w_uv, mlp_norm, mlp_w_up, mlp_w_down, final_norm, loss_target, m_hgrn_norm, m_hgrn_w_q, m_hgrn_w_f, m_hgrn_w_i, m_hgrn_w_g, m_hgrn_g_norm, m_hgrn_w_o, m_hgrn_lb_logits, m_mla_norm, m_mla_w_dq, m_mla_q_norm, m_mla_w_uq, m_mla_w_o, m_kv_in_norm, m_kv_w_dkv, m_kv_norm, m_kv_w_uk, m_kv_w_uv, m_mlp_norm, m_mlp_w_up, m_mlp_w_down, m_final_norm, v_hgrn_norm, v_hgrn_w_q, v_hgrn_w_f, v_hgrn_w_i, v_hgrn_w_g, v_hgrn_g_norm, v_hgrn_w_o, v_hgrn_lb_logits, v_mla_norm, v_mla_w_dq, v_mla_q_norm, v_mla_w_uq, v_mla_w_o, v_kv_in_norm, v_kv_w_dkv, v_kv_norm, v_kv_w_uk, v_kv_w_uv, v_mlp_norm, v_mlp_w_up, v_mlp_w_down, v_final_norm):
    given = dict(locals())
    weight_names = ["hgrn_norm", "hgrn_w_q", "hgrn_w_f", "hgrn_w_i", "hgrn_w_g", "hgrn_g_norm", "hgrn_w_o",
                    "hgrn_lb_logits", "mla_norm", "mla_w_dq", "mla_q_norm", "mla_w_uq", "mla_w_o", "kv_in_norm",
                    "kv_w_dkv", "kv_norm", "kv_w_uk", "kv_w_uv", "mlp_norm", "mlp_w_up", "mlp_w_down", "final_norm"]
    me = 4 * lax.axis_index("x") + 2 * lax.axis_index("y") + lax.axis_index("c")
    xs, tgt = x[0], loss_target[0]
    seq, d_model = xs.shape
    n_heads, hd = MLA_HEADS, LANES

    big_local = {
        "hgrn_w_q": hgrn_w_q[0], "hgrn_w_f": hgrn_w_f[0], "hgrn_w_i": hgrn_w_i[0], "hgrn_w_g": hgrn_w_g[0],
        "hgrn_w_o": hgrn_w_o[0], "mla_w_dq": mla_w_dq[0], "mla_w_uq": mla_w_uq[0], "mla_w_o": mla_w_o[0],
        "kv_w_dkv": kv_w_dkv, "kv_w_uk": kv_w_uk, "kv_w_uv": kv_w_uv,
        "mlp_w_up0": mlp_w_up[0], "mlp_w_up1": mlp_w_up[1], "mlp_w_down0": mlp_w_down[0], "mlp_w_down1": mlp_w_down[1],
    }
    big_names = list(big_local)
    col_sharded = {"mla_w_uq", "kv_w_uk", "kv_w_uv"}
    shard_major = {"mlp_w_up0", "mlp_w_up1"}
    vec_local = jnp.concatenate([hgrn_norm, hgrn_lb_logits], axis=0)
    first_names = ["hgrn_w_q", "hgrn_w_f", "hgrn_w_i"]
    proj_names = first_names + ["hgrn_w_g"]
    later_names = {"hgrn_o": ["hgrn_w_g", "hgrn_w_o"], "up0": ["mlp_w_up0"], "down0": ["mlp_w_down0"],
                   "mla": ["kv_w_dkv", "kv_w_uk", "kv_w_uv", "mla_w_dq", "mla_w_uq", "mla_w_o"],
                   "mlp1": ["mlp_w_up1", "mlp_w_down1"]}

    def unshard(names, arrays):
        return {k: (a if k in shard_major else _col_major(a) if k in col_sharded else _row_major(a))
                for k, a in zip(names, arrays)}

    two_level = {"down0", "mla"}
    first_state, token = _exchange_start([big_local[k].astype(BF16) for k in first_names] + [vec_local], scatter=False,
                                         chip_level=True, name="gather_first_start")
    gather_state = {}
    for tag, names in later_names.items():
        gather_state[tag], token = _exchange_start([big_local[k].astype(BF16) for k in names], scatter=False,
                                                   chip_level=tag in two_level, after=token, name=f"gather_{tag}_start")

    def gather_wait(tag, after):
        landed = _exchange_wait(gather_state[tag], after, name=f"gather_{tag}_wait")
        if tag in two_level:
            landed = _chip_forward(landed, name=f"gather_{tag}_forward")
        w.update(unshard(later_names[tag], landed))
        return [w[k] for k in later_names[tag]]

    gathered = _chip_forward(_exchange_wait(first_state, token, name="gather_first_wait"), name="gather_first_forward")
    w = unshard(first_names, gathered[:-1])
    vec_full = jnp.transpose(gathered[-1], (1, 0, 2)).reshape(3, d_model)
    hgrn_norm_full, lb_logits_full = vec_full[0:1], vec_full[1:3]
    t_c, t_s1, t_s2 = _rope_tables(seq)
    kv_lora = kv_w_uk.shape[0]

    def hgrn_proj(a, g, *weights):
        xn = _rms(a, g).astype(BF16)
        return (xn, *[_dot(xn, wt, _NN) for wt in weights])

    xn0, zq, zf, zi = _rowcall(hgrn_proj, [xs], [hgrn_norm_full] + [w[k] for k in first_names],
                               [(d_model, BF16)] + [(d_model, F32)] * 3, [], tr=512, name="hgrn_proj")
    o_rec, states = _hgrn_fwd(zq, zf, zi, lb_logits_full, name="hgrn_fwd")
    gather_wait("hgrn_o", o_rec)

    def gate_out(o, x_n, res, gn, wg, wo):
        z = _dot(x_n, wg, _NN)
        m = _head_norm_gate(o, z, gn)[0].astype(BF16)
        return z, m, res + _dot(m, wo, _NN)

    zg, mixed, h1 = _rowcall(gate_out, [o_rec, xn0, xs], [hgrn_g_norm, w["hgrn_w_g"], w["hgrn_w_o"]],
                             [(d_model, F32), (d_model, BF16), (d_model, F32)], [], name="hgrn_gate_out")
    h2, mlp0_saved = _mlp_fwd(h1, mlp_norm[0:1], gather_wait("up0", h1)[0], lambda act: gather_wait("down0", act)[0],
                              "mlp0")
    gather_wait("mla", h2)
    w_uq3 = w["mla_w_uq"].reshape(-1, n_heads, MLA_NOPE + MLA_ROPE)
    w_uq_nope = w_uq3[:, :, :MLA_NOPE].reshape(-1, n_heads * hd)
    w_uq_rope = jnp.pad(w_uq3[:, :, MLA_NOPE:], ((0, 0), (0, 0), (0, hd - MLA_ROPE))).reshape(-1, n_heads * hd)
    w_dkv_pad = jnp.pad(w["kv_w_dkv"], ((0, 0), (0, kv_lora + hd - w["kv_w_dkv"].shape[1])))

    q_lora, qk_cols = w["mla_w_dq"].shape[1], n_heads * hd

    def mla_qkv(a, tc, ts1, ts2, g_kv_in, g_mla, g_q, g_kv, wdq, wn, wr, wdkv, wuk, wuv):
        h_n, x_n = _rms(a, g_kv_in).astype(BF16), _rms(a, g_mla).astype(BF16)
        cq = _dot(x_n, wdq, _NN)
        cq_n = _rms(cq, g_q).astype(BF16)
        q_nope = _dot(cq_n, wn, _NN) * Q_PRESCALE
        q_rope = _rope_slabs(_dot(cq_n, wr, _NN) * Q_PRESCALE, tc, ts1, ts2, False)
        c_all = _dot(h_n, wdkv, _NN)
        lat = _rms(c_all[:, :kv_lora], g_kv).astype(BF16)
        return (h_n, x_n, cq, cq_n, q_nope, q_rope, c_all, lat, _rope(c_all[:, kv_lora:], tc, ts1, ts2),
                _dot(lat, wuk, _NN), _dot(lat, wuv, _NN))

    hn, xn2, cq_pre, c_q, qn, qr, ckr, c_kv, kr, kn, vv = _rowcall(
        mla_qkv, [h2, t_c, t_s1, t_s2],
        [kv_in_norm[None, :], mla_norm, mla_q_norm, kv_norm[None, :], w["mla_w_dq"], w_uq_nope, w_uq_rope, w_dkv_pad,
         w["kv_w_uk"], w["kv_w_uv"]],
        [(d_model, BF16), (d_model, BF16), (q_lora, F32), (q_lora, BF16), (qk_cols, BF16), (qk_cols, BF16),
         (kv_lora + hd, F32), (kv_lora, BF16), (hd, BF16), (qk_cols, BF16), (qk_cols, BF16)], [], tr=512, name="mla_qkv")
    o_att, lse = _attn_fwd(qn, qr, kn, kr, vv, name="attn_fwd")
    h3 = _mm(o_att, w["mla_w_o"], mode="nn", add=h2, name="attn_out")
    gather_wait("mlp1", h3)
    (dh4, dh4_bf, g_final_norm, loss_part), mlp1_saved = _mlp_fwd(
        h3, mlp_norm[1:2], w["mlp_w_up1"], w["mlp_w_down1"], "mlp1", loss_head=(tgt, final_norm[None, :]))

    g = {}
    groups = {"mlp1": ["mlp_w_up1", "mlp_w_down1"],
              "mla": ["mla_w_o", "mla_w_uq", "mla_w_dq", "kv_w_uk", "kv_w_uv", "kv_w_dkv"],
              "mlp0": ["mlp_w_up0", "mlp_w_down0"],
              "hgrn_out": ["hgrn_w_o", "hgrn_w_g"],
              "hgrn_in": ["hgrn_w_q", "hgrn_w_f", "hgrn_w_i"]}
    scatter_state = {}

    def scatter_start(tag, after=None):
        scatter_state[tag], tok = _exchange_start(
            [g[k] if k in shard_major else (_col_terms if k in col_sharded else _row_terms)(g[k]) for k in groups[tag]],
            scatter=True, after=after,
            name=f"scatter_{tag}_start")
        return tok

    dh3, dh3_bf, g_mlp_norm1, g["mlp_w_up1"], g["mlp_w_down1"] = _mlp_bwd(
        dh4, dh4_bf, mlp1_saved, mlp_norm[1:2], w["mlp_w_up1"], w["mlp_w_down1"], "mlp1")
    def attn_out_bwd(dres, o, wo):
        d_o = _dot(dres, wo, _NT).astype(BF16)
        prod = d_o.astype(F32) * o.astype(F32)
        return d_o, jnp.concatenate([jnp.broadcast_to(jnp.sum(prod[:, h * hd:(h + 1) * hd], axis=1, keepdims=True),
                                                      (prod.shape[0], hd)) for h in range(n_heads)], axis=1)

    d_oatt, delta = _rowcall(attn_out_bwd, [dh3_bf, o_att], [w["mla_w_o"]], [(qk_cols, BF16), (qk_cols, F32)], [],
                             after=scatter_start("mlp1"), name="attn_out_bwd_x")
    g["mla_w_o"] = _mm(o_att, dh3_bf, mode="tn", name="attn_out_bwd_w")
    dqn, dqr, dkn, dvv, dkr = _attn_bwd(qn, qr, kn, kr, vv, d_oatt, lse, delta, name="attn_bwd")

    def q_path_bwd(cq, cq_n, x_n, d_qn, d_qr, tc, ts1, ts2, g_q, wdq, wn, wr):
        d_qn, d_qr = d_qn.astype(BF16), _rope_slabs(d_qr, tc, ts1, ts2, True).astype(BF16)
        d_cq, d_gq = _rms_bwd(cq, g_q, _dot(d_qn, wn, _NT) + _dot(d_qr, wr, _NT))
        d_cq = d_cq.astype(BF16)
        return _dot(d_cq, wdq, _NT), d_gq, _dot(x_n, d_cq, _TN), _dot(cq_n, d_qn, _TN), _dot(cq_n, d_qr, _TN)

    dxn2, g_q_norm, g_dq, g_uq_nope, g_uq_rope = _rowcall(
        q_path_bwd, [cq_pre, c_q, xn2, dqn, dqr, t_c, t_s1, t_s2], [mla_q_norm, w["mla_w_dq"], w_uq_nope, w_uq_rope],
        [(d_model, F32)], [q_lora, (d_model, q_lora), (q_lora, qk_cols), (q_lora, qk_cols)], tr=512, name="mla_q_bwd")
    g["mla_w_dq"] = g_dq.astype(GRAD_WIRE_DTYPE)
    g["mla_w_uq"] = jnp.concatenate([g_uq_nope.reshape(q_lora, n_heads, hd),
                                     g_uq_rope.reshape(q_lora, n_heads, hd)[:, :, :MLA_ROPE]],
                                    axis=2).reshape(q_lora, -1).astype(GRAD_WIRE_DTYPE)

    def kv_path_bwd(c_all, lat, h_n, d_kn, d_v, d_kr_heads, tc, ts1, ts2, a, d_xn2, dres,
                    g_kv, g_kv_in, g_mla, wdkv, wuk, wuv):
        d_lat, d_gkv = _rms_bwd(c_all[:, :kv_lora], g_kv, _dot(d_kn, wuk, _NT) + _dot(d_v, wuv, _NT))
        d_kr = d_kr_heads[:, :hd]
        for h in range(1, n_heads):
            d_kr = d_kr + d_kr_heads[:, h * hd:(h + 1) * hd]
        d_all = jnp.concatenate([d_lat, _rope_t(d_kr, tc, ts1, ts2)], axis=1).astype(BF16)
        dx1, d_gkv_in = _rms_bwd(a, g_kv_in, _dot(d_all, wdkv, _NT))
        dx2, d_gmla = _rms_bwd(a, g_mla, d_xn2)
        d_a = dx1 + dx2 + dres
        return (d_a, d_a, d_gkv, d_gkv_in, d_gmla, _dot(h_n, d_all, _TN), _dot(lat, d_kn, _TN), _dot(lat, d_v, _TN))

    dh2, dh2_bf, g_kv_norm, g_kv_in_norm, g_mla_norm, g_dkv, g_uk, g_uv = _rowcall(
        kv_path_bwd, [ckr, c_kv, hn, dkn, dvv, dkr, t_c, t_s1, t_s2, h2, dxn2, dh3],
        [kv_norm[None, :], kv_in_norm[None, :], mla_norm, w_dkv_pad, w["kv_w_uk"], w["kv_w_uv"]],
        [(d_model, F32), (d_model, BF16)],
        [kv_lora, d_model, d_model, (d_model, kv_lora + hd), (kv_lora, qk_cols), (kv_lora, qk_cols)], name="mla_kv_bwd")
    g["kv_w_dkv"] = g_dkv[:, :kv_w_dkv.shape[1]].astype(GRAD_WIRE_DTYPE)
    g["kv_w_uk"], g["kv_w_uv"] = g_uk.astype(GRAD_WIRE_DTYPE), g_uv.astype(GRAD_WIRE_DTYPE)
    dh1, dh1_bf, g_mlp_norm0, g["mlp_w_up0"], g["mlp_w_down0"] = _mlp_bwd(
        dh2, dh2_bf, mlp0_saved, mlp_norm[0:1], w["mlp_w_up0"], w["mlp_w_down0"], "mlp0", after=scatter_start("mla"))

    g["hgrn_w_o"] = _mm(mixed, dh1_bf, mode="tn", after=scatter_start("mlp0"), name="hgrn_out_bwd_w")
    do_rec, dzg, g_g_norm = _rowcall(
        lambda dres, o, z, wo, gn: _head_norm_gate_bwd(o, z, _dot(dres, wo, _NT), gn), [dh1_bf, o_rec, zg],
        [w["hgrn_w_o"], hgrn_g_norm], [(d_model, F32), (d_model, BF16)], [hd], name="hgrn_gate_out_bwd")
    g["hgrn_w_g"] = _mm(xn0, dzg, mode="tn", name="hgrn_w_g_bwd_w")
    dzq, dzf, dzi, g_lb = _hgrn_bwd(zq, zf, zi, lb_logits_full, states, do_rec, scatter_start("hgrn_out"),
                                    name="hgrn_bwd")
    for nm, dz in (("hgrn_w_q", dzq), ("hgrn_w_f", dzf), ("hgrn_w_i", dzi)):
        g[nm] = _mm(xn0, dz, mode="tn", name=f"{nm}_bwd_w")

    def hgrn_proj_bwd(a, dres, *rest):
        dzs, gw, weights = rest[:4], rest[4], rest[5:]
        dxn = _dot(dzs[0], weights[0], _NT)
        for dz, wt in zip(dzs[1:], weights[1:]):
            dxn = dxn + _dot(dz, wt, _NT)
        dx, dw = _rms_bwd(a, gw, dxn)
        return dx + dres, dw

    grad_x, g_hgrn_norm = _rowcall(hgrn_proj_bwd, [xs, dh1, dzq, dzf, dzi, dzg],
                                   [hgrn_norm_full] + [w[k] for k in proj_names], [(d_model, F32)], [d_model],
                                   tr=512, name="hgrn_proj_bwd")

    small_parts = [g_hgrn_norm, g_lb, g_g_norm, g_mla_norm, g_q_norm, g_kv_in_norm, g_kv_norm, g_mlp_norm0,
                   g_mlp_norm1, g_final_norm, loss_part]
    small_sizes = [p.shape[1] for p in small_parts]
    small_terms = _exchange([jnp.concatenate(small_parts, axis=1)], scatter=False, name="gather_small")[0]
    small_sum = _sum_terms(small_terms, name="sum_small")
    last = scatter_start("hgrn_in", after=small_sum)
    offs = [0]
    for sz in small_sizes:
        offs.append(offs[-1] + sz)
    (s_hgrn_norm, s_lb, s_g_norm, s_mla_norm, s_q_norm, s_kv_in_norm, s_kv_norm, s_mlp_norm0, s_mlp_norm1, s_final_norm,
     s_loss) = [small_sum[:, a:b] for a, b in zip(offs[:-1], offs[1:])]
    shard = hgrn_norm.shape[1]
    g_lb_logits = _lb_logits_grad(lax.dynamic_slice_in_dim(s_lb, me * shard, shard, axis=1), hgrn_lb_logits,
                                  name="lb_logits_grad")
    loss = s_loss[0, 0]

    res, layer_terms = {}, {}

    def update(k, term_list):
        shape = given[k].shape
        as_layers = (len(term_list), shape[-2], shape[-1])
        upd = _adam(given[k].reshape(as_layers), term_list, given["m_" + k].reshape(as_layers),
                    given["v_" + k].reshape(as_layers), name=f"adam_{k}")
        res[k] = [o.reshape(shape) for o in upd]
        return upd[0]

    for tag, names in groups.items():
        for k, t in zip(names, _exchange_wait(scatter_state[tag], last, name=f"scatter_{tag}_wait")):
            if k.startswith("mlp_w_"):
                layer_terms.setdefault(k[:-1], {})[int(k[-1])] = t
                if len(layer_terms[k[:-1]]) == 2:
                    last = update(k[:-1], [layer_terms[k[:-1]][0], layer_terms[k[:-1]][1]])
            else:
                last = update(k, [t])

    small_grads = {
        "hgrn_norm": lax.dynamic_slice_in_dim(s_hgrn_norm, me * shard, shard, axis=1),
        "hgrn_g_norm": s_g_norm, "hgrn_lb_logits": g_lb_logits, "mla_norm": s_mla_norm, "mla_q_norm": s_q_norm,
        "kv_in_norm": s_kv_in_norm, "kv_norm": s_kv_norm,
        "mlp_norm": jnp.concatenate([s_mlp_norm0, s_mlp_norm1], axis=0), "final_norm": s_final_norm,
    }
    small_names = list(small_grads)

    def flat(a):
        return a.reshape(1, -1)

    packed = [jnp.concatenate([flat(src[pre + k]) for k in small_names], axis=1)
              for src, pre in ((given, ""), (small_grads, ""), (given, "m_"), (given, "v_"))]
    small_out = _adam(packed[0][None], [packed[1][None]], packed[2][None], packed[3][None], name="adam_small")
    off = 0
    for k in small_names:
        size = given[k].size
        res[k] = [o[0, :, off:off + size].reshape(given[k].shape) for o in small_out]
        off += size

    outs = [loss, grad_x[None]]
    for i in range(4):
        outs += [res[k][i] for k in weight_names]
    return tuple(outs)
```

```python
import functools

import jax
import jax.numpy as jnp
from jax import lax
from jax.experimental import pallas as pl
from jax.experimental.pallas import tpu as pltpu

F32 = jnp.float32
BF16 = jnp.bfloat16

EPS = 1e-6
LANES = 128
N_DEV = 8
V7X_VMEM_LIMIT_BYTES = 56 << 20
MM_PIPELINE_BYTES = 30 << 20
MM_ROW_TILE = 512
GRAD_WIRE_DTYPE = BF16

HGRN_HEADS = 8
HGRN_CHUNK = 64
HGRN_SUB = 16
HGRN_HEADS_PER_STEP = 8
HGRN_CHUNKS_PER_STEP = 4
EXP_CLAMP = 80.0
MLA_HEADS = 16
MLA_NOPE = 128
MLA_ROPE = 64
ROPE_THETA = 10000.0
ATTN_SCALE = (MLA_NOPE + MLA_ROPE) ** -0.5

ADAM_LR = 0.001
ADAM_B1 = 0.9
ADAM_B2 = 0.999
ADAM_EPS = 1e-08
ADAM_WD = 0.01
ADAM_STEP = 10

_NN = ((1,), (0,))
_NT = ((1,), (1,))
_TN = ((0,), (0,))


def _params(*sem):
    return pltpu.CompilerParams(dimension_semantics=sem, vmem_limit_bytes=V7X_VMEM_LIMIT_BYTES)


def _dot(a, b, dims):
    return lax.dot_general(a.astype(BF16), b.astype(BF16), (dims, ((), ())), preferred_element_type=F32)


def _dot_f32(a, b, dims=_NN):
    return lax.dot_general(a, b, (dims, ((), ())), precision=lax.Precision.HIGH, preferred_element_type=F32)


def _sigmoid(x):
    return 1.0 / (1.0 + jnp.exp(-x))


def _rms(x, w):
    r = lax.rsqrt(jnp.mean(x * x, axis=-1, keepdims=True) + EPS)
    return x * r * w


def _rms_bwd(x, w, dy):
    r = lax.rsqrt(jnp.mean(x * x, axis=-1, keepdims=True) + EPS)
    xh = x * r
    dw = jnp.sum(dy * xh, axis=0, keepdims=True)
    dxh = dy * w
    dx = r * (dxh - xh * jnp.mean(dxh * xh, axis=-1, keepdims=True))
    return dx, dw


def _mm_tiles(m, n, k, a_bytes, b_bytes, out_tile_bytes):
    tm = min(m, MM_ROW_TILE)
    for tn in (n, 2048, 1024, 512, 256, LANES):
        if tn <= n and n % tn == 0:
            if 2 * (tm * k * a_bytes + k * tn * b_bytes + tm * tn * out_tile_bytes) <= MM_PIPELINE_BYTES:
                return tm, tn
    return tm, min(n, LANES)


def _mm(a, b, *, mode, name, out_dtype=None, add=None, relu2_of=None, after=None, col_shards=None):
    if mode == "nn":
        (m, k), (k2, n) = a.shape, b.shape
    elif mode == "nt":
        (m, k), (n, k2) = a.shape, b.shape
    else:
        (k, m), (k2, n) = a.shape, b.shape
    assert k == k2, (name, a.shape, b.shape)
    if out_dtype is None:
        out_dtype = GRAD_WIRE_DTYPE if mode == "tn" else F32
    tile_bytes = sum(x.dtype.itemsize for x in (add, relu2_of) if x is not None) + jnp.dtype(out_dtype).itemsize
    tm, tn = _mm_tiles(m, n, k, a.dtype.itemsize, b.dtype.itemsize, tile_bytes)
    if col_shards is not None:
        assert add is None and relu2_of is None
        tn = n // col_shards
    assert m % tm == 0 and n % tn == 0, (name, m, n)
    dims = {"nn": _NN, "nt": _NT, "tn": _TN}[mode]
    a_spec = pl.BlockSpec((k, tm), lambda i, j: (0, i)) if mode == "tn" else pl.BlockSpec((tm, k), lambda i, j: (i, 0))
    b_spec = pl.BlockSpec((tn, k), lambda i, j: (j, 0)) if mode == "nt" else pl.BlockSpec((k, tn), lambda i, j: (0, j))
    o_spec = pl.BlockSpec((tm, tn), lambda i, j: (i, j))
    operands, in_specs = [a, b], [a_spec, b_spec]
    for extra in (add, relu2_of):
        if extra is not None:
            assert extra.shape == (m, n), (name, extra.shape)
            operands.append(extra)
            in_specs.append(o_spec)
    n_in = len(operands)
    if after is not None:
        operands.append(after)
        in_specs.append(pl.BlockSpec(memory_space=pl.ANY))
    out_shape = jax.ShapeDtypeStruct((m, n), out_dtype)
    if col_shards is not None:
        out_shape = jax.ShapeDtypeStruct((col_shards, m, tn), out_dtype)
        o_spec = pl.BlockSpec((None, tm, tn), lambda i, j: (j, i, 0))

    def body(*refs):
        acc = _dot(refs[0][...], refs[1][...], dims)
        extras, outs = refs[2:n_in], refs[len(operands):]
        if add is not None:
            acc = acc + extras[0][...]
        if relu2_of is not None:
            acc = acc * (2.0 * jnp.sqrt(extras[-1][...].astype(F32)))
        outs[0][...] = acc.astype(out_dtype)

    return pl.pallas_call(
        body, name=name, grid=(m // tm, n // tn), in_specs=in_specs, out_specs=o_spec, out_shape=out_shape,
        compiler_params=_params("parallel", "parallel"),
    )(*operands)


def _rowcall(fn, rows, consts, outs, accs, *, name, tr=256, after=None):
    s = rows[0].shape[0]
    tr = min(tr, s)
    assert s % tr == 0
    n_out = len(outs)
    accs = [(1, a) if isinstance(a, int) else a for a in accs]
    in_specs = [pl.BlockSpec((tr, r.shape[1]), lambda i: (i, 0)) for r in rows]
    in_specs += [pl.BlockSpec(c.shape, lambda i, nd=c.ndim: (0,) * nd) for c in consts]
    out_shape = [jax.ShapeDtypeStruct((s, w), dt) for w, dt in outs] + [jax.ShapeDtypeStruct(a, F32) for a in accs]
    out_specs = [pl.BlockSpec((tr, w), lambda i: (i, 0)) for w, _ in outs] + [pl.BlockSpec(a, lambda i: (0, 0)) for a in accs]
    n_in = len(rows) + len(consts)

    def body(*refs):
        res = fn(*[r[...] for r in refs[:n_in]])
        out_refs = refs[n_in + (after is not None):]
        for ref, val in zip(out_refs[:n_out], res[:n_out]):
            ref[...] = val.astype(ref.dtype)
        i = pl.program_id(0)
        for ref, val in zip(out_refs[n_out:], res[n_out:]):
            @pl.when(i == 0)
            def _(ref=ref, val=val):
                ref[...] = val

            @pl.when(i > 0)
            def _(ref=ref, val=val):
                ref[...] += val

    behind = [] if after is None else [after]
    return pl.pallas_call(
        body, name=name, grid=(s // tr,), in_specs=in_specs + [pl.BlockSpec(memory_space=pl.ANY)] * len(behind),
        out_specs=out_specs, out_shape=out_shape, compiler_params=_params("arbitrary" if accs else "parallel"),
    )(*rows, *consts, *behind)


def _rope_tables(seq):
    half = MLA_ROPE // 2
    inv_freq = ROPE_THETA ** (-jnp.arange(half, dtype=F32) / half)
    ang = jnp.arange(seq, dtype=F32)[:, None] * inv_freq[None, :]
    cos, sin, zero = jnp.cos(ang), jnp.sin(ang), jnp.zeros((seq, half), F32)
    t_c = jnp.concatenate([cos, cos, zero, zero], axis=1)
    t_s1 = jnp.concatenate([-sin, zero, zero, zero], axis=1)
    t_s2 = jnp.concatenate([zero, sin, zero, zero], axis=1)
    return t_c, t_s1, t_s2


def _rope(slab, t_c, t_s1, t_s2):
    return slab * t_c + pltpu.roll(slab, 96, 1) * t_s1 + pltpu.roll(slab, 32, 1) * t_s2


def _rope_t(d, t_c, t_s1, t_s2):
    return d * t_c + pltpu.roll(d * t_s1, 32, 1) + pltpu.roll(d * t_s2, 96, 1)


def _lower_bound(logits):
    l0, l1 = logits[0:1, :], logits[1:2, :]
    mx = jnp.maximum(l0, l1)
    e0, e1 = jnp.exp(l0 - mx), jnp.exp(l1 - mx)
    return e0 / (e0 + e1)


def _tri(n, lower):
    row = lax.broadcasted_iota(jnp.int32, (n, n), 0)
    col = lax.broadcasted_iota(jnp.int32, (n, n), 1)
    return (row >= col) if lower else (row <= col)


def _hgrn_fwd(zq, zf, zi, lb_logits, *, name):
    s, d = zq.shape
    h_n, c, hp, cps = d // LANES, HGRN_CHUNK, HGRN_HEADS_PER_STEP, HGRN_CHUNKS_PER_STEP
    nc = s // c

    def body(zq_ref, zf_ref, zi_ref, lb_ref, o_ref, st_ref, state_sc, b_sc):
        @pl.when(pl.program_id(1) == 0)
        def _():
            state_sc[...] = jnp.zeros_like(state_sc)

        lower = _tri(c, True)
        lower_f = lower.astype(F32)
        hs, pairs = range(hp), [(cc, hh) for cc in range(cps) for hh in range(hp)]
        sls = [slice(hh * LANES, (hh + 1) * LANES) for hh in hs]
        rws = [slice(cc * c, (cc + 1) * c) for cc in range(cps)]
        lb = [_lower_bound(lb_ref[:, sl]) for sl in sls]
        zq_v = {p: zq_ref[rws[p[0]], sls[p[1]]] for p in pairs}
        q = {p: zq_v[p] * _sigmoid(zq_v[p]) for p in pairs}
        f = {p: lb[p[1]] + (1.0 - lb[p[1]]) * _sigmoid(zf_ref[rws[p[0]], sls[p[1]]]) for p in pairs}
        k = {p: 1.0 - f[p] for p in pairs}
        v = {p: zi_ref[rws[p[0]], sls[p[1]]] for p in pairs}
        b = {p: _dot_f32(lower_f, jnp.log(f[p])) for p in pairs}
        for p in pairs:
            b_sc[p[0], p[1]] = b[p]
        qe = {p: q[p] * jnp.exp(b[p]) for p in pairs}
        scores = {p: [] for p in pairs}
        for i in range(c // HGRN_SUB):
            lo = i * HGRN_SUB
            for p in pairs:
                ref = b_sc[p[0], p[1], lo - 1:lo, :] if i > 0 else jnp.zeros((1, LANES), F32)
                qt = q[p][lo:lo + HGRN_SUB, :] * jnp.exp(b[p][lo:lo + HGRN_SUB, :] - ref)
                dec = jnp.exp(jnp.minimum(ref - b[p], EXP_CLAMP))
                scores[p].append(_dot(qt, k[p] * dec, _NT))
        o_intra = {p: _dot(jnp.where(lower, jnp.concatenate(scores[p], axis=0), 0.0), v[p], _NN) for p in pairs}
        bl = {p: b_sc[p[0], p[1], c - 1:c, :] for p in pairs}
        k_end = {p: k[p] * jnp.exp(bl[p] - b[p]) for p in pairs}
        state = [state_sc[hh] for hh in hs]
        for cc in range(cps):
            for hh in hs:
                st_ref[hh, cc] = state[hh]
                o_ref[rws[cc], sls[hh]] = _dot(qe[cc, hh], state[hh], _NT) + o_intra[cc, hh]
            state = [state[hh] * jnp.exp(bl[cc, hh]) + _dot(v[cc, hh], k_end[cc, hh], _TN) for hh in hs]
        for hh in hs:
            state_sc[hh] = state[hh]

    tile = pl.BlockSpec((cps * c, hp * LANES), lambda h, i: (i, h))
    return pl.pallas_call(
        body, name=name, grid=(h_n // hp, nc // cps),
        in_specs=[tile, tile, tile, pl.BlockSpec((2, hp * LANES), lambda h, i: (0, h))],
        out_specs=[tile, pl.BlockSpec((hp, cps, LANES, LANES), lambda h, i: (h, i, 0, 0))],
        out_shape=[jax.ShapeDtypeStruct((s, d), F32), jax.ShapeDtypeStruct((h_n, nc, LANES, LANES), F32)],
        scratch_shapes=[pltpu.VMEM((hp, LANES, LANES), F32), pltpu.VMEM((cps, hp, c, LANES), F32)],
        compiler_params=_params("parallel", "arbitrary"),
    )(zq, zf, zi, lb_logits)


def _hgrn_bwd(zq, zf, zi, lb_logits, states, do, after, *, name):
    s, d = zq.shape
    h_n, c, hp, cps = d // LANES, HGRN_CHUNK, HGRN_HEADS_PER_STEP, HGRN_CHUNKS_PER_STEP
    nc = s // c
    n_steps = nc // cps

    def body(zq_ref, zf_ref, zi_ref, lb_ref, st_ref, do_ref, _, dzq_ref, dzf_ref, dzi_ref, dlb_ref, dstate_sc, b_sc):
        @pl.when(pl.program_id(1) == 0)
        def _():
            dstate_sc[...] = jnp.zeros_like(dstate_sc)
            dlb_ref[...] = jnp.zeros_like(dlb_ref)

        lower, upper = _tri(c, True), _tri(c, False).astype(F32)
        lower_f = lower.astype(F32)
        last_row = lax.broadcasted_iota(jnp.int32, (c, LANES), 0) == c - 1
        hs, pairs = range(hp), [(cc, hh) for cc in range(cps) for hh in range(hp)]
        sls = [slice(hh * LANES, (hh + 1) * LANES) for hh in hs]
        rws = [slice(cc * c, (cc + 1) * c) for cc in range(cps)]
        lb = [_lower_bound(lb_ref[:, sl]) for sl in sls]
        zq_v = {p: zq_ref[rws[p[0]], sls[p[1]]] for p in pairs}
        sq = {p: _sigmoid(zq_v[p]) for p in pairs}
        q = {p: zq_v[p] * sq[p] for p in pairs}
        sf = {p: _sigmoid(zf_ref[rws[p[0]], sls[p[1]]]) for p in pairs}
        f = {p: lb[p[1]] + (1.0 - lb[p[1]]) * sf[p] for p in pairs}
        k = {p: 1.0 - f[p] for p in pairs}
        v = {p: zi_ref[rws[p[0]], sls[p[1]]] for p in pairs}
        d_o = {p: do_ref[rws[p[0]], sls[p[1]]] for p in pairs}
        b = {p: _dot_f32(lower_f, jnp.log(f[p])) for p in pairs}
        s0t = {p: st_ref[p[1], p[0]] for p in pairs}
        for p in pairs:
            b_sc[p[0], p[1]] = b[p]
        bl = {p: b_sc[p[0], p[1], c - 1:c, :] for p in pairs}
        eb = {p: jnp.exp(b[p]) for p in pairs}
        ebl = {p: jnp.exp(bl[p]) for p in pairs}
        dec_end = {p: jnp.exp(bl[p] - b[p]) for p in pairs}
        da = {p: jnp.where(lower, _dot(d_o[p], v[p], _NT), 0.0) for p in pairs}
        dq = {p: _dot(d_o[p], s0t[p], _NN) * eb[p] for p in pairs}
        dstate_in = {p: _dot(d_o[p], q[p] * eb[p], _TN) for p in pairs}
        dk_intra = {p: jnp.zeros((c, LANES), F32) for p in pairs}
        scores, dq_blocks = {p: [] for p in pairs}, {p: [] for p in pairs}
        for i in range(c // HGRN_SUB):
            lo = i * HGRN_SUB
            for p in pairs:
                ref = b_sc[p[0], p[1], lo - 1:lo, :] if i > 0 else jnp.zeros((1, LANES), F32)
                grow = jnp.exp(b[p][lo:lo + HGRN_SUB, :] - ref)
                qt = q[p][lo:lo + HGRN_SUB, :] * grow
                dec = jnp.exp(jnp.minimum(ref - b[p], EXP_CLAMP))
                kd = k[p] * dec
                scores[p].append(_dot(qt, kd, _NT))
                da_i = da[p][lo:lo + HGRN_SUB, :]
                dq_blocks[p].append(_dot_f32(da_i, kd, _NN) * grow)
                dk_intra[p] = dk_intra[p] + _dot_f32(da_i, qt, _TN) * dec
        dv_intra = {p: _dot(jnp.where(lower, jnp.concatenate(scores[p], axis=0), 0.0), d_o[p], _TN) for p in pairs}
        dq = {p: dq[p] + jnp.concatenate(dq_blocks[p], axis=0) for p in pairs}
        q_dq = {p: q[p] * dq[p] for p in pairs}
        for p in pairs:
            dzq_ref[rws[p[0]], sls[p[1]]] = (dq[p] * sq[p] * (1.0 + zq_v[p] * (1.0 - sq[p]))).astype(BF16)
        dstate = [dstate_sc[hh] for hh in hs]
        for cc in reversed(range(cps)):
            ps = [(cc, hh) for hh in hs]
            dk_state = [_dot(v[p], dstate[p[1]], _NN) * dec_end[p] for p in ps]
            dv = [dv_intra[p] + _dot(k[p] * dec_end[p], dstate[p[1]], _NT) for p in ps]
            dk = [dk_intra[p] + dk_state[p[1]] for p in ps]
            db_last = [jnp.sum(k[p] * dk_state[p[1]], axis=0, keepdims=True)
                       + ebl[p] * jnp.sum(s0t[p] * dstate[p[1]], axis=0, keepdims=True) for p in ps]
            db = [q_dq[p] - k[p] * dk[p[1]] + jnp.where(last_row, db_last[p[1]], 0.0) for p in ps]
            df = [_dot_f32(upper, db[p[1]]) / f[p] - dk[p[1]] for p in ps]
            for p in ps:
                hh = p[1]
                dzf_ref[rws[cc], sls[hh]] = (df[hh] * (1.0 - lb[hh]) * sf[p] * (1.0 - sf[p])).astype(BF16)
                dlb_ref[:, sls[hh]] += jnp.sum(df[hh] * (1.0 - sf[p]), axis=0, keepdims=True)
                dzi_ref[rws[cc], sls[hh]] = dv[hh].astype(BF16)
            dstate = [dstate[p[1]] * ebl[p] + dstate_in[p] for p in ps]
        for hh in hs:
            dstate_sc[hh] = dstate[hh]

    tile = pl.BlockSpec((cps * c, hp * LANES), lambda h, i: (n_steps - 1 - i, h))
    out = jax.ShapeDtypeStruct((s, d), BF16)
    return pl.pallas_call(
        body, name=name, grid=(h_n // hp, n_steps),
        in_specs=[tile, tile, tile, pl.BlockSpec((2, hp * LANES), lambda h, i: (0, h)),
                  pl.BlockSpec((hp, cps, LANES, LANES), lambda h, i: (h, n_steps - 1 - i, 0, 0)), tile,
                  pl.BlockSpec(memory_space=pl.ANY)],
        out_specs=[tile, tile, tile, pl.BlockSpec((1, hp * LANES), lambda h, i: (0, h))],
        out_shape=[out, out, out, jax.ShapeDtypeStruct((1, d), F32)],
        scratch_shapes=[pltpu.VMEM((hp, LANES, LANES), F32), pltpu.VMEM((cps, hp, c, LANES), F32)],
        compiler_params=_params("parallel", "arbitrary"),
    )(zq, zf, zi, lb_logits, states, do, after)


ATTN_SUB_ROWS = 256
LOG2E = 1.4426950408889634
LN2 = 0.6931471805599453
Q_PRESCALE = ATTN_SCALE * LOG2E


def _attn_tile(s):
    return min(1024, max(128, s // 2))


def _causal_pairs(n, q_major):
    pairs = [(i, j) for i in range(n) for j in range(i + 1)] if q_major else [(i, j) for j in range(n) for i in range(j, n)]
    return jnp.asarray([p[0] for p in pairs], jnp.int32), jnp.asarray([p[1] for p in pairs], jnp.int32)


def _sub_scores(qn_ref, qr_ref, k, r, sub, t, diagonal):
    q = jnp.concatenate([qn_ref[r:r + sub, :], qr_ref[r:r + sub, :]], axis=1)
    if not diagonal:
        return q, _dot(q, k, _NT)
    cols = r + sub
    keep = lax.broadcasted_iota(jnp.int32, (sub, cols), 1) <= r + lax.broadcasted_iota(jnp.int32, (sub, cols), 0)
    return q, jnp.where(keep, _dot(q, k[:cols], _NT), -jnp.inf)


def _attn_fwd(qn, qr, kn, kr, v, *, name):
    s, t = qn.shape[0], _attn_tile(qn.shape[0])
    sub = min(t, ATTN_SUB_ROWS)
    q_blk, k_blk = _causal_pairs(s // t, True)

    def body(qi_ref, kj_ref, qn_ref, qr_ref, kn_ref, kr_ref, v_ref, o_ref, lse_ref, m_sc, l_sc, acc_sc):
        p_id = pl.program_id(1)
        i, j = qi_ref[p_id], kj_ref[p_id]

        @pl.when(j == 0)
        def _():
            m_sc[...] = jnp.full_like(m_sc, -jnp.inf)
            l_sc[...] = jnp.zeros_like(l_sc)
            acc_sc[...] = jnp.zeros_like(acc_sc)

        def update(diagonal):
            k = jnp.concatenate([kn_ref[...], kr_ref[...]], axis=1)
            v = v_ref[...]
            starts = list(range(0, t, sub))
            scs = [_sub_scores(qn_ref, qr_ref, k, r, sub, t, diagonal)[1] for r in starts]
            ps, alphas = [], []
            for r, sc in zip(starts, scs):
                m_prev = m_sc[r:r + sub, :]
                m_new = jnp.maximum(m_prev, jnp.max(sc, axis=1, keepdims=True))
                alpha = jnp.exp2(m_prev - m_new)
                p = jnp.exp2(sc - m_new[:, :1])
                l_sc[r:r + sub, :] = alpha * l_sc[r:r + sub, :] + jnp.sum(p, axis=1, keepdims=True)
                m_sc[r:r + sub, :] = m_new
                ps.append(p)
                alphas.append(alpha)
            for r, p, alpha in zip(starts, ps, alphas):
                acc_sc[r:r + sub, :] = alpha * acc_sc[r:r + sub, :] + _dot(p, v[:p.shape[1]], _NN)

        @pl.when(j < i)
        def _():
            update(False)

        @pl.when(j == i)
        def _():
            update(True)
            o_ref[...] = (acc_sc[...] / l_sc[...]).astype(BF16)
            lse_ref[...] = m_sc[...] + jnp.log(l_sc[...]) * LOG2E

    q_spec = pl.BlockSpec((t, LANES), lambda h, p, qi, kj: (qi[p], h))
    k_spec = pl.BlockSpec((t, LANES), lambda h, p, qi, kj: (kj[p], h))
    kr_spec = pl.BlockSpec((t, LANES), lambda h, p, qi, kj: (kj[p], 0))
    stat = pltpu.VMEM((t, LANES), F32)
    return pl.pallas_call(
        body, name=name,
        grid_spec=pltpu.PrefetchScalarGridSpec(
            num_scalar_prefetch=2, grid=(MLA_HEADS, q_blk.shape[0]),
            in_specs=[q_spec, q_spec, k_spec, kr_spec, k_spec], out_specs=[q_spec, q_spec],
            scratch_shapes=[stat, stat, stat]),
        out_shape=[jax.ShapeDtypeStruct(qn.shape, BF16), jax.ShapeDtypeStruct(qn.shape, F32)],
        compiler_params=_params("parallel", "arbitrary"),
    )(q_blk, k_blk, qn, qr, kn, kr, v)


def _attn_bwd(qn, qr, kn, kr, v, do, lse, delta, *, name):
    s, t = qn.shape[0], _attn_tile(qn.shape[0])
    n, sub = s // t, min(t, ATTN_SUB_ROWS)
    q_blk, k_blk = _causal_pairs(n, False)

    def body(qi_ref, kj_ref, qn_ref, qr_ref, kn_ref, kr_ref, v_ref, do_ref, lse_ref, delta_ref,
             dqn_ref, dqr_ref, dkn_ref, dv_ref, dkr_ref, dk_sc, dv_sc):
        p_id = pl.program_id(1)
        i, j = qi_ref[p_id], kj_ref[p_id]

        @pl.when(p_id == 0)
        def _():
            dqn_ref[...] = jnp.zeros_like(dqn_ref)
            dqr_ref[...] = jnp.zeros_like(dqr_ref)

        @pl.when(i == j)
        def _():
            dk_sc[...] = jnp.zeros_like(dk_sc)
            dv_sc[...] = jnp.zeros_like(dv_sc)

        def accumulate(diagonal):
            k = jnp.concatenate([kn_ref[...], kr_ref[...]], axis=1)
            v = v_ref[...]
            starts = list(range(0, t, sub))
            qs, d_os, scs, dps = [], [], [], []
            for r in starts:
                q, sc = _sub_scores(qn_ref, qr_ref, k, r, sub, t, diagonal)
                d_o = do_ref[r:r + sub, :]
                qs.append(q)
                d_os.append(d_o)
                scs.append(sc)
                dps.append(_dot(d_o, v[:sc.shape[1]], _NT))
            ps, dss = [], []
            for r, sc, dp in zip(starts, scs, dps):
                p = jnp.exp2(sc - lse_ref[r:r + sub, :][:, :1])
                ps.append(p.astype(BF16))
                dss.append((p * (dp - delta_ref[r:r + sub, :][:, :1])).astype(BF16))
            for r, q, d_o, p, ds in zip(starts, qs, d_os, ps, dss):
                cols = p.shape[1]
                dv_sc[:cols, :] += _dot(p, d_o, _TN)
                dk_sc[:cols, :] += _dot(ds, q, _TN)
                dq = _dot(ds, k[:cols], _NN) * ATTN_SCALE
                rows = pl.ds(pl.multiple_of(i * t + r, sub), sub)
                dqn_ref[rows, :] += dq[:, :LANES]
                dqr_ref[rows, :] += dq[:, LANES:]

        @pl.when(j < i)
        def _():
            accumulate(False)

        @pl.when(j == i)
        def _():
            accumulate(True)

        @pl.when(i == n - 1)
        def _():
            dkn_ref[...] = (dk_sc[:, :LANES] * LN2).astype(BF16)
            dkr_ref[...] = dk_sc[:, LANES:] * LN2
            dv_ref[...] = dv_sc[...].astype(BF16)

    q_spec = pl.BlockSpec((t, LANES), lambda h, p, qi, kj: (qi[p], h))
    k_spec = pl.BlockSpec((t, LANES), lambda h, p, qi, kj: (kj[p], h))
    kr_spec = pl.BlockSpec((t, LANES), lambda h, p, qi, kj: (kj[p], 0))
    head_spec = pl.BlockSpec((s, LANES), lambda h, p, qi, kj: (0, h))
    f32_out, bf16_out = jax.ShapeDtypeStruct(qn.shape, F32), jax.ShapeDtypeStruct(qn.shape, BF16)
    return pl.pallas_call(
        body, name=name,
        grid_spec=pltpu.PrefetchScalarGridSpec(
            num_scalar_prefetch=2, grid=(MLA_HEADS, q_blk.shape[0]),
            in_specs=[q_spec, q_spec, k_spec, kr_spec, k_spec, q_spec, q_spec, q_spec],
            out_specs=[head_spec, head_spec, k_spec, k_spec, k_spec],
            scratch_shapes=[pltpu.VMEM((t, 2 * LANES), F32), pltpu.VMEM((t, LANES), F32)]),
        out_shape=[f32_out, f32_out, bf16_out, bf16_out, f32_out],
        compiler_params=_params("parallel", "arbitrary"),
    )(q_blk, k_blk, qn, qr, kn, kr, v, do, lse, delta)


def _exchange(arrs, *, scatter, name):
    n = len(arrs)
    out_shape = [jax.ShapeDtypeStruct(a.shape if scatter else (N_DEV, *a.shape), a.dtype) for a in arrs]

    def body(*refs):
        ins, outs = refs[:n], refs[n:2 * n]
        send_sems, recv_sems, local_sems = refs[2 * n:]
        x, y, c = lax.axis_index("x"), lax.axis_index("y"), lax.axis_index("c")
        me = 4 * x + 2 * y + c
        copies = []
        for k in range(n):
            local = pltpu.make_async_copy(ins[k].at[me] if scatter else ins[k], outs[k].at[me], local_sems.at[k])
            local.start()
            copies.append(local)
            for d in range(1, N_DEV):
                px, py, pc = (x + (d >> 2)) % 2, (y + ((d >> 1) & 1)) % 2, (c + (d & 1)) % 2
                peer = 4 * px + 2 * py + pc
                remote = pltpu.make_async_remote_copy(
                    src_ref=ins[k].at[peer] if scatter else ins[k], dst_ref=outs[k].at[me],
                    send_sem=send_sems.at[k, d - 1], recv_sem=recv_sems.at[k, d - 1],
                    device_id=(px, py, pc), device_id_type=pl.DeviceIdType.MESH)
                remote.start()
                copies.append(remote)
        for cp in copies:
            cp.wait()

    any_spec = pl.BlockSpec(memory_space=pl.ANY)
    return pl.pallas_call(
        body, name=name, in_specs=[any_spec] * n, out_specs=[any_spec] * n, out_shape=out_shape,
        scratch_shapes=[pltpu.SemaphoreType.DMA((n, N_DEV - 1)), pltpu.SemaphoreType.DMA((n, N_DEV - 1)),
                        pltpu.SemaphoreType.DMA((n,))],
    )(*arrs)


def _peers(x, y, c):
    out = []
    for d in range(1, N_DEV):
        px, py, pc = (x + (d >> 2)) % 2, (y + ((d >> 1) & 1)) % 2, (c + (d & 1)) % 2
        out.append(((px, py, pc), 4 * px + 2 * py + pc))
    return out


CHIP_LEVEL_PEERS = (1, 2, 4, 6)


def _exchange_copies(ins, lands, send_sems, recv_sems, scatter, chip_level=False):
    x, y, c = lax.axis_index("x"), lax.axis_index("y"), lax.axis_index("c")
    me = 4 * x + 2 * y + c
    local, remote = [], []
    for k in range(len(ins)):
        local.append(pltpu.make_async_copy(ins[k].at[me] if scatter else ins[k], lands[k].at[me],
                                           recv_sems.at[k * N_DEV + N_DEV - 1]))
        for d, (coords, peer) in enumerate(_peers(x, y, c)):
            if chip_level and d + 1 not in CHIP_LEVEL_PEERS:
                continue
            remote.append(pltpu.make_async_remote_copy(
                src_ref=ins[k].at[peer] if scatter else ins[k], dst_ref=lands[k].at[me],
                send_sem=send_sems.at[k * N_DEV + d], recv_sem=recv_sems.at[k * N_DEV + d],
                device_id=coords, device_id_type=pl.DeviceIdType.MESH))
    return local, remote


def _exchange_start(arrs, *, scatter, name, after=None, chip_level=False):
    n = len(arrs)
    hbm = pl.BlockSpec(memory_space=pltpu.HBM)
    sem = pl.BlockSpec(memory_space=pltpu.SEMAPHORE)
    lands = [lax.empty(a.shape if scatter else (N_DEV, *a.shape), a.dtype) for a in arrs]

    def body(*refs):
        ins, land_refs = refs[:n], refs[n:2 * n]
        first_out = 2 * n + (after is not None)
        send_sems, recv_sems, token = refs[first_out], refs[first_out + 1], refs[-1]
        local, remote = _exchange_copies(ins, land_refs, send_sems, recv_sems, scatter, chip_level)
        for cp in local + remote:
            cp.start()
        token[...] = jnp.zeros_like(token)

    operands = [pltpu.with_memory_space_constraint(a, pltpu.HBM) for a in list(arrs) + lands]
    behind = [] if after is None else [after]
    res = pl.pallas_call(
        body, name=name,
        out_shape=(pltpu.SemaphoreType.DMA((n * N_DEV,)), pltpu.SemaphoreType.DMA((n * N_DEV,)),
                   *[pltpu.HBM(o.shape, o.dtype) for o in operands], jax.ShapeDtypeStruct((8, LANES), F32)),
        in_specs=[hbm] * (2 * n) + [pl.BlockSpec(memory_space=pl.ANY)] * len(behind),
        out_specs=(sem, sem, *[hbm] * (2 * n), pl.BlockSpec(memory_space=pltpu.VMEM)),
        input_output_aliases={i: 2 + i for i in range(2 * n)},
        compiler_params=pltpu.CompilerParams(has_side_effects=pltpu.SideEffectType.DATAFLOW_SIDE_EFFECTING),
    )(*operands, *behind)
    return (res[0], res[1], list(res[2:2 + n]), list(res[2 + n:2 + 2 * n]), scatter, chip_level), res[-1]


def _exchange_wait(state, after, *, name):
    send_sems, recv_sems, ins, lands, scatter, chip_level = state
    n = len(ins)
    hbm = pl.BlockSpec(memory_space=pltpu.HBM)
    sem = pl.BlockSpec(memory_space=pltpu.SEMAPHORE)

    def body(*refs):
        in_refs, land_refs = refs[:n], refs[n:2 * n]
        local, remote = _exchange_copies(in_refs, land_refs, refs[2 * n], refs[2 * n + 1], scatter, chip_level)
        for cp in local:
            cp.wait()
        for cp in remote:
            cp.wait_send()
            cp.wait_recv()

    res = pl.pallas_call(
        body, name=name, out_shape=tuple(pltpu.HBM(o.shape, o.dtype) for o in ins + lands),
        in_specs=[hbm] * (2 * n) + [sem, sem, pl.BlockSpec(memory_space=pl.ANY)], out_specs=tuple([hbm] * (2 * n)),
        input_output_aliases={i: i for i in range(2 * n)},
        compiler_params=pltpu.CompilerParams(has_side_effects=pltpu.SideEffectType.DATAFLOW_SIDE_EFFECTING),
    )(*ins, *lands, send_sems, recv_sems, after)
    return list(res[n:])


def _chip_forward(lands, *, name):
    n = len(lands)

    def body(*refs):
        ins, outs, send_sems, recv_sems = refs[:n], refs[n:2 * n], refs[2 * n], refs[2 * n + 1]
        x, y, c = lax.axis_index("x"), lax.axis_index("y"), lax.axis_index("c")
        copies = []
        for k in range(n):
            for j, (dx, dy) in enumerate(((0, 1), (1, 0), (1, 1))):
                held = 4 * ((x + dx) % 2) + 2 * ((y + dy) % 2) + c
                cp = pltpu.make_async_remote_copy(
                    src_ref=ins[k].at[held], dst_ref=outs[k].at[held], send_sem=send_sems.at[k, j],
                    recv_sem=recv_sems.at[k, j], device_id=(x, y, 1 - c), device_id_type=pl.DeviceIdType.MESH)
                cp.start()
                copies.append(cp)
        for cp in copies:
            cp.wait()

    any_spec = pl.BlockSpec(memory_space=pl.ANY)
    return pl.pallas_call(
        body, name=name, in_specs=[any_spec] * n, out_specs=[any_spec] * n,
        out_shape=[jax.ShapeDtypeStruct(a.shape, a.dtype) for a in lands], input_output_aliases={k: k for k in range(n)},
        scratch_shapes=[pltpu.SemaphoreType.DMA((n, 3)), pltpu.SemaphoreType.DMA((n, 3))],
    )(*lands)


def _adam(w, terms, m, v, *, name):
    n_layers, r, c = w.shape
    tr = min(r, 128)
    assert r % tr == 0 and len(terms) == n_layers
    steps = r // tr

    def body(w_ref, *rest):
        t_refs, (m_ref, v_ref, g_out, d_out, m_out, v_out) = rest[:n_layers], rest[n_layers:]
        for layer, t_ref in enumerate(t_refs):
            @pl.when(pl.program_id(0) == layer)
            def _(t_ref=t_ref):
                g = t_ref[0].astype(F32)
                for s in range(1, t_ref.shape[0]):
                    g = g + t_ref[s].astype(F32)
                m1 = ADAM_B1 * m_ref[...] + (1.0 - ADAM_B1) * g
                v1 = ADAM_B2 * v_ref[...] + (1.0 - ADAM_B2) * jnp.square(g)
                m_hat = m1 / (1.0 - ADAM_B1 ** ADAM_STEP)
                v_hat = v1 / (1.0 - ADAM_B2 ** ADAM_STEP)
                g_out[...] = g
                d_out[...] = -ADAM_LR * (m_hat / (jnp.sqrt(v_hat) + ADAM_EPS) + ADAM_WD * w_ref[...])
                m_out[...] = m1
                v_out[...] = v1

    def term_spec(layer, t):
        return pl.BlockSpec((t.shape[0], tr, c),
                            lambda l, i: (0, jnp.where(l == layer, i, jnp.where(l < layer, 0, steps - 1)), 0))

    spec = pl.BlockSpec((None, tr, c), lambda l, i: (l, i, 0))
    out = jax.ShapeDtypeStruct(w.shape, F32)
    return pl.pallas_call(
        body, name=name, grid=(n_layers, steps),
        in_specs=[spec] + [term_spec(layer, t) for layer, t in enumerate(terms)] + [spec, spec], out_specs=[spec] * 4,
        out_shape=[out] * 4, compiler_params=_params("arbitrary", "arbitrary"),
    )(w, *terms, m, v)


def _sum_terms(terms, *, name):
    n, _, p = terms.shape

    def body(t_ref, o_ref):
        acc = t_ref[0]
        for s in range(1, n):
            acc = acc + t_ref[s]
        o_ref[...] = acc

    return pl.pallas_call(body, name=name, out_shape=jax.ShapeDtypeStruct((1, p), F32))(terms)


def _lb_logits_grad(dlb, logits, *, name):
    def body(dlb_ref, l_ref, o_ref):
        lb = _lower_bound(l_ref[...])
        d0 = dlb_ref[...] * lb * (1.0 - lb)
        o_ref[...] = jnp.concatenate([d0, -d0], axis=0)

    return pl.pallas_call(body, name=name, out_shape=jax.ShapeDtypeStruct(logits.shape, F32))(dlb, logits)


def _silu_grad(z):
    sg = _sigmoid(z)
    return sg * (1.0 + z * (1.0 - sg))


def _head_norm_gate(o, zg, gn):
    outs = []
    for h in range(HGRN_HEADS):
        sl = slice(h * LANES, (h + 1) * LANES)
        zg_h = zg[:, sl]
        outs.append(_rms(o[:, sl], gn) * (zg_h * _sigmoid(zg_h)))
    return (jnp.concatenate(outs, axis=1),)


def _head_norm_gate_bwd(o, zg, dm, gn):
    do_parts, dzg_parts, dgn = [], [], jnp.zeros((1, LANES), F32)
    for h in range(HGRN_HEADS):
        sl = slice(h * LANES, (h + 1) * LANES)
        o_h, zg_h, dm_h = o[:, sl], zg[:, sl], dm[:, sl]
        gate = zg_h * _sigmoid(zg_h)
        do_h, dgn_h = _rms_bwd(o_h, gn, dm_h * gate)
        dgn = dgn + dgn_h
        do_parts.append(do_h)
        dzg_parts.append(dm_h * _rms(o_h, gn) * _silu_grad(zg_h))
    return jnp.concatenate(do_parts, axis=1), jnp.concatenate(dzg_parts, axis=1), dgn


def _rope_slabs(x, t_c, t_s1, t_s2, transpose):
    fn = _rope_t if transpose else _rope
    return jnp.concatenate(
        [fn(x[:, h * LANES:(h + 1) * LANES], t_c, t_s1, t_s2) for h in range(x.shape[1] // LANES)], axis=1)


def _loss_head(h, tgt, w):
    d = h.shape[1]
    r = lax.rsqrt(jnp.mean(h * h, axis=-1, keepdims=True) + EPS)
    xh = h * r
    err = xh * w - tgt
    loss = 0.5 * jnp.sum(jnp.mean(err * err, axis=-1, keepdims=True), axis=0, keepdims=True)
    dy = err / d
    dxh = dy * w
    dh = r * (dxh - xh * jnp.mean(dxh * xh, axis=-1, keepdims=True))
    return dh, dh, jnp.sum(dy * xh, axis=0, keepdims=True), jnp.broadcast_to(loss, (1, LANES))


def _mlp_fwd(h, norm, w_up, w_down, tag, loss_head=None):
    d = h.shape[1]

    def up(x, g, wu):
        x_n = _rms(x, g).astype(BF16)
        return x_n, jnp.concatenate([jnp.square(jnp.maximum(_dot(x_n, wu[j], _NN), 0.0)) for j in range(wu.shape[0])],
                                    axis=1)

    xn, act = _rowcall(up, [h], [norm, w_up], [(d, BF16), (w_up.shape[0] * w_up.shape[2], BF16)], [], tr=512,
                       name=f"{tag}_up")
    if callable(w_down):
        w_down = w_down(act)
    if loss_head is None:
        return _mm(act, w_down, mode="nn", add=h, name=f"{tag}_down"), (h, xn, act)
    tgt, final_norm = loss_head

    def down_and_loss(a, res, t, wd, g):
        return _loss_head(res + _dot(a, wd, _NN), t, g)

    return _rowcall(down_and_loss, [act, h, tgt], [w_down, final_norm], [(d, F32), (d, BF16)], [d, LANES],
                    name=f"{tag}_down_loss"), (h, xn, act)


def _mlp_bwd(dh_out, dh_out_bf, saved, norm, w_up, w_down, tag, after=None):
    h, xn, act = saved
    d = h.shape[1]
    du = _mm(dh_out_bf, w_down, mode="nt", relu2_of=act, out_dtype=BF16, after=after, name=f"{tag}_bwd_du")
    dw_down = _mm(act, dh_out_bf, mode="tn", name=f"{tag}_bwd_wdown")
    dw_up = _mm(xn, du, mode="tn", col_shards=w_up.shape[0], name=f"{tag}_bwd_wup")

    def up_norm_bwd(x, d_u, dres, g, wu):
        cols = wu.shape[2]
        dxn = _dot(d_u[:, :cols], wu[0], _NT)
        for j in range(1, wu.shape[0]):
            dxn = dxn + _dot(d_u[:, j * cols:(j + 1) * cols], wu[j], _NT)
        dx, dw = _rms_bwd(x, g, dxn)
        return dx + dres, dx + dres, dw

    dh, dh_bf, dnorm = _rowcall(up_norm_bwd, [h, du, dh_out], [norm, w_up], [(d, F32), (d, BF16)], [d], tr=512,
                                name=f"{tag}_bwd_dxn")
    return dh, dh_bf, dnorm, dw_up, dw_down


def _row_major(g):
    return g.reshape(g.shape[0] * g.shape[1], g.shape[2])


def _col_major(g):
    return jnp.transpose(g, (1, 0, 2)).reshape(g.shape[1], g.shape[0] * g.shape[2])


def _col_terms(dw):
    k, n = dw.shape
    return jnp.transpose(dw.reshape(k, N_DEV, n // N_DEV), (1, 0, 2))


def _row_terms(dw):
    return dw.reshape(N_DEV, dw.shape[0] // N_DEV, dw.shape[1])


def kernel(x, hgrn_norm, hgrn_w_q, hgrn_w_f, hgrn_w_i, hgrn_w_g, hgrn_g_norm, hgrn_w_o, hgrn_lb_logits, mla_norm, mla_w_dq, mla_q_norm, mla_w_uq, mla_w_o, kv_in_norm, kv_w_dkv, kv_norm, kv_w_uk, kv_w_uv, mlp_norm, mlp_w_up, mlp_w_down, final_norm, loss_target, m_hgrn_norm, m_hgrn_w_q, m_hgrn_w_f, m_hgrn_w_i, m_hgrn_w_g, m_hgrn_g_norm, m_hgrn_w_o, m_hgrn_lb_logits, m_mla_norm, m_mla_w_dq, m_mla_q_norm, m_mla_w_uq, m_mla_w_o, m_kv_in_norm, m_kv_w_dkv, m_kv_norm, m_kv_w_uk, m_kv_w_uv, m_mlp_norm, m_mlp_w_up, m_mlp_w_down, m_final_norm, v_hgrn_norm, v_hgrn_w_q, v_hgrn_w_f, v_hgrn_w_i, v_hgrn_w_g, v_hgrn_g_norm, v_hgrn_w_o, v_hgrn_lb_logits, v_mla_norm, v_mla_w_dq, v_mla_q_norm, v_mla_w_uq, v_mla_w_o, v_kv_in_norm, v_kv_w_dkv, v_kv_norm, v_kv_w_uk, v_kv_w_uv, v_mlp_norm, v_mlp_w_up, v_mlp_w_down, v_final_norm):
    given = dict(locals())
    weight_names = ["hgrn_norm", "hgrn_w_q", "hgrn_w_f", "hgrn_w_i", "hgrn_w_g", "hgrn_g_norm", "hgrn_w_o",
                    "hgrn_lb_logits", "mla_norm", "mla_w_dq", "mla_q_norm", "mla_w_uq", "mla_w_o", "kv_in_norm",
                    "kv_w_dkv", "kv_norm", "kv_w_uk", "kv_w_uv", "mlp_norm", "mlp_w_up", "mlp_w_down", "final_norm"]
    me = 4 * lax.axis_index("x") + 2 * lax.axis_index("y") + lax.axis_index("c")
    xs, tgt = x[0], loss_target[0]
    seq, d_model = xs.shape
    n_heads, hd = MLA_HEADS, LANES

    big_local = {
        "hgrn_w_q": hgrn_w_q[0], "hgrn_w_f": hgrn_w_f[0], "hgrn_w_i": hgrn_w_i[0], "hgrn_w_g": hgrn_w_g[0],
        "hgrn_w_o": hgrn_w_o[0], "mla_w_dq": mla_w_dq[0], "mla_w_uq": mla_w_uq[0], "mla_w_o": mla_w_o[0],
        "kv_w_dkv": kv_w_dkv, "kv_w_uk": kv_w_uk, "kv_w_uv": kv_w_uv,
        "mlp_w_up0": mlp_w_up[0], "mlp_w_up1": mlp_w_up[1], "mlp_w_down0": mlp_w_down[0], "mlp_w_down1": mlp_w_down[1],
    }
    big_names = list(big_local)
    col_sharded = {"mla_w_uq", "kv_w_uk", "kv_w_uv"}
    shard_major = {"mlp_w_up0", "mlp_w_up1"}
    vec_local = jnp.concatenate([hgrn_norm, hgrn_lb_logits], axis=0)
    first_names = ["hgrn_w_q", "hgrn_w_f", "hgrn_w_i"]
    proj_names = first_names + ["hgrn_w_g"]
    later_names = {"hgrn_o": ["hgrn_w_g", "hgrn_w_o"], "up0": ["mlp_w_up0"], "down0": ["mlp_w_down0"],
                   "mla": ["kv_w_dkv", "kv_w_uk", "kv_w_uv", "mla_w_dq", "mla_w_uq", "mla_w_o"],
                   "mlp1": ["mlp_w_up1", "mlp_w_down1"]}

    def unshard(names, arrays):
        return {k: (a if k in shard_major else _col_major(a) if k in col_sharded else _row_major(a))
                for k, a in zip(names, arrays)}

    two_level = {"down0", "mla"}
    first_state, token = _exchange_start([big_local[k].astype(BF16) for k in first_names] + [vec_local], scatter=False,
                                         chip_level=True, name="gather_first_start")
    gather_state = {}
    for tag, names in later_names.items():
        gather_state[tag], token = _exchange_start([big_local[k].astype(BF16) for k in names], scatter=False,
                                                   chip_level=tag in two_level, after=token, name=f"gather_{tag}_start")

    def gather_wait(tag, after):
        landed = _exchange_wait(gather_state[tag], after, name=f"gather_{tag}_wait")
        if tag in two_level:
            landed = _chip_forward(landed, name=f"gather_{tag}_forward")
        w.update(unshard(later_names[tag], landed))
        return [w[k] for k in later_names[tag]]

    gathered = _chip_forward(_exchange_wait(first_state, token, name="gather_first_wait"), name="gather_first_forward")
    w = unshard(first_names, gathered[:-1])
    vec_full = jnp.transpose(gathered[-1], (1, 0, 2)).reshape(3, d_model)
    hgrn_norm_full, lb_logits_full = vec_full[0:1], vec_full[1:3]
    t_c, t_s1, t_s2 = _rope_tables(seq)
    kv_lora = kv_w_uk.shape[0]

    def hgrn_proj(a, g, *weights):
        xn = _rms(a, g).astype(BF16)
        return (xn, *[_dot(xn, wt, _NN) for wt in weights])

    xn0, zq, zf, zi = _rowcall(hgrn_proj, [xs], [hgrn_norm_full] + [w[k] for k in first_names],
                               [(d_model, BF16)] + [(d_model, F32)] * 3, [], tr=512, name="hgrn_proj")
    o_rec, states = _hgrn_fwd(zq, zf, zi, lb_logits_full, name="hgrn_fwd")
    gather_wait("hgrn_o", o_rec)

    def gate_out(o, x_n, res, gn, wg, wo):
        z = _dot(x_n, wg, _NN)
        m = _head_norm_gate(o, z, gn)[0].astype(BF16)
        return z, m, res + _dot(m, wo, _NN)

    zg, mixed, h1 = _rowcall(gate_out, [o_rec, xn0, xs], [hgrn_g_norm, w["hgrn_w_g"], w["hgrn_w_o"]],
                             [(d_model, F32), (d_model, BF16), (d_model, F32)], [], name="hgrn_gate_out")
    h2, mlp0_saved = _mlp_fwd(h1, mlp_norm[0:1], gather_wait("up0", h1)[0], lambda act: gather_wait("down0", act)[0],
                              "mlp0")
    gather_wait("mla", h2)
    w_uq3 = w["mla_w_uq"].reshape(-1, n_heads, MLA_NOPE + MLA_ROPE)
    w_uq_nope = w_uq3[:, :, :MLA_NOPE].reshape(-1, n_heads * hd)
    w_uq_rope = jnp.pad(w_uq3[:, :, MLA_NOPE:], ((0, 0), (0, 0), (0, hd - MLA_ROPE))).reshape(-1, n_heads * hd)
    w_dkv_pad = jnp.pad(w["kv_w_dkv"], ((0, 0), (0, kv_lora + hd - w["kv_w_dkv"].shape[1])))

    q_lora, qk_cols = w["mla_w_dq"].shape[1], n_heads * hd

    def mla_qkv(a, tc, ts1, ts2, g_kv_in, g_mla, g_q, g_kv, wdq, wn, wr, wdkv, wuk, wuv):
        h_n, x_n = _rms(a, g_kv_in).astype(BF16), _rms(a, g_mla).astype(BF16)
        cq = _dot(x_n, wdq, _NN)
        cq_n = _rms(cq, g_q).astype(BF16)
        q_nope = _dot(cq_n, wn, _NN) * Q_PRESCALE
        q_rope = _rope_slabs(_dot(cq_n, wr, _NN) * Q_PRESCALE, tc, ts1, ts2, False)
        c_all = _dot(h_n, wdkv, _NN)
        lat = _rms(c_all[:, :kv_lora], g_kv).astype(BF16)
        return (h_n, x_n, cq, cq_n, q_nope, q_rope, c_all, lat, _rope(c_all[:, kv_lora:], tc, ts1, ts2),
                _dot(lat, wuk, _NN), _dot(lat, wuv, _NN))

    hn, xn2, cq_pre, c_q, qn, qr, ckr, c_kv, kr, kn, vv = _rowcall(
        mla_qkv, [h2, t_c, t_s1, t_s2],
        [kv_in_norm[None, :], mla_norm, mla_q_norm, kv_norm[None, :], w["mla_w_dq"], w_uq_nope, w_uq_rope, w_dkv_pad,
         w["kv_w_uk"], w["kv_w_uv"]],
        [(d_model, BF16), (d_model, BF16), (q_lora, F32), (q_lora, BF16), (qk_cols, BF16), (qk_cols, BF16),
         (kv_lora + hd, F32), (kv_lora, BF16), (hd, BF16), (qk_cols, BF16), (qk_cols, BF16)], [], tr=512, name="mla_qkv")
    o_att, lse = _attn_fwd(qn, qr, kn, kr, vv, name="attn_fwd")
    h3 = _mm(o_att, w["mla_w_o"], mode="nn", add=h2, name="attn_out")
    gather_wait("mlp1", h3)
    (dh4, dh4_bf, g_final_norm, loss_part), mlp1_saved = _mlp_fwd(
        h3, mlp_norm[1:2], w["mlp_w_up1"], w["mlp_w_down1"], "mlp1", loss_head=(tgt, final_norm[None, :]))

    g = {}
    groups = {"mlp1": ["mlp_w_up1", "mlp_w_down1"],
              "mla": ["mla_w_o", "mla_w_uq", "mla_w_dq", "kv_w_uk", "kv_w_uv", "kv_w_dkv"],
              "mlp0": ["mlp_w_up0", "mlp_w_down0"],
              "hgrn_out": ["hgrn_w_o", "hgrn_w_g"],
              "hgrn_in": ["hgrn_w_q", "hgrn_w_f", "hgrn_w_i"]}
    scatter_state = {}

    def scatter_start(tag, after=None):
        scatter_state[tag], tok = _exchange_start(
            [g[k] if k in shard_major else (_col_terms if k in col_sharded else _row_terms)(g[k]) for k in groups[tag]],
            scatter=True, after=after,
            name=f"scatter_{tag}_start")
        return tok

    dh3, dh3_bf, g_mlp_norm1, g["mlp_w_up1"], g["mlp_w_down1"] = _mlp_bwd(
        dh4, dh4_bf, mlp1_saved, mlp_norm[1:2], w["mlp_w_up1"], w["mlp_w_down1"], "mlp1")
    def attn_out_bwd(dres, o, wo):
        d_o = _dot(dres, wo, _NT).astype(BF16)
        prod = d_o.astype(F32) * o.astype(F32)
        return d_o, jnp.concatenate([jnp.broadcast_to(jnp.sum(prod[:, h * hd:(h + 1) * hd], axis=1, keepdims=True),
                                                      (prod.shape[0], hd)) for h in range(n_heads)], axis=1)

    d_oatt, delta = _rowcall(attn_out_bwd, [dh3_bf, o_att], [w["mla_w_o"]], [(qk_cols, BF16), (qk_cols, F32)], [],
                             after=scatter_start("mlp1"), name="attn_out_bwd_x")
    g["mla_w_o"] = _mm(o_att, dh3_bf, mode="tn", name="attn_out_bwd_w")
    dqn, dqr, dkn, dvv, dkr = _attn_bwd(qn, qr, kn, kr, vv, d_oatt, lse, delta, name="attn_bwd")

    def q_path_bwd(cq, cq_n, x_n, d_qn, d_qr, tc, ts1, ts2, g_q, wdq, wn, wr):
        d_qn, d_qr = d_qn.astype(BF16), _rope_slabs(d_qr, tc, ts1, ts2, True).astype(BF16)
        d_cq, d_gq = _rms_bwd(cq, g_q, _dot(d_qn, wn, _NT) + _dot(d_qr, wr, _NT))
        d_cq = d_cq.astype(BF16)
        return _dot(d_cq, wdq, _NT), d_gq, _dot(x_n, d_cq, _TN), _dot(cq_n, d_qn, _TN), _dot(cq_n, d_qr, _TN)

    dxn2, g_q_norm, g_dq, g_uq_nope, g_uq_rope = _rowcall(
        q_path_bwd, [cq_pre, c_q, xn2, dqn, dqr, t_c, t_s1, t_s2], [mla_q_norm, w["mla_w_dq"], w_uq_nope, w_uq_rope],
        [(d_model, F32)], [q_lora, (d_model, q_lora), (q_lora, qk_cols), (q_lora, qk_cols)], tr=512, name="mla_q_bwd")
    g["mla_w_dq"] = g_dq.astype(GRAD_WIRE_DTYPE)
    g["mla_w_uq"] = jnp.concatenate([g_uq_nope.reshape(q_lora, n_heads, hd),
                                     g_uq_rope.reshape(q_lora, n_heads, hd)[:, :, :MLA_ROPE]],
                                    axis=2).reshape(q_lora, -1).astype(GRAD_WIRE_DTYPE)

    def kv_path_bwd(c_all, lat, h_n, d_kn, d_v, d_kr_heads, tc, ts1, ts2, a, d_xn2, dres,
                    g_kv, g_kv_in, g_mla, wdkv, wuk, wuv):
        d_lat, d_gkv = _rms_bwd(c_all[:, :kv_lora], g_kv, _dot(d_kn, wuk, _NT) + _dot(d_v, wuv, _NT))
        d_kr = d_kr_heads[:, :hd]
        for h in range(1, n_heads):
            d_kr = d_kr + d_kr_heads[:, h * hd:(h + 1) * hd]
        d_all = jnp.concatenate([d_lat, _rope_t(d_kr, tc, ts1, ts2)], axis=1).astype(BF16)
        dx1, d_gkv_in = _rms_bwd(a, g_kv_in, _dot(d_all, wdkv, _NT))
        dx2, d_gmla = _rms_bwd(a, g_mla, d_xn2)
        d_a = dx1 + dx2 + dres
        return (d_a, d_a, d_gkv, d_gkv_in, d_gmla, _dot(h_n, d_all, _TN), _dot(lat, d_kn, _TN), _dot(lat, d_v, _TN))

    dh2, dh2_bf, g_kv_norm, g_kv_in_norm, g_mla_norm, g_dkv, g_uk, g_uv = _rowcall(
        kv_path_bwd, [ckr, c_kv, hn, dkn, dvv, dkr, t_c, t_s1, t_s2, h2, dxn2, dh3],
        [kv_norm[None, :], kv_in_norm[None, :], mla_norm, w_dkv_pad, w["kv_w_uk"], w["kv_w_uv"]],
        [(d_model, F32), (d_model, BF16)],
        [kv_lora, d_model, d_model, (d_model, kv_lora + hd), (kv_lora, qk_cols), (kv_lora, qk_cols)], name="mla_kv_bwd")
    g["kv_w_dkv"] = g_dkv[:, :kv_w_dkv.shape[1]].astype(GRAD_WIRE_DTYPE)
    g["kv_w_uk"], g["kv_w_uv"] = g_uk.astype(GRAD_WIRE_DTYPE), g_uv.astype(GRAD_WIRE_DTYPE)
    dh1, dh1_bf, g_mlp_norm0, g["mlp_w_up0"], g["mlp_w_down0"] = _mlp_bwd(
        dh2, dh2_bf, mlp0_saved, mlp_norm[0:1], w["mlp_w_up0"], w["mlp_w_down0"], "mlp0", after=scatter_start("mla"))

    g["hgrn_w_o"] = _mm(mixed, dh1_bf, mode="tn", after=scatter_start("mlp0"), name="hgrn_out_bwd_w")
    do_rec, dzg, g_g_norm = _rowcall(
        lambda dres, o, z, wo, gn: _head_norm_gate_bwd(o, z, _dot(dres, wo, _NT), gn), [dh1_bf, o_rec, zg],
        [w["hgrn_w_o"], hgrn_g_norm], [(d_model, F32), (d_model, BF16)], [hd], name="hgrn_gate_out_bwd")
    g["hgrn_w_g"] = _mm(xn0, dzg, mode="tn", name="hgrn_w_g_bwd_w")
    dzq, dzf, dzi, g_lb = _hgrn_bwd(zq, zf, zi, lb_logits_full, states, do_rec, scatter_start("hgrn_out"),
                                    name="hgrn_bwd")
    for nm, dz in (("hgrn_w_q", dzq), ("hgrn_w_f", dzf), ("hgrn_w_i", dzi)):
        g[nm] = _mm(xn0, dz, mode="tn", name=f"{nm}_bwd_w")

    def hgrn_proj_bwd(a, dres, *rest):
        dzs, gw, weights = rest[:4], rest[4], rest[5:]
        dxn = _dot(dzs[0], weights[0], _NT)
        for dz, wt in zip(dzs[1:], weights[1:]):
            dxn = dxn + _dot(dz, wt, _NT)
        dx, dw = _rms_bwd(a, gw, dxn)
        return dx + dres, dw

    grad_x, g_hgrn_norm = _rowcall(hgrn_proj_bwd, [xs, dh1, dzq, dzf, dzi, dzg],
                                   [hgrn_norm_full] + [w[k] for k in proj_names], [(d_model, F32)], [d_model],
                                   tr=512, name="hgrn_proj_bwd")

    small_parts = [g_hgrn_norm, g_lb, g_g_norm, g_mla_norm, g_q_norm, g_kv_in_norm, g_kv_norm, g_mlp_norm0,
                   g_mlp_norm1, g_final_norm, loss_part]
    small_sizes = [p.shape[1] for p in small_parts]
    small_terms = _exchange([jnp.concatenate(small_parts, axis=1)], scatter=False, name="gather_small")[0]
    small_sum = _sum_terms(small_terms, name="sum_small")
    last = scatter_start("hgrn_in", after=small_sum)
    offs = [0]
    for sz in small_sizes:
        offs.append(offs[-1] + sz)
    (s_hgrn_norm, s_lb, s_g_norm, s_mla_norm, s_q_norm, s_kv_in_norm, s_kv_norm, s_mlp_norm0, s_mlp_norm1, s_final_norm,
     s_loss) = [small_sum[:, a:b] for a, b in zip(offs[:-1], offs[1:])]
    shard = hgrn_norm.shape[1]
    g_lb_logits = _lb_logits_grad(lax.dynamic_slice_in_dim(s_lb, me * shard, shard, axis=1), hgrn_lb_logits,
                                  name="lb_logits_grad")
    loss = s_loss[0, 0]

    res, layer_terms = {}, {}

    def update(k, term_list):
        shape = given[k].shape
        as_layers = (len(term_list), shape[-2], shape[-1])
        upd = _adam(given[k].reshape(as_layers), term_list, given["m_" + k].reshape(as_layers),
                    given["v_" + k].reshape(as_layers), name=f"adam_{k}")
        res[k] = [o.reshape(shape) for o in upd]
        return upd[0]

    for tag, names in groups.items():
        for k, t in zip(names, _exchange_wait(scatter_state[tag], last, name=f"scatter_{tag}_wait")):
            if k.startswith("mlp_w_"):
                layer_terms.setdefault(k[:-1], {})[int(k[-1])] = t
                if len(layer_terms[k[:-1]]) == 2:
                    last = update(k[:-1], [layer_terms[k[:-1]][0], layer_terms[k[:-1]][1]])
            else:
                last = update(k, [t])

    small_grads = {
        "hgrn_norm": lax.dynamic_slice_in_dim(s_hgrn_norm, me * shard, shard, axis=1),
        "hgrn_g_norm": s_g_norm, "hgrn_lb_logits": g_lb_logits, "mla_norm": s_mla_norm, "mla_q_norm": s_q_norm,
        "kv_in_norm": s_kv_in_norm, "kv_norm": s_kv_norm,
        "mlp_norm": jnp.concatenate([s_mlp_norm0, s_mlp_norm1], axis=0), "final_norm": s_final_norm,
    }
    small_names = list(small_grads)

    def flat(a):
        return a.reshape(1, -1)

    packed = [jnp.concatenate([flat(src[pre + k]) for k in small_names], axis=1)
              for src, pre in ((given, ""), (small_grads, ""), (given, "m_"), (given, "v_"))]
    small_out = _adam(packed[0][None], [packed[1][None]], packed[2][None], packed[3][None], name="adam_small")
    off = 0
    for k in small_names:
        size = given[k].size
        res[k] = [o[0, :, off:off + size].reshape(given[k].shape) for o in small_out]
        off += size

    outs = [loss, grad_x[None]]
    for i in range(4):
        outs += [res[k][i] for k in weight_names]
    return tuple(outs)
```

```python
import functools

import jax
import jax.numpy as jnp
from jax import lax
from jax.experimental import pallas as pl
from jax.experimental.pallas import tpu as pltpu

F32 = jnp.float32
BF16 = jnp.bfloat16

EPS = 1e-6
LANES = 128
N_DEV = 8
V7X_VMEM_LIMIT_BYTES = 56 << 20
MM_PIPELINE_BYTES = 30 << 20
MM_ROW_TILE = 512
GRAD_WIRE_DTYPE = BF16

HGRN_HEADS = 8
HGRN_CHUNK = 64
HGRN_SUB = 16
HGRN_HEADS_PER_STEP = 8
HGRN_CHUNKS_PER_STEP = 4
EXP_CLAMP = 80.0
MLA_HEADS = 16
MLA_NOPE = 128
MLA_ROPE = 64
ROPE_THETA = 10000.0
ATTN_SCALE = (MLA_NOPE + MLA_ROPE) ** -0.5

ADAM_LR = 0.001
ADAM_B1 = 0.9
ADAM_B2 = 0.999
ADAM_EPS = 1e-08
ADAM_WD = 0.01
ADAM_STEP = 10

_NN = ((1,), (0,))
_NT = ((1,), (1,))
_TN = ((0,), (0,))


def _params(*sem):
    return pltpu.CompilerParams(dimension_semantics=sem, vmem_limit_bytes=V7X_VMEM_LIMIT_BYTES)


def _dot(a, b, dims):
    return lax.dot_general(a.astype(BF16), b.astype(BF16), (dims, ((), ())), preferred_element_type=F32)


def _dot_f32(a, b, dims=_NN):
    return lax.dot_general(a, b, (dims, ((), ())), precision=lax.Precision.HIGH, preferred_element_type=F32)


def _sigmoid(x):
    return 1.0 / (1.0 + jnp.exp(-x))


def _rms(x, w):
    r = lax.rsqrt(jnp.mean(x * x, axis=-1, keepdims=True) + EPS)
    return x * r * w


def _rms_bwd(x, w, dy):
    r = lax.rsqrt(jnp.mean(x * x, axis=-1, keepdims=True) + EPS)
    xh = x * r
    dw = jnp.sum(dy * xh, axis=0, keepdims=True)
    dxh = dy * w
    dx = r * (dxh - xh * jnp.mean(dxh * xh, axis=-1, keepdims=True))
    return dx, dw


def _mm_tiles(m, n, k, a_bytes, b_bytes, out_tile_bytes):
    tm = min(m, MM_ROW_TILE)
    for tn in (n, 2048, 1024, 512, 256, LANES):
        if tn <= n and n % tn == 0:
            if 2 * (tm * k * a_bytes + k * tn * b_bytes + tm * tn * out_tile_bytes) <= MM_PIPELINE_BYTES:
                return tm, tn
    return tm, min(n, LANES)


def _mm(a, b, *, mode, name, out_dtype=None, add=None, relu2_of=None, after=None, col_shards=None):
    if mode == "nn":
        (m, k), (k2, n) = a.shape, b.shape
    elif mode == "nt":
        (m, k), (n, k2) = a.shape, b.shape
    else:
        (k, m), (k2, n) = a.shape, b.shape
    assert k == k2, (name, a.shape, b.shape)
    if out_dtype is None:
        out_dtype = GRAD_WIRE_DTYPE if mode == "tn" else F32
    tile_bytes = sum(x.dtype.itemsize for x in (add, relu2_of) if x is not None) + jnp.dtype(out_dtype).itemsize
    tm, tn = _mm_tiles(m, n, k, a.dtype.itemsize, b.dtype.itemsize, tile_bytes)
    if col_shards is not None:
        assert add is None and relu2_of is None
        tn = n // col_shards
    assert m % tm == 0 and n % tn == 0, (name, m, n)
    dims = {"nn": _NN, "nt": _NT, "tn": _TN}[mode]
    a_spec = pl.BlockSpec((k, tm), lambda i, j: (0, i)) if mode == "tn" else pl.BlockSpec((tm, k), lambda i, j: (i, 0))
    b_spec = pl.BlockSpec((tn, k), lambda i, j: (j, 0)) if mode == "nt" else pl.BlockSpec((k, tn), lambda i, j: (0, j))
    o_spec = pl.BlockSpec((tm, tn), lambda i, j: (i, j))
    operands, in_specs = [a, b], [a_spec, b_spec]
    for extra in (add, relu2_of):
        if extra is not None:
            assert extra.shape == (m, n), (name, extra.shape)
            operands.append(extra)
            in_specs.append(o_spec)
    n_in = len(operands)
    if after is not None:
        operands.append(after)
        in_specs.append(pl.BlockSpec(memory_space=pl.ANY))
    out_shape = jax.ShapeDtypeStruct((m, n), out_dtype)
    if col_shards is not None:
        out_shape = jax.ShapeDtypeStruct((col_shards, m, tn), out_dtype)
        o_spec = pl.BlockSpec((None, tm, tn), lambda i, j: (j, i, 0))

    def body(*refs):
        acc = _dot(refs[0][...], refs[1][...], dims)
        extras, outs = refs[2:n_in], refs[len(operands):]
        if add is not None:
            acc = acc + extras[0][...]
        if relu2_of is not None:
            acc = acc * (2.0 * jnp.sqrt(extras[-1][...].astype(F32)))
        outs[0][...] = acc.astype(out_dtype)

    return pl.pallas_call(
        body, name=name, grid=(m // tm, n // tn), in_specs=in_specs, out_specs=o_spec, out_shape=out_shape,
        compiler_params=_params("parallel", "parallel"),
    )(*operands)


def _rowcall(fn, rows, consts, outs, accs, *, name, tr=256, after=None):
    s = rows[0].shape[0]
    tr = min(tr, s)
    assert s % tr == 0
    n_out = len(outs)
    accs = [(1, a) if isinstance(a, int) else a for a in accs]
    in_specs = [pl.BlockSpec((tr, r.shape[1]), lambda i: (i, 0)) for r in rows]
    in_specs += [pl.BlockSpec(c.shape, lambda i, nd=c.ndim: (0,) * nd) for c in consts]
    out_shape = [jax.ShapeDtypeStruct((s, w), dt) for w, dt in outs] + [jax.ShapeDtypeStruct(a, F32) for a in accs]
    out_specs = [pl.BlockSpec((tr, w), lambda i: (i, 0)) for w, _ in outs] + [pl.BlockSpec(a, lambda i: (0, 0)) for a in accs]
    n_in = len(rows) + len(consts)

    def body(*refs):
        res = fn(*[r[...] for r in refs[:n_in]])
        out_refs = refs[n_in + (after is not None):]
        for ref, val in zip(out_refs[:n_out], res[:n_out]):
            ref[...] = val.astype(ref.dtype)
        i = pl.program_id(0)
        for ref, val in zip(out_refs[n_out:], res[n_out:]):
            @pl.when(i == 0)
            def _(ref=ref, val=val):
                ref[...] = val

            @pl.when(i > 0)
            def _(ref=ref, val=val):
                ref[...] += val

    behind = [] if after is None else [after]
    return pl.pallas_call(
        body, name=name, grid=(s // tr,), in_specs=in_specs + [pl.BlockSpec(memory_space=pl.ANY)] * len(behind),
        out_specs=out_specs, out_shape=out_shape, compiler_params=_params("arbitrary" if accs else "parallel"),
    )(*rows, *consts, *behind)


def _rope_tables(seq):
    half = MLA_ROPE // 2
    inv_freq = ROPE_THETA ** (-jnp.arange(half, dtype=F32) / half)
    ang = jnp.arange(seq, dtype=F32)[:, None] * inv_freq[None, :]
    cos, sin, zero = jnp.cos(ang), jnp.sin(ang), jnp.zeros((seq, half), F32)
    t_c = jnp.concatenate([cos, cos, zero, zero], axis=1)
    t_s1 = jnp.concatenate([-sin, zero, zero, zero], axis=1)
    t_s2 = jnp.concatenate([zero, sin, zero, zero], axis=1)
    return t_c, t_s1, t_s2


def _rope(slab, t_c, t_s1, t_s2):
    return slab * t_c + pltpu.roll(slab, 96, 1) * t_s1 + pltpu.roll(slab, 32, 1) * t_s2


def _rope_t(d, t_c, t_s1, t_s2):
    return d * t_c + pltpu.roll(d * t_s1, 32, 1) + pltpu.roll(d * t_s2, 96, 1)


def _lower_bound(logits):
    l0, l1 = logits[0:1, :], logits[1:2, :]
    mx = jnp.maximum(l0, l1)
    e0, e1 = jnp.exp(l0 - mx), jnp.exp(l1 - mx)
    return e0 / (e0 + e1)


def _tri(n, lower):
    row = lax.broadcasted_iota(jnp.int32, (n, n), 0)
    col = lax.broadcasted_iota(jnp.int32, (n, n), 1)
    return (row >= col) if lower else (row <= col)


def _hgrn_fwd(zq, zf, zi, lb_logits, *, name):
    s, d = zq.shape
    h_n, c, hp, cps = d // LANES, HGRN_CHUNK, HGRN_HEADS_PER_STEP, HGRN_CHUNKS_PER_STEP
    nc = s // c

    def body(zq_ref, zf_ref, zi_ref, lb_ref, o_ref, st_ref, state_sc, b_sc):
        @pl.when(pl.program_id(1) == 0)
        def _():
            state_sc[...] = jnp.zeros_like(state_sc)

        lower = _tri(c, True)
        lower_f = lower.astype(F32)
        hs, pairs = range(hp), [(cc, hh) for cc in range(cps) for hh in range(hp)]
        sls = [slice(hh * LANES, (hh + 1) * LANES) for hh in hs]
        rws = [slice(cc * c, (cc + 1) * c) for cc in range(cps)]
        lb = [_lower_bound(lb_ref[:, sl]) for sl in sls]
        zq_v = {p: zq_ref[rws[p[0]], sls[p[1]]] for p in pairs}
        q = {p: zq_v[p] * _sigmoid(zq_v[p]) for p in pairs}
        f = {p: lb[p[1]] + (1.0 - lb[p[1]]) * _sigmoid(zf_ref[rws[p[0]], sls[p[1]]]) for p in pairs}
        k = {p: 1.0 - f[p] for p in pairs}
        v = {p: zi_ref[rws[p[0]], sls[p[1]]] for p in pairs}
        b = {p: _dot_f32(lower_f, jnp.log(f[p])) for p in pairs}
        for p in pairs:
            b_sc[p[0], p[1]] = b[p]
        qe = {p: q[p] * jnp.exp(b[p]) for p in pairs}
        scores = {p: [] for p in pairs}
        for i in range(c // HGRN_SUB):
            lo = i * HGRN_SUB
            for p in pairs:
                ref = b_sc[p[0], p[1], lo - 1:lo, :] if i > 0 else jnp.zeros((1, LANES), F32)
                qt = q[p][lo:lo + HGRN_SUB, :] * jnp.exp(b[p][lo:lo + HGRN_SUB, :] - ref)
                dec = jnp.exp(jnp.minimum(ref - b[p], EXP_CLAMP))
                scores[p].append(_dot(qt, k[p] * dec, _NT))
        o_intra = {p: _dot(jnp.where(lower, jnp.concatenate(scores[p], axis=0), 0.0), v[p], _NN) for p in pairs}
        bl = {p: b_sc[p[0], p[1], c - 1:c, :] for p in pairs}
        k_end = {p: k[p] * jnp.exp(bl[p] - b[p]) for p in pairs}
        state = [state_sc[hh] for hh in hs]
        for cc in range(cps):
            for hh in hs:
                st_ref[hh, cc] = state[hh]
                o_ref[rws[cc], sls[hh]] = _dot(qe[cc, hh], state[hh], _NT) + o_intra[cc, hh]
            state = [state[hh] * jnp.exp(bl[cc, hh]) + _dot(v[cc, hh], k_end[cc, hh], _TN) for hh in hs]
        for hh in hs:
            state_sc[hh] = state[hh]

    tile = pl.BlockSpec((cps * c, hp * LANES), lambda h, i: (i, h))
    return pl.pallas_call(
        body, name=name, grid=(h_n // hp, nc // cps),
        in_specs=[tile, tile, tile, pl.BlockSpec((2, hp * LANES), lambda h, i: (0, h))],
        out_specs=[tile, pl.BlockSpec((hp, cps, LANES, LANES), lambda h, i: (h, i, 0, 0))],
        out_shape=[jax.ShapeDtypeStruct((s, d), F32), jax.ShapeDtypeStruct((h_n, nc, LANES, LANES), F32)],
        scratch_shapes=[pltpu.VMEM((hp, LANES, LANES), F32), pltpu.VMEM((cps, hp, c, LANES), F32)],
        compiler_params=_params("parallel", "arbitrary"),
    )(zq, zf, zi, lb_logits)


def _hgrn_bwd(zq, zf, zi, lb_logits, states, do, after, *, name):
    s, d = zq.shape
    h_n, c, hp, cps = d // LANES, HGRN_CHUNK, HGRN_HEADS_PER_STEP, HGRN_CHUNKS_PER_STEP
    nc = s // c
    n_steps = nc // cps

    def body(zq_ref, zf_ref, zi_ref, lb_ref, st_ref, do_ref, _, dzq_ref, dzf_ref, dzi_ref, dlb_ref, dstate_sc, b_sc):
        @pl.when(pl.program_id(1) == 0)
        def _():
            dstate_sc[...] = jnp.zeros_like(dstate_sc)
            dlb_ref[...] = jnp.zeros_like(dlb_ref)

        lower, upper = _tri(c, True), _tri(c, False).astype(F32)
        lower_f = lower.astype(F32)
        last_row = lax.broadcasted_iota(jnp.int32, (c, LANES), 0) == c - 1
        hs, pairs = range(hp), [(cc, hh) for cc in range(cps) for hh in range(hp)]
        sls = [slice(hh * LANES, (hh + 1) * LANES) for hh in hs]
        rws = [slice(cc * c, (cc + 1) * c) for cc in range(cps)]
        lb = [_lower_bound(lb_ref[:, sl]) for sl in sls]
        zq_v = {p: zq_ref[rws[p[0]], sls[p[1]]] for p in pairs}
        sq = {p: _sigmoid(zq_v[p]) for p in pairs}
        q = {p: zq_v[p] * sq[p] for p in pairs}
        sf = {p: _sigmoid(zf_ref[rws[p[0]], sls[p[1]]]) for p in pairs}
        f = {p: lb[p[1]] + (1.0 - lb[p[1]]) * sf[p] for p in pairs}
        k = {p: 1.0 - f[p] for p in pairs}
        v = {p: zi_ref[rws[p[0]], sls[p[1]]] for p in pairs}
        d_o = {p: do_ref[rws[p[0]], sls[p[1]]] for p in pairs}
        b = {p: _dot_f32(lower_f, jnp.log(f[p])) for p in pairs}
        s0t = {p: st_ref[p[1], p[0]] for p in pairs}
        for p in pairs:
            b_sc[p[0], p[1]] = b[p]
        bl = {p: b_sc[p[0], p[1], c - 1:c, :] for p in pairs}
        eb = {p: jnp.exp(b[p]) for p in pairs}
        ebl = {p: jnp.exp(bl[p]) for p in pairs}
        dec_end = {p: jnp.exp(bl[p] - b[p]) for p in pairs}
        da = {p: jnp.where(lower, _dot(d_o[p], v[p], _NT), 0.0) for p in pairs}
        dq = {p: _dot(d_o[p], s0t[p], _NN) * eb[p] for p in pairs}
        dstate_in = {p: _dot(d_o[p], q[p] * eb[p], _TN) for p in pairs}
        dk_intra = {p: jnp.zeros((c, LANES), F32) for p in pairs}
        scores, dq_blocks = {p: [] for p in pairs}, {p: [] for p in pairs}
        for i in range(c // HGRN_SUB):
            lo = i * HGRN_SUB
            for p in pairs:
                ref = b_sc[p[0], p[1], lo - 1:lo, :] if i > 0 else jnp.zeros((1, LANES), F32)
                grow = jnp.exp(b[p][lo:lo + HGRN_SUB, :] - ref)
                qt = q[p][lo:lo + HGRN_SUB, :] * grow
                dec = jnp.exp(jnp.minimum(ref - b[p], EXP_CLAMP))
                kd = k[p] * dec
                scores[p].append(_dot(qt, kd, _NT))
                da_i = da[p][lo:lo + HGRN_SUB, :]
                dq_blocks[p].append(_dot_f32(da_i, kd, _NN) * grow)
                dk_intra[p] = dk_intra[p] + _dot_f32(da_i, qt, _TN) * dec
        dv_intra = {p: _dot(jnp.where(lower, jnp.concatenate(scores[p], axis=0), 0.0), d_o[p], _TN) for p in pairs}
        dq = {p: dq[p] + jnp.concatenate(dq_blocks[p], axis=0) for p in pairs}
        q_dq = {p: q[p] * dq[p] for p in pairs}
        for p in pairs:
            dzq_ref[rws[p[0]], sls[p[1]]] = (dq[p] * sq[p] * (1.0 + zq_v[p] * (1.0 - sq[p]))).astype(BF16)
        dstate = [dstate_sc[hh] for hh in hs]
        for cc in reversed(range(cps)):
            ps = [(cc, hh) for hh in hs]
            dk_state = [_dot(v[p], dstate[p[1]], _NN) * dec_end[p] for p in ps]
            dv = [dv_intra[p] + _dot(k[p] * dec_end[p], dstate[p[1]], _NT) for p in ps]
            dk = [dk_intra[p] + dk_state[p[1]] for p in ps]
            db_last = [jnp.sum(k[p] * dk_state[p[1]], axis=0, keepdims=True)
                       + ebl[p] * jnp.sum(s0t[p] * dstate[p[1]], axis=0, keepdims=True) for p in ps]
            db = [q_dq[p] - k[p] * dk[p[1]] + jnp.where(last_row, db_last[p[1]], 0.0) for p in ps]
            df = [_dot_f32(upper, db[p[1]]) / f[p] - dk[p[1]] for p in ps]
            for p in ps:
                hh = p[1]
                dzf_ref[rws[cc], sls[hh]] = (df[hh] * (1.0 - lb[hh]) * sf[p] * (1.0 - sf[p])).astype(BF16)
                dlb_ref[:, sls[hh]] += jnp.sum(df[hh] * (1.0 - sf[p]), axis=0, keepdims=True)
                dzi_ref[rws[cc], sls[hh]] = dv[hh].astype(BF16)
            dstate = [dstate[p[1]] * ebl[p] + dstate_in[p] for p in ps]
        for hh in hs:
            dstate_sc[hh] = dstate[hh]

    tile = pl.BlockSpec((cps * c, hp * LANES), lambda h, i: (n_steps - 1 - i, h))
    out = jax.ShapeDtypeStruct((s, d), BF16)
    return pl.pallas_call(
        body, name=name, grid=(h_n // hp, n_steps),
        in_specs=[tile, tile, tile, pl.BlockSpec((2, hp * LANES), lambda h, i: (0, h)),
                  pl.BlockSpec((hp, cps, LANES, LANES), lambda h, i: (h, n_steps - 1 - i, 0, 0)), tile,
                  pl.BlockSpec(memory_space=pl.ANY)],
        out_specs=[tile, tile, tile, pl.BlockSpec((1, hp * LANES), lambda h, i: (0, h))],
        out_shape=[out, out, out, jax.ShapeDtypeStruct((1, d), F32)],
        scratch_shapes=[pltpu.VMEM((hp, LANES, LANES), F32), pltpu.VMEM((cps, hp, c, LANES), F32)],
        compiler_params=_params("parallel", "arbitrary"),
    )(zq, zf, zi, lb_logits, states, do, after)


ATTN_SUB_ROWS = 256
LOG2E = 1.4426950408889634
LN2 = 0.6931471805599453
Q_PRESCALE = ATTN_SCALE * LOG2E


def _attn_tile(s):
    return min(1024, max(128, s // 2))


def _causal_pairs(n, q_major):
    pairs = [(i, j) for i in range(n) for j in range(i + 1)] if q_major else [(i, j) for j in range(n) for i in range(j, n)]
    return jnp.asarray([p[0] for p in pairs], jnp.int32), jnp.asarray([p[1] for p in pairs], jnp.int32)


def _sub_scores(qn_ref, qr_ref, k, r, sub, t, diagonal):
    q = jnp.concatenate([qn_ref[r:r + sub, :], qr_ref[r:r + sub, :]], axis=1)
    if not diagonal:
        return q, _dot(q, k, _NT)
    cols = r + sub
    keep = lax.broadcasted_iota(jnp.int32, (sub, cols), 1) <= r + lax.broadcasted_iota(jnp.int32, (sub, cols), 0)
    return q, jnp.where(keep, _dot(q, k[:cols], _NT), -jnp.inf)


def _attn_fwd(qn, qr, kn, kr, v, *, name):
    s, t = qn.shape[0], _attn_tile(qn.shape[0])
    sub = min(t, ATTN_SUB_ROWS)
    q_blk, k_blk = _causal_pairs(s // t, True)

    def body(qi_ref, kj_ref, qn_ref, qr_ref, kn_ref, kr_ref, v_ref, o_ref, lse_ref, m_sc, l_sc, acc_sc):
        p_id = pl.program_id(1)
        i, j = qi_ref[p_id], kj_ref[p_id]

        @pl.when(j == 0)
        def _():
            m_sc[...] = jnp.full_like(m_sc, -jnp.inf)
            l_sc[...] = jnp.zeros_like(l_sc)
            acc_sc[...] = jnp.zeros_like(acc_sc)

        def update(diagonal):
            k = jnp.concatenate([kn_ref[...], kr_ref[...]], axis=1)
            v = v_ref[...]
            starts = list(range(0, t, sub))
            scs = [_sub_scores(qn_ref, qr_ref, k, r, sub, t, diagonal)[1] for r in starts]
            ps, alphas = [], []
            for r, sc in zip(starts, scs):
                m_prev = m_sc[r:r + sub, :]
                m_new = jnp.maximum(m_prev, jnp.max(sc, axis=1, keepdims=True))
                alpha = jnp.exp2(m_prev - m_new)
                p = jnp.exp2(sc - m_new[:, :1])
                l_sc[r:r + sub, :] = alpha * l_sc[r:r + sub, :] + jnp.sum(p, axis=1, keepdims=True)
                m_sc[r:r + sub, :] = m_new
                ps.append(p)
                alphas.append(alpha)
            for r, p, alpha in zip(starts, ps, alphas):
                acc_sc[r:r + sub, :] = alpha * acc_sc[r:r + sub, :] + _dot(p, v[:p.shape[1]], _NN)

        @pl.when(j < i)
        def _():
            update(False)

        @pl.when(j == i)
        def _():
            update(True)
            o_ref[...] = (acc_sc[...] / l_sc[...]).astype(BF16)
            lse_ref[...] = m_sc[...] + jnp.log(l_sc[...]) * LOG2E

    q_spec = pl.BlockSpec((t, LANES), lambda h, p, qi, kj: (qi[p], h))
    k_spec = pl.BlockSpec((t, LANES), lambda h, p, qi, kj: (kj[p], h))
    kr_spec = pl.BlockSpec((t, LANES), lambda h, p, qi, kj: (kj[p], 0))
    stat = pltpu.VMEM((t, LANES), F32)
    return pl.pallas_call(
        body, name=name,
        grid_spec=pltpu.PrefetchScalarGridSpec(
            num_scalar_prefetch=2, grid=(MLA_HEADS, q_blk.shape[0]),
            in_specs=[q_spec, q_spec, k_spec, kr_spec, k_spec], out_specs=[q_spec, q_spec],
            scratch_shapes=[stat, stat, stat]),
        out_shape=[jax.ShapeDtypeStruct(qn.shape, BF16), jax.ShapeDtypeStruct(qn.shape, F32)],
        compiler_params=_params("parallel", "arbitrary"),
    )(q_blk, k_blk, qn, qr, kn, kr, v)


def _attn_bwd(qn, qr, kn, kr, v, do, lse, delta, *, name):
    s, t = qn.shape[0], _attn_tile(qn.shape[0])
    n, sub = s // t, min(t, ATTN_SUB_ROWS)
    q_blk, k_blk = _causal_pairs(n, False)

    def body(qi_ref, kj_ref, qn_ref, qr_ref, kn_ref, kr_ref, v_ref, do_ref, lse_ref, delta_ref,
             dqn_ref, dqr_ref, dkn_ref, dv_ref, dkr_ref, dk_sc, dv_sc):
        p_id = pl.program_id(1)
        i, j = qi_ref[p_id], kj_ref[p_id]

        @pl.when(p_id == 0)
        def _():
            dqn_ref[...] = jnp.zeros_like(dqn_ref)
            dqr_ref[...] = jnp.zeros_like(dqr_ref)

        @pl.when(i == j)
        def _():
            dk_sc[...] = jnp.zeros_like(dk_sc)
            dv_sc[...] = jnp.zeros_like(dv_sc)

        def accumulate(diagonal):
            k = jnp.concatenate([kn_ref[...], kr_ref[...]], axis=1)
            v = v_ref[...]
            starts = list(range(0, t, sub))
            qs, d_os, scs, dps = [], [], [], []
            for r in starts:
                q, sc = _sub_scores(qn_ref, qr_ref, k, r, sub, t, diagonal)
                d_o = do_ref[r:r + sub, :]
                qs.append(q)
                d_os.append(d_o)
                scs.append(sc)
                dps.append(_dot(d_o, v[:sc.shape[1]], _NT))
            ps, dss = [], []
            for r, sc, dp in zip(starts, scs, dps):
                p = jnp.exp2(sc - lse_ref[r:r + sub, :][:, :1])
                ps.append(p.astype(BF16))
                dss.append((p * (dp - delta_ref[r:r + sub, :][:, :1])).astype(BF16))
            for r, q, d_o, p, ds in zip(starts, qs, d_os, ps, dss):
                cols = p.shape[1]
                dv_sc[:cols, :] += _dot(p, d_o, _TN)
                dk_sc[:cols, :] += _dot(ds, q, _TN)
                dq = _dot(ds, k[:cols], _NN) * ATTN_SCALE
                rows = pl.ds(pl.multiple_of(i * t + r, sub), sub)
                dqn_ref[rows, :] += dq[:, :LANES]
                dqr_ref[rows, :] += dq[:, LANES:]

        @pl.when(j < i)
        def _():
            accumulate(False)

        @pl.when(j == i)
        def _():
            accumulate(True)

        @pl.when(i == n - 1)
        def _():
            dkn_ref[...] = (dk_sc[:, :LANES] * LN2).astype(BF16)
            dkr_ref[...] = dk_sc[:, LANES:] * LN2
            dv_ref[...] = dv_sc[...].astype(BF16)

    q_spec = pl.BlockSpec((t, LANES), lambda h, p, qi, kj: (qi[p], h))
    k_spec = pl.BlockSpec((t, LANES), lambda h, p, qi, kj: (kj[p], h))
    kr_spec = pl.BlockSpec((t, LANES), lambda h, p, qi, kj: (kj[p], 0))
    head_spec = pl.BlockSpec((s, LANES), lambda h, p, qi, kj: (0, h))
    f32_out, bf16_out = jax.ShapeDtypeStruct(qn.shape, F32), jax.ShapeDtypeStruct(qn.shape, BF16)
    return pl.pallas_call(
        body, name=name,
        grid_spec=pltpu.PrefetchScalarGridSpec(
            num_scalar_prefetch=2, grid=(MLA_HEADS, q_blk.shape[0]),
            in_specs=[q_spec, q_spec, k_spec, kr_spec, k_spec, q_spec, q_spec, q_spec],
            out_specs=[head_spec, head_spec, k_spec, k_spec, k_spec],
            scratch_shapes=[pltpu.VMEM((t, 2 * LANES), F32), pltpu.VMEM((t, LANES), F32)]),
        out_shape=[f32_out, f32_out, bf16_out, bf16_out, f32_out],
        compiler_params=_params("parallel", "arbitrary"),
    )(q_blk, k_blk, qn, qr, kn, kr, v, do, lse, delta)


def _exchange(arrs, *, scatter, name):
    n = len(arrs)
    out_shape = [jax.ShapeDtypeStruct(a.shape if scatter else (N_DEV, *a.shape), a.dtype) for a in arrs]

    def body(*refs):
        ins, outs = refs[:n], refs[n:2 * n]
        send_sems, recv_sems, local_sems = refs[2 * n:]
        x, y, c = lax.axis_index("x"), lax.axis_index("y"), lax.axis_index("c")
        me = 4 * x + 2 * y + c
        copies = []
        for k in range(n):
            local = pltpu.make_async_copy(ins[k].at[me] if scatter else ins[k], outs[k].at[me], local_sems.at[k])
            local.start()
            copies.append(local)
            for d in range(1, N_DEV):
                px, py, pc = (x + (d >> 2)) % 2, (y + ((d >> 1) & 1)) % 2, (c + (d & 1)) % 2
                peer = 4 * px + 2 * py + pc
                remote = pltpu.make_async_remote_copy(
                    src_ref=ins[k].at[peer] if scatter else ins[k], dst_ref=outs[k].at[me],
                    send_sem=send_sems.at[k, d - 1], recv_sem=recv_sems.at[k, d - 1],
                    device_id=(px, py, pc), device_id_type=pl.DeviceIdType.MESH)
                remote.start()
                copies.append(remote)
        for cp in copies:
            cp.wait()

    any_spec = pl.BlockSpec(memory_space=pl.ANY)
    return pl.pallas_call(
        body, name=name, in_specs=[any_spec] * n, out_specs=[any_spec] * n, out_shape=out_shape,
        scratch_shapes=[pltpu.SemaphoreType.DMA((n, N_DEV - 1)), pltpu.SemaphoreType.DMA((n, N_DEV - 1)),
                        pltpu.SemaphoreType.DMA((n,))],
    )(*arrs)


def _peers(x, y, c):
    out = []
    for d in range(1, N_DEV):
        px, py, pc = (x + (d >> 2)) % 2, (y + ((d >> 1) & 1)) % 2, (c + (d & 1)) % 2
        out.append(((px, py, pc), 4 * px + 2 * py + pc))
    return out


CHIP_LEVEL_PEERS = (1, 2, 4, 6)


def _exchange_copies(ins, lands, send_sems, recv_sems, scatter, chip_level=False):
    x, y, c = lax.axis_index("x"), lax.axis_index("y"), lax.axis_index("c")
    me = 4 * x + 2 * y + c
    local, remote = [], []
    for k in range(len(ins)):
        local.append(pltpu.make_async_copy(ins[k].at[me] if scatter else ins[k], lands[k].at[me],
                                           recv_sems.at[k * N_DEV + N_DEV - 1]))
        for d, (coords, peer) in enumerate(_peers(x, y, c)):
            if chip_level and d + 1 not in CHIP_LEVEL_PEERS:
                continue
            remote.append(pltpu.make_async_remote_copy(
                src_ref=ins[k].at[peer] if scatter else ins[k], dst_ref=lands[k].at[me],
                send_sem=send_sems.at[k * N_DEV + d], recv_sem=recv_sems.at[k * N_DEV + d],
                device_id=coords, device_id_type=pl.DeviceIdType.MESH))
    return local, remote


def _exchange_start(arrs, *, scatter, name, after=None, chip_level=False):
    n = len(arrs)
    hbm = pl.BlockSpec(memory_space=pltpu.HBM)
    sem = pl.BlockSpec(memory_space=pltpu.SEMAPHORE)
    lands = [lax.empty(a.shape if scatter else (N_DEV, *a.shape), a.dtype) for a in arrs]

    def body(*refs):
        ins, land_refs = refs[:n], refs[n:2 * n]
        first_out = 2 * n + (after is not None)
        send_sems, recv_sems, token = refs[first_out], refs[first_out + 1], refs[-1]
        local, remote = _exchange_copies(ins, land_refs, send_sems, recv_sems, scatter, chip_level)
        for cp in local + remote:
            cp.start()
        token[...] = jnp.zeros_like(token)

    operands = [pltpu.with_memory_space_constraint(a, pltpu.HBM) for a in list(arrs) + lands]
    behind = [] if after is None else [after]
    res = pl.pallas_call(
        body, name=name,
        out_shape=(pltpu.SemaphoreType.DMA((n * N_DEV,)), pltpu.SemaphoreType.DMA((n * N_DEV,)),
                   *[pltpu.HBM(o.shape, o.dtype) for o in operands], jax.ShapeDtypeStruct((8, LANES), F32)),
        in_specs=[hbm] * (2 * n) + [pl.BlockSpec(memory_space=pl.ANY)] * len(behind),
        out_specs=(sem, sem, *[hbm] * (2 * n), pl.BlockSpec(memory_space=pltpu.VMEM)),
        input_output_aliases={i: 2 + i for i in range(2 * n)},
        compiler_params=pltpu.CompilerParams(has_side_effects=pltpu.SideEffectType.DATAFLOW_SIDE_EFFECTING),
    )(*operands, *behind)
    return (res[0], res[1], list(res[2:2 + n]), list(res[2 + n:2 + 2 * n]), scatter, chip_level), res[-1]


def _exchange_wait(state, after, *, name):
    send_sems, recv_sems, ins, lands, scatter, chip_level = state
    n = len(ins)
    hbm = pl.BlockSpec(memory_space=pltpu.HBM)
    sem = pl.BlockSpec(memory_space=pltpu.SEMAPHORE)

    def body(*refs):
        in_refs, land_refs = refs[:n], refs[n:2 * n]
        local, remote = _exchange_copies(in_refs, land_refs, refs[2 * n], refs[2 * n + 1], scatter, chip_level)
        for cp in local:
            cp.wait()
        for cp in remote:
            cp.wait_send()
            cp.wait_recv()

    res = pl.pallas_call(
        body, name=name, out_shape=tuple(pltpu.HBM(o.shape, o.dtype) for o in ins + lands),
        in_specs=[hbm] * (2 * n) + [sem, sem, pl.BlockSpec(memory_space=pl.ANY)], out_specs=tuple([hbm] * (2 * n)),
        input_output_aliases={i: i for i in range(2 * n)},
        compiler_params=pltpu.CompilerParams(has_side_effects=pltpu.SideEffectType.DATAFLOW_SIDE_EFFECTING),
    )(*ins, *lands, send_sems, recv_sems, after)
    return list(res[n:])


def _chip_forward(lands, *, name):
    n = len(lands)

    def body(*refs):
        ins, outs, send_sems, recv_sems = refs[:n], refs[n:2 * n], refs[2 * n], refs[2 * n + 1]
        x, y, c = lax.axis_index("x"), lax.axis_index("y"), lax.axis_index("c")
        copies = []
        for k in range(n):
            for j, (dx, dy) in enumerate(((0, 1), (1, 0), (1, 1))):
                held = 4 * ((x + dx) % 2) + 2 * ((y + dy) % 2) + c
                cp = pltpu.make_async_remote_copy(
                    src_ref=ins[k].at[held], dst_ref=outs[k].at[held], send_sem=send_sems.at[k, j],
                    recv_sem=recv_sems.at[k, j], device_id=(x, y, 1 - c), device_id_type=pl.DeviceIdType.MESH)
                cp.start()
                copies.append(cp)
        for cp in copies:
            cp.wait()

    any_spec = pl.BlockSpec(memory_space=pl.ANY)
    return pl.pallas_call(
        body, name=name, in_specs=[any_spec] * n, out_specs=[any_spec] * n,
        out_shape=[jax.ShapeDtypeStruct(a.shape, a.dtype) for a in lands], input_output_aliases={k: k for k in range(n)},
        scratch_shapes=[pltpu.SemaphoreType.DMA((n, 3)), pltpu.SemaphoreType.DMA((n, 3))],
    )(*lands)


def _adam(w, terms, m, v, *, name):
    n_layers, r, c = w.shape
    tr = min(r, 128)
    assert r % tr == 0 and len(terms) == n_layers
    steps = r // tr

    def body(w_ref, *rest):
        t_refs, (m_ref, v_ref, g_out, d_out, m_out, v_out) = rest[:n_layers], rest[n_layers:]
        for layer, t_ref in enumerate(t_refs):
            @pl.when(pl.program_id(0) == layer)
            def _(t_ref=t_ref):
                g = t_ref[0].astype(F32)
                for s in range(1, t_ref.shape[0]):
                    g = g + t_ref[s].astype(F32)
                m1 = ADAM_B1 * m_ref[...] + (1.0 - ADAM_B1) * g
                v1 = ADAM_B2 * v_ref[...] + (1.0 - ADAM_B2) * jnp.square(g)
                m_hat = m1 / (1.0 - ADAM_B1 ** ADAM_STEP)
                v_hat = v1 / (1.0 - ADAM_B2 ** ADAM_STEP)
                g_out[...] = g
                d_out[...] = -ADAM_LR * (m_hat / (jnp.sqrt(v_hat) + ADAM_EPS) + ADAM_WD * w_ref[...])
                m_out[...] = m1
                v_out[...] = v1

    def term_spec(layer, t):
        return pl.BlockSpec((t.shape[0], tr, c),
                            lambda l, i: (0, jnp.where(l == layer, i, jnp.where(l < layer, 0, steps - 1)), 0))

    spec = pl.BlockSpec((None, tr, c), lambda l, i: (l, i, 0))
    out = jax.ShapeDtypeStruct(w.shape, F32)
    return pl.pallas_call(
        body, name=name, grid=(n_layers, steps),
        in_specs=[spec] + [term_spec(layer, t) for layer, t in enumerate(terms)] + [spec, spec], out_specs=[spec] * 4,
        out_shape=[out] * 4, compiler_params=_params("arbitrary", "arbitrary"),
    )(w, *terms, m, v)


def _sum_terms(terms, *, name):
    n, _, p = terms.shape

    def body(t_ref, o_ref):
        acc = t_ref[0]
        for s in range(1, n):
            acc = acc + t_ref[s]
        o_ref[...] = acc

    return pl.pallas_call(body, name=name, out_shape=jax.ShapeDtypeStruct((1, p), F32))(terms)


def _lb_logits_grad(dlb, logits, *, name):
    def body(dlb_ref, l_ref, o_ref):
        lb = _lower_bound(l_ref[...])
        d0 = dlb_ref[...] * lb * (1.0 - lb)
        o_ref[...] = jnp.concatenate([d0, -d0], axis=0)

    return pl.pallas_call(body, name=name, out_shape=jax.ShapeDtypeStruct(logits.shape, F32))(dlb, logits)


def _silu_grad(z):
    sg = _sigmoid(z)
    return sg * (1.0 + z * (1.0 - sg))


def _head_norm_gate(o, zg, gn):
    outs = []
    for h in range(HGRN_HEADS):
        sl = slice(h * LANES, (h + 1) * LANES)
        zg_h = zg[:, sl]
        outs.append(_rms(o[:, sl], gn) * (zg_h * _sigmoid(zg_h)))
    return (jnp.concatenate(outs, axis=1),)


def _head_norm_gate_bwd(o, zg, dm, gn):
    do_parts, dzg_parts, dgn = [], [], jnp.zeros((1, LANES), F32)
    for h in range(HGRN_HEADS):
        sl = slice(h * LANES, (h + 1) * LANES)
        o_h, zg_h, dm_h = o[:, sl], zg[:, sl], dm[:, sl]
        gate = zg_h * _sigmoid(zg_h)
        do_h, dgn_h = _rms_bwd(o_h, gn, dm_h * gate)
        dgn = dgn + dgn_h
        do_parts.append(do_h)
        dzg_parts.append(dm_h * _rms(o_h, gn) * _silu_grad(zg_h))
    return jnp.concatenate(do_parts, axis=1), jnp.concatenate(dzg_parts, axis=1), dgn


def _rope_slabs(x, t_c, t_s1, t_s2, transpose):
    fn = _rope_t if transpose else _rope
    return jnp.concatenate(
        [fn(x[:, h * LANES:(h + 1) * LANES], t_c, t_s1, t_s2) for h in range(x.shape[1] // LANES)], axis=1)


def _loss_head(h, tgt, w):
    d = h.shape[1]
    r = lax.rsqrt(jnp.mean(h * h, axis=-1, keepdims=True) + EPS)
    xh = h * r
    err = xh * w - tgt
    loss = 0.5 * jnp.sum(jnp.mean(err * err, axis=-1, keepdims=True), axis=0, keepdims=True)
    dy = err / d
    dxh = dy * w
    dh = r * (dxh - xh * jnp.mean(dxh * xh, axis=-1, keepdims=True))
    return dh, dh, jnp.sum(dy * xh, axis=0, keepdims=True), jnp.broadcast_to(loss, (1, LANES))


def _mlp_fwd(h, norm, w_up, w_down, tag, loss_head=None):
    made = not hasattr(h, "shape")
    make_h, rows, consts, made_outs = h if made else ((lambda x: (x,)), [h], [], [])
    d = norm.shape[1]

    def up(*args):
        *made_vals, x = make_h(*args[:-2])
        x_n = _rms(x, args[-2]).astype(BF16)
        wu = args[-1]
        act_v = jnp.concatenate([jnp.square(jnp.maximum(_dot(x_n, wu[j], _NN), 0.0)) for j in range(wu.shape[0])], axis=1)
        return (*made_vals, *([x] if made else []), x_n, act_v)

    *others, xn, act = _rowcall(up, rows, consts + [norm, w_up],
                                made_outs + [(d, BF16), (w_up.shape[0] * w_up.shape[2], BF16)], [],
                                tr=256 if made else 512, name=f"{tag}_up")
    if made:
        h = others.pop()
    if callable(w_down):
        w_down = w_down(act)
    if loss_head is None:
        return _mm(act, w_down, mode="nn", add=h, name=f"{tag}_down"), (h, xn, act), others
    tgt, final_norm = loss_head

    def down_and_loss(a, res, t, wd, g):
        return _loss_head(res + _dot(a, wd, _NN), t, g)

    return _rowcall(down_and_loss, [act, h, tgt], [w_down, final_norm], [(d, F32), (d, BF16)], [d, LANES],
                    name=f"{tag}_down_loss"), (h, xn, act), others


def _mlp_bwd(dh_out, dh_out_bf, saved, norm, w_up, w_down, tag, after=None):
    h, xn, act = saved
    d = h.shape[1]
    du = _mm(dh_out_bf, w_down, mode="nt", relu2_of=act, out_dtype=BF16, after=after, name=f"{tag}_bwd_du")
    dw_down = _mm(act, dh_out_bf, mode="tn", name=f"{tag}_bwd_wdown")
    dw_up = _mm(xn, du, mode="tn", col_shards=w_up.shape[0], name=f"{tag}_bwd_wup")

    def up_norm_bwd(x, d_u, dres, g, wu):
        cols = wu.shape[2]
        dxn = _dot(d_u[:, :cols], wu[0], _NT)
        for j in range(1, wu.shape[0]):
            dxn = dxn + _dot(d_u[:, j * cols:(j + 1) * cols], wu[j], _NT)
        dx, dw = _rms_bwd(x, g, dxn)
        return dx + dres, dx + dres, dw

    dh, dh_bf, dnorm = _rowcall(up_norm_bwd, [h, du, dh_out], [norm, w_up], [(d, F32), (d, BF16)], [d], tr=512,
                                name=f"{tag}_bwd_dxn")
    return dh, dh_bf, dnorm, dw_up, dw_down


def _row_major(g):
    return g.reshape(g.shape[0] * g.shape[1], g.shape[2])


def _col_major(g):
    return jnp.transpose(g, (1, 0, 2)).reshape(g.shape[1], g.shape[0] * g.shape[2])


def _col_terms(dw):
    k, n = dw.shape
    return jnp.transpose(dw.reshape(k, N_DEV, n // N_DEV), (1, 0, 2))


def _row_terms(dw):
    return dw.reshape(N_DEV, dw.shape[0] // N_DEV, dw.shape[1])


def kernel(x, hgrn_norm, hgrn_w_q, hgrn_w_f, hgrn_w_i, hgrn_w_g, hgrn_g_norm, hgrn_w_o, hgrn_lb_logits, mla_norm, mla_w_dq, mla_q_norm, mla_w_uq, mla_w_o, kv_in_norm, kv_w_dkv, kv_norm, kv_w_uk, kv_w_uv, mlp_norm, mlp_w_up, mlp_w_down, final_norm, loss_target, m_hgrn_norm, m_hgrn_w_q, m_hgrn_w_f, m_hgrn_w_i, m_hgrn_w_g, m_hgrn_g_norm, m_hgrn_w_o, m_hgrn_lb_logits, m_mla_norm, m_mla_w_dq, m_mla_q_norm, m_mla_w_uq, m_mla_w_o, m_kv_in_norm, m_kv_w_dkv, m_kv_norm, m_kv_w_uk, m_kv_w_uv, m_mlp_norm, m_mlp_w_up, m_mlp_w_down, m_final_norm, v_hgrn_norm, v_hgrn_w_q, v_hgrn_w_f, v_hgrn_w_i, v_hgrn_w_g, v_hgrn_g_norm, v_hgrn_w_o, v_hgrn_lb_logits, v_mla_norm, v_mla_w_dq, v_mla_q_norm, v_mla_w_uq, v_mla_w_o, v_kv_in_norm, v_kv_w_dkv, v_kv_norm, v_kv_w_uk, v_kv_w_uv, v_mlp_norm, v_mlp_w_up, v_mlp_w_down, v_final_norm):
    given = dict(locals())
    weight_names = ["hgrn_norm", "hgrn_w_q", "hgrn_w_f", "hgrn_w_i", "hgrn_w_g", "hgrn_g_norm", "hgrn_w_o",
                    "hgrn_lb_logits", "mla_norm", "mla_w_dq", "mla_q_norm", "mla_w_uq", "mla_w_o", "kv_in_norm",
                    "kv_w_dkv", "kv_norm", "kv_w_uk", "kv_w_uv", "mlp_norm", "mlp_w_up", "mlp_w_down", "final_norm"]
    me = 4 * lax.axis_index("x") + 2 * lax.axis_index("y") + lax.axis_index("c")
    xs, tgt = x[0], loss_target[0]
    seq, d_model = xs.shape
    n_heads, hd = MLA_HEADS, LANES

    big_local = {
        "hgrn_w_q": hgrn_w_q[0], "hgrn_w_f": hgrn_w_f[0], "hgrn_w_i": hgrn_w_i[0], "hgrn_w_g": hgrn_w_g[0],
        "hgrn_w_o": hgrn_w_o[0], "mla_w_dq": mla_w_dq[0], "mla_w_uq": mla_w_uq[0], "mla_w_o": mla_w_o[0],
        "kv_w_dkv": kv_w_dkv, "kv_w_uk": kv_w_uk, "kv_w_uv": kv_w_uv,
        "mlp_w_up0": mlp_w_up[0], "mlp_w_up1": mlp_w_up[1], "mlp_w_down0": mlp_w_down[0], "mlp_w_down1": mlp_w_down[1],
    }
    big_names = list(big_local)
    col_sharded = {"mla_w_uq", "kv_w_uk", "kv_w_uv"}
    shard_major = {"mlp_w_up0", "mlp_w_up1"}
    vec_local = jnp.concatenate([hgrn_norm, hgrn_lb_logits], axis=0)
    first_names = ["hgrn_w_q", "hgrn_w_f", "hgrn_w_i"]
    proj_names = first_names + ["hgrn_w_g"]
    later_names = {"hgrn_o": ["hgrn_w_g", "hgrn_w_o"], "up0": ["mlp_w_up0"], "down0": ["mlp_w_down0"],
                   "mla": ["kv_w_dkv", "kv_w_uk", "kv_w_uv", "mla_w_dq", "mla_w_uq", "mla_w_o"],
                   "mlp1": ["mlp_w_up1", "mlp_w_down1"]}

    def unshard(names, arrays):
        return {k: (a if k in shard_major else _col_major(a) if k in col_sharded else _row_major(a))
                for k, a in zip(names, arrays)}

    two_level = {"down0", "mla"}
    first_state, token = _exchange_start([big_local[k].astype(BF16) for k in first_names] + [vec_local], scatter=False,
                                         chip_level=True, name="gather_first_start")
    gather_state = {}
    for tag, names in later_names.items():
        gather_state[tag], token = _exchange_start([big_local[k].astype(BF16) for k in names], scatter=False,
                                                   chip_level=tag in two_level, after=token, name=f"gather_{tag}_start")

    def gather_wait(tag, after):
        landed = _exchange_wait(gather_state[tag], after, name=f"gather_{tag}_wait")
        if tag in two_level:
            landed = _chip_forward(landed, name=f"gather_{tag}_forward")
        w.update(unshard(later_names[tag], landed))
        return [w[k] for k in later_names[tag]]

    gathered = _chip_forward(_exchange_wait(first_state, token, name="gather_first_wait"), name="gather_first_forward")
    w = unshard(first_names, gathered[:-1])
    vec_full = jnp.transpose(gathered[-1], (1, 0, 2)).reshape(3, d_model)
    hgrn_norm_full, lb_logits_full = vec_full[0:1], vec_full[1:3]
    t_c, t_s1, t_s2 = _rope_tables(seq)
    kv_lora = kv_w_uk.shape[0]

    def hgrn_proj(a, g, *weights):
        xn = _rms(a, g).astype(BF16)
        return (xn, *[_dot(xn, wt, _NN) for wt in weights])

    xn0, zq, zf, zi = _rowcall(hgrn_proj, [xs], [hgrn_norm_full] + [w[k] for k in first_names],
                               [(d_model, BF16)] + [(d_model, F32)] * 3, [], tr=512, name="hgrn_proj")
    o_rec, states = _hgrn_fwd(zq, zf, zi, lb_logits_full, name="hgrn_fwd")
    gather_wait("hgrn_o", o_rec)

    def gate_out(o, x_n, res, gn, wg, wo):
        z = _dot(x_n, wg, _NN)
        m = _head_norm_gate(o, z, gn)[0].astype(BF16)
        return z, m, res + _dot(m, wo, _NN)

    h2, mlp0_saved, (zg, mixed) = _mlp_fwd(
        (gate_out, [o_rec, xn0, xs], [hgrn_g_norm, w["hgrn_w_g"], w["hgrn_w_o"]],
         [(d_model, F32), (d_model, BF16), (d_model, F32)]),
        mlp_norm[0:1], gather_wait("up0", o_rec)[0], lambda act: gather_wait("down0", act)[0], "mlp0")
    gather_wait("mla", h2)
    w_uq3 = w["mla_w_uq"].reshape(-1, n_heads, MLA_NOPE + MLA_ROPE)
    w_uq_nope = w_uq3[:, :, :MLA_NOPE].reshape(-1, n_heads * hd)
    w_uq_rope = jnp.pad(w_uq3[:, :, MLA_NOPE:], ((0, 0), (0, 0), (0, hd - MLA_ROPE))).reshape(-1, n_heads * hd)
    w_dkv_pad = jnp.pad(w["kv_w_dkv"], ((0, 0), (0, kv_lora + hd - w["kv_w_dkv"].shape[1])))

    q_lora, qk_cols = w["mla_w_dq"].shape[1], n_heads * hd

    def mla_qkv(a, tc, ts1, ts2, g_kv_in, g_mla, g_q, g_kv, wdq, wn, wr, wdkv, wuk, wuv):
        h_n, x_n = _rms(a, g_kv_in).astype(BF16), _rms(a, g_mla).astype(BF16)
        cq = _dot(x_n, wdq, _NN)
        cq_n = _rms(cq, g_q).astype(BF16)
        q_nope = _dot(cq_n, wn, _NN) * Q_PRESCALE
        q_rope = _rope_slabs(_dot(cq_n, wr, _NN) * Q_PRESCALE, tc, ts1, ts2, False)
        c_all = _dot(h_n, wdkv, _NN)
        lat = _rms(c_all[:, :kv_lora], g_kv).astype(BF16)
        return (h_n, x_n, cq, cq_n, q_nope, q_rope, c_all, lat, _rope(c_all[:, kv_lora:], tc, ts1, ts2),
                _dot(lat, wuk, _NN), _dot(lat, wuv, _NN))

    hn, xn2, cq_pre, c_q, qn, qr, ckr, c_kv, kr, kn, vv = _rowcall(
        mla_qkv, [h2, t_c, t_s1, t_s2],
        [kv_in_norm[None, :], mla_norm, mla_q_norm, kv_norm[None, :], w["mla_w_dq"], w_uq_nope, w_uq_rope, w_dkv_pad,
         w["kv_w_uk"], w["kv_w_uv"]],
        [(d_model, BF16), (d_model, BF16), (q_lora, F32), (q_lora, BF16), (qk_cols, BF16), (qk_cols, BF16),
         (kv_lora + hd, F32), (kv_lora, BF16), (hd, BF16), (qk_cols, BF16), (qk_cols, BF16)], [], tr=512, name="mla_qkv")
    o_att, lse = _attn_fwd(qn, qr, kn, kr, vv, name="attn_fwd")
    gather_wait("mlp1", o_att)
    (dh4, dh4_bf, g_final_norm, loss_part), mlp1_saved, _ = _mlp_fwd(
        ((lambda o, res, wo: (res + _dot(o, wo, _NN),)), [o_att, h2], [w["mla_w_o"]], [(d_model, F32)]),
        mlp_norm[1:2], w["mlp_w_up1"], w["mlp_w_down1"], "mlp1", loss_head=(tgt, final_norm[None, :]))

    g = {}
    groups = {"mlp1": ["mlp_w_up1", "mlp_w_down1"],
              "mla": ["mla_w_o", "mla_w_uq", "mla_w_dq", "kv_w_uk", "kv_w_uv", "kv_w_dkv"],
              "mlp0": ["mlp_w_up0", "mlp_w_down0"],
              "hgrn_out": ["hgrn_w_o", "hgrn_w_g"],
              "hgrn_in": ["hgrn_w_q", "hgrn_w_f", "hgrn_w_i"]}
    scatter_state = {}

    def scatter_start(tag, after=None):
        scatter_state[tag], tok = _exchange_start(
            [g[k] if k in shard_major else (_col_terms if k in col_sharded else _row_terms)(g[k]) for k in groups[tag]],
            scatter=True, after=after,
            name=f"scatter_{tag}_start")
        return tok

    dh3, dh3_bf, g_mlp_norm1, g["mlp_w_up1"], g["mlp_w_down1"] = _mlp_bwd(
        dh4, dh4_bf, mlp1_saved, mlp_norm[1:2], w["mlp_w_up1"], w["mlp_w_down1"], "mlp1")
    def attn_out_bwd(dres, o, wo):
        d_o = _dot(dres, wo, _NT).astype(BF16)
        prod = d_o.astype(F32) * o.astype(F32)
        return d_o, jnp.concatenate([jnp.broadcast_to(jnp.sum(prod[:, h * hd:(h + 1) * hd], axis=1, keepdims=True),
                                                      (prod.shape[0], hd)) for h in range(n_heads)], axis=1)

    d_oatt, delta = _rowcall(attn_out_bwd, [dh3_bf, o_att], [w["mla_w_o"]], [(qk_cols, BF16), (qk_cols, F32)], [],
                             after=scatter_start("mlp1"), name="attn_out_bwd_x")
    g["mla_w_o"] = _mm(o_att, dh3_bf, mode="tn", name="attn_out_bwd_w")
    dqn, dqr, dkn, dvv, dkr = _attn_bwd(qn, qr, kn, kr, vv, d_oatt, lse, delta, name="attn_bwd")

    def q_path_bwd(cq, cq_n, x_n, d_qn, d_qr, tc, ts1, ts2, g_q, wdq, wn, wr):
        d_qn, d_qr = d_qn.astype(BF16), _rope_slabs(d_qr, tc, ts1, ts2, True).astype(BF16)
        d_cq, d_gq = _rms_bwd(cq, g_q, _dot(d_qn, wn, _NT) + _dot(d_qr, wr, _NT))
        d_cq = d_cq.astype(BF16)
        return _dot(d_cq, wdq, _NT), d_gq, _dot(x_n, d_cq, _TN), _dot(cq_n, d_qn, _TN), _dot(cq_n, d_qr, _TN)

    dxn2, g_q_norm, g_dq, g_uq_nope, g_uq_rope = _rowcall(
        q_path_bwd, [cq_pre, c_q, xn2, dqn, dqr, t_c, t_s1, t_s2], [mla_q_norm, w["mla_w_dq"], w_uq_nope, w_uq_rope],
        [(d_model, F32)], [q_lora, (d_model, q_lora), (q_lora, qk_cols), (q_lora, qk_cols)], tr=512, name="mla_q_bwd")
    g["mla_w_dq"] = g_dq.astype(GRAD_WIRE_DTYPE)
    g["mla_w_uq"] = jnp.concatenate([g_uq_nope.reshape(q_lora, n_heads, hd),
                                     g_uq_rope.reshape(q_lora, n_heads, hd)[:, :, :MLA_ROPE]],
                                    axis=2).reshape(q_lora, -1).astype(GRAD_WIRE_DTYPE)

    def kv_path_bwd(c_all, lat, h_n, d_kn, d_v, d_kr_heads, tc, ts1, ts2, a, d_xn2, dres,
                    g_kv, g_kv_in, g_mla, wdkv, wuk, wuv):
        d_lat, d_gkv = _rms_bwd(c_all[:, :kv_lora], g_kv, _dot(d_kn, wuk, _NT) + _dot(d_v, wuv, _NT))
        d_kr = d_kr_heads[:, :hd]
        for h in range(1, n_heads):
            d_kr = d_kr + d_kr_heads[:, h * hd:(h + 1) * hd]
        d_all = jnp.concatenate([d_lat, _rope_t(d_kr, tc, ts1, ts2)], axis=1).astype(BF16)
        dx1, d_gkv_in = _rms_bwd(a, g_kv_in, _dot(d_all, wdkv, _NT))
        dx2, d_gmla = _rms_bwd(a, g_mla, d_xn2)
        d_a = dx1 + dx2 + dres
        return (d_a, d_a, d_gkv, d_gkv_in, d_gmla, _dot(h_n, d_all, _TN), _dot(lat, d_kn, _TN), _dot(lat, d_v, _TN))

    dh2, dh2_bf, g_kv_norm, g_kv_in_norm, g_mla_norm, g_dkv, g_uk, g_uv = _rowcall(
        kv_path_bwd, [ckr, c_kv, hn, dkn, dvv, dkr, t_c, t_s1, t_s2, h2, dxn2, dh3],
        [kv_norm[None, :], kv_in_norm[None, :], mla_norm, w_dkv_pad, w["kv_w_uk"], w["kv_w_uv"]],
        [(d_model, F32), (d_model, BF16)],
        [kv_lora, d_model, d_model, (d_model, kv_lora + hd), (kv_lora, qk_cols), (kv_lora, qk_cols)], name="mla_kv_bwd")
    g["kv_w_dkv"] = g_dkv[:, :kv_w_dkv.shape[1]].astype(GRAD_WIRE_DTYPE)
    g["kv_w_uk"], g["kv_w_uv"] = g_uk.astype(GRAD_WIRE_DTYPE), g_uv.astype(GRAD_WIRE_DTYPE)
    dh1, dh1_bf, g_mlp_norm0, g["mlp_w_up0"], g["mlp_w_down0"] = _mlp_bwd(
        dh2, dh2_bf, mlp0_saved, mlp_norm[0:1], w["mlp_w_up0"], w["mlp_w_down0"], "mlp0", after=scatter_start("mla"))

    g["hgrn_w_o"] = _mm(mixed, dh1_bf, mode="tn", after=scatter_start("mlp0"), name="hgrn_out_bwd_w")
    do_rec, dzg, g_g_norm = _rowcall(
        lambda dres, o, z, wo, gn: _head_norm_gate_bwd(o, z, _dot(dres, wo, _NT), gn), [dh1_bf, o_rec, zg],
        [w["hgrn_w_o"], hgrn_g_norm], [(d_model, F32), (d_model, BF16)], [hd], name="hgrn_gate_out_bwd")
    g["hgrn_w_g"] = _mm(xn0, dzg, mode="tn", name="hgrn_w_g_bwd_w")
    dzq, dzf, dzi, g_lb = _hgrn_bwd(zq, zf, zi, lb_logits_full, states, do_rec, scatter_start("hgrn_out"),
                                    name="hgrn_bwd")

    def hgrn_proj_bwd(a, dres, *rest):
        dzs, gw, weights = rest[:4], rest[4], rest[5:]
        dxn = _dot(dzs[0], weights[0], _NT)
        for dz, wt in zip(dzs[1:], weights[1:]):
            dxn = dxn + _dot(dz, wt, _NT)
        dx, dw = _rms_bwd(a, gw, dxn)
        return dx + dres, dw

    grad_x, g_hgrn_norm = _rowcall(hgrn_proj_bwd, [xs, dh1, dzq, dzf, dzi, dzg],
                                   [hgrn_norm_full] + [w[k] for k in proj_names], [(d_model, F32)], [d_model],
                                   tr=512, name="hgrn_proj_bwd")

    small_parts = [g_hgrn_norm, g_lb, g_g_norm, g_mla_norm, g_q_norm, g_kv_in_norm, g_kv_norm, g_mlp_norm0,
                   g_mlp_norm1, g_final_norm, loss_part]
    small_sizes = [p.shape[1] for p in small_parts]
    small_terms = _exchange([jnp.concatenate(small_parts, axis=1)], scatter=False, name="gather_small")[0]
    small_sum = _sum_terms(small_terms, name="sum_small")
    for nm, dz in (("hgrn_w_q", dzq), ("hgrn_w_f", dzf), ("hgrn_w_i", dzi)):
        g[nm] = _mm(xn0, dz, mode="tn", after=small_sum, name=f"{nm}_bwd_w")
    last = scatter_start("hgrn_in")
    offs = [0]
    for sz in small_sizes:
        offs.append(offs[-1] + sz)
    (s_hgrn_norm, s_lb, s_g_norm, s_mla_norm, s_q_norm, s_kv_in_norm, s_kv_norm, s_mlp_norm0, s_mlp_norm1, s_final_norm,
     s_loss) = [small_sum[:, a:b] for a, b in zip(offs[:-1], offs[1:])]
    shard = hgrn_norm.shape[1]
    g_lb_logits = _lb_logits_grad(lax.dynamic_slice_in_dim(s_lb, me * shard, shard, axis=1), hgrn_lb_logits,
                                  name="lb_logits_grad")
    loss = s_loss[0, 0]

    res, layer_terms = {}, {}

    def update(k, term_list):
        shape = given[k].shape
        as_layers = (len(term_list), shape[-2], shape[-1])
        upd = _adam(given[k].reshape(as_layers), term_list, given["m_" + k].reshape(as_layers),
                    given["v_" + k].reshape(as_layers), name=f"adam_{k}")
        res[k] = [o.reshape(shape) for o in upd]
        return upd[0]

    for tag, names in groups.items():
        for k, t in zip(names, _exchange_wait(scatter_state[tag], last, name=f"scatter_{tag}_wait")):
            if k.startswith("mlp_w_"):
                layer_terms.setdefault(k[:-1], {})[int(k[-1])] = t
                if len(layer_terms[k[:-1]]) == 2:
                    last = update(k[:-1], [layer_terms[k[:-1]][0], layer_terms[k[:-1]][1]])
            else:
                last = update(k, [t])

    small_grads = {
        "hgrn_norm": lax.dynamic_slice_in_dim(s_hgrn_norm, me * shard, shard, axis=1),
        "hgrn_g_norm": s_g_norm, "hgrn_lb_logits": g_lb_logits, "mla_norm": s_mla_norm, "mla_q_norm": s_q_norm,
        "kv_in_norm": s_kv_in_norm, "kv_norm": s_kv_norm,
        "mlp_norm": jnp.concatenate([s_mlp_norm0, s_mlp_norm1], axis=0), "final_norm": s_final_norm,
    }
    small_names = list(small_grads)

    def flat(a):
        return a.reshape(1, -1)

    packed = [jnp.concatenate([flat(src[pre + k]) for k in small_names], axis=1)
              for src, pre in ((given, ""), (small_grads, ""), (given, "m_"), (given, "v_"))]
    small_out = _adam(packed[0][None], [packed[1][None]], packed[2][None], packed[3][None], name="adam_small")
    off = 0
    for k in small_names:
        size = given[k].size
        res[k] = [o[0, :, off:off + size].reshape(given[k].shape) for o in small_out]
        off += size

    outs = [loss, grad_x[None]]
    for i in range(4):
        outs += [res[k][i] for k in weight_names]
    return tuple(outs)
```

```python
import functools

import jax
import jax.numpy as jnp
from jax import lax
from jax.experimental import pallas as pl
from jax.experimental.pallas import tpu as pltpu

F32 = jnp.float32
BF16 = jnp.bfloat16

EPS = 1e-6
LANES = 128
N_DEV = 8
V7X_VMEM_LIMIT_BYTES = 56 << 20
MM_PIPELINE_BYTES = 30 << 20
MM_ROW_TILE = 512
ADAM_ROW_TILE = 256
GRAD_WIRE_DTYPE = BF16

HGRN_HEADS = 8
HGRN_CHUNK = 64
HGRN_SUB = 16
HGRN_HEADS_PER_STEP = 8
HGRN_CHUNKS_PER_STEP = 4
EXP_CLAMP = 80.0
MLA_HEADS = 16
MLA_NOPE = 128
MLA_ROPE = 64
ROPE_THETA = 10000.0
ATTN_SCALE = (MLA_NOPE + MLA_ROPE) ** -0.5

ADAM_LR = 0.001
ADAM_B1 = 0.9
ADAM_B2 = 0.999
ADAM_EPS = 1e-08
ADAM_WD = 0.01
ADAM_STEP = 10

_NN = ((1,), (0,))
_NT = ((1,), (1,))
_TN = ((0,), (0,))


def _params(*sem):
    return pltpu.CompilerParams(dimension_semantics=sem, vmem_limit_bytes=V7X_VMEM_LIMIT_BYTES)


def _dot(a, b, dims):
    return lax.dot_general(a.astype(BF16), b.astype(BF16), (dims, ((), ())), preferred_element_type=F32)


def _dot_f32(a, b, dims=_NN):
    return lax.dot_general(a, b, (dims, ((), ())), precision=lax.Precision.HIGH, preferred_element_type=F32)


def _sigmoid(x):
    return 1.0 / (1.0 + jnp.exp(-x))


def _rms(x, w):
    r = lax.rsqrt(jnp.mean(x * x, axis=-1, keepdims=True) + EPS)
    return x * r * w


def _rms_bwd(x, w, dy):
    r = lax.rsqrt(jnp.mean(x * x, axis=-1, keepdims=True) + EPS)
    xh = x * r
    dw = jnp.sum(dy * xh, axis=0, keepdims=True)
    dxh = dy * w
    dx = r * (dxh - xh * jnp.mean(dxh * xh, axis=-1, keepdims=True))
    return dx, dw


def _mm_tiles(m, n, k, a_bytes, b_bytes, out_tile_bytes):
    tm = min(m, MM_ROW_TILE)
    for tn in (n, 2048, 1024, 512, 256, LANES):
        if tn <= n and n % tn == 0:
            if 2 * (tm * k * a_bytes + k * tn * b_bytes + tm * tn * out_tile_bytes) <= MM_PIPELINE_BYTES:
                return tm, tn
    return tm, min(n, LANES)


def _mm(a, b, *, mode, name, out_dtype=None, add=None, relu2_of=None, after=None, col_shards=None):
    if mode == "nn":
        (m, k), (k2, n) = a.shape, b.shape
    elif mode == "nt":
        (m, k), (n, k2) = a.shape, b.shape
    else:
        (k, m), (k2, n) = a.shape, b.shape
    assert k == k2, (name, a.shape, b.shape)
    if out_dtype is None:
        out_dtype = GRAD_WIRE_DTYPE if mode == "tn" else F32
    tile_bytes = sum(x.dtype.itemsize for x in (add, relu2_of) if x is not None) + jnp.dtype(out_dtype).itemsize
    tm, tn = _mm_tiles(m, n, k, a.dtype.itemsize, b.dtype.itemsize, tile_bytes)
    if col_shards is not None:
        assert add is None and relu2_of is None
        tn = n // col_shards
    assert m % tm == 0 and n % tn == 0, (name, m, n)
    dims = {"nn": _NN, "nt": _NT, "tn": _TN}[mode]
    a_spec = pl.BlockSpec((k, tm), lambda i, j: (0, i)) if mode == "tn" else pl.BlockSpec((tm, k), lambda i, j: (i, 0))
    b_spec = pl.BlockSpec((tn, k), lambda i, j: (j, 0)) if mode == "nt" else pl.BlockSpec((k, tn), lambda i, j: (0, j))
    o_spec = pl.BlockSpec((tm, tn), lambda i, j: (i, j))
    operands, in_specs = [a, b], [a_spec, b_spec]
    for extra in (add, relu2_of):
        if extra is not None:
            assert extra.shape == (m, n), (name, extra.shape)
            operands.append(extra)
            in_specs.append(o_spec)
    n_in = len(operands)
    if after is not None:
        operands.append(after)
        in_specs.append(pl.BlockSpec(memory_space=pl.ANY))
    out_shape = jax.ShapeDtypeStruct((m, n), out_dtype)
    if col_shards is not None:
        out_shape = jax.ShapeDtypeStruct((col_shards, m, tn), out_dtype)
        o_spec = pl.BlockSpec((None, tm, tn), lambda i, j: (j, i, 0))

    def body(*refs):
        acc = _dot(refs[0][...], refs[1][...], dims)
        extras, outs = refs[2:n_in], refs[len(operands):]
        if add is not None:
            acc = acc + extras[0][...]
        if relu2_of is not None:
            acc = acc * (2.0 * jnp.sqrt(extras[-1][...].astype(F32)))
        outs[0][...] = acc.astype(out_dtype)

    return pl.pallas_call(
        body, name=name, grid=(m // tm, n // tn), in_specs=in_specs, out_specs=o_spec, out_shape=out_shape,
        compiler_params=_params("parallel", "parallel"),
    )(*operands)


def _rowcall(fn, rows, consts, outs, accs, *, name, tr=256, after=None):
    s = rows[0].shape[0]
    tr = min(tr, s)
    assert s % tr == 0
    n_out = len(outs)
    accs = [(1, a) if isinstance(a, int) else a for a in accs]
    in_specs = [pl.BlockSpec((tr, r.shape[1]), lambda i: (i, 0)) for r in rows]
    in_specs += [pl.BlockSpec(c.shape, lambda i, nd=c.ndim: (0,) * nd) for c in consts]
    out_shape = [jax.ShapeDtypeStruct((s, w), dt) for w, dt in outs] + [jax.ShapeDtypeStruct(a, F32) for a in accs]
    out_specs = [pl.BlockSpec((tr, w), lambda i: (i, 0)) for w, _ in outs] + [pl.BlockSpec(a, lambda i: (0, 0)) for a in accs]
    n_in = len(rows) + len(consts)

    def body(*refs):
        res = fn(*[r[...] for r in refs[:n_in]])
        out_refs = refs[n_in + (after is not None):]
        for ref, val in zip(out_refs[:n_out], res[:n_out]):
            ref[...] = val.astype(ref.dtype)
        i = pl.program_id(0)
        for ref, val in zip(out_refs[n_out:], res[n_out:]):
            @pl.when(i == 0)
            def _(ref=ref, val=val):
                ref[...] = val

            @pl.when(i > 0)
            def _(ref=ref, val=val):
                ref[...] += val

    behind = [] if after is None else [after]
    return pl.pallas_call(
        body, name=name, grid=(s // tr,), in_specs=in_specs + [pl.BlockSpec(memory_space=pl.ANY)] * len(behind),
        out_specs=out_specs, out_shape=out_shape, compiler_params=_params("arbitrary" if accs else "parallel"),
    )(*rows, *consts, *behind)


def _rope_tables(seq):
    half = MLA_ROPE // 2
    inv_freq = ROPE_THETA ** (-jnp.arange(half, dtype=F32) / half)
    ang = jnp.arange(seq, dtype=F32)[:, None] * inv_freq[None, :]
    cos, sin, zero = jnp.cos(ang), jnp.sin(ang), jnp.zeros((seq, half), F32)
    t_c = jnp.concatenate([cos, cos, zero, zero], axis=1)
    t_s1 = jnp.concatenate([-sin, zero, zero, zero], axis=1)
    t_s2 = jnp.concatenate([zero, sin, zero, zero], axis=1)
    return t_c, t_s1, t_s2


def _rope(slab, t_c, t_s1, t_s2):
    return slab * t_c + pltpu.roll(slab, 96, 1) * t_s1 + pltpu.roll(slab, 32, 1) * t_s2


def _rope_t(d, t_c, t_s1, t_s2):
    return d * t_c + pltpu.roll(d * t_s1, 32, 1) + pltpu.roll(d * t_s2, 96, 1)


def _lower_bound(logits):
    l0, l1 = logits[0:1, :], logits[1:2, :]
    mx = jnp.maximum(l0, l1)
    e0, e1 = jnp.exp(l0 - mx), jnp.exp(l1 - mx)
    return e0 / (e0 + e1)


def _tri(n, lower):
    row = lax.broadcasted_iota(jnp.int32, (n, n), 0)
    col = lax.broadcasted_iota(jnp.int32, (n, n), 1)
    return (row >= col) if lower else (row <= col)


def _hgrn_fwd(zq, zf, zi, lb_logits, *, name):
    s, d = zq.shape
    h_n, c, hp, cps = d // LANES, HGRN_CHUNK, HGRN_HEADS_PER_STEP, HGRN_CHUNKS_PER_STEP
    nc = s // c

    def body(zq_ref, zf_ref, zi_ref, lb_ref, o_ref, st_ref, state_sc, b_sc):
        @pl.when(pl.program_id(1) == 0)
        def _():
            state_sc[...] = jnp.zeros_like(state_sc)

        lower = _tri(c, True)
        lower_f = lower.astype(F32)
        hs, pairs = range(hp), [(cc, hh) for cc in range(cps) for hh in range(hp)]
        sls = [slice(hh * LANES, (hh + 1) * LANES) for hh in hs]
        rws = [slice(cc * c, (cc + 1) * c) for cc in range(cps)]
        lb = [_lower_bound(lb_ref[:, sl]) for sl in sls]
        zq_v = {p: zq_ref[rws[p[0]], sls[p[1]]] for p in pairs}
        q = {p: zq_v[p] * _sigmoid(zq_v[p]) for p in pairs}
        f = {p: lb[p[1]] + (1.0 - lb[p[1]]) * _sigmoid(zf_ref[rws[p[0]], sls[p[1]]]) for p in pairs}
        k = {p: 1.0 - f[p] for p in pairs}
        v = {p: zi_ref[rws[p[0]], sls[p[1]]] for p in pairs}
        b = {p: _dot_f32(lower_f, jnp.log(f[p])) for p in pairs}
        for p in pairs:
            b_sc[p[0], p[1]] = b[p]
        qe = {p: q[p] * jnp.exp(b[p]) for p in pairs}
        scores = {p: [] for p in pairs}
        for i in range(c // HGRN_SUB):
            lo = i * HGRN_SUB
            for p in pairs:
                ref = b_sc[p[0], p[1], lo - 1:lo, :] if i > 0 else jnp.zeros((1, LANES), F32)
                qt = q[p][lo:lo + HGRN_SUB, :] * jnp.exp(b[p][lo:lo + HGRN_SUB, :] - ref)
                dec = jnp.exp(jnp.minimum(ref - b[p], EXP_CLAMP))
                scores[p].append(_dot(qt, k[p] * dec, _NT))
        o_intra = {p: _dot(jnp.where(lower, jnp.concatenate(scores[p], axis=0), 0.0), v[p], _NN) for p in pairs}
        bl = {p: b_sc[p[0], p[1], c - 1:c, :] for p in pairs}
        k_end = {p: k[p] * jnp.exp(bl[p] - b[p]) for p in pairs}
        state = [state_sc[hh] for hh in hs]
        for cc in range(cps):
            for hh in hs:
                st_ref[hh, cc] = state[hh]
                o_ref[rws[cc], sls[hh]] = _dot(qe[cc, hh], state[hh], _NT) + o_intra[cc, hh]
            state = [state[hh] * jnp.exp(bl[cc, hh]) + _dot(v[cc, hh], k_end[cc, hh], _TN) for hh in hs]
        for hh in hs:
            state_sc[hh] = state[hh]

    tile = pl.BlockSpec((cps * c, hp * LANES), lambda h, i: (i, h))
    return pl.pallas_call(
        body, name=name, grid=(h_n // hp, nc // cps),
        in_specs=[tile, tile, tile, pl.BlockSpec((2, hp * LANES), lambda h, i: (0, h))],
        out_specs=[tile, pl.BlockSpec((hp, cps, LANES, LANES), lambda h, i: (h, i, 0, 0))],
        out_shape=[jax.ShapeDtypeStruct((s, d), F32), jax.ShapeDtypeStruct((h_n, nc, LANES, LANES), F32)],
        scratch_shapes=[pltpu.VMEM((hp, LANES, LANES), F32), pltpu.VMEM((cps, hp, c, LANES), F32)],
        compiler_params=_params("parallel", "arbitrary"),
    )(zq, zf, zi, lb_logits)


def _hgrn_bwd(zq, zf, zi, lb_logits, states, do, after, *, name):
    s, d = zq.shape
    h_n, c, hp, cps = d // LANES, HGRN_CHUNK, HGRN_HEADS_PER_STEP, HGRN_CHUNKS_PER_STEP
    nc = s // c
    n_steps = nc // cps

    def body(zq_ref, zf_ref, zi_ref, lb_ref, st_ref, do_ref, _, dzq_ref, dzf_ref, dzi_ref, dlb_ref, dstate_sc, b_sc):
        @pl.when(pl.program_id(1) == 0)
        def _():
            dstate_sc[...] = jnp.zeros_like(dstate_sc)
            dlb_ref[...] = jnp.zeros_like(dlb_ref)

        lower, upper = _tri(c, True), _tri(c, False).astype(F32)
        lower_f = lower.astype(F32)
        last_row = lax.broadcasted_iota(jnp.int32, (c, LANES), 0) == c - 1
        hs, pairs = range(hp), [(cc, hh) for cc in range(cps) for hh in range(hp)]
        sls = [slice(hh * LANES, (hh + 1) * LANES) for hh in hs]
        rws = [slice(cc * c, (cc + 1) * c) for cc in range(cps)]
        lb = [_lower_bound(lb_ref[:, sl]) for sl in sls]
        zq_v = {p: zq_ref[rws[p[0]], sls[p[1]]] for p in pairs}
        sq = {p: _sigmoid(zq_v[p]) for p in pairs}
        q = {p: zq_v[p] * sq[p] for p in pairs}
        sf = {p: _sigmoid(zf_ref[rws[p[0]], sls[p[1]]]) for p in pairs}
        f = {p: lb[p[1]] + (1.0 - lb[p[1]]) * sf[p] for p in pairs}
        k = {p: 1.0 - f[p] for p in pairs}
        v = {p: zi_ref[rws[p[0]], sls[p[1]]] for p in pairs}
        d_o = {p: do_ref[rws[p[0]], sls[p[1]]] for p in pairs}
        b = {p: _dot_f32(lower_f, jnp.log(f[p])) for p in pairs}
        s0t = {p: st_ref[p[1], p[0]] for p in pairs}
        for p in pairs:
            b_sc[p[0], p[1]] = b[p]
        bl = {p: b_sc[p[0], p[1], c - 1:c, :] for p in pairs}
        eb = {p: jnp.exp(b[p]) for p in pairs}
        ebl = {p: jnp.exp(bl[p]) for p in pairs}
        dec_end = {p: jnp.exp(bl[p] - b[p]) for p in pairs}
        da = {p: jnp.where(lower, _dot(d_o[p], v[p], _NT), 0.0) for p in pairs}
        dq = {p: _dot(d_o[p], s0t[p], _NN) * eb[p] for p in pairs}
        dstate_in = {p: _dot(d_o[p], q[p] * eb[p], _TN) for p in pairs}
        dk_intra = {p: jnp.zeros((c, LANES), F32) for p in pairs}
        scores, dq_blocks = {p: [] for p in pairs}, {p: [] for p in pairs}
        for i in range(c // HGRN_SUB):
            lo = i * HGRN_SUB
            for p in pairs:
                ref = b_sc[p[0], p[1], lo - 1:lo, :] if i > 0 else jnp.zeros((1, LANES), F32)
                grow = jnp.exp(b[p][lo:lo + HGRN_SUB, :] - ref)
                qt = q[p][lo:lo + HGRN_SUB, :] * grow
                dec = jnp.exp(jnp.minimum(ref - b[p], EXP_CLAMP))
                kd = k[p] * dec
                scores[p].append(_dot(qt, kd, _NT))
                da_i = da[p][lo:lo + HGRN_SUB, :]
                dq_blocks[p].append(_dot_f32(da_i, kd, _NN) * grow)
                dk_intra[p] = dk_intra[p] + _dot_f32(da_i, qt, _TN) * dec
        dv_intra = {p: _dot(jnp.where(lower, jnp.concatenate(scores[p], axis=0), 0.0), d_o[p], _TN) for p in pairs}
        dq = {p: dq[p] + jnp.concatenate(dq_blocks[p], axis=0) for p in pairs}
        q_dq = {p: q[p] * dq[p] for p in pairs}
        for p in pairs:
            dzq_ref[rws[p[0]], sls[p[1]]] = (dq[p] * sq[p] * (1.0 + zq_v[p] * (1.0 - sq[p]))).astype(BF16)
        dstate = [dstate_sc[hh] for hh in hs]
        for cc in reversed(range(cps)):
            ps = [(cc, hh) for hh in hs]
            dk_state = [_dot(v[p], dstate[p[1]], _NN) * dec_end[p] for p in ps]
            dv = [dv_intra[p] + _dot(k[p] * dec_end[p], dstate[p[1]], _NT) for p in ps]
            dk = [dk_intra[p] + dk_state[p[1]] for p in ps]
            db_last = [jnp.sum(k[p] * dk_state[p[1]], axis=0, keepdims=True)
                       + ebl[p] * jnp.sum(s0t[p] * dstate[p[1]], axis=0, keepdims=True) for p in ps]
            db = [q_dq[p] - k[p] * dk[p[1]] + jnp.where(last_row, db_last[p[1]], 0.0) for p in ps]
            df = [_dot_f32(upper, db[p[1]]) / f[p] - dk[p[1]] for p in ps]
            for p in ps:
                hh = p[1]
                dzf_ref[rws[cc], sls[hh]] = (df[hh] * (1.0 - lb[hh]) * sf[p] * (1.0 - sf[p])).astype(BF16)
                dlb_ref[:, sls[hh]] += jnp.sum(df[hh] * (1.0 - sf[p]), axis=0, keepdims=True)
                dzi_ref[rws[cc], sls[hh]] = dv[hh].astype(BF16)
            dstate = [dstate[p[1]] * ebl[p] + dstate_in[p] for p in ps]
        for hh in hs:
            dstate_sc[hh] = dstate[hh]

    tile = pl.BlockSpec((cps * c, hp * LANES), lambda h, i: (n_steps - 1 - i, h))
    out = jax.ShapeDtypeStruct((s, d), BF16)
    return pl.pallas_call(
        body, name=name, grid=(h_n // hp, n_steps),
        in_specs=[tile, tile, tile, pl.BlockSpec((2, hp * LANES), lambda h, i: (0, h)),
                  pl.BlockSpec((hp, cps, LANES, LANES), lambda h, i: (h, n_steps - 1 - i, 0, 0)), tile,
                  pl.BlockSpec(memory_space=pl.ANY)],
        out_specs=[tile, tile, tile, pl.BlockSpec((1, hp * LANES), lambda h, i: (0, h))],
        out_shape=[out, out, out, jax.ShapeDtypeStruct((1, d), F32)],
        scratch_shapes=[pltpu.VMEM((hp, LANES, LANES), F32), pltpu.VMEM((cps, hp, c, LANES), F32)],
        compiler_params=_params("parallel", "arbitrary"),
    )(zq, zf, zi, lb_logits, states, do, after)


ATTN_SUB_ROWS = 256
LOG2E = 1.4426950408889634
LN2 = 0.6931471805599453
Q_PRESCALE = ATTN_SCALE * LOG2E


def _attn_tile(s):
    return min(1024, max(128, s // 2))


def _causal_pairs(n, q_major):
    pairs = [(i, j) for i in range(n) for j in range(i + 1)] if q_major else [(i, j) for j in range(n) for i in range(j, n)]
    return jnp.asarray([p[0] for p in pairs], jnp.int32), jnp.asarray([p[1] for p in pairs], jnp.int32)


def _sub_scores(qn_ref, qr_ref, k, r, sub, t, diagonal):
    q = jnp.concatenate([qn_ref[r:r + sub, :], qr_ref[r:r + sub, :]], axis=1)
    if not diagonal:
        return q, _dot(q, k, _NT)
    cols = r + sub
    keep = lax.broadcasted_iota(jnp.int32, (sub, cols), 1) <= r + lax.broadcasted_iota(jnp.int32, (sub, cols), 0)
    return q, jnp.where(keep, _dot(q, k[:cols], _NT), -jnp.inf)


def _attn_fwd(qn, qr, kn, kr, v, *, name):
    s, t = qn.shape[0], _attn_tile(qn.shape[0])
    sub = min(t, ATTN_SUB_ROWS)
    q_blk, k_blk = _causal_pairs(s // t, True)

    def body(qi_ref, kj_ref, qn_ref, qr_ref, kn_ref, kr_ref, v_ref, o_ref, lse_ref, m_sc, l_sc, acc_sc):
        p_id = pl.program_id(1)
        i, j = qi_ref[p_id], kj_ref[p_id]

        @pl.when(j == 0)
        def _():
            m_sc[...] = jnp.full_like(m_sc, -jnp.inf)
            l_sc[...] = jnp.zeros_like(l_sc)
            acc_sc[...] = jnp.zeros_like(acc_sc)

        def update(diagonal):
            k = jnp.concatenate([kn_ref[...], kr_ref[...]], axis=1)
            v = v_ref[...]
            starts = list(range(0, t, sub))
            scs = [_sub_scores(qn_ref, qr_ref, k, r, sub, t, diagonal)[1] for r in starts]
            ps, alphas = [], []
            for r, sc in zip(starts, scs):
                m_prev = m_sc[r:r + sub, :]
                m_new = jnp.maximum(m_prev, jnp.max(sc, axis=1, keepdims=True))
                alpha = jnp.exp2(m_prev - m_new)
                p = jnp.exp2(sc - m_new[:, :1])
                l_sc[r:r + sub, :] = alpha * l_sc[r:r + sub, :] + jnp.sum(p, axis=1, keepdims=True)
                m_sc[r:r + sub, :] = m_new
                ps.append(p)
                alphas.append(alpha)
            for r, p, alpha in zip(starts, ps, alphas):
                acc_sc[r:r + sub, :] = alpha * acc_sc[r:r + sub, :] + _dot(p, v[:p.shape[1]], _NN)

        @pl.when(j < i)
        def _():
            update(False)

        @pl.when(j == i)
        def _():
            update(True)
            o_ref[...] = (acc_sc[...] / l_sc[...]).astype(BF16)
            lse_ref[...] = m_sc[...] + jnp.log(l_sc[...]) * LOG2E

    q_spec = pl.BlockSpec((t, LANES), lambda h, p, qi, kj: (qi[p], h))
    k_spec = pl.BlockSpec((t, LANES), lambda h, p, qi, kj: (kj[p], h))
    kr_spec = pl.BlockSpec((t, LANES), lambda h, p, qi, kj: (kj[p], 0))
    stat = pltpu.VMEM((t, LANES), F32)
    return pl.pallas_call(
        body, name=name,
        grid_spec=pltpu.PrefetchScalarGridSpec(
            num_scalar_prefetch=2, grid=(MLA_HEADS, q_blk.shape[0]),
            in_specs=[q_spec, q_spec, k_spec, kr_spec, k_spec], out_specs=[q_spec, q_spec],
            scratch_shapes=[stat, stat, stat]),
        out_shape=[jax.ShapeDtypeStruct(qn.shape, BF16), jax.ShapeDtypeStruct(qn.shape, F32)],
        compiler_params=_params("parallel", "arbitrary"),
    )(q_blk, k_blk, qn, qr, kn, kr, v)


def _attn_bwd(qn, qr, kn, kr, v, do, lse, delta, *, name):
    s, t = qn.shape[0], _attn_tile(qn.shape[0])
    n, sub = s // t, min(t, ATTN_SUB_ROWS)
    q_blk, k_blk = _causal_pairs(n, False)

    def body(qi_ref, kj_ref, qn_ref, qr_ref, kn_ref, kr_ref, v_ref, do_ref, lse_ref, delta_ref,
             dqn_ref, dqr_ref, dkn_ref, dv_ref, dkr_ref, dk_sc, dv_sc):
        p_id = pl.program_id(1)
        i, j = qi_ref[p_id], kj_ref[p_id]

        @pl.when(p_id == 0)
        def _():
            dqn_ref[...] = jnp.zeros_like(dqn_ref)
            dqr_ref[...] = jnp.zeros_like(dqr_ref)

        @pl.when(i == j)
        def _():
            dk_sc[...] = jnp.zeros_like(dk_sc)
            dv_sc[...] = jnp.zeros_like(dv_sc)

        def accumulate(diagonal):
            k = jnp.concatenate([kn_ref[...], kr_ref[...]], axis=1)
            v = v_ref[...]
            starts = list(range(0, t, sub))
            qs, d_os, scs, dps = [], [], [], []
            for r in starts:
                q, sc = _sub_scores(qn_ref, qr_ref, k, r, sub, t, diagonal)
                d_o = do_ref[r:r + sub, :]
                qs.append(q)
                d_os.append(d_o)
                scs.append(sc)
                dps.append(_dot(d_o, v[:sc.shape[1]], _NT))
            ps, dss = [], []
            for r, sc, dp in zip(starts, scs, dps):
                p = jnp.exp2(sc - lse_ref[r:r + sub, :][:, :1])
                ps.append(p.astype(BF16))
                dss.append((p * (dp - delta_ref[r:r + sub, :][:, :1])).astype(BF16))
            for r, q, d_o, p, ds in zip(starts, qs, d_os, ps, dss):
                cols = p.shape[1]
                dv_sc[:cols, :] += _dot(p, d_o, _TN)
                dk_sc[:cols, :] += _dot(ds, q, _TN)
                dq = _dot(ds, k[:cols], _NN) * ATTN_SCALE
                rows = pl.ds(pl.multiple_of(i * t + r, sub), sub)
                dqn_ref[rows, :] += dq[:, :LANES]
                dqr_ref[rows, :] += dq[:, LANES:]

        @pl.when(j < i)
        def _():
            accumulate(False)

        @pl.when(j == i)
        def _():
            accumulate(True)

        @pl.when(i == n - 1)
        def _():
            dkn_ref[...] = (dk_sc[:, :LANES] * LN2).astype(BF16)
            dkr_ref[...] = dk_sc[:, LANES:] * LN2
            dv_ref[...] = dv_sc[...].astype(BF16)

    q_spec = pl.BlockSpec((t, LANES), lambda h, p, qi, kj: (qi[p], h))
    k_spec = pl.BlockSpec((t, LANES), lambda h, p, qi, kj: (kj[p], h))
    kr_spec = pl.BlockSpec((t, LANES), lambda h, p, qi, kj: (kj[p], 0))
    head_spec = pl.BlockSpec((s, LANES), lambda h, p, qi, kj: (0, h))
    f32_out, bf16_out = jax.ShapeDtypeStruct(qn.shape, F32), jax.ShapeDtypeStruct(qn.shape, BF16)
    return pl.pallas_call(
        body, name=name,
        grid_spec=pltpu.PrefetchScalarGridSpec(
            num_scalar_prefetch=2, grid=(MLA_HEADS, q_blk.shape[0]),
            in_specs=[q_spec, q_spec, k_spec, kr_spec, k_spec, q_spec, q_spec, q_spec],
            out_specs=[head_spec, head_spec, k_spec, k_spec, k_spec],
            scratch_shapes=[pltpu.VMEM((t, 2 * LANES), F32), pltpu.VMEM((t, LANES), F32)]),
        out_shape=[f32_out, f32_out, bf16_out, bf16_out, f32_out],
        compiler_params=_params("parallel", "arbitrary"),
    )(q_blk, k_blk, qn, qr, kn, kr, v, do, lse, delta)


def _exchange(arrs, *, scatter, name):
    n = len(arrs)
    out_shape = [jax.ShapeDtypeStruct(a.shape if scatter else (N_DEV, *a.shape), a.dtype) for a in arrs]

    def body(*refs):
        ins, outs = refs[:n], refs[n:2 * n]
        send_sems, recv_sems, local_sems = refs[2 * n:]
        x, y, c = lax.axis_index("x"), lax.axis_index("y"), lax.axis_index("c")
        me = 4 * x + 2 * y + c
        copies = []
        for k in range(n):
            local = pltpu.make_async_copy(ins[k].at[me] if scatter else ins[k], outs[k].at[me], local_sems.at[k])
            local.start()
            copies.append(local)
            for d in range(1, N_DEV):
                px, py, pc = (x + (d >> 2)) % 2, (y + ((d >> 1) & 1)) % 2, (c + (d & 1)) % 2
                peer = 4 * px + 2 * py + pc
                remote = pltpu.make_async_remote_copy(
                    src_ref=ins[k].at[peer] if scatter else ins[k], dst_ref=outs[k].at[me],
                    send_sem=send_sems.at[k, d - 1], recv_sem=recv_sems.at[k, d - 1],
                    device_id=(px, py, pc), device_id_type=pl.DeviceIdType.MESH)
                remote.start()
                copies.append(remote)
        for cp in copies:
            cp.wait()

    any_spec = pl.BlockSpec(memory_space=pl.ANY)
    return pl.pallas_call(
        body, name=name, in_specs=[any_spec] * n, out_specs=[any_spec] * n, out_shape=out_shape,
        scratch_shapes=[pltpu.SemaphoreType.DMA((n, N_DEV - 1)), pltpu.SemaphoreType.DMA((n, N_DEV - 1)),
                        pltpu.SemaphoreType.DMA((n,))],
    )(*arrs)


def _peers(x, y, c):
    out = []
    for d in range(1, N_DEV):
        px, py, pc = (x + (d >> 2)) % 2, (y + ((d >> 1) & 1)) % 2, (c + (d & 1)) % 2
        out.append(((px, py, pc), 4 * px + 2 * py + pc))
    return out


CHIP_LEVEL_PEERS = (1, 2, 4, 6)


def _exchange_copies(ins, lands, send_sems, recv_sems, scatter, chip_level=False):
    x, y, c = lax.axis_index("x"), lax.axis_index("y"), lax.axis_index("c")
    me = 4 * x + 2 * y + c
    local, remote = [], []
    for k in range(len(ins)):
        local.append(pltpu.make_async_copy(ins[k].at[me] if scatter else ins[k], lands[k].at[me],
                                           recv_sems.at[k * N_DEV + N_DEV - 1]))
        for d, (coords, peer) in enumerate(_peers(x, y, c)):
            if chip_level and d + 1 not in CHIP_LEVEL_PEERS:
                continue
            remote.append(pltpu.make_async_remote_copy(
                src_ref=ins[k].at[peer] if scatter else ins[k], dst_ref=lands[k].at[me],
                send_sem=send_sems.at[k * N_DEV + d], recv_sem=recv_sems.at[k * N_DEV + d],
                device_id=coords, device_id_type=pl.DeviceIdType.MESH))
    return local, remote


def _exchange_start(arrs, *, scatter, name, after=None, chip_level=False):
    n = len(arrs)
    hbm = pl.BlockSpec(memory_space=pltpu.HBM)
    sem = pl.BlockSpec(memory_space=pltpu.SEMAPHORE)
    lands = [lax.empty(a.shape if scatter else (N_DEV, *a.shape), a.dtype) for a in arrs]

    def body(*refs):
        ins, land_refs = refs[:n], refs[n:2 * n]
        first_out = 2 * n + (after is not None)
        send_sems, recv_sems, token = refs[first_out], refs[first_out + 1], refs[-1]
        local, remote = _exchange_copies(ins, land_refs, send_sems, recv_sems, scatter, chip_level)
        for cp in local + remote:
            cp.start()
        token[...] = jnp.zeros_like(token)

    operands = [pltpu.with_memory_space_constraint(a, pltpu.HBM) for a in list(arrs) + lands]
    behind = [] if after is None else [after]
    res = pl.pallas_call(
        body, name=name,
        out_shape=(pltpu.SemaphoreType.DMA((n * N_DEV,)), pltpu.SemaphoreType.DMA((n * N_DEV,)),
                   *[pltpu.HBM(o.shape, o.dtype) for o in operands], jax.ShapeDtypeStruct((8, LANES), F32)),
        in_specs=[hbm] * (2 * n) + [pl.BlockSpec(memory_space=pl.ANY)] * len(behind),
        out_specs=(sem, sem, *[hbm] * (2 * n), pl.BlockSpec(memory_space=pltpu.VMEM)),
        input_output_aliases={i: 2 + i for i in range(2 * n)},
        compiler_params=pltpu.CompilerParams(has_side_effects=pltpu.SideEffectType.DATAFLOW_SIDE_EFFECTING),
    )(*operands, *behind)
    return (res[0], res[1], list(res[2:2 + n]), list(res[2 + n:2 + 2 * n]), scatter, chip_level), res[-1]


def _exchange_wait(state, after, *, name):
    send_sems, recv_sems, ins, lands, scatter, chip_level = state
    n = len(ins)
    hbm = pl.BlockSpec(memory_space=pltpu.HBM)
    sem = pl.BlockSpec(memory_space=pltpu.SEMAPHORE)

    def body(*refs):
        in_refs, land_refs = refs[:n], refs[n:2 * n]
        local, remote = _exchange_copies(in_refs, land_refs, refs[2 * n], refs[2 * n + 1], scatter, chip_level)
        for cp in local:
            cp.wait()
        for cp in remote:
            cp.wait_send()
            cp.wait_recv()

    res = pl.pallas_call(
        body, name=name, out_shape=tuple(pltpu.HBM(o.shape, o.dtype) for o in ins + lands),
        in_specs=[hbm] * (2 * n) + [sem, sem, pl.BlockSpec(memory_space=pl.ANY)], out_specs=tuple([hbm] * (2 * n)),
        input_output_aliases={i: i for i in range(2 * n)},
        compiler_params=pltpu.CompilerParams(has_side_effects=pltpu.SideEffectType.DATAFLOW_SIDE_EFFECTING),
    )(*ins, *lands, send_sems, recv_sems, after)
    return list(res[n:])


def _chip_forward(lands, *, name):
    n = len(lands)

    def body(*refs):
        ins, outs, send_sems, recv_sems = refs[:n], refs[n:2 * n], refs[2 * n], refs[2 * n + 1]
        x, y, c = lax.axis_index("x"), lax.axis_index("y"), lax.axis_index("c")
        copies = []
        for k in range(n):
            for j, (dx, dy) in enumerate(((0, 1), (1, 0), (1, 1))):
                held = 4 * ((x + dx) % 2) + 2 * ((y + dy) % 2) + c
                cp = pltpu.make_async_remote_copy(
                    src_ref=ins[k].at[held], dst_ref=outs[k].at[held], send_sem=send_sems.at[k, j],
                    recv_sem=recv_sems.at[k, j], device_id=(x, y, 1 - c), device_id_type=pl.DeviceIdType.MESH)
                cp.start()
                copies.append(cp)
        for cp in copies:
            cp.wait()

    any_spec = pl.BlockSpec(memory_space=pl.ANY)
    return pl.pallas_call(
        body, name=name, in_specs=[any_spec] * n, out_specs=[any_spec] * n,
        out_shape=[jax.ShapeDtypeStruct(a.shape, a.dtype) for a in lands], input_output_aliases={k: k for k in range(n)},
        scratch_shapes=[pltpu.SemaphoreType.DMA((n, 3)), pltpu.SemaphoreType.DMA((n, 3))],
    )(*lands)


def _adam(w, terms, m, v, *, name):
    n_layers, r, c = w.shape
    tr = min(r, ADAM_ROW_TILE)
    assert r % tr == 0 and len(terms) == n_layers
    steps = r // tr

    def body(w_ref, *rest):
        t_refs, (m_ref, v_ref, g_out, d_out, m_out, v_out) = rest[:n_layers], rest[n_layers:]
        for layer, t_ref in enumerate(t_refs):
            @pl.when(pl.program_id(0) == layer)
            def _(t_ref=t_ref):
                g = t_ref[0].astype(F32)
                for s in range(1, t_ref.shape[0]):
                    g = g + t_ref[s].astype(F32)
                m1 = ADAM_B1 * m_ref[...] + (1.0 - ADAM_B1) * g
                v1 = ADAM_B2 * v_ref[...] + (1.0 - ADAM_B2) * jnp.square(g)
                m_hat = m1 / (1.0 - ADAM_B1 ** ADAM_STEP)
                v_hat = v1 / (1.0 - ADAM_B2 ** ADAM_STEP)
                g_out[...] = g
                d_out[...] = -ADAM_LR * (m_hat / (jnp.sqrt(v_hat) + ADAM_EPS) + ADAM_WD * w_ref[...])
                m_out[...] = m1
                v_out[...] = v1

    def term_spec(layer, t):
        return pl.BlockSpec((t.shape[0], tr, c),
                            lambda l, i: (0, jnp.where(l == layer, i, jnp.where(l < layer, 0, steps - 1)), 0))

    spec = pl.BlockSpec((None, tr, c), lambda l, i: (l, i, 0))
    out = jax.ShapeDtypeStruct(w.shape, F32)
    return pl.pallas_call(
        body, name=name, grid=(n_layers, steps),
        in_specs=[spec] + [term_spec(layer, t) for layer, t in enumerate(terms)] + [spec, spec], out_specs=[spec] * 4,
        out_shape=[out] * 4, compiler_params=_params("arbitrary", "arbitrary"),
    )(w, *terms, m, v)


def _sum_terms(terms, *, name):
    n, _, p = terms.shape

    def body(t_ref, o_ref):
        acc = t_ref[0]
        for s in range(1, n):
            acc = acc + t_ref[s]
        o_ref[...] = acc

    return pl.pallas_call(body, name=name, out_shape=jax.ShapeDtypeStruct((1, p), F32))(terms)


def _lb_logits_grad(dlb, logits, *, name):
    def body(dlb_ref, l_ref, o_ref):
        lb = _lower_bound(l_ref[...])
        d0 = dlb_ref[...] * lb * (1.0 - lb)
        o_ref[...] = jnp.concatenate([d0, -d0], axis=0)

    return pl.pallas_call(body, name=name, out_shape=jax.ShapeDtypeStruct(logits.shape, F32))(dlb, logits)


def _silu_grad(z):
    sg = _sigmoid(z)
    return sg * (1.0 + z * (1.0 - sg))


def _head_norm_gate(o, zg, gn):
    outs = []
    for h in range(HGRN_HEADS):
        sl = slice(h * LANES, (h + 1) * LANES)
        zg_h = zg[:, sl]
        outs.append(_rms(o[:, sl], gn) * (zg_h * _sigmoid(zg_h)))
    return (jnp.concatenate(outs, axis=1),)


def _head_norm_gate_bwd(o, zg, dm, gn):
    do_parts, dzg_parts, dgn = [], [], jnp.zeros((1, LANES), F32)
    for h in range(HGRN_HEADS):
        sl = slice(h * LANES, (h + 1) * LANES)
        o_h, zg_h, dm_h = o[:, sl], zg[:, sl], dm[:, sl]
        gate = zg_h * _sigmoid(zg_h)
        do_h, dgn_h = _rms_bwd(o_h, gn, dm_h * gate)
        dgn = dgn + dgn_h
        do_parts.append(do_h)
        dzg_parts.append(dm_h * _rms(o_h, gn) * _silu_grad(zg_h))
    return jnp.concatenate(do_parts, axis=1), jnp.concatenate(dzg_parts, axis=1), dgn


def _rope_slabs(x, t_c, t_s1, t_s2, transpose):
    fn = _rope_t if transpose else _rope
    return jnp.concatenate(
        [fn(x[:, h * LANES:(h + 1) * LANES], t_c, t_s1, t_s2) for h in range(x.shape[1] // LANES)], axis=1)


def _loss_head(h, tgt, w):
    d = h.shape[1]
    r = lax.rsqrt(jnp.mean(h * h, axis=-1, keepdims=True) + EPS)
    xh = h * r
    err = xh * w - tgt
    loss = 0.5 * jnp.sum(jnp.mean(err * err, axis=-1, keepdims=True), axis=0, keepdims=True)
    dy = err / d
    dxh = dy * w
    dh = r * (dxh - xh * jnp.mean(dxh * xh, axis=-1, keepdims=True))
    return dh, dh, jnp.sum(dy * xh, axis=0, keepdims=True), jnp.broadcast_to(loss, (1, LANES))


def _mlp_fwd(h, norm, w_up, w_down, tag, loss_head=None):
    d = h.shape[1]

    def up(x, g, wu):
        x_n = _rms(x, g).astype(BF16)
        return x_n, jnp.concatenate([jnp.square(jnp.maximum(_dot(x_n, wu[j], _NN), 0.0)) for j in range(wu.shape[0])],
                                    axis=1)

    xn, act = _rowcall(up, [h], [norm, w_up], [(d, BF16), (w_up.shape[0] * w_up.shape[2], BF16)], [], tr=512,
                       name=f"{tag}_up")
    if callable(w_down):
        w_down = w_down(act)
    if loss_head is None:
        return _mm(act, w_down, mode="nn", add=h, name=f"{tag}_down"), (h, xn, act)
    tgt, final_norm = loss_head

    def down_and_loss(a, res, t, wd, g):
        return _loss_head(res + _dot(a, wd, _NN), t, g)

    return _rowcall(down_and_loss, [act, h, tgt], [w_down, final_norm], [(d, F32), (d, BF16)], [d, LANES],
                    name=f"{tag}_down_loss"), (h, xn, act)


def _mlp_bwd(dh_out, dh_out_bf, saved, norm, w_up, w_down, tag, after=None):
    h, xn, act = saved
    d = h.shape[1]
    du = _mm(dh_out_bf, w_down, mode="nt", relu2_of=act, out_dtype=BF16, after=after, name=f"{tag}_bwd_du")
    dw_down = _mm(act, dh_out_bf, mode="tn", name=f"{tag}_bwd_wdown")
    dw_up = _mm(xn, du, mode="tn", col_shards=w_up.shape[0], name=f"{tag}_bwd_wup")

    def up_norm_bwd(x, d_u, dres, g, wu):
        cols = wu.shape[2]
        dxn = _dot(d_u[:, :cols], wu[0], _NT)
        for j in range(1, wu.shape[0]):
            dxn = dxn + _dot(d_u[:, j * cols:(j + 1) * cols], wu[j], _NT)
        dx, dw = _rms_bwd(x, g, dxn)
        return dx + dres, dx + dres, dw

    dh, dh_bf, dnorm = _rowcall(up_norm_bwd, [h, du, dh_out], [norm, w_up], [(d, F32), (d, BF16)], [d], tr=512,
                                name=f"{tag}_bwd_dxn")
    return dh, dh_bf, dnorm, dw_up, dw_down


def _row_major(g):
    return g.reshape(g.shape[0] * g.shape[1], g.shape[2])


def _col_major(g):
    return jnp.transpose(g, (1, 0, 2)).reshape(g.shape[1], g.shape[0] * g.shape[2])


def _col_terms(dw):
    k, n = dw.shape
    return jnp.transpose(dw.reshape(k, N_DEV, n // N_DEV), (1, 0, 2))


def _row_terms(dw):
    return dw.reshape(N_DEV, dw.shape[0] // N_DEV, dw.shape[1])


def kernel(x, hgrn_norm, hgrn_w_q, hgrn_w_f, hgrn_w_i, hgrn_w_g, hgrn_g_norm, hgrn_w_o, hgrn_lb_logits, mla_norm, mla_w_dq, mla_q_norm, mla_w_uq, mla_w_o, kv_in_norm, kv_w_dkv, kv_norm, kv_w_uk, kv_w_uv, mlp_norm, mlp_w_up, mlp_w_down, final_norm, loss_target, m_hgrn_norm, m_hgrn_w_q, m_hgrn_w_f, m_hgrn_w_i, m_hgrn_w_g, m_hgrn_g_norm, m_hgrn_w_o, m_hgrn_lb_logits, m_mla_norm, m_mla_w_dq, m_mla_q_norm, m_mla_w_uq, m_mla_w_o, m_kv_in_norm, m_kv_w_dkv, m_kv_norm, m_kv_w_uk, m_kv_w_uv, m_mlp_norm, m_mlp_w_up, m_mlp_w_down, m_final_norm, v_hgrn_norm, v_hgrn_w_q, v_hgrn_w_f, v_hgrn_w_i, v_hgrn_w_g, v_hgrn_g_norm, v_hgrn_w_o, v_hgrn_lb_logits, v_mla_norm, v_mla_w_dq, v_mla_q_norm, v_mla_w_uq, v_mla_w_o, v_kv_in_norm, v_kv_w_dkv, v_kv_norm, v_kv_w_uk, v_kv_w_uv, v_mlp_norm, v_mlp_w_up, v_mlp_w_down, v_final_norm):
    given = dict(locals())
    weight_names = ["hgrn_norm", "hgrn_w_q", "hgrn_w_f", "hgrn_w_i", "hgrn_w_g", "hgrn_g_norm", "hgrn_w_o",
                    "hgrn_lb_logits", "mla_norm", "mla_w_dq", "mla_q_norm", "mla_w_uq", "mla_w_o", "kv_in_norm",
                    "kv_w_dkv", "kv_norm", "kv_w_uk", "kv_w_uv", "mlp_norm", "mlp_w_up", "mlp_w_down", "final_norm"]
    me = 4 * lax.axis_index("x") + 2 * lax.axis_index("y") + lax.axis_index("c")
    xs, tgt = x[0], loss_target[0]
    seq, d_model = xs.shape
    n_heads, hd = MLA_HEADS, LANES

    big_local = {
        "hgrn_w_q": hgrn_w_q[0], "hgrn_w_f": hgrn_w_f[0], "hgrn_w_i": hgrn_w_i[0], "hgrn_w_g": hgrn_w_g[0],
        "hgrn_w_o": hgrn_w_o[0], "mla_w_dq": mla_w_dq[0], "mla_w_uq": mla_w_uq[0], "mla_w_o": mla_w_o[0],
        "kv_w_dkv": kv_w_dkv, "kv_w_uk": kv_w_uk, "kv_w_uv": kv_w_uv,
        "mlp_w_up0": mlp_w_up[0], "mlp_w_up1": mlp_w_up[1], "mlp_w_down0": mlp_w_down[0], "mlp_w_down1": mlp_w_down[1],
    }
    big_names = list(big_local)
    col_sharded = {"mla_w_uq", "kv_w_uk", "kv_w_uv"}
    shard_major = {"mlp_w_up0", "mlp_w_up1"}
    vec_local = jnp.concatenate([hgrn_norm, hgrn_lb_logits], axis=0)
    first_names = ["hgrn_w_q", "hgrn_w_f", "hgrn_w_i"]
    proj_names = first_names + ["hgrn_w_g"]
    later_names = {"hgrn_o": ["hgrn_w_g", "hgrn_w_o"], "up0": ["mlp_w_up0"], "down0": ["mlp_w_down0"],
                   "mla": ["kv_w_dkv", "kv_w_uk", "kv_w_uv", "mla_w_dq", "mla_w_uq", "mla_w_o"],
                   "mlp1": ["mlp_w_up1", "mlp_w_down1"]}

    def unshard(names, arrays):
        return {k: (a if k in shard_major else _col_major(a) if k in col_sharded else _row_major(a))
                for k, a in zip(names, arrays)}

    two_level = {"down0", "mla"}
    first_state, token = _exchange_start([big_local[k].astype(BF16) for k in first_names] + [vec_local], scatter=False,
                                         chip_level=True, name="gather_first_start")
    gather_state = {}
    for tag, names in later_names.items():
        gather_state[tag], token = _exchange_start([big_local[k].astype(BF16) for k in names], scatter=False,
                                                   chip_level=tag in two_level, after=token, name=f"gather_{tag}_start")

    def gather_wait(tag, after):
        landed = _exchange_wait(gather_state[tag], after, name=f"gather_{tag}_wait")
        if tag in two_level:
            landed = _chip_forward(landed, name=f"gather_{tag}_forward")
        w.update(unshard(later_names[tag], landed))
        return [w[k] for k in later_names[tag]]

    gathered = _chip_forward(_exchange_wait(first_state, token, name="gather_first_wait"), name="gather_first_forward")
    w = unshard(first_names, gathered[:-1])
    vec_full = jnp.transpose(gathered[-1], (1, 0, 2)).reshape(3, d_model)
    hgrn_norm_full, lb_logits_full = vec_full[0:1], vec_full[1:3]
    t_c, t_s1, t_s2 = _rope_tables(seq)
    kv_lora = kv_w_uk.shape[0]

    def hgrn_proj(a, g, *weights):
        xn = _rms(a, g).astype(BF16)
        return (xn, *[_dot(xn, wt, _NN) for wt in weights])

    xn0, zq, zf, zi = _rowcall(hgrn_proj, [xs], [hgrn_norm_full] + [w[k] for k in first_names],
                               [(d_model, BF16)] + [(d_model, F32)] * 3, [], tr=512, name="hgrn_proj")
    o_rec, states = _hgrn_fwd(zq, zf, zi, lb_logits_full, name="hgrn_fwd")
    gather_wait("hgrn_o", o_rec)

    def gate_out(o, x_n, res, gn, wg, wo):
        z = _dot(x_n, wg, _NN)
        m = _head_norm_gate(o, z, gn)[0].astype(BF16)
        return z, m, res + _dot(m, wo, _NN)

    zg, mixed, h1 = _rowcall(gate_out, [o_rec, xn0, xs], [hgrn_g_norm, w["hgrn_w_g"], w["hgrn_w_o"]],
                             [(d_model, F32), (d_model, BF16), (d_model, F32)], [], name="hgrn_gate_out")
    h2, mlp0_saved = _mlp_fwd(h1, mlp_norm[0:1], gather_wait("up0", h1)[0], lambda act: gather_wait("down0", act)[0],
                              "mlp0")
    gather_wait("mla", h2)
    w_uq3 = w["mla_w_uq"].reshape(-1, n_heads, MLA_NOPE + MLA_ROPE)
    w_uq_nope = w_uq3[:, :, :MLA_NOPE].reshape(-1, n_heads * hd)
    w_uq_rope = jnp.pad(w_uq3[:, :, MLA_NOPE:], ((0, 0), (0, 0), (0, hd - MLA_ROPE))).reshape(-1, n_heads * hd)
    w_dkv_pad = jnp.pad(w["kv_w_dkv"], ((0, 0), (0, kv_lora + hd - w["kv_w_dkv"].shape[1])))

    q_lora, qk_cols = w["mla_w_dq"].shape[1], n_heads * hd

    def mla_qkv(a, tc, ts1, ts2, g_kv_in, g_mla, g_q, g_kv, wdq, wn, wr, wdkv, wuk, wuv):
        h_n, x_n = _rms(a, g_kv_in).astype(BF16), _rms(a, g_mla).astype(BF16)
        cq = _dot(x_n, wdq, _NN)
        cq_n = _rms(cq, g_q).astype(BF16)
        q_nope = _dot(cq_n, wn, _NN) * Q_PRESCALE
        q_rope = _rope_slabs(_dot(cq_n, wr, _NN) * Q_PRESCALE, tc, ts1, ts2, False)
        c_all = _dot(h_n, wdkv, _NN)
        lat = _rms(c_all[:, :kv_lora], g_kv).astype(BF16)
        return (h_n, x_n, cq, cq_n, q_nope, q_rope, c_all, lat, _rope(c_all[:, kv_lora:], tc, ts1, ts2),
                _dot(lat, wuk, _NN), _dot(lat, wuv, _NN))

    hn, xn2, cq_pre, c_q, qn, qr, ckr, c_kv, kr, kn, vv = _rowcall(
        mla_qkv, [h2, t_c, t_s1, t_s2],
        [kv_in_norm[None, :], mla_norm, mla_q_norm, kv_norm[None, :], w["mla_w_dq"], w_uq_nope, w_uq_rope, w_dkv_pad,
         w["kv_w_uk"], w["kv_w_uv"]],
        [(d_model, BF16), (d_model, BF16), (q_lora, F32), (q_lora, BF16), (qk_cols, BF16), (qk_cols, BF16),
         (kv_lora + hd, F32), (kv_lora, BF16), (hd, BF16), (qk_cols, BF16), (qk_cols, BF16)], [], tr=512, name="mla_qkv")
    o_att, lse = _attn_fwd(qn, qr, kn, kr, vv, name="attn_fwd")
    h3 = _mm(o_att, w["mla_w_o"], mode="nn", add=h2, name="attn_out")
    gather_wait("mlp1", h3)
    (dh4, dh4_bf, g_final_norm, loss_part), mlp1_saved = _mlp_fwd(
        h3, mlp_norm[1:2], w["mlp_w_up1"], w["mlp_w_down1"], "mlp1", loss_head=(tgt, final_norm[None, :]))

    g = {}
    groups = {"mlp1": ["mlp_w_up1", "mlp_w_down1"],
              "mla": ["mla_w_o", "mla_w_uq", "mla_w_dq", "kv_w_uk", "kv_w_uv", "kv_w_dkv"],
              "mlp0": ["mlp_w_up0", "mlp_w_down0"],
              "hgrn_out": ["hgrn_w_o", "hgrn_w_g"],
              "hgrn_in": ["hgrn_w_q", "hgrn_w_f", "hgrn_w_i"]}
    scatter_state = {}

    def scatter_start(tag, after=None):
        scatter_state[tag], tok = _exchange_start(
            [g[k] if k in shard_major else (_col_terms if k in col_sharded else _row_terms)(g[k]) for k in groups[tag]],
            scatter=True, after=after,
            name=f"scatter_{tag}_start")
        return tok

    dh3, dh3_bf, g_mlp_norm1, g["mlp_w_up1"], g["mlp_w_down1"] = _mlp_bwd(
        dh4, dh4_bf, mlp1_saved, mlp_norm[1:2], w["mlp_w_up1"], w["mlp_w_down1"], "mlp1")
    def attn_out_bwd(dres, o, wo):
        d_o = _dot(dres, wo, _NT).astype(BF16)
        prod = d_o.astype(F32) * o.astype(F32)
        return d_o, jnp.concatenate([jnp.broadcast_to(jnp.sum(prod[:, h * hd:(h + 1) * hd], axis=1, keepdims=True),
                                                      (prod.shape[0], hd)) for h in range(n_heads)], axis=1)

    d_oatt, delta = _rowcall(attn_out_bwd, [dh3_bf, o_att], [w["mla_w_o"]], [(qk_cols, BF16), (qk_cols, F32)], [],
                             after=scatter_start("mlp1"), name="attn_out_bwd_x")
    g["mla_w_o"] = _mm(o_att, dh3_bf, mode="tn", name="attn_out_bwd_w")
    dqn, dqr, dkn, dvv, dkr = _attn_bwd(qn, qr, kn, kr, vv, d_oatt, lse, delta, name="attn_bwd")

    def q_path_bwd(cq, cq_n, x_n, d_qn, d_qr, tc, ts1, ts2, g_q, wdq, wn, wr):
        d_qn, d_qr = d_qn.astype(BF16), _rope_slabs(d_qr, tc, ts1, ts2, True).astype(BF16)
        d_cq, d_gq = _rms_bwd(cq, g_q, _dot(d_qn, wn, _NT) + _dot(d_qr, wr, _NT))
        d_cq = d_cq.astype(BF16)
        return _dot(d_cq, wdq, _NT), d_gq, _dot(x_n, d_cq, _TN), _dot(cq_n, d_qn, _TN), _dot(cq_n, d_qr, _TN)

    dxn2, g_q_norm, g_dq, g_uq_nope, g_uq_rope = _rowcall(
        q_path_bwd, [cq_pre, c_q, xn2, dqn, dqr, t_c, t_s1, t_s2], [mla_q_norm, w["mla_w_dq"], w_uq_nope, w_uq_rope],
        [(d_model, F32)], [q_lora, (d_model, q_lora), (q_lora, qk_cols), (q_lora, qk_cols)], tr=512, name="mla_q_bwd")
    g["mla_w_dq"] = g_dq.astype(GRAD_WIRE_DTYPE)
    g["mla_w_uq"] = jnp.concatenate([g_uq_nope.reshape(q_lora, n_heads, hd),
                                     g_uq_rope.reshape(q_lora, n_heads, hd)[:, :, :MLA_ROPE]],
                                    axis=2).reshape(q_lora, -1).astype(GRAD_WIRE_DTYPE)

    def kv_path_bwd(c_all, lat, h_n, d_kn, d_v, d_kr_heads, tc, ts1, ts2, a, d_xn2, dres,
                    g_kv, g_kv_in, g_mla, wdkv, wuk, wuv):
        d_lat, d_gkv = _rms_bwd(c_all[:, :kv_lora], g_kv, _dot(d_kn, wuk, _NT) + _dot(d_v, wuv, _NT))
        d_kr = d_kr_heads[:, :hd]
        for h in range(1, n_heads):
            d_kr = d_kr + d_kr_heads[:, h * hd:(h + 1) * hd]
        d_all = jnp.concatenate([d_lat, _rope_t(d_kr, tc, ts1, ts2)], axis=1).astype(BF16)
        dx1, d_gkv_in = _rms_bwd(a, g_kv_in, _dot(d_all, wdkv, _NT))
        dx2, d_gmla = _rms_bwd(a, g_mla, d_xn2)
        d_a = dx1 + dx2 + dres
        return (d_a, d_a, d_gkv, d_gkv_in, d_gmla, _dot(h_n, d_all, _TN), _dot(lat, d_kn, _TN), _dot(lat, d_v, _TN))

    dh2, dh2_bf, g_kv_norm, g_kv_in_norm, g_mla_norm, g_dkv, g_uk, g_uv = _rowcall(
        kv_path_bwd, [ckr, c_kv, hn, dkn, dvv, dkr, t_c, t_s1, t_s2, h2, dxn2, dh3],
        [kv_norm[None, :], kv_in_norm[None, :], mla_norm, w_dkv_pad, w["kv_w_uk"], w["kv_w_uv"]],
        [(d_model, F32), (d_model, BF16)],
        [kv_lora, d_model, d_model, (d_model, kv_lora + hd), (kv_lora, qk_cols), (kv_lora, qk_cols)], name="mla_kv_bwd")
    g["kv_w_dkv"] = g_dkv[:, :kv_w_dkv.shape[1]].astype(GRAD_WIRE_DTYPE)
    g["kv_w_uk"], g["kv_w_uv"] = g_uk.astype(GRAD_WIRE_DTYPE), g_uv.astype(GRAD_WIRE_DTYPE)
    dh1, dh1_bf, g_mlp_norm0, g["mlp_w_up0"], g["mlp_w_down0"] = _mlp_bwd(
        dh2, dh2_bf, mlp0_saved, mlp_norm[0:1], w["mlp_w_up0"], w["mlp_w_down0"], "mlp0", after=scatter_start("mla"))

    g["hgrn_w_o"] = _mm(mixed, dh1_bf, mode="tn", after=scatter_start("mlp0"), name="hgrn_out_bwd_w")
    do_rec, dzg, g_g_norm = _rowcall(
        lambda dres, o, z, wo, gn: _head_norm_gate_bwd(o, z, _dot(dres, wo, _NT), gn), [dh1_bf, o_rec, zg],
        [w["hgrn_w_o"], hgrn_g_norm], [(d_model, F32), (d_model, BF16)], [hd], name="hgrn_gate_out_bwd")
    g["hgrn_w_g"] = _mm(xn0, dzg, mode="tn", name="hgrn_w_g_bwd_w")
    dzq, dzf, dzi, g_lb = _hgrn_bwd(zq, zf, zi, lb_logits_full, states, do_rec, scatter_start("hgrn_out"),
                                    name="hgrn_bwd")
    for nm, dz in (("hgrn_w_q", dzq), ("hgrn_w_f", dzf), ("hgrn_w_i", dzi)):
        g[nm] = _mm(xn0, dz, mode="tn", name=f"{nm}_bwd_w")

    def hgrn_proj_bwd(a, dres, *rest):
        dzs, gw, weights = rest[:4], rest[4], rest[5:]
        dxn = _dot(dzs[0], weights[0], _NT)
        for dz, wt in zip(dzs[1:], weights[1:]):
            dxn = dxn + _dot(dz, wt, _NT)
        dx, dw = _rms_bwd(a, gw, dxn)
        return dx + dres, dw

    grad_x, g_hgrn_norm = _rowcall(hgrn_proj_bwd, [xs, dh1, dzq, dzf, dzi, dzg],
                                   [hgrn_norm_full] + [w[k] for k in proj_names], [(d_model, F32)], [d_model],
                                   tr=512, name="hgrn_proj_bwd")

    small_parts = [g_hgrn_norm, g_lb, g_g_norm, g_mla_norm, g_q_norm, g_kv_in_norm, g_kv_norm, g_mlp_norm0,
                   g_mlp_norm1, g_final_norm, loss_part]
    small_sizes = [p.shape[1] for p in small_parts]
    small_terms = _exchange([jnp.concatenate(small_parts, axis=1)], scatter=False, name="gather_small")[0]
    small_sum = _sum_terms(small_terms, name="sum_small")
    last = scatter_start("hgrn_in", after=small_sum)
    offs = [0]
    for sz in small_sizes:
        offs.append(offs[-1] + sz)
    (s_hgrn_norm, s_lb, s_g_norm, s_mla_norm, s_q_norm, s_kv_in_norm, s_kv_norm, s_mlp_norm0, s_mlp_norm1, s_final_norm,
     s_loss) = [small_sum[:, a:b] for a, b in zip(offs[:-1], offs[1:])]
    shard = hgrn_norm.shape[1]
    g_lb_logits = _lb_logits_grad(lax.dynamic_slice_in_dim(s_lb, me * shard, shard, axis=1), hgrn_lb_logits,
                                  name="lb_logits_grad")
    loss = s_loss[0, 0]

    res, layer_terms = {}, {}

    def update(k, term_list):
        shape = given[k].shape
        as_layers = (len(term_list), shape[-2], shape[-1])
        upd = _adam(given[k].reshape(as_layers), term_list, given["m_" + k].reshape(as_layers),
                    given["v_" + k].reshape(as_layers), name=f"adam_{k}")
        res[k] = [o.reshape(shape) for o in upd]
        return upd[0]

    for tag, names in groups.items():
        for k, t in zip(names, _exchange_wait(scatter_state[tag], last, name=f"scatter_{tag}_wait")):
            if k.startswith("mlp_w_"):
                layer_terms.setdefault(k[:-1], {})[int(k[-1])] = t
                if len(layer_terms[k[:-1]]) == 2:
                    last = update(k[:-1], [layer_terms[k[:-1]][0], layer_terms[k[:-1]][1]])
            else:
                last = update(k, [t])

    small_grads = {
        "hgrn_norm": lax.dynamic_slice_in_dim(s_hgrn_norm, me * shard, shard, axis=1),
        "hgrn_g_norm": s_g_norm, "hgrn_lb_logits": g_lb_logits, "mla_norm": s_mla_norm, "mla_q_norm": s_q_norm,
        "kv_in_norm": s_kv_in_norm, "kv_norm": s_kv_norm,
        "mlp_norm": jnp.concatenate([s_mlp_norm0, s_mlp_norm1], axis=0), "final_norm": s_final_norm,
    }
    small_names = list(small_grads)

    def flat(a):
        return a.reshape(1, -1)

    packed = [jnp.concatenate([flat(src[pre + k]) for k in small_names], axis=1)
              for src, pre in ((given, ""), (small_grads, ""), (given, "m_"), (given, "v_"))]
    small_out = _adam(packed[0][None], [packed[1][None]], packed[2][None], packed[3][None], name="adam_small")
    off = 0
    for k in small_names:
        size = given[k].size
        res[k] = [o[0, :, off:off + size].reshape(given[k].shape) for o in small_out]
        off += size

    outs = [loss, grad_x[None]]
    for i in range(4):
        outs += [res[k][i] for k in weight_names]
    return tuple(outs)
```

```python
import functools

import jax
import jax.numpy as jnp
from jax import lax
from jax.experimental import pallas as pl
from jax.experimental.pallas import tpu as pltpu

F32 = jnp.float32
BF16 = jnp.bfloat16

EPS = 1e-6
LANES = 128
N_DEV = 8
V7X_VMEM_LIMIT_BYTES = 56 << 20
MM_PIPELINE_BYTES = 30 << 20
MM_ROW_TILE = 512
ADAM_ROW_TILE = 256
GRAD_WIRE_DTYPE = BF16

HGRN_HEADS = 8
HGRN_CHUNK = 64
HGRN_SUB = 16
HGRN_HEADS_PER_STEP = 8
HGRN_CHUNKS_PER_STEP = 4
EXP_CLAMP = 80.0
MLA_HEADS = 16
MLA_NOPE = 128
MLA_ROPE = 64
ROPE_THETA = 10000.0
ATTN_SCALE = (MLA_NOPE + MLA_ROPE) ** -0.5

ADAM_LR = 0.001
ADAM_B1 = 0.9
ADAM_B2 = 0.999
ADAM_EPS = 1e-08
ADAM_WD = 0.01
ADAM_STEP = 10

_NN = ((1,), (0,))
_NT = ((1,), (1,))
_TN = ((0,), (0,))


def _params(*sem):
    return pltpu.CompilerParams(dimension_semantics=sem, vmem_limit_bytes=V7X_VMEM_LIMIT_BYTES)


def _dot(a, b, dims):
    return lax.dot_general(a.astype(BF16), b.astype(BF16), (dims, ((), ())), preferred_element_type=F32)


def _dot_f32(a, b, dims=_NN):
    return lax.dot_general(a, b, (dims, ((), ())), precision=lax.Precision.HIGH, preferred_element_type=F32)


def _sigmoid(x):
    return 1.0 / (1.0 + jnp.exp(-x))


def _rms(x, w):
    r = lax.rsqrt(jnp.mean(x * x, axis=-1, keepdims=True) + EPS)
    return x * r * w


def _rms_bwd(x, w, dy):
    r = lax.rsqrt(jnp.mean(x * x, axis=-1, keepdims=True) + EPS)
    xh = x * r
    dw = jnp.sum(dy * xh, axis=0, keepdims=True)
    dxh = dy * w
    dx = r * (dxh - xh * jnp.mean(dxh * xh, axis=-1, keepdims=True))
    return dx, dw


def _mm_tiles(m, n, k, a_bytes, b_bytes, out_tile_bytes):
    tm = min(m, MM_ROW_TILE)
    for tn in (n, 2048, 1024, 512, 256, LANES):
        if tn <= n and n % tn == 0:
            if 2 * (tm * k * a_bytes + k * tn * b_bytes + tm * tn * out_tile_bytes) <= MM_PIPELINE_BYTES:
                return tm, tn
    return tm, min(n, LANES)


def _mm(a, b, *, mode, name, out_dtype=None, add=None, after=None, col_shards=None):
    if mode == "nn":
        (m, k), (k2, n) = a.shape, b.shape
    elif mode == "nt":
        (m, k), (n, k2) = a.shape, b.shape
    else:
        (k, m), (k2, n) = a.shape, b.shape
    assert k == k2, (name, a.shape, b.shape)
    if out_dtype is None:
        out_dtype = GRAD_WIRE_DTYPE if mode == "tn" else F32
    tile_bytes = (0 if add is None else add.dtype.itemsize) + jnp.dtype(out_dtype).itemsize
    tm, tn = _mm_tiles(m, n, k, a.dtype.itemsize, b.dtype.itemsize, tile_bytes)
    if col_shards is not None:
        assert add is None
        tn = n // col_shards
    assert m % tm == 0 and n % tn == 0, (name, m, n)
    dims = {"nn": _NN, "nt": _NT, "tn": _TN}[mode]
    a_spec = pl.BlockSpec((k, tm), lambda i, j: (0, i)) if mode == "tn" else pl.BlockSpec((tm, k), lambda i, j: (i, 0))
    b_spec = pl.BlockSpec((tn, k), lambda i, j: (j, 0)) if mode == "nt" else pl.BlockSpec((k, tn), lambda i, j: (0, j))
    o_spec = pl.BlockSpec((tm, tn), lambda i, j: (i, j))
    operands, in_specs = [a, b], [a_spec, b_spec]
    if add is not None:
        assert add.shape == (m, n), (name, add.shape)
        operands.append(add)
        in_specs.append(o_spec)
    n_in = len(operands)
    if after is not None:
        operands.append(after)
        in_specs.append(pl.BlockSpec(memory_space=pl.ANY))
    out_shape = jax.ShapeDtypeStruct((m, n), out_dtype)
    if col_shards is not None:
        out_shape = jax.ShapeDtypeStruct((col_shards, m, tn), out_dtype)
        o_spec = pl.BlockSpec((None, tm, tn), lambda i, j: (j, i, 0))

    def body(*refs):
        acc = _dot(refs[0][...], refs[1][...], dims)
        extras, outs = refs[2:n_in], refs[len(operands):]
        if add is not None:
            acc = acc + extras[0][...]
        outs[0][...] = acc.astype(out_dtype)

    return pl.pallas_call(
        body, name=name, grid=(m // tm, n // tn), in_specs=in_specs, out_specs=o_spec, out_shape=out_shape,
        compiler_params=_params("parallel", "parallel"),
    )(*operands)


def _rowcall(fn, rows, consts, outs, accs, *, name, tr=256, after=None):
    s = rows[0].shape[0]
    tr = min(tr, s)
    assert s % tr == 0
    n_out = len(outs)
    accs = [(1, a) if isinstance(a, int) else a for a in accs]
    in_specs = [pl.BlockSpec((tr, r.shape[1]), lambda i: (i, 0)) for r in rows]
    in_specs += [pl.BlockSpec(c.shape, lambda i, nd=c.ndim: (0,) * nd) for c in consts]
    out_shape = [jax.ShapeDtypeStruct((s, w), dt) for w, dt in outs] + [jax.ShapeDtypeStruct(a, F32) for a in accs]
    out_specs = [pl.BlockSpec((tr, w), lambda i: (i, 0)) for w, _ in outs] + [pl.BlockSpec(a, lambda i: (0, 0)) for a in accs]
    n_in = len(rows) + len(consts)

    def body(*refs):
        res = fn(*[r[...] for r in refs[:n_in]])
        out_refs = refs[n_in + (after is not None):]
        for ref, val in zip(out_refs[:n_out], res[:n_out]):
            ref[...] = val.astype(ref.dtype)
        i = pl.program_id(0)
        for ref, val in zip(out_refs[n_out:], res[n_out:]):
            @pl.when(i == 0)
            def _(ref=ref, val=val):
                ref[...] = val

            @pl.when(i > 0)
            def _(ref=ref, val=val):
                ref[...] += val

    behind = [] if after is None else [after]
    return pl.pallas_call(
        body, name=name, grid=(s // tr,), in_specs=in_specs + [pl.BlockSpec(memory_space=pl.ANY)] * len(behind),
        out_specs=out_specs, out_shape=out_shape, compiler_params=_params("arbitrary" if accs else "parallel"),
    )(*rows, *consts, *behind)


def _rope_tables(seq):
    half = MLA_ROPE // 2
    inv_freq = ROPE_THETA ** (-jnp.arange(half, dtype=F32) / half)
    ang = jnp.arange(seq, dtype=F32)[:, None] * inv_freq[None, :]
    cos, sin, zero = jnp.cos(ang), jnp.sin(ang), jnp.zeros((seq, half), F32)
    t_c = jnp.concatenate([cos, cos, zero, zero], axis=1)
    t_s1 = jnp.concatenate([-sin, zero, zero, zero], axis=1)
    t_s2 = jnp.concatenate([zero, sin, zero, zero], axis=1)
    return t_c, t_s1, t_s2


def _rope(slab, t_c, t_s1, t_s2):
    return slab * t_c + pltpu.roll(slab, 96, 1) * t_s1 + pltpu.roll(slab, 32, 1) * t_s2


def _rope_t(d, t_c, t_s1, t_s2):
    return d * t_c + pltpu.roll(d * t_s1, 32, 1) + pltpu.roll(d * t_s2, 96, 1)


def _lower_bound(logits):
    l0, l1 = logits[0:1, :], logits[1:2, :]
    mx = jnp.maximum(l0, l1)
    e0, e1 = jnp.exp(l0 - mx), jnp.exp(l1 - mx)
    return e0 / (e0 + e1)


def _tri(n, lower):
    row = lax.broadcasted_iota(jnp.int32, (n, n), 0)
    col = lax.broadcasted_iota(jnp.int32, (n, n), 1)
    return (row >= col) if lower else (row <= col)


def _hgrn_fwd(zq, zf, zi, lb_logits, *, name):
    s, d = zq.shape
    h_n, c, hp, cps = d // LANES, HGRN_CHUNK, HGRN_HEADS_PER_STEP, HGRN_CHUNKS_PER_STEP
    nc = s // c

    def body(zq_ref, zf_ref, zi_ref, lb_ref, o_ref, st_ref, state_sc, b_sc):
        @pl.when(pl.program_id(1) == 0)
        def _():
            state_sc[...] = jnp.zeros_like(state_sc)

        lower = _tri(c, True)
        lower_f = lower.astype(F32)
        hs, pairs = range(hp), [(cc, hh) for cc in range(cps) for hh in range(hp)]
        sls = [slice(hh * LANES, (hh + 1) * LANES) for hh in hs]
        rws = [slice(cc * c, (cc + 1) * c) for cc in range(cps)]
        lb = [_lower_bound(lb_ref[:, sl]) for sl in sls]
        zq_v = {p: zq_ref[rws[p[0]], sls[p[1]]] for p in pairs}
        q = {p: zq_v[p] * _sigmoid(zq_v[p]) for p in pairs}
        f = {p: lb[p[1]] + (1.0 - lb[p[1]]) * _sigmoid(zf_ref[rws[p[0]], sls[p[1]]]) for p in pairs}
        k = {p: 1.0 - f[p] for p in pairs}
        v = {p: zi_ref[rws[p[0]], sls[p[1]]] for p in pairs}
        b = {p: _dot_f32(lower_f, jnp.log(f[p])) for p in pairs}
        for p in pairs:
            b_sc[p[0], p[1]] = b[p]
        qe = {p: q[p] * jnp.exp(b[p]) for p in pairs}
        scores = {p: [] for p in pairs}
        for i in range(c // HGRN_SUB):
            lo = i * HGRN_SUB
            for p in pairs:
                ref = b_sc[p[0], p[1], lo - 1:lo, :] if i > 0 else jnp.zeros((1, LANES), F32)
                qt = q[p][lo:lo + HGRN_SUB, :] * jnp.exp(b[p][lo:lo + HGRN_SUB, :] - ref)
                dec = jnp.exp(jnp.minimum(ref - b[p], EXP_CLAMP))
                scores[p].append(_dot(qt, k[p] * dec, _NT))
        o_intra = {p: _dot(jnp.where(lower, jnp.concatenate(scores[p], axis=0), 0.0), v[p], _NN) for p in pairs}
        bl = {p: b_sc[p[0], p[1], c - 1:c, :] for p in pairs}
        k_end = {p: k[p] * jnp.exp(bl[p] - b[p]) for p in pairs}
        state = [state_sc[hh] for hh in hs]
        for cc in range(cps):
            for hh in hs:
                st_ref[hh, cc] = state[hh]
                o_ref[rws[cc], sls[hh]] = _dot(qe[cc, hh], state[hh], _NT) + o_intra[cc, hh]
            state = [state[hh] * jnp.exp(bl[cc, hh]) + _dot(v[cc, hh], k_end[cc, hh], _TN) for hh in hs]
        for hh in hs:
            state_sc[hh] = state[hh]

    tile = pl.BlockSpec((cps * c, hp * LANES), lambda h, i: (i, h))
    return pl.pallas_call(
        body, name=name, grid=(h_n // hp, nc // cps),
        in_specs=[tile, tile, tile, pl.BlockSpec((2, hp * LANES), lambda h, i: (0, h))],
        out_specs=[tile, pl.BlockSpec((hp, cps, LANES, LANES), lambda h, i: (h, i, 0, 0))],
        out_shape=[jax.ShapeDtypeStruct((s, d), F32), jax.ShapeDtypeStruct((h_n, nc, LANES, LANES), F32)],
        scratch_shapes=[pltpu.VMEM((hp, LANES, LANES), F32), pltpu.VMEM((cps, hp, c, LANES), F32)],
        compiler_params=_params("parallel", "arbitrary"),
    )(zq, zf, zi, lb_logits)


def _hgrn_bwd(zq, zf, zi, lb_logits, states, do, after, *, name):
    s, d = zq.shape
    h_n, c, hp, cps = d // LANES, HGRN_CHUNK, HGRN_HEADS_PER_STEP, HGRN_CHUNKS_PER_STEP
    nc = s // c
    n_steps = nc // cps

    def body(zq_ref, zf_ref, zi_ref, lb_ref, st_ref, do_ref, _, dzq_ref, dzf_ref, dzi_ref, dlb_ref, dstate_sc, b_sc):
        @pl.when(pl.program_id(1) == 0)
        def _():
            dstate_sc[...] = jnp.zeros_like(dstate_sc)
            dlb_ref[...] = jnp.zeros_like(dlb_ref)

        lower, upper = _tri(c, True), _tri(c, False).astype(F32)
        lower_f = lower.astype(F32)
        last_row = lax.broadcasted_iota(jnp.int32, (c, LANES), 0) == c - 1
        hs, pairs = range(hp), [(cc, hh) for cc in range(cps) for hh in range(hp)]
        sls = [slice(hh * LANES, (hh + 1) * LANES) for hh in hs]
        rws = [slice(cc * c, (cc + 1) * c) for cc in range(cps)]
        lb = [_lower_bound(lb_ref[:, sl]) for sl in sls]
        zq_v = {p: zq_ref[rws[p[0]], sls[p[1]]] for p in pairs}
        sq = {p: _sigmoid(zq_v[p]) for p in pairs}
        q = {p: zq_v[p] * sq[p] for p in pairs}
        sf = {p: _sigmoid(zf_ref[rws[p[0]], sls[p[1]]]) for p in pairs}
        f = {p: lb[p[1]] + (1.0 - lb[p[1]]) * sf[p] for p in pairs}
        k = {p: 1.0 - f[p] for p in pairs}
        v = {p: zi_ref[rws[p[0]], sls[p[1]]] for p in pairs}
        d_o = {p: do_ref[rws[p[0]], sls[p[1]]] for p in pairs}
        b = {p: _dot_f32(lower_f, jnp.log(f[p])) for p in pairs}
        s0t = {p: st_ref[p[1], p[0]] for p in pairs}
        for p in pairs:
            b_sc[p[0], p[1]] = b[p]
        bl = {p: b_sc[p[0], p[1], c - 1:c, :] for p in pairs}
        eb = {p: jnp.exp(b[p]) for p in pairs}
        ebl = {p: jnp.exp(bl[p]) for p in pairs}
        dec_end = {p: jnp.exp(bl[p] - b[p]) for p in pairs}
        da = {p: jnp.where(lower, _dot(d_o[p], v[p], _NT), 0.0) for p in pairs}
        dq = {p: _dot(d_o[p], s0t[p], _NN) * eb[p] for p in pairs}
        dstate_in = {p: _dot(d_o[p], q[p] * eb[p], _TN) for p in pairs}
        dk_intra = {p: jnp.zeros((c, LANES), F32) for p in pairs}
        scores, dq_blocks = {p: [] for p in pairs}, {p: [] for p in pairs}
        for i in range(c // HGRN_SUB):
            lo = i * HGRN_SUB
            for p in pairs:
                ref = b_sc[p[0], p[1], lo - 1:lo, :] if i > 0 else jnp.zeros((1, LANES), F32)
                grow = jnp.exp(b[p][lo:lo + HGRN_SUB, :] - ref)
                qt = q[p][lo:lo + HGRN_SUB, :] * grow
                dec = jnp.exp(jnp.minimum(ref - b[p], EXP_CLAMP))
                kd = k[p] * dec
                scores[p].append(_dot(qt, kd, _NT))
                da_i = da[p][lo:lo + HGRN_SUB, :]
                dq_blocks[p].append(_dot_f32(da_i, kd, _NN) * grow)
                dk_intra[p] = dk_intra[p] + _dot_f32(da_i, qt, _TN) * dec
        dv_intra = {p: _dot(jnp.where(lower, jnp.concatenate(scores[p], axis=0), 0.0), d_o[p], _TN) for p in pairs}
        dq = {p: dq[p] + jnp.concatenate(dq_blocks[p], axis=0) for p in pairs}
        q_dq = {p: q[p] * dq[p] for p in pairs}
        for p in pairs:
            dzq_ref[rws[p[0]], sls[p[1]]] = (dq[p] * sq[p] * (1.0 + zq_v[p] * (1.0 - sq[p]))).astype(BF16)
        dstate = [dstate_sc[hh] for hh in hs]
        for cc in reversed(range(cps)):
            ps = [(cc, hh) for hh in hs]
            dk_state = [_dot(v[p], dstate[p[1]], _NN) * dec_end[p] for p in ps]
            dv = [dv_intra[p] + _dot(k[p] * dec_end[p], dstate[p[1]], _NT) for p in ps]
            dk = [dk_intra[p] + dk_state[p[1]] for p in ps]
            db_last = [jnp.sum(k[p] * dk_state[p[1]], axis=0, keepdims=True)
                       + ebl[p] * jnp.sum(s0t[p] * dstate[p[1]], axis=0, keepdims=True) for p in ps]
            db = [q_dq[p] - k[p] * dk[p[1]] + jnp.where(last_row, db_last[p[1]], 0.0) for p in ps]
            df = [_dot_f32(upper, db[p[1]]) / f[p] - dk[p[1]] for p in ps]
            for p in ps:
                hh = p[1]
                dzf_ref[rws[cc], sls[hh]] = (df[hh] * (1.0 - lb[hh]) * sf[p] * (1.0 - sf[p])).astype(BF16)
                dlb_ref[:, sls[hh]] += jnp.sum(df[hh] * (1.0 - sf[p]), axis=0, keepdims=True)
                dzi_ref[rws[cc], sls[hh]] = dv[hh].astype(BF16)
            dstate = [dstate[p[1]] * ebl[p] + dstate_in[p] for p in ps]
        for hh in hs:
            dstate_sc[hh] = dstate[hh]

    tile = pl.BlockSpec((cps * c, hp * LANES), lambda h, i: (n_steps - 1 - i, h))
    out = jax.ShapeDtypeStruct((s, d), BF16)
    return pl.pallas_call(
        body, name=name, grid=(h_n // hp, n_steps),
        in_specs=[tile, tile, tile, pl.BlockSpec((2, hp * LANES), lambda h, i: (0, h)),
                  pl.BlockSpec((hp, cps, LANES, LANES), lambda h, i: (h, n_steps - 1 - i, 0, 0)), tile,
                  pl.BlockSpec(memory_space=pl.ANY)],
        out_specs=[tile, tile, tile, pl.BlockSpec((1, hp * LANES), lambda h, i: (0, h))],
        out_shape=[out, out, out, jax.ShapeDtypeStruct((1, d), F32)],
        scratch_shapes=[pltpu.VMEM((hp, LANES, LANES), F32), pltpu.VMEM((cps, hp, c, LANES), F32)],
        compiler_params=_params("parallel", "arbitrary"),
    )(zq, zf, zi, lb_logits, states, do, after)


ATTN_SUB_ROWS = 256
LOG2E = 1.4426950408889634
LN2 = 0.6931471805599453
Q_PRESCALE = ATTN_SCALE * LOG2E


def _attn_tile(s):
    return min(1024, max(128, s // 2))


def _causal_pairs(n, q_major):
    pairs = [(i, j) for i in range(n) for j in range(i + 1)] if q_major else [(i, j) for j in range(n) for i in range(j, n)]
    return jnp.asarray([p[0] for p in pairs], jnp.int32), jnp.asarray([p[1] for p in pairs], jnp.int32)


def _sub_scores(qn_ref, qr_ref, k, r, sub, t, diagonal):
    q = jnp.concatenate([qn_ref[r:r + sub, :], qr_ref[r:r + sub, :]], axis=1)
    if not diagonal:
        return q, _dot(q, k, _NT)
    cols = r + sub
    keep = lax.broadcasted_iota(jnp.int32, (sub, cols), 1) <= r + lax.broadcasted_iota(jnp.int32, (sub, cols), 0)
    return q, jnp.where(keep, _dot(q, k[:cols], _NT), -jnp.inf)


def _attn_fwd(qn, qr, kn, kr, v, *, name):
    s, t = qn.shape[0], _attn_tile(qn.shape[0])
    sub = min(t, ATTN_SUB_ROWS)
    q_blk, k_blk = _causal_pairs(s // t, True)

    def body(qi_ref, kj_ref, qn_ref, qr_ref, kn_ref, kr_ref, v_ref, o_ref, lse_ref, m_sc, l_sc, acc_sc):
        p_id = pl.program_id(1)
        i, j = qi_ref[p_id], kj_ref[p_id]

        @pl.when(j == 0)
        def _():
            m_sc[...] = jnp.full_like(m_sc, -jnp.inf)
            l_sc[...] = jnp.zeros_like(l_sc)
            acc_sc[...] = jnp.zeros_like(acc_sc)

        def update(diagonal):
            k = jnp.concatenate([kn_ref[...], kr_ref[...]], axis=1)
            v = v_ref[...]
            starts = list(range(0, t, sub))
            scs = [_sub_scores(qn_ref, qr_ref, k, r, sub, t, diagonal)[1] for r in starts]
            ps, alphas = [], []
            for r, sc in zip(starts, scs):
                m_prev = m_sc[r:r + sub, :]
                m_new = jnp.maximum(m_prev, jnp.max(sc, axis=1, keepdims=True))
                alpha = jnp.exp2(m_prev - m_new)
                p = jnp.exp2(sc - m_new[:, :1])
                l_sc[r:r + sub, :] = alpha * l_sc[r:r + sub, :] + jnp.sum(p, axis=1, keepdims=True)
                m_sc[r:r + sub, :] = m_new
                ps.append(p)
                alphas.append(alpha)
            for r, p, alpha in zip(starts, ps, alphas):
                acc_sc[r:r + sub, :] = alpha * acc_sc[r:r + sub, :] + _dot(p, v[:p.shape[1]], _NN)

        @pl.when(j < i)
        def _():
            update(False)

        @pl.when(j == i)
        def _():
            update(True)
            o_ref[...] = (acc_sc[...] / l_sc[...]).astype(BF16)
            lse_ref[...] = m_sc[...] + jnp.log(l_sc[...]) * LOG2E

    q_spec = pl.BlockSpec((t, LANES), lambda h, p, qi, kj: (qi[p], h))
    k_spec = pl.BlockSpec((t, LANES), lambda h, p, qi, kj: (kj[p], h))
    kr_spec = pl.BlockSpec((t, LANES), lambda h, p, qi, kj: (kj[p], 0))
    stat = pltpu.VMEM((t, LANES), F32)
    return pl.pallas_call(
        body, name=name,
        grid_spec=pltpu.PrefetchScalarGridSpec(
            num_scalar_prefetch=2, grid=(MLA_HEADS, q_blk.shape[0]),
            in_specs=[q_spec, q_spec, k_spec, kr_spec, k_spec], out_specs=[q_spec, q_spec],
            scratch_shapes=[stat, stat, stat]),
        out_shape=[jax.ShapeDtypeStruct(qn.shape, BF16), jax.ShapeDtypeStruct(qn.shape, F32)],
        compiler_params=_params("parallel", "arbitrary"),
    )(q_blk, k_blk, qn, qr, kn, kr, v)


def _attn_bwd(qn, qr, kn, kr, v, do, lse, delta, *, name):
    s, t = qn.shape[0], _attn_tile(qn.shape[0])
    n, sub = s // t, min(t, ATTN_SUB_ROWS)
    q_blk, k_blk = _causal_pairs(n, False)

    def body(qi_ref, kj_ref, qn_ref, qr_ref, kn_ref, kr_ref, v_ref, do_ref, lse_ref, delta_ref,
             dqn_ref, dqr_ref, dkn_ref, dv_ref, dkr_ref, dk_sc, dv_sc):
        p_id = pl.program_id(1)
        i, j = qi_ref[p_id], kj_ref[p_id]

        @pl.when(p_id == 0)
        def _():
            dqn_ref[...] = jnp.zeros_like(dqn_ref)
            dqr_ref[...] = jnp.zeros_like(dqr_ref)

        @pl.when(i == j)
        def _():
            dk_sc[...] = jnp.zeros_like(dk_sc)
            dv_sc[...] = jnp.zeros_like(dv_sc)

        def accumulate(diagonal):
            k = jnp.concatenate([kn_ref[...], kr_ref[...]], axis=1)
            v = v_ref[...]
            starts = list(range(0, t, sub))
            qs, d_os, scs, dps = [], [], [], []
            for r in starts:
                q, sc = _sub_scores(qn_ref, qr_ref, k, r, sub, t, diagonal)
                d_o = do_ref[r:r + sub, :]
                qs.append(q)
                d_os.append(d_o)
                scs.append(sc)
                dps.append(_dot(d_o, v[:sc.shape[1]], _NT))
            ps, dss = [], []
            for r, sc, dp in zip(starts, scs, dps):
                p = jnp.exp2(sc - lse_ref[r:r + sub, :][:, :1])
                ps.append(p.astype(BF16))
                dss.append((p * (dp - delta_ref[r:r + sub, :][:, :1])).astype(BF16))
            for r, q, d_o, p, ds in zip(starts, qs, d_os, ps, dss):
                cols = p.shape[1]
                dv_sc[:cols, :] += _dot(p, d_o, _TN)
                dk_sc[:cols, :] += _dot(ds, q, _TN)
                dq = _dot(ds, k[:cols], _NN) * ATTN_SCALE
                rows = pl.ds(pl.multiple_of(i * t + r, sub), sub)
                dqn_ref[rows, :] += dq[:, :LANES]
                dqr_ref[rows, :] += dq[:, LANES:]

        @pl.when(j < i)
        def _():
            accumulate(False)

        @pl.when(j == i)
        def _():
            accumulate(True)

        @pl.when(i == n - 1)
        def _():
            dkn_ref[...] = (dk_sc[:, :LANES] * LN2).astype(BF16)
            dkr_ref[...] = dk_sc[:, LANES:] * LN2
            dv_ref[...] = dv_sc[...].astype(BF16)

    q_spec = pl.BlockSpec((t, LANES), lambda h, p, qi, kj: (qi[p], h))
    k_spec = pl.BlockSpec((t, LANES), lambda h, p, qi, kj: (kj[p], h))
    kr_spec = pl.BlockSpec((t, LANES), lambda h, p, qi, kj: (kj[p], 0))
    head_spec = pl.BlockSpec((s, LANES), lambda h, p, qi, kj: (0, h))
    f32_out, bf16_out = jax.ShapeDtypeStruct(qn.shape, F32), jax.ShapeDtypeStruct(qn.shape, BF16)
    return pl.pallas_call(
        body, name=name,
        grid_spec=pltpu.PrefetchScalarGridSpec(
            num_scalar_prefetch=2, grid=(MLA_HEADS, q_blk.shape[0]),
            in_specs=[q_spec, q_spec, k_spec, kr_spec, k_spec, q_spec, q_spec, q_spec],
            out_specs=[head_spec, head_spec, k_spec, k_spec, k_spec],
            scratch_shapes=[pltpu.VMEM((t, 2 * LANES), F32), pltpu.VMEM((t, LANES), F32)]),
        out_shape=[f32_out, f32_out, bf16_out, bf16_out, f32_out],
        compiler_params=_params("parallel", "arbitrary"),
    )(q_blk, k_blk, qn, qr, kn, kr, v, do, lse, delta)


def _exchange(arrs, *, scatter, name):
    n = len(arrs)
    out_shape = [jax.ShapeDtypeStruct(a.shape if scatter else (N_DEV, *a.shape), a.dtype) for a in arrs]

    def body(*refs):
        ins, outs = refs[:n], refs[n:2 * n]
        send_sems, recv_sems, local_sems = refs[2 * n:]
        x, y, c = lax.axis_index("x"), lax.axis_index("y"), lax.axis_index("c")
        me = 4 * x + 2 * y + c
        copies = []
        for k in range(n):
            local = pltpu.make_async_copy(ins[k].at[me] if scatter else ins[k], outs[k].at[me], local_sems.at[k])
            local.start()
            copies.append(local)
            for d in range(1, N_DEV):
                px, py, pc = (x + (d >> 2)) % 2, (y + ((d >> 1) & 1)) % 2, (c + (d & 1)) % 2
                peer = 4 * px + 2 * py + pc
                remote = pltpu.make_async_remote_copy(
                    src_ref=ins[k].at[peer] if scatter else ins[k], dst_ref=outs[k].at[me],
                    send_sem=send_sems.at[k, d - 1], recv_sem=recv_sems.at[k, d - 1],
                    device_id=(px, py, pc), device_id_type=pl.DeviceIdType.MESH)
                remote.start()
                copies.append(remote)
        for cp in copies:
            cp.wait()

    any_spec = pl.BlockSpec(memory_space=pl.ANY)
    return pl.pallas_call(
        body, name=name, in_specs=[any_spec] * n, out_specs=[any_spec] * n, out_shape=out_shape,
        scratch_shapes=[pltpu.SemaphoreType.DMA((n, N_DEV - 1)), pltpu.SemaphoreType.DMA((n, N_DEV - 1)),
                        pltpu.SemaphoreType.DMA((n,))],
    )(*arrs)


def _peers(x, y, c):
    out = []
    for d in range(1, N_DEV):
        px, py, pc = (x + (d >> 2)) % 2, (y + ((d >> 1) & 1)) % 2, (c + (d & 1)) % 2
        out.append(((px, py, pc), 4 * px + 2 * py + pc))
    return out


CHIP_LEVEL_PEERS = (1, 2, 4, 6)


def _exchange_copies(ins, lands, send_sems, recv_sems, scatter, chip_level=False):
    x, y, c = lax.axis_index("x"), lax.axis_index("y"), lax.axis_index("c")
    me = 4 * x + 2 * y + c
    local, remote = [], []
    for k in range(len(ins)):
        local.append(pltpu.make_async_copy(ins[k].at[me] if scatter else ins[k], lands[k].at[me],
                                           recv_sems.at[k * N_DEV + N_DEV - 1]))
        for d, (coords, peer) in enumerate(_peers(x, y, c)):
            if chip_level and d + 1 not in CHIP_LEVEL_PEERS:
                continue
            remote.append(pltpu.make_async_remote_copy(
                src_ref=ins[k].at[peer] if scatter else ins[k], dst_ref=lands[k].at[me],
                send_sem=send_sems.at[k * N_DEV + d], recv_sem=recv_sems.at[k * N_DEV + d],
                device_id=coords, device_id_type=pl.DeviceIdType.MESH))
    return local, remote


def _exchange_start(arrs, *, scatter, name, after=None, chip_level=False):
    n = len(arrs)
    hbm = pl.BlockSpec(memory_space=pltpu.HBM)
    sem = pl.BlockSpec(memory_space=pltpu.SEMAPHORE)
    lands = [lax.empty(a.shape if scatter else (N_DEV, *a.shape), a.dtype) for a in arrs]

    def body(*refs):
        ins, land_refs = refs[:n], refs[n:2 * n]
        first_out = 2 * n + (after is not None)
        send_sems, recv_sems, token = refs[first_out], refs[first_out + 1], refs[-1]
        local, remote = _exchange_copies(ins, land_refs, send_sems, recv_sems, scatter, chip_level)
        for cp in local + remote:
            cp.start()
        token[...] = jnp.zeros_like(token)

    operands = [pltpu.with_memory_space_constraint(a, pltpu.HBM) for a in list(arrs) + lands]
    behind = [] if after is None else [after]
    res = pl.pallas_call(
        body, name=name,
        out_shape=(pltpu.SemaphoreType.DMA((n * N_DEV,)), pltpu.SemaphoreType.DMA((n * N_DEV,)),
                   *[pltpu.HBM(o.shape, o.dtype) for o in operands], jax.ShapeDtypeStruct((8, LANES), F32)),
        in_specs=[hbm] * (2 * n) + [pl.BlockSpec(memory_space=pl.ANY)] * len(behind),
        out_specs=(sem, sem, *[hbm] * (2 * n), pl.BlockSpec(memory_space=pltpu.VMEM)),
        input_output_aliases={i: 2 + i for i in range(2 * n)},
        compiler_params=pltpu.CompilerParams(has_side_effects=pltpu.SideEffectType.DATAFLOW_SIDE_EFFECTING),
    )(*operands, *behind)
    return (res[0], res[1], list(res[2:2 + n]), list(res[2 + n:2 + 2 * n]), scatter, chip_level), res[-1]


def _exchange_wait(state, after, *, name):
    send_sems, recv_sems, ins, lands, scatter, chip_level = state
    n = len(ins)
    hbm = pl.BlockSpec(memory_space=pltpu.HBM)
    sem = pl.BlockSpec(memory_space=pltpu.SEMAPHORE)

    def body(*refs):
        in_refs, land_refs = refs[:n], refs[n:2 * n]
        local, remote = _exchange_copies(in_refs, land_refs, refs[2 * n], refs[2 * n + 1], scatter, chip_level)
        for cp in local:
            cp.wait()
        for cp in remote:
            cp.wait_send()
            cp.wait_recv()

    res = pl.pallas_call(
        body, name=name, out_shape=tuple(pltpu.HBM(o.shape, o.dtype) for o in ins + lands),
        in_specs=[hbm] * (2 * n) + [sem, sem, pl.BlockSpec(memory_space=pl.ANY)], out_specs=tuple([hbm] * (2 * n)),
        input_output_aliases={i: i for i in range(2 * n)},
        compiler_params=pltpu.CompilerParams(has_side_effects=pltpu.SideEffectType.DATAFLOW_SIDE_EFFECTING),
    )(*ins, *lands, send_sems, recv_sems, after)
    return list(res[n:])


def _chip_forward(lands, *, name):
    n = len(lands)

    def body(*refs):
        ins, outs, send_sems, recv_sems = refs[:n], refs[n:2 * n], refs[2 * n], refs[2 * n + 1]
        x, y, c = lax.axis_index("x"), lax.axis_index("y"), lax.axis_index("c")
        copies = []
        for k in range(n):
            for j, (dx, dy) in enumerate(((0, 1), (1, 0), (1, 1))):
                held = 4 * ((x + dx) % 2) + 2 * ((y + dy) % 2) + c
                cp = pltpu.make_async_remote_copy(
                    src_ref=ins[k].at[held], dst_ref=outs[k].at[held], send_sem=send_sems.at[k, j],
                    recv_sem=recv_sems.at[k, j], device_id=(x, y, 1 - c), device_id_type=pl.DeviceIdType.MESH)
                cp.start()
                copies.append(cp)
        for cp in copies:
            cp.wait()

    any_spec = pl.BlockSpec(memory_space=pl.ANY)
    return pl.pallas_call(
        body, name=name, in_specs=[any_spec] * n, out_specs=[any_spec] * n,
        out_shape=[jax.ShapeDtypeStruct(a.shape, a.dtype) for a in lands], input_output_aliases={k: k for k in range(n)},
        scratch_shapes=[pltpu.SemaphoreType.DMA((n, 3)), pltpu.SemaphoreType.DMA((n, 3))],
    )(*lands)


def _adam(w, terms, m, v, *, name):
    n_layers, r, c = w.shape
    tr = min(r, ADAM_ROW_TILE)
    assert r % tr == 0 and len(terms) == n_layers
    steps = r // tr

    def body(w_ref, *rest):
        t_refs, (m_ref, v_ref, g_out, d_out, m_out, v_out) = rest[:n_layers], rest[n_layers:]
        for layer, t_ref in enumerate(t_refs):
            @pl.when(pl.program_id(0) == layer)
            def _(t_ref=t_ref):
                g = t_ref[0].astype(F32)
                for s in range(1, t_ref.shape[0]):
                    g = g + t_ref[s].astype(F32)
                m1 = ADAM_B1 * m_ref[...] + (1.0 - ADAM_B1) * g
                v1 = ADAM_B2 * v_ref[...] + (1.0 - ADAM_B2) * jnp.square(g)
                m_hat = m1 / (1.0 - ADAM_B1 ** ADAM_STEP)
                v_hat = v1 / (1.0 - ADAM_B2 ** ADAM_STEP)
                g_out[...] = g
                d_out[...] = -ADAM_LR * (m_hat / (jnp.sqrt(v_hat) + ADAM_EPS) + ADAM_WD * w_ref[...])
                m_out[...] = m1
                v_out[...] = v1

    def term_spec(layer, t):
        return pl.BlockSpec((t.shape[0], tr, c),
                            lambda l, i: (0, jnp.where(l == layer, i, jnp.where(l < layer, 0, steps - 1)), 0))

    spec = pl.BlockSpec((None, tr, c), lambda l, i: (l, i, 0))
    out = jax.ShapeDtypeStruct(w.shape, F32)
    return pl.pallas_call(
        body, name=name, grid=(n_layers, steps),
        in_specs=[spec] + [term_spec(layer, t) for layer, t in enumerate(terms)] + [spec, spec], out_specs=[spec] * 4,
        out_shape=[out] * 4, compiler_params=_params("arbitrary", "arbitrary"),
    )(w, *terms, m, v)


def _sum_terms(terms, *, name):
    n, _, p = terms.shape

    def body(t_ref, o_ref):
        acc = t_ref[0]
        for s in range(1, n):
            acc = acc + t_ref[s]
        o_ref[...] = acc

    return pl.pallas_call(body, name=name, out_shape=jax.ShapeDtypeStruct((1, p), F32))(terms)


def _lb_logits_grad(dlb, logits, *, name):
    def body(dlb_ref, l_ref, o_ref):
        lb = _lower_bound(l_ref[...])
        d0 = dlb_ref[...] * lb * (1.0 - lb)
        o_ref[...] = jnp.concatenate([d0, -d0], axis=0)

    return pl.pallas_call(body, name=name, out_shape=jax.ShapeDtypeStruct(logits.shape, F32))(dlb, logits)


def _silu_grad(z):
    sg = _sigmoid(z)
    return sg * (1.0 + z * (1.0 - sg))


def _head_norm_gate(o, zg, gn):
    outs = []
    for h in range(HGRN_HEADS):
        sl = slice(h * LANES, (h + 1) * LANES)
        zg_h = zg[:, sl]
        outs.append(_rms(o[:, sl], gn) * (zg_h * _sigmoid(zg_h)))
    return (jnp.concatenate(outs, axis=1),)


def _head_norm_gate_bwd(o, zg, dm, gn):
    do_parts, dzg_parts, dgn = [], [], jnp.zeros((1, LANES), F32)
    for h in range(HGRN_HEADS):
        sl = slice(h * LANES, (h + 1) * LANES)
        o_h, zg_h, dm_h = o[:, sl], zg[:, sl], dm[:, sl]
        gate = zg_h * _sigmoid(zg_h)
        do_h, dgn_h = _rms_bwd(o_h, gn, dm_h * gate)
        dgn = dgn + dgn_h
        do_parts.append(do_h)
        dzg_parts.append(dm_h * _rms(o_h, gn) * _silu_grad(zg_h))
    return jnp.concatenate(do_parts, axis=1), jnp.concatenate(dzg_parts, axis=1), dgn


def _rope_slabs(x, t_c, t_s1, t_s2, transpose):
    fn = _rope_t if transpose else _rope
    return jnp.concatenate(
        [fn(x[:, h * LANES:(h + 1) * LANES], t_c, t_s1, t_s2) for h in range(x.shape[1] // LANES)], axis=1)


def _loss_head(h, tgt, w):
    d = h.shape[1]
    r = lax.rsqrt(jnp.mean(h * h, axis=-1, keepdims=True) + EPS)
    xh = h * r
    err = xh * w - tgt
    loss = 0.5 * jnp.sum(jnp.mean(err * err, axis=-1, keepdims=True), axis=0, keepdims=True)
    dy = err / d
    dxh = dy * w
    dh = r * (dxh - xh * jnp.mean(dxh * xh, axis=-1, keepdims=True))
    return dh, dh, jnp.sum(dy * xh, axis=0, keepdims=True), jnp.broadcast_to(loss, (1, LANES))


def _mlp_fwd(h, norm, w_up, w_down, tag, loss_head=None):
    d = h.shape[1]

    def up(x, g, wu):
        x_n = _rms(x, g).astype(BF16)
        return x_n, jnp.concatenate([jnp.square(jnp.maximum(_dot(x_n, wu[j], _NN), 0.0)) for j in range(wu.shape[0])],
                                    axis=1)

    xn, act = _rowcall(up, [h], [norm, w_up], [(d, BF16), (w_up.shape[0] * w_up.shape[2], BF16)], [], tr=512,
                       name=f"{tag}_up")
    if callable(w_down):
        w_down = w_down(act)
    if loss_head is None:
        return _rowcall(lambda a, res, wd: (res + _dot(a, wd, _NN),), [act, h], [w_down], [(d, F32)], [],
                        name=f"{tag}_down")[0], (h, xn, act)
    tgt, final_norm = loss_head

    def down_and_loss(a, res, t, wd, g):
        return _loss_head(res + _dot(a, wd, _NN), t, g)

    return _rowcall(down_and_loss, [act, h, tgt], [w_down, final_norm], [(d, F32), (d, BF16)], [d, LANES],
                    name=f"{tag}_down_loss"), (h, xn, act)


def _mlp_bwd(dh_out, dh_out_bf, saved, norm, w_up, w_down, tag, after=None):
    h, xn, act = saved
    d = h.shape[1]
    du = _rowcall(lambda dres, a, wd: (_dot(dres, wd, _NT) * (2.0 * jnp.sqrt(a.astype(F32))),), [dh_out_bf, act],
                  [w_down], [(act.shape[1], BF16)], [], after=after, name=f"{tag}_bwd_du")[0]
    dw_down = _mm(act, dh_out_bf, mode="tn", name=f"{tag}_bwd_wdown")
    dw_up = _mm(xn, du, mode="tn", col_shards=w_up.shape[0], name=f"{tag}_bwd_wup")

    def up_norm_bwd(x, d_u, dres, g, wu):
        cols = wu.shape[2]
        dxn = _dot(d_u[:, :cols], wu[0], _NT)
        for j in range(1, wu.shape[0]):
            dxn = dxn + _dot(d_u[:, j * cols:(j + 1) * cols], wu[j], _NT)
        dx, dw = _rms_bwd(x, g, dxn)
        return dx + dres, dx + dres, dw

    dh, dh_bf, dnorm = _rowcall(up_norm_bwd, [h, du, dh_out], [norm, w_up], [(d, F32), (d, BF16)], [d], tr=512,
                                name=f"{tag}_bwd_dxn")
    return dh, dh_bf, dnorm, dw_up, dw_down


def _row_major(g):
    return g.reshape(g.shape[0] * g.shape[1], g.shape[2])


def _col_major(g):
    return jnp.transpose(g, (1, 0, 2)).reshape(g.shape[1], g.shape[0] * g.shape[2])


def _col_terms(dw):
    k, n = dw.shape
    return jnp.transpose(dw.reshape(k, N_DEV, n // N_DEV), (1, 0, 2))


def _row_terms(dw):
    return dw.reshape(N_DEV, dw.shape[0] // N_DEV, dw.shape[1])


def kernel(x, hgrn_norm, hgrn_w_q, hgrn_w_f, hgrn_w_i, hgrn_w_g, hgrn_g_norm, hgrn_w_o, hgrn_lb_logits, mla_norm, mla_w_dq, mla_q_norm, mla_w_uq, mla_w_o, kv_in_norm, kv_w_dkv, kv_norm, kv_w_uk, kv_w_uv, mlp_norm, mlp_w_up, mlp_w_down, final_norm, loss_target, m_hgrn_norm, m_hgrn_w_q, m_hgrn_w_f, m_hgrn_w_i, m_hgrn_w_g, m_hgrn_g_norm, m_hgrn_w_o, m_hgrn_lb_logits, m_mla_norm, m_mla_w_dq, m_mla_q_norm, m_mla_w_uq, m_mla_w_o, m_kv_in_norm, m_kv_w_dkv, m_kv_norm, m_kv_w_uk, m_kv_w_uv, m_mlp_norm, m_mlp_w_up, m_mlp_w_down, m_final_norm, v_hgrn_norm, v_hgrn_w_q, v_hgrn_w_f, v_hgrn_w_i, v_hgrn_w_g, v_hgrn_g_norm, v_hgrn_w_o, v_hgrn_lb_logits, v_mla_norm, v_mla_w_dq, v_mla_q_norm, v_mla_w_uq, v_mla_w_o, v_kv_in_norm, v_kv_w_dkv, v_kv_norm, v_kv_w_uk, v_kv_w_uv, v_mlp_norm, v_mlp_w_up, v_mlp_w_down, v_final_norm):
    given = dict(locals())
    weight_names = ["hgrn_norm", "hgrn_w_q", "hgrn_w_f", "hgrn_w_i", "hgrn_w_g", "hgrn_g_norm", "hgrn_w_o",
                    "hgrn_lb_logits", "mla_norm", "mla_w_dq", "mla_q_norm", "mla_w_uq", "mla_w_o", "kv_in_norm",
                    "kv_w_dkv", "kv_norm", "kv_w_uk", "kv_w_uv", "mlp_norm", "mlp_w_up", "mlp_w_down", "final_norm"]
    me = 4 * lax.axis_index("x") + 2 * lax.axis_index("y") + lax.axis_index("c")
    xs, tgt = x[0], loss_target[0]
    seq, d_model = xs.shape
    n_heads, hd = MLA_HEADS, LANES

    big_local = {
        "hgrn_w_q": hgrn_w_q[0], "hgrn_w_f": hgrn_w_f[0], "hgrn_w_i": hgrn_w_i[0], "hgrn_w_g": hgrn_w_g[0],
        "hgrn_w_o": hgrn_w_o[0], "mla_w_dq": mla_w_dq[0], "mla_w_uq": mla_w_uq[0], "mla_w_o": mla_w_o[0],
        "kv_w_dkv": kv_w_dkv, "kv_w_uk": kv_w_uk, "kv_w_uv": kv_w_uv,
        "mlp_w_up0": mlp_w_up[0], "mlp_w_up1": mlp_w_up[1], "mlp_w_down0": mlp_w_down[0], "mlp_w_down1": mlp_w_down[1],
    }
    big_names = list(big_local)
    col_sharded = {"mla_w_uq", "kv_w_uk", "kv_w_uv"}
    shard_major = {"mlp_w_up0", "mlp_w_up1"}
    vec_local = jnp.concatenate([hgrn_norm, hgrn_lb_logits], axis=0)
    first_names = ["hgrn_w_q", "hgrn_w_f", "hgrn_w_i"]
    proj_names = first_names + ["hgrn_w_g"]
    later_names = {"hgrn_o": ["hgrn_w_g", "hgrn_w_o"], "up0": ["mlp_w_up0"], "down0": ["mlp_w_down0"],
                   "mla": ["kv_w_dkv", "kv_w_uk", "kv_w_uv", "mla_w_dq", "mla_w_uq", "mla_w_o"],
                   "mlp1": ["mlp_w_up1", "mlp_w_down1"]}

    def unshard(names, arrays):
        return {k: (a if k in shard_major else _col_major(a) if k in col_sharded else _row_major(a))
                for k, a in zip(names, arrays)}

    two_level = {"down0", "mla"}
    first_state, token = _exchange_start([big_local[k].astype(BF16) for k in first_names] + [vec_local], scatter=False,
                                         chip_level=True, name="gather_first_start")
    gather_state = {}
    for tag, names in later_names.items():
        gather_state[tag], token = _exchange_start([big_local[k].astype(BF16) for k in names], scatter=False,
                                                   chip_level=tag in two_level, after=token, name=f"gather_{tag}_start")

    def gather_wait(tag, after):
        landed = _exchange_wait(gather_state[tag], after, name=f"gather_{tag}_wait")
        if tag in two_level:
            landed = _chip_forward(landed, name=f"gather_{tag}_forward")
        w.update(unshard(later_names[tag], landed))
        return [w[k] for k in later_names[tag]]

    gathered = _chip_forward(_exchange_wait(first_state, token, name="gather_first_wait"), name="gather_first_forward")
    w = unshard(first_names, gathered[:-1])
    vec_full = jnp.transpose(gathered[-1], (1, 0, 2)).reshape(3, d_model)
    hgrn_norm_full, lb_logits_full = vec_full[0:1], vec_full[1:3]
    t_c, t_s1, t_s2 = _rope_tables(seq)
    kv_lora = kv_w_uk.shape[0]

    def hgrn_proj(a, g, *weights):
        xn = _rms(a, g).astype(BF16)
        return (xn, *[_dot(xn, wt, _NN) for wt in weights])

    xn0, zq, zf, zi = _rowcall(hgrn_proj, [xs], [hgrn_norm_full] + [w[k] for k in first_names],
                               [(d_model, BF16)] + [(d_model, F32)] * 3, [], tr=512, name="hgrn_proj")
    o_rec, states = _hgrn_fwd(zq, zf, zi, lb_logits_full, name="hgrn_fwd")
    gather_wait("hgrn_o", o_rec)

    def gate_out(o, x_n, res, gn, wg, wo):
        z = _dot(x_n, wg, _NN)
        m = _head_norm_gate(o, z, gn)[0].astype(BF16)
        return z, m, res + _dot(m, wo, _NN)

    zg, mixed, h1 = _rowcall(gate_out, [o_rec, xn0, xs], [hgrn_g_norm, w["hgrn_w_g"], w["hgrn_w_o"]],
                             [(d_model, F32), (d_model, BF16), (d_model, F32)], [], name="hgrn_gate_out")
    h2, mlp0_saved = _mlp_fwd(h1, mlp_norm[0:1], gather_wait("up0", h1)[0], lambda act: gather_wait("down0", act)[0],
                              "mlp0")
    gather_wait("mla", h2)
    w_uq3 = w["mla_w_uq"].reshape(-1, n_heads, MLA_NOPE + MLA_ROPE)
    w_uq_nope = w_uq3[:, :, :MLA_NOPE].reshape(-1, n_heads * hd)
    w_uq_rope = jnp.pad(w_uq3[:, :, MLA_NOPE:], ((0, 0), (0, 0), (0, hd - MLA_ROPE))).reshape(-1, n_heads * hd)
    w_dkv_pad = jnp.pad(w["kv_w_dkv"], ((0, 0), (0, kv_lora + hd - w["kv_w_dkv"].shape[1])))

    q_lora, qk_cols = w["mla_w_dq"].shape[1], n_heads * hd

    def mla_qkv(a, tc, ts1, ts2, g_kv_in, g_mla, g_q, g_kv, wdq, wn, wr, wdkv, wuk, wuv):
        h_n, x_n = _rms(a, g_kv_in).astype(BF16), _rms(a, g_mla).astype(BF16)
        cq = _dot(x_n, wdq, _NN)
        cq_n = _rms(cq, g_q).astype(BF16)
        q_nope = _dot(cq_n, wn, _NN) * Q_PRESCALE
        q_rope = _rope_slabs(_dot(cq_n, wr, _NN) * Q_PRESCALE, tc, ts1, ts2, False)
        c_all = _dot(h_n, wdkv, _NN)
        lat = _rms(c_all[:, :kv_lora], g_kv).astype(BF16)
        return (h_n, x_n, cq, cq_n, q_nope, q_rope, c_all, lat, _rope(c_all[:, kv_lora:], tc, ts1, ts2),
                _dot(lat, wuk, _NN), _dot(lat, wuv, _NN))

    hn, xn2, cq_pre, c_q, qn, qr, ckr, c_kv, kr, kn, vv = _rowcall(
        mla_qkv, [h2, t_c, t_s1, t_s2],
        [kv_in_norm[None, :], mla_norm, mla_q_norm, kv_norm[None, :], w["mla_w_dq"], w_uq_nope, w_uq_rope, w_dkv_pad,
         w["kv_w_uk"], w["kv_w_uv"]],
        [(d_model, BF16), (d_model, BF16), (q_lora, F32), (q_lora, BF16), (qk_cols, BF16), (qk_cols, BF16),
         (kv_lora + hd, F32), (kv_lora, BF16), (hd, BF16), (qk_cols, BF16), (qk_cols, BF16)], [], tr=512, name="mla_qkv")
    o_att, lse = _attn_fwd(qn, qr, kn, kr, vv, name="attn_fwd")
    h3 = _mm(o_att, w["mla_w_o"], mode="nn", add=h2, name="attn_out")
    gather_wait("mlp1", h3)
    (dh4, dh4_bf, g_final_norm, loss_part), mlp1_saved = _mlp_fwd(
        h3, mlp_norm[1:2], w["mlp_w_up1"], w["mlp_w_down1"], "mlp1", loss_head=(tgt, final_norm[None, :]))

    g = {}
    groups = {"mlp1": ["mlp_w_up1", "mlp_w_down1"],
              "mla": ["mla_w_o", "mla_w_uq", "mla_w_dq", "kv_w_uk", "kv_w_uv", "kv_w_dkv"],
              "mlp0": ["mlp_w_up0", "mlp_w_down0"],
              "hgrn_out": ["hgrn_w_o", "hgrn_w_g"],
              "hgrn_in": ["hgrn_w_q", "hgrn_w_f", "hgrn_w_i"]}
    scatter_state = {}

    def scatter_start(tag, after=None):
        scatter_state[tag], tok = _exchange_start(
            [g[k] if k in shard_major else (_col_terms if k in col_sharded else _row_terms)(g[k]) for k in groups[tag]],
            scatter=True, after=after,
            name=f"scatter_{tag}_start")
        return tok

    dh3, dh3_bf, g_mlp_norm1, g["mlp_w_up1"], g["mlp_w_down1"] = _mlp_bwd(
        dh4, dh4_bf, mlp1_saved, mlp_norm[1:2], w["mlp_w_up1"], w["mlp_w_down1"], "mlp1")
    def attn_out_bwd(dres, o, wo):
        d_o = _dot(dres, wo, _NT).astype(BF16)
        prod = d_o.astype(F32) * o.astype(F32)
        return d_o, jnp.concatenate([jnp.broadcast_to(jnp.sum(prod[:, h * hd:(h + 1) * hd], axis=1, keepdims=True),
                                                      (prod.shape[0], hd)) for h in range(n_heads)], axis=1)

    d_oatt, delta = _rowcall(attn_out_bwd, [dh3_bf, o_att], [w["mla_w_o"]], [(qk_cols, BF16), (qk_cols, F32)], [],
                             after=scatter_start("mlp1"), name="attn_out_bwd_x")
    g["mla_w_o"] = _mm(o_att, dh3_bf, mode="tn", name="attn_out_bwd_w")
    dqn, dqr, dkn, dvv, dkr = _attn_bwd(qn, qr, kn, kr, vv, d_oatt, lse, delta, name="attn_bwd")

    def q_path_bwd(cq, cq_n, x_n, d_qn, d_qr, tc, ts1, ts2, g_q, wdq, wn, wr):
        d_qn, d_qr = d_qn.astype(BF16), _rope_slabs(d_qr, tc, ts1, ts2, True).astype(BF16)
        d_cq, d_gq = _rms_bwd(cq, g_q, _dot(d_qn, wn, _NT) + _dot(d_qr, wr, _NT))
        d_cq = d_cq.astype(BF16)
        return _dot(d_cq, wdq, _NT), d_gq, _dot(x_n, d_cq, _TN), _dot(cq_n, d_qn, _TN), _dot(cq_n, d_qr, _TN)

    dxn2, g_q_norm, g_dq, g_uq_nope, g_uq_rope = _rowcall(
        q_path_bwd, [cq_pre, c_q, xn2, dqn, dqr, t_c, t_s1, t_s2], [mla_q_norm, w["mla_w_dq"], w_uq_nope, w_uq_rope],
        [(d_model, F32)], [q_lora, (d_model, q_lora), (q_lora, qk_cols), (q_lora, qk_cols)], tr=512, name="mla_q_bwd")
    g["mla_w_dq"] = g_dq.astype(GRAD_WIRE_DTYPE)
    g["mla_w_uq"] = jnp.concatenate([g_uq_nope.reshape(q_lora, n_heads, hd),
                                     g_uq_rope.reshape(q_lora, n_heads, hd)[:, :, :MLA_ROPE]],
                                    axis=2).reshape(q_lora, -1).astype(GRAD_WIRE_DTYPE)

    def kv_path_bwd(c_all, lat, h_n, d_kn, d_v, d_kr_heads, tc, ts1, ts2, a, d_xn2, dres,
                    g_kv, g_kv_in, g_mla, wdkv, wuk, wuv):
        d_lat, d_gkv = _rms_bwd(c_all[:, :kv_lora], g_kv, _dot(d_kn, wuk, _NT) + _dot(d_v, wuv, _NT))
        d_kr = d_kr_heads[:, :hd]
        for h in range(1, n_heads):
            d_kr = d_kr + d_kr_heads[:, h * hd:(h + 1) * hd]
        d_all = jnp.concatenate([d_lat, _rope_t(d_kr, tc, ts1, ts2)], axis=1).astype(BF16)
        dx1, d_gkv_in = _rms_bwd(a, g_kv_in, _dot(d_all, wdkv, _NT))
        dx2, d_gmla = _rms_bwd(a, g_mla, d_xn2)
        d_a = dx1 + dx2 + dres
        return (d_a, d_a, d_gkv, d_gkv_in, d_gmla, _dot(h_n, d_all, _TN), _dot(lat, d_kn, _TN), _dot(lat, d_v, _TN))

    dh2, dh2_bf, g_kv_norm, g_kv_in_norm, g_mla_norm, g_dkv, g_uk, g_uv = _rowcall(
        kv_path_bwd, [ckr, c_kv, hn, dkn, dvv, dkr, t_c, t_s1, t_s2, h2, dxn2, dh3],
        [kv_norm[None, :], kv_in_norm[None, :], mla_norm, w_dkv_pad, w["kv_w_uk"], w["kv_w_uv"]],
        [(d_model, F32), (d_model, BF16)],
        [kv_lora, d_model, d_model, (d_model, kv_lora + hd), (kv_lora, qk_cols), (kv_lora, qk_cols)], name="mla_kv_bwd")
    g["kv_w_dkv"] = g_dkv[:, :kv_w_dkv.shape[1]].astype(GRAD_WIRE_DTYPE)
    g["kv_w_uk"], g["kv_w_uv"] = g_uk.astype(GRAD_WIRE_DTYPE), g_uv.astype(GRAD_WIRE_DTYPE)
    dh1, dh1_bf, g_mlp_norm0, g["mlp_w_up0"], g["mlp_w_down0"] = _mlp_bwd(
        dh2, dh2_bf, mlp0_saved, mlp_norm[0:1], w["mlp_w_up0"], w["mlp_w_down0"], "mlp0", after=scatter_start("mla"))

    g["hgrn_w_o"] = _mm(mixed, dh1_bf, mode="tn", after=scatter_start("mlp0"), name="hgrn_out_bwd_w")
    do_rec, dzg, g_g_norm = _rowcall(
        lambda dres, o, z, wo, gn: _head_norm_gate_bwd(o, z, _dot(dres, wo, _NT), gn), [dh1_bf, o_rec, zg],
        [w["hgrn_w_o"], hgrn_g_norm], [(d_model, F32), (d_model, BF16)], [hd], name="hgrn_gate_out_bwd")
    g["hgrn_w_g"] = _mm(xn0, dzg, mode="tn", name="hgrn_w_g_bwd_w")
    dzq, dzf, dzi, g_lb = _hgrn_bwd(zq, zf, zi, lb_logits_full, states, do_rec, scatter_start("hgrn_out"),
                                    name="hgrn_bwd")
    for nm, dz in (("hgrn_w_q", dzq), ("hgrn_w_f", dzf), ("hgrn_w_i", dzi)):
        g[nm] = _mm(xn0, dz, mode="tn", name=f"{nm}_bwd_w")

    def hgrn_proj_bwd(a, dres, *rest):
        dzs, gw, weights = rest[:4], rest[4], rest[5:]
        dxn = _dot(dzs[0], weights[0], _NT)
        for dz, wt in zip(dzs[1:], weights[1:]):
            dxn = dxn + _dot(dz, wt, _NT)
        dx, dw = _rms_bwd(a, gw, dxn)
        return dx + dres, dw

    grad_x, g_hgrn_norm = _rowcall(hgrn_proj_bwd, [xs, dh1, dzq, dzf, dzi, dzg],
                                   [hgrn_norm_full] + [w[k] for k in proj_names], [(d_model, F32)], [d_model],
                                   tr=512, name="hgrn_proj_bwd")

    small_parts = [g_hgrn_norm, g_lb, g_g_norm, g_mla_norm, g_q_norm, g_kv_in_norm, g_kv_norm, g_mlp_norm0,
                   g_mlp_norm1, g_final_norm, loss_part]
    small_sizes = [p.shape[1] for p in small_parts]
    small_terms = _exchange([jnp.concatenate(small_parts, axis=1)], scatter=False, name="gather_small")[0]
    small_sum = _sum_terms(small_terms, name="sum_small")
    last = scatter_start("hgrn_in", after=small_sum)
    offs = [0]
    for sz in small_sizes:
        offs.append(offs[-1] + sz)
    (s_hgrn_norm, s_lb, s_g_norm, s_mla_norm, s_q_norm, s_kv_in_norm, s_kv_norm, s_mlp_norm0, s_mlp_norm1, s_final_norm,
     s_loss) = [small_sum[:, a:b] for a, b in zip(offs[:-1], offs[1:])]
    shard = hgrn_norm.shape[1]
    g_lb_logits = _lb_logits_grad(lax.dynamic_slice_in_dim(s_lb, me * shard, shard, axis=1), hgrn_lb_logits,
                                  name="lb_logits_grad")
    loss = s_loss[0, 0]

    res, layer_terms = {}, {}

    def update(k, term_list):
        shape = given[k].shape
        as_layers = (len(term_list), shape[-2], shape[-1])
        upd = _adam(given[k].reshape(as_layers), term_list, given["m_" + k].reshape(as_layers),
                    given["v_" + k].reshape(as_layers), name=f"adam_{k}")
        res[k] = [o.reshape(shape) for o in upd]
        return upd[0]

    for tag, names in groups.items():
        for k, t in zip(names, _exchange_wait(scatter_state[tag], last, name=f"scatter_{tag}_wait")):
            if k.startswith("mlp_w_"):
                layer_terms.setdefault(k[:-1], {})[int(k[-1])] = t
                if len(layer_terms[k[:-1]]) == 2:
                    last = update(k[:-1], [layer_terms[k[:-1]][0], layer_terms[k[:-1]][1]])
            else:
                last = update(k, [t])

    small_grads = {
        "hgrn_norm": lax.dynamic_slice_in_dim(s_hgrn_norm, me * shard, shard, axis=1),
        "hgrn_g_norm": s_g_norm, "hgrn_lb_logits": g_lb_logits, "mla_norm": s_mla_norm, "mla_q_norm": s_q_norm,
        "kv_in_norm": s_kv_in_norm, "kv_norm": s_kv_norm,
        "mlp_norm": jnp.concatenate([s_mlp_norm0, s_mlp_norm1], axis=0), "final_norm": s_final_norm,
    }
    small_names = list(small_grads)

    def flat(a):
        return a.reshape(1, -1)

    packed = [jnp.concatenate([flat(src[pre + k]) for k in small_names], axis=1)
              for src, pre in ((given, ""), (small_grads, ""), (given, "m_"), (given, "v_"))]
    small_out = _adam(packed[0][None], [packed[1][None]], packed[2][None], packed[3][None], name="adam_small")
    off = 0
    for k in small_names:
        size = given[k].size
        res[k] = [o[0, :, off:off + size].reshape(given[k].shape) for o in small_out]
        off += size

    outs = [loss, grad_x[None]]
    for i in range(4):
        outs += [res[k][i] for k in weight_names]
    return tuple(outs)
```

```python
import functools

import jax
import jax.numpy as jnp
from jax import lax
from jax.experimental import pallas as pl
from jax.experimental.pallas import tpu as pltpu

F32 = jnp.float32
BF16 = jnp.bfloat16

EPS = 1e-6
LANES = 128
N_DEV = 8
V7X_VMEM_LIMIT_BYTES = 56 << 20
MM_PIPELINE_BYTES = 30 << 20
MM_ROW_TILE = 512
ADAM_ROW_TILE = 256
GRAD_WIRE_DTYPE = BF16

HGRN_HEADS = 8
HGRN_CHUNK = 64
HGRN_SUB = 16
HGRN_HEADS_PER_STEP = 8
HGRN_CHUNKS_PER_STEP = 4
EXP_CLAMP = 80.0
MLA_HEADS = 16
MLA_NOPE = 128
MLA_ROPE = 64
ROPE_THETA = 10000.0
ATTN_SCALE = (MLA_NOPE + MLA_ROPE) ** -0.5

ADAM_LR = 0.001
ADAM_B1 = 0.9
ADAM_B2 = 0.999
ADAM_EPS = 1e-08
ADAM_WD = 0.01
ADAM_STEP = 10

_NN = ((1,), (0,))
_NT = ((1,), (1,))
_TN = ((0,), (0,))


def _params(*sem):
    return pltpu.CompilerParams(dimension_semantics=sem, vmem_limit_bytes=V7X_VMEM_LIMIT_BYTES)


def _dot(a, b, dims):
    return lax.dot_general(a.astype(BF16), b.astype(BF16), (dims, ((), ())), preferred_element_type=F32)


def _dot_f32(a, b, dims=_NN):
    return lax.dot_general(a, b, (dims, ((), ())), precision=lax.Precision.HIGH, preferred_element_type=F32)


def _sigmoid(x):
    return 1.0 / (1.0 + jnp.exp(-x))


def _rms(x, w):
    r = lax.rsqrt(jnp.mean(x * x, axis=-1, keepdims=True) + EPS)
    return x * r * w


def _rms_bwd(x, w, dy):
    r = lax.rsqrt(jnp.mean(x * x, axis=-1, keepdims=True) + EPS)
    xh = x * r
    dw = jnp.sum(dy * xh, axis=0, keepdims=True)
    dxh = dy * w
    dx = r * (dxh - xh * jnp.mean(dxh * xh, axis=-1, keepdims=True))
    return dx, dw


def _mm_tiles(m, n, k, a_bytes, b_bytes, out_tile_bytes):
    tm = min(m, MM_ROW_TILE)
    for tn in (n, 2048, 1024, 512, 256, LANES):
        if tn <= n and n % tn == 0:
            if 2 * (tm * k * a_bytes + k * tn * b_bytes + tm * tn * out_tile_bytes) <= MM_PIPELINE_BYTES:
                return tm, tn
    return tm, min(n, LANES)


def _mm(a, b, *, mode, name, out_dtype=None, after=None, col_shards=None):
    if mode == "nn":
        (m, k), (k2, n) = a.shape, b.shape
    elif mode == "nt":
        (m, k), (n, k2) = a.shape, b.shape
    else:
        (k, m), (k2, n) = a.shape, b.shape
    assert k == k2, (name, a.shape, b.shape)
    if out_dtype is None:
        out_dtype = GRAD_WIRE_DTYPE if mode == "tn" else F32
    tm, tn = _mm_tiles(m, n, k, a.dtype.itemsize, b.dtype.itemsize, jnp.dtype(out_dtype).itemsize)
    if col_shards is not None:
        tn = n // col_shards
    assert m % tm == 0 and n % tn == 0, (name, m, n)
    dims = {"nn": _NN, "nt": _NT, "tn": _TN}[mode]
    a_spec = pl.BlockSpec((k, tm), lambda i, j: (0, i)) if mode == "tn" else pl.BlockSpec((tm, k), lambda i, j: (i, 0))
    b_spec = pl.BlockSpec((tn, k), lambda i, j: (j, 0)) if mode == "nt" else pl.BlockSpec((k, tn), lambda i, j: (0, j))
    o_spec = pl.BlockSpec((tm, tn), lambda i, j: (i, j))
    operands, in_specs = [a, b], [a_spec, b_spec]
    if after is not None:
        operands.append(after)
        in_specs.append(pl.BlockSpec(memory_space=pl.ANY))
    out_shape = jax.ShapeDtypeStruct((m, n), out_dtype)
    if col_shards is not None:
        out_shape = jax.ShapeDtypeStruct((col_shards, m, tn), out_dtype)
        o_spec = pl.BlockSpec((None, tm, tn), lambda i, j: (j, i, 0))

    def body(*refs):
        refs[-1][...] = _dot(refs[0][...], refs[1][...], dims).astype(out_dtype)

    return pl.pallas_call(
        body, name=name, grid=(m // tm, n // tn), in_specs=in_specs, out_specs=o_spec, out_shape=out_shape,
        compiler_params=_params("parallel", "parallel"),
    )(*operands)


def _rowcall(fn, rows, consts, outs, accs, *, name, tr=256, after=None):
    s = rows[0].shape[0]
    tr = min(tr, s)
    assert s % tr == 0
    n_out = len(outs)
    accs = [(1, a) if isinstance(a, int) else a for a in accs]
    in_specs = [pl.BlockSpec((tr, r.shape[1]), lambda i: (i, 0)) for r in rows]
    in_specs += [pl.BlockSpec(c.shape, lambda i, nd=c.ndim: (0,) * nd) for c in consts]
    out_shape = [jax.ShapeDtypeStruct((s, w), dt) for w, dt in outs] + [jax.ShapeDtypeStruct(a, F32) for a in accs]
    out_specs = [pl.BlockSpec((tr, w), lambda i: (i, 0)) for w, _ in outs] + [pl.BlockSpec(a, lambda i: (0, 0)) for a in accs]
    n_in = len(rows) + len(consts)

    def body(*refs):
        res = fn(*[r[...] for r in refs[:n_in]])
        out_refs = refs[n_in + (after is not None):]
        for ref, val in zip(out_refs[:n_out], res[:n_out]):
            ref[...] = val.astype(ref.dtype)
        i = pl.program_id(0)
        for ref, val in zip(out_refs[n_out:], res[n_out:]):
            @pl.when(i == 0)
            def _(ref=ref, val=val):
                ref[...] = val

            @pl.when(i > 0)
            def _(ref=ref, val=val):
                ref[...] += val

    behind = [] if after is None else [after]
    return pl.pallas_call(
        body, name=name, grid=(s // tr,), in_specs=in_specs + [pl.BlockSpec(memory_space=pl.ANY)] * len(behind),
        out_specs=out_specs, out_shape=out_shape, compiler_params=_params("arbitrary" if accs else "parallel"),
    )(*rows, *consts, *behind)


def _rope_tables(seq):
    half = MLA_ROPE // 2
    inv_freq = ROPE_THETA ** (-jnp.arange(half, dtype=F32) / half)
    ang = jnp.arange(seq, dtype=F32)[:, None] * inv_freq[None, :]
    cos, sin, zero = jnp.cos(ang), jnp.sin(ang), jnp.zeros((seq, half), F32)
    t_c = jnp.concatenate([cos, cos, zero, zero], axis=1)
    t_s1 = jnp.concatenate([-sin, zero, zero, zero], axis=1)
    t_s2 = jnp.concatenate([zero, sin, zero, zero], axis=1)
    return t_c, t_s1, t_s2


def _rope(slab, t_c, t_s1, t_s2):
    return slab * t_c + pltpu.roll(slab, 96, 1) * t_s1 + pltpu.roll(slab, 32, 1) * t_s2


def _rope_t(d, t_c, t_s1, t_s2):
    return d * t_c + pltpu.roll(d * t_s1, 32, 1) + pltpu.roll(d * t_s2, 96, 1)


def _lower_bound(logits):
    l0, l1 = logits[0:1, :], logits[1:2, :]
    mx = jnp.maximum(l0, l1)
    e0, e1 = jnp.exp(l0 - mx), jnp.exp(l1 - mx)
    return e0 / (e0 + e1)


def _tri(n, lower):
    row = lax.broadcasted_iota(jnp.int32, (n, n), 0)
    col = lax.broadcasted_iota(jnp.int32, (n, n), 1)
    return (row >= col) if lower else (row <= col)


def _hgrn_fwd(zq, zf, zi, lb_logits, *, name):
    s, d = zq.shape
    h_n, c, hp, cps = d // LANES, HGRN_CHUNK, HGRN_HEADS_PER_STEP, HGRN_CHUNKS_PER_STEP
    nc = s // c

    def body(zq_ref, zf_ref, zi_ref, lb_ref, o_ref, st_ref, state_sc, b_sc):
        @pl.when(pl.program_id(1) == 0)
        def _():
            state_sc[...] = jnp.zeros_like(state_sc)

        lower = _tri(c, True)
        lower_f = lower.astype(F32)
        hs, pairs = range(hp), [(cc, hh) for cc in range(cps) for hh in range(hp)]
        sls = [slice(hh * LANES, (hh + 1) * LANES) for hh in hs]
        rws = [slice(cc * c, (cc + 1) * c) for cc in range(cps)]
        lb = [_lower_bound(lb_ref[:, sl]) for sl in sls]
        zq_v = {p: zq_ref[rws[p[0]], sls[p[1]]] for p in pairs}
        q = {p: zq_v[p] * _sigmoid(zq_v[p]) for p in pairs}
        f = {p: lb[p[1]] + (1.0 - lb[p[1]]) * _sigmoid(zf_ref[rws[p[0]], sls[p[1]]]) for p in pairs}
        k = {p: 1.0 - f[p] for p in pairs}
        v = {p: zi_ref[rws[p[0]], sls[p[1]]] for p in pairs}
        b = {p: _dot_f32(lower_f, jnp.log(f[p])) for p in pairs}
        for p in pairs:
            b_sc[p[0], p[1]] = b[p]
        qe = {p: q[p] * jnp.exp(b[p]) for p in pairs}
        scores = {p: [] for p in pairs}
        for i in range(c // HGRN_SUB):
            lo = i * HGRN_SUB
            for p in pairs:
                ref = b_sc[p[0], p[1], lo - 1:lo, :] if i > 0 else jnp.zeros((1, LANES), F32)
                qt = q[p][lo:lo + HGRN_SUB, :] * jnp.exp(b[p][lo:lo + HGRN_SUB, :] - ref)
                dec = jnp.exp(jnp.minimum(ref - b[p], EXP_CLAMP))
                scores[p].append(_dot(qt, k[p] * dec, _NT))
        o_intra = {p: _dot(jnp.where(lower, jnp.concatenate(scores[p], axis=0), 0.0), v[p], _NN) for p in pairs}
        bl = {p: b_sc[p[0], p[1], c - 1:c, :] for p in pairs}
        k_end = {p: k[p] * jnp.exp(bl[p] - b[p]) for p in pairs}
        state = [state_sc[hh] for hh in hs]
        for cc in range(cps):
            for hh in hs:
                st_ref[hh, cc] = state[hh]
                o_ref[rws[cc], sls[hh]] = _dot(qe[cc, hh], state[hh], _NT) + o_intra[cc, hh]
            state = [state[hh] * jnp.exp(bl[cc, hh]) + _dot(v[cc, hh], k_end[cc, hh], _TN) for hh in hs]
        for hh in hs:
            state_sc[hh] = state[hh]

    tile = pl.BlockSpec((cps * c, hp * LANES), lambda h, i: (i, h))
    return pl.pallas_call(
        body, name=name, grid=(h_n // hp, nc // cps),
        in_specs=[tile, tile, tile, pl.BlockSpec((2, hp * LANES), lambda h, i: (0, h))],
        out_specs=[tile, pl.BlockSpec((hp, cps, LANES, LANES), lambda h, i: (h, i, 0, 0))],
        out_shape=[jax.ShapeDtypeStruct((s, d), F32), jax.ShapeDtypeStruct((h_n, nc, LANES, LANES), F32)],
        scratch_shapes=[pltpu.VMEM((hp, LANES, LANES), F32), pltpu.VMEM((cps, hp, c, LANES), F32)],
        compiler_params=_params("parallel", "arbitrary"),
    )(zq, zf, zi, lb_logits)


def _hgrn_bwd(zq, zf, zi, lb_logits, states, do, after, *, name):
    s, d = zq.shape
    h_n, c, hp, cps = d // LANES, HGRN_CHUNK, HGRN_HEADS_PER_STEP, HGRN_CHUNKS_PER_STEP
    nc = s // c
    n_steps = nc // cps

    def body(zq_ref, zf_ref, zi_ref, lb_ref, st_ref, do_ref, _, dzq_ref, dzf_ref, dzi_ref, dlb_ref, dstate_sc, b_sc):
        @pl.when(pl.program_id(1) == 0)
        def _():
            dstate_sc[...] = jnp.zeros_like(dstate_sc)
            dlb_ref[...] = jnp.zeros_like(dlb_ref)

        lower, upper = _tri(c, True), _tri(c, False).astype(F32)
        lower_f = lower.astype(F32)
        last_row = lax.broadcasted_iota(jnp.int32, (c, LANES), 0) == c - 1
        hs, pairs = range(hp), [(cc, hh) for cc in range(cps) for hh in range(hp)]
        sls = [slice(hh * LANES, (hh + 1) * LANES) for hh in hs]
        rws = [slice(cc * c, (cc + 1) * c) for cc in range(cps)]
        lb = [_lower_bound(lb_ref[:, sl]) for sl in sls]
        zq_v = {p: zq_ref[rws[p[0]], sls[p[1]]] for p in pairs}
        sq = {p: _sigmoid(zq_v[p]) for p in pairs}
        q = {p: zq_v[p] * sq[p] for p in pairs}
        sf = {p: _sigmoid(zf_ref[rws[p[0]], sls[p[1]]]) for p in pairs}
        f = {p: lb[p[1]] + (1.0 - lb[p[1]]) * sf[p] for p in pairs}
        k = {p: 1.0 - f[p] for p in pairs}
        v = {p: zi_ref[rws[p[0]], sls[p[1]]] for p in pairs}
        d_o = {p: do_ref[rws[p[0]], sls[p[1]]] for p in pairs}
        b = {p: _dot_f32(lower_f, jnp.log(f[p])) for p in pairs}
        s0t = {p: st_ref[p[1], p[0]] for p in pairs}
        for p in pairs:
            b_sc[p[0], p[1]] = b[p]
        bl = {p: b_sc[p[0], p[1], c - 1:c, :] for p in pairs}
        eb = {p: jnp.exp(b[p]) for p in pairs}
        ebl = {p: jnp.exp(bl[p]) for p in pairs}
        dec_end = {p: jnp.exp(bl[p] - b[p]) for p in pairs}
        da = {p: jnp.where(lower, _dot(d_o[p], v[p], _NT), 0.0) for p in pairs}
        dq = {p: _dot(d_o[p], s0t[p], _NN) * eb[p] for p in pairs}
        dstate_in = {p: _dot(d_o[p], q[p] * eb[p], _TN) for p in pairs}
        dk_intra = {p: jnp.zeros((c, LANES), F32) for p in pairs}
        scores, dq_blocks = {p: [] for p in pairs}, {p: [] for p in pairs}
        for i in range(c // HGRN_SUB):
            lo = i * HGRN_SUB
            for p in pairs:
                ref = b_sc[p[0], p[1], lo - 1:lo, :] if i > 0 else jnp.zeros((1, LANES), F32)
                grow = jnp.exp(b[p][lo:lo + HGRN_SUB, :] - ref)
                qt = q[p][lo:lo + HGRN_SUB, :] * grow
                dec = jnp.exp(jnp.minimum(ref - b[p], EXP_CLAMP))
                kd = k[p] * dec
                scores[p].append(_dot(qt, kd, _NT))
                da_i = da[p][lo:lo + HGRN_SUB, :]
                dq_blocks[p].append(_dot_f32(da_i, kd, _NN) * grow)
                dk_intra[p] = dk_intra[p] + _dot_f32(da_i, qt, _TN) * dec
        dv_intra = {p: _dot(jnp.where(lower, jnp.concatenate(scores[p], axis=0), 0.0), d_o[p], _TN) for p in pairs}
        dq = {p: dq[p] + jnp.concatenate(dq_blocks[p], axis=0) for p in pairs}
        q_dq = {p: q[p] * dq[p] for p in pairs}
        for p in pairs:
            dzq_ref[rws[p[0]], sls[p[1]]] = (dq[p] * sq[p] * (1.0 + zq_v[p] * (1.0 - sq[p]))).astype(BF16)
        dstate = [dstate_sc[hh] for hh in hs]
        for cc in reversed(range(cps)):
            ps = [(cc, hh) for hh in hs]
            dk_state = [_dot(v[p], dstate[p[1]], _NN) * dec_end[p] for p in ps]
            dv = [dv_intra[p] + _dot(k[p] * dec_end[p], dstate[p[1]], _NT) for p in ps]
            dk = [dk_intra[p] + dk_state[p[1]] for p in ps]
            db_last = [jnp.sum(k[p] * dk_state[p[1]], axis=0, keepdims=True)
                       + ebl[p] * jnp.sum(s0t[p] * dstate[p[1]], axis=0, keepdims=True) for p in ps]
            db = [q_dq[p] - k[p] * dk[p[1]] + jnp.where(last_row, db_last[p[1]], 0.0) for p in ps]
            df = [_dot_f32(upper, db[p[1]]) / f[p] - dk[p[1]] for p in ps]
            for p in ps:
                hh = p[1]
                dzf_ref[rws[cc], sls[hh]] = (df[hh] * (1.0 - lb[hh]) * sf[p] * (1.0 - sf[p])).astype(BF16)
                dlb_ref[:, sls[hh]] += jnp.sum(df[hh] * (1.0 - sf[p]), axis=0, keepdims=True)
                dzi_ref[rws[cc], sls[hh]] = dv[hh].astype(BF16)
            dstate = [dstate[p[1]] * ebl[p] + dstate_in[p] for p in ps]
        for hh in hs:
            dstate_sc[hh] = dstate[hh]

    tile = pl.BlockSpec((cps * c, hp * LANES), lambda h, i: (n_steps - 1 - i, h))
    out = jax.ShapeDtypeStruct((s, d), BF16)
    return pl.pallas_call(
        body, name=name, grid=(h_n // hp, n_steps),
        in_specs=[tile, tile, tile, pl.BlockSpec((2, hp * LANES), lambda h, i: (0, h)),
                  pl.BlockSpec((hp, cps, LANES, LANES), lambda h, i: (h, n_steps - 1 - i, 0, 0)), tile,
                  pl.BlockSpec(memory_space=pl.ANY)],
        out_specs=[tile, tile, tile, pl.BlockSpec((1, hp * LANES), lambda h, i: (0, h))],
        out_shape=[out, out, out, jax.ShapeDtypeStruct((1, d), F32)],
        scratch_shapes=[pltpu.VMEM((hp, LANES, LANES), F32), pltpu.VMEM((cps, hp, c, LANES), F32)],
        compiler_params=_params("parallel", "arbitrary"),
    )(zq, zf, zi, lb_logits, states, do, after)


ATTN_SUB_ROWS = 256
LOG2E = 1.4426950408889634
LN2 = 0.6931471805599453
Q_PRESCALE = ATTN_SCALE * LOG2E


def _attn_tile(s):
    return min(1024, max(128, s // 2))


def _causal_pairs(n, q_major):
    pairs = [(i, j) for i in range(n) for j in range(i + 1)] if q_major else [(i, j) for j in range(n) for i in range(j, n)]
    return jnp.asarray([p[0] for p in pairs], jnp.int32), jnp.asarray([p[1] for p in pairs], jnp.int32)


def _sub_scores(qn_ref, qr_ref, k, r, sub, t, diagonal):
    q = jnp.concatenate([qn_ref[r:r + sub, :], qr_ref[r:r + sub, :]], axis=1)
    if not diagonal:
        return q, _dot(q, k, _NT)
    cols = r + sub
    keep = lax.broadcasted_iota(jnp.int32, (sub, cols), 1) <= r + lax.broadcasted_iota(jnp.int32, (sub, cols), 0)
    return q, jnp.where(keep, _dot(q, k[:cols], _NT), -jnp.inf)


def _attn_fwd(qn, qr, kn, kr, v, *, name):
    s, t = qn.shape[0], _attn_tile(qn.shape[0])
    sub = min(t, ATTN_SUB_ROWS)
    q_blk, k_blk = _causal_pairs(s // t, True)

    def body(qi_ref, kj_ref, qn_ref, qr_ref, kn_ref, kr_ref, v_ref, o_ref, lse_ref, m_sc, l_sc, acc_sc):
        p_id = pl.program_id(1)
        i, j = qi_ref[p_id], kj_ref[p_id]

        @pl.when(j == 0)
        def _():
            m_sc[...] = jnp.full_like(m_sc, -jnp.inf)
            l_sc[...] = jnp.zeros_like(l_sc)
            acc_sc[...] = jnp.zeros_like(acc_sc)

        def update(diagonal):
            k = jnp.concatenate([kn_ref[...], kr_ref[...]], axis=1)
            v = v_ref[...]
            starts = list(range(0, t, sub))
            scs = [_sub_scores(qn_ref, qr_ref, k, r, sub, t, diagonal)[1] for r in starts]
            ps, alphas = [], []
            for r, sc in zip(starts, scs):
                m_prev = m_sc[r:r + sub, :]
                m_new = jnp.maximum(m_prev, jnp.max(sc, axis=1, keepdims=True))
                alpha = jnp.exp2(m_prev - m_new)
                p = jnp.exp2(sc - m_new[:, :1])
                l_sc[r:r + sub, :] = alpha * l_sc[r:r + sub, :] + jnp.sum(p, axis=1, keepdims=True)
                m_sc[r:r + sub, :] = m_new
                ps.append(p)
                alphas.append(alpha)
            for r, p, alpha in zip(starts, ps, alphas):
                acc_sc[r:r + sub, :] = alpha * acc_sc[r:r + sub, :] + _dot(p, v[:p.shape[1]], _NN)

        @pl.when(j < i)
        def _():
            update(False)

        @pl.when(j == i)
        def _():
            update(True)
            o_ref[...] = (acc_sc[...] / l_sc[...]).astype(BF16)
            lse_ref[...] = m_sc[...] + jnp.log(l_sc[...]) * LOG2E

    q_spec = pl.BlockSpec((t, LANES), lambda h, p, qi, kj: (qi[p], h))
    k_spec = pl.BlockSpec((t, LANES), lambda h, p, qi, kj: (kj[p], h))
    kr_spec = pl.BlockSpec((t, LANES), lambda h, p, qi, kj: (kj[p], 0))
    stat = pltpu.VMEM((t, LANES), F32)
    return pl.pallas_call(
        body, name=name,
        grid_spec=pltpu.PrefetchScalarGridSpec(
            num_scalar_prefetch=2, grid=(MLA_HEADS, q_blk.shape[0]),
            in_specs=[q_spec, q_spec, k_spec, kr_spec, k_spec], out_specs=[q_spec, q_spec],
            scratch_shapes=[stat, stat, stat]),
        out_shape=[jax.ShapeDtypeStruct(qn.shape, BF16), jax.ShapeDtypeStruct(qn.shape, F32)],
        compiler_params=_params("parallel", "arbitrary"),
    )(q_blk, k_blk, qn, qr, kn, kr, v)


def _attn_bwd(qn, qr, kn, kr, v, do, lse, delta, *, name):
    s, t = qn.shape[0], _attn_tile(qn.shape[0])
    n, sub = s // t, min(t, ATTN_SUB_ROWS)
    q_blk, k_blk = _causal_pairs(n, False)

    def body(qi_ref, kj_ref, qn_ref, qr_ref, kn_ref, kr_ref, v_ref, do_ref, lse_ref, delta_ref,
             dqn_ref, dqr_ref, dkn_ref, dv_ref, dkr_ref, dk_sc, dv_sc):
        p_id = pl.program_id(1)
        i, j = qi_ref[p_id], kj_ref[p_id]

        @pl.when(p_id == 0)
        def _():
            dqn_ref[...] = jnp.zeros_like(dqn_ref)
            dqr_ref[...] = jnp.zeros_like(dqr_ref)

        @pl.when(i == j)
        def _():
            dk_sc[...] = jnp.zeros_like(dk_sc)
            dv_sc[...] = jnp.zeros_like(dv_sc)

        def accumulate(diagonal):
            k = jnp.concatenate([kn_ref[...], kr_ref[...]], axis=1)
            v = v_ref[...]
            starts = list(range(0, t, sub))
            qs, d_os, scs, dps = [], [], [], []
            for r in starts:
                q, sc = _sub_scores(qn_ref, qr_ref, k, r, sub, t, diagonal)
                d_o = do_ref[r:r + sub, :]
                qs.append(q)
                d_os.append(d_o)
                scs.append(sc)
                dps.append(_dot(d_o, v[:sc.shape[1]], _NT))
            ps, dss = [], []
            for r, sc, dp in zip(starts, scs, dps):
                p = jnp.exp2(sc - lse_ref[r:r + sub, :][:, :1])
                ps.append(p.astype(BF16))
                dss.append((p * (dp - delta_ref[r:r + sub, :][:, :1])).astype(BF16))
            for r, q, d_o, p, ds in zip(starts, qs, d_os, ps, dss):
                cols = p.shape[1]
                dv_sc[:cols, :] += _dot(p, d_o, _TN)
                dk_sc[:cols, :] += _dot(ds, q, _TN)
                dq = _dot(ds, k[:cols], _NN) * ATTN_SCALE
                rows = pl.ds(pl.multiple_of(i * t + r, sub), sub)
                dqn_ref[rows, :] += dq[:, :LANES]
                dqr_ref[rows, :] += dq[:, LANES:]

        @pl.when(j < i)
        def _():
            accumulate(False)

        @pl.when(j == i)
        def _():
            accumulate(True)

        @pl.when(i == n - 1)
        def _():
            dkn_ref[...] = (dk_sc[:, :LANES] * LN2).astype(BF16)
            dkr_ref[...] = dk_sc[:, LANES:] * LN2
            dv_ref[...] = dv_sc[...].astype(BF16)

    q_spec = pl.BlockSpec((t, LANES), lambda h, p, qi, kj: (qi[p], h))
    k_spec = pl.BlockSpec((t, LANES), lambda h, p, qi, kj: (kj[p], h))
    kr_spec = pl.BlockSpec((t, LANES), lambda h, p, qi, kj: (kj[p], 0))
    head_spec = pl.BlockSpec((s, LANES), lambda h, p, qi, kj: (0, h))
    f32_out, bf16_out = jax.ShapeDtypeStruct(qn.shape, F32), jax.ShapeDtypeStruct(qn.shape, BF16)
    return pl.pallas_call(
        body, name=name,
        grid_spec=pltpu.PrefetchScalarGridSpec(
            num_scalar_prefetch=2, grid=(MLA_HEADS, q_blk.shape[0]),
            in_specs=[q_spec, q_spec, k_spec, kr_spec, k_spec, q_spec, q_spec, q_spec],
            out_specs=[head_spec, head_spec, k_spec, k_spec, k_spec],
            scratch_shapes=[pltpu.VMEM((t, 2 * LANES), F32), pltpu.VMEM((t, LANES), F32)]),
        out_shape=[f32_out, f32_out, bf16_out, bf16_out, f32_out],
        compiler_params=_params("parallel", "arbitrary"),
    )(q_blk, k_blk, qn, qr, kn, kr, v, do, lse, delta)


def _exchange(arrs, *, scatter, name):
    n = len(arrs)
    out_shape = [jax.ShapeDtypeStruct(a.shape if scatter else (N_DEV, *a.shape), a.dtype) for a in arrs]

    def body(*refs):
        ins, outs = refs[:n], refs[n:2 * n]
        send_sems, recv_sems, local_sems = refs[2 * n:]
        x, y, c = lax.axis_index("x"), lax.axis_index("y"), lax.axis_index("c")
        me = 4 * x + 2 * y + c
        copies = []
        for k in range(n):
            local = pltpu.make_async_copy(ins[k].at[me] if scatter else ins[k], outs[k].at[me], local_sems.at[k])
            local.start()
            copies.append(local)
            for d in range(1, N_DEV):
                px, py, pc = (x + (d >> 2)) % 2, (y + ((d >> 1) & 1)) % 2, (c + (d & 1)) % 2
                peer = 4 * px + 2 * py + pc
                remote = pltpu.make_async_remote_copy(
                    src_ref=ins[k].at[peer] if scatter else ins[k], dst_ref=outs[k].at[me],
                    send_sem=send_sems.at[k, d - 1], recv_sem=recv_sems.at[k, d - 1],
                    device_id=(px, py, pc), device_id_type=pl.DeviceIdType.MESH)
                remote.start()
                copies.append(remote)
        for cp in copies:
            cp.wait()

    any_spec = pl.BlockSpec(memory_space=pl.ANY)
    return pl.pallas_call(
        body, name=name, in_specs=[any_spec] * n, out_specs=[any_spec] * n, out_shape=out_shape,
        scratch_shapes=[pltpu.SemaphoreType.DMA((n, N_DEV - 1)), pltpu.SemaphoreType.DMA((n, N_DEV - 1)),
                        pltpu.SemaphoreType.DMA((n,))],
    )(*arrs)


def _peers(x, y, c):
    out = []
    for d in range(1, N_DEV):
        px, py, pc = (x + (d >> 2)) % 2, (y + ((d >> 1) & 1)) % 2, (c + (d & 1)) % 2
        out.append(((px, py, pc), 4 * px + 2 * py + pc))
    return out


CHIP_LEVEL_PEERS = (1, 2, 4, 6)


def _exchange_copies(ins, lands, send_sems, recv_sems, scatter, chip_level=False):
    x, y, c = lax.axis_index("x"), lax.axis_index("y"), lax.axis_index("c")
    me = 4 * x + 2 * y + c
    local, remote = [], []
    for k in range(len(ins)):
        local.append(pltpu.make_async_copy(ins[k].at[me] if scatter else ins[k], lands[k].at[me],
                                           recv_sems.at[k * N_DEV + N_DEV - 1]))
        for d, (coords, peer) in enumerate(_peers(x, y, c)):
            if chip_level and d + 1 not in CHIP_LEVEL_PEERS:
                continue
            remote.append(pltpu.make_async_remote_copy(
                src_ref=ins[k].at[peer] if scatter else ins[k], dst_ref=lands[k].at[me],
                send_sem=send_sems.at[k * N_DEV + d], recv_sem=recv_sems.at[k * N_DEV + d],
                device_id=coords, device_id_type=pl.DeviceIdType.MESH))
    return local, remote


def _exchange_start(arrs, *, scatter, name, after=None, chip_level=False):
    n = len(arrs)
    hbm = pl.BlockSpec(memory_space=pltpu.HBM)
    sem = pl.BlockSpec(memory_space=pltpu.SEMAPHORE)
    lands = [lax.empty(a.shape if scatter else (N_DEV, *a.shape), a.dtype) for a in arrs]

    def body(*refs):
        ins, land_refs = refs[:n], refs[n:2 * n]
        first_out = 2 * n + (after is not None)
        send_sems, recv_sems, token = refs[first_out], refs[first_out + 1], refs[-1]
        local, remote = _exchange_copies(ins, land_refs, send_sems, recv_sems, scatter, chip_level)
        for cp in local + remote:
            cp.start()
        token[...] = jnp.zeros_like(token)

    operands = [pltpu.with_memory_space_constraint(a, pltpu.HBM) for a in list(arrs) + lands]
    behind = [] if after is None else [after]
    res = pl.pallas_call(
        body, name=name,
        out_shape=(pltpu.SemaphoreType.DMA((n * N_DEV,)), pltpu.SemaphoreType.DMA((n * N_DEV,)),
                   *[pltpu.HBM(o.shape, o.dtype) for o in operands], jax.ShapeDtypeStruct((8, LANES), F32)),
        in_specs=[hbm] * (2 * n) + [pl.BlockSpec(memory_space=pl.ANY)] * len(behind),
        out_specs=(sem, sem, *[hbm] * (2 * n), pl.BlockSpec(memory_space=pltpu.VMEM)),
        input_output_aliases={i: 2 + i for i in range(2 * n)},
        compiler_params=pltpu.CompilerParams(has_side_effects=pltpu.SideEffectType.DATAFLOW_SIDE_EFFECTING),
    )(*operands, *behind)
    return (res[0], res[1], list(res[2:2 + n]), list(res[2 + n:2 + 2 * n]), scatter, chip_level), res[-1]


def _exchange_wait(state, after, *, name):
    send_sems, recv_sems, ins, lands, scatter, chip_level = state
    n = len(ins)
    hbm = pl.BlockSpec(memory_space=pltpu.HBM)
    sem = pl.BlockSpec(memory_space=pltpu.SEMAPHORE)

    def body(*refs):
        in_refs, land_refs = refs[:n], refs[n:2 * n]
        local, remote = _exchange_copies(in_refs, land_refs, refs[2 * n], refs[2 * n + 1], scatter, chip_level)
        for cp in local:
            cp.wait()
        for cp in remote:
            cp.wait_send()
            cp.wait_recv()

    res = pl.pallas_call(
        body, name=name, out_shape=tuple(pltpu.HBM(o.shape, o.dtype) for o in ins + lands),
        in_specs=[hbm] * (2 * n) + [sem, sem, pl.BlockSpec(memory_space=pl.ANY)], out_specs=tuple([hbm] * (2 * n)),
        input_output_aliases={i: i for i in range(2 * n)},
        compiler_params=pltpu.CompilerParams(has_side_effects=pltpu.SideEffectType.DATAFLOW_SIDE_EFFECTING),
    )(*ins, *lands, send_sems, recv_sems, after)
    return list(res[n:])


def _chip_forward(lands, *, name):
    n = len(lands)

    def body(*refs):
        ins, outs, send_sems, recv_sems = refs[:n], refs[n:2 * n], refs[2 * n], refs[2 * n + 1]
        x, y, c = lax.axis_index("x"), lax.axis_index("y"), lax.axis_index("c")
        copies = []
        for k in range(n):
            for j, (dx, dy) in enumerate(((0, 1), (1, 0), (1, 1))):
                held = 4 * ((x + dx) % 2) + 2 * ((y + dy) % 2) + c
                cp = pltpu.make_async_remote_copy(
                    src_ref=ins[k].at[held], dst_ref=outs[k].at[held], send_sem=send_sems.at[k, j],
                    recv_sem=recv_sems.at[k, j], device_id=(x, y, 1 - c), device_id_type=pl.DeviceIdType.MESH)
                cp.start()
                copies.append(cp)
        for cp in copies:
            cp.wait()

    any_spec = pl.BlockSpec(memory_space=pl.ANY)
    return pl.pallas_call(
        body, name=name, in_specs=[any_spec] * n, out_specs=[any_spec] * n,
        out_shape=[jax.ShapeDtypeStruct(a.shape, a.dtype) for a in lands], input_output_aliases={k: k for k in range(n)},
        scratch_shapes=[pltpu.SemaphoreType.DMA((n, 3)), pltpu.SemaphoreType.DMA((n, 3))],
    )(*lands)


def _adam(w, terms, m, v, *, name):
    n_layers, r, c = w.shape
    tr = min(r, ADAM_ROW_TILE)
    assert r % tr == 0 and len(terms) == n_layers
    steps = r // tr

    def body(w_ref, *rest):
        t_refs, (m_ref, v_ref, g_out, d_out, m_out, v_out) = rest[:n_layers], rest[n_layers:]
        for layer, t_ref in enumerate(t_refs):
            @pl.when(pl.program_id(0) == layer)
            def _(t_ref=t_ref):
                g = t_ref[0].astype(F32)
                for s in range(1, t_ref.shape[0]):
                    g = g + t_ref[s].astype(F32)
                m1 = ADAM_B1 * m_ref[...] + (1.0 - ADAM_B1) * g
                v1 = ADAM_B2 * v_ref[...] + (1.0 - ADAM_B2) * jnp.square(g)
                m_hat = m1 / (1.0 - ADAM_B1 ** ADAM_STEP)
                v_hat = v1 / (1.0 - ADAM_B2 ** ADAM_STEP)
                g_out[...] = g
                d_out[...] = -ADAM_LR * (m_hat / (jnp.sqrt(v_hat) + ADAM_EPS) + ADAM_WD * w_ref[...])
                m_out[...] = m1
                v_out[...] = v1

    def term_spec(layer, t):
        return pl.BlockSpec((t.shape[0], tr, c),
                            lambda l, i: (0, jnp.where(l == layer, i, jnp.where(l < layer, 0, steps - 1)), 0))

    spec = pl.BlockSpec((None, tr, c), lambda l, i: (l, i, 0))
    out = jax.ShapeDtypeStruct(w.shape, F32)
    return pl.pallas_call(
        body, name=name, grid=(n_layers, steps),
        in_specs=[spec] + [term_spec(layer, t) for layer, t in enumerate(terms)] + [spec, spec], out_specs=[spec] * 4,
        out_shape=[out] * 4, compiler_params=_params("arbitrary", "arbitrary"),
    )(w, *terms, m, v)


def _sum_terms(terms, *, name):
    n, _, p = terms.shape

    def body(t_ref, o_ref):
        acc = t_ref[0]
        for s in range(1, n):
            acc = acc + t_ref[s]
        o_ref[...] = acc

    return pl.pallas_call(body, name=name, out_shape=jax.ShapeDtypeStruct((1, p), F32))(terms)


def _lb_logits_grad(dlb, logits, *, name):
    def body(dlb_ref, l_ref, o_ref):
        lb = _lower_bound(l_ref[...])
        d0 = dlb_ref[...] * lb * (1.0 - lb)
        o_ref[...] = jnp.concatenate([d0, -d0], axis=0)

    return pl.pallas_call(body, name=name, out_shape=jax.ShapeDtypeStruct(logits.shape, F32))(dlb, logits)


def _silu_grad(z):
    sg = _sigmoid(z)
    return sg * (1.0 + z * (1.0 - sg))


def _head_norm_gate(o, zg, gn):
    outs = []
    for h in range(HGRN_HEADS):
        sl = slice(h * LANES, (h + 1) * LANES)
        zg_h = zg[:, sl]
        outs.append(_rms(o[:, sl], gn) * (zg_h * _sigmoid(zg_h)))
    return (jnp.concatenate(outs, axis=1),)


def _head_norm_gate_bwd(o, zg, dm, gn):
    do_parts, dzg_parts, dgn = [], [], jnp.zeros((1, LANES), F32)
    for h in range(HGRN_HEADS):
        sl = slice(h * LANES, (h + 1) * LANES)
        o_h, zg_h, dm_h = o[:, sl], zg[:, sl], dm[:, sl]
        gate = zg_h * _sigmoid(zg_h)
        do_h, dgn_h = _rms_bwd(o_h, gn, dm_h * gate)
        dgn = dgn + dgn_h
        do_parts.append(do_h)
        dzg_parts.append(dm_h * _rms(o_h, gn) * _silu_grad(zg_h))
    return jnp.concatenate(do_parts, axis=1), jnp.concatenate(dzg_parts, axis=1), dgn


def _rope_slabs(x, t_c, t_s1, t_s2, transpose):
    fn = _rope_t if transpose else _rope
    return jnp.concatenate(
        [fn(x[:, h * LANES:(h + 1) * LANES], t_c, t_s1, t_s2) for h in range(x.shape[1] // LANES)], axis=1)


def _loss_head(h, tgt, w):
    d = h.shape[1]
    r = lax.rsqrt(jnp.mean(h * h, axis=-1, keepdims=True) + EPS)
    xh = h * r
    err = xh * w - tgt
    loss = 0.5 * jnp.sum(jnp.mean(err * err, axis=-1, keepdims=True), axis=0, keepdims=True)
    dy = err / d
    dxh = dy * w
    dh = r * (dxh - xh * jnp.mean(dxh * xh, axis=-1, keepdims=True))
    return dh, dh, jnp.sum(dy * xh, axis=0, keepdims=True), jnp.broadcast_to(loss, (1, LANES))


def _mlp_fwd(h, norm, w_up, w_down, tag, loss_head=None):
    d = h.shape[1]

    def up(x, g, wu):
        x_n = _rms(x, g).astype(BF16)
        return x_n, jnp.concatenate([jnp.square(jnp.maximum(_dot(x_n, wu[j], _NN), 0.0)) for j in range(wu.shape[0])],
                                    axis=1)

    xn, act = _rowcall(up, [h], [norm, w_up], [(d, BF16), (w_up.shape[0] * w_up.shape[2], BF16)], [], tr=512,
                       name=f"{tag}_up")
    if callable(w_down):
        w_down = w_down(act)
    if loss_head is None:
        return _rowcall(lambda a, res, wd: (res + _dot(a, wd, _NN),), [act, h], [w_down], [(d, F32)], [],
                        tr=512, name=f"{tag}_down")[0], (h, xn, act)
    tgt, final_norm = loss_head

    def down_and_loss(a, res, t, wd, g):
        return _loss_head(res + _dot(a, wd, _NN), t, g)

    return _rowcall(down_and_loss, [act, h, tgt], [w_down, final_norm], [(d, F32), (d, BF16)], [d, LANES],
                    name=f"{tag}_down_loss"), (h, xn, act)


def _mlp_bwd(dh_out, dh_out_bf, saved, norm, w_up, w_down, tag, after=None):
    h, xn, act = saved
    d = h.shape[1]
    du = _rowcall(lambda dres, a, wd: (_dot(dres, wd, _NT) * (2.0 * jnp.sqrt(a.astype(F32))),), [dh_out_bf, act],
                  [w_down], [(act.shape[1], BF16)], [], tr=512, after=after, name=f"{tag}_bwd_du")[0]
    dw_down = _mm(act, dh_out_bf, mode="tn", name=f"{tag}_bwd_wdown")
    dw_up = _mm(xn, du, mode="tn", col_shards=w_up.shape[0], name=f"{tag}_bwd_wup")

    def up_norm_bwd(x, d_u, dres, g, wu):
        cols = wu.shape[2]
        dxn = _dot(d_u[:, :cols], wu[0], _NT)
        for j in range(1, wu.shape[0]):
            dxn = dxn + _dot(d_u[:, j * cols:(j + 1) * cols], wu[j], _NT)
        dx, dw = _rms_bwd(x, g, dxn)
        return dx + dres, dx + dres, dw

    dh, dh_bf, dnorm = _rowcall(up_norm_bwd, [h, du, dh_out], [norm, w_up], [(d, F32), (d, BF16)], [d], tr=512,
                                name=f"{tag}_bwd_dxn")
    return dh, dh_bf, dnorm, dw_up, dw_down


def _row_major(g):
    return g.reshape(g.shape[0] * g.shape[1], g.shape[2])


def _col_major(g):
    return jnp.transpose(g, (1, 0, 2)).reshape(g.shape[1], g.shape[0] * g.shape[2])


def _col_terms(dw):
    k, n = dw.shape
    return jnp.transpose(dw.reshape(k, N_DEV, n // N_DEV), (1, 0, 2))


def _row_terms(dw):
    return dw.reshape(N_DEV, dw.shape[0] // N_DEV, dw.shape[1])


def kernel(x, hgrn_norm, hgrn_w_q, hgrn_w_f, hgrn_w_i, hgrn_w_g, hgrn_g_norm, hgrn_w_o, hgrn_lb_logits, mla_norm, mla_w_dq, mla_q_norm, mla_w_uq, mla_w_o, kv_in_norm, kv_w_dkv, kv_norm, kv_w_uk, kv_w_uv, mlp_norm, mlp_w_up, mlp_w_down, final_norm, loss_target, m_hgrn_norm, m_hgrn_w_q, m_hgrn_w_f, m_hgrn_w_i, m_hgrn_w_g, m_hgrn_g_norm, m_hgrn_w_o, m_hgrn_lb_logits, m_mla_norm, m_mla_w_dq, m_mla_q_norm, m_mla_w_uq, m_mla_w_o, m_kv_in_norm, m_kv_w_dkv, m_kv_norm, m_kv_w_uk, m_kv_w_uv, m_mlp_norm, m_mlp_w_up, m_mlp_w_down, m_final_norm, v_hgrn_norm, v_hgrn_w_q, v_hgrn_w_f, v_hgrn_w_i, v_hgrn_w_g, v_hgrn_g_norm, v_hgrn_w_o, v_hgrn_lb_logits, v_mla_norm, v_mla_w_dq, v_mla_q_norm, v_mla_w_uq, v_mla_w_o, v_kv_in_norm, v_kv_w_dkv, v_kv_norm, v_kv_w_uk, v_kv_w_uv, v_mlp_norm, v_mlp_w_up, v_mlp_w_down, v_final_norm):
    given = dict(locals())
    weight_names = ["hgrn_norm", "hgrn_w_q", "hgrn_w_f", "hgrn_w_i", "hgrn_w_g", "hgrn_g_norm", "hgrn_w_o",
                    "hgrn_lb_logits", "mla_norm", "mla_w_dq", "mla_q_norm", "mla_w_uq", "mla_w_o", "kv_in_norm",
                    "kv_w_dkv", "kv_norm", "kv_w_uk", "kv_w_uv", "mlp_norm", "mlp_w_up", "mlp_w_down", "final_norm"]
    me = 4 * lax.axis_index("x") + 2 * lax.axis_index("y") + lax.axis_index("c")
    xs, tgt = x[0], loss_target[0]
    seq, d_model = xs.shape
    n_heads, hd = MLA_HEADS, LANES

    big_local = {
        "hgrn_w_q": hgrn_w_q[0], "hgrn_w_f": hgrn_w_f[0], "hgrn_w_i": hgrn_w_i[0], "hgrn_w_g": hgrn_w_g[0],
        "hgrn_w_o": hgrn_w_o[0], "mla_w_dq": mla_w_dq[0], "mla_w_uq": mla_w_uq[0], "mla_w_o": mla_w_o[0],
        "kv_w_dkv": kv_w_dkv, "kv_w_uk": kv_w_uk, "kv_w_uv": kv_w_uv,
        "mlp_w_up0": mlp_w_up[0], "mlp_w_up1": mlp_w_up[1], "mlp_w_down0": mlp_w_down[0], "mlp_w_down1": mlp_w_down[1],
    }
    big_names = list(big_local)
    col_sharded = {"mla_w_uq", "kv_w_uk", "kv_w_uv"}
    shard_major = {"mlp_w_up0", "mlp_w_up1"}
    vec_local = jnp.concatenate([hgrn_norm, hgrn_lb_logits], axis=0)
    first_names = ["hgrn_w_q", "hgrn_w_f", "hgrn_w_i"]
    proj_names = first_names + ["hgrn_w_g"]
    later_names = {"hgrn_o": ["hgrn_w_g", "hgrn_w_o"], "up0": ["mlp_w_up0"], "down0": ["mlp_w_down0"],
                   "mla": ["kv_w_dkv", "kv_w_uk", "kv_w_uv", "mla_w_dq", "mla_w_uq", "mla_w_o"],
                   "mlp1": ["mlp_w_up1", "mlp_w_down1"]}

    def unshard(names, arrays):
        return {k: (a if k in shard_major else _col_major(a) if k in col_sharded else _row_major(a))
                for k, a in zip(names, arrays)}

    two_level = {"down0", "mla"}
    first_state, token = _exchange_start([big_local[k].astype(BF16) for k in first_names] + [vec_local], scatter=False,
                                         chip_level=True, name="gather_first_start")
    gather_state = {}
    for tag, names in later_names.items():
        gather_state[tag], token = _exchange_start([big_local[k].astype(BF16) for k in names], scatter=False,
                                                   chip_level=tag in two_level, after=token, name=f"gather_{tag}_start")

    def gather_wait(tag, after):
        landed = _exchange_wait(gather_state[tag], after, name=f"gather_{tag}_wait")
        if tag in two_level:
            landed = _chip_forward(landed, name=f"gather_{tag}_forward")
        w.update(unshard(later_names[tag], landed))
        return [w[k] for k in later_names[tag]]

    gathered = _chip_forward(_exchange_wait(first_state, token, name="gather_first_wait"), name="gather_first_forward")
    w = unshard(first_names, gathered[:-1])
    vec_full = jnp.transpose(gathered[-1], (1, 0, 2)).reshape(3, d_model)
    hgrn_norm_full, lb_logits_full = vec_full[0:1], vec_full[1:3]
    t_c, t_s1, t_s2 = _rope_tables(seq)
    kv_lora = kv_w_uk.shape[0]

    def hgrn_proj(a, g, *weights):
        xn = _rms(a, g).astype(BF16)
        return (xn, *[_dot(xn, wt, _NN) for wt in weights])

    xn0, zq, zf, zi = _rowcall(hgrn_proj, [xs], [hgrn_norm_full] + [w[k] for k in first_names],
                               [(d_model, BF16)] + [(d_model, F32)] * 3, [], tr=512, name="hgrn_proj")
    o_rec, states = _hgrn_fwd(zq, zf, zi, lb_logits_full, name="hgrn_fwd")
    gather_wait("hgrn_o", o_rec)

    def gate_out(o, x_n, res, gn, wg, wo):
        z = _dot(x_n, wg, _NN)
        m = _head_norm_gate(o, z, gn)[0].astype(BF16)
        return z, m, res + _dot(m, wo, _NN)

    zg, mixed, h1 = _rowcall(gate_out, [o_rec, xn0, xs], [hgrn_g_norm, w["hgrn_w_g"], w["hgrn_w_o"]],
                             [(d_model, F32), (d_model, BF16), (d_model, F32)], [], name="hgrn_gate_out")
    h2, mlp0_saved = _mlp_fwd(h1, mlp_norm[0:1], gather_wait("up0", h1)[0], lambda act: gather_wait("down0", act)[0],
                              "mlp0")
    gather_wait("mla", h2)
    w_uq3 = w["mla_w_uq"].reshape(-1, n_heads, MLA_NOPE + MLA_ROPE)
    w_uq_nope = w_uq3[:, :, :MLA_NOPE].reshape(-1, n_heads * hd)
    w_uq_rope = jnp.pad(w_uq3[:, :, MLA_NOPE:], ((0, 0), (0, 0), (0, hd - MLA_ROPE))).reshape(-1, n_heads * hd)
    w_dkv_pad = jnp.pad(w["kv_w_dkv"], ((0, 0), (0, kv_lora + hd - w["kv_w_dkv"].shape[1])))

    q_lora, qk_cols = w["mla_w_dq"].shape[1], n_heads * hd

    def mla_qkv(a, tc, ts1, ts2, g_kv_in, g_mla, g_q, g_kv, wdq, wn, wr, wdkv, wuk, wuv):
        h_n, x_n = _rms(a, g_kv_in).astype(BF16), _rms(a, g_mla).astype(BF16)
        cq = _dot(x_n, wdq, _NN)
        cq_n = _rms(cq, g_q).astype(BF16)
        q_nope = _dot(cq_n, wn, _NN) * Q_PRESCALE
        q_rope = _rope_slabs(_dot(cq_n, wr, _NN) * Q_PRESCALE, tc, ts1, ts2, False)
        c_all = _dot(h_n, wdkv, _NN)
        lat = _rms(c_all[:, :kv_lora], g_kv).astype(BF16)
        return (h_n, x_n, cq, cq_n, q_nope, q_rope, c_all, lat, _rope(c_all[:, kv_lora:], tc, ts1, ts2),
                _dot(lat, wuk, _NN), _dot(lat, wuv, _NN))

    hn, xn2, cq_pre, c_q, qn, qr, ckr, c_kv, kr, kn, vv = _rowcall(
        mla_qkv, [h2, t_c, t_s1, t_s2],
        [kv_in_norm[None, :], mla_norm, mla_q_norm, kv_norm[None, :], w["mla_w_dq"], w_uq_nope, w_uq_rope, w_dkv_pad,
         w["kv_w_uk"], w["kv_w_uv"]],
        [(d_model, BF16), (d_model, BF16), (q_lora, F32), (q_lora, BF16), (qk_cols, BF16), (qk_cols, BF16),
         (kv_lora + hd, F32), (kv_lora, BF16), (hd, BF16), (qk_cols, BF16), (qk_cols, BF16)], [], tr=512, name="mla_qkv")
    o_att, lse = _attn_fwd(qn, qr, kn, kr, vv, name="attn_fwd")
    h3 = _rowcall(lambda o, res, wo: (res + _dot(o, wo, _NN),), [o_att, h2], [w["mla_w_o"]], [(d_model, F32)], [],
                  tr=512, name="attn_out")[0]
    gather_wait("mlp1", h3)
    (dh4, dh4_bf, g_final_norm, loss_part), mlp1_saved = _mlp_fwd(
        h3, mlp_norm[1:2], w["mlp_w_up1"], w["mlp_w_down1"], "mlp1", loss_head=(tgt, final_norm[None, :]))

    g = {}
    groups = {"mlp1": ["mlp_w_up1", "mlp_w_down1"],
              "mla": ["mla_w_o", "mla_w_uq", "mla_w_dq", "kv_w_uk", "kv_w_uv", "kv_w_dkv"],
              "mlp0": ["mlp_w_up0", "mlp_w_down0"],
              "hgrn_out": ["hgrn_w_o", "hgrn_w_g"],
              "hgrn_in": ["hgrn_w_q", "hgrn_w_f", "hgrn_w_i"]}
    scatter_state = {}

    def scatter_start(tag, after=None):
        scatter_state[tag], tok = _exchange_start(
            [g[k] if k in shard_major else (_col_terms if k in col_sharded else _row_terms)(g[k]) for k in groups[tag]],
            scatter=True, after=after,
            name=f"scatter_{tag}_start")
        return tok

    dh3, dh3_bf, g_mlp_norm1, g["mlp_w_up1"], g["mlp_w_down1"] = _mlp_bwd(
        dh4, dh4_bf, mlp1_saved, mlp_norm[1:2], w["mlp_w_up1"], w["mlp_w_down1"], "mlp1")
    def attn_out_bwd(dres, o, wo):
        d_o = _dot(dres, wo, _NT).astype(BF16)
        prod = d_o.astype(F32) * o.astype(F32)
        return d_o, jnp.concatenate([jnp.broadcast_to(jnp.sum(prod[:, h * hd:(h + 1) * hd], axis=1, keepdims=True),
                                                      (prod.shape[0], hd)) for h in range(n_heads)], axis=1)

    d_oatt, delta = _rowcall(attn_out_bwd, [dh3_bf, o_att], [w["mla_w_o"]], [(qk_cols, BF16), (qk_cols, F32)], [],
                             after=scatter_start("mlp1"), name="attn_out_bwd_x")
    g["mla_w_o"] = _mm(o_att, dh3_bf, mode="tn", name="attn_out_bwd_w")
    dqn, dqr, dkn, dvv, dkr = _attn_bwd(qn, qr, kn, kr, vv, d_oatt, lse, delta, name="attn_bwd")

    def q_path_bwd(cq, cq_n, x_n, d_qn, d_qr, tc, ts1, ts2, g_q, wdq, wn, wr):
        d_qn, d_qr = d_qn.astype(BF16), _rope_slabs(d_qr, tc, ts1, ts2, True).astype(BF16)
        d_cq, d_gq = _rms_bwd(cq, g_q, _dot(d_qn, wn, _NT) + _dot(d_qr, wr, _NT))
        d_cq = d_cq.astype(BF16)
        return _dot(d_cq, wdq, _NT), d_gq, _dot(x_n, d_cq, _TN), _dot(cq_n, d_qn, _TN), _dot(cq_n, d_qr, _TN)

    dxn2, g_q_norm, g_dq, g_uq_nope, g_uq_rope = _rowcall(
        q_path_bwd, [cq_pre, c_q, xn2, dqn, dqr, t_c, t_s1, t_s2], [mla_q_norm, w["mla_w_dq"], w_uq_nope, w_uq_rope],
        [(d_model, F32)], [q_lora, (d_model, q_lora), (q_lora, qk_cols), (q_lora, qk_cols)], tr=512, name="mla_q_bwd")
    g["mla_w_dq"] = g_dq.astype(GRAD_WIRE_DTYPE)
    g["mla_w_uq"] = jnp.concatenate([g_uq_nope.reshape(q_lora, n_heads, hd),
                                     g_uq_rope.reshape(q_lora, n_heads, hd)[:, :, :MLA_ROPE]],
                                    axis=2).reshape(q_lora, -1).astype(GRAD_WIRE_DTYPE)

    def kv_path_bwd(c_all, lat, h_n, d_kn, d_v, d_kr_heads, tc, ts1, ts2, a, d_xn2, dres,
                    g_kv, g_kv_in, g_mla, wdkv, wuk, wuv):
        d_lat, d_gkv = _rms_bwd(c_all[:, :kv_lora], g_kv, _dot(d_kn, wuk, _NT) + _dot(d_v, wuv, _NT))
        d_kr = d_kr_heads[:, :hd]
        for h in range(1, n_heads):
            d_kr = d_kr + d_kr_heads[:, h * hd:(h + 1) * hd]
        d_all = jnp.concatenate([d_lat, _rope_t(d_kr, tc, ts1, ts2)], axis=1).astype(BF16)
        dx1, d_gkv_in = _rms_bwd(a, g_kv_in, _dot(d_all, wdkv, _NT))
        dx2, d_gmla = _rms_bwd(a, g_mla, d_xn2)
        d_a = dx1 + dx2 + dres
        return (d_a, d_a, d_gkv, d_gkv_in, d_gmla, _dot(h_n, d_all, _TN), _dot(lat, d_kn, _TN), _dot(lat, d_v, _TN))

    dh2, dh2_bf, g_kv_norm, g_kv_in_norm, g_mla_norm, g_dkv, g_uk, g_uv = _rowcall(
        kv_path_bwd, [ckr, c_kv, hn, dkn, dvv, dkr, t_c, t_s1, t_s2, h2, dxn2, dh3],
        [kv_norm[None, :], kv_in_norm[None, :], mla_norm, w_dkv_pad, w["kv_w_uk"], w["kv_w_uv"]],
        [(d_model, F32), (d_model, BF16)],
        [kv_lora, d_model, d_model, (d_model, kv_lora + hd), (kv_lora, qk_cols), (kv_lora, qk_cols)], name="mla_kv_bwd")
    g["kv_w_dkv"] = g_dkv[:, :kv_w_dkv.shape[1]].astype(GRAD_WIRE_DTYPE)
    g["kv_w_uk"], g["kv_w_uv"] = g_uk.astype(GRAD_WIRE_DTYPE), g_uv.astype(GRAD_WIRE_DTYPE)
    dh1, dh1_bf, g_mlp_norm0, g["mlp_w_up0"], g["mlp_w_down0"] = _mlp_bwd(
        dh2, dh2_bf, mlp0_saved, mlp_norm[0:1], w["mlp_w_up0"], w["mlp_w_down0"], "mlp0", after=scatter_start("mla"))

    g["hgrn_w_o"] = _mm(mixed, dh1_bf, mode="tn", after=scatter_start("mlp0"), name="hgrn_out_bwd_w")
    do_rec, dzg, g_g_norm = _rowcall(
        lambda dres, o, z, wo, gn: _head_norm_gate_bwd(o, z, _dot(dres, wo, _NT), gn), [dh1_bf, o_rec, zg],
        [w["hgrn_w_o"], hgrn_g_norm], [(d_model, F32), (d_model, BF16)], [hd], name="hgrn_gate_out_bwd")
    g["hgrn_w_g"] = _mm(xn0, dzg, mode="tn", name="hgrn_w_g_bwd_w")
    dzq, dzf, dzi, g_lb = _hgrn_bwd(zq, zf, zi, lb_logits_full, states, do_rec, scatter_start("hgrn_out"),
                                    name="hgrn_bwd")
    for nm, dz in (("hgrn_w_q", dzq), ("hgrn_w_f", dzf), ("hgrn_w_i", dzi)):
        g[nm] = _mm(xn0, dz, mode="tn", name=f"{nm}_bwd_w")

    def hgrn_proj_bwd(a, dres, *rest):
        dzs, gw, weights = rest[:4], rest[4], rest[5:]
        dxn = _dot(dzs[0], weights[0], _NT)
        for dz, wt in zip(dzs[1:], weights[1:]):
            dxn = dxn + _dot(dz, wt, _NT)
        dx, dw = _rms_bwd(a, gw, dxn)
        return dx + dres, dw

    grad_x, g_hgrn_norm = _rowcall(hgrn_proj_bwd, [xs, dh1, dzq, dzf, dzi, dzg],
                                   [hgrn_norm_full] + [w[k] for k in proj_names], [(d_model, F32)], [d_model],
                                   tr=512, name="hgrn_proj_bwd")

    small_parts = [g_hgrn_norm, g_lb, g_g_norm, g_mla_norm, g_q_norm, g_kv_in_norm, g_kv_norm, g_mlp_norm0,
                   g_mlp_norm1, g_final_norm, loss_part]
    small_sizes = [p.shape[1] for p in small_parts]
    small_terms = _exchange([jnp.concatenate(small_parts, axis=1)], scatter=False, name="gather_small")[0]
    small_sum = _sum_terms(small_terms, name="sum_small")
    last = scatter_start("hgrn_in", after=small_sum)
    offs = [0]
    for sz in small_sizes:
        offs.append(offs[-1] + sz)
    (s_hgrn_norm, s_lb, s_g_norm, s_mla_norm, s_q_norm, s_kv_in_norm, s_kv_norm, s_mlp_norm0, s_mlp_norm1, s_final_norm,
     s_loss) = [small_sum[:, a:b] for a, b in zip(offs[:-1], offs[1:])]
    shard = hgrn_norm.shape[1]
    g_lb_logits = _lb_logits_grad(lax.dynamic_slice_in_dim(s_lb, me * shard, shard, axis=1), hgrn_lb_logits,
                                  name="lb_logits_grad")
    loss = s_loss[0, 0]

    res, layer_terms = {}, {}

    def update(k, term_list):
        shape = given[k].shape
        as_layers = (len(term_list), shape[-2], shape[-1])
        upd = _adam(given[k].reshape(as_layers), term_list, given["m_" + k].reshape(as_layers),
                    given["v_" + k].reshape(as_layers), name=f"adam_{k}")
        res[k] = [o.reshape(shape) for o in upd]
        return upd[0]

    for tag, names in groups.items():
        for k, t in zip(names, _exchange_wait(scatter_state[tag], last, name=f"scatter_{tag}_wait")):
            if k.startswith("mlp_w_"):
                layer_terms.setdefault(k[:-1], {})[int(k[-1])] = t
                if len(layer_terms[k[:-1]]) == 2:
                    last = update(k[:-1], [layer_terms[k[:-1]][0], layer_terms[k[:-1]][1]])
            else:
                last = update(k, [t])

    small_grads = {
        "hgrn_norm": lax.dynamic_slice_in_dim(s_hgrn_norm, me * shard, shard, axis=1),
        "hgrn_g_norm": s_g_norm, "hgrn_lb_logits": g_lb_logits, "mla_norm": s_mla_norm, "mla_q_norm": s_q_norm,
        "kv_in_norm": s_kv_in_norm, "kv_norm": s_kv_norm,
        "mlp_norm": jnp.concatenate([s_mlp_norm0, s_mlp_norm1], axis=0), "final_norm": s_final_norm,
    }
    small_names = list(small_grads)

    def flat(a):
        return a.reshape(1, -1)

    packed = [jnp.concatenate([flat(src[pre + k]) for k in small_names], axis=1)
              for src, pre in ((given, ""), (small_grads, ""), (given, "m_"), (given, "v_"))]
    small_out = _adam(packed[0][None], [packed[1][None]], packed[2][None], packed[3][None], name="adam_small")
    off = 0
    for k in small_names:
        size = given[k].size
        res[k] = [o[0, :, off:off + size].reshape(given[k].shape) for o in small_out]
        off += size

    outs = [loss, grad_x[None]]
    for i in range(4):
        outs += [res[k][i] for k in weight_names]
    return tuple(outs)
```

```python
import functools

import jax
import jax.numpy as jnp
from jax import lax
from jax.experimental import pallas as pl
from jax.experimental.pallas import tpu as pltpu

F32 = jnp.float32
BF16 = jnp.bfloat16

EPS = 1e-6
LANES = 128
N_DEV = 8
V7X_VMEM_LIMIT_BYTES = 56 << 20
MM_PIPELINE_BYTES = 30 << 20
MM_ROW_TILE = 512
ADAM_ROW_TILE = 256
GRAD_WIRE_DTYPE = BF16

HGRN_HEADS = 8
HGRN_CHUNK = 64
HGRN_SUB = 16
HGRN_HEADS_PER_STEP = 8
HGRN_CHUNKS_PER_STEP = 4
EXP_CLAMP = 80.0
MLA_HEADS = 16
MLA_NOPE = 128
MLA_ROPE = 64
ROPE_THETA = 10000.0
ATTN_SCALE = (MLA_NOPE + MLA_ROPE) ** -0.5

ADAM_LR = 0.001
ADAM_B1 = 0.9
ADAM_B2 = 0.999
ADAM_EPS = 1e-08
ADAM_WD = 0.01
ADAM_STEP = 10

_NN = ((1,), (0,))
_NT = ((1,), (1,))
_TN = ((0,), (0,))


def _params(*sem):
    return pltpu.CompilerParams(dimension_semantics=sem, vmem_limit_bytes=V7X_VMEM_LIMIT_BYTES)


def _dot(a, b, dims):
    return lax.dot_general(a.astype(BF16), b.astype(BF16), (dims, ((), ())), preferred_element_type=F32)


def _dot_f32(a, b, dims=_NN):
    return lax.dot_general(a, b, (dims, ((), ())), precision=lax.Precision.HIGH, preferred_element_type=F32)


def _sigmoid(x):
    return 1.0 / (1.0 + jnp.exp(-x))


def _rms(x, w):
    r = lax.rsqrt(jnp.mean(x * x, axis=-1, keepdims=True) + EPS)
    return x * r * w


def _rms_bwd(x, w, dy):
    r = lax.rsqrt(jnp.mean(x * x, axis=-1, keepdims=True) + EPS)
    xh = x * r
    dw = jnp.sum(dy * xh, axis=0, keepdims=True)
    dxh = dy * w
    dx = r * (dxh - xh * jnp.mean(dxh * xh, axis=-1, keepdims=True))
    return dx, dw


def _mm_tiles(m, n, k, a_bytes, b_bytes, out_tile_bytes):
    tm = min(m, MM_ROW_TILE)
    for tn in (n, 2048, 1024, 512, 256, LANES):
        if tn <= n and n % tn == 0:
            if 2 * (tm * k * a_bytes + k * tn * b_bytes + tm * tn * out_tile_bytes) <= MM_PIPELINE_BYTES:
                return tm, tn
    return tm, min(n, LANES)


def _mm(a, b, *, mode, name, out_dtype=None, after=None, col_shards=None):
    if mode == "nn":
        (m, k), (k2, n) = a.shape, b.shape
    elif mode == "nt":
        (m, k), (n, k2) = a.shape, b.shape
    else:
        (k, m), (k2, n) = a.shape, b.shape
    assert k == k2, (name, a.shape, b.shape)
    if out_dtype is None:
        out_dtype = GRAD_WIRE_DTYPE if mode == "tn" else F32
    tm, tn = _mm_tiles(m, n, k, a.dtype.itemsize, b.dtype.itemsize, jnp.dtype(out_dtype).itemsize)
    if col_shards is not None:
        tn = n // col_shards
    assert m % tm == 0 and n % tn == 0, (name, m, n)
    dims = {"nn": _NN, "nt": _NT, "tn": _TN}[mode]
    a_spec = pl.BlockSpec((k, tm), lambda i, j: (0, i)) if mode == "tn" else pl.BlockSpec((tm, k), lambda i, j: (i, 0))
    b_spec = pl.BlockSpec((tn, k), lambda i, j: (j, 0)) if mode == "nt" else pl.BlockSpec((k, tn), lambda i, j: (0, j))
    o_spec = pl.BlockSpec((tm, tn), lambda i, j: (i, j))
    operands, in_specs = [a, b], [a_spec, b_spec]
    if after is not None:
        operands.append(after)
        in_specs.append(pl.BlockSpec(memory_space=pl.ANY))
    out_shape = jax.ShapeDtypeStruct((m, n), out_dtype)
    if col_shards is not None:
        out_shape = jax.ShapeDtypeStruct((col_shards, m, tn), out_dtype)
        o_spec = pl.BlockSpec((None, tm, tn), lambda i, j: (j, i, 0))

    def body(*refs):
        refs[-1][...] = _dot(refs[0][...], refs[1][...], dims).astype(out_dtype)

    return pl.pallas_call(
        body, name=name, grid=(m // tm, n // tn), in_specs=in_specs, out_specs=o_spec, out_shape=out_shape,
        compiler_params=_params("parallel", "parallel"),
    )(*operands)


def _rowcall(fn, rows, consts, outs, accs, *, name, tr=256, after=None):
    s = rows[0].shape[0]
    tr = min(tr, s)
    assert s % tr == 0
    n_out = len(outs)
    accs = [(1, a) if isinstance(a, int) else a for a in accs]
    in_specs = [pl.BlockSpec((tr, r.shape[1]), lambda i: (i, 0)) for r in rows]
    in_specs += [pl.BlockSpec(c.shape, lambda i, nd=c.ndim: (0,) * nd) for c in consts]
    out_shape = [jax.ShapeDtypeStruct((s, w), dt) for w, dt in outs] + [jax.ShapeDtypeStruct(a, F32) for a in accs]
    out_specs = [pl.BlockSpec((tr, w), lambda i: (i, 0)) for w, _ in outs] + [pl.BlockSpec(a, lambda i: (0, 0)) for a in accs]
    n_in = len(rows) + len(consts)

    def body(*refs):
        res = fn(*[r[...] for r in refs[:n_in]])
        out_refs = refs[n_in + (after is not None):]
        for ref, val in zip(out_refs[:n_out], res[:n_out]):
            ref[...] = val.astype(ref.dtype)
        i = pl.program_id(0)
        for ref, val in zip(out_refs[n_out:], res[n_out:]):
            @pl.when(i == 0)
            def _(ref=ref, val=val):
                ref[...] = val

            @pl.when(i > 0)
            def _(ref=ref, val=val):
                ref[...] += val

    behind = [] if after is None else [after]
    return pl.pallas_call(
        body, name=name, grid=(s // tr,), in_specs=in_specs + [pl.BlockSpec(memory_space=pl.ANY)] * len(behind),
        out_specs=out_specs, out_shape=out_shape, compiler_params=_params("arbitrary" if accs else "parallel"),
    )(*rows, *consts, *behind)


def _rope_tables(seq):
    half = MLA_ROPE // 2
    inv_freq = ROPE_THETA ** (-jnp.arange(half, dtype=F32) / half)
    ang = jnp.arange(seq, dtype=F32)[:, None] * inv_freq[None, :]
    cos, sin, zero = jnp.cos(ang), jnp.sin(ang), jnp.zeros((seq, half), F32)
    t_c = jnp.concatenate([cos, cos, zero, zero], axis=1)
    t_s1 = jnp.concatenate([-sin, zero, zero, zero], axis=1)
    t_s2 = jnp.concatenate([zero, sin, zero, zero], axis=1)
    return t_c, t_s1, t_s2


def _rope(slab, t_c, t_s1, t_s2):
    return slab * t_c + pltpu.roll(slab, 96, 1) * t_s1 + pltpu.roll(slab, 32, 1) * t_s2


def _rope_t(d, t_c, t_s1, t_s2):
    return d * t_c + pltpu.roll(d * t_s1, 32, 1) + pltpu.roll(d * t_s2, 96, 1)


def _lower_bound(logits):
    l0, l1 = logits[0:1, :], logits[1:2, :]
    mx = jnp.maximum(l0, l1)
    e0, e1 = jnp.exp(l0 - mx), jnp.exp(l1 - mx)
    return e0 / (e0 + e1)


def _tri(n, lower):
    row = lax.broadcasted_iota(jnp.int32, (n, n), 0)
    col = lax.broadcasted_iota(jnp.int32, (n, n), 1)
    return (row >= col) if lower else (row <= col)


def _hgrn_fwd(zq, zf, zi, lb_logits, *, name):
    s, d = zq.shape
    h_n, c, hp, cps = d // LANES, HGRN_CHUNK, HGRN_HEADS_PER_STEP, HGRN_CHUNKS_PER_STEP
    nc = s // c

    def body(zq_ref, zf_ref, zi_ref, lb_ref, o_ref, st_ref, state_sc, b_sc):
        @pl.when(pl.program_id(1) == 0)
        def _():
            state_sc[...] = jnp.zeros_like(state_sc)

        lower = _tri(c, True)
        lower_f = lower.astype(F32)
        hs, pairs = range(hp), [(cc, hh) for cc in range(cps) for hh in range(hp)]
        sls = [slice(hh * LANES, (hh + 1) * LANES) for hh in hs]
        rws = [slice(cc * c, (cc + 1) * c) for cc in range(cps)]
        lb = [_lower_bound(lb_ref[:, sl]) for sl in sls]
        zq_v = {p: zq_ref[rws[p[0]], sls[p[1]]] for p in pairs}
        q = {p: zq_v[p] * _sigmoid(zq_v[p]) for p in pairs}
        f = {p: lb[p[1]] + (1.0 - lb[p[1]]) * _sigmoid(zf_ref[rws[p[0]], sls[p[1]]]) for p in pairs}
        k = {p: 1.0 - f[p] for p in pairs}
        v = {p: zi_ref[rws[p[0]], sls[p[1]]] for p in pairs}
        b = {p: _dot_f32(lower_f, jnp.log(f[p])) for p in pairs}
        for p in pairs:
            b_sc[p[0], p[1]] = b[p]
        qe = {p: q[p] * jnp.exp(b[p]) for p in pairs}
        scores = {p: [] for p in pairs}
        for i in range(c // HGRN_SUB):
            lo = i * HGRN_SUB
            for p in pairs:
                ref = b_sc[p[0], p[1], lo - 1:lo, :] if i > 0 else jnp.zeros((1, LANES), F32)
                qt = q[p][lo:lo + HGRN_SUB, :] * jnp.exp(b[p][lo:lo + HGRN_SUB, :] - ref)
                dec = jnp.exp(jnp.minimum(ref - b[p], EXP_CLAMP))
                scores[p].append(_dot(qt, k[p] * dec, _NT))
        o_intra = {p: _dot(jnp.where(lower, jnp.concatenate(scores[p], axis=0), 0.0), v[p], _NN) for p in pairs}
        bl = {p: b_sc[p[0], p[1], c - 1:c, :] for p in pairs}
        k_end = {p: k[p] * jnp.exp(bl[p] - b[p]) for p in pairs}
        state = [state_sc[hh] for hh in hs]
        for cc in range(cps):
            for hh in hs:
                st_ref[hh, cc] = state[hh]
                o_ref[rws[cc], sls[hh]] = _dot(qe[cc, hh], state[hh], _NT) + o_intra[cc, hh]
            state = [state[hh] * jnp.exp(bl[cc, hh]) + _dot(v[cc, hh], k_end[cc, hh], _TN) for hh in hs]
        for hh in hs:
            state_sc[hh] = state[hh]

    tile = pl.BlockSpec((cps * c, hp * LANES), lambda h, i: (i, h))
    return pl.pallas_call(
        body, name=name, grid=(h_n // hp, nc // cps),
        in_specs=[tile, tile, tile, pl.BlockSpec((2, hp * LANES), lambda h, i: (0, h))],
        out_specs=[tile, pl.BlockSpec((hp, cps, LANES, LANES), lambda h, i: (h, i, 0, 0))],
        out_shape=[jax.ShapeDtypeStruct((s, d), F32), jax.ShapeDtypeStruct((h_n, nc, LANES, LANES), F32)],
        scratch_shapes=[pltpu.VMEM((hp, LANES, LANES), F32), pltpu.VMEM((cps, hp, c, LANES), F32)],
        compiler_params=_params("parallel", "arbitrary"),
    )(zq, zf, zi, lb_logits)


def _hgrn_bwd(zq, zf, zi, lb_logits, states, do, after, *, name):
    s, d = zq.shape
    h_n, c, hp, cps = d // LANES, HGRN_CHUNK, HGRN_HEADS_PER_STEP, HGRN_CHUNKS_PER_STEP
    nc = s // c
    n_steps = nc // cps

    def body(zq_ref, zf_ref, zi_ref, lb_ref, st_ref, do_ref, _, dzq_ref, dzf_ref, dzi_ref, dlb_ref, dstate_sc, b_sc):
        @pl.when(pl.program_id(1) == 0)
        def _():
            dstate_sc[...] = jnp.zeros_like(dstate_sc)
            dlb_ref[...] = jnp.zeros_like(dlb_ref)

        lower, upper = _tri(c, True), _tri(c, False).astype(F32)
        lower_f = lower.astype(F32)
        last_row = lax.broadcasted_iota(jnp.int32, (c, LANES), 0) == c - 1
        hs, pairs = range(hp), [(cc, hh) for cc in range(cps) for hh in range(hp)]
        sls = [slice(hh * LANES, (hh + 1) * LANES) for hh in hs]
        rws = [slice(cc * c, (cc + 1) * c) for cc in range(cps)]
        lb = [_lower_bound(lb_ref[:, sl]) for sl in sls]
        zq_v = {p: zq_ref[rws[p[0]], sls[p[1]]] for p in pairs}
        sq = {p: _sigmoid(zq_v[p]) for p in pairs}
        q = {p: zq_v[p] * sq[p] for p in pairs}
        sf = {p: _sigmoid(zf_ref[rws[p[0]], sls[p[1]]]) for p in pairs}
        f = {p: lb[p[1]] + (1.0 - lb[p[1]]) * sf[p] for p in pairs}
        k = {p: 1.0 - f[p] for p in pairs}
        v = {p: zi_ref[rws[p[0]], sls[p[1]]] for p in pairs}
        d_o = {p: do_ref[rws[p[0]], sls[p[1]]] for p in pairs}
        b = {p: _dot_f32(lower_f, jnp.log(f[p])) for p in pairs}
        s0t = {p: st_ref[p[1], p[0]] for p in pairs}
        for p in pairs:
            b_sc[p[0], p[1]] = b[p]
        bl = {p: b_sc[p[0], p[1], c - 1:c, :] for p in pairs}
        eb = {p: jnp.exp(b[p]) for p in pairs}
        ebl = {p: jnp.exp(bl[p]) for p in pairs}
        dec_end = {p: jnp.exp(bl[p] - b[p]) for p in pairs}
        da = {p: jnp.where(lower, _dot(d_o[p], v[p], _NT), 0.0) for p in pairs}
        dq = {p: _dot(d_o[p], s0t[p], _NN) * eb[p] for p in pairs}
        dstate_in = {p: _dot(d_o[p], q[p] * eb[p], _TN) for p in pairs}
        dk_intra = {p: jnp.zeros((c, LANES), F32) for p in pairs}
        scores, dq_blocks = {p: [] for p in pairs}, {p: [] for p in pairs}
        for i in range(c // HGRN_SUB):
            lo = i * HGRN_SUB
            for p in pairs:
                ref = b_sc[p[0], p[1], lo - 1:lo, :] if i > 0 else jnp.zeros((1, LANES), F32)
                grow = jnp.exp(b[p][lo:lo + HGRN_SUB, :] - ref)
                qt = q[p][lo:lo + HGRN_SUB, :] * grow
                dec = jnp.exp(jnp.minimum(ref - b[p], EXP_CLAMP))
                kd = k[p] * dec
                scores[p].append(_dot(qt, kd, _NT))
                da_i = da[p][lo:lo + HGRN_SUB, :]
                dq_blocks[p].append(_dot_f32(da_i, kd, _NN) * grow)
                dk_intra[p] = dk_intra[p] + _dot_f32(da_i, qt, _TN) * dec
        dv_intra = {p: _dot(jnp.where(lower, jnp.concatenate(scores[p], axis=0), 0.0), d_o[p], _TN) for p in pairs}
        dq = {p: dq[p] + jnp.concatenate(dq_blocks[p], axis=0) for p in pairs}
        q_dq = {p: q[p] * dq[p] for p in pairs}
        for p in pairs:
            dzq_ref[rws[p[0]], sls[p[1]]] = (dq[p] * sq[p] * (1.0 + zq_v[p] * (1.0 - sq[p]))).astype(BF16)
        dstate = [dstate_sc[hh] for hh in hs]
        for cc in reversed(range(cps)):
            ps = [(cc, hh) for hh in hs]
            dk_state = [_dot(v[p], dstate[p[1]], _NN) * dec_end[p] for p in ps]
            dv = [dv_intra[p] + _dot(k[p] * dec_end[p], dstate[p[1]], _NT) for p in ps]
            dk = [dk_intra[p] + dk_state[p[1]] for p in ps]
            db_last = [jnp.sum(k[p] * dk_state[p[1]], axis=0, keepdims=True)
                       + ebl[p] * jnp.sum(s0t[p] * dstate[p[1]], axis=0, keepdims=True) for p in ps]
            db = [q_dq[p] - k[p] * dk[p[1]] + jnp.where(last_row, db_last[p[1]], 0.0) for p in ps]
            df = [_dot_f32(upper, db[p[1]]) / f[p] - dk[p[1]] for p in ps]
            for p in ps:
                hh = p[1]
                dzf_ref[rws[cc], sls[hh]] = (df[hh] * (1.0 - lb[hh]) * sf[p] * (1.0 - sf[p])).astype(BF16)
                dlb_ref[:, sls[hh]] += jnp.sum(df[hh] * (1.0 - sf[p]), axis=0, keepdims=True)
                dzi_ref[rws[cc], sls[hh]] = dv[hh].astype(BF16)
            dstate = [dstate[p[1]] * ebl[p] + dstate_in[p] for p in ps]
        for hh in hs:
            dstate_sc[hh] = dstate[hh]

    tile = pl.BlockSpec((cps * c, hp * LANES), lambda h, i: (n_steps - 1 - i, h))
    out = jax.ShapeDtypeStruct((s, d), BF16)
    return pl.pallas_call(
        body, name=name, grid=(h_n // hp, n_steps),
        in_specs=[tile, tile, tile, pl.BlockSpec((2, hp * LANES), lambda h, i: (0, h)),
                  pl.BlockSpec((hp, cps, LANES, LANES), lambda h, i: (h, n_steps - 1 - i, 0, 0)), tile,
                  pl.BlockSpec(memory_space=pl.ANY)],
        out_specs=[tile, tile, tile, pl.BlockSpec((1, hp * LANES), lambda h, i: (0, h))],
        out_shape=[out, out, out, jax.ShapeDtypeStruct((1, d), F32)],
        scratch_shapes=[pltpu.VMEM((hp, LANES, LANES), F32), pltpu.VMEM((cps, hp, c, LANES), F32)],
        compiler_params=_params("parallel", "arbitrary"),
    )(zq, zf, zi, lb_logits, states, do, after)


ATTN_SUB_ROWS = 256
LOG2E = 1.4426950408889634
LN2 = 0.6931471805599453
Q_PRESCALE = ATTN_SCALE * LOG2E


def _attn_tile(s):
    return min(1024, max(128, s // 2))


def _causal_pairs(n, q_major):
    pairs = [(i, j) for i in range(n) for j in range(i + 1)] if q_major else [(i, j) for j in range(n) for i in range(j, n)]
    return jnp.asarray([p[0] for p in pairs], jnp.int32), jnp.asarray([p[1] for p in pairs], jnp.int32)


def _sub_scores(qn_ref, qr_ref, k, r, sub, t, diagonal):
    q = jnp.concatenate([qn_ref[r:r + sub, :], qr_ref[r:r + sub, :]], axis=1)
    if not diagonal:
        return q, _dot(q, k, _NT)
    cols = r + sub
    keep = lax.broadcasted_iota(jnp.int32, (sub, cols), 1) <= r + lax.broadcasted_iota(jnp.int32, (sub, cols), 0)
    return q, jnp.where(keep, _dot(q, k[:cols], _NT), -jnp.inf)


def _attn_fwd(qn, qr, kn, kr, v, *, name):
    s, t = qn.shape[0], _attn_tile(qn.shape[0])
    sub = min(t, ATTN_SUB_ROWS)
    q_blk, k_blk = _causal_pairs(s // t, True)

    def body(qi_ref, kj_ref, qn_ref, qr_ref, kn_ref, kr_ref, v_ref, o_ref, lse_ref, m_sc, l_sc, acc_sc):
        p_id = pl.program_id(1)
        i, j = qi_ref[p_id], kj_ref[p_id]

        @pl.when(j == 0)
        def _():
            m_sc[...] = jnp.full_like(m_sc, -jnp.inf)
            l_sc[...] = jnp.zeros_like(l_sc)
            acc_sc[...] = jnp.zeros_like(acc_sc)

        def update(diagonal):
            k = jnp.concatenate([kn_ref[...], kr_ref[...]], axis=1)
            v = v_ref[...]
            starts = list(range(0, t, sub))
            scs = [_sub_scores(qn_ref, qr_ref, k, r, sub, t, diagonal)[1] for r in starts]
            ps, alphas = [], []
            for r, sc in zip(starts, scs):
                m_prev = m_sc[r:r + sub, :]
                m_new = jnp.maximum(m_prev, jnp.max(sc, axis=1, keepdims=True))
                alpha = jnp.exp2(m_prev - m_new)
                p = jnp.exp2(sc - m_new[:, :1])
                l_sc[r:r + sub, :] = alpha * l_sc[r:r + sub, :] + jnp.sum(p, axis=1, keepdims=True)
                m_sc[r:r + sub, :] = m_new
                ps.append(p)
                alphas.append(alpha)
            for r, p, alpha in zip(starts, ps, alphas):
                acc_sc[r:r + sub, :] = alpha * acc_sc[r:r + sub, :] + _dot(p, v[:p.shape[1]], _NN)

        @pl.when(j < i)
        def _():
            update(False)

        @pl.when(j == i)
        def _():
            update(True)
            o_ref[...] = (acc_sc[...] / l_sc[...]).astype(BF16)
            lse_ref[...] = m_sc[...] + jnp.log(l_sc[...]) * LOG2E

    q_spec = pl.BlockSpec((t, LANES), lambda h, p, qi, kj: (qi[p], h))
    k_spec = pl.BlockSpec((t, LANES), lambda h, p, qi, kj: (kj[p], h))
    kr_spec = pl.BlockSpec((t, LANES), lambda h, p, qi, kj: (kj[p], 0))
    stat = pltpu.VMEM((t, LANES), F32)
    return pl.pallas_call(
        body, name=name,
        grid_spec=pltpu.PrefetchScalarGridSpec(
            num_scalar_prefetch=2, grid=(MLA_HEADS, q_blk.shape[0]),
            in_specs=[q_spec, q_spec, k_spec, kr_spec, k_spec], out_specs=[q_spec, q_spec],
            scratch_shapes=[stat, stat, stat]),
        out_shape=[jax.ShapeDtypeStruct(qn.shape, BF16), jax.ShapeDtypeStruct(qn.shape, F32)],
        compiler_params=_params("parallel", "arbitrary"),
    )(q_blk, k_blk, qn, qr, kn, kr, v)


def _attn_bwd(qn, qr, kn, kr, v, do, lse, delta, *, name):
    s, t = qn.shape[0], _attn_tile(qn.shape[0])
    n, sub = s // t, min(t, ATTN_SUB_ROWS)
    q_blk, k_blk = _causal_pairs(n, False)

    def body(qi_ref, kj_ref, qn_ref, qr_ref, kn_ref, kr_ref, v_ref, do_ref, lse_ref, delta_ref,
             dqn_ref, dqr_ref, dkn_ref, dv_ref, dkr_ref, dk_sc, dv_sc):
        p_id = pl.program_id(1)
        i, j = qi_ref[p_id], kj_ref[p_id]

        @pl.when(p_id == 0)
        def _():
            dqn_ref[...] = jnp.zeros_like(dqn_ref)
            dqr_ref[...] = jnp.zeros_like(dqr_ref)

        @pl.when(i == j)
        def _():
            dk_sc[...] = jnp.zeros_like(dk_sc)
            dv_sc[...] = jnp.zeros_like(dv_sc)

        def accumulate(diagonal):
            k = jnp.concatenate([kn_ref[...], kr_ref[...]], axis=1)
            v = v_ref[...]
            starts = list(range(0, t, sub))
            qs, d_os, scs, dps = [], [], [], []
            for r in starts:
                q, sc = _sub_scores(qn_ref, qr_ref, k, r, sub, t, diagonal)
                d_o = do_ref[r:r + sub, :]
                qs.append(q)
                d_os.append(d_o)
                scs.append(sc)
                dps.append(_dot(d_o, v[:sc.shape[1]], _NT))
            ps, dss = [], []
            for r, sc, dp in zip(starts, scs, dps):
                p = jnp.exp2(sc - lse_ref[r:r + sub, :][:, :1])
                ps.append(p.astype(BF16))
                dss.append((p * (dp - delta_ref[r:r + sub, :][:, :1])).astype(BF16))
            for r, q, d_o, p, ds in zip(starts, qs, d_os, ps, dss):
                cols = p.shape[1]
                dv_sc[:cols, :] += _dot(p, d_o, _TN)
                dk_sc[:cols, :] += _dot(ds, q, _TN)
                dq = _dot(ds, k[:cols], _NN) * ATTN_SCALE
                rows = pl.ds(pl.multiple_of(i * t + r, sub), sub)
                dqn_ref[rows, :] += dq[:, :LANES]
                dqr_ref[rows, :] += dq[:, LANES:]

        @pl.when(j < i)
        def _():
            accumulate(False)

        @pl.when(j == i)
        def _():
            accumulate(True)

        @pl.when(i == n - 1)
        def _():
            dkn_ref[...] = (dk_sc[:, :LANES] * LN2).astype(BF16)
            dkr_ref[...] = dk_sc[:, LANES:] * LN2
            dv_ref[...] = dv_sc[...].astype(BF16)

    q_spec = pl.BlockSpec((t, LANES), lambda h, p, qi, kj: (qi[p], h))
    k_spec = pl.BlockSpec((t, LANES), lambda h, p, qi, kj: (kj[p], h))
    kr_spec = pl.BlockSpec((t, LANES), lambda h, p, qi, kj: (kj[p], 0))
    head_spec = pl.BlockSpec((s, LANES), lambda h, p, qi, kj: (0, h))
    f32_out, bf16_out = jax.ShapeDtypeStruct(qn.shape, F32), jax.ShapeDtypeStruct(qn.shape, BF16)
    return pl.pallas_call(
        body, name=name,
        grid_spec=pltpu.PrefetchScalarGridSpec(
            num_scalar_prefetch=2, grid=(MLA_HEADS, q_blk.shape[0]),
            in_specs=[q_spec, q_spec, k_spec, kr_spec, k_spec, q_spec, q_spec, q_spec],
            out_specs=[head_spec, head_spec, k_spec, k_spec, k_spec],
            scratch_shapes=[pltpu.VMEM((t, 2 * LANES), F32), pltpu.VMEM((t, LANES), F32)]),
        out_shape=[f32_out, f32_out, bf16_out, bf16_out, f32_out],
        compiler_params=_params("parallel", "arbitrary"),
    )(q_blk, k_blk, qn, qr, kn, kr, v, do, lse, delta)


def _exchange(arrs, *, scatter, name):
    n = len(arrs)
    out_shape = [jax.ShapeDtypeStruct(a.shape if scatter else (N_DEV, *a.shape), a.dtype) for a in arrs]

    def body(*refs):
        ins, outs = refs[:n], refs[n:2 * n]
        send_sems, recv_sems, local_sems = refs[2 * n:]
        x, y, c = lax.axis_index("x"), lax.axis_index("y"), lax.axis_index("c")
        me = 4 * x + 2 * y + c
        copies = []
        for k in range(n):
            local = pltpu.make_async_copy(ins[k].at[me] if scatter else ins[k], outs[k].at[me], local_sems.at[k])
            local.start()
            copies.append(local)
            for d in range(1, N_DEV):
                px, py, pc = (x + (d >> 2)) % 2, (y + ((d >> 1) & 1)) % 2, (c + (d & 1)) % 2
                peer = 4 * px + 2 * py + pc
                remote = pltpu.make_async_remote_copy(
                    src_ref=ins[k].at[peer] if scatter else ins[k], dst_ref=outs[k].at[me],
                    send_sem=send_sems.at[k, d - 1], recv_sem=recv_sems.at[k, d - 1],
                    device_id=(px, py, pc), device_id_type=pl.DeviceIdType.MESH)
                remote.start()
                copies.append(remote)
        for cp in copies:
            cp.wait()

    any_spec = pl.BlockSpec(memory_space=pl.ANY)
    return pl.pallas_call(
        body, name=name, in_specs=[any_spec] * n, out_specs=[any_spec] * n, out_shape=out_shape,
        scratch_shapes=[pltpu.SemaphoreType.DMA((n, N_DEV - 1)), pltpu.SemaphoreType.DMA((n, N_DEV - 1)),
                        pltpu.SemaphoreType.DMA((n,))],
    )(*arrs)


def _peers(x, y, c):
    out = []
    for d in range(1, N_DEV):
        px, py, pc = (x + (d >> 2)) % 2, (y + ((d >> 1) & 1)) % 2, (c + (d & 1)) % 2
        out.append(((px, py, pc), 4 * px + 2 * py + pc))
    return out


CHIP_LEVEL_PEERS = (1, 2, 4, 6)


def _exchange_copies(ins, lands, send_sems, recv_sems, scatter, chip_level=False):
    x, y, c = lax.axis_index("x"), lax.axis_index("y"), lax.axis_index("c")
    me = 4 * x + 2 * y + c
    local, remote = [], []
    for k in range(len(ins)):
        local.append(pltpu.make_async_copy(ins[k].at[me] if scatter else ins[k], lands[k].at[me],
                                           recv_sems.at[k * N_DEV + N_DEV - 1]))
        for d, (coords, peer) in enumerate(_peers(x, y, c)):
            if chip_level and d + 1 not in CHIP_LEVEL_PEERS:
                continue
            remote.append(pltpu.make_async_remote_copy(
                src_ref=ins[k].at[peer] if scatter else ins[k], dst_ref=lands[k].at[me],
                send_sem=send_sems.at[k * N_DEV + d], recv_sem=recv_sems.at[k * N_DEV + d],
                device_id=coords, device_id_type=pl.DeviceIdType.MESH))
    return local, remote


def _exchange_start(arrs, *, scatter, name, after=None, chip_level=False):
    n = len(arrs)
    hbm = pl.BlockSpec(memory_space=pltpu.HBM)
    sem = pl.BlockSpec(memory_space=pltpu.SEMAPHORE)
    lands = [lax.empty(a.shape if scatter else (N_DEV, *a.shape), a.dtype) for a in arrs]

    def body(*refs):
        ins, land_refs = refs[:n], refs[n:2 * n]
        first_out = 2 * n + (after is not None)
        send_sems, recv_sems, token = refs[first_out], refs[first_out + 1], refs[-1]
        local, remote = _exchange_copies(ins, land_refs, send_sems, recv_sems, scatter, chip_level)
        for cp in local + remote:
            cp.start()
        token[...] = jnp.zeros_like(token)

    operands = [pltpu.with_memory_space_constraint(a, pltpu.HBM) for a in list(arrs) + lands]
    behind = [] if after is None else [after]
    res = pl.pallas_call(
        body, name=name,
        out_shape=(pltpu.SemaphoreType.DMA((n * N_DEV,)), pltpu.SemaphoreType.DMA((n * N_DEV,)),
                   *[pltpu.HBM(o.shape, o.dtype) for o in operands], jax.ShapeDtypeStruct((8, LANES), F32)),
        in_specs=[hbm] * (2 * n) + [pl.BlockSpec(memory_space=pl.ANY)] * len(behind),
        out_specs=(sem, sem, *[hbm] * (2 * n), pl.BlockSpec(memory_space=pltpu.VMEM)),
        input_output_aliases={i: 2 + i for i in range(2 * n)},
        compiler_params=pltpu.CompilerParams(has_side_effects=pltpu.SideEffectType.DATAFLOW_SIDE_EFFECTING),
    )(*operands, *behind)
    return (res[0], res[1], list(res[2:2 + n]), list(res[2 + n:2 + 2 * n]), scatter, chip_level), res[-1]


def _exchange_wait(state, after, *, name):
    send_sems, recv_sems, ins, lands, scatter, chip_level = state
    n = len(ins)
    hbm = pl.BlockSpec(memory_space=pltpu.HBM)
    sem = pl.BlockSpec(memory_space=pltpu.SEMAPHORE)

    def body(*refs):
        in_refs, land_refs = refs[:n], refs[n:2 * n]
        local, remote = _exchange_copies(in_refs, land_refs, refs[2 * n], refs[2 * n + 1], scatter, chip_level)
        for cp in local:
            cp.wait()
        for cp in remote:
            cp.wait_send()
            cp.wait_recv()

    res = pl.pallas_call(
        body, name=name, out_shape=tuple(pltpu.HBM(o.shape, o.dtype) for o in ins + lands),
        in_specs=[hbm] * (2 * n) + [sem, sem, pl.BlockSpec(memory_space=pl.ANY)], out_specs=tuple([hbm] * (2 * n)),
        input_output_aliases={i: i for i in range(2 * n)},
        compiler_params=pltpu.CompilerParams(has_side_effects=pltpu.SideEffectType.DATAFLOW_SIDE_EFFECTING),
    )(*ins, *lands, send_sems, recv_sems, after)
    return list(res[n:])


def _chip_forward(lands, *, name):
    n = len(lands)

    def body(*refs):
        ins, outs, send_sems, recv_sems = refs[:n], refs[n:2 * n], refs[2 * n], refs[2 * n + 1]
        x, y, c = lax.axis_index("x"), lax.axis_index("y"), lax.axis_index("c")
        copies = []
        for k in range(n):
            for j, (dx, dy) in enumerate(((0, 1), (1, 0), (1, 1))):
                held = 4 * ((x + dx) % 2) + 2 * ((y + dy) % 2) + c
                cp = pltpu.make_async_remote_copy(
                    src_ref=ins[k].at[held], dst_ref=outs[k].at[held], send_sem=send_sems.at[k, j],
                    recv_sem=recv_sems.at[k, j], device_id=(x, y, 1 - c), device_id_type=pl.DeviceIdType.MESH)
                cp.start()
                copies.append(cp)
        for cp in copies:
            cp.wait()

    any_spec = pl.BlockSpec(memory_space=pl.ANY)
    return pl.pallas_call(
        body, name=name, in_specs=[any_spec] * n, out_specs=[any_spec] * n,
        out_shape=[jax.ShapeDtypeStruct(a.shape, a.dtype) for a in lands], input_output_aliases={k: k for k in range(n)},
        scratch_shapes=[pltpu.SemaphoreType.DMA((n, 3)), pltpu.SemaphoreType.DMA((n, 3))],
    )(*lands)


def _adam(w, terms, m, v, *, name):
    n_layers, r, c = w.shape
    tr = min(r, ADAM_ROW_TILE)
    assert r % tr == 0 and len(terms) == n_layers
    steps = r // tr

    def body(w_ref, *rest):
        t_refs, (m_ref, v_ref, g_out, d_out, m_out, v_out) = rest[:n_layers], rest[n_layers:]
        for layer, t_ref in enumerate(t_refs):
            @pl.when(pl.program_id(0) == layer)
            def _(t_ref=t_ref):
                g = t_ref[0].astype(F32)
                for s in range(1, t_ref.shape[0]):
                    g = g + t_ref[s].astype(F32)
                m1 = ADAM_B1 * m_ref[...] + (1.0 - ADAM_B1) * g
                v1 = ADAM_B2 * v_ref[...] + (1.0 - ADAM_B2) * jnp.square(g)
                m_hat = m1 / (1.0 - ADAM_B1 ** ADAM_STEP)
                v_hat = v1 / (1.0 - ADAM_B2 ** ADAM_STEP)
                g_out[...] = g
                d_out[...] = -ADAM_LR * (m_hat / (jnp.sqrt(v_hat) + ADAM_EPS) + ADAM_WD * w_ref[...])
                m_out[...] = m1
                v_out[...] = v1

    def term_spec(layer, t):
        return pl.BlockSpec((t.shape[0], tr, c),
                            lambda l, i: (0, jnp.where(l == layer, i, jnp.where(l < layer, 0, steps - 1)), 0))

    spec = pl.BlockSpec((None, tr, c), lambda l, i: (l, i, 0))
    out = jax.ShapeDtypeStruct(w.shape, F32)
    return pl.pallas_call(
        body, name=name, grid=(n_layers, steps),
        in_specs=[spec] + [term_spec(layer, t) for layer, t in enumerate(terms)] + [spec, spec], out_specs=[spec] * 4,
        out_shape=[out] * 4, compiler_params=_params("arbitrary", "arbitrary"),
    )(w, *terms, m, v)


def _sum_terms(terms, *, name):
    n, _, p = terms.shape

    def body(t_ref, o_ref):
        acc = t_ref[0]
        for s in range(1, n):
            acc = acc + t_ref[s]
        o_ref[...] = acc

    return pl.pallas_call(body, name=name, out_shape=jax.ShapeDtypeStruct((1, p), F32))(terms)


def _lb_logits_grad(dlb, logits, *, name):
    def body(dlb_ref, l_ref, o_ref):
        lb = _lower_bound(l_ref[...])
        d0 = dlb_ref[...] * lb * (1.0 - lb)
        o_ref[...] = jnp.concatenate([d0, -d0], axis=0)

    return pl.pallas_call(body, name=name, out_shape=jax.ShapeDtypeStruct(logits.shape, F32))(dlb, logits)


def _silu_grad(z):
    sg = _sigmoid(z)
    return sg * (1.0 + z * (1.0 - sg))


def _head_norm_gate(o, zg, gn):
    outs = []
    for h in range(HGRN_HEADS):
        sl = slice(h * LANES, (h + 1) * LANES)
        zg_h = zg[:, sl]
        outs.append(_rms(o[:, sl], gn) * (zg_h * _sigmoid(zg_h)))
    return (jnp.concatenate(outs, axis=1),)


def _head_norm_gate_bwd(o, zg, dm, gn):
    do_parts, dzg_parts, dgn = [], [], jnp.zeros((1, LANES), F32)
    for h in range(HGRN_HEADS):
        sl = slice(h * LANES, (h + 1) * LANES)
        o_h, zg_h, dm_h = o[:, sl], zg[:, sl], dm[:, sl]
        gate = zg_h * _sigmoid(zg_h)
        do_h, dgn_h = _rms_bwd(o_h, gn, dm_h * gate)
        dgn = dgn + dgn_h
        do_parts.append(do_h)
        dzg_parts.append(dm_h * _rms(o_h, gn) * _silu_grad(zg_h))
    return jnp.concatenate(do_parts, axis=1), jnp.concatenate(dzg_parts, axis=1), dgn


def _rope_slabs(x, t_c, t_s1, t_s2, transpose):
    fn = _rope_t if transpose else _rope
    return jnp.concatenate(
        [fn(x[:, h * LANES:(h + 1) * LANES], t_c, t_s1, t_s2) for h in range(x.shape[1] // LANES)], axis=1)


def _loss_head(h, tgt, w):
    d = h.shape[1]
    r = lax.rsqrt(jnp.mean(h * h, axis=-1, keepdims=True) + EPS)
    xh = h * r
    err = xh * w - tgt
    loss = 0.5 * jnp.sum(jnp.mean(err * err, axis=-1, keepdims=True), axis=0, keepdims=True)
    dy = err / d
    dxh = dy * w
    dh = r * (dxh - xh * jnp.mean(dxh * xh, axis=-1, keepdims=True))
    return dh, dh, jnp.sum(dy * xh, axis=0, keepdims=True), jnp.broadcast_to(loss, (1, LANES))


def _mlp_fwd(h, norm, w_up, w_down, tag, loss_head=None):
    d = h.shape[1]

    def up(x, g, wu):
        x_n = _rms(x, g).astype(BF16)
        return x_n, jnp.concatenate([jnp.square(jnp.maximum(_dot(x_n, wu[j], _NN), 0.0)) for j in range(wu.shape[0])],
                                    axis=1)

    xn, act = _rowcall(up, [h], [norm, w_up], [(d, BF16), (w_up.shape[0] * w_up.shape[2], BF16)], [], tr=512,
                       name=f"{tag}_up")
    if callable(w_down):
        w_down = w_down(act)
    if loss_head is None:
        return _rowcall(lambda a, res, wd: (res + _dot(a, wd, _NN),), [act, h], [w_down], [(d, F32)], [],
                        tr=512, name=f"{tag}_down")[0], (h, xn, act)
    tgt, final_norm = loss_head

    def down_and_loss(a, res, t, wd, g):
        return _loss_head(res + _dot(a, wd, _NN), t, g)

    return _rowcall(down_and_loss, [act, h, tgt], [w_down, final_norm], [(d, F32), (d, BF16)], [d, LANES],
                    tr=512, name=f"{tag}_down_loss"), (h, xn, act)


def _mlp_bwd(dh_out, dh_out_bf, saved, norm, w_up, w_down, tag, after=None):
    h, xn, act = saved
    d = h.shape[1]
    du = _rowcall(lambda dres, a, wd: (_dot(dres, wd, _NT) * (2.0 * jnp.sqrt(a.astype(F32))),), [dh_out_bf, act],
                  [w_down], [(act.shape[1], BF16)], [], tr=512, after=after, name=f"{tag}_bwd_du")[0]
    dw_down = _mm(act, dh_out_bf, mode="tn", name=f"{tag}_bwd_wdown")
    dw_up = _mm(xn, du, mode="tn", col_shards=w_up.shape[0], name=f"{tag}_bwd_wup")

    def up_norm_bwd(x, d_u, dres, g, wu):
        cols = wu.shape[2]
        dxn = _dot(d_u[:, :cols], wu[0], _NT)
        for j in range(1, wu.shape[0]):
            dxn = dxn + _dot(d_u[:, j * cols:(j + 1) * cols], wu[j], _NT)
        dx, dw = _rms_bwd(x, g, dxn)
        return dx + dres, dx + dres, dw

    dh, dh_bf, dnorm = _rowcall(up_norm_bwd, [h, du, dh_out], [norm, w_up], [(d, F32), (d, BF16)], [d], tr=512,
                                name=f"{tag}_bwd_dxn")
    return dh, dh_bf, dnorm, dw_up, dw_down


def _row_major(g):
    return g.reshape(g.shape[0] * g.shape[1], g.shape[2])


def _col_major(g):
    return jnp.transpose(g, (1, 0, 2)).reshape(g.shape[1], g.shape[0] * g.shape[2])


def _col_terms(dw):
    k, n = dw.shape
    return jnp.transpose(dw.reshape(k, N_DEV, n // N_DEV), (1, 0, 2))


def _row_terms(dw):
    return dw.reshape(N_DEV, dw.shape[0] // N_DEV, dw.shape[1])


def kernel(x, hgrn_norm, hgrn_w_q, hgrn_w_f, hgrn_w_i, hgrn_w_g, hgrn_g_norm, hgrn_w_o, hgrn_lb_logits, mla_norm, mla_w_dq, mla_q_norm, mla_w_uq, mla_w_o, kv_in_norm, kv_w_dkv, kv_norm, kv_w_uk, kv_w_uv, mlp_norm, mlp_w_up, mlp_w_down, final_norm, loss_target, m_hgrn_norm, m_hgrn_w_q, m_hgrn_w_f, m_hgrn_w_i, m_hgrn_w_g, m_hgrn_g_norm, m_hgrn_w_o, m_hgrn_lb_logits, m_mla_norm, m_mla_w_dq, m_mla_q_norm, m_mla_w_uq, m_mla_w_o, m_kv_in_norm, m_kv_w_dkv, m_kv_norm, m_kv_w_uk, m_kv_w_uv, m_mlp_norm, m_mlp_w_up, m_mlp_w_down, m_final_norm, v_hgrn_norm, v_hgrn_w_q, v_hgrn_w_f, v_hgrn_w_i, v_hgrn_w_g, v_hgrn_g_norm, v_hgrn_w_o, v_hgrn_lb_logits, v_mla_norm, v_mla_w_dq, v_mla_q_norm, v_mla_w_uq, v_mla_w_o, v_kv_in_norm, v_kv_w_dkv, v_kv_norm, v_kv_w_uk, v_kv_w_uv, v_mlp_norm, v_mlp_w_up, v_mlp_w_down, v_final_norm):
    given = dict(locals())
    weight_names = ["hgrn_norm", "hgrn_w_q", "hgrn_w_f", "hgrn_w_i", "hgrn_w_g", "hgrn_g_norm", "hgrn_w_o",
                    "hgrn_lb_logits", "mla_norm", "mla_w_dq", "mla_q_norm", "mla_w_uq", "mla_w_o", "kv_in_norm",
                    "kv_w_dkv", "kv_norm", "kv_w_uk", "kv_w_uv", "mlp_norm", "mlp_w_up", "mlp_w_down", "final_norm"]
    me = 4 * lax.axis_index("x") + 2 * lax.axis_index("y") + lax.axis_index("c")
    xs, tgt = x[0], loss_target[0]
    seq, d_model = xs.shape
    n_heads, hd = MLA_HEADS, LANES

    big_local = {
        "hgrn_w_q": hgrn_w_q[0], "hgrn_w_f": hgrn_w_f[0], "hgrn_w_i": hgrn_w_i[0], "hgrn_w_g": hgrn_w_g[0],
        "hgrn_w_o": hgrn_w_o[0], "mla_w_dq": mla_w_dq[0], "mla_w_uq": mla_w_uq[0], "mla_w_o": mla_w_o[0],
        "kv_w_dkv": kv_w_dkv, "kv_w_uk": kv_w_uk, "kv_w_uv": kv_w_uv,
        "mlp_w_up0": mlp_w_up[0], "mlp_w_up1": mlp_w_up[1], "mlp_w_down0": mlp_w_down[0], "mlp_w_down1": mlp_w_down[1],
    }
    big_names = list(big_local)
    col_sharded = {"mla_w_uq", "kv_w_uk", "kv_w_uv"}
    shard_major = {"mlp_w_up0", "mlp_w_up1"}
    vec_local = jnp.concatenate([hgrn_norm, hgrn_lb_logits], axis=0)
    first_names = ["hgrn_w_q", "hgrn_w_f", "hgrn_w_i"]
    proj_names = first_names + ["hgrn_w_g"]
    later_names = {"hgrn_o": ["hgrn_w_g", "hgrn_w_o"], "up0": ["mlp_w_up0"], "down0": ["mlp_w_down0"],
                   "mla": ["kv_w_dkv", "kv_w_uk", "kv_w_uv", "mla_w_dq", "mla_w_uq", "mla_w_o"],
                   "mlp1": ["mlp_w_up1", "mlp_w_down1"]}

    def unshard(names, arrays):
        return {k: (a if k in shard_major else _col_major(a) if k in col_sharded else _row_major(a))
                for k, a in zip(names, arrays)}

    two_level = {"down0", "mla"}
    first_state, token = _exchange_start([big_local[k].astype(BF16) for k in first_names] + [vec_local], scatter=False,
                                         chip_level=True, name="gather_first_start")
    gather_state = {}
    for tag, names in later_names.items():
        gather_state[tag], token = _exchange_start([big_local[k].astype(BF16) for k in names], scatter=False,
                                                   chip_level=tag in two_level, after=token, name=f"gather_{tag}_start")

    def gather_wait(tag, after):
        landed = _exchange_wait(gather_state[tag], after, name=f"gather_{tag}_wait")
        if tag in two_level:
            landed = _chip_forward(landed, name=f"gather_{tag}_forward")
        w.update(unshard(later_names[tag], landed))
        return [w[k] for k in later_names[tag]]

    gathered = _chip_forward(_exchange_wait(first_state, token, name="gather_first_wait"), name="gather_first_forward")
    w = unshard(first_names, gathered[:-1])
    vec_full = jnp.transpose(gathered[-1], (1, 0, 2)).reshape(3, d_model)
    hgrn_norm_full, lb_logits_full = vec_full[0:1], vec_full[1:3]
    t_c, t_s1, t_s2 = _rope_tables(seq)
    kv_lora = kv_w_uk.shape[0]

    def hgrn_proj(a, g, *weights):
        xn = _rms(a, g).astype(BF16)
        return (xn, *[_dot(xn, wt, _NN) for wt in weights])

    xn0, zq, zf, zi = _rowcall(hgrn_proj, [xs], [hgrn_norm_full] + [w[k] for k in first_names],
                               [(d_model, BF16)] + [(d_model, F32)] * 3, [], tr=512, name="hgrn_proj")
    o_rec, states = _hgrn_fwd(zq, zf, zi, lb_logits_full, name="hgrn_fwd")
    gather_wait("hgrn_o", o_rec)

    def gate_out(o, x_n, res, gn, wg, wo):
        z = _dot(x_n, wg, _NN)
        m = _head_norm_gate(o, z, gn)[0].astype(BF16)
        return z, m, res + _dot(m, wo, _NN)

    zg, mixed, h1 = _rowcall(gate_out, [o_rec, xn0, xs], [hgrn_g_norm, w["hgrn_w_g"], w["hgrn_w_o"]],
                             [(d_model, F32), (d_model, BF16), (d_model, F32)], [], tr=512, name="hgrn_gate_out")
    h2, mlp0_saved = _mlp_fwd(h1, mlp_norm[0:1], gather_wait("up0", h1)[0], lambda act: gather_wait("down0", act)[0],
                              "mlp0")
    gather_wait("mla", h2)
    w_uq3 = w["mla_w_uq"].reshape(-1, n_heads, MLA_NOPE + MLA_ROPE)
    w_uq_nope = w_uq3[:, :, :MLA_NOPE].reshape(-1, n_heads * hd)
    w_uq_rope = jnp.pad(w_uq3[:, :, MLA_NOPE:], ((0, 0), (0, 0), (0, hd - MLA_ROPE))).reshape(-1, n_heads * hd)
    w_dkv_pad = jnp.pad(w["kv_w_dkv"], ((0, 0), (0, kv_lora + hd - w["kv_w_dkv"].shape[1])))

    q_lora, qk_cols = w["mla_w_dq"].shape[1], n_heads * hd

    def mla_qkv(a, tc, ts1, ts2, g_kv_in, g_mla, g_q, g_kv, wdq, wn, wr, wdkv, wuk, wuv):
        h_n, x_n = _rms(a, g_kv_in).astype(BF16), _rms(a, g_mla).astype(BF16)
        cq = _dot(x_n, wdq, _NN)
        cq_n = _rms(cq, g_q).astype(BF16)
        q_nope = _dot(cq_n, wn, _NN) * Q_PRESCALE
        q_rope = _rope_slabs(_dot(cq_n, wr, _NN) * Q_PRESCALE, tc, ts1, ts2, False)
        c_all = _dot(h_n, wdkv, _NN)
        lat = _rms(c_all[:, :kv_lora], g_kv).astype(BF16)
        return (h_n, x_n, cq, cq_n, q_nope, q_rope, c_all, lat, _rope(c_all[:, kv_lora:], tc, ts1, ts2),
                _dot(lat, wuk, _NN), _dot(lat, wuv, _NN))

    hn, xn2, cq_pre, c_q, qn, qr, ckr, c_kv, kr, kn, vv = _rowcall(
        mla_qkv, [h2, t_c, t_s1, t_s2],
        [kv_in_norm[None, :], mla_norm, mla_q_norm, kv_norm[None, :], w["mla_w_dq"], w_uq_nope, w_uq_rope, w_dkv_pad,
         w["kv_w_uk"], w["kv_w_uv"]],
        [(d_model, BF16), (d_model, BF16), (q_lora, F32), (q_lora, BF16), (qk_cols, BF16), (qk_cols, BF16),
         (kv_lora + hd, F32), (kv_lora, BF16), (hd, BF16), (qk_cols, BF16), (qk_cols, BF16)], [], tr=512, name="mla_qkv")
    o_att, lse = _attn_fwd(qn, qr, kn, kr, vv, name="attn_fwd")
    h3 = _rowcall(lambda o, res, wo: (res + _dot(o, wo, _NN),), [o_att, h2], [w["mla_w_o"]], [(d_model, F32)], [],
                  tr=512, name="attn_out")[0]
    gather_wait("mlp1", h3)
    (dh4, dh4_bf, g_final_norm, loss_part), mlp1_saved = _mlp_fwd(
        h3, mlp_norm[1:2], w["mlp_w_up1"], w["mlp_w_down1"], "mlp1", loss_head=(tgt, final_norm[None, :]))

    g = {}
    groups = {"mlp1": ["mlp_w_up1", "mlp_w_down1"],
              "mla": ["mla_w_o", "mla_w_uq", "mla_w_dq", "kv_w_uk", "kv_w_uv", "kv_w_dkv"],
              "mlp0": ["mlp_w_up0", "mlp_w_down0"],
              "hgrn_out": ["hgrn_w_o", "hgrn_w_g"],
              "hgrn_in": ["hgrn_w_q", "hgrn_w_f", "hgrn_w_i"]}
    scatter_state = {}

    def scatter_start(tag, after=None):
        scatter_state[tag], tok = _exchange_start(
            [g[k] if k in shard_major else (_col_terms if k in col_sharded else _row_terms)(g[k]) for k in groups[tag]],
            scatter=True, after=after,
            name=f"scatter_{tag}_start")
        return tok

    dh3, dh3_bf, g_mlp_norm1, g["mlp_w_up1"], g["mlp_w_down1"] = _mlp_bwd(
        dh4, dh4_bf, mlp1_saved, mlp_norm[1:2], w["mlp_w_up1"], w["mlp_w_down1"], "mlp1")
    def attn_out_bwd(dres, o, wo):
        d_o = _dot(dres, wo, _NT).astype(BF16)
        prod = d_o.astype(F32) * o.astype(F32)
        return d_o, jnp.concatenate([jnp.broadcast_to(jnp.sum(prod[:, h * hd:(h + 1) * hd], axis=1, keepdims=True),
                                                      (prod.shape[0], hd)) for h in range(n_heads)], axis=1)

    d_oatt, delta = _rowcall(attn_out_bwd, [dh3_bf, o_att], [w["mla_w_o"]], [(qk_cols, BF16), (qk_cols, F32)], [],
                             tr=512, after=scatter_start("mlp1"), name="attn_out_bwd_x")
    g["mla_w_o"] = _mm(o_att, dh3_bf, mode="tn", name="attn_out_bwd_w")
    dqn, dqr, dkn, dvv, dkr = _attn_bwd(qn, qr, kn, kr, vv, d_oatt, lse, delta, name="attn_bwd")

    def q_path_bwd(cq, cq_n, x_n, d_qn, d_qr, tc, ts1, ts2, g_q, wdq, wn, wr):
        d_qn, d_qr = d_qn.astype(BF16), _rope_slabs(d_qr, tc, ts1, ts2, True).astype(BF16)
        d_cq, d_gq = _rms_bwd(cq, g_q, _dot(d_qn, wn, _NT) + _dot(d_qr, wr, _NT))
        d_cq = d_cq.astype(BF16)
        return _dot(d_cq, wdq, _NT), d_gq, _dot(x_n, d_cq, _TN), _dot(cq_n, d_qn, _TN), _dot(cq_n, d_qr, _TN)

    dxn2, g_q_norm, g_dq, g_uq_nope, g_uq_rope = _rowcall(
        q_path_bwd, [cq_pre, c_q, xn2, dqn, dqr, t_c, t_s1, t_s2], [mla_q_norm, w["mla_w_dq"], w_uq_nope, w_uq_rope],
        [(d_model, F32)], [q_lora, (d_model, q_lora), (q_lora, qk_cols), (q_lora, qk_cols)], tr=512, name="mla_q_bwd")
    g["mla_w_dq"] = g_dq.astype(GRAD_WIRE_DTYPE)
    g["mla_w_uq"] = jnp.concatenate([g_uq_nope.reshape(q_lora, n_heads, hd),
                                     g_uq_rope.reshape(q_lora, n_heads, hd)[:, :, :MLA_ROPE]],
                                    axis=2).reshape(q_lora, -1).astype(GRAD_WIRE_DTYPE)

    def kv_path_bwd(c_all, lat, h_n, d_kn, d_v, d_kr_heads, tc, ts1, ts2, a, d_xn2, dres,
                    g_kv, g_kv_in, g_mla, wdkv, wuk, wuv):
        d_lat, d_gkv = _rms_bwd(c_all[:, :kv_lora], g_kv, _dot(d_kn, wuk, _NT) + _dot(d_v, wuv, _NT))
        d_kr = d_kr_heads[:, :hd]
        for h in range(1, n_heads):
            d_kr = d_kr + d_kr_heads[:, h * hd:(h + 1) * hd]
        d_all = jnp.concatenate([d_lat, _rope_t(d_kr, tc, ts1, ts2)], axis=1).astype(BF16)
        dx1, d_gkv_in = _rms_bwd(a, g_kv_in, _dot(d_all, wdkv, _NT))
        dx2, d_gmla = _rms_bwd(a, g_mla, d_xn2)
        d_a = dx1 + dx2 + dres
        return (d_a, d_a, d_gkv, d_gkv_in, d_gmla, _dot(h_n, d_all, _TN), _dot(lat, d_kn, _TN), _dot(lat, d_v, _TN))

    dh2, dh2_bf, g_kv_norm, g_kv_in_norm, g_mla_norm, g_dkv, g_uk, g_uv = _rowcall(
        kv_path_bwd, [ckr, c_kv, hn, dkn, dvv, dkr, t_c, t_s1, t_s2, h2, dxn2, dh3],
        [kv_norm[None, :], kv_in_norm[None, :], mla_norm, w_dkv_pad, w["kv_w_uk"], w["kv_w_uv"]],
        [(d_model, F32), (d_model, BF16)],
        [kv_lora, d_model, d_model, (d_model, kv_lora + hd), (kv_lora, qk_cols), (kv_lora, qk_cols)], name="mla_kv_bwd")
    g["kv_w_dkv"] = g_dkv[:, :kv_w_dkv.shape[1]].astype(GRAD_WIRE_DTYPE)
    g["kv_w_uk"], g["kv_w_uv"] = g_uk.astype(GRAD_WIRE_DTYPE), g_uv.astype(GRAD_WIRE_DTYPE)
    dh1, dh1_bf, g_mlp_norm0, g["mlp_w_up0"], g["mlp_w_down0"] = _mlp_bwd(
        dh2, dh2_bf, mlp0_saved, mlp_norm[0:1], w["mlp_w_up0"], w["mlp_w_down0"], "mlp0", after=scatter_start("mla"))

    g["hgrn_w_o"] = _mm(mixed, dh1_bf, mode="tn", after=scatter_start("mlp0"), name="hgrn_out_bwd_w")
    do_rec, dzg, g_g_norm = _rowcall(
        lambda dres, o, z, wo, gn: _head_norm_gate_bwd(o, z, _dot(dres, wo, _NT), gn), [dh1_bf, o_rec, zg],
        [w["hgrn_w_o"], hgrn_g_norm], [(d_model, F32), (d_model, BF16)], [hd], tr=512, name="hgrn_gate_out_bwd")
    g["hgrn_w_g"] = _mm(xn0, dzg, mode="tn", name="hgrn_w_g_bwd_w")
    dzq, dzf, dzi, g_lb = _hgrn_bwd(zq, zf, zi, lb_logits_full, states, do_rec, scatter_start("hgrn_out"),
                                    name="hgrn_bwd")
    for nm, dz in (("hgrn_w_q", dzq), ("hgrn_w_f", dzf), ("hgrn_w_i", dzi)):
        g[nm] = _mm(xn0, dz, mode="tn", name=f"{nm}_bwd_w")

    def hgrn_proj_bwd(a, dres, *rest):
        dzs, gw, weights = rest[:4], rest[4], rest[5:]
        dxn = _dot(dzs[0], weights[0], _NT)
        for dz, wt in zip(dzs[1:], weights[1:]):
            dxn = dxn + _dot(dz, wt, _NT)
        dx, dw = _rms_bwd(a, gw, dxn)
        return dx + dres, dw

    grad_x, g_hgrn_norm = _rowcall(hgrn_proj_bwd, [xs, dh1, dzq, dzf, dzi, dzg],
                                   [hgrn_norm_full] + [w[k] for k in proj_names], [(d_model, F32)], [d_model],
                                   tr=512, name="hgrn_proj_bwd")

    small_parts = [g_hgrn_norm, g_lb, g_g_norm, g_mla_norm, g_q_norm, g_kv_in_norm, g_kv_norm, g_mlp_norm0,
                   g_mlp_norm1, g_final_norm, loss_part]
    small_sizes = [p.shape[1] for p in small_parts]
    small_terms = _exchange([jnp.concatenate(small_parts, axis=1)], scatter=False, name="gather_small")[0]
    small_sum = _sum_terms(small_terms, name="sum_small")
    last = scatter_start("hgrn_in", after=small_sum)
    offs = [0]
    for sz in small_sizes:
        offs.append(offs[-1] + sz)
    (s_hgrn_norm, s_lb, s_g_norm, s_mla_norm, s_q_norm, s_kv_in_norm, s_kv_norm, s_mlp_norm0, s_mlp_norm1, s_final_norm,
     s_loss) = [small_sum[:, a:b] for a, b in zip(offs[:-1], offs[1:])]
    shard = hgrn_norm.shape[1]
    g_lb_logits = _lb_logits_grad(lax.dynamic_slice_in_dim(s_lb, me * shard, shard, axis=1), hgrn_lb_logits,
                                  name="lb_logits_grad")
    loss = s_loss[0, 0]

    res, layer_terms = {}, {}

    def update(k, term_list):
        shape = given[k].shape
        as_layers = (len(term_list), shape[-2], shape[-1])
        upd = _adam(given[k].reshape(as_layers), term_list, given["m_" + k].reshape(as_layers),
                    given["v_" + k].reshape(as_layers), name=f"adam_{k}")
        res[k] = [o.reshape(shape) for o in upd]
        return upd[0]

    for tag, names in groups.items():
        for k, t in zip(names, _exchange_wait(scatter_state[tag], last, name=f"scatter_{tag}_wait")):
            if k.startswith("mlp_w_"):
                layer_terms.setdefault(k[:-1], {})[int(k[-1])] = t
                if len(layer_terms[k[:-1]]) == 2:
                    last = update(k[:-1], [layer_terms[k[:-1]][0], layer_terms[k[:-1]][1]])
            else:
                last = update(k, [t])

    small_grads = {
        "hgrn_norm": lax.dynamic_slice_in_dim(s_hgrn_norm, me * shard, shard, axis=1),
        "hgrn_g_norm": s_g_norm, "hgrn_lb_logits": g_lb_logits, "mla_norm": s_mla_norm, "mla_q_norm": s_q_norm,
        "kv_in_norm": s_kv_in_norm, "kv_norm": s_kv_norm,
        "mlp_norm": jnp.concatenate([s_mlp_norm0, s_mlp_norm1], axis=0), "final_norm": s_final_norm,
    }
    small_names = list(small_grads)

    def flat(a):
        return a.reshape(1, -1)

    packed = [jnp.concatenate([flat(src[pre + k]) for k in small_names], axis=1)
              for src, pre in ((given, ""), (small_grads, ""), (given, "m_"), (given, "v_"))]
    small_out = _adam(packed[0][None], [packed[1][None]], packed[2][None], packed[3][None], name="adam_small")
    off = 0
    for k in small_names:
        size = given[k].size
        res[k] = [o[0, :, off:off + size].reshape(given[k].shape) for o in small_out]
        off += size

    outs = [loss, grad_x[None]]
    for i in range(4):
        outs += [res[k][i] for k in weight_names]
    return tuple(outs)
```

```python
import functools

import jax
import jax.numpy as jnp
from jax import lax
from jax.experimental import pallas as pl
from jax.experimental.pallas import tpu as pltpu

F32 = jnp.float32
BF16 = jnp.bfloat16

EPS = 1e-6
LANES = 128
N_DEV = 8
V7X_VMEM_LIMIT_BYTES = 56 << 20
MM_PIPELINE_BYTES = 30 << 20
MM_ROW_TILE = 512
ADAM_ROW_TILE = 256
GRAD_WIRE_DTYPE = BF16

HGRN_HEADS = 8
HGRN_CHUNK = 64
HGRN_SUB = 16
HGRN_HEADS_PER_STEP = 8
HGRN_CHUNKS_PER_STEP = 4
EXP_CLAMP = 80.0
MLA_HEADS = 16
MLA_NOPE = 128
MLA_ROPE = 64
ROPE_THETA = 10000.0
ATTN_SCALE = (MLA_NOPE + MLA_ROPE) ** -0.5

ADAM_LR = 0.001
ADAM_B1 = 0.9
ADAM_B2 = 0.999
ADAM_EPS = 1e-08
ADAM_WD = 0.01
ADAM_STEP = 10

_NN = ((1,), (0,))
_NT = ((1,), (1,))
_TN = ((0,), (0,))


def _params(*sem):
    return pltpu.CompilerParams(dimension_semantics=sem, vmem_limit_bytes=V7X_VMEM_LIMIT_BYTES)


def _dot(a, b, dims):
    return lax.dot_general(a.astype(BF16), b.astype(BF16), (dims, ((), ())), preferred_element_type=F32)


def _dot_f32(a, b, dims=_NN):
    return lax.dot_general(a, b, (dims, ((), ())), precision=lax.Precision.HIGH, preferred_element_type=F32)


def _sigmoid(x):
    return 1.0 / (1.0 + jnp.exp(-x))


def _rms(x, w):
    r = lax.rsqrt(jnp.mean(x * x, axis=-1, keepdims=True) + EPS)
    return x * r * w


def _rms_bwd(x, w, dy):
    r = lax.rsqrt(jnp.mean(x * x, axis=-1, keepdims=True) + EPS)
    xh = x * r
    dw = jnp.sum(dy * xh, axis=0, keepdims=True)
    dxh = dy * w
    dx = r * (dxh - xh * jnp.mean(dxh * xh, axis=-1, keepdims=True))
    return dx, dw


def _mm_tiles(m, n, k, a_bytes, b_bytes, out_tile_bytes):
    tm = min(m, MM_ROW_TILE)
    for tn in (n, 2048, 1024, 512, 256, LANES):
        if tn <= n and n % tn == 0:
            if 2 * (tm * k * a_bytes + k * tn * b_bytes + tm * tn * out_tile_bytes) <= MM_PIPELINE_BYTES:
                return tm, tn
    return tm, min(n, LANES)


def _mm(a, b, *, mode, name, out_dtype=None, after=None, col_shards=None):
    if mode == "nn":
        (m, k), (k2, n) = a.shape, b.shape
    elif mode == "nt":
        (m, k), (n, k2) = a.shape, b.shape
    else:
        (k, m), (k2, n) = a.shape, b.shape
    assert k == k2, (name, a.shape, b.shape)
    if out_dtype is None:
        out_dtype = GRAD_WIRE_DTYPE if mode == "tn" else F32
    tm, tn = _mm_tiles(m, n, k, a.dtype.itemsize, b.dtype.itemsize, jnp.dtype(out_dtype).itemsize)
    if col_shards is not None:
        tn = n // col_shards
    assert m % tm == 0 and n % tn == 0, (name, m, n)
    dims = {"nn": _NN, "nt": _NT, "tn": _TN}[mode]
    a_spec = pl.BlockSpec((k, tm), lambda i, j: (0, i)) if mode == "tn" else pl.BlockSpec((tm, k), lambda i, j: (i, 0))
    b_spec = pl.BlockSpec((tn, k), lambda i, j: (j, 0)) if mode == "nt" else pl.BlockSpec((k, tn), lambda i, j: (0, j))
    o_spec = pl.BlockSpec((tm, tn), lambda i, j: (i, j))
    operands, in_specs = [a, b], [a_spec, b_spec]
    if after is not None:
        operands.append(after)
        in_specs.append(pl.BlockSpec(memory_space=pl.ANY))
    out_shape = jax.ShapeDtypeStruct((m, n), out_dtype)
    if col_shards is not None:
        out_shape = jax.ShapeDtypeStruct((col_shards, m, tn), out_dtype)
        o_spec = pl.BlockSpec((None, tm, tn), lambda i, j: (j, i, 0))

    def body(*refs):
        refs[-1][...] = _dot(refs[0][...], refs[1][...], dims).astype(out_dtype)

    return pl.pallas_call(
        body, name=name, grid=(m // tm, n // tn), in_specs=in_specs, out_specs=o_spec, out_shape=out_shape,
        compiler_params=_params("parallel", "parallel"),
    )(*operands)


def _rowcall(fn, rows, consts, outs, accs, *, name, tr=256, after=None):
    s = rows[0].shape[0]
    tr = min(tr, s)
    assert s % tr == 0
    n_out = len(outs)
    accs = [(1, a) if isinstance(a, int) else a for a in accs]
    in_specs = [pl.BlockSpec((tr, r.shape[1]), lambda i: (i, 0)) for r in rows]
    in_specs += [pl.BlockSpec(c.shape, lambda i, nd=c.ndim: (0,) * nd) for c in consts]
    out_shape = [jax.ShapeDtypeStruct((s, w), dt) for w, dt in outs] + [jax.ShapeDtypeStruct(a, F32) for a in accs]
    out_specs = [pl.BlockSpec((tr, w), lambda i: (i, 0)) for w, _ in outs] + [pl.BlockSpec(a, lambda i: (0, 0)) for a in accs]
    n_in = len(rows) + len(consts)

    def body(*refs):
        res = fn(*[r[...] for r in refs[:n_in]])
        out_refs = refs[n_in + (after is not None):]
        for ref, val in zip(out_refs[:n_out], res[:n_out]):
            ref[...] = val.astype(ref.dtype)
        i = pl.program_id(0)
        for ref, val in zip(out_refs[n_out:], res[n_out:]):
            @pl.when(i == 0)
            def _(ref=ref, val=val):
                ref[...] = val

            @pl.when(i > 0)
            def _(ref=ref, val=val):
                ref[...] += val

    behind = [] if after is None else [after]
    return pl.pallas_call(
        body, name=name, grid=(s // tr,), in_specs=in_specs + [pl.BlockSpec(memory_space=pl.ANY)] * len(behind),
        out_specs=out_specs, out_shape=out_shape, compiler_params=_params("arbitrary" if accs else "parallel"),
    )(*rows, *consts, *behind)


def _rope_tables(seq):
    half = MLA_ROPE // 2
    inv_freq = ROPE_THETA ** (-jnp.arange(half, dtype=F32) / half)
    ang = jnp.arange(seq, dtype=F32)[:, None] * inv_freq[None, :]
    cos, sin, zero = jnp.cos(ang), jnp.sin(ang), jnp.zeros((seq, half), F32)
    t_c = jnp.concatenate([cos, cos, zero, zero], axis=1)
    t_s1 = jnp.concatenate([-sin, zero, zero, zero], axis=1)
    t_s2 = jnp.concatenate([zero, sin, zero, zero], axis=1)
    return t_c, t_s1, t_s2


def _rope(slab, t_c, t_s1, t_s2):
    return slab * t_c + pltpu.roll(slab, 96, 1) * t_s1 + pltpu.roll(slab, 32, 1) * t_s2


def _rope_t(d, t_c, t_s1, t_s2):
    return d * t_c + pltpu.roll(d * t_s1, 32, 1) + pltpu.roll(d * t_s2, 96, 1)


def _lower_bound(logits):
    l0, l1 = logits[0:1, :], logits[1:2, :]
    mx = jnp.maximum(l0, l1)
    e0, e1 = jnp.exp(l0 - mx), jnp.exp(l1 - mx)
    return e0 / (e0 + e1)


def _tri(n, lower):
    row = lax.broadcasted_iota(jnp.int32, (n, n), 0)
    col = lax.broadcasted_iota(jnp.int32, (n, n), 1)
    return (row >= col) if lower else (row <= col)


def _hgrn_fwd(zq, zf, zi, lb_logits, *, name):
    s, d = zq.shape
    h_n, c, hp, cps = d // LANES, HGRN_CHUNK, HGRN_HEADS_PER_STEP, HGRN_CHUNKS_PER_STEP
    nc = s // c

    def body(zq_ref, zf_ref, zi_ref, lb_ref, o_ref, st_ref, state_sc, b_sc):
        @pl.when(pl.program_id(1) == 0)
        def _():
            state_sc[...] = jnp.zeros_like(state_sc)

        lower = _tri(c, True)
        lower_f = lower.astype(F32)
        hs, pairs = range(hp), [(cc, hh) for cc in range(cps) for hh in range(hp)]
        sls = [slice(hh * LANES, (hh + 1) * LANES) for hh in hs]
        rws = [slice(cc * c, (cc + 1) * c) for cc in range(cps)]
        lb = [_lower_bound(lb_ref[:, sl]) for sl in sls]
        zq_v = {p: zq_ref[rws[p[0]], sls[p[1]]] for p in pairs}
        q = {p: zq_v[p] * _sigmoid(zq_v[p]) for p in pairs}
        f = {p: lb[p[1]] + (1.0 - lb[p[1]]) * _sigmoid(zf_ref[rws[p[0]], sls[p[1]]]) for p in pairs}
        k = {p: 1.0 - f[p] for p in pairs}
        v = {p: zi_ref[rws[p[0]], sls[p[1]]] for p in pairs}
        b = {p: _dot_f32(lower_f, jnp.log(f[p])) for p in pairs}
        for p in pairs:
            b_sc[p[0], p[1]] = b[p]
        qe = {p: q[p] * jnp.exp(b[p]) for p in pairs}
        scores = {p: [] for p in pairs}
        for i in range(c // HGRN_SUB):
            lo = i * HGRN_SUB
            for p in pairs:
                ref = b_sc[p[0], p[1], lo - 1:lo, :] if i > 0 else jnp.zeros((1, LANES), F32)
                qt = q[p][lo:lo + HGRN_SUB, :] * jnp.exp(b[p][lo:lo + HGRN_SUB, :] - ref)
                dec = jnp.exp(jnp.minimum(ref - b[p], EXP_CLAMP))
                scores[p].append(_dot(qt, k[p] * dec, _NT))
        o_intra = {p: _dot(jnp.where(lower, jnp.concatenate(scores[p], axis=0), 0.0), v[p], _NN) for p in pairs}
        bl = {p: b_sc[p[0], p[1], c - 1:c, :] for p in pairs}
        k_end = {p: k[p] * jnp.exp(bl[p] - b[p]) for p in pairs}
        state = [state_sc[hh] for hh in hs]
        for cc in range(cps):
            for hh in hs:
                st_ref[hh, cc] = state[hh]
                o_ref[rws[cc], sls[hh]] = _dot(qe[cc, hh], state[hh], _NT) + o_intra[cc, hh]
            state = [state[hh] * jnp.exp(bl[cc, hh]) + _dot(v[cc, hh], k_end[cc, hh], _TN) for hh in hs]
        for hh in hs:
            state_sc[hh] = state[hh]

    tile = pl.BlockSpec((cps * c, hp * LANES), lambda h, i: (i, h))
    return pl.pallas_call(
        body, name=name, grid=(h_n // hp, nc // cps),
        in_specs=[tile, tile, tile, pl.BlockSpec((2, hp * LANES), lambda h, i: (0, h))],
        out_specs=[tile, pl.BlockSpec((hp, cps, LANES, LANES), lambda h, i: (h, i, 0, 0))],
        out_shape=[jax.ShapeDtypeStruct((s, d), F32), jax.ShapeDtypeStruct((h_n, nc, LANES, LANES), F32)],
        scratch_shapes=[pltpu.VMEM((hp, LANES, LANES), F32), pltpu.VMEM((cps, hp, c, LANES), F32)],
        compiler_params=_params("parallel", "arbitrary"),
    )(zq, zf, zi, lb_logits)


def _hgrn_bwd(zq, zf, zi, lb_logits, states, do, after, *, name):
    s, d = zq.shape
    h_n, c, hp, cps = d // LANES, HGRN_CHUNK, HGRN_HEADS_PER_STEP, HGRN_CHUNKS_PER_STEP
    nc = s // c
    n_steps = nc // cps

    def body(zq_ref, zf_ref, zi_ref, lb_ref, st_ref, do_ref, _, dzq_ref, dzf_ref, dzi_ref, dlb_ref, dstate_sc, b_sc):
        @pl.when(pl.program_id(1) == 0)
        def _():
            dstate_sc[...] = jnp.zeros_like(dstate_sc)
            dlb_ref[...] = jnp.zeros_like(dlb_ref)

        lower, upper = _tri(c, True), _tri(c, False).astype(F32)
        lower_f = lower.astype(F32)
        last_row = lax.broadcasted_iota(jnp.int32, (c, LANES), 0) == c - 1
        hs, pairs = range(hp), [(cc, hh) for cc in range(cps) for hh in range(hp)]
        sls = [slice(hh * LANES, (hh + 1) * LANES) for hh in hs]
        rws = [slice(cc * c, (cc + 1) * c) for cc in range(cps)]
        lb = [_lower_bound(lb_ref[:, sl]) for sl in sls]
        zq_v = {p: zq_ref[rws[p[0]], sls[p[1]]] for p in pairs}
        sq = {p: _sigmoid(zq_v[p]) for p in pairs}
        q = {p: zq_v[p] * sq[p] for p in pairs}
        sf = {p: _sigmoid(zf_ref[rws[p[0]], sls[p[1]]]) for p in pairs}
        f = {p: lb[p[1]] + (1.0 - lb[p[1]]) * sf[p] for p in pairs}
        k = {p: 1.0 - f[p] for p in pairs}
        v = {p: zi_ref[rws[p[0]], sls[p[1]]] for p in pairs}
        d_o = {p: do_ref[rws[p[0]], sls[p[1]]] for p in pairs}
        b = {p: _dot_f32(lower_f, jnp.log(f[p])) for p in pairs}
        s0t = {p: st_ref[p[1], p[0]] for p in pairs}
        for p in pairs:
            b_sc[p[0], p[1]] = b[p]
        bl = {p: b_sc[p[0], p[1], c - 1:c, :] for p in pairs}
        eb = {p: jnp.exp(b[p]) for p in pairs}
        ebl = {p: jnp.exp(bl[p]) for p in pairs}
        dec_end = {p: jnp.exp(bl[p] - b[p]) for p in pairs}
        da = {p: jnp.where(lower, _dot(d_o[p], v[p], _NT), 0.0) for p in pairs}
        dq = {p: _dot(d_o[p], s0t[p], _NN) * eb[p] for p in pairs}
        dstate_in = {p: _dot(d_o[p], q[p] * eb[p], _TN) for p in pairs}
        dk_intra = {p: jnp.zeros((c, LANES), F32) for p in pairs}
        scores, dq_blocks = {p: [] for p in pairs}, {p: [] for p in pairs}
        for i in range(c // HGRN_SUB):
            lo = i * HGRN_SUB
            for p in pairs:
                ref = b_sc[p[0], p[1], lo - 1:lo, :] if i > 0 else jnp.zeros((1, LANES), F32)
                grow = jnp.exp(b[p][lo:lo + HGRN_SUB, :] - ref)
                qt = q[p][lo:lo + HGRN_SUB, :] * grow
                dec = jnp.exp(jnp.minimum(ref - b[p], EXP_CLAMP))
                kd = k[p] * dec
                scores[p].append(_dot(qt, kd, _NT))
                da_i = da[p][lo:lo + HGRN_SUB, :]
                dq_blocks[p].append(_dot_f32(da_i, kd, _NN) * grow)
                dk_intra[p] = dk_intra[p] + _dot_f32(da_i, qt, _TN) * dec
        dv_intra = {p: _dot(jnp.where(lower, jnp.concatenate(scores[p], axis=0), 0.0), d_o[p], _TN) for p in pairs}
        dq = {p: dq[p] + jnp.concatenate(dq_blocks[p], axis=0) for p in pairs}
        q_dq = {p: q[p] * dq[p] for p in pairs}
        for p in pairs:
            dzq_ref[rws[p[0]], sls[p[1]]] = (dq[p] * sq[p] * (1.0 + zq_v[p] * (1.0 - sq[p]))).astype(BF16)
        dstate = [dstate_sc[hh] for hh in hs]
        for cc in reversed(range(cps)):
            ps = [(cc, hh) for hh in hs]
            dk_state = [_dot(v[p], dstate[p[1]], _NN) * dec_end[p] for p in ps]
            dv = [dv_intra[p] + _dot(k[p] * dec_end[p], dstate[p[1]], _NT) for p in ps]
            dk = [dk_intra[p] + dk_state[p[1]] for p in ps]
            db_last = [jnp.sum(k[p] * dk_state[p[1]], axis=0, keepdims=True)
                       + ebl[p] * jnp.sum(s0t[p] * dstate[p[1]], axis=0, keepdims=True) for p in ps]
            db = [q_dq[p] - k[p] * dk[p[1]] + jnp.where(last_row, db_last[p[1]], 0.0) for p in ps]
            df = [_dot_f32(upper, db[p[1]]) / f[p] - dk[p[1]] for p in ps]
            for p in ps:
                hh = p[1]
                dzf_ref[rws[cc], sls[hh]] = (df[hh] * (1.0 - lb[hh]) * sf[p] * (1.0 - sf[p])).astype(BF16)
                dlb_ref[:, sls[hh]] += jnp.sum(df[hh] * (1.0 - sf[p]), axis=0, keepdims=True)
                dzi_ref[rws[cc], sls[hh]] = dv[hh].astype(BF16)
            dstate = [dstate[p[1]] * ebl[p] + dstate_in[p] for p in ps]
        for hh in hs:
            dstate_sc[hh] = dstate[hh]

    tile = pl.BlockSpec((cps * c, hp * LANES), lambda h, i: (n_steps - 1 - i, h))
    out = jax.ShapeDtypeStruct((s, d), BF16)
    return pl.pallas_call(
        body, name=name, grid=(h_n // hp, n_steps),
        in_specs=[tile, tile, tile, pl.BlockSpec((2, hp * LANES), lambda h, i: (0, h)),
                  pl.BlockSpec((hp, cps, LANES, LANES), lambda h, i: (h, n_steps - 1 - i, 0, 0)), tile,
                  pl.BlockSpec(memory_space=pl.ANY)],
        out_specs=[tile, tile, tile, pl.BlockSpec((1, hp * LANES), lambda h, i: (0, h))],
        out_shape=[out, out, out, jax.ShapeDtypeStruct((1, d), F32)],
        scratch_shapes=[pltpu.VMEM((hp, LANES, LANES), F32), pltpu.VMEM((cps, hp, c, LANES), F32)],
        compiler_params=_params("parallel", "arbitrary"),
    )(zq, zf, zi, lb_logits, states, do, after)


ATTN_SUB_ROWS = 256
LOG2E = 1.4426950408889634
LN2 = 0.6931471805599453
Q_PRESCALE = ATTN_SCALE * LOG2E


def _attn_tile(s):
    return min(1024, max(128, s // 2))


def _causal_pairs(n, q_major):
    pairs = [(i, j) for i in range(n) for j in range(i + 1)] if q_major else [(i, j) for j in range(n) for i in range(j, n)]
    return jnp.asarray([p[0] for p in pairs], jnp.int32), jnp.asarray([p[1] for p in pairs], jnp.int32)


def _sub_scores(qn_ref, qr_ref, k, r, sub, t, diagonal):
    q = jnp.concatenate([qn_ref[r:r + sub, :], qr_ref[r:r + sub, :]], axis=1)
    if not diagonal:
        return q, _dot(q, k, _NT)
    cols = r + sub
    keep = lax.broadcasted_iota(jnp.int32, (sub, cols), 1) <= r + lax.broadcasted_iota(jnp.int32, (sub, cols), 0)
    return q, jnp.where(keep, _dot(q, k[:cols], _NT), -jnp.inf)


def _attn_fwd(qn, qr, kn, kr, v, *, name):
    s, t = qn.shape[0], _attn_tile(qn.shape[0])
    sub = min(t, ATTN_SUB_ROWS)
    q_blk, k_blk = _causal_pairs(s // t, True)

    def body(qi_ref, kj_ref, qn_ref, qr_ref, kn_ref, kr_ref, v_ref, o_ref, lse_ref, m_sc, l_sc, acc_sc):
        p_id = pl.program_id(1)
        i, j = qi_ref[p_id], kj_ref[p_id]

        @pl.when(j == 0)
        def _():
            m_sc[...] = jnp.full_like(m_sc, -jnp.inf)
            l_sc[...] = jnp.zeros_like(l_sc)
            acc_sc[...] = jnp.zeros_like(acc_sc)

        def update(diagonal):
            k = jnp.concatenate([kn_ref[...], kr_ref[...]], axis=1)
            v = v_ref[...]
            starts = list(range(0, t, sub))
            scs = [_sub_scores(qn_ref, qr_ref, k, r, sub, t, diagonal)[1] for r in starts]
            ps, alphas = [], []
            for r, sc in zip(starts, scs):
                m_prev = m_sc[r:r + sub, :]
                m_new = jnp.maximum(m_prev, jnp.max(sc, axis=1, keepdims=True))
                alpha = jnp.exp2(m_prev - m_new)
                p = jnp.exp2(sc - m_new[:, :1])
                l_sc[r:r + sub, :] = alpha * l_sc[r:r + sub, :] + jnp.sum(p, axis=1, keepdims=True)
                m_sc[r:r + sub, :] = m_new
                ps.append(p)
                alphas.append(alpha)
            for r, p, alpha in zip(starts, ps, alphas):
                acc_sc[r:r + sub, :] = alpha * acc_sc[r:r + sub, :] + _dot(p, v[:p.shape[1]], _NN)

        @pl.when(j < i)
        def _():
            update(False)

        @pl.when(j == i)
        def _():
            update(True)
            o_ref[...] = (acc_sc[...] / l_sc[...]).astype(BF16)
            lse_ref[...] = m_sc[...] + jnp.log(l_sc[...]) * LOG2E

    q_spec = pl.BlockSpec((t, LANES), lambda h, p, qi, kj: (qi[p], h))
    k_spec = pl.BlockSpec((t, LANES), lambda h, p, qi, kj: (kj[p], h))
    kr_spec = pl.BlockSpec((t, LANES), lambda h, p, qi, kj: (kj[p], 0))
    stat = pltpu.VMEM((t, LANES), F32)
    return pl.pallas_call(
        body, name=name,
        grid_spec=pltpu.PrefetchScalarGridSpec(
            num_scalar_prefetch=2, grid=(MLA_HEADS, q_blk.shape[0]),
            in_specs=[q_spec, q_spec, k_spec, kr_spec, k_spec], out_specs=[q_spec, q_spec],
            scratch_shapes=[stat, stat, stat]),
        out_shape=[jax.ShapeDtypeStruct(qn.shape, BF16), jax.ShapeDtypeStruct(qn.shape, F32)],
        compiler_params=_params("parallel", "arbitrary"),
    )(q_blk, k_blk, qn, qr, kn, kr, v)


def _attn_bwd(qn, qr, kn, kr, v, do, lse, delta, *, name):
    s, t = qn.shape[0], _attn_tile(qn.shape[0])
    n, sub = s // t, min(t, ATTN_SUB_ROWS)
    q_blk, k_blk = _causal_pairs(n, False)

    def body(qi_ref, kj_ref, qn_ref, qr_ref, kn_ref, kr_ref, v_ref, do_ref, lse_ref, delta_ref,
             dqn_ref, dqr_ref, dkn_ref, dv_ref, dkr_ref, dk_sc, dv_sc):
        p_id = pl.program_id(1)
        i, j = qi_ref[p_id], kj_ref[p_id]

        @pl.when(p_id == 0)
        def _():
            dqn_ref[...] = jnp.zeros_like(dqn_ref)
            dqr_ref[...] = jnp.zeros_like(dqr_ref)

        @pl.when(i == j)
        def _():
            dk_sc[...] = jnp.zeros_like(dk_sc)
            dv_sc[...] = jnp.zeros_like(dv_sc)

        def accumulate(diagonal):
            k = jnp.concatenate([kn_ref[...], kr_ref[...]], axis=1)
            v = v_ref[...]
            starts = list(range(0, t, sub))
            qs, d_os, scs, dps = [], [], [], []
            for r in starts:
                q, sc = _sub_scores(qn_ref, qr_ref, k, r, sub, t, diagonal)
                d_o = do_ref[r:r + sub, :]
                qs.append(q)
                d_os.append(d_o)
                scs.append(sc)
                dps.append(_dot(d_o, v[:sc.shape[1]], _NT))
            ps, dss = [], []
            for r, sc, dp in zip(starts, scs, dps):
                p = jnp.exp2(sc - lse_ref[r:r + sub, :][:, :1])
                ps.append(p.astype(BF16))
                dss.append((p * (dp - delta_ref[r:r + sub, :][:, :1])).astype(BF16))
            for r, q, d_o, p, ds in zip(starts, qs, d_os, ps, dss):
                cols = p.shape[1]
                dv_sc[:cols, :] += _dot(p, d_o, _TN)
                dk_sc[:cols, :] += _dot(ds, q, _TN)
                dq = _dot(ds, k[:cols], _NN) * ATTN_SCALE
                rows = pl.ds(pl.multiple_of(i * t + r, sub), sub)
                dqn_ref[rows, :] += dq[:, :LANES]
                dqr_ref[rows, :] += dq[:, LANES:]

        @pl.when(j < i)
        def _():
            accumulate(False)

        @pl.when(j == i)
        def _():
            accumulate(True)

        @pl.when(i == n - 1)
        def _():
            dkn_ref[...] = (dk_sc[:, :LANES] * LN2).astype(BF16)
            dkr_ref[...] = dk_sc[:, LANES:] * LN2
            dv_ref[...] = dv_sc[...].astype(BF16)

    q_spec = pl.BlockSpec((t, LANES), lambda h, p, qi, kj: (qi[p], h))
    k_spec = pl.BlockSpec((t, LANES), lambda h, p, qi, kj: (kj[p], h))
    kr_spec = pl.BlockSpec((t, LANES), lambda h, p, qi, kj: (kj[p], 0))
    head_spec = pl.BlockSpec((s, LANES), lambda h, p, qi, kj: (0, h))
    f32_out, bf16_out = jax.ShapeDtypeStruct(qn.shape, F32), jax.ShapeDtypeStruct(qn.shape, BF16)
    return pl.pallas_call(
        body, name=name,
        grid_spec=pltpu.PrefetchScalarGridSpec(
            num_scalar_prefetch=2, grid=(MLA_HEADS, q_blk.shape[0]),
            in_specs=[q_spec, q_spec, k_spec, kr_spec, k_spec, q_spec, q_spec, q_spec],
            out_specs=[head_spec, head_spec, k_spec, k_spec, k_spec],
            scratch_shapes=[pltpu.VMEM((t, 2 * LANES), F32), pltpu.VMEM((t, LANES), F32)]),
        out_shape=[f32_out, f32_out, bf16_out, bf16_out, f32_out],
        compiler_params=_params("parallel", "arbitrary"),
    )(q_blk, k_blk, qn, qr, kn, kr, v, do, lse, delta)


def _exchange(arrs, *, scatter, name):
    n = len(arrs)
    out_shape = [jax.ShapeDtypeStruct(a.shape if scatter else (N_DEV, *a.shape), a.dtype) for a in arrs]

    def body(*refs):
        ins, outs = refs[:n], refs[n:2 * n]
        send_sems, recv_sems, local_sems = refs[2 * n:]
        x, y, c = lax.axis_index("x"), lax.axis_index("y"), lax.axis_index("c")
        me = 4 * x + 2 * y + c
        copies = []
        for k in range(n):
            local = pltpu.make_async_copy(ins[k].at[me] if scatter else ins[k], outs[k].at[me], local_sems.at[k])
            local.start()
            copies.append(local)
            for d in range(1, N_DEV):
                px, py, pc = (x + (d >> 2)) % 2, (y + ((d >> 1) & 1)) % 2, (c + (d & 1)) % 2
                peer = 4 * px + 2 * py + pc
                remote = pltpu.make_async_remote_copy(
                    src_ref=ins[k].at[peer] if scatter else ins[k], dst_ref=outs[k].at[me],
                    send_sem=send_sems.at[k, d - 1], recv_sem=recv_sems.at[k, d - 1],
                    device_id=(px, py, pc), device_id_type=pl.DeviceIdType.MESH)
                remote.start()
                copies.append(remote)
        for cp in copies:
            cp.wait()

    any_spec = pl.BlockSpec(memory_space=pl.ANY)
    return pl.pallas_call(
        body, name=name, in_specs=[any_spec] * n, out_specs=[any_spec] * n, out_shape=out_shape,
        scratch_shapes=[pltpu.SemaphoreType.DMA((n, N_DEV - 1)), pltpu.SemaphoreType.DMA((n, N_DEV - 1)),
                        pltpu.SemaphoreType.DMA((n,))],
    )(*arrs)


def _peers(x, y, c):
    out = []
    for d in range(1, N_DEV):
        px, py, pc = (x + (d >> 2)) % 2, (y + ((d >> 1) & 1)) % 2, (c + (d & 1)) % 2
        out.append(((px, py, pc), 4 * px + 2 * py + pc))
    return out


CHIP_LEVEL_PEERS = (1, 2, 4, 6)


def _exchange_copies(ins, lands, send_sems, recv_sems, scatter, chip_level=False):
    x, y, c = lax.axis_index("x"), lax.axis_index("y"), lax.axis_index("c")
    me = 4 * x + 2 * y + c
    local, remote = [], []
    for k in range(len(ins)):
        local.append(pltpu.make_async_copy(ins[k].at[me] if scatter else ins[k], lands[k].at[me],
                                           recv_sems.at[k * N_DEV + N_DEV - 1]))
        for d, (coords, peer) in enumerate(_peers(x, y, c)):
            if chip_level and d + 1 not in CHIP_LEVEL_PEERS:
                continue
            remote.append(pltpu.make_async_remote_copy(
                src_ref=ins[k].at[peer] if scatter else ins[k], dst_ref=lands[k].at[me],
                send_sem=send_sems.at[k * N_DEV + d], recv_sem=recv_sems.at[k * N_DEV + d],
                device_id=coords, device_id_type=pl.DeviceIdType.MESH))
    return local, remote


def _exchange_start(arrs, *, scatter, name, after=None, chip_level=False):
    n = len(arrs)
    hbm = pl.BlockSpec(memory_space=pltpu.HBM)
    sem = pl.BlockSpec(memory_space=pltpu.SEMAPHORE)
    lands = [lax.empty(a.shape if scatter else (N_DEV, *a.shape), a.dtype) for a in arrs]

    def body(*refs):
        ins, land_refs = refs[:n], refs[n:2 * n]
        first_out = 2 * n + (after is not None)
        send_sems, recv_sems, token = refs[first_out], refs[first_out + 1], refs[-1]
        local, remote = _exchange_copies(ins, land_refs, send_sems, recv_sems, scatter, chip_level)
        for cp in local + remote:
            cp.start()
        token[...] = jnp.zeros_like(token)

    operands = [pltpu.with_memory_space_constraint(a, pltpu.HBM) for a in list(arrs) + lands]
    behind = [] if after is None else [after]
    res = pl.pallas_call(
        body, name=name,
        out_shape=(pltpu.SemaphoreType.DMA((n * N_DEV,)), pltpu.SemaphoreType.DMA((n * N_DEV,)),
                   *[pltpu.HBM(o.shape, o.dtype) for o in operands], jax.ShapeDtypeStruct((8, LANES), F32)),
        in_specs=[hbm] * (2 * n) + [pl.BlockSpec(memory_space=pl.ANY)] * len(behind),
        out_specs=(sem, sem, *[hbm] * (2 * n), pl.BlockSpec(memory_space=pltpu.VMEM)),
        input_output_aliases={i: 2 + i for i in range(2 * n)},
        compiler_params=pltpu.CompilerParams(has_side_effects=pltpu.SideEffectType.DATAFLOW_SIDE_EFFECTING),
    )(*operands, *behind)
    return (res[0], res[1], list(res[2:2 + n]), list(res[2 + n:2 + 2 * n]), scatter, chip_level), res[-1]


def _exchange_wait(state, after, *, name):
    send_sems, recv_sems, ins, lands, scatter, chip_level = state
    n = len(ins)
    hbm = pl.BlockSpec(memory_space=pltpu.HBM)
    sem = pl.BlockSpec(memory_space=pltpu.SEMAPHORE)

    def body(*refs):
        in_refs, land_refs = refs[:n], refs[n:2 * n]
        local, remote = _exchange_copies(in_refs, land_refs, refs[2 * n], refs[2 * n + 1], scatter, chip_level)
        for cp in local:
            cp.wait()
        for cp in remote:
            cp.wait_send()
            cp.wait_recv()

    res = pl.pallas_call(
        body, name=name, out_shape=tuple(pltpu.HBM(o.shape, o.dtype) for o in ins + lands),
        in_specs=[hbm] * (2 * n) + [sem, sem, pl.BlockSpec(memory_space=pl.ANY)], out_specs=tuple([hbm] * (2 * n)),
        input_output_aliases={i: i for i in range(2 * n)},
        compiler_params=pltpu.CompilerParams(has_side_effects=pltpu.SideEffectType.DATAFLOW_SIDE_EFFECTING),
    )(*ins, *lands, send_sems, recv_sems, after)
    return list(res[n:])


def _chip_forward(lands, *, name):
    n = len(lands)

    def body(*refs):
        ins, outs, send_sems, recv_sems = refs[:n], refs[n:2 * n], refs[2 * n], refs[2 * n + 1]
        x, y, c = lax.axis_index("x"), lax.axis_index("y"), lax.axis_index("c")
        copies = []
        for k in range(n):
            for j, (dx, dy) in enumerate(((0, 1), (1, 0), (1, 1))):
                held = 4 * ((x + dx) % 2) + 2 * ((y + dy) % 2) + c
                cp = pltpu.make_async_remote_copy(
                    src_ref=ins[k].at[held], dst_ref=outs[k].at[held], send_sem=send_sems.at[k, j],
                    recv_sem=recv_sems.at[k, j], device_id=(x, y, 1 - c), device_id_type=pl.DeviceIdType.MESH)
                cp.start()
                copies.append(cp)
        for cp in copies:
            cp.wait()

    any_spec = pl.BlockSpec(memory_space=pl.ANY)
    return pl.pallas_call(
        body, name=name, in_specs=[any_spec] * n, out_specs=[any_spec] * n,
        out_shape=[jax.ShapeDtypeStruct(a.shape, a.dtype) for a in lands], input_output_aliases={k: k for k in range(n)},
        scratch_shapes=[pltpu.SemaphoreType.DMA((n, 3)), pltpu.SemaphoreType.DMA((n, 3))],
    )(*lands)


def _adam(w, terms, m, v, *, name):
    n_layers, r, c = w.shape
    tr = min(r, ADAM_ROW_TILE)
    assert r % tr == 0 and len(terms) == n_layers
    steps = r // tr

    def body(w_ref, *rest):
        t_refs, (m_ref, v_ref, g_out, d_out, m_out, v_out) = rest[:n_layers], rest[n_layers:]
        for layer, t_ref in enumerate(t_refs):
            @pl.when(pl.program_id(0) == layer)
            def _(t_ref=t_ref):
                g = t_ref[0].astype(F32)
                for s in range(1, t_ref.shape[0]):
                    g = g + t_ref[s].astype(F32)
                m1 = ADAM_B1 * m_ref[...] + (1.0 - ADAM_B1) * g
                v1 = ADAM_B2 * v_ref[...] + (1.0 - ADAM_B2) * jnp.square(g)
                m_hat = m1 / (1.0 - ADAM_B1 ** ADAM_STEP)
                v_hat = v1 / (1.0 - ADAM_B2 ** ADAM_STEP)
                g_out[...] = g
                d_out[...] = -ADAM_LR * (m_hat / (jnp.sqrt(v_hat) + ADAM_EPS) + ADAM_WD * w_ref[...])
                m_out[...] = m1
                v_out[...] = v1

    def term_spec(layer, t):
        return pl.BlockSpec((t.shape[0], tr, c),
                            lambda l, i: (0, jnp.where(l == layer, i, jnp.where(l < layer, 0, steps - 1)), 0))

    spec = pl.BlockSpec((None, tr, c), lambda l, i: (l, i, 0))
    out = jax.ShapeDtypeStruct(w.shape, F32)
    return pl.pallas_call(
        body, name=name, grid=(n_layers, steps),
        in_specs=[spec] + [term_spec(layer, t) for layer, t in enumerate(terms)] + [spec, spec], out_specs=[spec] * 4,
        out_shape=[out] * 4, compiler_params=_params("arbitrary", "arbitrary"),
    )(w, *terms, m, v)


def _sum_terms(terms, *, name):
    n, _, p = terms.shape

    def body(t_ref, o_ref):
        acc = t_ref[0]
        for s in range(1, n):
            acc = acc + t_ref[s]
        o_ref[...] = acc

    return pl.pallas_call(body, name=name, out_shape=jax.ShapeDtypeStruct((1, p), F32))(terms)


def _lb_logits_grad(dlb, logits, *, name):
    def body(dlb_ref, l_ref, o_ref):
        lb = _lower_bound(l_ref[...])
        d0 = dlb_ref[...] * lb * (1.0 - lb)
        o_ref[...] = jnp.concatenate([d0, -d0], axis=0)

    return pl.pallas_call(body, name=name, out_shape=jax.ShapeDtypeStruct(logits.shape, F32))(dlb, logits)


def _silu_grad(z):
    sg = _sigmoid(z)
    return sg * (1.0 + z * (1.0 - sg))


def _head_norm_gate(o, zg, gn):
    outs = []
    for h in range(HGRN_HEADS):
        sl = slice(h * LANES, (h + 1) * LANES)
        zg_h = zg[:, sl]
        outs.append(_rms(o[:, sl], gn) * (zg_h * _sigmoid(zg_h)))
    return (jnp.concatenate(outs, axis=1),)


def _head_norm_gate_bwd(o, zg, dm, gn):
    do_parts, dzg_parts, dgn = [], [], jnp.zeros((1, LANES), F32)
    for h in range(HGRN_HEADS):
        sl = slice(h * LANES, (h + 1) * LANES)
        o_h, zg_h, dm_h = o[:, sl], zg[:, sl], dm[:, sl]
        gate = zg_h * _sigmoid(zg_h)
        do_h, dgn_h = _rms_bwd(o_h, gn, dm_h * gate)
        dgn = dgn + dgn_h
        do_parts.append(do_h)
        dzg_parts.append(dm_h * _rms(o_h, gn) * _silu_grad(zg_h))
    return jnp.concatenate(do_parts, axis=1), jnp.concatenate(dzg_parts, axis=1), dgn


def _rope_slabs(x, t_c, t_s1, t_s2, transpose):
    fn = _rope_t if transpose else _rope
    return jnp.concatenate(
        [fn(x[:, h * LANES:(h + 1) * LANES], t_c, t_s1, t_s2) for h in range(x.shape[1] // LANES)], axis=1)


def _loss_head(h, tgt, w):
    d = h.shape[1]
    r = lax.rsqrt(jnp.mean(h * h, axis=-1, keepdims=True) + EPS)
    xh = h * r
    err = xh * w - tgt
    loss = 0.5 * jnp.sum(jnp.mean(err * err, axis=-1, keepdims=True), axis=0, keepdims=True)
    dy = err / d
    dxh = dy * w
    dh = r * (dxh - xh * jnp.mean(dxh * xh, axis=-1, keepdims=True))
    return dh, dh, jnp.sum(dy * xh, axis=0, keepdims=True), jnp.broadcast_to(loss, (1, LANES))


def _mlp_fwd(h, norm, w_up, w_down, tag, loss_head=None):
    d = h.shape[1]

    def up(x, g, wu):
        x_n = _rms(x, g).astype(BF16)
        return x_n, jnp.concatenate([jnp.square(jnp.maximum(_dot(x_n, wu[j], _NN), 0.0)) for j in range(wu.shape[0])],
                                    axis=1)

    xn, act = _rowcall(up, [h], [norm, w_up], [(d, BF16), (w_up.shape[0] * w_up.shape[2], BF16)], [], tr=512,
                       name=f"{tag}_up")
    if callable(w_down):
        w_down = w_down(act)
    if loss_head is None:
        return _rowcall(lambda a, res, wd: (res + _dot(a, wd, _NN),), [act, h], [w_down], [(d, F32)], [],
                        tr=512, name=f"{tag}_down")[0], (h, xn, act)
    tgt, final_norm = loss_head

    def down_and_loss(a, res, t, wd, g):
        return _loss_head(res + _dot(a, wd, _NN), t, g)

    return _rowcall(down_and_loss, [act, h, tgt], [w_down, final_norm], [(d, F32), (d, BF16)], [d, LANES],
                    name=f"{tag}_down_loss"), (h, xn, act)


def _mlp_bwd(dh_out, dh_out_bf, saved, norm, w_up, w_down, tag, after=None):
    h, xn, act = saved
    d = h.shape[1]
    du = _rowcall(lambda dres, a, wd: (_dot(dres, wd, _NT) * (2.0 * jnp.sqrt(a.astype(F32))),), [dh_out_bf, act],
                  [w_down], [(act.shape[1], BF16)], [], tr=512, after=after, name=f"{tag}_bwd_du")[0]
    dw_down = _mm(act, dh_out_bf, mode="tn", name=f"{tag}_bwd_wdown")
    dw_up = _mm(xn, du, mode="tn", col_shards=w_up.shape[0], name=f"{tag}_bwd_wup")

    def up_norm_bwd(x, d_u, dres, g, wu):
        cols = wu.shape[2]
        dxn = _dot(d_u[:, :cols], wu[0], _NT)
        for j in range(1, wu.shape[0]):
            dxn = dxn + _dot(d_u[:, j * cols:(j + 1) * cols], wu[j], _NT)
        dx, dw = _rms_bwd(x, g, dxn)
        return dx + dres, dx + dres, dw

    dh, dh_bf, dnorm = _rowcall(up_norm_bwd, [h, du, dh_out], [norm, w_up], [(d, F32), (d, BF16)], [d], tr=512,
                                name=f"{tag}_bwd_dxn")
    return dh, dh_bf, dnorm, dw_up, dw_down


def _row_major(g):
    return g.reshape(g.shape[0] * g.shape[1], g.shape[2])


def _col_major(g):
    return jnp.transpose(g, (1, 0, 2)).reshape(g.shape[1], g.shape[0] * g.shape[2])


def _col_terms(dw):
    k, n = dw.shape
    return jnp.transpose(dw.reshape(k, N_DEV, n // N_DEV), (1, 0, 2))


def _row_terms(dw):
    return dw.reshape(N_DEV, dw.shape[0] // N_DEV, dw.shape[1])


def kernel(x, hgrn_norm, hgrn_w_q, hgrn_w_f, hgrn_w_i, hgrn_w_g, hgrn_g_norm, hgrn_w_o, hgrn_lb_logits, mla_norm, mla_w_dq, mla_q_norm, mla_w_uq, mla_w_o, kv_in_norm, kv_w_dkv, kv_norm, kv_w_uk, kv_w_uv, mlp_norm, mlp_w_up, mlp_w_down, final_norm, loss_target, m_hgrn_norm, m_hgrn_w_q, m_hgrn_w_f, m_hgrn_w_i, m_hgrn_w_g, m_hgrn_g_norm, m_hgrn_w_o, m_hgrn_lb_logits, m_mla_norm, m_mla_w_dq, m_mla_q_norm, m_mla_w_uq, m_mla_w_o, m_kv_in_norm, m_kv_w_dkv, m_kv_norm, m_kv_w_uk, m_kv_w_uv, m_mlp_norm, m_mlp_w_up, m_mlp_w_down, m_final_norm, v_hgrn_norm, v_hgrn_w_q, v_hgrn_w_f, v_hgrn_w_i, v_hgrn_w_g, v_hgrn_g_norm, v_hgrn_w_o, v_hgrn_lb_logits, v_mla_norm, v_mla_w_dq, v_mla_q_norm, v_mla_w_uq, v_mla_w_o, v_kv_in_norm, v_kv_w_dkv, v_kv_norm, v_kv_w_uk, v_kv_w_uv, v_mlp_norm, v_mlp_w_up, v_mlp_w_down, v_final_norm):
    given = dict(locals())
    weight_names = ["hgrn_norm", "hgrn_w_q", "hgrn_w_f", "hgrn_w_i", "hgrn_w_g", "hgrn_g_norm", "hgrn_w_o",
                    "hgrn_lb_logits", "mla_norm", "mla_w_dq", "mla_q_norm", "mla_w_uq", "mla_w_o", "kv_in_norm",
                    "kv_w_dkv", "kv_norm", "kv_w_uk", "kv_w_uv", "mlp_norm", "mlp_w_up", "mlp_w_down", "final_norm"]
    me = 4 * lax.axis_index("x") + 2 * lax.axis_index("y") + lax.axis_index("c")
    xs, tgt = x[0], loss_target[0]
    seq, d_model = xs.shape
    n_heads, hd = MLA_HEADS, LANES

    big_local = {
        "hgrn_w_q": hgrn_w_q[0], "hgrn_w_f": hgrn_w_f[0], "hgrn_w_i": hgrn_w_i[0], "hgrn_w_g": hgrn_w_g[0],
        "hgrn_w_o": hgrn_w_o[0], "mla_w_dq": mla_w_dq[0], "mla_w_uq": mla_w_uq[0], "mla_w_o": mla_w_o[0],
        "kv_w_dkv": kv_w_dkv, "kv_w_uk": kv_w_uk, "kv_w_uv": kv_w_uv,
        "mlp_w_up0": mlp_w_up[0], "mlp_w_up1": mlp_w_up[1], "mlp_w_down0": mlp_w_down[0], "mlp_w_down1": mlp_w_down[1],
    }
    big_names = list(big_local)
    col_sharded = {"mla_w_uq", "kv_w_uk", "kv_w_uv"}
    shard_major = {"mlp_w_up0", "mlp_w_up1"}
    vec_local = jnp.concatenate([hgrn_norm, hgrn_lb_logits], axis=0)
    first_names = ["hgrn_w_q", "hgrn_w_f", "hgrn_w_i"]
    proj_names = first_names + ["hgrn_w_g"]
    later_names = {"hgrn_o": ["hgrn_w_g", "hgrn_w_o"], "up0": ["mlp_w_up0"], "down0": ["mlp_w_down0"],
                   "mla": ["kv_w_dkv", "kv_w_uk", "kv_w_uv", "mla_w_dq", "mla_w_uq", "mla_w_o"],
                   "mlp1": ["mlp_w_up1", "mlp_w_down1"]}

    def unshard(names, arrays):
        return {k: (a if k in shard_major else _col_major(a) if k in col_sharded else _row_major(a))
                for k, a in zip(names, arrays)}

    two_level = {"down0", "mla"}
    first_state, token = _exchange_start([big_local[k].astype(BF16) for k in first_names] + [vec_local], scatter=False,
                                         chip_level=True, name="gather_first_start")
    gather_state = {}
    for tag, names in later_names.items():
        gather_state[tag], token = _exchange_start([big_local[k].astype(BF16) for k in names], scatter=False,
                                                   chip_level=tag in two_level, after=token, name=f"gather_{tag}_start")

    def gather_wait(tag, after):
        landed = _exchange_wait(gather_state[tag], after, name=f"gather_{tag}_wait")
        if tag in two_level:
            landed = _chip_forward(landed, name=f"gather_{tag}_forward")
        w.update(unshard(later_names[tag], landed))
        return [w[k] for k in later_names[tag]]

    gathered = _chip_forward(_exchange_wait(first_state, token, name="gather_first_wait"), name="gather_first_forward")
    w = unshard(first_names, gathered[:-1])
    vec_full = jnp.transpose(gathered[-1], (1, 0, 2)).reshape(3, d_model)
    hgrn_norm_full, lb_logits_full = vec_full[0:1], vec_full[1:3]
    t_c, t_s1, t_s2 = _rope_tables(seq)
    kv_lora = kv_w_uk.shape[0]

    def hgrn_proj(a, g, *weights):
        xn = _rms(a, g).astype(BF16)
        return (xn, *[_dot(xn, wt, _NN) for wt in weights])

    xn0, zq, zf, zi = _rowcall(hgrn_proj, [xs], [hgrn_norm_full] + [w[k] for k in first_names],
                               [(d_model, BF16)] + [(d_model, F32)] * 3, [], tr=512, name="hgrn_proj")
    o_rec, states = _hgrn_fwd(zq, zf, zi, lb_logits_full, name="hgrn_fwd")
    gather_wait("hgrn_o", o_rec)

    def gate_out(o, x_n, res, gn, wg, wo):
        z = _dot(x_n, wg, _NN)
        m = _head_norm_gate(o, z, gn)[0].astype(BF16)
        return z, m, res + _dot(m, wo, _NN)

    zg, mixed, h1 = _rowcall(gate_out, [o_rec, xn0, xs], [hgrn_g_norm, w["hgrn_w_g"], w["hgrn_w_o"]],
                             [(d_model, F32), (d_model, BF16), (d_model, F32)], [], name="hgrn_gate_out")
    h2, mlp0_saved = _mlp_fwd(h1, mlp_norm[0:1], gather_wait("up0", h1)[0], lambda act: gather_wait("down0", act)[0],
                              "mlp0")
    gather_wait("mla", h2)
    w_uq3 = w["mla_w_uq"].reshape(-1, n_heads, MLA_NOPE + MLA_ROPE)
    w_uq_nope = w_uq3[:, :, :MLA_NOPE].reshape(-1, n_heads * hd)
    w_uq_rope = jnp.pad(w_uq3[:, :, MLA_NOPE:], ((0, 0), (0, 0), (0, hd - MLA_ROPE))).reshape(-1, n_heads * hd)
    w_dkv_pad = jnp.pad(w["kv_w_dkv"], ((0, 0), (0, kv_lora + hd - w["kv_w_dkv"].shape[1])))

    q_lora, qk_cols = w["mla_w_dq"].shape[1], n_heads * hd

    def mla_qkv(a, tc, ts1, ts2, g_kv_in, g_mla, g_q, g_kv, wdq, wn, wr, wdkv, wuk, wuv):
        h_n, x_n = _rms(a, g_kv_in).astype(BF16), _rms(a, g_mla).astype(BF16)
        cq = _dot(x_n, wdq, _NN)
        cq_n = _rms(cq, g_q).astype(BF16)
        q_nope = _dot(cq_n, wn, _NN) * Q_PRESCALE
        q_rope = _rope_slabs(_dot(cq_n, wr, _NN) * Q_PRESCALE, tc, ts1, ts2, False)
        c_all = _dot(h_n, wdkv, _NN)
        lat = _rms(c_all[:, :kv_lora], g_kv).astype(BF16)
        return (h_n, x_n, cq, cq_n, q_nope, q_rope, c_all, lat, _rope(c_all[:, kv_lora:], tc, ts1, ts2),
                _dot(lat, wuk, _NN), _dot(lat, wuv, _NN))

    hn, xn2, cq_pre, c_q, qn, qr, ckr, c_kv, kr, kn, vv = _rowcall(
        mla_qkv, [h2, t_c, t_s1, t_s2],
        [kv_in_norm[None, :], mla_norm, mla_q_norm, kv_norm[None, :], w["mla_w_dq"], w_uq_nope, w_uq_rope, w_dkv_pad,
         w["kv_w_uk"], w["kv_w_uv"]],
        [(d_model, BF16), (d_model, BF16), (q_lora, F32), (q_lora, BF16), (qk_cols, BF16), (qk_cols, BF16),
         (kv_lora + hd, F32), (kv_lora, BF16), (hd, BF16), (qk_cols, BF16), (qk_cols, BF16)], [], tr=512, name="mla_qkv")
    o_att, lse = _attn_fwd(qn, qr, kn, kr, vv, name="attn_fwd")
    h3 = _rowcall(lambda o, res, wo: (res + _dot(o, wo, _NN),), [o_att, h2], [w["mla_w_o"]], [(d_model, F32)], [],
                  tr=512, name="attn_out")[0]
    gather_wait("mlp1", h3)
    (dh4, dh4_bf, g_final_norm, loss_part), mlp1_saved = _mlp_fwd(
        h3, mlp_norm[1:2], w["mlp_w_up1"], w["mlp_w_down1"], "mlp1", loss_head=(tgt, final_norm[None, :]))

    g = {}
    groups = {"mlp1": ["mlp_w_up1", "mlp_w_down1"],
              "mla": ["mla_w_o", "mla_w_uq", "mla_w_dq", "kv_w_uk", "kv_w_uv", "kv_w_dkv"],
              "mlp0": ["mlp_w_up0", "mlp_w_down0"],
              "hgrn_out": ["hgrn_w_o", "hgrn_w_g"],
              "hgrn_in": ["hgrn_w_q", "hgrn_w_f", "hgrn_w_i"]}
    scatter_state = {}

    def scatter_start(tag, after=None):
        scatter_state[tag], tok = _exchange_start(
            [g[k] if k in shard_major else (_col_terms if k in col_sharded else _row_terms)(g[k]) for k in groups[tag]],
            scatter=True, after=after,
            name=f"scatter_{tag}_start")
        return tok

    dh3, dh3_bf, g_mlp_norm1, g["mlp_w_up1"], g["mlp_w_down1"] = _mlp_bwd(
        dh4, dh4_bf, mlp1_saved, mlp_norm[1:2], w["mlp_w_up1"], w["mlp_w_down1"], "mlp1")
    def attn_out_bwd(dres, o, wo):
        d_o = _dot(dres, wo, _NT).astype(BF16)
        prod = d_o.astype(F32) * o.astype(F32)
        return d_o, jnp.concatenate([jnp.broadcast_to(jnp.sum(prod[:, h * hd:(h + 1) * hd], axis=1, keepdims=True),
                                                      (prod.shape[0], hd)) for h in range(n_heads)], axis=1)

    d_oatt, delta = _rowcall(attn_out_bwd, [dh3_bf, o_att], [w["mla_w_o"]], [(qk_cols, BF16), (qk_cols, F32)], [],
                             after=scatter_start("mlp1"), name="attn_out_bwd_x")
    g["mla_w_o"] = _mm(o_att, dh3_bf, mode="tn", name="attn_out_bwd_w")
    dqn, dqr, dkn, dvv, dkr = _attn_bwd(qn, qr, kn, kr, vv, d_oatt, lse, delta, name="attn_bwd")

    def q_path_bwd(cq, cq_n, x_n, d_qn, d_qr, tc, ts1, ts2, g_q, wdq, wn, wr):
        d_qn, d_qr = d_qn.astype(BF16), _rope_slabs(d_qr, tc, ts1, ts2, True).astype(BF16)
        d_cq, d_gq = _rms_bwd(cq, g_q, _dot(d_qn, wn, _NT) + _dot(d_qr, wr, _NT))
        d_cq = d_cq.astype(BF16)
        return _dot(d_cq, wdq, _NT), d_gq, _dot(x_n, d_cq, _TN), _dot(cq_n, d_qn, _TN), _dot(cq_n, d_qr, _TN)

    dxn2, g_q_norm, g_dq, g_uq_nope, g_uq_rope = _rowcall(
        q_path_bwd, [cq_pre, c_q, xn2, dqn, dqr, t_c, t_s1, t_s2], [mla_q_norm, w["mla_w_dq"], w_uq_nope, w_uq_rope],
        [(d_model, F32)], [q_lora, (d_model, q_lora), (q_lora, qk_cols), (q_lora, qk_cols)], tr=512, name="mla_q_bwd")
    g["mla_w_dq"] = g_dq.astype(GRAD_WIRE_DTYPE)
    g["mla_w_uq"] = jnp.concatenate([g_uq_nope.reshape(q_lora, n_heads, hd),
                                     g_uq_rope.reshape(q_lora, n_heads, hd)[:, :, :MLA_ROPE]],
                                    axis=2).reshape(q_lora, -1).astype(GRAD_WIRE_DTYPE)

    def kv_path_bwd(c_all, lat, h_n, d_kn, d_v, d_kr_heads, tc, ts1, ts2, a, d_xn2, dres,
                    g_kv, g_kv_in, g_mla, wdkv, wuk, wuv):
        d_lat, d_gkv = _rms_bwd(c_all[:, :kv_lora], g_kv, _dot(d_kn, wuk, _NT) + _dot(d_v, wuv, _NT))
        d_kr = d_kr_heads[:, :hd]
        for h in range(1, n_heads):
            d_kr = d_kr + d_kr_heads[:, h * hd:(h + 1) * hd]
        d_all = jnp.concatenate([d_lat, _rope_t(d_kr, tc, ts1, ts2)], axis=1).astype(BF16)
        dx1, d_gkv_in = _rms_bwd(a, g_kv_in, _dot(d_all, wdkv, _NT))
        dx2, d_gmla = _rms_bwd(a, g_mla, d_xn2)
        d_a = dx1 + dx2 + dres
        return (d_a, d_a, d_gkv, d_gkv_in, d_gmla, _dot(h_n, d_all, _TN), _dot(lat, d_kn, _TN), _dot(lat, d_v, _TN))

    dh2, dh2_bf, g_kv_norm, g_kv_in_norm, g_mla_norm, g_dkv, g_uk, g_uv = _rowcall(
        kv_path_bwd, [ckr, c_kv, hn, dkn, dvv, dkr, t_c, t_s1, t_s2, h2, dxn2, dh3],
        [kv_norm[None, :], kv_in_norm[None, :], mla_norm, w_dkv_pad, w["kv_w_uk"], w["kv_w_uv"]],
        [(d_model, F32), (d_model, BF16)],
        [kv_lora, d_model, d_model, (d_model, kv_lora + hd), (kv_lora, qk_cols), (kv_lora, qk_cols)], name="mla_kv_bwd")
    g["kv_w_dkv"] = g_dkv[:, :kv_w_dkv.shape[1]].astype(GRAD_WIRE_DTYPE)
    g["kv_w_uk"], g["kv_w_uv"] = g_uk.astype(GRAD_WIRE_DTYPE), g_uv.astype(GRAD_WIRE_DTYPE)
    dh1, dh1_bf, g_mlp_norm0, g["mlp_w_up0"], g["mlp_w_down0"] = _mlp_bwd(
        dh2, dh2_bf, mlp0_saved, mlp_norm[0:1], w["mlp_w_up0"], w["mlp_w_down0"], "mlp0", after=scatter_start("mla"))

    g["hgrn_w_o"] = _mm(mixed, dh1_bf, mode="tn", after=scatter_start("mlp0"), name="hgrn_out_bwd_w")
    do_rec, dzg, g_g_norm = _rowcall(
        lambda dres, o, z, wo, gn: _head_norm_gate_bwd(o, z, _dot(dres, wo, _NT), gn), [dh1_bf, o_rec, zg],
        [w["hgrn_w_o"], hgrn_g_norm], [(d_model, F32), (d_model, BF16)], [hd], name="hgrn_gate_out_bwd")
    g["hgrn_w_g"] = _mm(xn0, dzg, mode="tn", name="hgrn_w_g_bwd_w")
    dzq, dzf, dzi, g_lb = _hgrn_bwd(zq, zf, zi, lb_logits_full, states, do_rec, scatter_start("hgrn_out"),
                                    name="hgrn_bwd")
    for nm, dz in (("hgrn_w_q", dzq), ("hgrn_w_f", dzf), ("hgrn_w_i", dzi)):
        g[nm] = _mm(xn0, dz, mode="tn", name=f"{nm}_bwd_w")
    last = scatter_start("hgrn_in")

    def hgrn_proj_bwd(a, dres, *rest):
        dzs, gw, weights = rest[:4], rest[4], rest[5:]
        dxn = _dot(dzs[0], weights[0], _NT)
        for dz, wt in zip(dzs[1:], weights[1:]):
            dxn = dxn + _dot(dz, wt, _NT)
        dx, dw = _rms_bwd(a, gw, dxn)
        return dx + dres, dw

    grad_x, g_hgrn_norm = _rowcall(hgrn_proj_bwd, [xs, dh1, dzq, dzf, dzi, dzg],
                                   [hgrn_norm_full] + [w[k] for k in proj_names], [(d_model, F32)], [d_model],
                                   tr=512, after=last, name="hgrn_proj_bwd")

    small_parts = [g_hgrn_norm, g_lb, g_g_norm, g_mla_norm, g_q_norm, g_kv_in_norm, g_kv_norm, g_mlp_norm0,
                   g_mlp_norm1, g_final_norm, loss_part]
    small_sizes = [p.shape[1] for p in small_parts]
    small_terms = _exchange([jnp.concatenate(small_parts, axis=1)], scatter=False, name="gather_small")[0]
    small_sum = _sum_terms(small_terms, name="sum_small")
    offs = [0]
    for sz in small_sizes:
        offs.append(offs[-1] + sz)
    (s_hgrn_norm, s_lb, s_g_norm, s_mla_norm, s_q_norm, s_kv_in_norm, s_kv_norm, s_mlp_norm0, s_mlp_norm1, s_final_norm,
     s_loss) = [small_sum[:, a:b] for a, b in zip(offs[:-1], offs[1:])]
    shard = hgrn_norm.shape[1]
    g_lb_logits = _lb_logits_grad(lax.dynamic_slice_in_dim(s_lb, me * shard, shard, axis=1), hgrn_lb_logits,
                                  name="lb_logits_grad")
    loss = s_loss[0, 0]

    res, layer_terms = {}, {}

    def update(k, term_list):
        shape = given[k].shape
        as_layers = (len(term_list), shape[-2], shape[-1])
        upd = _adam(given[k].reshape(as_layers), term_list, given["m_" + k].reshape(as_layers),
                    given["v_" + k].reshape(as_layers), name=f"adam_{k}")
        res[k] = [o.reshape(shape) for o in upd]
        return upd[0]

    for tag, names in groups.items():
        for k, t in zip(names, _exchange_wait(scatter_state[tag], last, name=f"scatter_{tag}_wait")):
            if k.startswith("mlp_w_"):
                layer_terms.setdefault(k[:-1], {})[int(k[-1])] = t
                if len(layer_terms[k[:-1]]) == 2:
                    last = update(k[:-1], [layer_terms[k[:-1]][0], layer_terms[k[:-1]][1]])
            else:
                last = update(k, [t])

    small_grads = {
        "hgrn_norm": lax.dynamic_slice_in_dim(s_hgrn_norm, me * shard, shard, axis=1),
        "hgrn_g_norm": s_g_norm, "hgrn_lb_logits": g_lb_logits, "mla_norm": s_mla_norm, "mla_q_norm": s_q_norm,
        "kv_in_norm": s_kv_in_norm, "kv_norm": s_kv_norm,
        "mlp_norm": jnp.concatenate([s_mlp_norm0, s_mlp_norm1], axis=0), "final_norm": s_final_norm,
    }
    small_names = list(small_grads)

    def flat(a):
        return a.reshape(1, -1)

    packed = [jnp.concatenate([flat(src[pre + k]) for k in small_names], axis=1)
              for src, pre in ((given, ""), (small_grads, ""), (given, "m_"), (given, "v_"))]
    small_out = _adam(packed[0][None], [packed[1][None]], packed[2][None], packed[3][None], name="adam_small")
    off = 0
    for k in small_names:
        size = given[k].size
        res[k] = [o[0, :, off:off + size].reshape(given[k].shape) for o in small_out]
        off += size

    outs = [loss, grad_x[None]]
    for i in range(4):
        outs += [res[k][i] for k in weight_names]
    return tuple(outs)
```

```python
import functools

import jax
import jax.numpy as jnp
from jax import lax
from jax.experimental import pallas as pl
from jax.experimental.pallas import tpu as pltpu

F32 = jnp.float32
BF16 = jnp.bfloat16

EPS = 1e-6
LANES = 128
N_DEV = 8
V7X_VMEM_LIMIT_BYTES = 56 << 20
MM_PIPELINE_BYTES = 30 << 20
MM_ROW_TILE = 512
ADAM_ROW_TILE = 256
GRAD_WIRE_DTYPE = BF16

HGRN_HEADS = 8
HGRN_CHUNK = 64
HGRN_SUB = 16
HGRN_HEADS_PER_STEP = 8
HGRN_CHUNKS_PER_STEP = 4
EXP_CLAMP = 80.0
MLA_HEADS = 16
MLA_NOPE = 128
MLA_ROPE = 64
ROPE_THETA = 10000.0
ATTN_SCALE = (MLA_NOPE + MLA_ROPE) ** -0.5

ADAM_LR = 0.001
ADAM_B1 = 0.9
ADAM_B2 = 0.999
ADAM_EPS = 1e-08
ADAM_WD = 0.01
ADAM_STEP = 10

_NN = ((1,), (0,))
_NT = ((1,), (1,))
_TN = ((0,), (0,))


def _params(*sem):
    return pltpu.CompilerParams(dimension_semantics=sem, vmem_limit_bytes=V7X_VMEM_LIMIT_BYTES)


def _dot(a, b, dims):
    return lax.dot_general(a.astype(BF16), b.astype(BF16), (dims, ((), ())), preferred_element_type=F32)


def _dot_f32(a, b, dims=_NN):
    return lax.dot_general(a, b, (dims, ((), ())), precision=lax.Precision.HIGH, preferred_element_type=F32)


def _sigmoid(x):
    return 1.0 / (1.0 + jnp.exp(-x))


def _rms(x, w):
    r = lax.rsqrt(jnp.mean(x * x, axis=-1, keepdims=True) + EPS)
    return x * r * w


def _rms_bwd(x, w, dy):
    r = lax.rsqrt(jnp.mean(x * x, axis=-1, keepdims=True) + EPS)
    xh = x * r
    dw = jnp.sum(dy * xh, axis=0, keepdims=True)
    dxh = dy * w
    dx = r * (dxh - xh * jnp.mean(dxh * xh, axis=-1, keepdims=True))
    return dx, dw


def _mm_tiles(m, n, k, a_bytes, b_bytes, out_tile_bytes):
    tm = min(m, MM_ROW_TILE)
    for tn in (n, 2048, 1024, 512, 256, LANES):
        if tn <= n and n % tn == 0:
            if 2 * (tm * k * a_bytes + k * tn * b_bytes + tm * tn * out_tile_bytes) <= MM_PIPELINE_BYTES:
                return tm, tn
    return tm, min(n, LANES)


def _mm(a, b, *, mode, name, out_dtype=None, after=None, col_shards=None):
    if mode == "nn":
        (m, k), (k2, n) = a.shape, b.shape
    elif mode == "nt":
        (m, k), (n, k2) = a.shape, b.shape
    else:
        (k, m), (k2, n) = a.shape, b.shape
    assert k == k2, (name, a.shape, b.shape)
    if out_dtype is None:
        out_dtype = GRAD_WIRE_DTYPE if mode == "tn" else F32
    tm, tn = _mm_tiles(m, n, k, a.dtype.itemsize, b.dtype.itemsize, jnp.dtype(out_dtype).itemsize)
    if col_shards is not None:
        tn = n // col_shards
    assert m % tm == 0 and n % tn == 0, (name, m, n)
    dims = {"nn": _NN, "nt": _NT, "tn": _TN}[mode]
    a_spec = pl.BlockSpec((k, tm), lambda i, j: (0, i)) if mode == "tn" else pl.BlockSpec((tm, k), lambda i, j: (i, 0))
    b_spec = pl.BlockSpec((tn, k), lambda i, j: (j, 0)) if mode == "nt" else pl.BlockSpec((k, tn), lambda i, j: (0, j))
    o_spec = pl.BlockSpec((tm, tn), lambda i, j: (i, j))
    operands, in_specs = [a, b], [a_spec, b_spec]
    if after is not None:
        operands.append(after)
        in_specs.append(pl.BlockSpec(memory_space=pl.ANY))
    out_shape = jax.ShapeDtypeStruct((m, n), out_dtype)
    if col_shards is not None:
        out_shape = jax.ShapeDtypeStruct((col_shards, m, tn), out_dtype)
        o_spec = pl.BlockSpec((None, tm, tn), lambda i, j: (j, i, 0))

    def body(*refs):
        refs[-1][...] = _dot(refs[0][...], refs[1][...], dims).astype(out_dtype)

    return pl.pallas_call(
        body, name=name, grid=(m // tm, n // tn), in_specs=in_specs, out_specs=o_spec, out_shape=out_shape,
        compiler_params=_params("parallel", "parallel"),
    )(*operands)


def _rowcall(fn, rows, consts, outs, accs, *, name, tr=256, after=None):
    s = rows[0].shape[0]
    tr = min(tr, s)
    assert s % tr == 0
    n_out = len(outs)
    accs = [(1, a) if isinstance(a, int) else a for a in accs]
    in_specs = [pl.BlockSpec((tr, r.shape[1]), lambda i: (i, 0)) for r in rows]
    in_specs += [pl.BlockSpec(c.shape, lambda i, nd=c.ndim: (0,) * nd) for c in consts]
    out_shape = [jax.ShapeDtypeStruct((s, w), dt) for w, dt in outs] + [jax.ShapeDtypeStruct(a, F32) for a in accs]
    out_specs = [pl.BlockSpec((tr, w), lambda i: (i, 0)) for w, _ in outs] + [pl.BlockSpec(a, lambda i: (0, 0)) for a in accs]
    n_in = len(rows) + len(consts)

    def body(*refs):
        res = fn(*[r[...] for r in refs[:n_in]])
        out_refs = refs[n_in + (after is not None):]
        for ref, val in zip(out_refs[:n_out], res[:n_out]):
            ref[...] = val.astype(ref.dtype)
        i = pl.program_id(0)
        for ref, val in zip(out_refs[n_out:], res[n_out:]):
            @pl.when(i == 0)
            def _(ref=ref, val=val):
                ref[...] = val

            @pl.when(i > 0)
            def _(ref=ref, val=val):
                ref[...] += val

    behind = [] if after is None else [after]
    return pl.pallas_call(
        body, name=name, grid=(s // tr,), in_specs=in_specs + [pl.BlockSpec(memory_space=pl.ANY)] * len(behind),
        out_specs=out_specs, out_shape=out_shape, compiler_params=_params("arbitrary" if accs else "parallel"),
    )(*rows, *consts, *behind)


def _rope_tables(seq):
    half = MLA_ROPE // 2
    inv_freq = ROPE_THETA ** (-jnp.arange(half, dtype=F32) / half)
    ang = jnp.arange(seq, dtype=F32)[:, None] * inv_freq[None, :]
    cos, sin, zero = jnp.cos(ang), jnp.sin(ang), jnp.zeros((seq, half), F32)
    t_c = jnp.concatenate([cos, cos, zero, zero], axis=1)
    t_s1 = jnp.concatenate([-sin, zero, zero, zero], axis=1)
    t_s2 = jnp.concatenate([zero, sin, zero, zero], axis=1)
    return t_c, t_s1, t_s2


def _rope(slab, t_c, t_s1, t_s2):
    return slab * t_c + pltpu.roll(slab, 96, 1) * t_s1 + pltpu.roll(slab, 32, 1) * t_s2


def _rope_t(d, t_c, t_s1, t_s2):
    return d * t_c + pltpu.roll(d * t_s1, 32, 1) + pltpu.roll(d * t_s2, 96, 1)


def _lower_bound(logits):
    l0, l1 = logits[0:1, :], logits[1:2, :]
    mx = jnp.maximum(l0, l1)
    e0, e1 = jnp.exp(l0 - mx), jnp.exp(l1 - mx)
    return e0 / (e0 + e1)


def _tri(n, lower):
    row = lax.broadcasted_iota(jnp.int32, (n, n), 0)
    col = lax.broadcasted_iota(jnp.int32, (n, n), 1)
    return (row >= col) if lower else (row <= col)


def _hgrn_fwd(zq, zf, zi, lb_logits, *, name):
    s, d = zq.shape
    h_n, c, hp, cps = d // LANES, HGRN_CHUNK, HGRN_HEADS_PER_STEP, HGRN_CHUNKS_PER_STEP
    nc = s // c

    def body(zq_ref, zf_ref, zi_ref, lb_ref, o_ref, st_ref, state_sc, b_sc):
        @pl.when(pl.program_id(1) == 0)
        def _():
            state_sc[...] = jnp.zeros_like(state_sc)

        lower = _tri(c, True)
        lower_f = lower.astype(F32)
        hs, pairs = range(hp), [(cc, hh) for cc in range(cps) for hh in range(hp)]
        sls = [slice(hh * LANES, (hh + 1) * LANES) for hh in hs]
        rws = [slice(cc * c, (cc + 1) * c) for cc in range(cps)]
        lb = [_lower_bound(lb_ref[:, sl]) for sl in sls]
        zq_v = {p: zq_ref[rws[p[0]], sls[p[1]]] for p in pairs}
        q = {p: zq_v[p] * _sigmoid(zq_v[p]) for p in pairs}
        f = {p: lb[p[1]] + (1.0 - lb[p[1]]) * _sigmoid(zf_ref[rws[p[0]], sls[p[1]]]) for p in pairs}
        k = {p: 1.0 - f[p] for p in pairs}
        v = {p: zi_ref[rws[p[0]], sls[p[1]]] for p in pairs}
        b = {p: _dot_f32(lower_f, jnp.log(f[p])) for p in pairs}
        for p in pairs:
            b_sc[p[0], p[1]] = b[p]
        qe = {p: q[p] * jnp.exp(b[p]) for p in pairs}
        scores = {p: [] for p in pairs}
        for i in range(c // HGRN_SUB):
            lo = i * HGRN_SUB
            for p in pairs:
                ref = b_sc[p[0], p[1], lo - 1:lo, :] if i > 0 else jnp.zeros((1, LANES), F32)
                qt = q[p][lo:lo + HGRN_SUB, :] * jnp.exp(b[p][lo:lo + HGRN_SUB, :] - ref)
                dec = jnp.exp(jnp.minimum(ref - b[p], EXP_CLAMP))
                scores[p].append(_dot(qt, k[p] * dec, _NT))
        o_intra = {p: _dot(jnp.where(lower, jnp.concatenate(scores[p], axis=0), 0.0), v[p], _NN) for p in pairs}
        bl = {p: b_sc[p[0], p[1], c - 1:c, :] for p in pairs}
        k_end = {p: k[p] * jnp.exp(bl[p] - b[p]) for p in pairs}
        state = [state_sc[hh] for hh in hs]
        for cc in range(cps):
            for hh in hs:
                st_ref[hh, cc] = state[hh]
                o_ref[rws[cc], sls[hh]] = _dot(qe[cc, hh], state[hh], _NT) + o_intra[cc, hh]
            state = [state[hh] * jnp.exp(bl[cc, hh]) + _dot(v[cc, hh], k_end[cc, hh], _TN) for hh in hs]
        for hh in hs:
            state_sc[hh] = state[hh]

    tile = pl.BlockSpec((cps * c, hp * LANES), lambda h, i: (i, h))
    return pl.pallas_call(
        body, name=name, grid=(h_n // hp, nc // cps),
        in_specs=[tile, tile, tile, pl.BlockSpec((2, hp * LANES), lambda h, i: (0, h))],
        out_specs=[tile, pl.BlockSpec((hp, cps, LANES, LANES), lambda h, i: (h, i, 0, 0))],
        out_shape=[jax.ShapeDtypeStruct((s, d), F32), jax.ShapeDtypeStruct((h_n, nc, LANES, LANES), F32)],
        scratch_shapes=[pltpu.VMEM((hp, LANES, LANES), F32), pltpu.VMEM((cps, hp, c, LANES), F32)],
        compiler_params=_params("parallel", "arbitrary"),
    )(zq, zf, zi, lb_logits)


def _hgrn_bwd(zq, zf, zi, lb_logits, states, do, after, *, name):
    s, d = zq.shape
    h_n, c, hp, cps = d // LANES, HGRN_CHUNK, HGRN_HEADS_PER_STEP, HGRN_CHUNKS_PER_STEP
    nc = s // c
    n_steps = nc // cps

    def body(zq_ref, zf_ref, zi_ref, lb_ref, st_ref, do_ref, _, dzq_ref, dzf_ref, dzi_ref, dlb_ref, dstate_sc, b_sc):
        @pl.when(pl.program_id(1) == 0)
        def _():
            dstate_sc[...] = jnp.zeros_like(dstate_sc)
            dlb_ref[...] = jnp.zeros_like(dlb_ref)

        lower, upper = _tri(c, True), _tri(c, False).astype(F32)
        lower_f = lower.astype(F32)
        last_row = lax.broadcasted_iota(jnp.int32, (c, LANES), 0) == c - 1
        hs, pairs = range(hp), [(cc, hh) for cc in range(cps) for hh in range(hp)]
        sls = [slice(hh * LANES, (hh + 1) * LANES) for hh in hs]
        rws = [slice(cc * c, (cc + 1) * c) for cc in range(cps)]
        lb = [_lower_bound(lb_ref[:, sl]) for sl in sls]
        zq_v = {p: zq_ref[rws[p[0]], sls[p[1]]] for p in pairs}
        sq = {p: _sigmoid(zq_v[p]) for p in pairs}
        q = {p: zq_v[p] * sq[p] for p in pairs}
        sf = {p: _sigmoid(zf_ref[rws[p[0]], sls[p[1]]]) for p in pairs}
        f = {p: lb[p[1]] + (1.0 - lb[p[1]]) * sf[p] for p in pairs}
        k = {p: 1.0 - f[p] for p in pairs}
        v = {p: zi_ref[rws[p[0]], sls[p[1]]] for p in pairs}
        d_o = {p: do_ref[rws[p[0]], sls[p[1]]] for p in pairs}
        b = {p: _dot_f32(lower_f, jnp.log(f[p])) for p in pairs}
        s0t = {p: st_ref[p[1], p[0]] for p in pairs}
        for p in pairs:
            b_sc[p[0], p[1]] = b[p]
        bl = {p: b_sc[p[0], p[1], c - 1:c, :] for p in pairs}
        eb = {p: jnp.exp(b[p]) for p in pairs}
        ebl = {p: jnp.exp(bl[p]) for p in pairs}
        dec_end = {p: jnp.exp(bl[p] - b[p]) for p in pairs}
        da = {p: jnp.where(lower, _dot(d_o[p], v[p], _NT), 0.0) for p in pairs}
        dq = {p: _dot(d_o[p], s0t[p], _NN) * eb[p] for p in pairs}
        dstate_in = {p: _dot(d_o[p], q[p] * eb[p], _TN) for p in pairs}
        dk_intra = {p: jnp.zeros((c, LANES), F32) for p in pairs}
        scores, dq_blocks = {p: [] for p in pairs}, {p: [] for p in pairs}
        for i in range(c // HGRN_SUB):
            lo = i * HGRN_SUB
            for p in pairs:
                ref = b_sc[p[0], p[1], lo - 1:lo, :] if i > 0 else jnp.zeros((1, LANES), F32)
                grow = jnp.exp(b[p][lo:lo + HGRN_SUB, :] - ref)
                qt = q[p][lo:lo + HGRN_SUB, :] * grow
                dec = jnp.exp(jnp.minimum(ref - b[p], EXP_CLAMP))
                kd = k[p] * dec
                scores[p].append(_dot(qt, kd, _NT))
                da_i = da[p][lo:lo + HGRN_SUB, :]
                dq_blocks[p].append(_dot_f32(da_i, kd, _NN) * grow)
                dk_intra[p] = dk_intra[p] + _dot_f32(da_i, qt, _TN) * dec
        dv_intra = {p: _dot(jnp.where(lower, jnp.concatenate(scores[p], axis=0), 0.0), d_o[p], _TN) for p in pairs}
        dq = {p: dq[p] + jnp.concatenate(dq_blocks[p], axis=0) for p in pairs}
        q_dq = {p: q[p] * dq[p] for p in pairs}
        for p in pairs:
            dzq_ref[rws[p[0]], sls[p[1]]] = (dq[p] * sq[p] * (1.0 + zq_v[p] * (1.0 - sq[p]))).astype(BF16)
        dstate = [dstate_sc[hh] for hh in hs]
        for cc in reversed(range(cps)):
            ps = [(cc, hh) for hh in hs]
            dk_state = [_dot(v[p], dstate[p[1]], _NN) * dec_end[p] for p in ps]
            dv = [dv_intra[p] + _dot(k[p] * dec_end[p], dstate[p[1]], _NT) for p in ps]
            dk = [dk_intra[p] + dk_state[p[1]] for p in ps]
            db_last = [jnp.sum(k[p] * dk_state[p[1]], axis=0, keepdims=True)
                       + ebl[p] * jnp.sum(s0t[p] * dstate[p[1]], axis=0, keepdims=True) for p in ps]
            db = [q_dq[p] - k[p] * dk[p[1]] + jnp.where(last_row, db_last[p[1]], 0.0) for p in ps]
            df = [_dot_f32(upper, db[p[1]]) / f[p] - dk[p[1]] for p in ps]
            for p in ps:
                hh = p[1]
                dzf_ref[rws[cc], sls[hh]] = (df[hh] * (1.0 - lb[hh]) * sf[p] * (1.0 - sf[p])).astype(BF16)
                dlb_ref[:, sls[hh]] += jnp.sum(df[hh] * (1.0 - sf[p]), axis=0, keepdims=True)
                dzi_ref[rws[cc], sls[hh]] = dv[hh].astype(BF16)
            dstate = [dstate[p[1]] * ebl[p] + dstate_in[p] for p in ps]
        for hh in hs:
            dstate_sc[hh] = dstate[hh]

    tile = pl.BlockSpec((cps * c, hp * LANES), lambda h, i: (n_steps - 1 - i, h))
    out = jax.ShapeDtypeStruct((s, d), BF16)
    return pl.pallas_call(
        body, name=name, grid=(h_n // hp, n_steps),
        in_specs=[tile, tile, tile, pl.BlockSpec((2, hp * LANES), lambda h, i: (0, h)),
                  pl.BlockSpec((hp, cps, LANES, LANES), lambda h, i: (h, n_steps - 1 - i, 0, 0)), tile,
                  pl.BlockSpec(memory_space=pl.ANY)],
        out_specs=[tile, tile, tile, pl.BlockSpec((1, hp * LANES), lambda h, i: (0, h))],
        out_shape=[out, out, out, jax.ShapeDtypeStruct((1, d), F32)],
        scratch_shapes=[pltpu.VMEM((hp, LANES, LANES), F32), pltpu.VMEM((cps, hp, c, LANES), F32)],
        compiler_params=_params("parallel", "arbitrary"),
    )(zq, zf, zi, lb_logits, states, do, after)


ATTN_SUB_ROWS = 256
LOG2E = 1.4426950408889634
LN2 = 0.6931471805599453
Q_PRESCALE = ATTN_SCALE * LOG2E


def _attn_tile(s):
    return min(2048, max(128, s // 2))


def _causal_pairs(n, q_major):
    pairs = [(i, j) for i in range(n) for j in range(i + 1)] if q_major else [(i, j) for j in range(n) for i in range(j, n)]
    return jnp.asarray([p[0] for p in pairs], jnp.int32), jnp.asarray([p[1] for p in pairs], jnp.int32)


def _sub_scores(qn_ref, qr_ref, k, r, sub, t, diagonal):
    q = jnp.concatenate([qn_ref[r:r + sub, :], qr_ref[r:r + sub, :]], axis=1)
    if not diagonal:
        return q, _dot(q, k, _NT)
    cols = r + sub
    keep = lax.broadcasted_iota(jnp.int32, (sub, cols), 1) <= r + lax.broadcasted_iota(jnp.int32, (sub, cols), 0)
    return q, jnp.where(keep, _dot(q, k[:cols], _NT), -jnp.inf)


def _attn_fwd(qn, qr, kn, kr, v, *, name):
    s, t = qn.shape[0], _attn_tile(qn.shape[0])
    sub = min(t, ATTN_SUB_ROWS)
    q_blk, k_blk = _causal_pairs(s // t, True)

    def body(qi_ref, kj_ref, qn_ref, qr_ref, kn_ref, kr_ref, v_ref, o_ref, lse_ref, m_sc, l_sc, acc_sc):
        p_id = pl.program_id(1)
        i, j = qi_ref[p_id], kj_ref[p_id]

        @pl.when(j == 0)
        def _():
            m_sc[...] = jnp.full_like(m_sc, -jnp.inf)
            l_sc[...] = jnp.zeros_like(l_sc)
            acc_sc[...] = jnp.zeros_like(acc_sc)

        def update(diagonal):
            k = jnp.concatenate([kn_ref[...], kr_ref[...]], axis=1)
            v = v_ref[...]
            starts = list(range(0, t, sub))
            scs = [_sub_scores(qn_ref, qr_ref, k, r, sub, t, diagonal)[1] for r in starts]
            ps, alphas = [], []
            for r, sc in zip(starts, scs):
                m_prev = m_sc[r:r + sub, :]
                m_new = jnp.maximum(m_prev, jnp.max(sc, axis=1, keepdims=True))
                alpha = jnp.exp2(m_prev - m_new)
                p = jnp.exp2(sc - m_new[:, :1])
                l_sc[r:r + sub, :] = alpha * l_sc[r:r + sub, :] + jnp.sum(p, axis=1, keepdims=True)
                m_sc[r:r + sub, :] = m_new
                ps.append(p)
                alphas.append(alpha)
            for r, p, alpha in zip(starts, ps, alphas):
                acc_sc[r:r + sub, :] = alpha * acc_sc[r:r + sub, :] + _dot(p, v[:p.shape[1]], _NN)

        @pl.when(j < i)
        def _():
            update(False)

        @pl.when(j == i)
        def _():
            update(True)
            o_ref[...] = (acc_sc[...] / l_sc[...]).astype(BF16)
            lse_ref[...] = m_sc[...] + jnp.log(l_sc[...]) * LOG2E

    q_spec = pl.BlockSpec((t, LANES), lambda h, p, qi, kj: (qi[p], h))
    k_spec = pl.BlockSpec((t, LANES), lambda h, p, qi, kj: (kj[p], h))
    kr_spec = pl.BlockSpec((t, LANES), lambda h, p, qi, kj: (kj[p], 0))
    stat = pltpu.VMEM((t, LANES), F32)
    return pl.pallas_call(
        body, name=name,
        grid_spec=pltpu.PrefetchScalarGridSpec(
            num_scalar_prefetch=2, grid=(MLA_HEADS, q_blk.shape[0]),
            in_specs=[q_spec, q_spec, k_spec, kr_spec, k_spec], out_specs=[q_spec, q_spec],
            scratch_shapes=[stat, stat, stat]),
        out_shape=[jax.ShapeDtypeStruct(qn.shape, BF16), jax.ShapeDtypeStruct(qn.shape, F32)],
        compiler_params=_params("parallel", "arbitrary"),
    )(q_blk, k_blk, qn, qr, kn, kr, v)


def _attn_bwd(qn, qr, kn, kr, v, do, lse, delta, *, name):
    s, t = qn.shape[0], _attn_tile(qn.shape[0])
    n, sub = s // t, min(t, ATTN_SUB_ROWS)
    q_blk, k_blk = _causal_pairs(n, False)

    def body(qi_ref, kj_ref, qn_ref, qr_ref, kn_ref, kr_ref, v_ref, do_ref, lse_ref, delta_ref,
             dqn_ref, dqr_ref, dkn_ref, dv_ref, dkr_ref, dk_sc, dv_sc):
        p_id = pl.program_id(1)
        i, j = qi_ref[p_id], kj_ref[p_id]

        @pl.when(p_id == 0)
        def _():
            dqn_ref[...] = jnp.zeros_like(dqn_ref)
            dqr_ref[...] = jnp.zeros_like(dqr_ref)

        @pl.when(i == j)
        def _():
            dk_sc[...] = jnp.zeros_like(dk_sc)
            dv_sc[...] = jnp.zeros_like(dv_sc)

        def accumulate(diagonal):
            k = jnp.concatenate([kn_ref[...], kr_ref[...]], axis=1)
            v = v_ref[...]
            starts = list(range(0, t, sub))
            qs, d_os, scs, dps = [], [], [], []
            for r in starts:
                q, sc = _sub_scores(qn_ref, qr_ref, k, r, sub, t, diagonal)
                d_o = do_ref[r:r + sub, :]
                qs.append(q)
                d_os.append(d_o)
                scs.append(sc)
                dps.append(_dot(d_o, v[:sc.shape[1]], _NT))
            ps, dss = [], []
            for r, sc, dp in zip(starts, scs, dps):
                p = jnp.exp2(sc - lse_ref[r:r + sub, :][:, :1])
                ps.append(p.astype(BF16))
                dss.append((p * (dp - delta_ref[r:r + sub, :][:, :1])).astype(BF16))
            for r, q, d_o, p, ds in zip(starts, qs, d_os, ps, dss):
                cols = p.shape[1]
                dv_sc[:cols, :] += _dot(p, d_o, _TN)
                dk_sc[:cols, :] += _dot(ds, q, _TN)
                dq = _dot(ds, k[:cols], _NN) * ATTN_SCALE
                rows = pl.ds(pl.multiple_of(i * t + r, sub), sub)
                dqn_ref[rows, :] += dq[:, :LANES]
                dqr_ref[rows, :] += dq[:, LANES:]

        @pl.when(j < i)
        def _():
            accumulate(False)

        @pl.when(j == i)
        def _():
            accumulate(True)

        @pl.when(i == n - 1)
        def _():
            dkn_ref[...] = (dk_sc[:, :LANES] * LN2).astype(BF16)
            dkr_ref[...] = dk_sc[:, LANES:] * LN2
            dv_ref[...] = dv_sc[...].astype(BF16)

    q_spec = pl.BlockSpec((t, LANES), lambda h, p, qi, kj: (qi[p], h))
    k_spec = pl.BlockSpec((t, LANES), lambda h, p, qi, kj: (kj[p], h))
    kr_spec = pl.BlockSpec((t, LANES), lambda h, p, qi, kj: (kj[p], 0))
    head_spec = pl.BlockSpec((s, LANES), lambda h, p, qi, kj: (0, h))
    f32_out, bf16_out = jax.ShapeDtypeStruct(qn.shape, F32), jax.ShapeDtypeStruct(qn.shape, BF16)
    return pl.pallas_call(
        body, name=name,
        grid_spec=pltpu.PrefetchScalarGridSpec(
            num_scalar_prefetch=2, grid=(MLA_HEADS, q_blk.shape[0]),
            in_specs=[q_spec, q_spec, k_spec, kr_spec, k_spec, q_spec, q_spec, q_spec],
            out_specs=[head_spec, head_spec, k_spec, k_spec, k_spec],
            scratch_shapes=[pltpu.VMEM((t, 2 * LANES), F32), pltpu.VMEM((t, LANES), F32)]),
        out_shape=[f32_out, f32_out, bf16_out, bf16_out, f32_out],
        compiler_params=_params("parallel", "arbitrary"),
    )(q_blk, k_blk, qn, qr, kn, kr, v, do, lse, delta)


def _exchange(arrs, *, scatter, name):
    n = len(arrs)
    out_shape = [jax.ShapeDtypeStruct(a.shape if scatter else (N_DEV, *a.shape), a.dtype) for a in arrs]

    def body(*refs):
        ins, outs = refs[:n], refs[n:2 * n]
        send_sems, recv_sems, local_sems = refs[2 * n:]
        x, y, c = lax.axis_index("x"), lax.axis_index("y"), lax.axis_index("c")
        me = 4 * x + 2 * y + c
        copies = []
        for k in range(n):
            local = pltpu.make_async_copy(ins[k].at[me] if scatter else ins[k], outs[k].at[me], local_sems.at[k])
            local.start()
            copies.append(local)
            for d in range(1, N_DEV):
                px, py, pc = (x + (d >> 2)) % 2, (y + ((d >> 1) & 1)) % 2, (c + (d & 1)) % 2
                peer = 4 * px + 2 * py + pc
                remote = pltpu.make_async_remote_copy(
                    src_ref=ins[k].at[peer] if scatter else ins[k], dst_ref=outs[k].at[me],
                    send_sem=send_sems.at[k, d - 1], recv_sem=recv_sems.at[k, d - 1],
                    device_id=(px, py, pc), device_id_type=pl.DeviceIdType.MESH)
                remote.start()
                copies.append(remote)
        for cp in copies:
            cp.wait()

    any_spec = pl.BlockSpec(memory_space=pl.ANY)
    return pl.pallas_call(
        body, name=name, in_specs=[any_spec] * n, out_specs=[any_spec] * n, out_shape=out_shape,
        scratch_shapes=[pltpu.SemaphoreType.DMA((n, N_DEV - 1)), pltpu.SemaphoreType.DMA((n, N_DEV - 1)),
                        pltpu.SemaphoreType.DMA((n,))],
    )(*arrs)


def _peers(x, y, c):
    out = []
    for d in range(1, N_DEV):
        px, py, pc = (x + (d >> 2)) % 2, (y + ((d >> 1) & 1)) % 2, (c + (d & 1)) % 2
        out.append(((px, py, pc), 4 * px + 2 * py + pc))
    return out


CHIP_LEVEL_PEERS = (1, 2, 4, 6)


def _exchange_copies(ins, lands, send_sems, recv_sems, scatter, chip_level=False):
    x, y, c = lax.axis_index("x"), lax.axis_index("y"), lax.axis_index("c")
    me = 4 * x + 2 * y + c
    local, remote = [], []
    for k in range(len(ins)):
        local.append(pltpu.make_async_copy(ins[k].at[me] if scatter else ins[k], lands[k].at[me],
                                           recv_sems.at[k * N_DEV + N_DEV - 1]))
        for d, (coords, peer) in enumerate(_peers(x, y, c)):
            if chip_level and d + 1 not in CHIP_LEVEL_PEERS:
                continue
            remote.append(pltpu.make_async_remote_copy(
                src_ref=ins[k].at[peer] if scatter else ins[k], dst_ref=lands[k].at[me],
                send_sem=send_sems.at[k * N_DEV + d], recv_sem=recv_sems.at[k * N_DEV + d],
                device_id=coords, device_id_type=pl.DeviceIdType.MESH))
    return local, remote


def _exchange_start(arrs, *, scatter, name, after=None, chip_level=False):
    n = len(arrs)
    hbm = pl.BlockSpec(memory_space=pltpu.HBM)
    sem = pl.BlockSpec(memory_space=pltpu.SEMAPHORE)
    lands = [lax.empty(a.shape if scatter else (N_DEV, *a.shape), a.dtype) for a in arrs]

    def body(*refs):
        ins, land_refs = refs[:n], refs[n:2 * n]
        first_out = 2 * n + (after is not None)
        send_sems, recv_sems, token = refs[first_out], refs[first_out + 1], refs[-1]
        local, remote = _exchange_copies(ins, land_refs, send_sems, recv_sems, scatter, chip_level)
        for cp in local + remote:
            cp.start()
        token[...] = jnp.zeros_like(token)

    operands = [pltpu.with_memory_space_constraint(a, pltpu.HBM) for a in list(arrs) + lands]
    behind = [] if after is None else [after]
    res = pl.pallas_call(
        body, name=name,
        out_shape=(pltpu.SemaphoreType.DMA((n * N_DEV,)), pltpu.SemaphoreType.DMA((n * N_DEV,)),
                   *[pltpu.HBM(o.shape, o.dtype) for o in operands], jax.ShapeDtypeStruct((8, LANES), F32)),
        in_specs=[hbm] * (2 * n) + [pl.BlockSpec(memory_space=pl.ANY)] * len(behind),
        out_specs=(sem, sem, *[hbm] * (2 * n), pl.BlockSpec(memory_space=pltpu.VMEM)),
        input_output_aliases={i: 2 + i for i in range(2 * n)},
        compiler_params=pltpu.CompilerParams(has_side_effects=pltpu.SideEffectType.DATAFLOW_SIDE_EFFECTING),
    )(*operands, *behind)
    return (res[0], res[1], list(res[2:2 + n]), list(res[2 + n:2 + 2 * n]), scatter, chip_level), res[-1]


def _exchange_wait(state, after, *, name):
    send_sems, recv_sems, ins, lands, scatter, chip_level = state
    n = len(ins)
    hbm = pl.BlockSpec(memory_space=pltpu.HBM)
    sem = pl.BlockSpec(memory_space=pltpu.SEMAPHORE)

    def body(*refs):
        in_refs, land_refs = refs[:n], refs[n:2 * n]
        local, remote = _exchange_copies(in_refs, land_refs, refs[2 * n], refs[2 * n + 1], scatter, chip_level)
        for cp in local:
            cp.wait()
        for cp in remote:
            cp.wait_send()
            cp.wait_recv()

    res = pl.pallas_call(
        body, name=name, out_shape=tuple(pltpu.HBM(o.shape, o.dtype) for o in ins + lands),
        in_specs=[hbm] * (2 * n) + [sem, sem, pl.BlockSpec(memory_space=pl.ANY)], out_specs=tuple([hbm] * (2 * n)),
        input_output_aliases={i: i for i in range(2 * n)},
        compiler_params=pltpu.CompilerParams(has_side_effects=pltpu.SideEffectType.DATAFLOW_SIDE_EFFECTING),
    )(*ins, *lands, send_sems, recv_sems, after)
    return list(res[n:])


def _chip_forward(lands, *, name):
    n = len(lands)

    def body(*refs):
        ins, outs, send_sems, recv_sems = refs[:n], refs[n:2 * n], refs[2 * n], refs[2 * n + 1]
        x, y, c = lax.axis_index("x"), lax.axis_index("y"), lax.axis_index("c")
        copies = []
        for k in range(n):
            for j, (dx, dy) in enumerate(((0, 1), (1, 0), (1, 1))):
                held = 4 * ((x + dx) % 2) + 2 * ((y + dy) % 2) + c
                cp = pltpu.make_async_remote_copy(
                    src_ref=ins[k].at[held], dst_ref=outs[k].at[held], send_sem=send_sems.at[k, j],
                    recv_sem=recv_sems.at[k, j], device_id=(x, y, 1 - c), device_id_type=pl.DeviceIdType.MESH)
                cp.start()
                copies.append(cp)
        for cp in copies:
            cp.wait()

    any_spec = pl.BlockSpec(memory_space=pl.ANY)
    return pl.pallas_call(
        body, name=name, in_specs=[any_spec] * n, out_specs=[any_spec] * n,
        out_shape=[jax.ShapeDtypeStruct(a.shape, a.dtype) for a in lands], input_output_aliases={k: k for k in range(n)},
        scratch_shapes=[pltpu.SemaphoreType.DMA((n, 3)), pltpu.SemaphoreType.DMA((n, 3))],
    )(*lands)


def _adam(w, terms, m, v, *, name):
    n_layers, r, c = w.shape
    tr = min(r, ADAM_ROW_TILE)
    assert r % tr == 0 and len(terms) == n_layers
    steps = r // tr

    def body(w_ref, *rest):
        t_refs, (m_ref, v_ref, g_out, d_out, m_out, v_out) = rest[:n_layers], rest[n_layers:]
        for layer, t_ref in enumerate(t_refs):
            @pl.when(pl.program_id(0) == layer)
            def _(t_ref=t_ref):
                g = t_ref[0].astype(F32)
                for s in range(1, t_ref.shape[0]):
                    g = g + t_ref[s].astype(F32)
                m1 = ADAM_B1 * m_ref[...] + (1.0 - ADAM_B1) * g
                v1 = ADAM_B2 * v_ref[...] + (1.0 - ADAM_B2) * jnp.square(g)
                m_hat = m1 / (1.0 - ADAM_B1 ** ADAM_STEP)
                v_hat = v1 / (1.0 - ADAM_B2 ** ADAM_STEP)
                g_out[...] = g
                d_out[...] = -ADAM_LR * (m_hat / (jnp.sqrt(v_hat) + ADAM_EPS) + ADAM_WD * w_ref[...])
                m_out[...] = m1
                v_out[...] = v1

    def term_spec(layer, t):
        return pl.BlockSpec((t.shape[0], tr, c),
                            lambda l, i: (0, jnp.where(l == layer, i, jnp.where(l < layer, 0, steps - 1)), 0))

    spec = pl.BlockSpec((None, tr, c), lambda l, i: (l, i, 0))
    out = jax.ShapeDtypeStruct(w.shape, F32)
    return pl.pallas_call(
        body, name=name, grid=(n_layers, steps),
        in_specs=[spec] + [term_spec(layer, t) for layer, t in enumerate(terms)] + [spec, spec], out_specs=[spec] * 4,
        out_shape=[out] * 4, compiler_params=_params("arbitrary", "arbitrary"),
    )(w, *terms, m, v)


def _sum_terms(terms, *, name):
    n, _, p = terms.shape

    def body(t_ref, o_ref):
        acc = t_ref[0]
        for s in range(1, n):
            acc = acc + t_ref[s]
        o_ref[...] = acc

    return pl.pallas_call(body, name=name, out_shape=jax.ShapeDtypeStruct((1, p), F32))(terms)


def _lb_logits_grad(dlb, logits, *, name):
    def body(dlb_ref, l_ref, o_ref):
        lb = _lower_bound(l_ref[...])
        d0 = dlb_ref[...] * lb * (1.0 - lb)
        o_ref[...] = jnp.concatenate([d0, -d0], axis=0)

    return pl.pallas_call(body, name=name, out_shape=jax.ShapeDtypeStruct(logits.shape, F32))(dlb, logits)


def _silu_grad(z):
    sg = _sigmoid(z)
    return sg * (1.0 + z * (1.0 - sg))


def _head_norm_gate(o, zg, gn):
    outs = []
    for h in range(HGRN_HEADS):
        sl = slice(h * LANES, (h + 1) * LANES)
        zg_h = zg[:, sl]
        outs.append(_rms(o[:, sl], gn) * (zg_h * _sigmoid(zg_h)))
    return (jnp.concatenate(outs, axis=1),)


def _head_norm_gate_bwd(o, zg, dm, gn):
    do_parts, dzg_parts, dgn = [], [], jnp.zeros((1, LANES), F32)
    for h in range(HGRN_HEADS):
        sl = slice(h * LANES, (h + 1) * LANES)
        o_h, zg_h, dm_h = o[:, sl], zg[:, sl], dm[:, sl]
        gate = zg_h * _sigmoid(zg_h)
        do_h, dgn_h = _rms_bwd(o_h, gn, dm_h * gate)
        dgn = dgn + dgn_h
        do_parts.append(do_h)
        dzg_parts.append(dm_h * _rms(o_h, gn) * _silu_grad(zg_h))
    return jnp.concatenate(do_parts, axis=1), jnp.concatenate(dzg_parts, axis=1), dgn


def _rope_slabs(x, t_c, t_s1, t_s2, transpose):
    fn = _rope_t if transpose else _rope
    return jnp.concatenate(
        [fn(x[:, h * LANES:(h + 1) * LANES], t_c, t_s1, t_s2) for h in range(x.shape[1] // LANES)], axis=1)


def _loss_head(h, tgt, w):
    d = h.shape[1]
    r = lax.rsqrt(jnp.mean(h * h, axis=-1, keepdims=True) + EPS)
    xh = h * r
    err = xh * w - tgt
    loss = 0.5 * jnp.sum(jnp.mean(err * err, axis=-1, keepdims=True), axis=0, keepdims=True)
    dy = err / d
    dxh = dy * w
    dh = r * (dxh - xh * jnp.mean(dxh * xh, axis=-1, keepdims=True))
    return dh, dh, jnp.sum(dy * xh, axis=0, keepdims=True), jnp.broadcast_to(loss, (1, LANES))


def _mlp_fwd(h, norm, w_up, w_down, tag, loss_head=None):
    d = h.shape[1]

    def up(x, g, wu):
        x_n = _rms(x, g).astype(BF16)
        return x_n, jnp.concatenate([jnp.square(jnp.maximum(_dot(x_n, wu[j], _NN), 0.0)) for j in range(wu.shape[0])],
                                    axis=1)

    xn, act = _rowcall(up, [h], [norm, w_up], [(d, BF16), (w_up.shape[0] * w_up.shape[2], BF16)], [], tr=512,
                       name=f"{tag}_up")
    if callable(w_down):
        w_down = w_down(act)
    if loss_head is None:
        return _rowcall(lambda a, res, wd: (res + _dot(a, wd, _NN),), [act, h], [w_down], [(d, F32)], [],
                        tr=512, name=f"{tag}_down")[0], (h, xn, act)
    tgt, final_norm = loss_head

    def down_and_loss(a, res, t, wd, g):
        return _loss_head(res + _dot(a, wd, _NN), t, g)

    return _rowcall(down_and_loss, [act, h, tgt], [w_down, final_norm], [(d, F32), (d, BF16)], [d, LANES],
                    name=f"{tag}_down_loss"), (h, xn, act)


def _mlp_bwd(dh_out, dh_out_bf, saved, norm, w_up, w_down, tag, after=None):
    h, xn, act = saved
    d = h.shape[1]
    du = _rowcall(lambda dres, a, wd: (_dot(dres, wd, _NT) * (2.0 * jnp.sqrt(a.astype(F32))),), [dh_out_bf, act],
                  [w_down], [(act.shape[1], BF16)], [], tr=512, after=after, name=f"{tag}_bwd_du")[0]
    dw_down = _mm(act, dh_out_bf, mode="tn", name=f"{tag}_bwd_wdown")
    dw_up = _mm(xn, du, mode="tn", col_shards=w_up.shape[0], name=f"{tag}_bwd_wup")

    def up_norm_bwd(x, d_u, dres, g, wu):
        cols = wu.shape[2]
        dxn = _dot(d_u[:, :cols], wu[0], _NT)
        for j in range(1, wu.shape[0]):
            dxn = dxn + _dot(d_u[:, j * cols:(j + 1) * cols], wu[j], _NT)
        dx, dw = _rms_bwd(x, g, dxn)
        return dx + dres, dx + dres, dw

    dh, dh_bf, dnorm = _rowcall(up_norm_bwd, [h, du, dh_out], [norm, w_up], [(d, F32), (d, BF16)], [d], tr=512,
                                name=f"{tag}_bwd_dxn")
    return dh, dh_bf, dnorm, dw_up, dw_down


def _row_major(g):
    return g.reshape(g.shape[0] * g.shape[1], g.shape[2])


def _col_major(g):
    return jnp.transpose(g, (1, 0, 2)).reshape(g.shape[1], g.shape[0] * g.shape[2])


def _col_terms(dw):
    k, n = dw.shape
    return jnp.transpose(dw.reshape(k, N_DEV, n // N_DEV), (1, 0, 2))


def _row_terms(dw):
    return dw.reshape(N_DEV, dw.shape[0] // N_DEV, dw.shape[1])


def kernel(x, hgrn_norm, hgrn_w_q, hgrn_w_f, hgrn_w_i, hgrn_w_g, hgrn_g_norm, hgrn_w_o, hgrn_lb_logits, mla_norm, mla_w_dq, mla_q_norm, mla_w_uq, mla_w_o, kv_in_norm, kv_w_dkv, kv_norm, kv_w_uk, kv_w_uv, mlp_norm, mlp_w_up, mlp_w_down, final_norm, loss_target, m_hgrn_norm, m_hgrn_w_q, m_hgrn_w_f, m_hgrn_w_i, m_hgrn_w_g, m_hgrn_g_norm, m_hgrn_w_o, m_hgrn_lb_logits, m_mla_norm, m_mla_w_dq, m_mla_q_norm, m_mla_w_uq, m_mla_w_o, m_kv_in_norm, m_kv_w_dkv, m_kv_norm, m_kv_w_uk, m_kv_w_uv, m_mlp_norm, m_mlp_w_up, m_mlp_w_down, m_final_norm, v_hgrn_norm, v_hgrn_w_q, v_hgrn_w_f, v_hgrn_w_i, v_hgrn_w_g, v_hgrn_g_norm, v_hgrn_w_o, v_hgrn_lb_logits, v_mla_norm, v_mla_w_dq, v_mla_q_norm, v_mla_w_uq, v_mla_w_o, v_kv_in_norm, v_kv_w_dkv, v_kv_norm, v_kv_w_uk, v_kv_w_uv, v_mlp_norm, v_mlp_w_up, v_mlp_w_down, v_final_norm):
    given = dict(locals())
    weight_names = ["hgrn_norm", "hgrn_w_q", "hgrn_w_f", "hgrn_w_i", "hgrn_w_g", "hgrn_g_norm", "hgrn_w_o",
                    "hgrn_lb_logits", "mla_norm", "mla_w_dq", "mla_q_norm", "mla_w_uq", "mla_w_o", "kv_in_norm",
                    "kv_w_dkv", "kv_norm", "kv_w_uk", "kv_w_uv", "mlp_norm", "mlp_w_up", "mlp_w_down", "final_norm"]
    me = 4 * lax.axis_index("x") + 2 * lax.axis_index("y") + lax.axis_index("c")
    xs, tgt = x[0], loss_target[0]
    seq, d_model = xs.shape
    n_heads, hd = MLA_HEADS, LANES

    big_local = {
        "hgrn_w_q": hgrn_w_q[0], "hgrn_w_f": hgrn_w_f[0], "hgrn_w_i": hgrn_w_i[0], "hgrn_w_g": hgrn_w_g[0],
        "hgrn_w_o": hgrn_w_o[0], "mla_w_dq": mla_w_dq[0], "mla_w_uq": mla_w_uq[0], "mla_w_o": mla_w_o[0],
        "kv_w_dkv": kv_w_dkv, "kv_w_uk": kv_w_uk, "kv_w_uv": kv_w_uv,
        "mlp_w_up0": mlp_w_up[0], "mlp_w_up1": mlp_w_up[1], "mlp_w_down0": mlp_w_down[0], "mlp_w_down1": mlp_w_down[1],
    }
    big_names = list(big_local)
    col_sharded = {"mla_w_uq", "kv_w_uk", "kv_w_uv"}
    shard_major = {"mlp_w_up0", "mlp_w_up1"}
    vec_local = jnp.concatenate([hgrn_norm, hgrn_lb_logits], axis=0)
    first_names = ["hgrn_w_q", "hgrn_w_f", "hgrn_w_i"]
    proj_names = first_names + ["hgrn_w_g"]
    later_names = {"hgrn_o": ["hgrn_w_g", "hgrn_w_o"], "up0": ["mlp_w_up0"], "down0": ["mlp_w_down0"],
                   "mla": ["kv_w_dkv", "kv_w_uk", "kv_w_uv", "mla_w_dq", "mla_w_uq", "mla_w_o"],
                   "mlp1": ["mlp_w_up1", "mlp_w_down1"]}

    def unshard(names, arrays):
        return {k: (a if k in shard_major else _col_major(a) if k in col_sharded else _row_major(a))
                for k, a in zip(names, arrays)}

    two_level = {"down0", "mla"}
    first_state, token = _exchange_start([big_local[k].astype(BF16) for k in first_names] + [vec_local], scatter=False,
                                         chip_level=True, name="gather_first_start")
    gather_state = {}
    for tag, names in later_names.items():
        gather_state[tag], token = _exchange_start([big_local[k].astype(BF16) for k in names], scatter=False,
                                                   chip_level=tag in two_level, after=token, name=f"gather_{tag}_start")

    def gather_wait(tag, after):
        landed = _exchange_wait(gather_state[tag], after, name=f"gather_{tag}_wait")
        if tag in two_level:
            landed = _chip_forward(landed, name=f"gather_{tag}_forward")
        w.update(unshard(later_names[tag], landed))
        return [w[k] for k in later_names[tag]]

    gathered = _chip_forward(_exchange_wait(first_state, token, name="gather_first_wait"), name="gather_first_forward")
    w = unshard(first_names, gathered[:-1])
    vec_full = jnp.transpose(gathered[-1], (1, 0, 2)).reshape(3, d_model)
    hgrn_norm_full, lb_logits_full = vec_full[0:1], vec_full[1:3]
    t_c, t_s1, t_s2 = _rope_tables(seq)
    kv_lora = kv_w_uk.shape[0]

    def hgrn_proj(a, g, *weights):
        xn = _rms(a, g).astype(BF16)
        return (xn, *[_dot(xn, wt, _NN) for wt in weights])

    xn0, zq, zf, zi = _rowcall(hgrn_proj, [xs], [hgrn_norm_full] + [w[k] for k in first_names],
                               [(d_model, BF16)] + [(d_model, F32)] * 3, [], tr=512, name="hgrn_proj")
    o_rec, states = _hgrn_fwd(zq, zf, zi, lb_logits_full, name="hgrn_fwd")
    gather_wait("hgrn_o", o_rec)

    def gate_out(o, x_n, res, gn, wg, wo):
        z = _dot(x_n, wg, _NN)
        m = _head_norm_gate(o, z, gn)[0].astype(BF16)
        return z, m, res + _dot(m, wo, _NN)

    zg, mixed, h1 = _rowcall(gate_out, [o_rec, xn0, xs], [hgrn_g_norm, w["hgrn_w_g"], w["hgrn_w_o"]],
                             [(d_model, F32), (d_model, BF16), (d_model, F32)], [], name="hgrn_gate_out")
    h2, mlp0_saved = _mlp_fwd(h1, mlp_norm[0:1], gather_wait("up0", h1)[0], lambda act: gather_wait("down0", act)[0],
                              "mlp0")
    gather_wait("mla", h2)
    w_uq3 = w["mla_w_uq"].reshape(-1, n_heads, MLA_NOPE + MLA_ROPE)
    w_uq_nope = w_uq3[:, :, :MLA_NOPE].reshape(-1, n_heads * hd)
    w_uq_rope = jnp.pad(w_uq3[:, :, MLA_NOPE:], ((0, 0), (0, 0), (0, hd - MLA_ROPE))).reshape(-1, n_heads * hd)
    w_dkv_pad = jnp.pad(w["kv_w_dkv"], ((0, 0), (0, kv_lora + hd - w["kv_w_dkv"].shape[1])))

    q_lora, qk_cols = w["mla_w_dq"].shape[1], n_heads * hd

    def mla_qkv(a, tc, ts1, ts2, g_kv_in, g_mla, g_q, g_kv, wdq, wn, wr, wdkv, wuk, wuv):
        h_n, x_n = _rms(a, g_kv_in).astype(BF16), _rms(a, g_mla).astype(BF16)
        cq = _dot(x_n, wdq, _NN)
        cq_n = _rms(cq, g_q).astype(BF16)
        q_nope = _dot(cq_n, wn, _NN) * Q_PRESCALE
        q_rope = _rope_slabs(_dot(cq_n, wr, _NN) * Q_PRESCALE, tc, ts1, ts2, False)
        c_all = _dot(h_n, wdkv, _NN)
        lat = _rms(c_all[:, :kv_lora], g_kv).astype(BF16)
        return (h_n, x_n, cq, cq_n, q_nope, q_rope, c_all, lat, _rope(c_all[:, kv_lora:], tc, ts1, ts2),
                _dot(lat, wuk, _NN), _dot(lat, wuv, _NN))

    hn, xn2, cq_pre, c_q, qn, qr, ckr, c_kv, kr, kn, vv = _rowcall(
        mla_qkv, [h2, t_c, t_s1, t_s2],
        [kv_in_norm[None, :], mla_norm, mla_q_norm, kv_norm[None, :], w["mla_w_dq"], w_uq_nope, w_uq_rope, w_dkv_pad,
         w["kv_w_uk"], w["kv_w_uv"]],
        [(d_model, BF16), (d_model, BF16), (q_lora, F32), (q_lora, BF16), (qk_cols, BF16), (qk_cols, BF16),
         (kv_lora + hd, F32), (kv_lora, BF16), (hd, BF16), (qk_cols, BF16), (qk_cols, BF16)], [], tr=512, name="mla_qkv")
    o_att, lse = _attn_fwd(qn, qr, kn, kr, vv, name="attn_fwd")
    h3 = _rowcall(lambda o, res, wo: (res + _dot(o, wo, _NN),), [o_att, h2], [w["mla_w_o"]], [(d_model, F32)], [],
                  tr=512, name="attn_out")[0]
    gather_wait("mlp1", h3)
    (dh4, dh4_bf, g_final_norm, loss_part), mlp1_saved = _mlp_fwd(
        h3, mlp_norm[1:2], w["mlp_w_up1"], w["mlp_w_down1"], "mlp1", loss_head=(tgt, final_norm[None, :]))

    g = {}
    groups = {"mlp1": ["mlp_w_up1", "mlp_w_down1"],
              "mla": ["mla_w_o", "mla_w_uq", "mla_w_dq", "kv_w_uk", "kv_w_uv", "kv_w_dkv"],
              "mlp0": ["mlp_w_up0", "mlp_w_down0"],
              "hgrn_out": ["hgrn_w_o", "hgrn_w_g"],
              "hgrn_in": ["hgrn_w_q", "hgrn_w_f", "hgrn_w_i"]}
    scatter_state = {}

    def scatter_start(tag, after=None):
        scatter_state[tag], tok = _exchange_start(
            [g[k] if k in shard_major else (_col_terms if k in col_sharded else _row_terms)(g[k]) for k in groups[tag]],
            scatter=True, after=after,
            name=f"scatter_{tag}_start")
        return tok

    dh3, dh3_bf, g_mlp_norm1, g["mlp_w_up1"], g["mlp_w_down1"] = _mlp_bwd(
        dh4, dh4_bf, mlp1_saved, mlp_norm[1:2], w["mlp_w_up1"], w["mlp_w_down1"], "mlp1")
    def attn_out_bwd(dres, o, wo):
        d_o = _dot(dres, wo, _NT).astype(BF16)
        prod = d_o.astype(F32) * o.astype(F32)
        return d_o, jnp.concatenate([jnp.broadcast_to(jnp.sum(prod[:, h * hd:(h + 1) * hd], axis=1, keepdims=True),
                                                      (prod.shape[0], hd)) for h in range(n_heads)], axis=1)

    d_oatt, delta = _rowcall(attn_out_bwd, [dh3_bf, o_att], [w["mla_w_o"]], [(qk_cols, BF16), (qk_cols, F32)], [],
                             after=scatter_start("mlp1"), name="attn_out_bwd_x")
    g["mla_w_o"] = _mm(o_att, dh3_bf, mode="tn", name="attn_out_bwd_w")
    dqn, dqr, dkn, dvv, dkr = _attn_bwd(qn, qr, kn, kr, vv, d_oatt, lse, delta, name="attn_bwd")

    def q_path_bwd(cq, cq_n, x_n, d_qn, d_qr, tc, ts1, ts2, g_q, wdq, wn, wr):
        d_qn, d_qr = d_qn.astype(BF16), _rope_slabs(d_qr, tc, ts1, ts2, True).astype(BF16)
        d_cq, d_gq = _rms_bwd(cq, g_q, _dot(d_qn, wn, _NT) + _dot(d_qr, wr, _NT))
        d_cq = d_cq.astype(BF16)
        return _dot(d_cq, wdq, _NT), d_gq, _dot(x_n, d_cq, _TN), _dot(cq_n, d_qn, _TN), _dot(cq_n, d_qr, _TN)

    dxn2, g_q_norm, g_dq, g_uq_nope, g_uq_rope = _rowcall(
        q_path_bwd, [cq_pre, c_q, xn2, dqn, dqr, t_c, t_s1, t_s2], [mla_q_norm, w["mla_w_dq"], w_uq_nope, w_uq_rope],
        [(d_model, F32)], [q_lora, (d_model, q_lora), (q_lora, qk_cols), (q_lora, qk_cols)], tr=512, name="mla_q_bwd")
    g["mla_w_dq"] = g_dq.astype(GRAD_WIRE_DTYPE)
    g["mla_w_uq"] = jnp.concatenate([g_uq_nope.reshape(q_lora, n_heads, hd),
                                     g_uq_rope.reshape(q_lora, n_heads, hd)[:, :, :MLA_ROPE]],
                                    axis=2).reshape(q_lora, -1).astype(GRAD_WIRE_DTYPE)

    def kv_path_bwd(c_all, lat, h_n, d_kn, d_v, d_kr_heads, tc, ts1, ts2, a, d_xn2, dres,
                    g_kv, g_kv_in, g_mla, wdkv, wuk, wuv):
        d_lat, d_gkv = _rms_bwd(c_all[:, :kv_lora], g_kv, _dot(d_kn, wuk, _NT) + _dot(d_v, wuv, _NT))
        d_kr = d_kr_heads[:, :hd]
        for h in range(1, n_heads):
            d_kr = d_kr + d_kr_heads[:, h * hd:(h + 1) * hd]
        d_all = jnp.concatenate([d_lat, _rope_t(d_kr, tc, ts1, ts2)], axis=1).astype(BF16)
        dx1, d_gkv_in = _rms_bwd(a, g_kv_in, _dot(d_all, wdkv, _NT))
        dx2, d_gmla = _rms_bwd(a, g_mla, d_xn2)
        d_a = dx1 + dx2 + dres
        return (d_a, d_a, d_gkv, d_gkv_in, d_gmla, _dot(h_n, d_all, _TN), _dot(lat, d_kn, _TN), _dot(lat, d_v, _TN))

    dh2, dh2_bf, g_kv_norm, g_kv_in_norm, g_mla_norm, g_dkv, g_uk, g_uv = _rowcall(
        kv_path_bwd, [ckr, c_kv, hn, dkn, dvv, dkr, t_c, t_s1, t_s2, h2, dxn2, dh3],
        [kv_norm[None, :], kv_in_norm[None, :], mla_norm, w_dkv_pad, w["kv_w_uk"], w["kv_w_uv"]],
        [(d_model, F32), (d_model, BF16)],
        [kv_lora, d_model, d_model, (d_model, kv_lora + hd), (kv_lora, qk_cols), (kv_lora, qk_cols)], name="mla_kv_bwd")
    g["kv_w_dkv"] = g_dkv[:, :kv_w_dkv.shape[1]].astype(GRAD_WIRE_DTYPE)
    g["kv_w_uk"], g["kv_w_uv"] = g_uk.astype(GRAD_WIRE_DTYPE), g_uv.astype(GRAD_WIRE_DTYPE)
    dh1, dh1_bf, g_mlp_norm0, g["mlp_w_up0"], g["mlp_w_down0"] = _mlp_bwd(
        dh2, dh2_bf, mlp0_saved, mlp_norm[0:1], w["mlp_w_up0"], w["mlp_w_down0"], "mlp0", after=scatter_start("mla"))

    g["hgrn_w_o"] = _mm(mixed, dh1_bf, mode="tn", after=scatter_start("mlp0"), name="hgrn_out_bwd_w")
    do_rec, dzg, g_g_norm = _rowcall(
        lambda dres, o, z, wo, gn: _head_norm_gate_bwd(o, z, _dot(dres, wo, _NT), gn), [dh1_bf, o_rec, zg],
        [w["hgrn_w_o"], hgrn_g_norm], [(d_model, F32), (d_model, BF16)], [hd], name="hgrn_gate_out_bwd")
    g["hgrn_w_g"] = _mm(xn0, dzg, mode="tn", name="hgrn_w_g_bwd_w")
    dzq, dzf, dzi, g_lb = _hgrn_bwd(zq, zf, zi, lb_logits_full, states, do_rec, scatter_start("hgrn_out"),
                                    name="hgrn_bwd")
    for nm, dz in (("hgrn_w_q", dzq), ("hgrn_w_f", dzf), ("hgrn_w_i", dzi)):
        g[nm] = _mm(xn0, dz, mode="tn", name=f"{nm}_bwd_w")

    def hgrn_proj_bwd(a, dres, *rest):
        dzs, gw, weights = rest[:4], rest[4], rest[5:]
        dxn = _dot(dzs[0], weights[0], _NT)
        for dz, wt in zip(dzs[1:], weights[1:]):
            dxn = dxn + _dot(dz, wt, _NT)
        dx, dw = _rms_bwd(a, gw, dxn)
        return dx + dres, dw

    grad_x, g_hgrn_norm = _rowcall(hgrn_proj_bwd, [xs, dh1, dzq, dzf, dzi, dzg],
                                   [hgrn_norm_full] + [w[k] for k in proj_names], [(d_model, F32)], [d_model],
                                   tr=512, name="hgrn_proj_bwd")

    small_parts = [g_hgrn_norm, g_lb, g_g_norm, g_mla_norm, g_q_norm, g_kv_in_norm, g_kv_norm, g_mlp_norm0,
                   g_mlp_norm1, g_final_norm, loss_part]
    small_sizes = [p.shape[1] for p in small_parts]
    small_terms = _exchange([jnp.concatenate(small_parts, axis=1)], scatter=False, name="gather_small")[0]
    small_sum = _sum_terms(small_terms, name="sum_small")
    last = scatter_start("hgrn_in", after=small_sum)
    offs = [0]
    for sz in small_sizes:
        offs.append(offs[-1] + sz)
    (s_hgrn_norm, s_lb, s_g_norm, s_mla_norm, s_q_norm, s_kv_in_norm, s_kv_norm, s_mlp_norm0, s_mlp_norm1, s_final_norm,
     s_loss) = [small_sum[:, a:b] for a, b in zip(offs[:-1], offs[1:])]
    shard = hgrn_norm.shape[1]
    g_lb_logits = _lb_logits_grad(lax.dynamic_slice_in_dim(s_lb, me * shard, shard, axis=1), hgrn_lb_logits,
                                  name="lb_logits_grad")
    loss = s_loss[0, 0]

    res, layer_terms = {}, {}

    def update(k, term_list):
        shape = given[k].shape
        as_layers = (len(term_list), shape[-2], shape[-1])
        upd = _adam(given[k].reshape(as_layers), term_list, given["m_" + k].reshape(as_layers),
                    given["v_" + k].reshape(as_layers), name=f"adam_{k}")
        res[k] = [o.reshape(shape) for o in upd]
        return upd[0]

    for tag, names in groups.items():
        for k, t in zip(names, _exchange_wait(scatter_state[tag], last, name=f"scatter_{tag}_wait")):
            if k.startswith("mlp_w_"):
                layer_terms.setdefault(k[:-1], {})[int(k[-1])] = t
                if len(layer_terms[k[:-1]]) == 2:
                    last = update(k[:-1], [layer_terms[k[:-1]][0], layer_terms[k[:-1]][1]])
            else:
                last = update(k, [t])

    small_grads = {
        "hgrn_norm": lax.dynamic_slice_in_dim(s_hgrn_norm, me * shard, shard, axis=1),
        "hgrn_g_norm": s_g_norm, "hgrn_lb_logits": g_lb_logits, "mla_norm": s_mla_norm, "mla_q_norm": s_q_norm,
        "kv_in_norm": s_kv_in_norm, "kv_norm": s_kv_norm,
        "mlp_norm": jnp.concatenate([s_mlp_norm0, s_mlp_norm1], axis=0), "final_norm": s_final_norm,
    }
    small_names = list(small_grads)

    def flat(a):
        return a.reshape(1, -1)

    packed = [jnp.concatenate([flat(src[pre + k]) for k in small_names], axis=1)
              for src, pre in ((given, ""), (small_grads, ""), (given, "m_"), (given, "v_"))]
    small_out = _adam(packed[0][None], [packed[1][None]], packed[2][None], packed[3][None], name="adam_small")
    off = 0
    for k in small_names:
        size = given[k].size
        res[k] = [o[0, :, off:off + size].reshape(given[k].shape) for o in small_out]
        off += size

    outs = [loss, grad_x[None]]
    for i in range(4):
        outs += [res[k][i] for k in weight_names]
    return tuple(outs)
```

```python
import functools

import jax
import jax.numpy as jnp
from jax import lax
from jax.experimental import pallas as pl
from jax.experimental.pallas import tpu as pltpu

F32 = jnp.float32
BF16 = jnp.bfloat16

EPS = 1e-6
LANES = 128
N_DEV = 8
V7X_VMEM_LIMIT_BYTES = 56 << 20
MM_PIPELINE_BYTES = 30 << 20
MM_ROW_TILE = 512
ADAM_ROW_TILE = 256
GRAD_WIRE_DTYPE = BF16

HGRN_HEADS = 8
HGRN_CHUNK = 64
HGRN_SUB = 16
HGRN_HEADS_PER_STEP = 8
HGRN_CHUNKS_PER_STEP = 4
EXP_CLAMP = 80.0
MLA_HEADS = 16
MLA_NOPE = 128
MLA_ROPE = 64
ROPE_THETA = 10000.0
ATTN_SCALE = (MLA_NOPE + MLA_ROPE) ** -0.5

ADAM_LR = 0.001
ADAM_B1 = 0.9
ADAM_B2 = 0.999
ADAM_EPS = 1e-08
ADAM_WD = 0.01
ADAM_STEP = 10

_NN = ((1,), (0,))
_NT = ((1,), (1,))
_TN = ((0,), (0,))


def _params(*sem):
    return pltpu.CompilerParams(dimension_semantics=sem, vmem_limit_bytes=V7X_VMEM_LIMIT_BYTES)


def _dot(a, b, dims):
    return lax.dot_general(a.astype(BF16), b.astype(BF16), (dims, ((), ())), preferred_element_type=F32)


def _dot_f32(a, b, dims=_NN):
    return lax.dot_general(a, b, (dims, ((), ())), precision=lax.Precision.HIGH, preferred_element_type=F32)


def _sigmoid(x):
    return 1.0 / (1.0 + jnp.exp(-x))


def _rms(x, w):
    r = lax.rsqrt(jnp.mean(x * x, axis=-1, keepdims=True) + EPS)
    return x * r * w


def _rms_bwd(x, w, dy):
    r = lax.rsqrt(jnp.mean(x * x, axis=-1, keepdims=True) + EPS)
    xh = x * r
    dw = jnp.sum(dy * xh, axis=0, keepdims=True)
    dxh = dy * w
    dx = r * (dxh - xh * jnp.mean(dxh * xh, axis=-1, keepdims=True))
    return dx, dw


def _mm_tiles(m, n, k, a_bytes, b_bytes, out_tile_bytes):
    tm = min(m, MM_ROW_TILE)
    for tn in (n, 2048, 1024, 512, 256, LANES):
        if tn <= n and n % tn == 0:
            if 2 * (tm * k * a_bytes + k * tn * b_bytes + tm * tn * out_tile_bytes) <= MM_PIPELINE_BYTES:
                return tm, tn
    return tm, min(n, LANES)


def _mm(a, b, *, mode, name, out_dtype=None, after=None, col_shards=None):
    if mode == "nn":
        (m, k), (k2, n) = a.shape, b.shape
    elif mode == "nt":
        (m, k), (n, k2) = a.shape, b.shape
    else:
        (k, m), (k2, n) = a.shape, b.shape
    assert k == k2, (name, a.shape, b.shape)
    if out_dtype is None:
        out_dtype = GRAD_WIRE_DTYPE if mode == "tn" else F32
    tm, tn = _mm_tiles(m, n, k, a.dtype.itemsize, b.dtype.itemsize, jnp.dtype(out_dtype).itemsize)
    if col_shards is not None:
        tn = n // col_shards
    assert m % tm == 0 and n % tn == 0, (name, m, n)
    dims = {"nn": _NN, "nt": _NT, "tn": _TN}[mode]
    a_spec = pl.BlockSpec((k, tm), lambda i, j: (0, i)) if mode == "tn" else pl.BlockSpec((tm, k), lambda i, j: (i, 0))
    b_spec = pl.BlockSpec((tn, k), lambda i, j: (j, 0)) if mode == "nt" else pl.BlockSpec((k, tn), lambda i, j: (0, j))
    o_spec = pl.BlockSpec((tm, tn), lambda i, j: (i, j))
    operands, in_specs = [a, b], [a_spec, b_spec]
    if after is not None:
        operands.append(after)
        in_specs.append(pl.BlockSpec(memory_space=pl.ANY))
    out_shape = jax.ShapeDtypeStruct((m, n), out_dtype)
    if col_shards is not None:
        out_shape = jax.ShapeDtypeStruct((col_shards, m, tn), out_dtype)
        o_spec = pl.BlockSpec((None, tm, tn), lambda i, j: (j, i, 0))

    def body(*refs):
        refs[-1][...] = _dot(refs[0][...], refs[1][...], dims).astype(out_dtype)

    return pl.pallas_call(
        body, name=name, grid=(m // tm, n // tn), in_specs=in_specs, out_specs=o_spec, out_shape=out_shape,
        compiler_params=_params("parallel", "parallel"),
    )(*operands)


def _dw_shared(a, bs, *, name):
    (k, m), (_, n) = a.shape, bs[0].shape
    tm, tn = min(m, MM_ROW_TILE), min(n, MM_ROW_TILE)
    nb = len(bs)

    def body(*refs):
        a_tile = refs[0][...]
        for b_ref, o_ref in zip(refs[1:1 + nb], refs[1 + nb:]):
            o_ref[...] = _dot(a_tile, b_ref[...], _TN).astype(GRAD_WIRE_DTYPE)

    b_spec = pl.BlockSpec((k, tn), lambda i, j: (0, j))
    o_spec = pl.BlockSpec((tm, tn), lambda i, j: (i, j))
    return pl.pallas_call(
        body, name=name, grid=(m // tm, n // tn), in_specs=[pl.BlockSpec((k, tm), lambda i, j: (0, i))] + [b_spec] * nb,
        out_specs=[o_spec] * nb, out_shape=[jax.ShapeDtypeStruct((m, n), GRAD_WIRE_DTYPE)] * nb,
        compiler_params=_params("parallel", "parallel"),
    )(a, *bs)


def _rowcall(fn, rows, consts, outs, accs, *, name, tr=256, after=None):
    s = rows[0].shape[0]
    tr = min(tr, s)
    assert s % tr == 0
    n_out = len(outs)
    accs = [(1, a) if isinstance(a, int) else a for a in accs]
    in_specs = [pl.BlockSpec((tr, r.shape[1]), lambda i: (i, 0)) for r in rows]
    in_specs += [pl.BlockSpec(c.shape, lambda i, nd=c.ndim: (0,) * nd) for c in consts]
    out_shape = [jax.ShapeDtypeStruct((s, w), dt) for w, dt in outs] + [jax.ShapeDtypeStruct(a, F32) for a in accs]
    out_specs = [pl.BlockSpec((tr, w), lambda i: (i, 0)) for w, _ in outs] + [pl.BlockSpec(a, lambda i: (0, 0)) for a in accs]
    n_in = len(rows) + len(consts)

    def body(*refs):
        res = fn(*[r[...] for r in refs[:n_in]])
        out_refs = refs[n_in + (after is not None):]
        for ref, val in zip(out_refs[:n_out], res[:n_out]):
            ref[...] = val.astype(ref.dtype)
        i = pl.program_id(0)
        for ref, val in zip(out_refs[n_out:], res[n_out:]):
            @pl.when(i == 0)
            def _(ref=ref, val=val):
                ref[...] = val

            @pl.when(i > 0)
            def _(ref=ref, val=val):
                ref[...] += val

    behind = [] if after is None else [after]
    return pl.pallas_call(
        body, name=name, grid=(s // tr,), in_specs=in_specs + [pl.BlockSpec(memory_space=pl.ANY)] * len(behind),
        out_specs=out_specs, out_shape=out_shape, compiler_params=_params("arbitrary" if accs else "parallel"),
    )(*rows, *consts, *behind)


def _rope_tables(seq):
    half = MLA_ROPE // 2
    inv_freq = ROPE_THETA ** (-jnp.arange(half, dtype=F32) / half)
    ang = jnp.arange(seq, dtype=F32)[:, None] * inv_freq[None, :]
    cos, sin, zero = jnp.cos(ang), jnp.sin(ang), jnp.zeros((seq, half), F32)
    t_c = jnp.concatenate([cos, cos, zero, zero], axis=1)
    t_s1 = jnp.concatenate([-sin, zero, zero, zero], axis=1)
    t_s2 = jnp.concatenate([zero, sin, zero, zero], axis=1)
    return t_c, t_s1, t_s2


def _rope(slab, t_c, t_s1, t_s2):
    return slab * t_c + pltpu.roll(slab, 96, 1) * t_s1 + pltpu.roll(slab, 32, 1) * t_s2


def _rope_t(d, t_c, t_s1, t_s2):
    return d * t_c + pltpu.roll(d * t_s1, 32, 1) + pltpu.roll(d * t_s2, 96, 1)


def _lower_bound(logits):
    l0, l1 = logits[0:1, :], logits[1:2, :]
    mx = jnp.maximum(l0, l1)
    e0, e1 = jnp.exp(l0 - mx), jnp.exp(l1 - mx)
    return e0 / (e0 + e1)


def _tri(n, lower):
    row = lax.broadcasted_iota(jnp.int32, (n, n), 0)
    col = lax.broadcasted_iota(jnp.int32, (n, n), 1)
    return (row >= col) if lower else (row <= col)


def _hgrn_fwd(zq, zf, zi, lb_logits, *, name):
    s, d = zq.shape
    h_n, c, hp, cps = d // LANES, HGRN_CHUNK, HGRN_HEADS_PER_STEP, HGRN_CHUNKS_PER_STEP
    nc = s // c

    def body(zq_ref, zf_ref, zi_ref, lb_ref, o_ref, st_ref, state_sc, b_sc):
        @pl.when(pl.program_id(1) == 0)
        def _():
            state_sc[...] = jnp.zeros_like(state_sc)

        lower = _tri(c, True)
        lower_f = lower.astype(F32)
        hs, pairs = range(hp), [(cc, hh) for cc in range(cps) for hh in range(hp)]
        sls = [slice(hh * LANES, (hh + 1) * LANES) for hh in hs]
        rws = [slice(cc * c, (cc + 1) * c) for cc in range(cps)]
        lb = [_lower_bound(lb_ref[:, sl]) for sl in sls]
        zq_v = {p: zq_ref[rws[p[0]], sls[p[1]]] for p in pairs}
        q = {p: zq_v[p] * _sigmoid(zq_v[p]) for p in pairs}
        f = {p: lb[p[1]] + (1.0 - lb[p[1]]) * _sigmoid(zf_ref[rws[p[0]], sls[p[1]]]) for p in pairs}
        k = {p: 1.0 - f[p] for p in pairs}
        v = {p: zi_ref[rws[p[0]], sls[p[1]]] for p in pairs}
        b = {p: _dot_f32(lower_f, jnp.log(f[p])) for p in pairs}
        for p in pairs:
            b_sc[p[0], p[1]] = b[p]
        qe = {p: q[p] * jnp.exp(b[p]) for p in pairs}
        scores = {p: [] for p in pairs}
        for i in range(c // HGRN_SUB):
            lo = i * HGRN_SUB
            for p in pairs:
                ref = b_sc[p[0], p[1], lo - 1:lo, :] if i > 0 else jnp.zeros((1, LANES), F32)
                qt = q[p][lo:lo + HGRN_SUB, :] * jnp.exp(b[p][lo:lo + HGRN_SUB, :] - ref)
                dec = jnp.exp(jnp.minimum(ref - b[p], EXP_CLAMP))
                scores[p].append(_dot(qt, k[p] * dec, _NT))
        o_intra = {p: _dot(jnp.where(lower, jnp.concatenate(scores[p], axis=0), 0.0), v[p], _NN) for p in pairs}
        bl = {p: b_sc[p[0], p[1], c - 1:c, :] for p in pairs}
        k_end = {p: k[p] * jnp.exp(bl[p] - b[p]) for p in pairs}
        state = [state_sc[hh] for hh in hs]
        for cc in range(cps):
            for hh in hs:
                st_ref[hh, cc] = state[hh]
                o_ref[rws[cc], sls[hh]] = _dot(qe[cc, hh], state[hh], _NT) + o_intra[cc, hh]
            state = [state[hh] * jnp.exp(bl[cc, hh]) + _dot(v[cc, hh], k_end[cc, hh], _TN) for hh in hs]
        for hh in hs:
            state_sc[hh] = state[hh]

    tile = pl.BlockSpec((cps * c, hp * LANES), lambda h, i: (i, h))
    return pl.pallas_call(
        body, name=name, grid=(h_n // hp, nc // cps),
        in_specs=[tile, tile, tile, pl.BlockSpec((2, hp * LANES), lambda h, i: (0, h))],
        out_specs=[tile, pl.BlockSpec((hp, cps, LANES, LANES), lambda h, i: (h, i, 0, 0))],
        out_shape=[jax.ShapeDtypeStruct((s, d), F32), jax.ShapeDtypeStruct((h_n, nc, LANES, LANES), F32)],
        scratch_shapes=[pltpu.VMEM((hp, LANES, LANES), F32), pltpu.VMEM((cps, hp, c, LANES), F32)],
        compiler_params=_params("parallel", "arbitrary"),
    )(zq, zf, zi, lb_logits)


def _hgrn_bwd(zq, zf, zi, lb_logits, states, do, after, *, name):
    s, d = zq.shape
    h_n, c, hp, cps = d // LANES, HGRN_CHUNK, HGRN_HEADS_PER_STEP, HGRN_CHUNKS_PER_STEP
    nc = s // c
    n_steps = nc // cps

    def body(zq_ref, zf_ref, zi_ref, lb_ref, st_ref, do_ref, _, dzq_ref, dzf_ref, dzi_ref, dlb_ref, dstate_sc, b_sc):
        @pl.when(pl.program_id(1) == 0)
        def _():
            dstate_sc[...] = jnp.zeros_like(dstate_sc)
            dlb_ref[...] = jnp.zeros_like(dlb_ref)

        lower, upper = _tri(c, True), _tri(c, False).astype(F32)
        lower_f = lower.astype(F32)
        last_row = lax.broadcasted_iota(jnp.int32, (c, LANES), 0) == c - 1
        hs, pairs = range(hp), [(cc, hh) for cc in range(cps) for hh in range(hp)]
        sls = [slice(hh * LANES, (hh + 1) * LANES) for hh in hs]
        rws = [slice(cc * c, (cc + 1) * c) for cc in range(cps)]
        lb = [_lower_bound(lb_ref[:, sl]) for sl in sls]
        zq_v = {p: zq_ref[rws[p[0]], sls[p[1]]] for p in pairs}
        sq = {p: _sigmoid(zq_v[p]) for p in pairs}
        q = {p: zq_v[p] * sq[p] for p in pairs}
        sf = {p: _sigmoid(zf_ref[rws[p[0]], sls[p[1]]]) for p in pairs}
        f = {p: lb[p[1]] + (1.0 - lb[p[1]]) * sf[p] for p in pairs}
        k = {p: 1.0 - f[p] for p in pairs}
        v = {p: zi_ref[rws[p[0]], sls[p[1]]] for p in pairs}
        d_o = {p: do_ref[rws[p[0]], sls[p[1]]] for p in pairs}
        b = {p: _dot_f32(lower_f, jnp.log(f[p])) for p in pairs}
        s0t = {p: st_ref[p[1], p[0]] for p in pairs}
        for p in pairs:
            b_sc[p[0], p[1]] = b[p]
        bl = {p: b_sc[p[0], p[1], c - 1:c, :] for p in pairs}
        eb = {p: jnp.exp(b[p]) for p in pairs}
        ebl = {p: jnp.exp(bl[p]) for p in pairs}
        dec_end = {p: jnp.exp(bl[p] - b[p]) for p in pairs}
        da = {p: jnp.where(lower, _dot(d_o[p], v[p], _NT), 0.0) for p in pairs}
        dq = {p: _dot(d_o[p], s0t[p], _NN) * eb[p] for p in pairs}
        dstate_in = {p: _dot(d_o[p], q[p] * eb[p], _TN) for p in pairs}
        dk_intra = {p: jnp.zeros((c, LANES), F32) for p in pairs}
        scores, dq_blocks = {p: [] for p in pairs}, {p: [] for p in pairs}
        for i in range(c // HGRN_SUB):
            lo = i * HGRN_SUB
            for p in pairs:
                ref = b_sc[p[0], p[1], lo - 1:lo, :] if i > 0 else jnp.zeros((1, LANES), F32)
                grow = jnp.exp(b[p][lo:lo + HGRN_SUB, :] - ref)
                qt = q[p][lo:lo + HGRN_SUB, :] * grow
                dec = jnp.exp(jnp.minimum(ref - b[p], EXP_CLAMP))
                kd = k[p] * dec
                scores[p].append(_dot(qt, kd, _NT))
                da_i = da[p][lo:lo + HGRN_SUB, :]
                dq_blocks[p].append(_dot_f32(da_i, kd, _NN) * grow)
                dk_intra[p] = dk_intra[p] + _dot_f32(da_i, qt, _TN) * dec
        dv_intra = {p: _dot(jnp.where(lower, jnp.concatenate(scores[p], axis=0), 0.0), d_o[p], _TN) for p in pairs}
        dq = {p: dq[p] + jnp.concatenate(dq_blocks[p], axis=0) for p in pairs}
        q_dq = {p: q[p] * dq[p] for p in pairs}
        for p in pairs:
            dzq_ref[rws[p[0]], sls[p[1]]] = (dq[p] * sq[p] * (1.0 + zq_v[p] * (1.0 - sq[p]))).astype(BF16)
        dstate = [dstate_sc[hh] for hh in hs]
        for cc in reversed(range(cps)):
            ps = [(cc, hh) for hh in hs]
            dk_state = [_dot(v[p], dstate[p[1]], _NN) * dec_end[p] for p in ps]
            dv = [dv_intra[p] + _dot(k[p] * dec_end[p], dstate[p[1]], _NT) for p in ps]
            dk = [dk_intra[p] + dk_state[p[1]] for p in ps]
            db_last = [jnp.sum(k[p] * dk_state[p[1]], axis=0, keepdims=True)
                       + ebl[p] * jnp.sum(s0t[p] * dstate[p[1]], axis=0, keepdims=True) for p in ps]
            db = [q_dq[p] - k[p] * dk[p[1]] + jnp.where(last_row, db_last[p[1]], 0.0) for p in ps]
            df = [_dot_f32(upper, db[p[1]]) / f[p] - dk[p[1]] for p in ps]
            for p in ps:
                hh = p[1]
                dzf_ref[rws[cc], sls[hh]] = (df[hh] * (1.0 - lb[hh]) * sf[p] * (1.0 - sf[p])).astype(BF16)
                dlb_ref[:, sls[hh]] += jnp.sum(df[hh] * (1.0 - sf[p]), axis=0, keepdims=True)
                dzi_ref[rws[cc], sls[hh]] = dv[hh].astype(BF16)
            dstate = [dstate[p[1]] * ebl[p] + dstate_in[p] for p in ps]
        for hh in hs:
            dstate_sc[hh] = dstate[hh]

    tile = pl.BlockSpec((cps * c, hp * LANES), lambda h, i: (n_steps - 1 - i, h))
    out = jax.ShapeDtypeStruct((s, d), BF16)
    return pl.pallas_call(
        body, name=name, grid=(h_n // hp, n_steps),
        in_specs=[tile, tile, tile, pl.BlockSpec((2, hp * LANES), lambda h, i: (0, h)),
                  pl.BlockSpec((hp, cps, LANES, LANES), lambda h, i: (h, n_steps - 1 - i, 0, 0)), tile,
                  pl.BlockSpec(memory_space=pl.ANY)],
        out_specs=[tile, tile, tile, pl.BlockSpec((1, hp * LANES), lambda h, i: (0, h))],
        out_shape=[out, out, out, jax.ShapeDtypeStruct((1, d), F32)],
        scratch_shapes=[pltpu.VMEM((hp, LANES, LANES), F32), pltpu.VMEM((cps, hp, c, LANES), F32)],
        compiler_params=_params("parallel", "arbitrary"),
    )(zq, zf, zi, lb_logits, states, do, after)


ATTN_SUB_ROWS = 256
LOG2E = 1.4426950408889634
LN2 = 0.6931471805599453
Q_PRESCALE = ATTN_SCALE * LOG2E


def _attn_tile(s):
    return min(2048, max(128, s // 2))


def _causal_pairs(n, q_major):
    pairs = [(i, j) for i in range(n) for j in range(i + 1)] if q_major else [(i, j) for j in range(n) for i in range(j, n)]
    return jnp.asarray([p[0] for p in pairs], jnp.int32), jnp.asarray([p[1] for p in pairs], jnp.int32)


def _sub_scores(qn_ref, qr_ref, k, r, sub, t, diagonal):
    q = jnp.concatenate([qn_ref[r:r + sub, :], qr_ref[r:r + sub, :]], axis=1)
    if not diagonal:
        return q, _dot(q, k, _NT)
    cols = r + sub
    keep = lax.broadcasted_iota(jnp.int32, (sub, cols), 1) <= r + lax.broadcasted_iota(jnp.int32, (sub, cols), 0)
    return q, jnp.where(keep, _dot(q, k[:cols], _NT), -jnp.inf)


def _attn_fwd(qn, qr, kn, kr, v, *, name):
    s, t = qn.shape[0], _attn_tile(qn.shape[0])
    sub = min(t, ATTN_SUB_ROWS)
    q_blk, k_blk = _causal_pairs(s // t, True)

    def body(qi_ref, kj_ref, qn_ref, qr_ref, kn_ref, kr_ref, v_ref, o_ref, lse_ref, m_sc, l_sc, acc_sc):
        p_id = pl.program_id(1)
        i, j = qi_ref[p_id], kj_ref[p_id]

        @pl.when(j == 0)
        def _():
            m_sc[...] = jnp.full_like(m_sc, -jnp.inf)
            l_sc[...] = jnp.zeros_like(l_sc)
            acc_sc[...] = jnp.zeros_like(acc_sc)

        def update(diagonal):
            k = jnp.concatenate([kn_ref[...], kr_ref[...]], axis=1)
            v = v_ref[...]
            starts = list(range(0, t, sub))
            scs = [_sub_scores(qn_ref, qr_ref, k, r, sub, t, diagonal)[1] for r in starts]
            ps, alphas = [], []
            for r, sc in zip(starts, scs):
                m_prev = m_sc[r:r + sub, :]
                m_new = jnp.maximum(m_prev, jnp.max(sc, axis=1, keepdims=True))
                alpha = jnp.exp2(m_prev - m_new)
                p = jnp.exp2(sc - m_new[:, :1])
                l_sc[r:r + sub, :] = alpha * l_sc[r:r + sub, :] + jnp.sum(p, axis=1, keepdims=True)
                m_sc[r:r + sub, :] = m_new
                ps.append(p)
                alphas.append(alpha)
            for r, p, alpha in zip(starts, ps, alphas):
                acc_sc[r:r + sub, :] = alpha * acc_sc[r:r + sub, :] + _dot(p, v[:p.shape[1]], _NN)

        @pl.when(j < i)
        def _():
            update(False)

        @pl.when(j == i)
        def _():
            update(True)
            o_ref[...] = (acc_sc[...] / l_sc[...]).astype(BF16)
            lse_ref[...] = m_sc[...] + jnp.log(l_sc[...]) * LOG2E

    q_spec = pl.BlockSpec((t, LANES), lambda h, p, qi, kj: (qi[p], h))
    k_spec = pl.BlockSpec((t, LANES), lambda h, p, qi, kj: (kj[p], h))
    kr_spec = pl.BlockSpec((t, LANES), lambda h, p, qi, kj: (kj[p], 0))
    stat = pltpu.VMEM((t, LANES), F32)
    return pl.pallas_call(
        body, name=name,
        grid_spec=pltpu.PrefetchScalarGridSpec(
            num_scalar_prefetch=2, grid=(MLA_HEADS, q_blk.shape[0]),
            in_specs=[q_spec, q_spec, k_spec, kr_spec, k_spec], out_specs=[q_spec, q_spec],
            scratch_shapes=[stat, stat, stat]),
        out_shape=[jax.ShapeDtypeStruct(qn.shape, BF16), jax.ShapeDtypeStruct(qn.shape, F32)],
        compiler_params=_params("parallel", "arbitrary"),
    )(q_blk, k_blk, qn, qr, kn, kr, v)


def _attn_bwd(qn, qr, kn, kr, v, do, lse, delta, *, name):
    s, t = qn.shape[0], _attn_tile(qn.shape[0])
    n, sub = s // t, min(t, ATTN_SUB_ROWS)
    q_blk, k_blk = _causal_pairs(n, False)

    def body(qi_ref, kj_ref, qn_ref, qr_ref, kn_ref, kr_ref, v_ref, do_ref, lse_ref, delta_ref,
             dqn_ref, dqr_ref, dkn_ref, dv_ref, dkr_ref, dk_sc, dv_sc):
        p_id = pl.program_id(1)
        i, j = qi_ref[p_id], kj_ref[p_id]

        @pl.when(p_id == 0)
        def _():
            dqn_ref[...] = jnp.zeros_like(dqn_ref)
            dqr_ref[...] = jnp.zeros_like(dqr_ref)

        @pl.when(i == j)
        def _():
            dk_sc[...] = jnp.zeros_like(dk_sc)
            dv_sc[...] = jnp.zeros_like(dv_sc)

        def accumulate(diagonal):
            k = jnp.concatenate([kn_ref[...], kr_ref[...]], axis=1)
            v = v_ref[...]
            starts = list(range(0, t, sub))
            qs, d_os, scs, dps = [], [], [], []
            for r in starts:
                q, sc = _sub_scores(qn_ref, qr_ref, k, r, sub, t, diagonal)
                d_o = do_ref[r:r + sub, :]
                qs.append(q)
                d_os.append(d_o)
                scs.append(sc)
                dps.append(_dot(d_o, v[:sc.shape[1]], _NT))
            ps, dss = [], []
            for r, sc, dp in zip(starts, scs, dps):
                p = jnp.exp2(sc - lse_ref[r:r + sub, :][:, :1])
                ps.append(p.astype(BF16))
                dss.append((p * (dp - delta_ref[r:r + sub, :][:, :1])).astype(BF16))
            for r, q, d_o, p, ds in zip(starts, qs, d_os, ps, dss):
                cols = p.shape[1]
                dv_sc[:cols, :] += _dot(p, d_o, _TN)
                dk_sc[:cols, :] += _dot(ds, q, _TN)
                dq = _dot(ds, k[:cols], _NN) * ATTN_SCALE
                rows = pl.ds(pl.multiple_of(i * t + r, sub), sub)
                dqn_ref[rows, :] += dq[:, :LANES]
                dqr_ref[rows, :] += dq[:, LANES:]

        @pl.when(j < i)
        def _():
            accumulate(False)

        @pl.when(j == i)
        def _():
            accumulate(True)

        @pl.when(i == n - 1)
        def _():
            dkn_ref[...] = (dk_sc[:, :LANES] * LN2).astype(BF16)
            dkr_ref[...] = dk_sc[:, LANES:] * LN2
            dv_ref[...] = dv_sc[...].astype(BF16)

    q_spec = pl.BlockSpec((t, LANES), lambda h, p, qi, kj: (qi[p], h))
    k_spec = pl.BlockSpec((t, LANES), lambda h, p, qi, kj: (kj[p], h))
    kr_spec = pl.BlockSpec((t, LANES), lambda h, p, qi, kj: (kj[p], 0))
    head_spec = pl.BlockSpec((s, LANES), lambda h, p, qi, kj: (0, h))
    f32_out, bf16_out = jax.ShapeDtypeStruct(qn.shape, F32), jax.ShapeDtypeStruct(qn.shape, BF16)
    return pl.pallas_call(
        body, name=name,
        grid_spec=pltpu.PrefetchScalarGridSpec(
            num_scalar_prefetch=2, grid=(MLA_HEADS, q_blk.shape[0]),
            in_specs=[q_spec, q_spec, k_spec, kr_spec, k_spec, q_spec, q_spec, q_spec],
            out_specs=[head_spec, head_spec, k_spec, k_spec, k_spec],
            scratch_shapes=[pltpu.VMEM((t, 2 * LANES), F32), pltpu.VMEM((t, LANES), F32)]),
        out_shape=[f32_out, f32_out, bf16_out, bf16_out, f32_out],
        compiler_params=_params("parallel", "arbitrary"),
    )(q_blk, k_blk, qn, qr, kn, kr, v, do, lse, delta)


def _exchange(arrs, *, scatter, name):
    n = len(arrs)
    out_shape = [jax.ShapeDtypeStruct(a.shape if scatter else (N_DEV, *a.shape), a.dtype) for a in arrs]

    def body(*refs):
        ins, outs = refs[:n], refs[n:2 * n]
        send_sems, recv_sems, local_sems = refs[2 * n:]
        x, y, c = lax.axis_index("x"), lax.axis_index("y"), lax.axis_index("c")
        me = 4 * x + 2 * y + c
        copies = []
        for k in range(n):
            local = pltpu.make_async_copy(ins[k].at[me] if scatter else ins[k], outs[k].at[me], local_sems.at[k])
            local.start()
            copies.append(local)
            for d in range(1, N_DEV):
                px, py, pc = (x + (d >> 2)) % 2, (y + ((d >> 1) & 1)) % 2, (c + (d & 1)) % 2
                peer = 4 * px + 2 * py + pc
                remote = pltpu.make_async_remote_copy(
                    src_ref=ins[k].at[peer] if scatter else ins[k], dst_ref=outs[k].at[me],
                    send_sem=send_sems.at[k, d - 1], recv_sem=recv_sems.at[k, d - 1],
                    device_id=(px, py, pc), device_id_type=pl.DeviceIdType.MESH)
                remote.start()
                copies.append(remote)
        for cp in copies:
            cp.wait()

    any_spec = pl.BlockSpec(memory_space=pl.ANY)
    return pl.pallas_call(
        body, name=name, in_specs=[any_spec] * n, out_specs=[any_spec] * n, out_shape=out_shape,
        scratch_shapes=[pltpu.SemaphoreType.DMA((n, N_DEV - 1)), pltpu.SemaphoreType.DMA((n, N_DEV - 1)),
                        pltpu.SemaphoreType.DMA((n,))],
    )(*arrs)


def _peers(x, y, c):
    out = []
    for d in range(1, N_DEV):
        px, py, pc = (x + (d >> 2)) % 2, (y + ((d >> 1) & 1)) % 2, (c + (d & 1)) % 2
        out.append(((px, py, pc), 4 * px + 2 * py + pc))
    return out


CHIP_LEVEL_PEERS = (1, 2, 4, 6)


def _exchange_copies(ins, lands, send_sems, recv_sems, scatter, chip_level=False):
    x, y, c = lax.axis_index("x"), lax.axis_index("y"), lax.axis_index("c")
    me = 4 * x + 2 * y + c
    local, remote = [], []
    for k in range(len(ins)):
        local.append(pltpu.make_async_copy(ins[k].at[me] if scatter else ins[k], lands[k].at[me],
                                           recv_sems.at[k * N_DEV + N_DEV - 1]))
        for d, (coords, peer) in enumerate(_peers(x, y, c)):
            if chip_level and d + 1 not in CHIP_LEVEL_PEERS:
                continue
            remote.append(pltpu.make_async_remote_copy(
                src_ref=ins[k].at[peer] if scatter else ins[k], dst_ref=lands[k].at[me],
                send_sem=send_sems.at[k * N_DEV + d], recv_sem=recv_sems.at[k * N_DEV + d],
                device_id=coords, device_id_type=pl.DeviceIdType.MESH))
    return local, remote


def _exchange_start(arrs, *, scatter, name, after=None, chip_level=False):
    n = len(arrs)
    hbm = pl.BlockSpec(memory_space=pltpu.HBM)
    sem = pl.BlockSpec(memory_space=pltpu.SEMAPHORE)
    lands = [lax.empty(a.shape if scatter else (N_DEV, *a.shape), a.dtype) for a in arrs]

    def body(*refs):
        ins, land_refs = refs[:n], refs[n:2 * n]
        first_out = 2 * n + (after is not None)
        send_sems, recv_sems, token = refs[first_out], refs[first_out + 1], refs[-1]
        local, remote = _exchange_copies(ins, land_refs, send_sems, recv_sems, scatter, chip_level)
        for cp in local + remote:
            cp.start()
        token[...] = jnp.zeros_like(token)

    operands = [pltpu.with_memory_space_constraint(a, pltpu.HBM) for a in list(arrs) + lands]
    behind = [] if after is None else [after]
    res = pl.pallas_call(
        body, name=name,
        out_shape=(pltpu.SemaphoreType.DMA((n * N_DEV,)), pltpu.SemaphoreType.DMA((n * N_DEV,)),
                   *[pltpu.HBM(o.shape, o.dtype) for o in operands], jax.ShapeDtypeStruct((8, LANES), F32)),
        in_specs=[hbm] * (2 * n) + [pl.BlockSpec(memory_space=pl.ANY)] * len(behind),
        out_specs=(sem, sem, *[hbm] * (2 * n), pl.BlockSpec(memory_space=pltpu.VMEM)),
        input_output_aliases={i: 2 + i for i in range(2 * n)},
        compiler_params=pltpu.CompilerParams(has_side_effects=pltpu.SideEffectType.DATAFLOW_SIDE_EFFECTING),
    )(*operands, *behind)
    return (res[0], res[1], list(res[2:2 + n]), list(res[2 + n:2 + 2 * n]), scatter, chip_level), res[-1]


def _exchange_wait(state, after, *, name):
    send_sems, recv_sems, ins, lands, scatter, chip_level = state
    n = len(ins)
    hbm = pl.BlockSpec(memory_space=pltpu.HBM)
    sem = pl.BlockSpec(memory_space=pltpu.SEMAPHORE)

    def body(*refs):
        in_refs, land_refs = refs[:n], refs[n:2 * n]
        local, remote = _exchange_copies(in_refs, land_refs, refs[2 * n], refs[2 * n + 1], scatter, chip_level)
        for cp in local:
            cp.wait()
        for cp in remote:
            cp.wait_send()
            cp.wait_recv()

    res = pl.pallas_call(
        body, name=name, out_shape=tuple(pltpu.HBM(o.shape, o.dtype) for o in ins + lands),
        in_specs=[hbm] * (2 * n) + [sem, sem, pl.BlockSpec(memory_space=pl.ANY)], out_specs=tuple([hbm] * (2 * n)),
        input_output_aliases={i: i for i in range(2 * n)},
        compiler_params=pltpu.CompilerParams(has_side_effects=pltpu.SideEffectType.DATAFLOW_SIDE_EFFECTING),
    )(*ins, *lands, send_sems, recv_sems, after)
    return list(res[n:])


def _chip_forward(lands, *, name):
    n = len(lands)

    def body(*refs):
        ins, outs, send_sems, recv_sems = refs[:n], refs[n:2 * n], refs[2 * n], refs[2 * n + 1]
        x, y, c = lax.axis_index("x"), lax.axis_index("y"), lax.axis_index("c")
        copies = []
        for k in range(n):
            for j, (dx, dy) in enumerate(((0, 1), (1, 0), (1, 1))):
                held = 4 * ((x + dx) % 2) + 2 * ((y + dy) % 2) + c
                cp = pltpu.make_async_remote_copy(
                    src_ref=ins[k].at[held], dst_ref=outs[k].at[held], send_sem=send_sems.at[k, j],
                    recv_sem=recv_sems.at[k, j], device_id=(x, y, 1 - c), device_id_type=pl.DeviceIdType.MESH)
                cp.start()
                copies.append(cp)
        for cp in copies:
            cp.wait()

    any_spec = pl.BlockSpec(memory_space=pl.ANY)
    return pl.pallas_call(
        body, name=name, in_specs=[any_spec] * n, out_specs=[any_spec] * n,
        out_shape=[jax.ShapeDtypeStruct(a.shape, a.dtype) for a in lands], input_output_aliases={k: k for k in range(n)},
        scratch_shapes=[pltpu.SemaphoreType.DMA((n, 3)), pltpu.SemaphoreType.DMA((n, 3))],
    )(*lands)


def _adam(w, terms, m, v, *, name):
    n_layers, r, c = w.shape
    tr = min(r, ADAM_ROW_TILE)
    assert r % tr == 0 and len(terms) == n_layers
    steps = r // tr

    def body(w_ref, *rest):
        t_refs, (m_ref, v_ref, g_out, d_out, m_out, v_out) = rest[:n_layers], rest[n_layers:]
        for layer, t_ref in enumerate(t_refs):
            @pl.when(pl.program_id(0) == layer)
            def _(t_ref=t_ref):
                g = t_ref[0].astype(F32)
                for s in range(1, t_ref.shape[0]):
                    g = g + t_ref[s].astype(F32)
                m1 = ADAM_B1 * m_ref[...] + (1.0 - ADAM_B1) * g
                v1 = ADAM_B2 * v_ref[...] + (1.0 - ADAM_B2) * jnp.square(g)
                m_hat = m1 / (1.0 - ADAM_B1 ** ADAM_STEP)
                v_hat = v1 / (1.0 - ADAM_B2 ** ADAM_STEP)
                g_out[...] = g
                d_out[...] = -ADAM_LR * (m_hat / (jnp.sqrt(v_hat) + ADAM_EPS) + ADAM_WD * w_ref[...])
                m_out[...] = m1
                v_out[...] = v1

    def term_spec(layer, t):
        return pl.BlockSpec((t.shape[0], tr, c),
                            lambda l, i: (0, jnp.where(l == layer, i, jnp.where(l < layer, 0, steps - 1)), 0))

    spec = pl.BlockSpec((None, tr, c), lambda l, i: (l, i, 0))
    out = jax.ShapeDtypeStruct(w.shape, F32)
    return pl.pallas_call(
        body, name=name, grid=(n_layers, steps),
        in_specs=[spec] + [term_spec(layer, t) for layer, t in enumerate(terms)] + [spec, spec], out_specs=[spec] * 4,
        out_shape=[out] * 4, compiler_params=_params("arbitrary", "arbitrary"),
    )(w, *terms, m, v)


def _sum_terms(terms, *, name):
    n, _, p = terms.shape

    def body(t_ref, o_ref):
        acc = t_ref[0]
        for s in range(1, n):
            acc = acc + t_ref[s]
        o_ref[...] = acc

    return pl.pallas_call(body, name=name, out_shape=jax.ShapeDtypeStruct((1, p), F32))(terms)


def _lb_logits_grad(dlb, logits, *, name):
    def body(dlb_ref, l_ref, o_ref):
        lb = _lower_bound(l_ref[...])
        d0 = dlb_ref[...] * lb * (1.0 - lb)
        o_ref[...] = jnp.concatenate([d0, -d0], axis=0)

    return pl.pallas_call(body, name=name, out_shape=jax.ShapeDtypeStruct(logits.shape, F32))(dlb, logits)


def _silu_grad(z):
    sg = _sigmoid(z)
    return sg * (1.0 + z * (1.0 - sg))


def _head_norm_gate(o, zg, gn):
    outs = []
    for h in range(HGRN_HEADS):
        sl = slice(h * LANES, (h + 1) * LANES)
        zg_h = zg[:, sl]
        outs.append(_rms(o[:, sl], gn) * (zg_h * _sigmoid(zg_h)))
    return (jnp.concatenate(outs, axis=1),)


def _head_norm_gate_bwd(o, zg, dm, gn):
    do_parts, dzg_parts, dgn = [], [], jnp.zeros((1, LANES), F32)
    for h in range(HGRN_HEADS):
        sl = slice(h * LANES, (h + 1) * LANES)
        o_h, zg_h, dm_h = o[:, sl], zg[:, sl], dm[:, sl]
        gate = zg_h * _sigmoid(zg_h)
        do_h, dgn_h = _rms_bwd(o_h, gn, dm_h * gate)
        dgn = dgn + dgn_h
        do_parts.append(do_h)
        dzg_parts.append(dm_h * _rms(o_h, gn) * _silu_grad(zg_h))
    return jnp.concatenate(do_parts, axis=1), jnp.concatenate(dzg_parts, axis=1), dgn


def _rope_slabs(x, t_c, t_s1, t_s2, transpose):
    fn = _rope_t if transpose else _rope
    return jnp.concatenate(
        [fn(x[:, h * LANES:(h + 1) * LANES], t_c, t_s1, t_s2) for h in range(x.shape[1] // LANES)], axis=1)


def _loss_head(h, tgt, w):
    d = h.shape[1]
    r = lax.rsqrt(jnp.mean(h * h, axis=-1, keepdims=True) + EPS)
    xh = h * r
    err = xh * w - tgt
    loss = 0.5 * jnp.sum(jnp.mean(err * err, axis=-1, keepdims=True), axis=0, keepdims=True)
    dy = err / d
    dxh = dy * w
    dh = r * (dxh - xh * jnp.mean(dxh * xh, axis=-1, keepdims=True))
    return dh, dh, jnp.sum(dy * xh, axis=0, keepdims=True), jnp.broadcast_to(loss, (1, LANES))


def _mlp_fwd(h, norm, w_up, w_down, tag, loss_head=None):
    d = h.shape[1]

    def up(x, g, wu):
        x_n = _rms(x, g).astype(BF16)
        return x_n, jnp.concatenate([jnp.square(jnp.maximum(_dot(x_n, wu[j], _NN), 0.0)) for j in range(wu.shape[0])],
                                    axis=1)

    xn, act = _rowcall(up, [h], [norm, w_up], [(d, BF16), (w_up.shape[0] * w_up.shape[2], BF16)], [], tr=512,
                       name=f"{tag}_up")
    if callable(w_down):
        w_down = w_down(act)
    if loss_head is None:
        return _rowcall(lambda a, res, wd: (res + _dot(a, wd, _NN),), [act, h], [w_down], [(d, F32)], [],
                        tr=512, name=f"{tag}_down")[0], (h, xn, act)
    tgt, final_norm = loss_head

    def down_and_loss(a, res, t, wd, g):
        return _loss_head(res + _dot(a, wd, _NN), t, g)

    return _rowcall(down_and_loss, [act, h, tgt], [w_down, final_norm], [(d, F32), (d, BF16)], [d, LANES],
                    name=f"{tag}_down_loss"), (h, xn, act)


def _mlp_bwd(dh_out, dh_out_bf, saved, norm, w_up, w_down, tag, after=None):
    h, xn, act = saved
    d = h.shape[1]
    du = _rowcall(lambda dres, a, wd: (_dot(dres, wd, _NT) * (2.0 * jnp.sqrt(a.astype(F32))),), [dh_out_bf, act],
                  [w_down], [(act.shape[1], BF16)], [], tr=512, after=after, name=f"{tag}_bwd_du")[0]
    dw_down = _mm(act, dh_out_bf, mode="tn", name=f"{tag}_bwd_wdown")
    dw_up = _mm(xn, du, mode="tn", col_shards=w_up.shape[0], name=f"{tag}_bwd_wup")

    def up_norm_bwd(x, d_u, dres, g, wu):
        cols = wu.shape[2]
        dxn = _dot(d_u[:, :cols], wu[0], _NT)
        for j in range(1, wu.shape[0]):
            dxn = dxn + _dot(d_u[:, j * cols:(j + 1) * cols], wu[j], _NT)
        dx, dw = _rms_bwd(x, g, dxn)
        return dx + dres, dx + dres, dw

    dh, dh_bf, dnorm = _rowcall(up_norm_bwd, [h, du, dh_out], [norm, w_up], [(d, F32), (d, BF16)], [d], tr=512,
                                name=f"{tag}_bwd_dxn")
    return dh, dh_bf, dnorm, dw_up, dw_down


def _row_major(g):
    return g.reshape(g.shape[0] * g.shape[1], g.shape[2])


def _col_major(g):
    return jnp.transpose(g, (1, 0, 2)).reshape(g.shape[1], g.shape[0] * g.shape[2])


def _col_terms(dw):
    k, n = dw.shape
    return jnp.transpose(dw.reshape(k, N_DEV, n // N_DEV), (1, 0, 2))


def _row_terms(dw):
    return dw.reshape(N_DEV, dw.shape[0] // N_DEV, dw.shape[1])


def kernel(x, hgrn_norm, hgrn_w_q, hgrn_w_f, hgrn_w_i, hgrn_w_g, hgrn_g_norm, hgrn_w_o, hgrn_lb_logits, mla_norm, mla_w_dq, mla_q_norm, mla_w_uq, mla_w_o, kv_in_norm, kv_w_dkv, kv_norm, kv_w_uk, kv_w_uv, mlp_norm, mlp_w_up, mlp_w_down, final_norm, loss_target, m_hgrn_norm, m_hgrn_w_q, m_hgrn_w_f, m_hgrn_w_i, m_hgrn_w_g, m_hgrn_g_norm, m_hgrn_w_o, m_hgrn_lb_logits, m_mla_norm, m_mla_w_dq, m_mla_q_norm, m_mla_w_uq, m_mla_w_o, m_kv_in_norm, m_kv_w_dkv, m_kv_norm, m_kv_w_uk, m_kv_w_uv, m_mlp_norm, m_mlp_w_up, m_mlp_w_down, m_final_norm, v_hgrn_norm, v_hgrn_w_q, v_hgrn_w_f, v_hgrn_w_i, v_hgrn_w_g, v_hgrn_g_norm, v_hgrn_w_o, v_hgrn_lb_logits, v_mla_norm, v_mla_w_dq, v_mla_q_norm, v_mla_w_uq, v_mla_w_o, v_kv_in_norm, v_kv_w_dkv, v_kv_norm, v_kv_w_uk, v_kv_w_uv, v_mlp_norm, v_mlp_w_up, v_mlp_w_down, v_final_norm):
    given = dict(locals())
    weight_names = ["hgrn_norm", "hgrn_w_q", "hgrn_w_f", "hgrn_w_i", "hgrn_w_g", "hgrn_g_norm", "hgrn_w_o",
                    "hgrn_lb_logits", "mla_norm", "mla_w_dq", "mla_q_norm", "mla_w_uq", "mla_w_o", "kv_in_norm",
                    "kv_w_dkv", "kv_norm", "kv_w_uk", "kv_w_uv", "mlp_norm", "mlp_w_up", "mlp_w_down", "final_norm"]
    me = 4 * lax.axis_index("x") + 2 * lax.axis_index("y") + lax.axis_index("c")
    xs, tgt = x[0], loss_target[0]
    seq, d_model = xs.shape
    n_heads, hd = MLA_HEADS, LANES

    big_local = {
        "hgrn_w_q": hgrn_w_q[0], "hgrn_w_f": hgrn_w_f[0], "hgrn_w_i": hgrn_w_i[0], "hgrn_w_g": hgrn_w_g[0],
        "hgrn_w_o": hgrn_w_o[0], "mla_w_dq": mla_w_dq[0], "mla_w_uq": mla_w_uq[0], "mla_w_o": mla_w_o[0],
        "kv_w_dkv": kv_w_dkv, "kv_w_uk": kv_w_uk, "kv_w_uv": kv_w_uv,
        "mlp_w_up0": mlp_w_up[0], "mlp_w_up1": mlp_w_up[1], "mlp_w_down0": mlp_w_down[0], "mlp_w_down1": mlp_w_down[1],
    }
    big_names = list(big_local)
    col_sharded = {"mla_w_uq", "kv_w_uk", "kv_w_uv"}
    shard_major = {"mlp_w_up0", "mlp_w_up1"}
    vec_local = jnp.concatenate([hgrn_norm, hgrn_lb_logits], axis=0)
    first_names = ["hgrn_w_q", "hgrn_w_f", "hgrn_w_i"]
    proj_names = first_names + ["hgrn_w_g"]
    later_names = {"hgrn_o": ["hgrn_w_g", "hgrn_w_o"], "up0": ["mlp_w_up0"], "down0": ["mlp_w_down0"],
                   "mla": ["kv_w_dkv", "kv_w_uk", "kv_w_uv", "mla_w_dq", "mla_w_uq", "mla_w_o"],
                   "mlp1": ["mlp_w_up1", "mlp_w_down1"]}

    def unshard(names, arrays):
        return {k: (a if k in shard_major else _col_major(a) if k in col_sharded else _row_major(a))
                for k, a in zip(names, arrays)}

    two_level = {"down0", "mla"}
    first_state, token = _exchange_start([big_local[k].astype(BF16) for k in first_names] + [vec_local], scatter=False,
                                         chip_level=True, name="gather_first_start")
    gather_state = {}
    for tag, names in later_names.items():
        gather_state[tag], token = _exchange_start([big_local[k].astype(BF16) for k in names], scatter=False,
                                                   chip_level=tag in two_level, after=token, name=f"gather_{tag}_start")

    def gather_wait(tag, after):
        landed = _exchange_wait(gather_state[tag], after, name=f"gather_{tag}_wait")
        if tag in two_level:
            landed = _chip_forward(landed, name=f"gather_{tag}_forward")
        w.update(unshard(later_names[tag], landed))
        return [w[k] for k in later_names[tag]]

    gathered = _chip_forward(_exchange_wait(first_state, token, name="gather_first_wait"), name="gather_first_forward")
    w = unshard(first_names, gathered[:-1])
    vec_full = jnp.transpose(gathered[-1], (1, 0, 2)).reshape(3, d_model)
    hgrn_norm_full, lb_logits_full = vec_full[0:1], vec_full[1:3]
    t_c, t_s1, t_s2 = _rope_tables(seq)
    kv_lora = kv_w_uk.shape[0]

    def hgrn_proj(a, g, *weights):
        xn = _rms(a, g).astype(BF16)
        return (xn, *[_dot(xn, wt, _NN) for wt in weights])

    xn0, zq, zf, zi = _rowcall(hgrn_proj, [xs], [hgrn_norm_full] + [w[k] for k in first_names],
                               [(d_model, BF16)] + [(d_model, F32)] * 3, [], tr=512, name="hgrn_proj")
    o_rec, states = _hgrn_fwd(zq, zf, zi, lb_logits_full, name="hgrn_fwd")
    gather_wait("hgrn_o", o_rec)

    def gate_out(o, x_n, res, gn, wg, wo):
        z = _dot(x_n, wg, _NN)
        m = _head_norm_gate(o, z, gn)[0].astype(BF16)
        return z, m, res + _dot(m, wo, _NN)

    zg, mixed, h1 = _rowcall(gate_out, [o_rec, xn0, xs], [hgrn_g_norm, w["hgrn_w_g"], w["hgrn_w_o"]],
                             [(d_model, F32), (d_model, BF16), (d_model, F32)], [], name="hgrn_gate_out")
    h2, mlp0_saved = _mlp_fwd(h1, mlp_norm[0:1], gather_wait("up0", h1)[0], lambda act: gather_wait("down0", act)[0],
                              "mlp0")
    gather_wait("mla", h2)
    w_uq3 = w["mla_w_uq"].reshape(-1, n_heads, MLA_NOPE + MLA_ROPE)
    w_uq_nope = w_uq3[:, :, :MLA_NOPE].reshape(-1, n_heads * hd)
    w_uq_rope = jnp.pad(w_uq3[:, :, MLA_NOPE:], ((0, 0), (0, 0), (0, hd - MLA_ROPE))).reshape(-1, n_heads * hd)
    w_dkv_pad = jnp.pad(w["kv_w_dkv"], ((0, 0), (0, kv_lora + hd - w["kv_w_dkv"].shape[1])))

    q_lora, qk_cols = w["mla_w_dq"].shape[1], n_heads * hd

    def mla_qkv(a, tc, ts1, ts2, g_kv_in, g_mla, g_q, g_kv, wdq, wn, wr, wdkv, wuk, wuv):
        h_n, x_n = _rms(a, g_kv_in).astype(BF16), _rms(a, g_mla).astype(BF16)
        cq = _dot(x_n, wdq, _NN)
        cq_n = _rms(cq, g_q).astype(BF16)
        q_nope = _dot(cq_n, wn, _NN) * Q_PRESCALE
        q_rope = _rope_slabs(_dot(cq_n, wr, _NN) * Q_PRESCALE, tc, ts1, ts2, False)
        c_all = _dot(h_n, wdkv, _NN)
        lat = _rms(c_all[:, :kv_lora], g_kv).astype(BF16)
        return (h_n, x_n, cq, cq_n, q_nope, q_rope, c_all, lat, _rope(c_all[:, kv_lora:], tc, ts1, ts2),
                _dot(lat, wuk, _NN), _dot(lat, wuv, _NN))

    hn, xn2, cq_pre, c_q, qn, qr, ckr, c_kv, kr, kn, vv = _rowcall(
        mla_qkv, [h2, t_c, t_s1, t_s2],
        [kv_in_norm[None, :], mla_norm, mla_q_norm, kv_norm[None, :], w["mla_w_dq"], w_uq_nope, w_uq_rope, w_dkv_pad,
         w["kv_w_uk"], w["kv_w_uv"]],
        [(d_model, BF16), (d_model, BF16), (q_lora, F32), (q_lora, BF16), (qk_cols, BF16), (qk_cols, BF16),
         (kv_lora + hd, F32), (kv_lora, BF16), (hd, BF16), (qk_cols, BF16), (qk_cols, BF16)], [], tr=512, name="mla_qkv")
    o_att, lse = _attn_fwd(qn, qr, kn, kr, vv, name="attn_fwd")
    h3 = _rowcall(lambda o, res, wo: (res + _dot(o, wo, _NN),), [o_att, h2], [w["mla_w_o"]], [(d_model, F32)], [],
                  tr=512, name="attn_out")[0]
    gather_wait("mlp1", h3)
    (dh4, dh4_bf, g_final_norm, loss_part), mlp1_saved = _mlp_fwd(
        h3, mlp_norm[1:2], w["mlp_w_up1"], w["mlp_w_down1"], "mlp1", loss_head=(tgt, final_norm[None, :]))

    g = {}
    groups = {"mlp1": ["mlp_w_up1", "mlp_w_down1"],
              "mla": ["mla_w_o", "mla_w_uq", "mla_w_dq", "kv_w_uk", "kv_w_uv", "kv_w_dkv"],
              "mlp0": ["mlp_w_up0", "mlp_w_down0"],
              "hgrn_out": ["hgrn_w_o", "hgrn_w_g"],
              "hgrn_in": ["hgrn_w_q", "hgrn_w_f", "hgrn_w_i"]}
    scatter_state = {}

    def scatter_start(tag, after=None):
        scatter_state[tag], tok = _exchange_start(
            [g[k] if k in shard_major else (_col_terms if k in col_sharded else _row_terms)(g[k]) for k in groups[tag]],
            scatter=True, after=after,
            name=f"scatter_{tag}_start")
        return tok

    dh3, dh3_bf, g_mlp_norm1, g["mlp_w_up1"], g["mlp_w_down1"] = _mlp_bwd(
        dh4, dh4_bf, mlp1_saved, mlp_norm[1:2], w["mlp_w_up1"], w["mlp_w_down1"], "mlp1")
    def attn_out_bwd(dres, o, wo):
        d_o = _dot(dres, wo, _NT).astype(BF16)
        prod = d_o.astype(F32) * o.astype(F32)
        return d_o, jnp.concatenate([jnp.broadcast_to(jnp.sum(prod[:, h * hd:(h + 1) * hd], axis=1, keepdims=True),
                                                      (prod.shape[0], hd)) for h in range(n_heads)], axis=1)

    d_oatt, delta = _rowcall(attn_out_bwd, [dh3_bf, o_att], [w["mla_w_o"]], [(qk_cols, BF16), (qk_cols, F32)], [],
                             after=scatter_start("mlp1"), name="attn_out_bwd_x")
    g["mla_w_o"] = _mm(o_att, dh3_bf, mode="tn", name="attn_out_bwd_w")
    dqn, dqr, dkn, dvv, dkr = _attn_bwd(qn, qr, kn, kr, vv, d_oatt, lse, delta, name="attn_bwd")

    def q_path_bwd(cq, cq_n, x_n, d_qn, d_qr, tc, ts1, ts2, g_q, wdq, wn, wr):
        d_qn, d_qr = d_qn.astype(BF16), _rope_slabs(d_qr, tc, ts1, ts2, True).astype(BF16)
        d_cq, d_gq = _rms_bwd(cq, g_q, _dot(d_qn, wn, _NT) + _dot(d_qr, wr, _NT))
        d_cq = d_cq.astype(BF16)
        return _dot(d_cq, wdq, _NT), d_gq, _dot(x_n, d_cq, _TN), _dot(cq_n, d_qn, _TN), _dot(cq_n, d_qr, _TN)

    dxn2, g_q_norm, g_dq, g_uq_nope, g_uq_rope = _rowcall(
        q_path_bwd, [cq_pre, c_q, xn2, dqn, dqr, t_c, t_s1, t_s2], [mla_q_norm, w["mla_w_dq"], w_uq_nope, w_uq_rope],
        [(d_model, F32)], [q_lora, (d_model, q_lora), (q_lora, qk_cols), (q_lora, qk_cols)], tr=512, name="mla_q_bwd")
    g["mla_w_dq"] = g_dq.astype(GRAD_WIRE_DTYPE)
    g["mla_w_uq"] = jnp.concatenate([g_uq_nope.reshape(q_lora, n_heads, hd),
                                     g_uq_rope.reshape(q_lora, n_heads, hd)[:, :, :MLA_ROPE]],
                                    axis=2).reshape(q_lora, -1).astype(GRAD_WIRE_DTYPE)

    def kv_path_bwd(c_all, lat, h_n, d_kn, d_v, d_kr_heads, tc, ts1, ts2, a, d_xn2, dres,
                    g_kv, g_kv_in, g_mla, wdkv, wuk, wuv):
        d_lat, d_gkv = _rms_bwd(c_all[:, :kv_lora], g_kv, _dot(d_kn, wuk, _NT) + _dot(d_v, wuv, _NT))
        d_kr = d_kr_heads[:, :hd]
        for h in range(1, n_heads):
            d_kr = d_kr + d_kr_heads[:, h * hd:(h + 1) * hd]
        d_all = jnp.concatenate([d_lat, _rope_t(d_kr, tc, ts1, ts2)], axis=1).astype(BF16)
        dx1, d_gkv_in = _rms_bwd(a, g_kv_in, _dot(d_all, wdkv, _NT))
        dx2, d_gmla = _rms_bwd(a, g_mla, d_xn2)
        d_a = dx1 + dx2 + dres
        return (d_a, d_a, d_gkv, d_gkv_in, d_gmla, _dot(h_n, d_all, _TN), _dot(lat, d_kn, _TN), _dot(lat, d_v, _TN))

    dh2, dh2_bf, g_kv_norm, g_kv_in_norm, g_mla_norm, g_dkv, g_uk, g_uv = _rowcall(
        kv_path_bwd, [ckr, c_kv, hn, dkn, dvv, dkr, t_c, t_s1, t_s2, h2, dxn2, dh3],
        [kv_norm[None, :], kv_in_norm[None, :], mla_norm, w_dkv_pad, w["kv_w_uk"], w["kv_w_uv"]],
        [(d_model, F32), (d_model, BF16)],
        [kv_lora, d_model, d_model, (d_model, kv_lora + hd), (kv_lora, qk_cols), (kv_lora, qk_cols)], name="mla_kv_bwd")
    g["kv_w_dkv"] = g_dkv[:, :kv_w_dkv.shape[1]].astype(GRAD_WIRE_DTYPE)
    g["kv_w_uk"], g["kv_w_uv"] = g_uk.astype(GRAD_WIRE_DTYPE), g_uv.astype(GRAD_WIRE_DTYPE)
    dh1, dh1_bf, g_mlp_norm0, g["mlp_w_up0"], g["mlp_w_down0"] = _mlp_bwd(
        dh2, dh2_bf, mlp0_saved, mlp_norm[0:1], w["mlp_w_up0"], w["mlp_w_down0"], "mlp0", after=scatter_start("mla"))

    g["hgrn_w_o"] = _mm(mixed, dh1_bf, mode="tn", after=scatter_start("mlp0"), name="hgrn_out_bwd_w")
    do_rec, dzg, g_g_norm = _rowcall(
        lambda dres, o, z, wo, gn: _head_norm_gate_bwd(o, z, _dot(dres, wo, _NT), gn), [dh1_bf, o_rec, zg],
        [w["hgrn_w_o"], hgrn_g_norm], [(d_model, F32), (d_model, BF16)], [hd], name="hgrn_gate_out_bwd")
    g["hgrn_w_g"] = _mm(xn0, dzg, mode="tn", name="hgrn_w_g_bwd_w")
    dzq, dzf, dzi, g_lb = _hgrn_bwd(zq, zf, zi, lb_logits_full, states, do_rec, scatter_start("hgrn_out"),
                                    name="hgrn_bwd")
    g["hgrn_w_q"], g["hgrn_w_f"], g["hgrn_w_i"] = _dw_shared(xn0, [dzq, dzf, dzi], name="hgrn_w_qfi_bwd_w")

    def hgrn_proj_bwd(a, dres, *rest):
        dzs, gw, weights = rest[:4], rest[4], rest[5:]
        dxn = _dot(dzs[0], weights[0], _NT)
        for dz, wt in zip(dzs[1:], weights[1:]):
            dxn = dxn + _dot(dz, wt, _NT)
        dx, dw = _rms_bwd(a, gw, dxn)
        return dx + dres, dw

    grad_x, g_hgrn_norm = _rowcall(hgrn_proj_bwd, [xs, dh1, dzq, dzf, dzi, dzg],
                                   [hgrn_norm_full] + [w[k] for k in proj_names], [(d_model, F32)], [d_model],
                                   tr=512, name="hgrn_proj_bwd")

    small_parts = [g_hgrn_norm, g_lb, g_g_norm, g_mla_norm, g_q_norm, g_kv_in_norm, g_kv_norm, g_mlp_norm0,
                   g_mlp_norm1, g_final_norm, loss_part]
    small_sizes = [p.shape[1] for p in small_parts]
    small_terms = _exchange([jnp.concatenate(small_parts, axis=1)], scatter=False, name="gather_small")[0]
    small_sum = _sum_terms(small_terms, name="sum_small")
    last = scatter_start("hgrn_in", after=small_sum)
    offs = [0]
    for sz in small_sizes:
        offs.append(offs[-1] + sz)
    (s_hgrn_norm, s_lb, s_g_norm, s_mla_norm, s_q_norm, s_kv_in_norm, s_kv_norm, s_mlp_norm0, s_mlp_norm1, s_final_norm,
     s_loss) = [small_sum[:, a:b] for a, b in zip(offs[:-1], offs[1:])]
    shard = hgrn_norm.shape[1]
    g_lb_logits = _lb_logits_grad(lax.dynamic_slice_in_dim(s_lb, me * shard, shard, axis=1), hgrn_lb_logits,
                                  name="lb_logits_grad")
    loss = s_loss[0, 0]

    res, layer_terms = {}, {}

    def update(k, term_list):
        shape = given[k].shape
        as_layers = (len(term_list), shape[-2], shape[-1])
        upd = _adam(given[k].reshape(as_layers), term_list, given["m_" + k].reshape(as_layers),
                    given["v_" + k].reshape(as_layers), name=f"adam_{k}")
        res[k] = [o.reshape(shape) for o in upd]
        return upd[0]

    for tag, names in groups.items():
        for k, t in zip(names, _exchange_wait(scatter_state[tag], last, name=f"scatter_{tag}_wait")):
            if k.startswith("mlp_w_"):
                layer_terms.setdefault(k[:-1], {})[int(k[-1])] = t
                if len(layer_terms[k[:-1]]) == 2:
                    last = update(k[:-1], [layer_terms[k[:-1]][0], layer_terms[k[:-1]][1]])
            else:
                last = update(k, [t])

    small_grads = {
        "hgrn_norm": lax.dynamic_slice_in_dim(s_hgrn_norm, me * shard, shard, axis=1),
        "hgrn_g_norm": s_g_norm, "hgrn_lb_logits": g_lb_logits, "mla_norm": s_mla_norm, "mla_q_norm": s_q_norm,
        "kv_in_norm": s_kv_in_norm, "kv_norm": s_kv_norm,
        "mlp_norm": jnp.concatenate([s_mlp_norm0, s_mlp_norm1], axis=0), "final_norm": s_final_norm,
    }
    small_names = list(small_grads)

    def flat(a):
        return a.reshape(1, -1)

    packed = [jnp.concatenate([flat(src[pre + k]) for k in small_names], axis=1)
              for src, pre in ((given, ""), (small_grads, ""), (given, "m_"), (given, "v_"))]
    small_out = _adam(packed[0][None], [packed[1][None]], packed[2][None], packed[3][None], name="adam_small")
    off = 0
    for k in small_names:
        size = given[k].size
        res[k] = [o[0, :, off:off + size].reshape(given[k].shape) for o in small_out]
        off += size

    outs = [loss, grad_x[None]]
    for i in range(4):
        outs += [res[k][i] for k in weight_names]
    return tuple(outs)
```
